```python
import math
import jax, jax.numpy as jnp
from jax import lax
import numpy as np

D_MODEL = 1024
BATCH = 8
SEQ = 8192
DEPTH = 1

N_META = 16
HEAD_DIM = 64
N_Q_HEADS = 16
N_KV_HEADS = 4
GQA_GROUP = N_Q_HEADS // N_KV_HEADS
WINDOW = 128
SSM_WIDTH = D_MODEL
SSM_GROUP_CH = 16
SSM_GROUPS = SSM_WIDTH // SSM_GROUP_CH
SSM_STATE = 64
D_FF = -(-8 * D_MODEL // (3 * 256)) * 256
Q_W = N_Q_HEADS * HEAD_DIM
KV_W = N_KV_HEADS * HEAD_DIM
IN_COLS = Q_W + 2 * KV_W + SSM_WIDTH + 2 * D_MODEL
EPS = 1e-6
DT_MIN = 1e-3
DT_MAX = 1e-1

kernel_name = "hybrid_swa_sink_s5_gated_block"


def rmsnorm(x, g):
    xf = x.astype(jnp.float32)
    xf = xf * lax.rsqrt(jnp.mean(xf * xf, axis=-1, keepdims=True) + EPS)
    return (xf * g.astype(jnp.float32)).astype(x.dtype)


def softmax_with_sink(s, sink):
    sink = sink.astype(jnp.float32)[:, :, None, None]
    m = jnp.maximum(jnp.max(s, axis=-1, keepdims=True), sink)
    e = jnp.exp(s - m)
    return e / (jnp.sum(e, axis=-1, keepdims=True) + jnp.exp(sink - m))


def sliding_window_attention(q, k, v, sinks):
    b, L = q.shape[0], q.shape[1]
    n_blk = (L - N_META) // WINDOW
    scale = HEAD_DIM ** -0.5
    sink = sinks.reshape(N_KV_HEADS, GQA_GROUP)
    qm = q[:, :N_META].reshape(b, N_META, N_KV_HEADS, GQA_GROUP, HEAD_DIM)
    km, vm = k[:, :N_META], v[:, :N_META]
    s_m = jnp.einsum('bqkrd,bskd->bkrqs', qm, km, preferred_element_type=jnp.float32) * scale
    causal = jnp.tril(jnp.ones((N_META, N_META), dtype=bool))
    s_m = jnp.where(causal, s_m, -jnp.inf)
    p_m = softmax_with_sink(s_m, sink).astype(v.dtype)
    o_m = jnp.einsum('bkrqs,bskd->bqkrd', p_m, vm).reshape(b, N_META, Q_W)
    qb = q[:, N_META:].reshape(b, n_blk, WINDOW, N_KV_HEADS, GQA_GROUP, HEAD_DIM)
    kb = k[:, N_META:].reshape(b, n_blk, WINDOW, N_KV_HEADS, HEAD_DIM)
    vb = v[:, N_META:].reshape(b, n_blk, WINDOW, N_KV_HEADS, HEAD_DIM)
    pad = ((0, 0), (1, 0), (0, 0), (0, 0), (0, 0))
    k_prev = jnp.pad(kb[:, :-1], pad)
    v_prev = jnp.pad(vb[:, :-1], pad)
    meta_shape = (b, n_blk, N_META, N_KV_HEADS, HEAD_DIM)
    k_win = jnp.concatenate([jnp.broadcast_to(km[:, None], meta_shape), k_prev, kb], axis=2)
    v_win = jnp.concatenate([jnp.broadcast_to(vm[:, None], meta_shape), v_prev, vb], axis=2)
    s = jnp.einsum('bnqkrd,bnskd->bnkrqs', qb, k_win, preferred_element_type=jnp.float32) * scale
    qi = jnp.arange(WINDOW)[:, None]
    kj = jnp.arange(WINDOW)[None, :]
    blk = jnp.arange(n_blk)[:, None, None]
    meta_vis = jnp.ones((n_blk, WINDOW, N_META), dtype=bool)
    prev_vis = (kj > qi)[None] & (blk > 0)
    cur_vis = jnp.broadcast_to((kj <= qi)[None], (n_blk, WINDOW, WINDOW))
    mask = jnp.concatenate([meta_vis, prev_vis, cur_vis], axis=-1)
    s = jnp.where(mask[None, :, None, None], s, -jnp.inf)
    p = softmax_with_sink(s, sink).astype(v.dtype)
    o_r = jnp.einsum('bnkrqs,bnskd->bnqkrd', p, v_win).reshape(b, n_blk * WINDOW, Q_W)
    return jnp.concatenate([o_m, o_r], axis=1)


def s5_ssm(u, lam_re, lam_im, log_dt, b_re, b_im, c_re, c_im, d_skip):
    bsz, L = u.shape[0], u.shape[1]
    uf = u.astype(jnp.float32)
    ug = uf.reshape(bsz, L, SSM_GROUPS, SSM_GROUP_CH)
    dt = jnp.exp(log_dt.astype(jnp.float32))[:, None]
    lr, li = lam_re.astype(jnp.float32), lam_im.astype(jnp.float32)
    mag = jnp.exp(lr * dt)
    ar, ai = mag * jnp.cos(li * dt), mag * jnp.sin(li * dt)
    den = lr * lr + li * li
    nr, ni = ar - 1.0, ai
    fr, fi = (nr * lr + ni * li) / den, (ni * lr - nr * li) / den
    br, bi = b_re.astype(jnp.float32), b_im.astype(jnp.float32)
    bbar_re = fr[..., None] * br - fi[..., None] * bi
    bbar_im = fr[..., None] * bi + fi[..., None] * br
    xr = jnp.einsum('gpc,blgc->blgp', bbar_re, ug)
    xi = jnp.einsum('gpc,blgc->blgp', bbar_im, ug)
    a_re = jnp.broadcast_to(ar[None, None], (1, L, SSM_GROUPS, SSM_STATE))
    a_im = jnp.broadcast_to(ai[None, None], (1, L, SSM_GROUPS, SSM_STATE))

    def combine(e1, e2):
        a1r, a1i, b1r, b1i = e1
        a2r, a2i, b2r, b2i = e2
        return (a1r * a2r - a1i * a2i,
                a1r * a2i + a1i * a2r,
                a2r * b1r - a2i * b1i + b2r,
                a2r * b1i + a2i * b1r + b2i)

    _, _, sr, si = lax.associative_scan(combine, (a_re, a_im, xr, xi), axis=1)
    y = (jnp.einsum('gcp,blgp->blgc', c_re.astype(jnp.float32), sr)
         - jnp.einsum('gcp,blgp->blgc', c_im.astype(jnp.float32), si))
    y = y.reshape(bsz, L, SSM_WIDTH) + d_skip.astype(jnp.float32) * uf
    return y


def _fwd_setup_inputs(seed: int = 0) -> dict:
    key = jax.random.key(seed)
    ks = jax.random.split(key, 24)
    f32 = jnp.float32
    nrm = lambda k, shape, s: jax.random.normal(k, shape, f32) * s
    gain = lambda k, shape: 1.0 + 0.02 * jax.random.normal(k, shape, f32)
    n_idx = jnp.arange(SSM_STATE, dtype=f32)
    return {
        "x": nrm(ks[0], (BATCH, SEQ, D_MODEL), 1.0),
        "meta_tokens": nrm(ks[1], (N_META, D_MODEL), 1.0),
        "norm_mix": gain(ks[2], (DEPTH, D_MODEL)),
        "w_in": nrm(ks[3], (DEPTH, D_MODEL, IN_COLS), D_MODEL ** -0.5),
        "q_norm": gain(ks[4], (DEPTH, HEAD_DIM)),
        "k_norm": gain(ks[5], (DEPTH, HEAD_DIM)),
        "attn_sinks": nrm(ks[6], (DEPTH, N_Q_HEADS), 0.5),
        "lam_re": -0.5 + nrm(ks[7], (DEPTH, SSM_GROUPS, SSM_STATE), 0.01),
        "lam_im": math.pi * n_idx + nrm(ks[8], (DEPTH, SSM_GROUPS, SSM_STATE), 0.01),
        "log_dt": jax.random.uniform(ks[9], (DEPTH, SSM_GROUPS), f32, math.log(DT_MIN), math.log(DT_MAX)),
        "ssm_b_re": nrm(ks[10], (DEPTH, SSM_GROUPS, SSM_STATE, SSM_GROUP_CH), (2 * SSM_GROUP_CH) ** -0.5),
        "ssm_b_im": nrm(ks[11], (DEPTH, SSM_GROUPS, SSM_STATE, SSM_GROUP_CH), (2 * SSM_GROUP_CH) ** -0.5),
        "ssm_c_re": nrm(ks[12], (DEPTH, SSM_GROUPS, SSM_GROUP_CH, SSM_STATE), (2 * SSM_STATE) ** -0.5),
        "ssm_c_im": nrm(ks[13], (DEPTH, SSM_GROUPS, SSM_GROUP_CH, SSM_STATE), (2 * SSM_STATE) ** -0.5),
        "ssm_d": nrm(ks[14], (DEPTH, SSM_WIDTH), 1.0),
        "w_glu": nrm(ks[15], (DEPTH, SSM_WIDTH, 2 * D_MODEL), SSM_WIDTH ** -0.5),
        "attn_branch_norm": gain(ks[16], (DEPTH, D_MODEL)),
        "ssm_branch_norm": gain(ks[17], (DEPTH, D_MODEL)),
        "w_out": nrm(ks[18], (DEPTH, D_MODEL, D_MODEL), D_MODEL ** -0.5),
        "norm_ffn": gain(ks[19], (DEPTH, D_MODEL)),
        "w_ffn_in": nrm(ks[20], (DEPTH, D_MODEL, 2 * D_FF), D_MODEL ** -0.5),
        "w_ffn_out": nrm(ks[21], (DEPTH, D_FF, D_MODEL), D_FF ** -0.5),
    }


def _fwd_reference(x, meta_tokens, norm_mix, w_in, q_norm, k_norm, attn_sinks, lam_re, lam_im, log_dt,
              ssm_b_re, ssm_b_im, ssm_c_re, ssm_c_im, ssm_d, w_glu, attn_branch_norm, ssm_branch_norm,
              w_out, norm_ffn, w_ffn_in, w_ffn_out):
    b = x.shape[0]
    meta = jnp.broadcast_to(meta_tokens.astype(x.dtype)[None], (b, N_META, D_MODEL))
    h = jnp.concatenate([meta, x], axis=1)
    L = h.shape[1]
    offs = np.cumsum([Q_W, KV_W, KV_W, SSM_WIDTH, D_MODEL]).tolist()
    for i in range(DEPTH):
        xn = rmsnorm(h, norm_mix[i])
        proj = xn @ w_in[i]
        q, k, v, u, g_att, g_ssm = jnp.split(proj, offs, axis=-1)
        q = rmsnorm(q.reshape(b, L, N_Q_HEADS, HEAD_DIM), q_norm[i])
        k = rmsnorm(k.reshape(b, L, N_KV_HEADS, HEAD_DIM), k_norm[i])
        v = v.reshape(b, L, N_KV_HEADS, HEAD_DIM)
        attn = sliding_window_attention(q, k, v, attn_sinks[i])
        y = s5_ssm(u, lam_re[i], lam_im[i], log_dt[i], ssm_b_re[i], ssm_b_im[i],
                   ssm_c_re[i], ssm_c_im[i], ssm_d[i])
        z = jax.nn.gelu(y).astype(h.dtype)
        za, zb = jnp.split(z @ w_glu[i], 2, axis=-1)
        ssm = za * jax.nn.sigmoid(zb)
        merged = (jax.nn.sigmoid(g_att) * rmsnorm(attn, attn_branch_norm[i])
                  + jax.nn.sigmoid(g_ssm) * rmsnorm(ssm, ssm_branch_norm[i]))
        h = h + merged @ w_out[i]
        hn = rmsnorm(h, norm_ffn[i])
        gate, up = jnp.split(hn @ w_ffn_in[i], 2, axis=-1)
        h = h + (jax.nn.silu(gate) * up) @ w_ffn_out[i]
    return h[:, N_META:]


import jax as _jax
import jax.numpy as _jnp

TWIN_FORMAT = 'train_step'
FWD_PARAMS = ['x', 'meta_tokens', 'norm_mix', 'w_in', 'q_norm', 'k_norm', 'attn_sinks', 'lam_re', 'lam_im', 'log_dt', 'ssm_b_re', 'ssm_b_im', 'ssm_c_re', 'ssm_c_im', 'ssm_d', 'w_glu', 'attn_branch_norm', 'ssm_branch_norm', 'w_out', 'norm_ffn', 'w_ffn_in', 'w_ffn_out']
TWIN_WEIGHTS = ['meta_tokens', 'norm_mix', 'w_in', 'q_norm', 'k_norm', 'attn_sinks', 'lam_re', 'lam_im', 'log_dt', 'ssm_b_re', 'ssm_b_im', 'ssm_c_re', 'ssm_c_im', 'ssm_d', 'w_glu', 'attn_branch_norm', 'ssm_branch_norm', 'w_out', 'norm_ffn', 'w_ffn_in', 'w_ffn_out']
TWIN_DIFF_INPUT = 'x'
TWIN_INPUTS = ['x', 'meta_tokens', 'norm_mix', 'w_in', 'q_norm', 'k_norm', 'attn_sinks', 'lam_re', 'lam_im', 'log_dt', 'ssm_b_re', 'ssm_b_im', 'ssm_c_re', 'ssm_c_im', 'ssm_d', 'w_glu', 'attn_branch_norm', 'ssm_branch_norm', 'w_out', 'norm_ffn', 'w_ffn_in', 'w_ffn_out', 'loss_target', 'm_meta_tokens', 'm_norm_mix', 'm_w_in', 'm_q_norm', 'm_k_norm', 'm_attn_sinks', 'm_lam_re', 'm_lam_im', 'm_log_dt', 'm_ssm_b_re', 'm_ssm_b_im', 'm_ssm_c_re', 'm_ssm_c_im', 'm_ssm_d', 'm_w_glu', 'm_attn_branch_norm', 'm_ssm_branch_norm', 'm_w_out', 'm_norm_ffn', 'm_w_ffn_in', 'm_w_ffn_out', 'v_meta_tokens', 'v_norm_mix', 'v_w_in', 'v_q_norm', 'v_k_norm', 'v_attn_sinks', 'v_lam_re', 'v_lam_im', 'v_log_dt', 'v_ssm_b_re', 'v_ssm_b_im', 'v_ssm_c_re', 'v_ssm_c_im', 'v_ssm_d', 'v_w_glu', 'v_attn_branch_norm', 'v_ssm_branch_norm', 'v_w_out', 'v_norm_ffn', 'v_w_ffn_in', 'v_w_ffn_out']
TWIN_OUTPUTS = ['loss', 'grad_x', 'grad_meta_tokens', 'grad_norm_mix', 'grad_w_in', 'grad_q_norm', 'grad_k_norm', 'grad_attn_sinks', 'grad_lam_re', 'grad_lam_im', 'grad_log_dt', 'grad_ssm_b_re', 'grad_ssm_b_im', 'grad_ssm_c_re', 'grad_ssm_c_im', 'grad_ssm_d', 'grad_w_glu', 'grad_attn_branch_norm', 'grad_ssm_branch_norm', 'grad_w_out', 'grad_norm_ffn', 'grad_w_ffn_in', 'grad_w_ffn_out', 'delta_meta_tokens', 'delta_norm_mix', 'delta_w_in', 'delta_q_norm', 'delta_k_norm', 'delta_attn_sinks', 'delta_lam_re', 'delta_lam_im', 'delta_log_dt', 'delta_ssm_b_re', 'delta_ssm_b_im', 'delta_ssm_c_re', 'delta_ssm_c_im', 'delta_ssm_d', 'delta_w_glu', 'delta_attn_branch_norm', 'delta_ssm_branch_norm', 'delta_w_out', 'delta_norm_ffn', 'delta_w_ffn_in', 'delta_w_ffn_out', 'new_m_meta_tokens', 'new_m_norm_mix', 'new_m_w_in', 'new_m_q_norm', 'new_m_k_norm', 'new_m_attn_sinks', 'new_m_lam_re', 'new_m_lam_im', 'new_m_log_dt', 'new_m_ssm_b_re', 'new_m_ssm_b_im', 'new_m_ssm_c_re', 'new_m_ssm_c_im', 'new_m_ssm_d', 'new_m_w_glu', 'new_m_attn_branch_norm', 'new_m_ssm_branch_norm', 'new_m_w_out', 'new_m_norm_ffn', 'new_m_w_ffn_in', 'new_m_w_ffn_out', 'new_v_meta_tokens', 'new_v_norm_mix', 'new_v_w_in', 'new_v_q_norm', 'new_v_k_norm', 'new_v_attn_sinks', 'new_v_lam_re', 'new_v_lam_im', 'new_v_log_dt', 'new_v_ssm_b_re', 'new_v_ssm_b_im', 'new_v_ssm_c_re', 'new_v_ssm_c_im', 'new_v_ssm_d', 'new_v_w_glu', 'new_v_attn_branch_norm', 'new_v_ssm_branch_norm', 'new_v_w_out', 'new_v_norm_ffn', 'new_v_w_ffn_in', 'new_v_w_ffn_out']
TWIN_LEAF_KINDS = {'loss': 'loss', 'grad_x': 'grad_x', 'grad_meta_tokens': 'grad_w', 'grad_norm_mix': 'grad_w', 'grad_w_in': 'grad_w', 'grad_q_norm': 'grad_w', 'grad_k_norm': 'grad_w', 'grad_attn_sinks': 'grad_w', 'grad_lam_re': 'grad_w', 'grad_lam_im': 'grad_w', 'grad_log_dt': 'grad_w', 'grad_ssm_b_re': 'grad_w', 'grad_ssm_b_im': 'grad_w', 'grad_ssm_c_re': 'grad_w', 'grad_ssm_c_im': 'grad_w', 'grad_ssm_d': 'grad_w', 'grad_w_glu': 'grad_w', 'grad_attn_branch_norm': 'grad_w', 'grad_ssm_branch_norm': 'grad_w', 'grad_w_out': 'grad_w', 'grad_norm_ffn': 'grad_w', 'grad_w_ffn_in': 'grad_w', 'grad_w_ffn_out': 'grad_w', 'delta_meta_tokens': 'delta_w', 'delta_norm_mix': 'delta_w', 'delta_w_in': 'delta_w', 'delta_q_norm': 'delta_w', 'delta_k_norm': 'delta_w', 'delta_attn_sinks': 'delta_w', 'delta_lam_re': 'delta_w', 'delta_lam_im': 'delta_w', 'delta_log_dt': 'delta_w', 'delta_ssm_b_re': 'delta_w', 'delta_ssm_b_im': 'delta_w', 'delta_ssm_c_re': 'delta_w', 'delta_ssm_c_im': 'delta_w', 'delta_ssm_d': 'delta_w', 'delta_w_glu': 'delta_w', 'delta_attn_branch_norm': 'delta_w', 'delta_ssm_branch_norm': 'delta_w', 'delta_w_out': 'delta_w', 'delta_norm_ffn': 'delta_w', 'delta_w_ffn_in': 'delta_w', 'delta_w_ffn_out': 'delta_w', 'new_m_meta_tokens': 'new_m', 'new_m_norm_mix': 'new_m', 'new_m_w_in': 'new_m', 'new_m_q_norm': 'new_m', 'new_m_k_norm': 'new_m', 'new_m_attn_sinks': 'new_m', 'new_m_lam_re': 'new_m', 'new_m_lam_im': 'new_m', 'new_m_log_dt': 'new_m', 'new_m_ssm_b_re': 'new_m', 'new_m_ssm_b_im': 'new_m', 'new_m_ssm_c_re': 'new_m', 'new_m_ssm_c_im': 'new_m', 'new_m_ssm_d': 'new_m', 'new_m_w_glu': 'new_m', 'new_m_attn_branch_norm': 'new_m', 'new_m_ssm_branch_norm': 'new_m', 'new_m_w_out': 'new_m', 'new_m_norm_ffn': 'new_m', 'new_m_w_ffn_in': 'new_m', 'new_m_w_ffn_out': 'new_m', 'new_v_meta_tokens': 'new_v', 'new_v_norm_mix': 'new_v', 'new_v_w_in': 'new_v', 'new_v_q_norm': 'new_v', 'new_v_k_norm': 'new_v', 'new_v_attn_sinks': 'new_v', 'new_v_lam_re': 'new_v', 'new_v_lam_im': 'new_v', 'new_v_log_dt': 'new_v', 'new_v_ssm_b_re': 'new_v', 'new_v_ssm_b_im': 'new_v', 'new_v_ssm_c_re': 'new_v', 'new_v_ssm_c_im': 'new_v', 'new_v_ssm_d': 'new_v', 'new_v_w_glu': 'new_v', 'new_v_attn_branch_norm': 'new_v', 'new_v_ssm_branch_norm': 'new_v', 'new_v_w_out': 'new_v', 'new_v_norm_ffn': 'new_v', 'new_v_w_ffn_in': 'new_v', 'new_v_w_ffn_out': 'new_v'}


def _forward(args):
    return _fwd_reference(*[args[k] for k in FWD_PARAMS])


def _output_shape():
    def fwd():
        inp = _fwd_setup_inputs(0)
        return _fwd_reference(*[inp[k] for k in FWD_PARAMS])
    out = _jax.eval_shape(fwd)
    return out.shape, out.dtype

N_MICROBATCH = 1
ADAM_LR = 0.001
ADAM_B1 = 0.9
ADAM_B2 = 0.999
ADAM_EPS = 1e-08
ADAM_WD = 0.01
ADAM_STEP = 10
PER_EXAMPLE_BATCH_AXIS = {'x': 0, 'loss_target': 0}
SHARED_INPUTS = []
_WEIGHT_DTYPES = {'meta_tokens': _jnp.float32, 'norm_mix': _jnp.float32, 'w_in': _jnp.float32, 'q_norm': _jnp.float32, 'k_norm': _jnp.float32, 'attn_sinks': _jnp.float32, 'lam_re': _jnp.float32, 'lam_im': _jnp.float32, 'log_dt': _jnp.float32, 'ssm_b_re': _jnp.float32, 'ssm_b_im': _jnp.float32, 'ssm_c_re': _jnp.float32, 'ssm_c_im': _jnp.float32, 'ssm_d': _jnp.float32, 'w_glu': _jnp.float32, 'attn_branch_norm': _jnp.float32, 'ssm_branch_norm': _jnp.float32, 'w_out': _jnp.float32, 'norm_ffn': _jnp.float32, 'w_ffn_in': _jnp.float32, 'w_ffn_out': _jnp.float32}
MOMENT_SCALE = {'meta_tokens': 4.490389e-01, 'norm_mix': 3.859274e+00, 'w_in': 8.835346e-01, 'q_norm': 1.504124e+00, 'k_norm': 1.509811e+00, 'attn_sinks': 7.442463e-02, 'lam_re': 1.910608e-02, 'lam_im': 1.954409e-02, 'log_dt': 1.031607e+01, 'ssm_b_re': 1.312368e-02, 'ssm_b_im': 1.242054e-02, 'ssm_c_re': 2.592787e-02, 'ssm_c_im': 2.595787e-02, 'ssm_d': 4.505993e+00, 'w_glu': 3.022530e+00, 'attn_branch_norm': 1.882206e+01, 'ssm_branch_norm': 2.071387e+01, 'w_out': 4.527087e+00, 'norm_ffn': 4.913330e+01, 'w_ffn_in': 9.615730e-01, 'w_ffn_out': 8.848019e-01}


def _to_microbatches(a, axis):
    t = _jnp.moveaxis(a, axis, 0)
    t = t.reshape((N_MICROBATCH, t.shape[0] // N_MICROBATCH) + t.shape[1:])
    return _jnp.moveaxis(t, 1, axis + 1)


def setup_inputs(seed: int = 0) -> dict:
    inp = _fwd_setup_inputs(seed)
    key = _jax.random.fold_in(_jax.random.key(seed), 7919)
    shape, _ = _output_shape()
    out = dict(inp)
    out["loss_target"] = _jax.random.normal(_jax.random.fold_in(key, 0), shape, _jnp.float32)
    for i, name in enumerate(TWIN_WEIGHTS):
        w = inp[name].astype(_jnp.float32)
        if MOMENT_SCALE is None:
            s = _jnp.sqrt(_jnp.mean(_jnp.square(w)) + 1e-30)
        else:
            s = MOMENT_SCALE[name]
        km, kv = _jax.random.split(_jax.random.fold_in(key, i + 1))
        out[name] = w
        out["m_" + name] = s * _jax.random.normal(km, w.shape, _jnp.float32)
        out["v_" + name] = (s * s) * _jax.random.uniform(kv, w.shape, _jnp.float32, 0.5, 1.5)
    if N_MICROBATCH > 1:
        for name, axis in PER_EXAMPLE_BATCH_AXIS.items():
            out[name] = _to_microbatches(out[name], axis)
    return {'x': out['x'], 'meta_tokens': out['meta_tokens'], 'norm_mix': out['norm_mix'], 'w_in': out['w_in'], 'q_norm': out['q_norm'], 'k_norm': out['k_norm'], 'attn_sinks': out['attn_sinks'], 'lam_re': out['lam_re'], 'lam_im': out['lam_im'], 'log_dt': out['log_dt'], 'ssm_b_re': out['ssm_b_re'], 'ssm_b_im': out['ssm_b_im'], 'ssm_c_re': out['ssm_c_re'], 'ssm_c_im': out['ssm_c_im'], 'ssm_d': out['ssm_d'], 'w_glu': out['w_glu'], 'attn_branch_norm': out['attn_branch_norm'], 'ssm_branch_norm': out['ssm_branch_norm'], 'w_out': out['w_out'], 'norm_ffn': out['norm_ffn'], 'w_ffn_in': out['w_ffn_in'], 'w_ffn_out': out['w_ffn_out'], 'loss_target': out['loss_target'], 'm_meta_tokens': out['m_meta_tokens'], 'm_norm_mix': out['m_norm_mix'], 'm_w_in': out['m_w_in'], 'm_q_norm': out['m_q_norm'], 'm_k_norm': out['m_k_norm'], 'm_attn_sinks': out['m_attn_sinks'], 'm_lam_re': out['m_lam_re'], 'm_lam_im': out['m_lam_im'], 'm_log_dt': out['m_log_dt'], 'm_ssm_b_re': out['m_ssm_b_re'], 'm_ssm_b_im': out['m_ssm_b_im'], 'm_ssm_c_re': out['m_ssm_c_re'], 'm_ssm_c_im': out['m_ssm_c_im'], 'm_ssm_d': out['m_ssm_d'], 'm_w_glu': out['m_w_glu'], 'm_attn_branch_norm': out['m_attn_branch_norm'], 'm_ssm_branch_norm': out['m_ssm_branch_norm'], 'm_w_out': out['m_w_out'], 'm_norm_ffn': out['m_norm_ffn'], 'm_w_ffn_in': out['m_w_ffn_in'], 'm_w_ffn_out': out['m_w_ffn_out'], 'v_meta_tokens': out['v_meta_tokens'], 'v_norm_mix': out['v_norm_mix'], 'v_w_in': out['v_w_in'], 'v_q_norm': out['v_q_norm'], 'v_k_norm': out['v_k_norm'], 'v_attn_sinks': out['v_attn_sinks'], 'v_lam_re': out['v_lam_re'], 'v_lam_im': out['v_lam_im'], 'v_log_dt': out['v_log_dt'], 'v_ssm_b_re': out['v_ssm_b_re'], 'v_ssm_b_im': out['v_ssm_b_im'], 'v_ssm_c_re': out['v_ssm_c_re'], 'v_ssm_c_im': out['v_ssm_c_im'], 'v_ssm_d': out['v_ssm_d'], 'v_w_glu': out['v_w_glu'], 'v_attn_branch_norm': out['v_attn_branch_norm'], 'v_ssm_branch_norm': out['v_ssm_branch_norm'], 'v_w_out': out['v_w_out'], 'v_norm_ffn': out['v_norm_ffn'], 'v_w_ffn_in': out['v_w_ffn_in'], 'v_w_ffn_out': out['v_w_ffn_out']}


def _loss(weights, diff, rest, loss_target):
    with _jax.named_scope("forward"):
        args = {**rest, TWIN_DIFF_INPUT: diff, **{k: w.astype(_WEIGHT_DTYPES[k]) for k, w in weights.items()}}
        y = _forward(args)
    with _jax.named_scope("loss_head"):
        err = _jnp.square(y.astype(_jnp.float32) - loss_target)
        return 0.5 * _jnp.sum(_jnp.mean(err, axis=-1)) if err.ndim else 0.5 * err


def _adamw(w, g, m, v):
    m = ADAM_B1 * m + (1.0 - ADAM_B1) * g
    v = ADAM_B2 * v + (1.0 - ADAM_B2) * _jnp.square(g)
    m_hat = m / (1.0 - ADAM_B1 ** ADAM_STEP)
    v_hat = v / (1.0 - ADAM_B2 ** ADAM_STEP)
    delta = -ADAM_LR * (m_hat / (_jnp.sqrt(v_hat) + ADAM_EPS) + ADAM_WD * w)
    return delta, m, v


def reference(x, meta_tokens, norm_mix, w_in, q_norm, k_norm, attn_sinks, lam_re, lam_im, log_dt, ssm_b_re, ssm_b_im, ssm_c_re, ssm_c_im, ssm_d, w_glu, attn_branch_norm, ssm_branch_norm, w_out, norm_ffn, w_ffn_in, w_ffn_out, loss_target, m_meta_tokens, m_norm_mix, m_w_in, m_q_norm, m_k_norm, m_attn_sinks, m_lam_re, m_lam_im, m_log_dt, m_ssm_b_re, m_ssm_b_im, m_ssm_c_re, m_ssm_c_im, m_ssm_d, m_w_glu, m_attn_branch_norm, m_ssm_branch_norm, m_w_out, m_norm_ffn, m_w_ffn_in, m_w_ffn_out, v_meta_tokens, v_norm_mix, v_w_in, v_q_norm, v_k_norm, v_attn_sinks, v_lam_re, v_lam_im, v_log_dt, v_ssm_b_re, v_ssm_b_im, v_ssm_c_re, v_ssm_c_im, v_ssm_d, v_w_glu, v_attn_branch_norm, v_ssm_branch_norm, v_w_out, v_norm_ffn, v_w_ffn_in, v_w_ffn_out):
    given = dict(x=x, meta_tokens=meta_tokens, norm_mix=norm_mix, w_in=w_in, q_norm=q_norm, k_norm=k_norm, attn_sinks=attn_sinks, lam_re=lam_re, lam_im=lam_im, log_dt=log_dt, ssm_b_re=ssm_b_re, ssm_b_im=ssm_b_im, ssm_c_re=ssm_c_re, ssm_c_im=ssm_c_im, ssm_d=ssm_d, w_glu=w_glu, attn_branch_norm=attn_branch_norm, ssm_branch_norm=ssm_branch_norm, w_out=w_out, norm_ffn=norm_ffn, w_ffn_in=w_ffn_in, w_ffn_out=w_ffn_out, loss_target=loss_target, m_meta_tokens=m_meta_tokens, m_norm_mix=m_norm_mix, m_w_in=m_w_in, m_q_norm=m_q_norm, m_k_norm=m_k_norm, m_attn_sinks=m_attn_sinks, m_lam_re=m_lam_re, m_lam_im=m_lam_im, m_log_dt=m_log_dt, m_ssm_b_re=m_ssm_b_re, m_ssm_b_im=m_ssm_b_im, m_ssm_c_re=m_ssm_c_re, m_ssm_c_im=m_ssm_c_im, m_ssm_d=m_ssm_d, m_w_glu=m_w_glu, m_attn_branch_norm=m_attn_branch_norm, m_ssm_branch_norm=m_ssm_branch_norm, m_w_out=m_w_out, m_norm_ffn=m_norm_ffn, m_w_ffn_in=m_w_ffn_in, m_w_ffn_out=m_w_ffn_out, v_meta_tokens=v_meta_tokens, v_norm_mix=v_norm_mix, v_w_in=v_w_in, v_q_norm=v_q_norm, v_k_norm=v_k_norm, v_attn_sinks=v_attn_sinks, v_lam_re=v_lam_re, v_lam_im=v_lam_im, v_log_dt=v_log_dt, v_ssm_b_re=v_ssm_b_re, v_ssm_b_im=v_ssm_b_im, v_ssm_c_re=v_ssm_c_re, v_ssm_c_im=v_ssm_c_im, v_ssm_d=v_ssm_d, v_w_glu=v_w_glu, v_attn_branch_norm=v_attn_branch_norm, v_ssm_branch_norm=v_ssm_branch_norm, v_w_out=v_w_out, v_norm_ffn=v_norm_ffn, v_w_ffn_in=v_w_ffn_in, v_w_ffn_out=v_w_ffn_out)
    weights = {n: given[n] for n in TWIN_WEIGHTS}
    shared = {n: given[n] for n in SHARED_INPUTS}
    per_example = {n: given[n] for n in ['x']}
    grad_fn = _jax.value_and_grad(_loss, argnums=(0, 1))

    def one_microbatch(ex, loss_target):
        ex = dict(ex)
        diff = ex.pop(TWIN_DIFF_INPUT)
        return grad_fn(weights, diff, {**shared, **ex}, loss_target)

    if N_MICROBATCH == 1:
        loss, (grad_w, grad_x) = one_microbatch(per_example, given["loss_target"])
    else:
        def body(carry, xs):
            loss_sum, grad_sum = carry
            l_k, (gw_k, gx_k) = one_microbatch(xs[0], xs[1])
            with _jax.named_scope("update"):
                return (loss_sum + l_k, _jax.tree.map(_jnp.add, grad_sum, gw_k)), gx_k

        init = (_jnp.zeros((), _jnp.float32), _jax.tree.map(_jnp.zeros_like, weights))
        (loss, grad_w), grad_x = _jax.lax.scan(body, init, (per_example, given["loss_target"]))
    with _jax.named_scope("update"):
        delta_w, new_m, new_v = {}, {}, {}
        for n in TWIN_WEIGHTS:
            delta_w[n], new_m[n], new_v[n] = _adamw(weights[n], grad_w[n], given["m_" + n], given["v_" + n])
    return (loss, grad_x, *[grad_w[n] for n in TWIN_WEIGHTS], *[delta_w[n] for n in TWIN_WEIGHTS],
            *[new_m[n] for n in TWIN_WEIGHTS], *[new_v[n] for n in TWIN_WEIGHTS])
```

```python
import functools
import math

import jax
import jax.numpy as jnp
from jax import lax
from jax.experimental import pallas as pl
from jax.experimental.pallas import tpu as pltpu

F32 = jnp.float32
BF = jnp.bfloat16

D = 1024
N_META = 16
HEAD = 64
NQ = 16
NKV = 4
QW = NQ * HEAD
KVW = NKV * HEAD
WIN = 128
GROUPS = 64
GCH = 16
NSTATE = 64
SW = GROUPS * NSTATE
DFF = 2816
IN_COLS = QW + 2 * KVW + D + 2 * D
PAD = 256
META0 = PAD - N_META
EPS = 1e-6
NEG = -1e30

TM = 256
TS = 128
LC = 512
NLC = SW // LC

LR, B1, B2, AEPS, WD, STEP = 0.001, 0.9, 0.999, 1e-08, 0.01, 10
BC1 = 1.0 - B1 ** STEP
BC2 = 1.0 - B2 ** STEP

MESH = pl.DeviceIdType.MESH
ANY = pl.BlockSpec(memory_space=pl.ANY)


def _cp(sem=None, vmem_mb=48):
    return pltpu.CompilerParams(dimension_semantics=sem, vmem_limit_bytes=vmem_mb << 20)


def _full(shape):
    n = len(shape)
    return pl.BlockSpec(shape, lambda *a: (0,) * n)


def _rows(tm, width):
    return pl.BlockSpec((tm, width), lambda i: (i, 0))


def _dot(a, b):
    return jnp.dot(a, b, preferred_element_type=F32)


def _dot_nt(a, b):
    return lax.dot_general(a, b, (((1,), (1,)), ((), ())), preferred_element_type=F32)


def _dot_tn(a, b):
    return lax.dot_general(a, b, (((0,), (0,)), ((), ())), preferred_element_type=F32)


def _sigmoid(x):
    return 1.0 / (1.0 + jnp.exp(-x))


def _seg_mean(x, bd):
    outs = []
    for j in range(x.shape[1] // 256):
        c = x[:, 256 * j:256 * (j + 1)]
        hi = c.astype(BF)
        lo = (c - hi.astype(F32)).astype(BF)
        outs.append(_dot(hi, bd) + _dot(lo, bd))
    out = outs[0] if len(outs) == 1 else jnp.concatenate(outs, axis=1)
    return out * (1.0 / HEAD)


_GC = math.sqrt(2.0 / math.pi)


def _gelu(x):
    t = jnp.tanh(_GC * (x + 0.044715 * x * x * x))
    return 0.5 * x * (1.0 + t), t


def _gelu_grad(x, t):
    return 0.5 * (1.0 + t) + 0.5 * x * (1.0 - t * t) * _GC * (1.0 + 3.0 * 0.044715 * x * x)


def in_proj_fwd(hp, w_in, nm, qn_t, kn_t, bd):
    lp = hp.shape[0]

    def body(h_ref, w_ref, nm_ref, qn_ref, kn_ref, bd_ref,
             xn_o, qr_o, kr_o, qn_o, kn_o, v_o, u_o, gg_o):
        h = h_ref[...]
        r = lax.rsqrt(jnp.mean(h * h, axis=-1, keepdims=True) + EPS)
        xb = ((h * r) * nm_ref[...]).astype(BF)
        xn_o[...] = xb
        bdv = bd_ref[...]
        q = _dot(xb, w_ref[:, 0:QW])
        qr_o[...] = q
        qn_o[...] = (q * lax.rsqrt(_seg_mean(q * q, bdv) + EPS) * qn_ref[...]).astype(BF)
        k = _dot(xb, w_ref[:, QW:QW + KVW])
        kr_o[...] = k
        kn_o[...] = (k * lax.rsqrt(_seg_mean(k * k, bdv) + EPS) * kn_ref[...]).astype(BF)
        v_o[...] = _dot(xb, w_ref[:, QW + KVW:QW + 2 * KVW]).astype(BF)
        u_o[...] = _dot(xb, w_ref[:, QW + 2 * KVW:QW + 2 * KVW + D])
        gg_o[...] = _dot(xb, w_ref[:, QW + 2 * KVW + D:IN_COLS])

    outs = [(D, BF), (QW, F32), (KVW, F32), (QW, BF), (KVW, BF), (KVW, BF), (D, F32), (2 * D, F32)]
    return pl.pallas_call(
        body, name="in_proj_fwd", grid=(lp // TM,),
        in_specs=[_rows(TM, D), _full((D, IN_COLS)), _full((1, D)), _full((1, QW)), _full((1, KVW)),
                  _full((256, 256))],
        out_specs=[_rows(TM, w) for w, _ in outs],
        out_shape=[jax.ShapeDtypeStruct((lp, w), t) for w, t in outs],
        compiler_params=_cp(("arbitrary",)),
    )(hp, w_in, nm, qn_t, kn_t, bd)


def in_proj_bwd(dqn, dkn, dv, du, dgg, qr, kr, hp, dh1, w_in, nm, qn_t, kn_t, bd):
    lp = hp.shape[0]

    def body(dqn_ref, dkn_ref, dv_ref, du_ref, dgg_ref, qr_ref, kr_ref, h_ref, dh1_ref,
             w_ref, nm_ref, qn_ref, kn_ref, bd_ref,
             dproj_o, dh0_o, dnm_o, dqw_o, dkw_o):
        i = pl.program_id(0)

        @pl.when(i == 0)
        def _():
            dnm_o[...] = jnp.zeros_like(dnm_o)
            dqw_o[...] = jnp.zeros_like(dqw_o)
            dkw_o[...] = jnp.zeros_like(dkw_o)

        bdv = bd_ref[...]

        def norm_bwd(x, dy, g, acc_ref):
            rr = lax.rsqrt(_seg_mean(x * x, bdv) + EPS)
            xh = x * rr
            acc_ref[...] += jnp.sum(dy * xh, axis=0, keepdims=True)
            dxh = dy * g
            return rr * (dxh - xh * _seg_mean(dxh * xh, bdv))

        dq = norm_bwd(qr_ref[...], dqn_ref[...], qn_ref[...], dqw_o)
        dk = norm_bwd(kr_ref[...], dkn_ref[...], kn_ref[...], dkw_o)
        dproj_o[:, 0:QW] = dq.astype(BF)
        dproj_o[:, QW:QW + KVW] = dk.astype(BF)
        dproj_o[:, QW + KVW:QW + 2 * KVW] = dv_ref[...].astype(BF)
        dproj_o[:, QW + 2 * KVW:QW + 2 * KVW + D] = du_ref[...]
        dproj_o[:, QW + 2 * KVW + D:IN_COLS] = dgg_ref[...]
        dxn = jnp.zeros((TM, D), F32)
        for c0 in range(0, IN_COLS, 512):
            dxn += _dot_nt(dproj_o[:, c0:c0 + 512], w_ref[:, c0:c0 + 512])
        h = h_ref[...]
        r = lax.rsqrt(jnp.mean(h * h, axis=-1, keepdims=True) + EPS)
        xh = h * r
        dnm_o[...] += jnp.sum(dxn * xh, axis=0, keepdims=True)
        dxh = dxn * nm_ref[...]
        dh0_o[...] = dh1_ref[...] + r * (dxh - xh * jnp.mean(dxh * xh, axis=-1, keepdims=True))

    return pl.pallas_call(
        body, name="in_proj_bwd", grid=(lp // TM,),
        in_specs=[_rows(TM, QW), _rows(TM, KVW), _rows(TM, KVW), _rows(TM, D), _rows(TM, 2 * D),
                  _rows(TM, QW), _rows(TM, KVW), _rows(TM, D), _rows(TM, D),
                  _full((D, IN_COLS)), _full((1, D)), _full((1, QW)), _full((1, KVW)), _full((256, 256))],
        out_specs=[_rows(TM, IN_COLS), _rows(TM, D), _full((1, D)), _full((1, QW)), _full((1, KVW))],
        out_shape=[jax.ShapeDtypeStruct((lp, IN_COLS), BF), jax.ShapeDtypeStruct((lp, D), F32),
                   jax.ShapeDtypeStruct((1, D), F32), jax.ShapeDtypeStruct((1, QW), F32),
                   jax.ShapeDtypeStruct((1, KVW), F32)],
        compiler_params=_cp(("arbitrary",)),
    )(dqn, dkn, dv, du, dgg, qr, kr, hp, dh1, w_in, nm, qn_t, kn_t, bd)


def _att_masks(b):
    qi = lax.broadcasted_iota(jnp.int32, (4 * WIN, WIN), 0) & (WIN - 1)
    kj = lax.broadcasted_iota(jnp.int32, (4 * WIN, WIN), 1)
    mask_c = (kj <= qi) & (b * WIN + kj >= META0)
    mask_p = (kj > qi) & ((b - 1) * WIN + kj >= PAD)
    mask_m = (kj >= WIN - N_META) & (b * WIN + qi >= PAD)
    return mask_m, mask_p, mask_c


def _dup_half(ref, kp, pos, lo):
    pair = ref[:, 128 * kp:128 * kp + 128].astype(F32)
    rolled = pltpu.roll(pair, 64, axis=1)
    keep = lo if pos == 0 else jnp.logical_not(lo)
    return jnp.where(keep, pair, rolled).astype(BF)


def _stack_heads(ref, kv, lo):
    parts = []
    for pp in range(2):
        pair = ref[:, 256 * kv + 128 * pp:256 * kv + 128 * pp + 128]
        zero = jnp.zeros_like(pair)
        parts.append(jnp.where(lo, pair, zero))
        parts.append(jnp.where(lo, zero, pair))
    return jnp.concatenate(parts, axis=0)


def _unstack_heads(x4, lo):
    a = jnp.where(lo, x4[0:WIN], x4[WIN:2 * WIN])
    b = jnp.where(lo, x4[2 * WIN:3 * WIN], x4[3 * WIN:4 * WIN])
    return a, b


def _sink_col(sink_ref, kv):
    row = lax.broadcasted_iota(jnp.int32, (4 * WIN, 1), 0) >> 7
    col = jnp.full((4 * WIN, 1), sink_ref[4 * kv + 3], F32)
    for r in range(3):
        col = jnp.where(row == r, sink_ref[4 * kv + r], col)
    return col


def _att_probs(q4, kds, masks, sink4):
    scale = HEAD ** -0.5
    ss = [jnp.where(mk, _dot_nt(q4, kd) * scale, NEG) for kd, mk in zip(kds, masks)]
    m = sink4
    for s in ss:
        m = jnp.maximum(m, jnp.max(s, axis=-1, keepdims=True))
    es = [jnp.exp(s - m) for s in ss]
    esink = jnp.exp(sink4 - m)
    den = esink
    for e in es:
        den = den + jnp.sum(e, axis=-1, keepdims=True)
    inv = 1.0 / den
    return [e * inv for e in es], esink * inv


def _kv_specs():
    return [pl.BlockSpec((WIN, KVW), lambda b: (1, 0)),
            pl.BlockSpec((WIN, KVW), lambda b: (jnp.maximum(b - 1, 0), 0)),
            pl.BlockSpec((WIN, KVW), lambda b: (b, 0))]


def attention_fwd(sinks, qn, kn, v):
    lp = qn.shape[0]

    def body(sink_ref, q_ref, km_ref, kp_ref, kc_ref, vm_ref, vp_ref, vc_ref, o_ref):
        b = pl.program_id(0)
        masks = _att_masks(b)
        lo = lax.broadcasted_iota(jnp.int32, (WIN, WIN), 1) < HEAD
        for kv in range(NKV):
            kp, pos = kv // 2, kv % 2
            kds = [_dup_half(r, kp, pos, lo) for r in (km_ref, kp_ref, kc_ref)]
            vds = [_dup_half(r, kp, pos, lo) for r in (vm_ref, vp_ref, vc_ref)]
            q4 = _stack_heads(q_ref, kv, lo)
            ps, _ = _att_probs(q4, kds, masks, _sink_col(sink_ref, kv))
            o4 = _dot(ps[0].astype(BF), vds[0]) + _dot(ps[1].astype(BF), vds[1]) + _dot(ps[2].astype(BF), vds[2])
            oa, ob = _unstack_heads(o4, lo)
            o_ref[:, 256 * kv:256 * kv + 128] = oa
            o_ref[:, 256 * kv + 128:256 * kv + 256] = ob

    return pl.pallas_call(
        body, name="attention_fwd", grid=(lp // WIN,),
        in_specs=[pl.BlockSpec(memory_space=pltpu.SMEM), _rows(WIN, QW)] + _kv_specs() + _kv_specs(),
        out_specs=_rows(WIN, QW),
        out_shape=jax.ShapeDtypeStruct((lp, QW), F32),
        compiler_params=_cp(("arbitrary",)),
    )(sinks, qn, kn, kn, kn, v, v, v)


def attention_bwd(sinks, qn, kn, v, do):
    lp = qn.shape[0]
    nb = lp // WIN

    def body(sink_ref, q_ref, km_ref, kp_ref, kc_ref, vm_ref, vp_ref, vc_ref, do_ref,
             dq_ref, dk_hbm, dv_hbm, dsink_ref, dk_acc, dv_acc):
        b = pl.program_id(0)

        @pl.when(b == 0)
        def _():
            dk_acc[...] = jnp.zeros_like(dk_acc)
            dv_acc[...] = jnp.zeros_like(dv_acc)
            dsink_ref[...] = jnp.zeros_like(dsink_ref)

        masks = _att_masks(b)
        lo = lax.broadcasted_iota(jnp.int32, (WIN, WIN), 1) < HEAD
        lane8 = lax.broadcasted_iota(jnp.int32, (8, 128), 1)
        row4 = lax.broadcasted_iota(jnp.int32, (4 * WIN, 1), 0) >> 7
        starts = [WIN, pl.multiple_of(jnp.maximum(b - 1, 0) * WIN, WIN), pl.multiple_of(b * WIN, WIN)]
        scale = HEAD ** -0.5
        for kv in range(NKV):
            kp, pos = kv // 2, kv % 2
            keep = lo if pos == 0 else jnp.logical_not(lo)
            kds = [_dup_half(r, kp, pos, lo) for r in (km_ref, kp_ref, kc_ref)]
            vds = [_dup_half(r, kp, pos, lo) for r in (vm_ref, vp_ref, vc_ref)]
            q4 = _stack_heads(q_ref, kv, lo)
            do4 = _stack_heads(do_ref, kv, lo)
            ps, psink = _att_probs(q4, kds, masks, _sink_col(sink_ref, kv))
            dps = [_dot_nt(do4, vd) for vd in vds]
            delta = jnp.sum(ps[0] * dps[0] + ps[1] * dps[1] + ps[2] * dps[2], axis=-1, keepdims=True)
            dss = [(p * (dp - delta) * scale).astype(BF) for p, dp in zip(ps, dps)]
            dsk = psink * delta
            for r in range(4):
                val = -jnp.sum(jnp.where(row4 == r, dsk, 0.0), axis=0, keepdims=True)
                dsink_ref[...] += jnp.where(lane8 == 4 * kv + r, val, 0.0)
            dq4 = _dot(dss[0], kds[0]) + _dot(dss[1], kds[1]) + _dot(dss[2], kds[2])
            dqa, dqb = _unstack_heads(dq4, lo)
            dq_ref[:, 256 * kv:256 * kv + 128] = dqa
            dq_ref[:, 256 * kv + 128:256 * kv + 256] = dqb
            for x in range(3):
                dkd = _dot_tn(dss[x], q4)
                dvd = _dot_tn(ps[x].astype(BF), do4)
                dkd = jnp.where(keep, dkd + pltpu.roll(dkd, 64, axis=1), 0.0)
                dvd = jnp.where(keep, dvd + pltpu.roll(dvd, 64, axis=1), 0.0)
                dk_acc[pl.ds(starts[x], WIN), 128 * kp:128 * kp + 128] += dkd
                dv_acc[pl.ds(starts[x], WIN), 128 * kp:128 * kp + 128] += dvd

        @pl.when(b == nb - 1)
        def _():
            pltpu.sync_copy(dk_acc, dk_hbm)
            pltpu.sync_copy(dv_acc, dv_hbm)

    return pl.pallas_call(
        body, name="attention_bwd", grid=(nb,),
        in_specs=[pl.BlockSpec(memory_space=pltpu.SMEM), _rows(WIN, QW)] + _kv_specs() + _kv_specs()
                 + [_rows(WIN, QW)],
        out_specs=[_rows(WIN, QW), ANY, ANY, _full((8, 128))],
        out_shape=[jax.ShapeDtypeStruct((lp, QW), F32), jax.ShapeDtypeStruct((lp, KVW), F32),
                   jax.ShapeDtypeStruct((lp, KVW), F32), jax.ShapeDtypeStruct((8, 128), F32)],
        scratch_shapes=[pltpu.VMEM((lp, KVW), F32), pltpu.VMEM((lp, KVW), F32)],
        compiler_params=_cp(("arbitrary",)),
    )(sinks, qn, kn, kn, kn, v, v, v, do)


def _scan_tile(xr, xi, a1, a2, a4, rid, reverse):
    for k, (akr, aki) in zip((1, 2, 4), (a1, a2, a4)):
        if reverse:
            sr = pltpu.roll(xr, 8 - k, axis=0)
            si = pltpu.roll(xi, 8 - k, axis=0)
            ok = rid < 8 - k
        else:
            sr = pltpu.roll(xr, k, axis=0)
            si = pltpu.roll(xi, k, axis=0)
            ok = rid >= k
        sr = jnp.where(ok, sr, 0.0)
        si = jnp.where(ok, si, 0.0)
        xr, xi = xr + akr * sr - aki * si, xi + akr * si + aki * sr
    return xr, xi


def ssm_fwd(u, pw, br, bi, cr, ci, dsk):
    lp = u.shape[0]

    def body(u_ref, pw_ref, br_ref, bi_ref, cr_ref, ci_ref, d_ref, y_ref, sr_ref, si_ref, car_r, car_i):
        i = pl.program_id(0)

        @pl.when(i == 0)
        def _():
            car_r[...] = jnp.zeros_like(car_r)
            car_i[...] = jnp.zeros_like(car_i)

        u = u_ref[...]
        ub = u.astype(BF)
        for j in range(NLC):
            uj = ub[:, 128 * j:128 * j + 128]
            sr_ref[:, LC * j:LC * j + LC] = _dot(uj, br_ref[j])
            si_ref[:, LC * j:LC * j + LC] = _dot(uj, bi_ref[j])
        rid = lax.broadcasted_iota(jnp.int32, (8, LC), 0)
        for j in range(NLC):
            ls = slice(LC * j, LC * j + LC)
            a1 = (pw_ref[0:1, ls], pw_ref[1:2, ls])
            a2 = (pw_ref[2:3, ls], pw_ref[3:4, ls])
            a4 = (pw_ref[4:5, ls], pw_ref[5:6, ls])
            pr, pi = pw_ref[8:16, ls], pw_ref[16:24, ls]

            def group(g, carry):
                cr_, ci_ = carry
                rows = pl.ds(pl.multiple_of(g * 8, 8), 8)
                xr, xi = _scan_tile(sr_ref[rows, ls], si_ref[rows, ls], a1, a2, a4, rid, False)
                xr, xi = xr + pr * cr_ - pi * ci_, xi + pr * ci_ + pi * cr_
                sr_ref[rows, ls] = xr
                si_ref[rows, ls] = xi
                return xr[7:8, :], xi[7:8, :]

            cr_, ci_ = lax.fori_loop(0, TS // 8, group, (car_r[:, ls], car_i[:, ls]))
            car_r[:, ls] = cr_
            car_i[:, ls] = ci_
        for j in range(NLC):
            ls = slice(LC * j, LC * j + LC)
            y_ref[:, 128 * j:128 * j + 128] = (_dot(sr_ref[:, ls].astype(BF), cr_ref[j])
                                               - _dot(si_ref[:, ls].astype(BF), ci_ref[j]))
        y_ref[...] += d_ref[...] * u

    return pl.pallas_call(
        body, name="ssm_fwd", grid=(lp // TS,),
        in_specs=[_rows(TS, D), _full((40, SW)), _full((NLC, 128, LC)), _full((NLC, 128, LC)),
                  _full((NLC, LC, 128)), _full((NLC, LC, 128)), _full((1, D))],
        out_specs=[_rows(TS, D), _rows(TS, SW), _rows(TS, SW)],
        out_shape=[jax.ShapeDtypeStruct((lp, D), F32), jax.ShapeDtypeStruct((lp, SW), F32),
                   jax.ShapeDtypeStruct((lp, SW), F32)],
        scratch_shapes=[pltpu.VMEM((1, SW), F32), pltpu.VMEM((1, SW), F32)],
        compiler_params=_cp(("arbitrary",)),
    )(u, pw, br, bi, cr, ci, dsk)


def ssm_bwd(dy, u, sr, si, pw, br, bi, cr, ci, dsk):
    lp = u.shape[0]
    nt = lp // TS

    def rev(i):
        return (nt - 1 - i, 0)

    def prev8(i):
        return (jnp.maximum((nt - 1 - i) * (TS // 8) - 1, 0), 0)

    def body(dy_ref, u_ref, sr_ref, si_ref, pr8_ref, pi8_ref, pw_ref, br_ref, bi_ref, cr_ref, ci_ref, d_ref,
             du_ref, da_ref, dbr_ref, dbi_ref, dcr_ref, dci_ref, dd_ref, gr_s, gi_s, car_r, car_i):
        i = pl.program_id(0)

        @pl.when(i == 0)
        def _():
            car_r[...] = jnp.zeros_like(car_r)
            car_i[...] = jnp.zeros_like(car_i)
            for ref in (da_ref, dbr_ref, dbi_ref, dcr_ref, dci_ref, dd_ref):
                ref[...] = jnp.zeros_like(ref)

        first = (i == nt - 1).astype(F32)
        dy = dy_ref[...]
        u = u_ref[...]
        dyb = dy.astype(BF)
        ub = u.astype(BF)
        dd_ref[...] += jnp.sum(dy * u, axis=0, keepdims=True)
        for j in range(NLC):
            ls = slice(LC * j, LC * j + LC)
            dyj = dyb[:, 128 * j:128 * j + 128]
            gr_s[:, ls] = _dot_nt(dyj, cr_ref[j])
            gi_s[:, ls] = -_dot_nt(dyj, ci_ref[j])
            dcr_ref[j] += _dot_tn(sr_ref[:, ls].astype(BF), dyj)
            dci_ref[j] -= _dot_tn(si_ref[:, ls].astype(BF), dyj)
        rid = lax.broadcasted_iota(jnp.int32, (8, LC), 0)
        for j in range(NLC):
            ls = slice(LC * j, LC * j + LC)
            a1 = (pw_ref[0:1, ls], -pw_ref[1:2, ls])
            a2 = (pw_ref[2:3, ls], -pw_ref[3:4, ls])
            a4 = (pw_ref[4:5, ls], -pw_ref[5:6, ls])
            pr, pi = pw_ref[24:32, ls], pw_ref[32:40, ls]

            def one_group(rows, spr, spi, carry):
                cr_, ci_ = carry
                gr, gi = _scan_tile(gr_s[rows, ls], gi_s[rows, ls], a1, a2, a4, rid, True)
                gr, gi = gr + pr * cr_ - pi * ci_, gi + pr * ci_ + pi * cr_
                gr_s[rows, ls] = gr
                gi_s[rows, ls] = gi
                s_r = jnp.where(rid >= 1, pltpu.roll(sr_ref[rows, ls], 1, axis=0), spr[7:8, :])
                s_i = jnp.where(rid >= 1, pltpu.roll(si_ref[rows, ls], 1, axis=0), spi[7:8, :])
                da_ref[0:8, ls] += gr * s_r + gi * s_i
                da_ref[8:16, ls] += gi * s_r - gr * s_i
                return gr[0:1, :], gi[0:1, :]

            def group(k, carry):
                g = TS // 8 - 1 - k
                rows = pl.ds(pl.multiple_of(g * 8, 8), 8)
                before = pl.ds(pl.multiple_of(g * 8 - 8, 8), 8)
                return one_group(rows, sr_ref[before, ls], si_ref[before, ls], carry)

            carry = lax.fori_loop(0, TS // 8 - 1, group, (car_r[:, ls], car_i[:, ls]))
            keep = 1.0 - first
            cr_, ci_ = one_group(pl.ds(0, 8), pr8_ref[:, ls] * keep, pi8_ref[:, ls] * keep, carry)
            car_r[:, ls] = cr_
            car_i[:, ls] = ci_
        for j in range(NLC):
            ls = slice(LC * j, LC * j + LC)
            uj = ub[:, 128 * j:128 * j + 128]
            grb = gr_s[:, ls].astype(BF)
            gib = gi_s[:, ls].astype(BF)
            du_ref[:, 128 * j:128 * j + 128] = (_dot_nt(grb, br_ref[j]) + _dot_nt(gib, bi_ref[j])
                                                + d_ref[:, 128 * j:128 * j + 128] * dy[:, 128 * j:128 * j + 128]
                                                ).astype(BF)
            dbr_ref[j] += _dot_tn(uj, grb)
            dbi_ref[j] += _dot_tn(uj, gib)

    return pl.pallas_call(
        body, name="ssm_bwd", grid=(nt,),
        in_specs=[pl.BlockSpec((TS, D), rev), pl.BlockSpec((TS, D), rev), pl.BlockSpec((TS, SW), rev),
                  pl.BlockSpec((TS, SW), rev), pl.BlockSpec((8, SW), prev8), pl.BlockSpec((8, SW), prev8),
                  _full((40, SW)), _full((NLC, 128, LC)), _full((NLC, 128, LC)),
                  _full((NLC, LC, 128)), _full((NLC, LC, 128)), _full((1, D))],
        out_specs=[pl.BlockSpec((TS, D), rev), _full((16, SW)), _full((NLC, 128, LC)), _full((NLC, 128, LC)),
                   _full((NLC, LC, 128)), _full((NLC, LC, 128)), _full((1, D))],
        out_shape=[jax.ShapeDtypeStruct((lp, D), BF), jax.ShapeDtypeStruct((16, SW), F32),
                   jax.ShapeDtypeStruct((NLC, 128, LC), F32), jax.ShapeDtypeStruct((NLC, 128, LC), F32),
                   jax.ShapeDtypeStruct((NLC, LC, 128), F32), jax.ShapeDtypeStruct((NLC, LC, 128), F32),
                   jax.ShapeDtypeStruct((1, D), F32)],
        scratch_shapes=[pltpu.VMEM((TS, SW), F32), pltpu.VMEM((TS, SW), F32),
                        pltpu.VMEM((1, SW), F32), pltpu.VMEM((1, SW), F32)],
        compiler_params=_cp(("arbitrary",)),
    )(dy, u, sr, si, sr, si, pw, br, bi, cr, ci, dsk)


def _merge_parts(attn, zz, gg, abn, sbn):
    za, zb = zz[:, 0:D], zz[:, D:2 * D]
    sgz = _sigmoid(zb)
    ssm = za * sgz
    ra = lax.rsqrt(jnp.mean(attn * attn, axis=-1, keepdims=True) + EPS)
    rs = lax.rsqrt(jnp.mean(ssm * ssm, axis=-1, keepdims=True) + EPS)
    ah, sh = attn * ra, ssm * rs
    sga, sgs = _sigmoid(gg[:, 0:D]), _sigmoid(gg[:, D:2 * D])
    return za, sgz, ra, rs, ah, sh, sga, sgs, ah * abn, sh * sbn


def mid_fwd(y, attn, gg, hp, w_glu, w_out, abn, sbn):
    lp = y.shape[0]

    def body(y_ref, a_ref, gg_ref, h_ref, wg_ref, wo_ref, abn_ref, sbn_ref, z_o, zz_o, mg_o, h1_o):
        z, _ = _gelu(y_ref[...])
        zb = z.astype(BF)
        z_o[...] = zb
        zz = _dot(zb, wg_ref[...])
        zz_o[...] = zz.astype(BF)
        _, _, _, _, _, _, sga, sgs, an, sn = _merge_parts(a_ref[...], zz, gg_ref[...], abn_ref[...], sbn_ref[...])
        mb = (sga * an + sgs * sn).astype(BF)
        mg_o[...] = mb
        h1_o[...] = h_ref[...] + _dot(mb, wo_ref[...])

    return pl.pallas_call(
        body, name="mid_fwd", grid=(lp // TM,),
        in_specs=[_rows(TM, D), _rows(TM, D), _rows(TM, 2 * D), _rows(TM, D), _full((D, 2 * D)), _full((D, D)),
                  _full((1, D)), _full((1, D))],
        out_specs=[_rows(TM, D), _rows(TM, 2 * D), _rows(TM, D), _rows(TM, D)],
        out_shape=[jax.ShapeDtypeStruct((lp, D), BF), jax.ShapeDtypeStruct((lp, 2 * D), BF),
                   jax.ShapeDtypeStruct((lp, D), BF), jax.ShapeDtypeStruct((lp, D), F32)],
        compiler_params=_cp(("arbitrary",)),
    )(y, attn, gg, hp, w_glu, w_out, abn, sbn)


def mid_bwd(dh1b, y, attn, gg, zzb, w_glu, w_out, abn, sbn):
    lp = y.shape[0]

    def body(dh_ref, y_ref, a_ref, gg_ref, zz_ref, wg_ref, wo_ref, abn_ref, sbn_ref,
             dy_o, dattn_o, dgg_o, dzz_o, dabn_o, dsbn_o):
        i = pl.program_id(0)

        @pl.when(i == 0)
        def _():
            dabn_o[...] = jnp.zeros_like(dabn_o)
            dsbn_o[...] = jnp.zeros_like(dsbn_o)

        dm = _dot_nt(dh_ref[...], wo_ref[...])
        abn, sbn = abn_ref[...], sbn_ref[...]
        za, sgz, ra, rs, ah, sh, sga, sgs, an, sn = _merge_parts(
            a_ref[...], zz_ref[...].astype(F32), gg_ref[...], abn, sbn)
        dgg_o[:, 0:D] = (dm * an * sga * (1.0 - sga)).astype(BF)
        dgg_o[:, D:2 * D] = (dm * sn * sgs * (1.0 - sgs)).astype(BF)
        da = dm * sga
        dabn_o[...] += jnp.sum(da * ah, axis=0, keepdims=True)
        dah = da * abn
        dattn_o[...] = (ra * (dah - ah * jnp.mean(dah * ah, axis=-1, keepdims=True))).astype(BF)
        ds = dm * sgs
        dsbn_o[...] += jnp.sum(ds * sh, axis=0, keepdims=True)
        dsh = ds * sbn
        dssm = rs * (dsh - sh * jnp.mean(dsh * sh, axis=-1, keepdims=True))
        dzz_o[:, 0:D] = (dssm * sgz).astype(BF)
        dzz_o[:, D:2 * D] = (dssm * za * sgz * (1.0 - sgz)).astype(BF)
        dz = _dot_nt(dzz_o[...], wg_ref[...])
        yv = y_ref[...]
        _, t = _gelu(yv)
        dy_o[...] = dz * _gelu_grad(yv, t)

    return pl.pallas_call(
        body, name="mid_bwd", grid=(lp // TM,),
        in_specs=[_rows(TM, D), _rows(TM, D), _rows(TM, D), _rows(TM, 2 * D), _rows(TM, 2 * D),
                  _full((D, 2 * D)), _full((D, D)), _full((1, D)), _full((1, D))],
        out_specs=[_rows(TM, D), _rows(TM, D), _rows(TM, 2 * D), _rows(TM, 2 * D), _full((1, D)), _full((1, D))],
        out_shape=[jax.ShapeDtypeStruct((lp, D), F32), jax.ShapeDtypeStruct((lp, D), BF),
                   jax.ShapeDtypeStruct((lp, 2 * D), BF), jax.ShapeDtypeStruct((lp, 2 * D), BF),
                   jax.ShapeDtypeStruct((1, D), F32), jax.ShapeDtypeStruct((1, D), F32)],
        compiler_params=_cp(("arbitrary",)),
    )(dh1b, y, attn, gg, zzb, w_glu, w_out, abn, sbn)


def ffn_fwd_bwd(h1, tgt, w_fi, w_fo, gf):
    lp = h1.shape[0]
    tf = 128

    def body(h_ref, t_ref, wi_ref, wo_ref, gf_ref, hn_o, dgu_o, act_o, dh2_o, dh1_o, dh1b_o, dgf_o, loss_o):
        i = pl.program_id(0)

        @pl.when(i == 0)
        def _():
            dgf_o[...] = jnp.zeros_like(dgf_o)
            loss_o[...] = jnp.zeros_like(loss_o)

        h = h_ref[...]
        r = lax.rsqrt(jnp.mean(h * h, axis=-1, keepdims=True) + EPS)
        xh = h * r
        hb = (xh * gf_ref[...]).astype(BF)
        hn_o[...] = hb
        gate = _dot(hb, wi_ref[:, 0:DFF])
        up = _dot(hb, wi_ref[:, DFF:2 * DFF])
        sg = _sigmoid(gate)
        sl = gate * sg
        ab = (sl * up).astype(BF)
        act_o[...] = ab
        h2 = h + _dot(ab, wo_ref[...])
        real = (i >= PAD // tf).astype(F32)
        e = (h2 - t_ref[...]) * real
        loss_o[...] += 0.5 * jnp.sum(jnp.sum(e * e, axis=-1, keepdims=True), axis=0, keepdims=True) * (1.0 / D)
        dh2 = e * (1.0 / D)
        db = dh2.astype(BF)
        dh2_o[...] = db
        dact = _dot_nt(db, wo_ref[...])
        dgate = (dact * up * (sg * (1.0 + gate * (1.0 - sg)))).astype(BF)
        dup = (dact * sl).astype(BF)
        dgu_o[:, 0:DFF] = dgate
        dgu_o[:, DFF:2 * DFF] = dup
        dhn = _dot_nt(dgate, wi_ref[:, 0:DFF]) + _dot_nt(dup, wi_ref[:, DFF:2 * DFF])
        dgf_o[...] += jnp.sum(dhn * xh, axis=0, keepdims=True)
        dxh = dhn * gf_ref[...]
        dh1 = dh2 + r * (dxh - xh * jnp.mean(dxh * xh, axis=-1, keepdims=True))
        dh1_o[...] = dh1
        dh1b_o[...] = dh1.astype(BF)

    return pl.pallas_call(
        body, name="ffn_fwd_bwd", grid=(lp // tf,),
        in_specs=[_rows(tf, D), pl.BlockSpec((tf, D), lambda i: (jnp.maximum(i - PAD // tf, 0), 0)),
                  _full((D, 2 * DFF)), _full((DFF, D)), _full((1, D))],
        out_specs=[_rows(tf, D), _rows(tf, 2 * DFF), _rows(tf, DFF), _rows(tf, D), _rows(tf, D), _rows(tf, D),
                   _full((1, D)), _full((1, 1))],
        out_shape=[jax.ShapeDtypeStruct((lp, D), BF), jax.ShapeDtypeStruct((lp, 2 * DFF), BF),
                   jax.ShapeDtypeStruct((lp, DFF), BF), jax.ShapeDtypeStruct((lp, D), BF),
                   jax.ShapeDtypeStruct((lp, D), F32), jax.ShapeDtypeStruct((lp, D), BF),
                   jax.ShapeDtypeStruct((1, D), F32), jax.ShapeDtypeStruct((1, 1), F32)],
        compiler_params=_cp(("arbitrary",), vmem_mb=56),
    )(h1, tgt, w_fi, w_fo, gf)


def weight_grad(name, a, b, tn):
    lp, k = a.shape
    n = b.shape[1]
    tr = next(t for t in (768, 512, 384, 256, 128) if lp % t == 0)

    def body(a_ref, b_ref, o_ref):
        @pl.when(pl.program_id(1) == 0)
        def _():
            o_ref[...] = jnp.zeros_like(o_ref)

        o_ref[...] += _dot_tn(a_ref[...], b_ref[...])

    return pl.pallas_call(
        body, name=name, grid=(n // tn, lp // tr),
        in_specs=[pl.BlockSpec((tr, k), lambda j, m: (m, 0)), pl.BlockSpec((tr, tn), lambda j, m: (m, j))],
        out_specs=pl.BlockSpec((k, tn), lambda j, m: (0, j)),
        out_shape=jax.ShapeDtypeStruct((k, n), F32),
        compiler_params=_cp(("arbitrary", "arbitrary")),
    )(a, b)


def _adamw_math(w, g, m, v):
    m = B1 * m + (1.0 - B1) * g
    v = B2 * v + (1.0 - B2) * (g * g)
    delta = -LR * ((m / BC1) / (jnp.sqrt(v / BC2) + AEPS) + WD * w)
    return delta, m, v


def adamw(name, w, ga, gb, m, v):
    r, c = w.shape
    tr = r
    for cand in (256, 128, 64, 32, 16, 8):
        if r % cand == 0:
            tr = cand
            break

    def body(w_ref, ga_ref, gb_ref, m_ref, v_ref, g_o, d_o, m_o, v_o):
        g = ga_ref[...] + gb_ref[...]
        g_o[...] = g
        d_o[...], m_o[...], v_o[...] = _adamw_math(w_ref[...], g, m_ref[...], v_ref[...])

    spec = pl.BlockSpec((tr, c), lambda i: (i, 0))
    return pl.pallas_call(
        body, name=name, grid=(r // tr,), in_specs=[spec] * 5, out_specs=[spec] * 4,
        out_shape=[jax.ShapeDtypeStruct((r, c), F32)] * 4,
        compiler_params=_cp(("arbitrary",)),
    )(w, ga, gb, m, v)


def sum4(name, parts):
    _, r, c = parts.shape
    tr = r
    for cand in (256, 128, 64, 32, 16, 8):
        if r % cand == 0:
            tr = cand
            break

    def body(p_ref, o_ref):
        o_ref[...] = (p_ref[0] + p_ref[1]) + (p_ref[2] + p_ref[3])

    return pl.pallas_call(
        body, name=name, grid=(r // tr,),
        in_specs=[pl.BlockSpec((4, tr, c), lambda i: (0, i, 0))], out_specs=pl.BlockSpec((tr, c), lambda i: (i, 0)),
        out_shape=jax.ShapeDtypeStruct((r, c), F32),
        compiler_params=_cp(("arbitrary",)),
    )(parts)


def _place():
    x, y, c = lax.axis_index("x"), lax.axis_index("y"), lax.axis_index("c")
    return x, y, c


def _other_chips(x, y):
    return [(1 - x, y), (x, 1 - y), (1 - x, 1 - y)]


SHARDED = (("w_in", D, IN_COLS, 1), ("w_glu", D, 2 * D, 1), ("w_out", D, D, 0),
           ("w_ffn_in", D, 2 * DFF, 1), ("w_ffn_out", DFF, D, 0))


def _shard_of(ref, axis, k, size):
    if axis == 0:
        return ref.at[pl.ds(k * size, size), :]
    return ref.at[:, pl.ds(k * size, size)]


def gather_weights(shards, meta):
    items = [(s, r, c, ax) for s, (_, r, c, ax) in zip(shards, SHARDED)] + [(meta, N_META, D, 1)]
    n = len(items)

    def body(*refs):
        ins, outs = refs[:n], refs[n:2 * n]
        send_sems, recv_sems, local_sems = refs[2 * n:]
        x, y, c = _place()
        mine = 2 * x + y
        copies = []
        for t, (src, dst, (_, r, cc, ax)) in enumerate(zip(ins, outs, items)):
            size = (r if ax == 0 else cc) // 4
            own = _shard_of(dst, ax, mine, size)
            loc = pltpu.make_async_copy(src, own, local_sems.at[t])
            loc.start()
            copies.append(loc)
            for j, chip in enumerate(_other_chips(x, y)):
                cp = pltpu.make_async_remote_copy(
                    src_ref=src, dst_ref=own, send_sem=send_sems.at[3 * t + j], recv_sem=recv_sems.at[3 * t + j],
                    device_id=(*chip, c), device_id_type=MESH)
                cp.start()
                copies.append(cp)
        for t, (src, dst, (_, r, cc, ax)) in enumerate(zip(ins, outs, items)):
            size = (r if ax == 0 else cc) // 4
            for j, (px, py) in enumerate(_other_chips(x, y)):
                theirs = _shard_of(dst, ax, 2 * px + py, size)
                pltpu.make_async_remote_copy(
                    src_ref=src, dst_ref=theirs, send_sem=send_sems.at[3 * t + j], recv_sem=recv_sems.at[3 * t + j],
                    device_id=(px, py, c), device_id_type=MESH).wait_recv()
        for t in range(n):
            copies[4 * t].wait()
            for j in range(3):
                copies[4 * t + 1 + j].wait_send()

    return pl.pallas_call(
        body, name="gather_weights",
        in_specs=[ANY] * n, out_specs=[ANY] * n,
        out_shape=[jax.ShapeDtypeStruct((r, c), s.dtype) for s, r, c, _ in items],
        scratch_shapes=[pltpu.SemaphoreType.DMA((3 * n,)), pltpu.SemaphoreType.DMA((3 * n,)),
                        pltpu.SemaphoreType.DMA((n,))],
        compiler_params=pltpu.CompilerParams(has_side_effects=True),
    )(*shards, meta)


def scatter_grads(grads):
    n = len(grads)

    def shapes():
        out = []
        for _, r, c, ax in SHARDED:
            out.append((r // 4, c) if ax == 0 else (r, c // 4))
        return out

    def body(*refs):
        ins, outs = refs[:n], refs[n:2 * n]
        send_sems, recv_sems, local_sems = refs[2 * n:]
        x, y, c = _place()
        mine = 2 * x + y
        started = []
        for t, (src, dst, (_, r, cc, ax)) in enumerate(zip(ins, outs, SHARDED)):
            size = (r if ax == 0 else cc) // 4
            loc = pltpu.make_async_copy(_shard_of(src, ax, mine, size), dst.at[3], local_sems.at[t])
            loc.start()
            started.append(loc)
            for j, (px, py) in enumerate(_other_chips(x, y)):
                cp = pltpu.make_async_remote_copy(
                    src_ref=_shard_of(src, ax, 2 * px + py, size), dst_ref=dst.at[j],
                    send_sem=send_sems.at[3 * t + j], recv_sem=recv_sems.at[3 * t + j],
                    device_id=(px, py, c), device_id_type=MESH)
                cp.start()
                started.append(cp)
        for t in range(n):
            started[4 * t].wait()
            for j in range(3):
                started[4 * t + 1 + j].wait()

    return pl.pallas_call(
        body, name="scatter_grads",
        in_specs=[ANY] * n, out_specs=[ANY] * n,
        out_shape=[jax.ShapeDtypeStruct((4,) + s, F32) for s in shapes()],
        scratch_shapes=[pltpu.SemaphoreType.DMA((3 * n,)), pltpu.SemaphoreType.DMA((3 * n,)),
                        pltpu.SemaphoreType.DMA((n,))],
        compiler_params=pltpu.CompilerParams(has_side_effects=True),
    )(*grads)


def swap_with_sibling(parts):
    n = len(parts)

    def body(*refs):
        ins, outs = refs[:n], refs[n:2 * n]
        send_sems, recv_sems = refs[2 * n:]
        x, y, c = _place()
        cps = []
        for t in range(n):
            cp = pltpu.make_async_remote_copy(
                src_ref=ins[t], dst_ref=outs[t], send_sem=send_sems.at[t], recv_sem=recv_sems.at[t],
                device_id=(x, y, 1 - c), device_id_type=MESH)
            cp.start()
            cps.append(cp)
        for cp in cps:
            cp.wait()

    return pl.pallas_call(
        body, name="swap_with_sibling",
        in_specs=[ANY] * n, out_specs=[ANY] * n,
        out_shape=[jax.ShapeDtypeStruct(p.shape, p.dtype) for p in parts],
        scratch_shapes=[pltpu.SemaphoreType.DMA((n,)), pltpu.SemaphoreType.DMA((n,))],
        compiler_params=pltpu.CompilerParams(has_side_effects=True),
    )(*parts)


def allreduce_small(pack):
    rows = pack.shape[0]

    def body(p_ref, o_ref, all_ref, send_sems, recv_sems):
        x, y, c = _place()
        me = 4 * x + 2 * y + c
        all_ref[me] = p_ref[...]
        peers = []
        for dx in range(2):
            for dy in range(2):
                for dc in range(2):
                    if dx or dy or dc:
                        peers.append((dx, dy, dc))
        cps = []
        for k, (dx, dy, dc) in enumerate(peers):
            cp = pltpu.make_async_remote_copy(
                src_ref=p_ref, dst_ref=all_ref.at[me], send_sem=send_sems.at[k], recv_sem=recv_sems.at[k],
                device_id=(x ^ dx, y ^ dy, c ^ dc), device_id_type=MESH)
            cp.start()
            cps.append(cp)
        for cp in cps:
            cp.wait()
        acc = all_ref[0]
        for d in range(1, 8):
            acc = acc + all_ref[d]
        o_ref[...] = acc

    return pl.pallas_call(
        body, name="allreduce_small",
        in_specs=[pl.BlockSpec(memory_space=pltpu.VMEM)], out_specs=pl.BlockSpec(memory_space=pltpu.VMEM),
        out_shape=jax.ShapeDtypeStruct((rows, 128), F32),
        scratch_shapes=[pltpu.VMEM((8, rows, 128), F32), pltpu.SemaphoreType.DMA((7,)),
                        pltpu.SemaphoreType.DMA((7,))],
        compiler_params=pltpu.CompilerParams(has_side_effects=True, vmem_limit_bytes=40 << 20),
    )(pack)


def _ssm_discretize(lam_re, lam_im, log_dt, b_re, b_im):
    dt = jnp.exp(log_dt)[:, None]
    mag = jnp.exp(lam_re * dt)
    ar, ai = mag * jnp.cos(lam_im * dt), mag * jnp.sin(lam_im * dt)
    den = lam_re * lam_re + lam_im * lam_im
    nr, ni = ar - 1.0, ai
    fr, fi = (nr * lam_re + ni * lam_im) / den, (ni * lam_re - nr * lam_im) / den
    bbr = fr[..., None] * b_re - fi[..., None] * b_im
    bbi = fr[..., None] * b_im + fi[..., None] * b_re
    return ar, ai, bbr, bbi


def _cmul(a, b):
    return a[0] * b[0] - a[1] * b[1], a[0] * b[1] + a[1] * b[0]


def _scan_powers(ar, ai):
    a = (ar.reshape(1, SW), ai.reshape(1, SW))
    pows = [a]
    for _ in range(7):
        pows.append(_cmul(pows[-1], a))
    fr = jnp.concatenate([p[0] for p in pows], axis=0)
    fi = jnp.concatenate([p[1] for p in pows], axis=0)
    head = jnp.concatenate([pows[0][0], pows[0][1], pows[1][0], pows[1][1], pows[3][0], pows[3][1],
                            jnp.zeros((2, SW), F32)], axis=0)
    return jnp.concatenate([head, fr, fi, fr[::-1], -fi[::-1]], axis=0)


def _pack_b(bb):
    t = bb.reshape(NLC, 8, NSTATE, GCH).transpose(0, 1, 3, 2)
    return jnp.einsum("jgcp,gh->jgchp", t, jnp.eye(8, dtype=bb.dtype)).reshape(NLC, 128, LC)


def _unpack_b(db):
    t = jnp.einsum("jgchp,gh->jgcp", db.reshape(NLC, 8, GCH, 8, NSTATE), jnp.eye(8, dtype=db.dtype))
    return t.transpose(0, 1, 3, 2).reshape(GROUPS, NSTATE, GCH)


def _pack_c(cc):
    t = cc.reshape(NLC, 8, GCH, NSTATE).transpose(0, 1, 3, 2)
    return jnp.einsum("jgpc,gh->jgphc", t, jnp.eye(8, dtype=cc.dtype)).reshape(NLC, LC, 128)


def _unpack_c(dc):
    t = jnp.einsum("jgphc,gh->jgpc", dc.reshape(NLC, 8, NSTATE, 8, GCH), jnp.eye(8, dtype=dc.dtype))
    return t.transpose(0, 1, 3, 2).reshape(GROUPS, GCH, NSTATE)


SMALL = ("norm_mix", "q_norm", "k_norm", "attn_sinks", "lam_re", "lam_im", "log_dt", "ssm_b_re", "ssm_b_im",
         "ssm_c_re", "ssm_c_im", "ssm_d", "attn_branch_norm", "ssm_branch_norm", "norm_ffn")


def _pack_small(arrs):
    flat = jnp.concatenate([a.reshape(-1).astype(F32) for a in arrs])
    n = flat.shape[0]
    rows = -(-n // 128)
    rows = -(-rows // 8) * 8
    return jnp.pad(flat, (0, rows * 128 - n)).reshape(rows, 128)


def _unpack_small(pack, like):
    flat = pack.reshape(-1)
    out, off = [], 0
    for a in like:
        out.append(flat[off:off + a.size].reshape(a.shape))
        off += a.size
    return out


def local_step(x, tgt, meta_full, p, w):
    seq = x.shape[0]
    hp = jnp.concatenate([jnp.zeros((META0, D), F32), meta_full, x], axis=0)
    qn_t = jnp.tile(p["q_norm"].reshape(1, HEAD), (1, NQ))
    kn_t = jnp.tile(p["k_norm"].reshape(1, HEAD), (1, NKV))
    seg = jnp.arange(256) // HEAD
    bd = (seg[:, None] == seg[None, :]).astype(BF)
    nm = p["norm_mix"].reshape(1, D)
    sinks = p["attn_sinks"].reshape(NQ)
    dsk = p["ssm_d"].reshape(1, D)
    abn, sbn, gf = (p[k].reshape(1, D) for k in ("attn_branch_norm", "ssm_branch_norm", "norm_ffn"))

    disc_in = (p["lam_re"], p["lam_im"], p["log_dt"], p["ssm_b_re"], p["ssm_b_im"])
    (ar, ai, bbr, bbi), disc_vjp = jax.vjp(_ssm_discretize, *disc_in)
    pw = _scan_powers(ar, ai)
    brp, bip = _pack_b(bbr).astype(BF), _pack_b(bbi).astype(BF)
    crp, cip = _pack_c(p["ssm_c_re"]).astype(BF), _pack_c(p["ssm_c_im"]).astype(BF)

    xn, qr, kr, qn, kn, v, u, gg = in_proj_fwd(hp, w["w_in"], nm, qn_t, kn_t, bd)
    attn = attention_fwd(sinks, qn, kn, v)
    y, sr, si = ssm_fwd(u, pw, brp, bip, crp, cip, dsk)
    zb, zzb, mgb, h1 = mid_fwd(y, attn, gg, hp, w["w_glu"], w["w_out"], abn, sbn)
    hnb, dgub, actb, dh2b, dh1, dh1b, dgf, loss = ffn_fwd_bwd(h1, tgt, w["w_ffn_in"], w["w_ffn_out"], gf)
    dy, dattn, dggb, dzzb, dabn, dsbn = mid_bwd(dh1b, y, attn, gg, zzb, w["w_glu"], w["w_out"], abn, sbn)
    dub, da, dbr, dbi, dcr, dci, dd = ssm_bwd(dy, u, sr, si, pw, brp, bip, crp, cip, dsk)
    dqn, dkn, dv, dsink = attention_bwd(sinks, qn, kn, v, dattn)
    dproj, dh0, dnm, dqw, dkw = in_proj_bwd(dqn, dkn, dv, dub, dggb, qr, kr, hp, dh1, w["w_in"], nm, qn_t, kn_t, bd)

    big = [weight_grad("grad_w_in", xn, dproj, 1152), weight_grad("grad_w_glu", zb, dzzb, 1024),
           weight_grad("grad_w_out", mgb, dh1b, 1024), weight_grad("grad_w_ffn_in", hnb, dgub, 1408),
           weight_grad("grad_w_ffn_out", actb, dh2b, 512)]

    dar = jnp.sum(da[0:8], axis=0).reshape(GROUPS, NSTATE)
    dai = jnp.sum(da[8:16], axis=0).reshape(GROUPS, NSTATE)
    dlr, dli, dldt, dbre, dbim = disc_vjp((dar, dai, _unpack_b(dbr), _unpack_b(dbi)))
    small = {
        "norm_mix": dnm, "q_norm": dqw.reshape(NQ, HEAD).sum(0), "k_norm": dkw.reshape(NKV, HEAD).sum(0),
        "attn_sinks": dsink[0, 0:NQ], "lam_re": dlr, "lam_im": dli, "log_dt": dldt,
        "ssm_b_re": dbre, "ssm_b_im": dbim, "ssm_c_re": _unpack_c(dcr), "ssm_c_im": _unpack_c(dci),
        "ssm_d": dd, "attn_branch_norm": dabn, "ssm_branch_norm": dsbn, "norm_ffn": dgf,
    }
    return loss[0, 0], dh0[PAD:PAD + seq], dh0[META0:PAD], small, big


def kernel(x, meta_tokens, norm_mix, w_in, q_norm, k_norm, attn_sinks, lam_re, lam_im, log_dt, ssm_b_re, ssm_b_im, ssm_c_re, ssm_c_im, ssm_d, w_glu, attn_branch_norm, ssm_branch_norm, w_out, norm_ffn, w_ffn_in, w_ffn_out, loss_target, m_meta_tokens, m_norm_mix, m_w_in, m_q_norm, m_k_norm, m_attn_sinks, m_lam_re, m_lam_im, m_log_dt, m_ssm_b_re, m_ssm_b_im, m_ssm_c_re, m_ssm_c_im, m_ssm_d, m_w_glu, m_attn_branch_norm, m_ssm_branch_norm, m_w_out, m_norm_ffn, m_w_ffn_in, m_w_ffn_out, v_meta_tokens, v_norm_mix, v_w_in, v_q_norm, v_k_norm, v_attn_sinks, v_lam_re, v_lam_im, v_log_dt, v_ssm_b_re, v_ssm_b_im, v_ssm_c_re, v_ssm_c_im, v_ssm_d, v_w_glu, v_attn_branch_norm, v_ssm_branch_norm, v_w_out, v_norm_ffn, v_w_ffn_in, v_w_ffn_out):
    args = dict(locals())
    names = ["meta_tokens", "norm_mix", "w_in", "q_norm", "k_norm", "attn_sinks", "lam_re", "lam_im", "log_dt",
             "ssm_b_re", "ssm_b_im", "ssm_c_re", "ssm_c_im", "ssm_d", "w_glu", "attn_branch_norm",
             "ssm_branch_norm", "w_out", "norm_ffn", "w_ffn_in", "w_ffn_out"]
    big_names = [s[0] for s in SHARDED]
    xs, ys, cs = _place()
    chip = 2 * xs + ys

    shards = [args[n][0].astype(BF) for n in big_names]
    *full, meta_full = gather_weights(shards, meta_tokens)
    w = dict(zip(big_names, full))

    p = {n: args[n][0] for n in SMALL}
    loss, dx, dmeta, gsmall, gbig = local_step(x[0], loss_target[0], meta_full, p, w)
    loss = lax.psum(loss, ("x", "y", "c"))

    parts = scatter_grads(gbig)
    partial = [sum4("sum4_" + n, pt) for n, pt in zip(big_names, parts)]
    theirs = swap_with_sibling(partial)
    small_list = [gsmall[n].reshape(args[n][0].shape) for n in SMALL] + [dmeta]
    red = allreduce_small(_pack_small(small_list))
    small_red = _unpack_small(red, small_list)
    gsm = dict(zip(SMALL, small_red[:-1]))
    dmeta_full = small_red[-1]
    gmeta = lax.dynamic_slice_in_dim(dmeta_full, chip * (D // 4), D // 4, axis=1)

    out = {}
    for n, mine, sib in zip(big_names, partial, theirs):
        g, d, nm_, nv_ = adamw("adamw_" + n, args[n][0], mine, sib, args["m_" + n][0], args["v_" + n][0])
        out[n] = tuple(t[None] for t in (g, d, nm_, nv_))
    g, d, nm_, nv_ = adamw("adamw_meta", meta_tokens, gmeta, jnp.zeros_like(gmeta), m_meta_tokens, v_meta_tokens)
    out["meta_tokens"] = (g, d, nm_, nv_)
    wp = _pack_small([args[n] for n in SMALL])
    gp = _pack_small([gsm[n] for n in SMALL])
    mp = _pack_small([args["m_" + n] for n in SMALL])
    vp = _pack_small([args["v_" + n] for n in SMALL])
    _, dp, nmp, nvp = adamw("adamw_small", wp, gp, jnp.zeros_like(gp), mp, vp)
    like = [args[n] for n in SMALL]
    for n, g_, d_, m_, v_ in zip(SMALL, [gsm[n].reshape(args[n].shape) for n in SMALL], _unpack_small(dp, like),
                                 _unpack_small(nmp, like), _unpack_small(nvp, like)):
        out[n] = (g_, d_, m_, v_)

    res = [loss, dx[None]]
    for k in range(4):
        res += [out[n][k] for n in names]
    return tuple(res)
```

```python
import functools
import math

import jax
import jax.numpy as jnp
from jax import lax
from jax.experimental import pallas as pl
from jax.experimental.pallas import tpu as pltpu

F32 = jnp.float32
BF = jnp.bfloat16

D = 1024
N_META = 16
HEAD = 64
NQ = 16
NKV = 4
QW = NQ * HEAD
KVW = NKV * HEAD
WIN = 128
GROUPS = 64
GCH = 16
NSTATE = 64
SW = GROUPS * NSTATE
DFF = 2816
IN_COLS = QW + 2 * KVW + D + 2 * D
PAD = 256
META0 = PAD - N_META
EPS = 1e-6
NEG = -1e30

TM = 256
TS = 128
LC = 512
NLC = SW // LC

LR, B1, B2, AEPS, WD, STEP = 0.001, 0.9, 0.999, 1e-08, 0.01, 10
BC1 = 1.0 - B1 ** STEP
BC2 = 1.0 - B2 ** STEP

MESH = pl.DeviceIdType.MESH
ANY = pl.BlockSpec(memory_space=pl.ANY)


def _cp(sem=None, vmem_mb=48):
    return pltpu.CompilerParams(dimension_semantics=sem, vmem_limit_bytes=vmem_mb << 20)


def _full(shape):
    n = len(shape)
    return pl.BlockSpec(shape, lambda *a: (0,) * n)


def _rows(tm, width):
    return pl.BlockSpec((tm, width), lambda i: (i, 0))


def _dot(a, b):
    return jnp.dot(a, b, preferred_element_type=F32)


def _dot_nt(a, b):
    return lax.dot_general(a, b, (((1,), (1,)), ((), ())), preferred_element_type=F32)


def _dot_tn(a, b):
    return lax.dot_general(a, b, (((0,), (0,)), ((), ())), preferred_element_type=F32)


def _sigmoid(x):
    return 1.0 / (1.0 + jnp.exp(-x))


def _seg_mean(x, bd):
    outs = []
    for j in range(x.shape[1] // 256):
        c = x[:, 256 * j:256 * (j + 1)]
        hi = c.astype(BF)
        lo = (c - hi.astype(F32)).astype(BF)
        outs.append(_dot(hi, bd) + _dot(lo, bd))
    out = outs[0] if len(outs) == 1 else jnp.concatenate(outs, axis=1)
    return out * (1.0 / HEAD)


_GC = math.sqrt(2.0 / math.pi)


def _gelu(x):
    t = jnp.tanh(_GC * (x + 0.044715 * x * x * x))
    return 0.5 * x * (1.0 + t), t


def _gelu_grad(x, t):
    return 0.5 * (1.0 + t) + 0.5 * x * (1.0 - t * t) * _GC * (1.0 + 3.0 * 0.044715 * x * x)


def in_proj_fwd(hp, w_in, nm, qn_t, kn_t, bd):
    lp = hp.shape[0]

    def body(h_ref, w_ref, nm_ref, qn_ref, kn_ref, bd_ref,
             xn_o, qr_o, kr_o, qn_o, kn_o, v_o, u_o, gg_o):
        h = h_ref[...]
        r = lax.rsqrt(jnp.mean(h * h, axis=-1, keepdims=True) + EPS)
        xb = ((h * r) * nm_ref[...]).astype(BF)
        xn_o[...] = xb
        bdv = bd_ref[...]
        q = _dot(xb, w_ref[:, 0:QW])
        qr_o[...] = q
        qn_o[...] = (q * lax.rsqrt(_seg_mean(q * q, bdv) + EPS) * qn_ref[...]).astype(BF)
        k = _dot(xb, w_ref[:, QW:QW + KVW])
        kr_o[...] = k
        kn_o[...] = (k * lax.rsqrt(_seg_mean(k * k, bdv) + EPS) * kn_ref[...]).astype(BF)
        v_o[...] = _dot(xb, w_ref[:, QW + KVW:QW + 2 * KVW]).astype(BF)
        u_o[...] = _dot(xb, w_ref[:, QW + 2 * KVW:QW + 2 * KVW + D])
        gg_o[...] = _dot(xb, w_ref[:, QW + 2 * KVW + D:IN_COLS])

    outs = [(D, BF), (QW, F32), (KVW, F32), (QW, BF), (KVW, BF), (KVW, BF), (D, F32), (2 * D, F32)]
    return pl.pallas_call(
        body, name="in_proj_fwd", grid=(lp // TM,),
        in_specs=[_rows(TM, D), _full((D, IN_COLS)), _full((1, D)), _full((1, QW)), _full((1, KVW)),
                  _full((256, 256))],
        out_specs=[_rows(TM, w) for w, _ in outs],
        out_shape=[jax.ShapeDtypeStruct((lp, w), t) for w, t in outs],
        compiler_params=_cp(("arbitrary",)),
    )(hp, w_in, nm, qn_t, kn_t, bd)


def in_proj_bwd(dqn, dkn, dv, du, dgg, qr, kr, hp, dh1, w_in, nm, qn_t, kn_t, bd):
    lp = hp.shape[0]

    def body(dqn_ref, dkn_ref, dv_ref, du_ref, dgg_ref, qr_ref, kr_ref, h_ref, dh1_ref,
             w_ref, nm_ref, qn_ref, kn_ref, bd_ref,
             dproj_o, dh0_o, dnm_o, dqw_o, dkw_o):
        i = pl.program_id(0)

        @pl.when(i == 0)
        def _():
            dnm_o[...] = jnp.zeros_like(dnm_o)
            dqw_o[...] = jnp.zeros_like(dqw_o)
            dkw_o[...] = jnp.zeros_like(dkw_o)

        bdv = bd_ref[...]

        def norm_bwd(x, dy, g, acc_ref):
            rr = lax.rsqrt(_seg_mean(x * x, bdv) + EPS)
            xh = x * rr
            acc_ref[...] += jnp.sum(dy * xh, axis=0, keepdims=True)
            dxh = dy * g
            return rr * (dxh - xh * _seg_mean(dxh * xh, bdv))

        dq = norm_bwd(qr_ref[...], dqn_ref[...], qn_ref[...], dqw_o)
        dk = norm_bwd(kr_ref[...], dkn_ref[...], kn_ref[...], dkw_o)
        dproj_o[:, 0:QW] = dq.astype(BF)
        dproj_o[:, QW:QW + KVW] = dk.astype(BF)
        dproj_o[:, QW + KVW:QW + 2 * KVW] = dv_ref[...].astype(BF)
        dproj_o[:, QW + 2 * KVW:QW + 2 * KVW + D] = du_ref[...]
        dproj_o[:, QW + 2 * KVW + D:IN_COLS] = dgg_ref[...]
        dxn = jnp.zeros((TM, D), F32)
        for c0 in range(0, IN_COLS, 512):
            dxn += _dot_nt(dproj_o[:, c0:c0 + 512], w_ref[:, c0:c0 + 512])
        h = h_ref[...]
        r = lax.rsqrt(jnp.mean(h * h, axis=-1, keepdims=True) + EPS)
        xh = h * r
        dnm_o[...] += jnp.sum(dxn * xh, axis=0, keepdims=True)
        dxh = dxn * nm_ref[...]
        dh0_o[...] = dh1_ref[...] + r * (dxh - xh * jnp.mean(dxh * xh, axis=-1, keepdims=True))

    return pl.pallas_call(
        body, name="in_proj_bwd", grid=(lp // TM,),
        in_specs=[_rows(TM, QW), _rows(TM, KVW), _rows(TM, KVW), _rows(TM, D), _rows(TM, 2 * D),
                  _rows(TM, QW), _rows(TM, KVW), _rows(TM, D), _rows(TM, D),
                  _full((D, IN_COLS)), _full((1, D)), _full((1, QW)), _full((1, KVW)), _full((256, 256))],
        out_specs=[_rows(TM, IN_COLS), _rows(TM, D), _full((1, D)), _full((1, QW)), _full((1, KVW))],
        out_shape=[jax.ShapeDtypeStruct((lp, IN_COLS), BF), jax.ShapeDtypeStruct((lp, D), F32),
                   jax.ShapeDtypeStruct((1, D), F32), jax.ShapeDtypeStruct((1, QW), F32),
                   jax.ShapeDtypeStruct((1, KVW), F32)],
        compiler_params=_cp(("arbitrary",)),
    )(dqn, dkn, dv, du, dgg, qr, kr, hp, dh1, w_in, nm, qn_t, kn_t, bd)


def _att_masks(b):
    qi = lax.broadcasted_iota(jnp.int32, (4 * WIN, WIN), 0) & (WIN - 1)
    kj = lax.broadcasted_iota(jnp.int32, (4 * WIN, WIN), 1)
    mask_c = (kj <= qi) & (b * WIN + kj >= META0)
    mask_p = (kj > qi) & ((b - 1) * WIN + kj >= PAD)
    mask_m = (kj >= WIN - N_META) & (b * WIN + qi >= PAD)
    return mask_m, mask_p, mask_c


def _dup_half(ref, kp, pos, lo):
    pair = ref[:, 128 * kp:128 * kp + 128].astype(F32)
    rolled = pltpu.roll(pair, 64, axis=1)
    keep = lo if pos == 0 else jnp.logical_not(lo)
    return jnp.where(keep, pair, rolled).astype(BF)


def _stack_heads(ref, kv, lo):
    parts = []
    for pp in range(2):
        pair = ref[:, 256 * kv + 128 * pp:256 * kv + 128 * pp + 128]
        zero = jnp.zeros_like(pair)
        parts.append(jnp.where(lo, pair, zero))
        parts.append(jnp.where(lo, zero, pair))
    return jnp.concatenate(parts, axis=0)


def _unstack_heads(x4, lo):
    a = jnp.where(lo, x4[0:WIN], x4[WIN:2 * WIN])
    b = jnp.where(lo, x4[2 * WIN:3 * WIN], x4[3 * WIN:4 * WIN])
    return a, b


def _sink_col(sink_ref, kv):
    row = lax.broadcasted_iota(jnp.int32, (4 * WIN, 1), 0) >> 7
    col = jnp.full((4 * WIN, 1), sink_ref[4 * kv + 3], F32)
    for r in range(3):
        col = jnp.where(row == r, sink_ref[4 * kv + r], col)
    return col


def _att_probs(q4, kds, masks, sink4):
    scale = HEAD ** -0.5
    ss = [jnp.where(mk, _dot_nt(q4, kd) * scale, NEG) for kd, mk in zip(kds, masks)]
    m = sink4
    for s in ss:
        m = jnp.maximum(m, jnp.max(s, axis=-1, keepdims=True))
    es = [jnp.exp(s - m) for s in ss]
    esink = jnp.exp(sink4 - m)
    den = esink
    for e in es:
        den = den + jnp.sum(e, axis=-1, keepdims=True)
    inv = 1.0 / den
    return [e * inv for e in es], esink * inv


def _kv_specs():
    return [pl.BlockSpec((WIN, KVW), lambda b: (1, 0)),
            pl.BlockSpec((WIN, KVW), lambda b: (jnp.maximum(b - 1, 0), 0)),
            pl.BlockSpec((WIN, KVW), lambda b: (b, 0))]


def attention_fwd(sinks, qn, kn, v):
    lp = qn.shape[0]

    def body(sink_ref, q_ref, km_ref, kp_ref, kc_ref, vm_ref, vp_ref, vc_ref, o_ref):
        b = pl.program_id(0)
        masks = _att_masks(b)
        lo = lax.broadcasted_iota(jnp.int32, (WIN, WIN), 1) < HEAD
        for kv in range(NKV):
            kp, pos = kv // 2, kv % 2
            kds = [_dup_half(r, kp, pos, lo) for r in (km_ref, kp_ref, kc_ref)]
            vds = [_dup_half(r, kp, pos, lo) for r in (vm_ref, vp_ref, vc_ref)]
            q4 = _stack_heads(q_ref, kv, lo)
            ps, _ = _att_probs(q4, kds, masks, _sink_col(sink_ref, kv))
            o4 = _dot(ps[0].astype(BF), vds[0]) + _dot(ps[1].astype(BF), vds[1]) + _dot(ps[2].astype(BF), vds[2])
            oa, ob = _unstack_heads(o4, lo)
            o_ref[:, 256 * kv:256 * kv + 128] = oa
            o_ref[:, 256 * kv + 128:256 * kv + 256] = ob

    return pl.pallas_call(
        body, name="attention_fwd", grid=(lp // WIN,),
        in_specs=[pl.BlockSpec(memory_space=pltpu.SMEM), _rows(WIN, QW)] + _kv_specs() + _kv_specs(),
        out_specs=_rows(WIN, QW),
        out_shape=jax.ShapeDtypeStruct((lp, QW), F32),
        compiler_params=_cp(("arbitrary",)),
    )(sinks, qn, kn, kn, kn, v, v, v)


def attention_bwd(sinks, qn, kn, v, do):
    lp = qn.shape[0]
    nb = lp // WIN

    def body(sink_ref, q_ref, km_ref, kp_ref, kc_ref, vm_ref, vp_ref, vc_ref, do_ref,
             dq_ref, dk_hbm, dv_hbm, dsink_ref, dk_acc, dv_acc):
        b = pl.program_id(0)

        @pl.when(b == 0)
        def _():
            dk_acc[...] = jnp.zeros_like(dk_acc)
            dv_acc[...] = jnp.zeros_like(dv_acc)
            dsink_ref[...] = jnp.zeros_like(dsink_ref)

        masks = _att_masks(b)
        lo = lax.broadcasted_iota(jnp.int32, (WIN, WIN), 1) < HEAD
        lane8 = lax.broadcasted_iota(jnp.int32, (8, 128), 1)
        row4 = lax.broadcasted_iota(jnp.int32, (4 * WIN, 1), 0) >> 7
        starts = [WIN, pl.multiple_of(jnp.maximum(b - 1, 0) * WIN, WIN), pl.multiple_of(b * WIN, WIN)]
        scale = HEAD ** -0.5
        for kv in range(NKV):
            kp, pos = kv // 2, kv % 2
            keep = lo if pos == 0 else jnp.logical_not(lo)
            kds = [_dup_half(r, kp, pos, lo) for r in (km_ref, kp_ref, kc_ref)]
            vds = [_dup_half(r, kp, pos, lo) for r in (vm_ref, vp_ref, vc_ref)]
            q4 = _stack_heads(q_ref, kv, lo)
            do4 = _stack_heads(do_ref, kv, lo)
            ps, psink = _att_probs(q4, kds, masks, _sink_col(sink_ref, kv))
            dps = [_dot_nt(do4, vd) for vd in vds]
            delta = jnp.sum(ps[0] * dps[0] + ps[1] * dps[1] + ps[2] * dps[2], axis=-1, keepdims=True)
            dss = [(p * (dp - delta) * scale).astype(BF) for p, dp in zip(ps, dps)]
            dsk = psink * delta
            for r in range(4):
                val = -jnp.sum(jnp.where(row4 == r, dsk, 0.0), axis=0, keepdims=True)
                dsink_ref[...] += jnp.where(lane8 == 4 * kv + r, val, 0.0)
            dq4 = _dot(dss[0], kds[0]) + _dot(dss[1], kds[1]) + _dot(dss[2], kds[2])
            dqa, dqb = _unstack_heads(dq4, lo)
            dq_ref[:, 256 * kv:256 * kv + 128] = dqa
            dq_ref[:, 256 * kv + 128:256 * kv + 256] = dqb
            for x in range(3):
                dkd = _dot_tn(dss[x], q4)
                dvd = _dot_tn(ps[x].astype(BF), do4)
                dkd = jnp.where(keep, dkd + pltpu.roll(dkd, 64, axis=1), 0.0)
                dvd = jnp.where(keep, dvd + pltpu.roll(dvd, 64, axis=1), 0.0)
                dk_acc[pl.ds(starts[x], WIN), 128 * kp:128 * kp + 128] += dkd
                dv_acc[pl.ds(starts[x], WIN), 128 * kp:128 * kp + 128] += dvd

        @pl.when(b == nb - 1)
        def _():
            pltpu.sync_copy(dk_acc, dk_hbm)
            pltpu.sync_copy(dv_acc, dv_hbm)

    return pl.pallas_call(
        body, name="attention_bwd", grid=(nb,),
        in_specs=[pl.BlockSpec(memory_space=pltpu.SMEM), _rows(WIN, QW)] + _kv_specs() + _kv_specs()
                 + [_rows(WIN, QW)],
        out_specs=[_rows(WIN, QW), ANY, ANY, _full((8, 128))],
        out_shape=[jax.ShapeDtypeStruct((lp, QW), F32), jax.ShapeDtypeStruct((lp, KVW), F32),
                   jax.ShapeDtypeStruct((lp, KVW), F32), jax.ShapeDtypeStruct((8, 128), F32)],
        scratch_shapes=[pltpu.VMEM((lp, KVW), F32), pltpu.VMEM((lp, KVW), F32)],
        compiler_params=_cp(("arbitrary",)),
    )(sinks, qn, kn, kn, kn, v, v, v, do)


def _scan_tile(xr, xi, a1, a2, a4, rid, reverse):
    for k, (akr, aki) in zip((1, 2, 4), (a1, a2, a4)):
        if reverse:
            sr = pltpu.roll(xr, 8 - k, axis=0)
            si = pltpu.roll(xi, 8 - k, axis=0)
            ok = rid < 8 - k
        else:
            sr = pltpu.roll(xr, k, axis=0)
            si = pltpu.roll(xi, k, axis=0)
            ok = rid >= k
        sr = jnp.where(ok, sr, 0.0)
        si = jnp.where(ok, si, 0.0)
        xr, xi = xr + akr * sr - aki * si, xi + akr * si + aki * sr
    return xr, xi


def ssm_fwd(u, pw, br, bi, cr, ci, dsk):
    lp = u.shape[0]

    def body(u_ref, pw_ref, br_ref, bi_ref, cr_ref, ci_ref, d_ref, y_ref, sr_ref, si_ref, car_r, car_i):
        i = pl.program_id(0)

        @pl.when(i == 0)
        def _():
            car_r[...] = jnp.zeros_like(car_r)
            car_i[...] = jnp.zeros_like(car_i)

        u = u_ref[...]
        ub = u.astype(BF)
        for j in range(NLC):
            uj = ub[:, 128 * j:128 * j + 128]
            sr_ref[:, LC * j:LC * j + LC] = _dot(uj, br_ref[j])
            si_ref[:, LC * j:LC * j + LC] = _dot(uj, bi_ref[j])
        rid = lax.broadcasted_iota(jnp.int32, (8, LC), 0)
        for j in range(NLC):
            ls = slice(LC * j, LC * j + LC)
            a1 = (pw_ref[0:1, ls], pw_ref[1:2, ls])
            a2 = (pw_ref[2:3, ls], pw_ref[3:4, ls])
            a4 = (pw_ref[4:5, ls], pw_ref[5:6, ls])
            pr, pi = pw_ref[8:16, ls], pw_ref[16:24, ls]

            def group(g, carry):
                cr_, ci_ = carry
                rows = pl.ds(pl.multiple_of(g * 8, 8), 8)
                xr, xi = _scan_tile(sr_ref[rows, ls], si_ref[rows, ls], a1, a2, a4, rid, False)
                xr, xi = xr + pr * cr_ - pi * ci_, xi + pr * ci_ + pi * cr_
                sr_ref[rows, ls] = xr
                si_ref[rows, ls] = xi
                return xr[7:8, :], xi[7:8, :]

            cr_, ci_ = lax.fori_loop(0, TS // 8, group, (car_r[:, ls], car_i[:, ls]))
            car_r[:, ls] = cr_
            car_i[:, ls] = ci_
        for j in range(NLC):
            ls = slice(LC * j, LC * j + LC)
            y_ref[:, 128 * j:128 * j + 128] = (_dot(sr_ref[:, ls].astype(BF), cr_ref[j])
                                               - _dot(si_ref[:, ls].astype(BF), ci_ref[j]))
        y_ref[...] += d_ref[...] * u

    return pl.pallas_call(
        body, name="ssm_fwd", grid=(lp // TS,),
        in_specs=[_rows(TS, D), _full((40, SW)), _full((NLC, 128, LC)), _full((NLC, 128, LC)),
                  _full((NLC, LC, 128)), _full((NLC, LC, 128)), _full((1, D))],
        out_specs=[_rows(TS, D), _rows(TS, SW), _rows(TS, SW)],
        out_shape=[jax.ShapeDtypeStruct((lp, D), F32), jax.ShapeDtypeStruct((lp, SW), F32),
                   jax.ShapeDtypeStruct((lp, SW), F32)],
        scratch_shapes=[pltpu.VMEM((1, SW), F32), pltpu.VMEM((1, SW), F32)],
        compiler_params=_cp(("arbitrary",)),
    )(u, pw, br, bi, cr, ci, dsk)


def ssm_bwd(dy, u, sr, si, pw, br, bi, cr, ci, dsk):
    lp = u.shape[0]
    nt = lp // TS

    def rev(i):
        return (nt - 1 - i, 0)

    def prev8(i):
        return (jnp.maximum((nt - 1 - i) * (TS // 8) - 1, 0), 0)

    def body(dy_ref, u_ref, sr_ref, si_ref, pr8_ref, pi8_ref, pw_ref, br_ref, bi_ref, cr_ref, ci_ref, d_ref,
             du_ref, da_ref, dbr_ref, dbi_ref, dcr_ref, dci_ref, dd_ref, gr_s, gi_s, car_r, car_i):
        i = pl.program_id(0)

        @pl.when(i == 0)
        def _():
            car_r[...] = jnp.zeros_like(car_r)
            car_i[...] = jnp.zeros_like(car_i)
            for ref in (da_ref, dbr_ref, dbi_ref, dcr_ref, dci_ref, dd_ref):
                ref[...] = jnp.zeros_like(ref)

        first = (i == nt - 1).astype(F32)
        dy = dy_ref[...]
        u = u_ref[...]
        dyb = dy.astype(BF)
        ub = u.astype(BF)
        dd_ref[...] += jnp.sum(dy * u, axis=0, keepdims=True)
        for j in range(NLC):
            ls = slice(LC * j, LC * j + LC)
            dyj = dyb[:, 128 * j:128 * j + 128]
            gr_s[:, ls] = _dot_nt(dyj, cr_ref[j])
            gi_s[:, ls] = -_dot_nt(dyj, ci_ref[j])
            dcr_ref[j] += _dot_tn(sr_ref[:, ls].astype(BF), dyj)
            dci_ref[j] -= _dot_tn(si_ref[:, ls].astype(BF), dyj)
        rid = lax.broadcasted_iota(jnp.int32, (8, LC), 0)
        for j in range(NLC):
            ls = slice(LC * j, LC * j + LC)
            a1 = (pw_ref[0:1, ls], -pw_ref[1:2, ls])
            a2 = (pw_ref[2:3, ls], -pw_ref[3:4, ls])
            a4 = (pw_ref[4:5, ls], -pw_ref[5:6, ls])
            pr, pi = pw_ref[24:32, ls], pw_ref[32:40, ls]

            def one_group(rows, spr, spi, carry):
                cr_, ci_ = carry
                gr, gi = _scan_tile(gr_s[rows, ls], gi_s[rows, ls], a1, a2, a4, rid, True)
                gr, gi = gr + pr * cr_ - pi * ci_, gi + pr * ci_ + pi * cr_
                gr_s[rows, ls] = gr
                gi_s[rows, ls] = gi
                s_r = jnp.where(rid >= 1, pltpu.roll(sr_ref[rows, ls], 1, axis=0), spr[7:8, :])
                s_i = jnp.where(rid >= 1, pltpu.roll(si_ref[rows, ls], 1, axis=0), spi[7:8, :])
                da_ref[0:8, ls] += gr * s_r + gi * s_i
                da_ref[8:16, ls] += gi * s_r - gr * s_i
                return gr[0:1, :], gi[0:1, :]

            def group(k, carry):
                g = TS // 8 - 1 - k
                rows = pl.ds(pl.multiple_of(g * 8, 8), 8)
                before = pl.ds(pl.multiple_of(g * 8 - 8, 8), 8)
                return one_group(rows, sr_ref[before, ls], si_ref[before, ls], carry)

            carry = lax.fori_loop(0, TS // 8 - 1, group, (car_r[:, ls], car_i[:, ls]))
            keep = 1.0 - first
            cr_, ci_ = one_group(pl.ds(0, 8), pr8_ref[:, ls] * keep, pi8_ref[:, ls] * keep, carry)
            car_r[:, ls] = cr_
            car_i[:, ls] = ci_
        for j in range(NLC):
            ls = slice(LC * j, LC * j + LC)
            uj = ub[:, 128 * j:128 * j + 128]
            grb = gr_s[:, ls].astype(BF)
            gib = gi_s[:, ls].astype(BF)
            du_ref[:, 128 * j:128 * j + 128] = (_dot_nt(grb, br_ref[j]) + _dot_nt(gib, bi_ref[j])
                                                + d_ref[:, 128 * j:128 * j + 128] * dy[:, 128 * j:128 * j + 128]
                                                ).astype(BF)
            dbr_ref[j] += _dot_tn(uj, grb)
            dbi_ref[j] += _dot_tn(uj, gib)

    return pl.pallas_call(
        body, name="ssm_bwd", grid=(nt,),
        in_specs=[pl.BlockSpec((TS, D), rev), pl.BlockSpec((TS, D), rev), pl.BlockSpec((TS, SW), rev),
                  pl.BlockSpec((TS, SW), rev), pl.BlockSpec((8, SW), prev8), pl.BlockSpec((8, SW), prev8),
                  _full((40, SW)), _full((NLC, 128, LC)), _full((NLC, 128, LC)),
                  _full((NLC, LC, 128)), _full((NLC, LC, 128)), _full((1, D))],
        out_specs=[pl.BlockSpec((TS, D), rev), _full((16, SW)), _full((NLC, 128, LC)), _full((NLC, 128, LC)),
                   _full((NLC, LC, 128)), _full((NLC, LC, 128)), _full((1, D))],
        out_shape=[jax.ShapeDtypeStruct((lp, D), BF), jax.ShapeDtypeStruct((16, SW), F32),
                   jax.ShapeDtypeStruct((NLC, 128, LC), F32), jax.ShapeDtypeStruct((NLC, 128, LC), F32),
                   jax.ShapeDtypeStruct((NLC, LC, 128), F32), jax.ShapeDtypeStruct((NLC, LC, 128), F32),
                   jax.ShapeDtypeStruct((1, D), F32)],
        scratch_shapes=[pltpu.VMEM((TS, SW), F32), pltpu.VMEM((TS, SW), F32),
                        pltpu.VMEM((1, SW), F32), pltpu.VMEM((1, SW), F32)],
        compiler_params=_cp(("arbitrary",)),
    )(dy, u, sr, si, sr, si, pw, br, bi, cr, ci, dsk)


def _merge_parts(attn, zz, gg, abn, sbn):
    za, zb = zz[:, 0:D], zz[:, D:2 * D]
    sgz = _sigmoid(zb)
    ssm = za * sgz
    ra = lax.rsqrt(jnp.mean(attn * attn, axis=-1, keepdims=True) + EPS)
    rs = lax.rsqrt(jnp.mean(ssm * ssm, axis=-1, keepdims=True) + EPS)
    ah, sh = attn * ra, ssm * rs
    sga, sgs = _sigmoid(gg[:, 0:D]), _sigmoid(gg[:, D:2 * D])
    return za, sgz, ra, rs, ah, sh, sga, sgs, ah * abn, sh * sbn


def mid_fwd(y, attn, gg, hp, w_glu, w_out, abn, sbn):
    lp = y.shape[0]

    def body(y_ref, a_ref, gg_ref, h_ref, wg_ref, wo_ref, abn_ref, sbn_ref, z_o, zz_o, mg_o, h1_o):
        z, _ = _gelu(y_ref[...])
        zb = z.astype(BF)
        z_o[...] = zb
        zz = _dot(zb, wg_ref[...])
        zz_o[...] = zz.astype(BF)
        _, _, _, _, _, _, sga, sgs, an, sn = _merge_parts(a_ref[...], zz, gg_ref[...], abn_ref[...], sbn_ref[...])
        mb = (sga * an + sgs * sn).astype(BF)
        mg_o[...] = mb
        h1_o[...] = h_ref[...] + _dot(mb, wo_ref[...])

    return pl.pallas_call(
        body, name="mid_fwd", grid=(lp // TM,),
        in_specs=[_rows(TM, D), _rows(TM, D), _rows(TM, 2 * D), _rows(TM, D), _full((D, 2 * D)), _full((D, D)),
                  _full((1, D)), _full((1, D))],
        out_specs=[_rows(TM, D), _rows(TM, 2 * D), _rows(TM, D), _rows(TM, D)],
        out_shape=[jax.ShapeDtypeStruct((lp, D), BF), jax.ShapeDtypeStruct((lp, 2 * D), BF),
                   jax.ShapeDtypeStruct((lp, D), BF), jax.ShapeDtypeStruct((lp, D), F32)],
        compiler_params=_cp(("arbitrary",)),
    )(y, attn, gg, hp, w_glu, w_out, abn, sbn)


def mid_bwd(dh1b, y, attn, gg, zzb, w_glu, w_out, abn, sbn):
    lp = y.shape[0]

    def body(dh_ref, y_ref, a_ref, gg_ref, zz_ref, wg_ref, wo_ref, abn_ref, sbn_ref,
             dy_o, dattn_o, dgg_o, dzz_o, dabn_o, dsbn_o):
        i = pl.program_id(0)

        @pl.when(i == 0)
        def _():
            dabn_o[...] = jnp.zeros_like(dabn_o)
            dsbn_o[...] = jnp.zeros_like(dsbn_o)

        dm = _dot_nt(dh_ref[...], wo_ref[...])
        abn, sbn = abn_ref[...], sbn_ref[...]
        za, sgz, ra, rs, ah, sh, sga, sgs, an, sn = _merge_parts(
            a_ref[...], zz_ref[...].astype(F32), gg_ref[...], abn, sbn)
        dgg_o[:, 0:D] = (dm * an * sga * (1.0 - sga)).astype(BF)
        dgg_o[:, D:2 * D] = (dm * sn * sgs * (1.0 - sgs)).astype(BF)
        da = dm * sga
        dabn_o[...] += jnp.sum(da * ah, axis=0, keepdims=True)
        dah = da * abn
        dattn_o[...] = (ra * (dah - ah * jnp.mean(dah * ah, axis=-1, keepdims=True))).astype(BF)
        ds = dm * sgs
        dsbn_o[...] += jnp.sum(ds * sh, axis=0, keepdims=True)
        dsh = ds * sbn
        dssm = rs * (dsh - sh * jnp.mean(dsh * sh, axis=-1, keepdims=True))
        dzz_o[:, 0:D] = (dssm * sgz).astype(BF)
        dzz_o[:, D:2 * D] = (dssm * za * sgz * (1.0 - sgz)).astype(BF)
        dz = _dot_nt(dzz_o[...], wg_ref[...])
        yv = y_ref[...]
        _, t = _gelu(yv)
        dy_o[...] = dz * _gelu_grad(yv, t)

    return pl.pallas_call(
        body, name="mid_bwd", grid=(lp // TM,),
        in_specs=[_rows(TM, D), _rows(TM, D), _rows(TM, D), _rows(TM, 2 * D), _rows(TM, 2 * D),
                  _full((D, 2 * D)), _full((D, D)), _full((1, D)), _full((1, D))],
        out_specs=[_rows(TM, D), _rows(TM, D), _rows(TM, 2 * D), _rows(TM, 2 * D), _full((1, D)), _full((1, D))],
        out_shape=[jax.ShapeDtypeStruct((lp, D), F32), jax.ShapeDtypeStruct((lp, D), BF),
                   jax.ShapeDtypeStruct((lp, 2 * D), BF), jax.ShapeDtypeStruct((lp, 2 * D), BF),
                   jax.ShapeDtypeStruct((1, D), F32), jax.ShapeDtypeStruct((1, D), F32)],
        compiler_params=_cp(("arbitrary",)),
    )(dh1b, y, attn, gg, zzb, w_glu, w_out, abn, sbn)


def ffn_fwd_bwd(h1, tgt, w_fi, w_fo, gf):
    lp = h1.shape[0]
    tf = 128

    def body(h_ref, t_ref, wi_ref, wo_ref, gf_ref, hn_o, dgu_o, act_o, dh2_o, dh1_o, dh1b_o, dgf_o, loss_o):
        i = pl.program_id(0)

        @pl.when(i == 0)
        def _():
            dgf_o[...] = jnp.zeros_like(dgf_o)
            loss_o[...] = jnp.zeros_like(loss_o)

        h = h_ref[...]
        r = lax.rsqrt(jnp.mean(h * h, axis=-1, keepdims=True) + EPS)
        xh = h * r
        hb = (xh * gf_ref[...]).astype(BF)
        hn_o[...] = hb
        gate = _dot(hb, wi_ref[:, 0:DFF])
        up = _dot(hb, wi_ref[:, DFF:2 * DFF])
        sg = _sigmoid(gate)
        sl = gate * sg
        ab = (sl * up).astype(BF)
        act_o[...] = ab
        h2 = h + _dot(ab, wo_ref[...])
        real = (i >= PAD // tf).astype(F32)
        e = (h2 - t_ref[...]) * real
        loss_o[...] += 0.5 * jnp.sum(jnp.sum(e * e, axis=-1, keepdims=True), axis=0, keepdims=True) * (1.0 / D)
        dh2 = e * (1.0 / D)
        db = dh2.astype(BF)
        dh2_o[...] = db
        dact = _dot_nt(db, wo_ref[...])
        dgate = (dact * up * (sg * (1.0 + gate * (1.0 - sg)))).astype(BF)
        dup = (dact * sl).astype(BF)
        dgu_o[:, 0:DFF] = dgate
        dgu_o[:, DFF:2 * DFF] = dup
        dhn = _dot_nt(dgate, wi_ref[:, 0:DFF]) + _dot_nt(dup, wi_ref[:, DFF:2 * DFF])
        dgf_o[...] += jnp.sum(dhn * xh, axis=0, keepdims=True)
        dxh = dhn * gf_ref[...]
        dh1 = dh2 + r * (dxh - xh * jnp.mean(dxh * xh, axis=-1, keepdims=True))
        dh1_o[...] = dh1
        dh1b_o[...] = dh1.astype(BF)

    return pl.pallas_call(
        body, name="ffn_fwd_bwd", grid=(lp // tf,),
        in_specs=[_rows(tf, D), pl.BlockSpec((tf, D), lambda i: (jnp.maximum(i - PAD // tf, 0), 0)),
                  _full((D, 2 * DFF)), _full((DFF, D)), _full((1, D))],
        out_specs=[_rows(tf, D), _rows(tf, 2 * DFF), _rows(tf, DFF), _rows(tf, D), _rows(tf, D), _rows(tf, D),
                   _full((1, D)), _full((1, 1))],
        out_shape=[jax.ShapeDtypeStruct((lp, D), BF), jax.ShapeDtypeStruct((lp, 2 * DFF), BF),
                   jax.ShapeDtypeStruct((lp, DFF), BF), jax.ShapeDtypeStruct((lp, D), BF),
                   jax.ShapeDtypeStruct((lp, D), F32), jax.ShapeDtypeStruct((lp, D), BF),
                   jax.ShapeDtypeStruct((1, D), F32), jax.ShapeDtypeStruct((1, 1), F32)],
        compiler_params=_cp(("arbitrary",), vmem_mb=56),
    )(h1, tgt, w_fi, w_fo, gf)


def weight_grad(name, a, b, tn):
    lp, k = a.shape
    n = b.shape[1]
    tr = next(t for t in (768, 512, 384, 256, 128) if lp % t == 0)

    def body(a_ref, b_ref, o_ref):
        @pl.when(pl.program_id(1) == 0)
        def _():
            o_ref[...] = jnp.zeros_like(o_ref)

        o_ref[...] += _dot_tn(a_ref[...], b_ref[...])

    return pl.pallas_call(
        body, name=name, grid=(n // tn, lp // tr),
        in_specs=[pl.BlockSpec((tr, k), lambda j, m: (m, 0)), pl.BlockSpec((tr, tn), lambda j, m: (m, j))],
        out_specs=pl.BlockSpec((k, tn), lambda j, m: (0, j)),
        out_shape=jax.ShapeDtypeStruct((k, n), F32),
        compiler_params=_cp(("arbitrary", "arbitrary")),
    )(a, b)


def _adamw_math(w, g, m, v):
    m = B1 * m + (1.0 - B1) * g
    v = B2 * v + (1.0 - B2) * (g * g)
    delta = -LR * ((m / BC1) / (jnp.sqrt(v / BC2) + AEPS) + WD * w)
    return delta, m, v


def _row_tile(r):
    return next((t for t in (256, 128, 64, 32, 16, 8) if r % t == 0), r)


def adamw(name, w, g, m, v):
    r, c = w.shape
    tr = _row_tile(r)

    def body(w_ref, g_ref, m_ref, v_ref, d_o, m_o, v_o):
        d_o[...], m_o[...], v_o[...] = _adamw_math(w_ref[...], g_ref[...], m_ref[...], v_ref[...])

    spec = pl.BlockSpec((tr, c), lambda i: (i, 0))
    return pl.pallas_call(
        body, name=name, grid=(r // tr,), in_specs=[spec] * 4, out_specs=[spec] * 3,
        out_shape=[jax.ShapeDtypeStruct((r, c), F32)] * 3,
        compiler_params=_cp(("arbitrary",)),
    )(w, g, m, v)


def pair_sum(name, a, b):
    r, c = a.shape
    tr = _row_tile(r)

    def body(a_ref, b_ref, o_ref):
        o_ref[...] = (a_ref[...] + b_ref[...]).astype(BF)

    spec = pl.BlockSpec((tr, c), lambda i: (i, 0))
    return pl.pallas_call(
        body, name=name, grid=(r // tr,), in_specs=[spec] * 2, out_specs=spec,
        out_shape=jax.ShapeDtypeStruct((r, c), BF),
        compiler_params=_cp(("arbitrary",)),
    )(a, b)


def sum4(name, parts):
    _, r, c = parts.shape
    tr = _row_tile(r)

    def body(p_ref, o_ref):
        o_ref[...] = ((p_ref[0].astype(F32) + p_ref[1].astype(F32))
                      + (p_ref[2].astype(F32) + p_ref[3].astype(F32)))

    return pl.pallas_call(
        body, name=name, grid=(r // tr,),
        in_specs=[pl.BlockSpec((4, tr, c), lambda i: (0, i, 0))], out_specs=pl.BlockSpec((tr, c), lambda i: (i, 0)),
        out_shape=jax.ShapeDtypeStruct((r, c), F32),
        compiler_params=_cp(("arbitrary",)),
    )(parts)


def _place():
    x, y, c = lax.axis_index("x"), lax.axis_index("y"), lax.axis_index("c")
    return x, y, c


def _other_chips(x, y):
    return [(1 - x, y), (x, 1 - y), (1 - x, 1 - y)]


SHARDED = (("w_in", D, IN_COLS, 1), ("w_glu", D, 2 * D, 1), ("w_out", D, D, 0),
           ("w_ffn_in", D, 2 * DFF, 1), ("w_ffn_out", DFF, D, 0))


def _shard_of(ref, axis, k, size):
    if axis == 0:
        return ref.at[pl.ds(k * size, size), :]
    return ref.at[:, pl.ds(k * size, size)]


def gather_weights(shards, meta):
    items = [(s, r, c, ax) for s, (_, r, c, ax) in zip(shards, SHARDED)] + [(meta, N_META, D, 1)]
    n = len(items)

    def body(*refs):
        ins, outs = refs[:n], refs[n:2 * n]
        ici_send, ici_recv, d2d_send, d2d_recv, local_sems = refs[2 * n:]
        x, y, c = _place()
        mine = 2 * x + y
        chips = _other_chips(x, y)

        def piece(t, k, h):
            _, r, cc, ax = items[t]
            if ax == 0:
                return outs[t].at[pl.ds(k * (r // 4) + h * (r // 8), r // 8), :]
            return outs[t].at[pl.ds(h * (r // 2), r // 2), pl.ds(k * (cc // 4), cc // 4)]

        def src_half(t, h):
            rs = ins[t].shape[0]
            return ins[t].at[pl.ds(h * (rs // 2), rs // 2), :]

        local, first, passed = [], [], []
        for t in range(n):
            _, r, cc, ax = items[t]
            size = (r if ax == 0 else cc) // 4
            loc = pltpu.make_async_copy(ins[t], _shard_of(outs[t], ax, mine, size), local_sems.at[t])
            loc.start()
            local.append(loc)
            for j, chip in enumerate(chips):
                cp = pltpu.make_async_remote_copy(
                    src_ref=src_half(t, c), dst_ref=piece(t, mine, c),
                    send_sem=ici_send.at[3 * t + j], recv_sem=ici_recv.at[3 * t + j],
                    device_id=(*chip, c), device_id_type=MESH)
                cp.start()
                first.append(cp)
        for t in range(n):
            for j, (px, py) in enumerate(chips):
                got = piece(t, 2 * px + py, c)
                pltpu.make_async_remote_copy(
                    src_ref=src_half(t, c), dst_ref=got, send_sem=ici_send.at[3 * t + j],
                    recv_sem=ici_recv.at[3 * t + j], device_id=(px, py, c), device_id_type=MESH).wait_recv()
                cp = pltpu.make_async_remote_copy(
                    src_ref=got, dst_ref=got, send_sem=d2d_send.at[3 * t + j], recv_sem=d2d_recv.at[3 * t + j],
                    device_id=(x, y, 1 - c), device_id_type=MESH)
                cp.start()
                passed.append(cp)
        for t in range(n):
            for j, (px, py) in enumerate(chips):
                other = piece(t, 2 * px + py, 1 - c)
                pltpu.make_async_remote_copy(
                    src_ref=other, dst_ref=other, send_sem=d2d_send.at[3 * t + j], recv_sem=d2d_recv.at[3 * t + j],
                    device_id=(x, y, 1 - c), device_id_type=MESH).wait_recv()
        for cp in first + passed:
            cp.wait_send()
        for loc in local:
            loc.wait()

    return pl.pallas_call(
        body, name="gather_weights",
        in_specs=[ANY] * n, out_specs=[ANY] * n,
        out_shape=[jax.ShapeDtypeStruct((r, c), s.dtype) for s, r, c, _ in items],
        scratch_shapes=[pltpu.SemaphoreType.DMA((3 * n,)), pltpu.SemaphoreType.DMA((3 * n,)),
                        pltpu.SemaphoreType.DMA((3 * n,)), pltpu.SemaphoreType.DMA((3 * n,)),
                        pltpu.SemaphoreType.DMA((n,))],
        compiler_params=pltpu.CompilerParams(has_side_effects=True),
    )(*shards, meta)


def _half_of(ref, ax, h):
    r, c = ref.shape
    if ax == 1:
        return ref.at[pl.ds(h * (r // 2), r // 2), :]
    return ref.at[:, pl.ds(h * (c // 2), c // 2)]


def _half_shape(r, c, ax):
    return (r // 2, c) if ax == 1 else (r, c // 2)


def pair_exchange(grads):
    n = len(grads)

    def body(*refs):
        ins, own, got = refs[:n], refs[n:2 * n], refs[2 * n:3 * n]
        send_sems, recv_sems, local_sems = refs[3 * n:]
        x, y, c = _place()
        cps, locs = [], []
        for t, (_, _, _, ax) in enumerate(SHARDED):
            cp = pltpu.make_async_remote_copy(
                src_ref=_half_of(ins[t], ax, 1 - c), dst_ref=got[t], send_sem=send_sems.at[t],
                recv_sem=recv_sems.at[t], device_id=(x, y, 1 - c), device_id_type=MESH)
            cp.start()
            cps.append(cp)
            loc = pltpu.make_async_copy(_half_of(ins[t], ax, c), own[t], local_sems.at[t])
            loc.start()
            locs.append(loc)
        for cp in cps:
            cp.wait()
        for loc in locs:
            loc.wait()

    halves = [jax.ShapeDtypeStruct(_half_shape(r, c, ax), F32) for _, r, c, ax in SHARDED]
    res = pl.pallas_call(
        body, name="pair_exchange",
        in_specs=[ANY] * n, out_specs=[ANY] * (2 * n), out_shape=halves + halves,
        scratch_shapes=[pltpu.SemaphoreType.DMA((n,)), pltpu.SemaphoreType.DMA((n,)),
                        pltpu.SemaphoreType.DMA((n,))],
        compiler_params=pltpu.CompilerParams(has_side_effects=True),
    )(*grads)
    return res[:n], res[n:]


def scatter_grads(halves):
    n = len(halves)

    def shapes():
        out = []
        for _, r, c, ax in SHARDED:
            hr, hc = _half_shape(r, c, ax)
            out.append((hr // 4, hc) if ax == 0 else (hr, hc // 4))
        return out

    def body(*refs):
        ins, outs = refs[:n], refs[n:2 * n]
        send_sems, recv_sems, local_sems = refs[2 * n:]
        x, y, c = _place()
        mine = 2 * x + y
        started = []
        for t, (src, dst, (_, _, _, ax)) in enumerate(zip(ins, outs, SHARDED)):
            size = src.shape[ax] // 4
            loc = pltpu.make_async_copy(_shard_of(src, ax, mine, size), dst.at[3], local_sems.at[t])
            loc.start()
            started.append(loc)
            for j, (px, py) in enumerate(_other_chips(x, y)):
                cp = pltpu.make_async_remote_copy(
                    src_ref=_shard_of(src, ax, 2 * px + py, size), dst_ref=dst.at[j],
                    send_sem=send_sems.at[3 * t + j], recv_sem=recv_sems.at[3 * t + j],
                    device_id=(px, py, c), device_id_type=MESH)
                cp.start()
                started.append(cp)
        for cp in started:
            cp.wait()

    return pl.pallas_call(
        body, name="scatter_grads",
        in_specs=[ANY] * n, out_specs=[ANY] * n,
        out_shape=[jax.ShapeDtypeStruct((4,) + s, BF) for s in shapes()],
        scratch_shapes=[pltpu.SemaphoreType.DMA((3 * n,)), pltpu.SemaphoreType.DMA((3 * n,)),
                        pltpu.SemaphoreType.DMA((n,))],
        compiler_params=pltpu.CompilerParams(has_side_effects=True),
    )(*halves)


def join_halves(pieces):
    n = len(pieces)

    def body(*refs):
        ins, outs = refs[:n], refs[n:2 * n]
        send_sems, recv_sems, local_sems = refs[2 * n:]
        x, y, c = _place()
        cps = []
        for t, (_, _, _, ax) in enumerate(SHARDED):
            place = _half_of(outs[t], ax, c)
            loc = pltpu.make_async_copy(ins[t], place, local_sems.at[t])
            loc.start()
            cp = pltpu.make_async_remote_copy(
                src_ref=ins[t], dst_ref=place, send_sem=send_sems.at[t], recv_sem=recv_sems.at[t],
                device_id=(x, y, 1 - c), device_id_type=MESH)
            cp.start()
            cps.append((loc, cp))
        for t, (loc, cp) in enumerate(cps):
            loc.wait()
            cp.wait_send()
            theirs = _half_of(outs[t], SHARDED[t][3], 1 - c)
            pltpu.make_async_remote_copy(
                src_ref=ins[t], dst_ref=theirs, send_sem=send_sems.at[t], recv_sem=recv_sems.at[t],
                device_id=(x, y, 1 - c), device_id_type=MESH).wait_recv()

    def shard_shape(r, c, ax):
        return (r // 4, c) if ax == 0 else (r, c // 4)

    return pl.pallas_call(
        body, name="join_halves",
        in_specs=[ANY] * n, out_specs=[ANY] * n,
        out_shape=[jax.ShapeDtypeStruct(shard_shape(r, c, ax), F32) for _, r, c, ax in SHARDED],
        scratch_shapes=[pltpu.SemaphoreType.DMA((n,)), pltpu.SemaphoreType.DMA((n,)),
                        pltpu.SemaphoreType.DMA((n,))],
        compiler_params=pltpu.CompilerParams(has_side_effects=True),
    )(*pieces)


def allreduce_small(pack):
    rows = pack.shape[0]
    hr = rows // 2

    def body(p_ref, o_ref, sib_ref, parts_ref, send_sems, recv_sems):
        x, y, c = _place()
        mine = 2 * x + y
        sibling = (x, y, 1 - c)
        swap = pltpu.make_async_remote_copy(
            src_ref=p_ref, dst_ref=sib_ref, send_sem=send_sems.at[0], recv_sem=recv_sems.at[0],
            device_id=sibling, device_id_type=MESH)
        swap.start()
        swap.wait()
        half = pl.ds(pl.multiple_of(c * hr, 8), hr)
        parts_ref[mine] = p_ref[half, :] + sib_ref[half, :]
        cps = []
        for j, chip in enumerate(_other_chips(x, y)):
            cp = pltpu.make_async_remote_copy(
                src_ref=parts_ref.at[mine], dst_ref=parts_ref.at[mine], send_sem=send_sems.at[1 + j],
                recv_sem=recv_sems.at[1 + j], device_id=(*chip, c), device_id_type=MESH)
            cp.start()
            cps.append(cp)
        for j, (px, py) in enumerate(_other_chips(x, y)):
            cps[j].wait_send()
            theirs = parts_ref.at[2 * px + py]
            pltpu.make_async_remote_copy(
                src_ref=theirs, dst_ref=theirs, send_sem=send_sems.at[1 + j], recv_sem=recv_sems.at[1 + j],
                device_id=(px, py, c), device_id_type=MESH).wait_recv()
        o_ref[half, :] = (parts_ref[0] + parts_ref[1]) + (parts_ref[2] + parts_ref[3])
        back = pltpu.make_async_remote_copy(
            src_ref=o_ref.at[half, :], dst_ref=o_ref.at[half, :], send_sem=send_sems.at[4],
            recv_sem=recv_sems.at[4], device_id=sibling, device_id_type=MESH)
        back.start()
        back.wait_send()
        other = pl.ds(pl.multiple_of((1 - c) * hr, 8), hr)
        pltpu.make_async_remote_copy(
            src_ref=o_ref.at[other, :], dst_ref=o_ref.at[other, :], send_sem=send_sems.at[4],
            recv_sem=recv_sems.at[4], device_id=sibling, device_id_type=MESH).wait_recv()

    return pl.pallas_call(
        body, name="allreduce_small",
        in_specs=[pl.BlockSpec(memory_space=pltpu.VMEM)], out_specs=pl.BlockSpec(memory_space=pltpu.VMEM),
        out_shape=jax.ShapeDtypeStruct((rows, 128), F32),
        scratch_shapes=[pltpu.VMEM((rows, 128), F32), pltpu.VMEM((4, hr, 128), F32),
                        pltpu.SemaphoreType.DMA((5,)), pltpu.SemaphoreType.DMA((5,))],
        compiler_params=pltpu.CompilerParams(has_side_effects=True, vmem_limit_bytes=40 << 20),
    )(pack)


def _ssm_discretize(lam_re, lam_im, log_dt, b_re, b_im):
    dt = jnp.exp(log_dt)[:, None]
    mag = jnp.exp(lam_re * dt)
    ar, ai = mag * jnp.cos(lam_im * dt), mag * jnp.sin(lam_im * dt)
    den = lam_re * lam_re + lam_im * lam_im
    nr, ni = ar - 1.0, ai
    fr, fi = (nr * lam_re + ni * lam_im) / den, (ni * lam_re - nr * lam_im) / den
    bbr = fr[..., None] * b_re - fi[..., None] * b_im
    bbi = fr[..., None] * b_im + fi[..., None] * b_re
    return ar, ai, bbr, bbi


def _cmul(a, b):
    return a[0] * b[0] - a[1] * b[1], a[0] * b[1] + a[1] * b[0]


def _scan_powers(ar, ai):
    a = (ar.reshape(1, SW), ai.reshape(1, SW))
    pows = [a]
    for _ in range(7):
        pows.append(_cmul(pows[-1], a))
    fr = jnp.concatenate([p[0] for p in pows], axis=0)
    fi = jnp.concatenate([p[1] for p in pows], axis=0)
    head = jnp.concatenate([pows[0][0], pows[0][1], pows[1][0], pows[1][1], pows[3][0], pows[3][1],
                            jnp.zeros((2, SW), F32)], axis=0)
    return jnp.concatenate([head, fr, fi, fr[::-1], -fi[::-1]], axis=0)


def _pack_b(bb):
    t = bb.reshape(NLC, 8, NSTATE, GCH).transpose(0, 1, 3, 2)
    return jnp.einsum("jgcp,gh->jgchp", t, jnp.eye(8, dtype=bb.dtype)).reshape(NLC, 128, LC)


def _unpack_b(db):
    t = jnp.einsum("jgchp,gh->jgcp", db.reshape(NLC, 8, GCH, 8, NSTATE), jnp.eye(8, dtype=db.dtype))
    return t.transpose(0, 1, 3, 2).reshape(GROUPS, NSTATE, GCH)


def _pack_c(cc):
    t = cc.reshape(NLC, 8, GCH, NSTATE).transpose(0, 1, 3, 2)
    return jnp.einsum("jgpc,gh->jgphc", t, jnp.eye(8, dtype=cc.dtype)).reshape(NLC, LC, 128)


def _unpack_c(dc):
    t = jnp.einsum("jgphc,gh->jgpc", dc.reshape(NLC, 8, NSTATE, 8, GCH), jnp.eye(8, dtype=dc.dtype))
    return t.transpose(0, 1, 3, 2).reshape(GROUPS, GCH, NSTATE)


SMALL = ("norm_mix", "q_norm", "k_norm", "attn_sinks", "lam_re", "lam_im", "log_dt", "ssm_b_re", "ssm_b_im",
         "ssm_c_re", "ssm_c_im", "ssm_d", "attn_branch_norm", "ssm_branch_norm", "norm_ffn")


def _pack_small(arrs, rows=None):
    flat = jnp.concatenate([a.reshape(-1).astype(F32) for a in arrs])
    n = flat.shape[0]
    if rows is None:
        rows = -(-n // (128 * 256)) * 256
    return jnp.pad(flat, (0, rows * 128 - n)).reshape(rows, 128)


def _unpack_small(pack, like):
    flat = pack.reshape(-1)
    out, off = [], 0
    for a in like:
        out.append(flat[off:off + a.size].reshape(a.shape))
        off += a.size
    return out


def local_step(x, tgt, meta_full, p, w):
    seq = x.shape[0]
    hp = jnp.concatenate([jnp.zeros((META0, D), F32), meta_full, x], axis=0)
    qn_t = jnp.tile(p["q_norm"].reshape(1, HEAD), (1, NQ))
    kn_t = jnp.tile(p["k_norm"].reshape(1, HEAD), (1, NKV))
    seg = jnp.arange(256) // HEAD
    bd = (seg[:, None] == seg[None, :]).astype(BF)
    nm = p["norm_mix"].reshape(1, D)
    sinks = p["attn_sinks"].reshape(NQ)
    dsk = p["ssm_d"].reshape(1, D)
    abn, sbn, gf = (p[k].reshape(1, D) for k in ("attn_branch_norm", "ssm_branch_norm", "norm_ffn"))

    disc_in = (p["lam_re"], p["lam_im"], p["log_dt"], p["ssm_b_re"], p["ssm_b_im"])
    (ar, ai, bbr, bbi), disc_vjp = jax.vjp(_ssm_discretize, *disc_in)
    pw = _scan_powers(ar, ai)
    brp, bip = _pack_b(bbr).astype(BF), _pack_b(bbi).astype(BF)
    crp, cip = _pack_c(p["ssm_c_re"]).astype(BF), _pack_c(p["ssm_c_im"]).astype(BF)

    xn, qr, kr, qn, kn, v, u, gg = in_proj_fwd(hp, w["w_in"], nm, qn_t, kn_t, bd)
    attn = attention_fwd(sinks, qn, kn, v)
    y, sr, si = ssm_fwd(u, pw, brp, bip, crp, cip, dsk)
    zb, zzb, mgb, h1 = mid_fwd(y, attn, gg, hp, w["w_glu"], w["w_out"], abn, sbn)
    hnb, dgub, actb, dh2b, dh1, dh1b, dgf, loss = ffn_fwd_bwd(h1, tgt, w["w_ffn_in"], w["w_ffn_out"], gf)
    dy, dattn, dggb, dzzb, dabn, dsbn = mid_bwd(dh1b, y, attn, gg, zzb, w["w_glu"], w["w_out"], abn, sbn)
    dub, da, dbr, dbi, dcr, dci, dd = ssm_bwd(dy, u, sr, si, pw, brp, bip, crp, cip, dsk)
    dqn, dkn, dv, dsink = attention_bwd(sinks, qn, kn, v, dattn)
    dproj, dh0, dnm, dqw, dkw = in_proj_bwd(dqn, dkn, dv, dub, dggb, qr, kr, hp, dh1, w["w_in"], nm, qn_t, kn_t, bd)

    big = [weight_grad("grad_w_in", xn, dproj, 1152), weight_grad("grad_w_glu", zb, dzzb, 1024),
           weight_grad("grad_w_out", mgb, dh1b, 1024), weight_grad("grad_w_ffn_in", hnb, dgub, 1408),
           weight_grad("grad_w_ffn_out", actb, dh2b, 512)]

    dar = jnp.sum(da[0:8], axis=0).reshape(GROUPS, NSTATE)
    dai = jnp.sum(da[8:16], axis=0).reshape(GROUPS, NSTATE)
    dlr, dli, dldt, dbre, dbim = disc_vjp((dar, dai, _unpack_b(dbr), _unpack_b(dbi)))
    small = {
        "norm_mix": dnm, "q_norm": dqw.reshape(NQ, HEAD).sum(0), "k_norm": dkw.reshape(NKV, HEAD).sum(0),
        "attn_sinks": dsink[0, 0:NQ], "lam_re": dlr, "lam_im": dli, "log_dt": dldt,
        "ssm_b_re": dbre, "ssm_b_im": dbim, "ssm_c_re": _unpack_c(dcr), "ssm_c_im": _unpack_c(dci),
        "ssm_d": dd, "attn_branch_norm": dabn, "ssm_branch_norm": dsbn, "norm_ffn": dgf,
    }
    return loss[0, 0], dh0[PAD:PAD + seq], dh0[META0:PAD], small, big


def kernel(x, meta_tokens, norm_mix, w_in, q_norm, k_norm, attn_sinks, lam_re, lam_im, log_dt, ssm_b_re, ssm_b_im, ssm_c_re, ssm_c_im, ssm_d, w_glu, attn_branch_norm, ssm_branch_norm, w_out, norm_ffn, w_ffn_in, w_ffn_out, loss_target, m_meta_tokens, m_norm_mix, m_w_in, m_q_norm, m_k_norm, m_attn_sinks, m_lam_re, m_lam_im, m_log_dt, m_ssm_b_re, m_ssm_b_im, m_ssm_c_re, m_ssm_c_im, m_ssm_d, m_w_glu, m_attn_branch_norm, m_ssm_branch_norm, m_w_out, m_norm_ffn, m_w_ffn_in, m_w_ffn_out, v_meta_tokens, v_norm_mix, v_w_in, v_q_norm, v_k_norm, v_attn_sinks, v_lam_re, v_lam_im, v_log_dt, v_ssm_b_re, v_ssm_b_im, v_ssm_c_re, v_ssm_c_im, v_ssm_d, v_w_glu, v_attn_branch_norm, v_ssm_branch_norm, v_w_out, v_norm_ffn, v_w_ffn_in, v_w_ffn_out):
    args = dict(locals())
    names = ["meta_tokens", "norm_mix", "w_in", "q_norm", "k_norm", "attn_sinks", "lam_re", "lam_im", "log_dt",
             "ssm_b_re", "ssm_b_im", "ssm_c_re", "ssm_c_im", "ssm_d", "w_glu", "attn_branch_norm",
             "ssm_branch_norm", "w_out", "norm_ffn", "w_ffn_in", "w_ffn_out"]
    big_names = [s[0] for s in SHARDED]
    xs, ys, cs = _place()
    chip = 2 * xs + ys

    shards = [args[n][0].astype(BF) for n in big_names]
    *full, meta_full = gather_weights(shards, meta_tokens)
    w = dict(zip(big_names, full))

    p = {n: args[n][0] for n in SMALL}
    loss, dx, dmeta, gsmall, gbig = local_step(x[0], loss_target[0], meta_full, p, w)
    loss = lax.psum(loss, ("x", "y", "c"))

    own, got = pair_exchange(gbig)
    halves = [pair_sum("pair_sum_" + n, a, b) for n, a, b in zip(big_names, own, got)]
    parts = scatter_grads(halves)
    pieces = [sum4("sum4_" + n, pt) for n, pt in zip(big_names, parts)]
    gfull = join_halves(pieces)
    small_list = [gsmall[n].reshape(args[n].shape) for n in SMALL] + [dmeta]
    red = allreduce_small(_pack_small(small_list))
    small_red = _unpack_small(red, small_list)
    gmeta = lax.dynamic_slice_in_dim(small_red[-1], chip * (D // 4), D // 4, axis=1)

    out = {}
    for n, g in zip(big_names, gfull):
        d, nm_, nv_ = adamw("adamw_" + n, args[n][0], g, args["m_" + n][0], args["v_" + n][0])
        out[n] = tuple(t[None] for t in (g, d, nm_, nv_))
    d, nm_, nv_ = adamw("adamw_meta", meta_tokens, gmeta, m_meta_tokens, v_meta_tokens)
    out["meta_tokens"] = (gmeta, d, nm_, nv_)
    rows = red.shape[0]
    wp = _pack_small([args[n] for n in SMALL], rows)
    mp = _pack_small([args["m_" + n] for n in SMALL], rows)
    vp = _pack_small([args["v_" + n] for n in SMALL], rows)
    dp, nmp, nvp = adamw("adamw_small", wp, red, mp, vp)
    like = [args[n] for n in SMALL]
    for n, g_, d_, m_, v_ in zip(SMALL, small_red[:-1], _unpack_small(dp, like),
                                 _unpack_small(nmp, like), _unpack_small(nvp, like)):
        out[n] = (g_, d_, m_, v_)

    res = [loss, dx[None]]
    for k in range(4):
        res += [out[n][k] for n in names]
    return tuple(res)
```

```python
import functools
import math

import jax
import jax.numpy as jnp
from jax import lax
from jax.experimental import pallas as pl
from jax.experimental.pallas import tpu as pltpu

F32 = jnp.float32
BF = jnp.bfloat16

D = 1024
N_META = 16
HEAD = 64
NQ = 16
NKV = 4
QW = NQ * HEAD
KVW = NKV * HEAD
WIN = 128
GROUPS = 64
GCH = 16
NSTATE = 64
SW = GROUPS * NSTATE
DFF = 2816
IN_COLS = QW + 2 * KVW + D + 2 * D
PAD = 256
META0 = PAD - N_META
EPS = 1e-6
NEG = -1e30

TM = 256
TS = 128
LC = 512
NLC = SW // LC

LR, B1, B2, AEPS, WD, STEP = 0.001, 0.9, 0.999, 1e-08, 0.01, 10
BC1 = 1.0 - B1 ** STEP
BC2 = 1.0 - B2 ** STEP

MESH = pl.DeviceIdType.MESH
ANY = pl.BlockSpec(memory_space=pl.ANY)


def _cp(sem=None, vmem_mb=48):
    return pltpu.CompilerParams(dimension_semantics=sem, vmem_limit_bytes=vmem_mb << 20)


def _full(shape):
    n = len(shape)
    return pl.BlockSpec(shape, lambda *a: (0,) * n)


def _rows(tm, width):
    return pl.BlockSpec((tm, width), lambda i: (i, 0))


def _dot(a, b):
    return jnp.dot(a, b, preferred_element_type=F32)


def _dot_nt(a, b):
    return lax.dot_general(a, b, (((1,), (1,)), ((), ())), preferred_element_type=F32)


def _dot_tn(a, b):
    return lax.dot_general(a, b, (((0,), (0,)), ((), ())), preferred_element_type=F32)


def _sigmoid(x):
    return 1.0 / (1.0 + jnp.exp(-x))


def _seg_mean(x, bd):
    outs = []
    for j in range(x.shape[1] // 256):
        c = x[:, 256 * j:256 * (j + 1)]
        hi = c.astype(BF)
        lo = (c - hi.astype(F32)).astype(BF)
        outs.append(_dot(hi, bd) + _dot(lo, bd))
    out = outs[0] if len(outs) == 1 else jnp.concatenate(outs, axis=1)
    return out * (1.0 / HEAD)


_GC = math.sqrt(2.0 / math.pi)


def _gelu(x):
    t = jnp.tanh(_GC * (x + 0.044715 * x * x * x))
    return 0.5 * x * (1.0 + t), t


def _gelu_grad(x, t):
    return 0.5 * (1.0 + t) + 0.5 * x * (1.0 - t * t) * _GC * (1.0 + 3.0 * 0.044715 * x * x)


def in_proj_fwd(hp, w_in, nm, qn_t, kn_t, bd):
    lp = hp.shape[0]

    def body(h_ref, w_ref, nm_ref, qn_ref, kn_ref, bd_ref,
             xn_o, qr_o, kr_o, qn_o, kn_o, v_o, u_o, gg_o):
        h = h_ref[...]
        r = lax.rsqrt(jnp.mean(h * h, axis=-1, keepdims=True) + EPS)
        xb = ((h * r) * nm_ref[...]).astype(BF)
        xn_o[...] = xb
        bdv = bd_ref[...]
        q = _dot(xb, w_ref[:, 0:QW])
        qr_o[...] = q
        qn_o[...] = (q * lax.rsqrt(_seg_mean(q * q, bdv) + EPS) * qn_ref[...]).astype(BF)
        k = _dot(xb, w_ref[:, QW:QW + KVW])
        kr_o[...] = k
        kn_o[...] = (k * lax.rsqrt(_seg_mean(k * k, bdv) + EPS) * kn_ref[...]).astype(BF)
        v_o[...] = _dot(xb, w_ref[:, QW + KVW:QW + 2 * KVW]).astype(BF)
        u_o[...] = _dot(xb, w_ref[:, QW + 2 * KVW:QW + 2 * KVW + D])
        gg_o[...] = _dot(xb, w_ref[:, QW + 2 * KVW + D:IN_COLS])

    outs = [(D, BF), (QW, F32), (KVW, F32), (QW, BF), (KVW, BF), (KVW, BF), (D, F32), (2 * D, F32)]
    return pl.pallas_call(
        body, name="in_proj_fwd", grid=(lp // TM,),
        in_specs=[_rows(TM, D), _full((D, IN_COLS)), _full((1, D)), _full((1, QW)), _full((1, KVW)),
                  _full((256, 256))],
        out_specs=[_rows(TM, w) for w, _ in outs],
        out_shape=[jax.ShapeDtypeStruct((lp, w), t) for w, t in outs],
        compiler_params=_cp(("arbitrary",)),
    )(hp, w_in, nm, qn_t, kn_t, bd)


def in_proj_bwd(dqn, dkn, dv, du, dgg, qr, kr, hp, dh1, w_in, nm, qn_t, kn_t, bd):
    lp = hp.shape[0]

    def body(dqn_ref, dkn_ref, dv_ref, du_ref, dgg_ref, qr_ref, kr_ref, h_ref, dh1_ref,
             w_ref, nm_ref, qn_ref, kn_ref, bd_ref,
             dproj_o, dh0_o, dnm_o, dqw_o, dkw_o):
        i = pl.program_id(0)

        @pl.when(i == 0)
        def _():
            dnm_o[...] = jnp.zeros_like(dnm_o)
            dqw_o[...] = jnp.zeros_like(dqw_o)
            dkw_o[...] = jnp.zeros_like(dkw_o)

        bdv = bd_ref[...]

        def norm_bwd(x, dy, g, acc_ref):
            rr = lax.rsqrt(_seg_mean(x * x, bdv) + EPS)
            xh = x * rr
            acc_ref[...] += jnp.sum(dy * xh, axis=0, keepdims=True)
            dxh = dy * g
            return rr * (dxh - xh * _seg_mean(dxh * xh, bdv))

        dq = norm_bwd(qr_ref[...], dqn_ref[...], qn_ref[...], dqw_o)
        dk = norm_bwd(kr_ref[...], dkn_ref[...], kn_ref[...], dkw_o)
        dproj_o[:, 0:QW] = dq.astype(BF)
        dproj_o[:, QW:QW + KVW] = dk.astype(BF)
        dproj_o[:, QW + KVW:QW + 2 * KVW] = dv_ref[...].astype(BF)
        dproj_o[:, QW + 2 * KVW:QW + 2 * KVW + D] = du_ref[...]
        dproj_o[:, QW + 2 * KVW + D:IN_COLS] = dgg_ref[...]
        dxn = jnp.zeros((TM, D), F32)
        for c0 in range(0, IN_COLS, 512):
            dxn += _dot_nt(dproj_o[:, c0:c0 + 512], w_ref[:, c0:c0 + 512])
        h = h_ref[...]
        r = lax.rsqrt(jnp.mean(h * h, axis=-1, keepdims=True) + EPS)
        xh = h * r
        dnm_o[...] += jnp.sum(dxn * xh, axis=0, keepdims=True)
        dxh = dxn * nm_ref[...]
        dh0_o[...] = dh1_ref[...] + r * (dxh - xh * jnp.mean(dxh * xh, axis=-1, keepdims=True))

    return pl.pallas_call(
        body, name="in_proj_bwd", grid=(lp // TM,),
        in_specs=[_rows(TM, QW), _rows(TM, KVW), _rows(TM, KVW), _rows(TM, D), _rows(TM, 2 * D),
                  _rows(TM, QW), _rows(TM, KVW), _rows(TM, D), _rows(TM, D),
                  _full((D, IN_COLS)), _full((1, D)), _full((1, QW)), _full((1, KVW)), _full((256, 256))],
        out_specs=[_rows(TM, IN_COLS), _rows(TM, D), _full((1, D)), _full((1, QW)), _full((1, KVW))],
        out_shape=[jax.ShapeDtypeStruct((lp, IN_COLS), BF), jax.ShapeDtypeStruct((lp, D), F32),
                   jax.ShapeDtypeStruct((1, D), F32), jax.ShapeDtypeStruct((1, QW), F32),
                   jax.ShapeDtypeStruct((1, KVW), F32)],
        compiler_params=_cp(("arbitrary",)),
    )(dqn, dkn, dv, du, dgg, qr, kr, hp, dh1, w_in, nm, qn_t, kn_t, bd)


def _att_masks(b):
    qi = lax.broadcasted_iota(jnp.int32, (4 * WIN, WIN), 0) & (WIN - 1)
    kj = lax.broadcasted_iota(jnp.int32, (4 * WIN, WIN), 1)
    mask_c = (kj <= qi) & (b * WIN + kj >= META0)
    mask_p = (kj > qi) & ((b - 1) * WIN + kj >= PAD)
    mask_m = (kj >= WIN - N_META) & (b * WIN + qi >= PAD)
    return mask_m, mask_p, mask_c


def _dup_half(ref, kp, pos, lo):
    pair = ref[:, 128 * kp:128 * kp + 128].astype(F32)
    rolled = pltpu.roll(pair, 64, axis=1)
    keep = lo if pos == 0 else jnp.logical_not(lo)
    return jnp.where(keep, pair, rolled).astype(BF)


def _stack_heads(ref, kv, lo):
    parts = []
    for pp in range(2):
        pair = ref[:, 256 * kv + 128 * pp:256 * kv + 128 * pp + 128]
        zero = jnp.zeros_like(pair)
        parts.append(jnp.where(lo, pair, zero))
        parts.append(jnp.where(lo, zero, pair))
    return jnp.concatenate(parts, axis=0)


def _unstack_heads(x4, lo):
    a = jnp.where(lo, x4[0:WIN], x4[WIN:2 * WIN])
    b = jnp.where(lo, x4[2 * WIN:3 * WIN], x4[3 * WIN:4 * WIN])
    return a, b


def _sink_col(sink_ref, kv):
    row = lax.broadcasted_iota(jnp.int32, (4 * WIN, 1), 0) >> 7
    col = jnp.full((4 * WIN, 1), sink_ref[4 * kv + 3], F32)
    for r in range(3):
        col = jnp.where(row == r, sink_ref[4 * kv + r], col)
    return col


def _att_probs(q4, kds, masks, sink4):
    scale = HEAD ** -0.5
    ss = [jnp.where(mk, _dot_nt(q4, kd) * scale, NEG) for kd, mk in zip(kds, masks)]
    m = sink4
    for s in ss:
        m = jnp.maximum(m, jnp.max(s, axis=-1, keepdims=True))
    es = [jnp.exp(s - m) for s in ss]
    esink = jnp.exp(sink4 - m)
    den = esink
    for e in es:
        den = den + jnp.sum(e, axis=-1, keepdims=True)
    inv = 1.0 / den
    return [e * inv for e in es], esink * inv


def _kv_specs():
    return [pl.BlockSpec((WIN, KVW), lambda b: (1, 0)),
            pl.BlockSpec((WIN, KVW), lambda b: (jnp.maximum(b - 1, 0), 0)),
            pl.BlockSpec((WIN, KVW), lambda b: (b, 0))]


def attention_fwd(sinks, qn, kn, v):
    lp = qn.shape[0]

    def body(sink_ref, q_ref, km_ref, kp_ref, kc_ref, vm_ref, vp_ref, vc_ref, o_ref):
        b = pl.program_id(0)
        masks = _att_masks(b)
        lo = lax.broadcasted_iota(jnp.int32, (WIN, WIN), 1) < HEAD
        for kv in range(NKV):
            kp, pos = kv // 2, kv % 2
            kds = [_dup_half(r, kp, pos, lo) for r in (km_ref, kp_ref, kc_ref)]
            vds = [_dup_half(r, kp, pos, lo) for r in (vm_ref, vp_ref, vc_ref)]
            q4 = _stack_heads(q_ref, kv, lo)
            ps, _ = _att_probs(q4, kds, masks, _sink_col(sink_ref, kv))
            o4 = _dot(ps[0].astype(BF), vds[0]) + _dot(ps[1].astype(BF), vds[1]) + _dot(ps[2].astype(BF), vds[2])
            oa, ob = _unstack_heads(o4, lo)
            o_ref[:, 256 * kv:256 * kv + 128] = oa
            o_ref[:, 256 * kv + 128:256 * kv + 256] = ob

    return pl.pallas_call(
        body, name="attention_fwd", grid=(lp // WIN,),
        in_specs=[pl.BlockSpec(memory_space=pltpu.SMEM), _rows(WIN, QW)] + _kv_specs() + _kv_specs(),
        out_specs=_rows(WIN, QW),
        out_shape=jax.ShapeDtypeStruct((lp, QW), F32),
        compiler_params=_cp(("arbitrary",)),
    )(sinks, qn, kn, kn, kn, v, v, v)


def attention_bwd(sinks, qn, kn, v, do):
    lp = qn.shape[0]
    nb = lp // WIN

    def body(sink_ref, q_ref, km_ref, kp_ref, kc_ref, vm_ref, vp_ref, vc_ref, do_ref,
             dq_ref, dk_hbm, dv_hbm, dsink_ref, dk_acc, dv_acc):
        b = pl.program_id(0)

        @pl.when(b == 0)
        def _():
            dk_acc[...] = jnp.zeros_like(dk_acc)
            dv_acc[...] = jnp.zeros_like(dv_acc)
            dsink_ref[...] = jnp.zeros_like(dsink_ref)

        masks = _att_masks(b)
        lo = lax.broadcasted_iota(jnp.int32, (WIN, WIN), 1) < HEAD
        lane8 = lax.broadcasted_iota(jnp.int32, (8, 128), 1)
        row4 = lax.broadcasted_iota(jnp.int32, (4 * WIN, 1), 0) >> 7
        starts = [WIN, pl.multiple_of(jnp.maximum(b - 1, 0) * WIN, WIN), pl.multiple_of(b * WIN, WIN)]
        scale = HEAD ** -0.5
        for kv in range(NKV):
            kp, pos = kv // 2, kv % 2
            keep = lo if pos == 0 else jnp.logical_not(lo)
            kds = [_dup_half(r, kp, pos, lo) for r in (km_ref, kp_ref, kc_ref)]
            vds = [_dup_half(r, kp, pos, lo) for r in (vm_ref, vp_ref, vc_ref)]
            q4 = _stack_heads(q_ref, kv, lo)
            do4 = _stack_heads(do_ref, kv, lo)
            ps, psink = _att_probs(q4, kds, masks, _sink_col(sink_ref, kv))
            dps = [_dot_nt(do4, vd) for vd in vds]
            delta = jnp.sum(ps[0] * dps[0] + ps[1] * dps[1] + ps[2] * dps[2], axis=-1, keepdims=True)
            dss = [(p * (dp - delta) * scale).astype(BF) for p, dp in zip(ps, dps)]
            dsk = psink * delta
            for r in range(4):
                val = -jnp.sum(jnp.where(row4 == r, dsk, 0.0), axis=0, keepdims=True)
                dsink_ref[...] += jnp.where(lane8 == 4 * kv + r, val, 0.0)
            dq4 = _dot(dss[0], kds[0]) + _dot(dss[1], kds[1]) + _dot(dss[2], kds[2])
            dqa, dqb = _unstack_heads(dq4, lo)
            dq_ref[:, 256 * kv:256 * kv + 128] = dqa
            dq_ref[:, 256 * kv + 128:256 * kv + 256] = dqb
            for x in range(3):
                dkd = _dot_tn(dss[x], q4)
                dvd = _dot_tn(ps[x].astype(BF), do4)
                dkd = jnp.where(keep, dkd + pltpu.roll(dkd, 64, axis=1), 0.0)
                dvd = jnp.where(keep, dvd + pltpu.roll(dvd, 64, axis=1), 0.0)
                dk_acc[pl.ds(starts[x], WIN), 128 * kp:128 * kp + 128] += dkd
                dv_acc[pl.ds(starts[x], WIN), 128 * kp:128 * kp + 128] += dvd

        @pl.when(b == nb - 1)
        def _():
            pltpu.sync_copy(dk_acc, dk_hbm)
            pltpu.sync_copy(dv_acc, dv_hbm)

    return pl.pallas_call(
        body, name="attention_bwd", grid=(nb,),
        in_specs=[pl.BlockSpec(memory_space=pltpu.SMEM), _rows(WIN, QW)] + _kv_specs() + _kv_specs()
                 + [_rows(WIN, QW)],
        out_specs=[_rows(WIN, QW), ANY, ANY, _full((8, 128))],
        out_shape=[jax.ShapeDtypeStruct((lp, QW), F32), jax.ShapeDtypeStruct((lp, KVW), F32),
                   jax.ShapeDtypeStruct((lp, KVW), F32), jax.ShapeDtypeStruct((8, 128), F32)],
        scratch_shapes=[pltpu.VMEM((lp, KVW), F32), pltpu.VMEM((lp, KVW), F32)],
        compiler_params=_cp(("arbitrary",)),
    )(sinks, qn, kn, kn, kn, v, v, v, do)


def _scan_tile(xr, xi, a1, a2, a4, rid, reverse):
    for k, (akr, aki) in zip((1, 2, 4), (a1, a2, a4)):
        if reverse:
            sr = pltpu.roll(xr, 8 - k, axis=0)
            si = pltpu.roll(xi, 8 - k, axis=0)
            ok = rid < 8 - k
        else:
            sr = pltpu.roll(xr, k, axis=0)
            si = pltpu.roll(xi, k, axis=0)
            ok = rid >= k
        sr = jnp.where(ok, sr, 0.0)
        si = jnp.where(ok, si, 0.0)
        xr, xi = xr + akr * sr - aki * si, xi + akr * si + aki * sr
    return xr, xi


def ssm_fwd(u, pw, br, bi, cr, ci, dsk):
    lp = u.shape[0]

    def body(u_ref, pw_ref, br_ref, bi_ref, cr_ref, ci_ref, d_ref, y_ref, sr_ref, si_ref, car_r, car_i):
        i = pl.program_id(0)

        @pl.when(i == 0)
        def _():
            car_r[...] = jnp.zeros_like(car_r)
            car_i[...] = jnp.zeros_like(car_i)

        u = u_ref[...]
        ub = u.astype(BF)
        for j in range(NLC):
            uj = ub[:, 128 * j:128 * j + 128]
            sr_ref[:, LC * j:LC * j + LC] = _dot(uj, br_ref[j])
            si_ref[:, LC * j:LC * j + LC] = _dot(uj, bi_ref[j])
        rid = lax.broadcasted_iota(jnp.int32, (8, LC), 0)
        for j in range(NLC):
            ls = slice(LC * j, LC * j + LC)
            a1 = (pw_ref[0:1, ls], pw_ref[1:2, ls])
            a2 = (pw_ref[2:3, ls], pw_ref[3:4, ls])
            a4 = (pw_ref[4:5, ls], pw_ref[5:6, ls])
            pr, pi = pw_ref[8:16, ls], pw_ref[16:24, ls]

            def group(g, carry):
                cr_, ci_ = carry
                rows = pl.ds(pl.multiple_of(g * 8, 8), 8)
                xr, xi = _scan_tile(sr_ref[rows, ls], si_ref[rows, ls], a1, a2, a4, rid, False)
                xr, xi = xr + pr * cr_ - pi * ci_, xi + pr * ci_ + pi * cr_
                sr_ref[rows, ls] = xr
                si_ref[rows, ls] = xi
                return xr[7:8, :], xi[7:8, :]

            cr_, ci_ = lax.fori_loop(0, TS // 8, group, (car_r[:, ls], car_i[:, ls]))
            car_r[:, ls] = cr_
            car_i[:, ls] = ci_
        for j in range(NLC):
            ls = slice(LC * j, LC * j + LC)
            y_ref[:, 128 * j:128 * j + 128] = (_dot(sr_ref[:, ls].astype(BF), cr_ref[j])
                                               - _dot(si_ref[:, ls].astype(BF), ci_ref[j]))
        y_ref[...] += d_ref[...] * u

    return pl.pallas_call(
        body, name="ssm_fwd", grid=(lp // TS,),
        in_specs=[_rows(TS, D), _full((40, SW)), _full((NLC, 128, LC)), _full((NLC, 128, LC)),
                  _full((NLC, LC, 128)), _full((NLC, LC, 128)), _full((1, D))],
        out_specs=[_rows(TS, D), _rows(TS, SW), _rows(TS, SW)],
        out_shape=[jax.ShapeDtypeStruct((lp, D), F32), jax.ShapeDtypeStruct((lp, SW), F32),
                   jax.ShapeDtypeStruct((lp, SW), F32)],
        scratch_shapes=[pltpu.VMEM((1, SW), F32), pltpu.VMEM((1, SW), F32)],
        compiler_params=_cp(("arbitrary",)),
    )(u, pw, br, bi, cr, ci, dsk)


def ssm_bwd(dy, u, sr, si, pw, br, bi, cr, ci, dsk):
    lp = u.shape[0]
    nt = lp // TS

    def rev(i):
        return (nt - 1 - i, 0)

    def prev8(i):
        return (jnp.maximum((nt - 1 - i) * (TS // 8) - 1, 0), 0)

    def body(dy_ref, u_ref, sr_ref, si_ref, pr8_ref, pi8_ref, pw_ref, br_ref, bi_ref, cr_ref, ci_ref, d_ref,
             du_ref, da_ref, dbr_ref, dbi_ref, dcr_ref, dci_ref, dd_ref, gr_s, gi_s, car_r, car_i):
        i = pl.program_id(0)

        @pl.when(i == 0)
        def _():
            car_r[...] = jnp.zeros_like(car_r)
            car_i[...] = jnp.zeros_like(car_i)
            for ref in (da_ref, dbr_ref, dbi_ref, dcr_ref, dci_ref, dd_ref):
                ref[...] = jnp.zeros_like(ref)

        first = (i == nt - 1).astype(F32)
        dy = dy_ref[...]
        u = u_ref[...]
        dyb = dy.astype(BF)
        ub = u.astype(BF)
        dd_ref[...] += jnp.sum(dy * u, axis=0, keepdims=True)
        for j in range(NLC):
            ls = slice(LC * j, LC * j + LC)
            dyj = dyb[:, 128 * j:128 * j + 128]
            gr_s[:, ls] = _dot_nt(dyj, cr_ref[j])
            gi_s[:, ls] = -_dot_nt(dyj, ci_ref[j])
            dcr_ref[j] += _dot_tn(sr_ref[:, ls].astype(BF), dyj)
            dci_ref[j] -= _dot_tn(si_ref[:, ls].astype(BF), dyj)
        rid = lax.broadcasted_iota(jnp.int32, (8, LC), 0)
        for j in range(NLC):
            ls = slice(LC * j, LC * j + LC)
            a1 = (pw_ref[0:1, ls], -pw_ref[1:2, ls])
            a2 = (pw_ref[2:3, ls], -pw_ref[3:4, ls])
            a4 = (pw_ref[4:5, ls], -pw_ref[5:6, ls])
            pr, pi = pw_ref[24:32, ls], pw_ref[32:40, ls]

            def one_group(rows, spr, spi, carry):
                cr_, ci_ = carry
                gr, gi = _scan_tile(gr_s[rows, ls], gi_s[rows, ls], a1, a2, a4, rid, True)
                gr, gi = gr + pr * cr_ - pi * ci_, gi + pr * ci_ + pi * cr_
                gr_s[rows, ls] = gr
                gi_s[rows, ls] = gi
                s_r = jnp.where(rid >= 1, pltpu.roll(sr_ref[rows, ls], 1, axis=0), spr[7:8, :])
                s_i = jnp.where(rid >= 1, pltpu.roll(si_ref[rows, ls], 1, axis=0), spi[7:8, :])
                da_ref[0:8, ls] += gr * s_r + gi * s_i
                da_ref[8:16, ls] += gi * s_r - gr * s_i
                return gr[0:1, :], gi[0:1, :]

            def group(k, carry):
                g = TS // 8 - 1 - k
                rows = pl.ds(pl.multiple_of(g * 8, 8), 8)
                before = pl.ds(pl.multiple_of(g * 8 - 8, 8), 8)
                return one_group(rows, sr_ref[before, ls], si_ref[before, ls], carry)

            carry = lax.fori_loop(0, TS // 8 - 1, group, (car_r[:, ls], car_i[:, ls]))
            keep = 1.0 - first
            cr_, ci_ = one_group(pl.ds(0, 8), pr8_ref[:, ls] * keep, pi8_ref[:, ls] * keep, carry)
            car_r[:, ls] = cr_
            car_i[:, ls] = ci_
        for j in range(NLC):
            ls = slice(LC * j, LC * j + LC)
            uj = ub[:, 128 * j:128 * j + 128]
            grb = gr_s[:, ls].astype(BF)
            gib = gi_s[:, ls].astype(BF)
            du_ref[:, 128 * j:128 * j + 128] = (_dot_nt(grb, br_ref[j]) + _dot_nt(gib, bi_ref[j])
                                                + d_ref[:, 128 * j:128 * j + 128] * dy[:, 128 * j:128 * j + 128]
                                                ).astype(BF)
            dbr_ref[j] += _dot_tn(uj, grb)
            dbi_ref[j] += _dot_tn(uj, gib)

    return pl.pallas_call(
        body, name="ssm_bwd", grid=(nt,),
        in_specs=[pl.BlockSpec((TS, D), rev), pl.BlockSpec((TS, D), rev), pl.BlockSpec((TS, SW), rev),
                  pl.BlockSpec((TS, SW), rev), pl.BlockSpec((8, SW), prev8), pl.BlockSpec((8, SW), prev8),
                  _full((40, SW)), _full((NLC, 128, LC)), _full((NLC, 128, LC)),
                  _full((NLC, LC, 128)), _full((NLC, LC, 128)), _full((1, D))],
        out_specs=[pl.BlockSpec((TS, D), rev), _full((16, SW)), _full((NLC, 128, LC)), _full((NLC, 128, LC)),
                   _full((NLC, LC, 128)), _full((NLC, LC, 128)), _full((1, D))],
        out_shape=[jax.ShapeDtypeStruct((lp, D), BF), jax.ShapeDtypeStruct((16, SW), F32),
                   jax.ShapeDtypeStruct((NLC, 128, LC), F32), jax.ShapeDtypeStruct((NLC, 128, LC), F32),
                   jax.ShapeDtypeStruct((NLC, LC, 128), F32), jax.ShapeDtypeStruct((NLC, LC, 128), F32),
                   jax.ShapeDtypeStruct((1, D), F32)],
        scratch_shapes=[pltpu.VMEM((TS, SW), F32), pltpu.VMEM((TS, SW), F32),
                        pltpu.VMEM((1, SW), F32), pltpu.VMEM((1, SW), F32)],
        compiler_params=_cp(("arbitrary",)),
    )(dy, u, sr, si, sr, si, pw, br, bi, cr, ci, dsk)


def _merge_parts(attn, zz, gg, abn, sbn):
    za, zb = zz[:, 0:D], zz[:, D:2 * D]
    sgz = _sigmoid(zb)
    ssm = za * sgz
    ra = lax.rsqrt(jnp.mean(attn * attn, axis=-1, keepdims=True) + EPS)
    rs = lax.rsqrt(jnp.mean(ssm * ssm, axis=-1, keepdims=True) + EPS)
    ah, sh = attn * ra, ssm * rs
    sga, sgs = _sigmoid(gg[:, 0:D]), _sigmoid(gg[:, D:2 * D])
    return za, sgz, ra, rs, ah, sh, sga, sgs, ah * abn, sh * sbn


def mid_fwd(y, attn, gg, hp, w_glu, w_out, abn, sbn):
    lp = y.shape[0]

    def body(y_ref, a_ref, gg_ref, h_ref, wg_ref, wo_ref, abn_ref, sbn_ref, z_o, zz_o, mg_o, h1_o):
        z, _ = _gelu(y_ref[...])
        zb = z.astype(BF)
        z_o[...] = zb
        zz = _dot(zb, wg_ref[...])
        zz_o[...] = zz.astype(BF)
        _, _, _, _, _, _, sga, sgs, an, sn = _merge_parts(a_ref[...], zz, gg_ref[...], abn_ref[...], sbn_ref[...])
        mb = (sga * an + sgs * sn).astype(BF)
        mg_o[...] = mb
        h1_o[...] = h_ref[...] + _dot(mb, wo_ref[...])

    return pl.pallas_call(
        body, name="mid_fwd", grid=(lp // TM,),
        in_specs=[_rows(TM, D), _rows(TM, D), _rows(TM, 2 * D), _rows(TM, D), _full((D, 2 * D)), _full((D, D)),
                  _full((1, D)), _full((1, D))],
        out_specs=[_rows(TM, D), _rows(TM, 2 * D), _rows(TM, D), _rows(TM, D)],
        out_shape=[jax.ShapeDtypeStruct((lp, D), BF), jax.ShapeDtypeStruct((lp, 2 * D), BF),
                   jax.ShapeDtypeStruct((lp, D), BF), jax.ShapeDtypeStruct((lp, D), F32)],
        compiler_params=_cp(("arbitrary",)),
    )(y, attn, gg, hp, w_glu, w_out, abn, sbn)


def mid_bwd(dh1b, y, attn, gg, zzb, w_glu, w_out, abn, sbn):
    lp = y.shape[0]

    def body(dh_ref, y_ref, a_ref, gg_ref, zz_ref, wg_ref, wo_ref, abn_ref, sbn_ref,
             dy_o, dattn_o, dgg_o, dzz_o, dabn_o, dsbn_o):
        i = pl.program_id(0)

        @pl.when(i == 0)
        def _():
            dabn_o[...] = jnp.zeros_like(dabn_o)
            dsbn_o[...] = jnp.zeros_like(dsbn_o)

        dm = _dot_nt(dh_ref[...], wo_ref[...])
        abn, sbn = abn_ref[...], sbn_ref[...]
        za, sgz, ra, rs, ah, sh, sga, sgs, an, sn = _merge_parts(
            a_ref[...], zz_ref[...].astype(F32), gg_ref[...], abn, sbn)
        dgg_o[:, 0:D] = (dm * an * sga * (1.0 - sga)).astype(BF)
        dgg_o[:, D:2 * D] = (dm * sn * sgs * (1.0 - sgs)).astype(BF)
        da = dm * sga
        dabn_o[...] += jnp.sum(da * ah, axis=0, keepdims=True)
        dah = da * abn
        dattn_o[...] = (ra * (dah - ah * jnp.mean(dah * ah, axis=-1, keepdims=True))).astype(BF)
        ds = dm * sgs
        dsbn_o[...] += jnp.sum(ds * sh, axis=0, keepdims=True)
        dsh = ds * sbn
        dssm = rs * (dsh - sh * jnp.mean(dsh * sh, axis=-1, keepdims=True))
        dzz_o[:, 0:D] = (dssm * sgz).astype(BF)
        dzz_o[:, D:2 * D] = (dssm * za * sgz * (1.0 - sgz)).astype(BF)
        dz = _dot_nt(dzz_o[...], wg_ref[...])
        yv = y_ref[...]
        _, t = _gelu(yv)
        dy_o[...] = dz * _gelu_grad(yv, t)

    return pl.pallas_call(
        body, name="mid_bwd", grid=(lp // TM,),
        in_specs=[_rows(TM, D), _rows(TM, D), _rows(TM, D), _rows(TM, 2 * D), _rows(TM, 2 * D),
                  _full((D, 2 * D)), _full((D, D)), _full((1, D)), _full((1, D))],
        out_specs=[_rows(TM, D), _rows(TM, D), _rows(TM, 2 * D), _rows(TM, 2 * D), _full((1, D)), _full((1, D))],
        out_shape=[jax.ShapeDtypeStruct((lp, D), F32), jax.ShapeDtypeStruct((lp, D), BF),
                   jax.ShapeDtypeStruct((lp, 2 * D), BF), jax.ShapeDtypeStruct((lp, 2 * D), BF),
                   jax.ShapeDtypeStruct((1, D), F32), jax.ShapeDtypeStruct((1, D), F32)],
        compiler_params=_cp(("arbitrary",)),
    )(dh1b, y, attn, gg, zzb, w_glu, w_out, abn, sbn)


def ffn_fwd_bwd(h1, tgt, w_fi, w_fo, gf):
    lp = h1.shape[0]
    tf = 128

    def body(h_ref, t_ref, wi_ref, wo_ref, gf_ref, hn_o, dgu_o, act_o, dh2_o, dh1_o, dh1b_o, dgf_o, loss_o):
        i = pl.program_id(0)

        @pl.when(i == 0)
        def _():
            dgf_o[...] = jnp.zeros_like(dgf_o)
            loss_o[...] = jnp.zeros_like(loss_o)

        h = h_ref[...]
        r = lax.rsqrt(jnp.mean(h * h, axis=-1, keepdims=True) + EPS)
        xh = h * r
        hb = (xh * gf_ref[...]).astype(BF)
        hn_o[...] = hb
        gate = _dot(hb, wi_ref[:, 0:DFF])
        up = _dot(hb, wi_ref[:, DFF:2 * DFF])
        sg = _sigmoid(gate)
        sl = gate * sg
        ab = (sl * up).astype(BF)
        act_o[...] = ab
        h2 = h + _dot(ab, wo_ref[...])
        real = (i >= PAD // tf).astype(F32)
        e = (h2 - t_ref[...]) * real
        loss_o[...] += 0.5 * jnp.sum(jnp.sum(e * e, axis=-1, keepdims=True), axis=0, keepdims=True) * (1.0 / D)
        dh2 = e * (1.0 / D)
        db = dh2.astype(BF)
        dh2_o[...] = db
        dact = _dot_nt(db, wo_ref[...])
        dgate = (dact * up * (sg * (1.0 + gate * (1.0 - sg)))).astype(BF)
        dup = (dact * sl).astype(BF)
        dgu_o[:, 0:DFF] = dgate
        dgu_o[:, DFF:2 * DFF] = dup
        dhn = _dot_nt(dgate, wi_ref[:, 0:DFF]) + _dot_nt(dup, wi_ref[:, DFF:2 * DFF])
        dgf_o[...] += jnp.sum(dhn * xh, axis=0, keepdims=True)
        dxh = dhn * gf_ref[...]
        dh1 = dh2 + r * (dxh - xh * jnp.mean(dxh * xh, axis=-1, keepdims=True))
        dh1_o[...] = dh1
        dh1b_o[...] = dh1.astype(BF)

    return pl.pallas_call(
        body, name="ffn_fwd_bwd", grid=(lp // tf,),
        in_specs=[_rows(tf, D), pl.BlockSpec((tf, D), lambda i: (jnp.maximum(i - PAD // tf, 0), 0)),
                  _full((D, 2 * DFF)), _full((DFF, D)), _full((1, D))],
        out_specs=[_rows(tf, D), _rows(tf, 2 * DFF), _rows(tf, DFF), _rows(tf, D), _rows(tf, D), _rows(tf, D),
                   _full((1, D)), _full((1, 1))],
        out_shape=[jax.ShapeDtypeStruct((lp, D), BF), jax.ShapeDtypeStruct((lp, 2 * DFF), BF),
                   jax.ShapeDtypeStruct((lp, DFF), BF), jax.ShapeDtypeStruct((lp, D), BF),
                   jax.ShapeDtypeStruct((lp, D), F32), jax.ShapeDtypeStruct((lp, D), BF),
                   jax.ShapeDtypeStruct((1, D), F32), jax.ShapeDtypeStruct((1, 1), F32)],
        compiler_params=_cp(("arbitrary",), vmem_mb=56),
    )(h1, tgt, w_fi, w_fo, gf)


def weight_grad(name, a, b, tn):
    lp, k = a.shape
    n = b.shape[1]
    tr = next(t for t in (768, 512, 384, 256, 128) if lp % t == 0)

    def body(a_ref, b_ref, o_ref):
        @pl.when(pl.program_id(1) == 0)
        def _():
            o_ref[...] = jnp.zeros_like(o_ref)

        o_ref[...] += _dot_tn(a_ref[...], b_ref[...])

    return pl.pallas_call(
        body, name=name, grid=(n // tn, lp // tr),
        in_specs=[pl.BlockSpec((tr, k), lambda j, m: (m, 0)), pl.BlockSpec((tr, tn), lambda j, m: (m, j))],
        out_specs=pl.BlockSpec((k, tn), lambda j, m: (0, j)),
        out_shape=jax.ShapeDtypeStruct((k, n), F32),
        compiler_params=_cp(("arbitrary", "arbitrary")),
    )(a, b)


def _adamw_math(w, g, m, v):
    m = B1 * m + (1.0 - B1) * g
    v = B2 * v + (1.0 - B2) * (g * g)
    delta = -LR * ((m / BC1) / (jnp.sqrt(v / BC2) + AEPS) + WD * w)
    return delta, m, v


def _row_tile(r):
    return next((t for t in (256, 128, 64, 32, 16, 8) if r % t == 0), r)


def adamw(name, w, g, m, v):
    r, c = w.shape
    tr = _row_tile(r)

    def body(w_ref, g_ref, m_ref, v_ref, d_o, m_o, v_o):
        d_o[...], m_o[...], v_o[...] = _adamw_math(w_ref[...], g_ref[...], m_ref[...], v_ref[...])

    spec = pl.BlockSpec((tr, c), lambda i: (i, 0))
    return pl.pallas_call(
        body, name=name, grid=(r // tr,), in_specs=[spec] * 4, out_specs=[spec] * 3,
        out_shape=[jax.ShapeDtypeStruct((r, c), F32)] * 3,
        compiler_params=_cp(("arbitrary",)),
    )(w, g, m, v)


def pair_sum(name, place, full, got, ax):
    r, c = got.shape
    tr = _row_tile(r)

    def body(s_ref, a_ref, b_ref, o_ref):
        o_ref[...] = (a_ref[...] + b_ref[...]).astype(BF)

    if ax == 1:
        mine = pl.BlockSpec((tr, c), lambda i, s: (s[1] * (r // tr) + i, 0))
    else:
        mine = pl.BlockSpec((tr, c), lambda i, s: (i, s[1]))
    spec = pl.BlockSpec((tr, c), lambda i, s: (i, 0))
    return pl.pallas_call(
        body, name=name,
        grid_spec=pltpu.PrefetchScalarGridSpec(num_scalar_prefetch=1, grid=(r // tr,), in_specs=[mine, spec],
                                               out_specs=spec),
        out_shape=jax.ShapeDtypeStruct((r, c), BF),
        compiler_params=_cp(("arbitrary",)),
    )(place, full, got)


def sum4(name, place, parts, half, ax):
    _, r, c = parts.shape
    tr = _row_tile(r)

    def body(s_ref, p_ref, h_ref, o_ref):
        o_ref[...] = ((p_ref[0].astype(F32) + p_ref[1].astype(F32))
                      + (p_ref[2].astype(F32) + h_ref[...].astype(F32)))

    if ax == 1:
        own = pl.BlockSpec((tr, c), lambda i, s: (i, s[0]))
        out = pl.BlockSpec((tr, c), lambda i, s: (s[1] * (r // tr) + i, 0))
        shape = (2 * r, c)
    else:
        own = pl.BlockSpec((tr, c), lambda i, s: (s[0] * (r // tr) + i, 0))
        out = pl.BlockSpec((tr, c), lambda i, s: (i, s[1]))
        shape = (r, 2 * c)
    return pl.pallas_call(
        body, name=name,
        grid_spec=pltpu.PrefetchScalarGridSpec(
            num_scalar_prefetch=1, grid=(r // tr,),
            in_specs=[pl.BlockSpec((3, tr, c), lambda i, s: (0, i, 0)), own], out_specs=out),
        out_shape=jax.ShapeDtypeStruct(shape, F32),
        compiler_params=_cp(("arbitrary",)),
    )(place, parts, half)


def _place():
    x, y, c = lax.axis_index("x"), lax.axis_index("y"), lax.axis_index("c")
    return x, y, c


def _other_chips(x, y):
    return [(1 - x, y), (x, 1 - y), (1 - x, 1 - y)]


SHARDED = (("w_in", D, IN_COLS, 1), ("w_glu", D, 2 * D, 1), ("w_out", D, D, 0),
           ("w_ffn_in", D, 2 * DFF, 1), ("w_ffn_out", DFF, D, 0))


def _shard_of(ref, axis, k, size):
    if axis == 0:
        return ref.at[pl.ds(k * size, size), :]
    return ref.at[:, pl.ds(k * size, size)]


def gather_weights(shards, meta):
    items = [(s, r, c, ax) for s, (_, r, c, ax) in zip(shards, SHARDED)] + [(meta, N_META, D, 1)]
    n = len(items)

    def body(*refs):
        ins, outs = refs[:n], refs[n:2 * n]
        ici_send, ici_recv, d2d_send, d2d_recv, local_sems = refs[2 * n:]
        x, y, c = _place()
        mine = 2 * x + y
        chips = _other_chips(x, y)

        def piece(t, k, h):
            _, r, cc, ax = items[t]
            if ax == 0:
                return outs[t].at[pl.ds(k * (r // 4) + h * (r // 8), r // 8), :]
            return outs[t].at[pl.ds(h * (r // 2), r // 2), pl.ds(k * (cc // 4), cc // 4)]

        def src_half(t, h):
            rs = ins[t].shape[0]
            return ins[t].at[pl.ds(h * (rs // 2), rs // 2), :]

        local, first, passed = [], [], []
        for t in range(n):
            _, r, cc, ax = items[t]
            size = (r if ax == 0 else cc) // 4
            loc = pltpu.make_async_remote_copy(
                src_ref=ins[t], dst_ref=_shard_of(outs[t], ax, mine, size), send_sem=local_sems.at[t],
                recv_sem=local_sems.at[n + t], device_id=(x, y, 1 - c), device_id_type=MESH)
            loc.start()
            local.append(loc)
            for j, chip in enumerate(chips):
                cp = pltpu.make_async_remote_copy(
                    src_ref=src_half(t, c), dst_ref=piece(t, mine, c),
                    send_sem=ici_send.at[3 * t + j], recv_sem=ici_recv.at[3 * t + j],
                    device_id=(*chip, c), device_id_type=MESH)
                cp.start()
                first.append(cp)
        for t in range(n):
            for j, (px, py) in enumerate(chips):
                got = piece(t, 2 * px + py, c)
                pltpu.make_async_remote_copy(
                    src_ref=src_half(t, c), dst_ref=got, send_sem=ici_send.at[3 * t + j],
                    recv_sem=ici_recv.at[3 * t + j], device_id=(px, py, c), device_id_type=MESH).wait_recv()
                cp = pltpu.make_async_remote_copy(
                    src_ref=got, dst_ref=got, send_sem=d2d_send.at[3 * t + j], recv_sem=d2d_recv.at[3 * t + j],
                    device_id=(x, y, 1 - c), device_id_type=MESH)
                cp.start()
                passed.append(cp)
        for t in range(n):
            for j, (px, py) in enumerate(chips):
                other = piece(t, 2 * px + py, 1 - c)
                pltpu.make_async_remote_copy(
                    src_ref=other, dst_ref=other, send_sem=d2d_send.at[3 * t + j], recv_sem=d2d_recv.at[3 * t + j],
                    device_id=(x, y, 1 - c), device_id_type=MESH).wait_recv()
        for cp in first + passed:
            cp.wait_send()
        for loc in local:
            loc.wait()

    return pl.pallas_call(
        body, name="gather_weights",
        in_specs=[ANY] * n, out_specs=[ANY] * n,
        out_shape=[jax.ShapeDtypeStruct((r, c), s.dtype) for s, r, c, _ in items],
        scratch_shapes=[pltpu.SemaphoreType.DMA((3 * n,)), pltpu.SemaphoreType.DMA((3 * n,)),
                        pltpu.SemaphoreType.DMA((3 * n,)), pltpu.SemaphoreType.DMA((3 * n,)),
                        pltpu.SemaphoreType.DMA((2 * n,))],
        compiler_params=pltpu.CompilerParams(has_side_effects=True),
    )(*shards, meta)


def _half_of(ref, ax, h):
    r, c = ref.shape
    if ax == 1:
        return ref.at[pl.ds(h * (r // 2), r // 2), :]
    return ref.at[:, pl.ds(h * (c // 2), c // 2)]


def _half_shape(r, c, ax):
    return (r // 2, c) if ax == 1 else (r, c // 2)


def pair_exchange(grads):
    n = len(grads)

    def body(*refs):
        ins, got = refs[:n], refs[n:2 * n]
        send_sems, recv_sems = refs[2 * n:]
        x, y, c = _place()
        cps = []
        for t, (_, _, _, ax) in enumerate(SHARDED):
            cp = pltpu.make_async_remote_copy(
                src_ref=_half_of(ins[t], ax, 1 - c), dst_ref=got[t], send_sem=send_sems.at[t],
                recv_sem=recv_sems.at[t], device_id=(x, y, 1 - c), device_id_type=MESH)
            cp.start()
            cps.append(cp)
        for cp in cps:
            cp.wait()

    return pl.pallas_call(
        body, name="pair_exchange",
        in_specs=[ANY] * n, out_specs=[ANY] * n,
        out_shape=[jax.ShapeDtypeStruct(_half_shape(r, c, ax), F32) for _, r, c, ax in SHARDED],
        scratch_shapes=[pltpu.SemaphoreType.DMA((n,)), pltpu.SemaphoreType.DMA((n,))],
        compiler_params=pltpu.CompilerParams(has_side_effects=True),
    )(*grads)


def scatter_grads(halves):
    n = len(halves)

    def shapes():
        out = []
        for _, r, c, ax in SHARDED:
            hr, hc = _half_shape(r, c, ax)
            out.append((hr // 4, hc) if ax == 0 else (hr, hc // 4))
        return out

    def body(*refs):
        ins, outs = refs[:n], refs[n:2 * n]
        send_sems, recv_sems = refs[2 * n:]
        x, y, c = _place()
        started = []
        for t, (src, dst, (_, _, _, ax)) in enumerate(zip(ins, outs, SHARDED)):
            size = src.shape[ax] // 4
            for j, (px, py) in enumerate(_other_chips(x, y)):
                cp = pltpu.make_async_remote_copy(
                    src_ref=_shard_of(src, ax, 2 * px + py, size), dst_ref=dst.at[j],
                    send_sem=send_sems.at[3 * t + j], recv_sem=recv_sems.at[3 * t + j],
                    device_id=(px, py, c), device_id_type=MESH)
                cp.start()
                started.append(cp)
        for cp in started:
            cp.wait()

    return pl.pallas_call(
        body, name="scatter_grads",
        in_specs=[ANY] * n, out_specs=[ANY] * n,
        out_shape=[jax.ShapeDtypeStruct((3,) + s, BF) for s in shapes()],
        scratch_shapes=[pltpu.SemaphoreType.DMA((3 * n,)), pltpu.SemaphoreType.DMA((3 * n,))],
        compiler_params=pltpu.CompilerParams(has_side_effects=True),
    )(*halves)


def join_halves(shards):
    n = len(shards)

    def body(*refs):
        ins, outs = refs[:n], refs[n:2 * n]
        send_sems, recv_sems = refs[2 * n:]
        x, y, c = _place()
        cps = []
        for t, (_, _, _, ax) in enumerate(SHARDED):
            cp = pltpu.make_async_remote_copy(
                src_ref=_half_of(ins[t], ax, c), dst_ref=_half_of(outs[t], ax, c), send_sem=send_sems.at[t],
                recv_sem=recv_sems.at[t], device_id=(x, y, 1 - c), device_id_type=MESH)
            cp.start()
            cps.append(cp)
        for t, cp in enumerate(cps):
            cp.wait_send()
            theirs = _half_of(outs[t], SHARDED[t][3], 1 - c)
            pltpu.make_async_remote_copy(
                src_ref=theirs, dst_ref=theirs, send_sem=send_sems.at[t], recv_sem=recv_sems.at[t],
                device_id=(x, y, 1 - c), device_id_type=MESH).wait_recv()

    return pl.pallas_call(
        body, name="join_halves",
        in_specs=[ANY] * n, out_specs=[ANY] * n,
        out_shape=[jax.ShapeDtypeStruct(s.shape, F32) for s in shards],
        input_output_aliases={t: t for t in range(n)},
        scratch_shapes=[pltpu.SemaphoreType.DMA((n,)), pltpu.SemaphoreType.DMA((n,))],
        compiler_params=pltpu.CompilerParams(has_side_effects=True),
    )(*shards)


def allreduce_small(pack):
    rows = pack.shape[0]
    hr = rows // 2

    def body(p_ref, o_ref, sib_ref, parts_ref, send_sems, recv_sems):
        x, y, c = _place()
        mine = 2 * x + y
        sibling = (x, y, 1 - c)
        swap = pltpu.make_async_remote_copy(
            src_ref=p_ref, dst_ref=sib_ref, send_sem=send_sems.at[0], recv_sem=recv_sems.at[0],
            device_id=sibling, device_id_type=MESH)
        swap.start()
        swap.wait()
        half = pl.ds(pl.multiple_of(c * hr, 8), hr)
        parts_ref[mine] = p_ref[half, :] + sib_ref[half, :]
        cps = []
        for j, chip in enumerate(_other_chips(x, y)):
            cp = pltpu.make_async_remote_copy(
                src_ref=parts_ref.at[mine], dst_ref=parts_ref.at[mine], send_sem=send_sems.at[1 + j],
                recv_sem=recv_sems.at[1 + j], device_id=(*chip, c), device_id_type=MESH)
            cp.start()
            cps.append(cp)
        for j, (px, py) in enumerate(_other_chips(x, y)):
            cps[j].wait_send()
            theirs = parts_ref.at[2 * px + py]
            pltpu.make_async_remote_copy(
                src_ref=theirs, dst_ref=theirs, send_sem=send_sems.at[1 + j], recv_sem=recv_sems.at[1 + j],
                device_id=(px, py, c), device_id_type=MESH).wait_recv()
        o_ref[half, :] = (parts_ref[0] + parts_ref[1]) + (parts_ref[2] + parts_ref[3])
        back = pltpu.make_async_remote_copy(
            src_ref=o_ref.at[half, :], dst_ref=o_ref.at[half, :], send_sem=send_sems.at[4],
            recv_sem=recv_sems.at[4], device_id=sibling, device_id_type=MESH)
        back.start()
        back.wait_send()
        other = pl.ds(pl.multiple_of((1 - c) * hr, 8), hr)
        pltpu.make_async_remote_copy(
            src_ref=o_ref.at[other, :], dst_ref=o_ref.at[other, :], send_sem=send_sems.at[4],
            recv_sem=recv_sems.at[4], device_id=sibling, device_id_type=MESH).wait_recv()

    return pl.pallas_call(
        body, name="allreduce_small",
        in_specs=[pl.BlockSpec(memory_space=pltpu.VMEM)], out_specs=pl.BlockSpec(memory_space=pltpu.VMEM),
        out_shape=jax.ShapeDtypeStruct((rows, 128), F32),
        scratch_shapes=[pltpu.VMEM((rows, 128), F32), pltpu.VMEM((4, hr, 128), F32),
                        pltpu.SemaphoreType.DMA((5,)), pltpu.SemaphoreType.DMA((5,))],
        compiler_params=pltpu.CompilerParams(has_side_effects=True, vmem_limit_bytes=40 << 20),
    )(pack)


def _ssm_discretize(lam_re, lam_im, log_dt, b_re, b_im):
    dt = jnp.exp(log_dt)[:, None]
    mag = jnp.exp(lam_re * dt)
    ar, ai = mag * jnp.cos(lam_im * dt), mag * jnp.sin(lam_im * dt)
    den = lam_re * lam_re + lam_im * lam_im
    nr, ni = ar - 1.0, ai
    fr, fi = (nr * lam_re + ni * lam_im) / den, (ni * lam_re - nr * lam_im) / den
    bbr = fr[..., None] * b_re - fi[..., None] * b_im
    bbi = fr[..., None] * b_im + fi[..., None] * b_re
    return ar, ai, bbr, bbi


def _cmul(a, b):
    return a[0] * b[0] - a[1] * b[1], a[0] * b[1] + a[1] * b[0]


def _scan_powers(ar, ai):
    a = (ar.reshape(1, SW), ai.reshape(1, SW))
    pows = [a]
    for _ in range(7):
        pows.append(_cmul(pows[-1], a))
    fr = jnp.concatenate([p[0] for p in pows], axis=0)
    fi = jnp.concatenate([p[1] for p in pows], axis=0)
    head = jnp.concatenate([pows[0][0], pows[0][1], pows[1][0], pows[1][1], pows[3][0], pows[3][1],
                            jnp.zeros((2, SW), F32)], axis=0)
    return jnp.concatenate([head, fr, fi, fr[::-1], -fi[::-1]], axis=0)


def _pack_b(bb):
    t = bb.reshape(NLC, 8, NSTATE, GCH).transpose(0, 1, 3, 2)
    return jnp.einsum("jgcp,gh->jgchp", t, jnp.eye(8, dtype=bb.dtype)).reshape(NLC, 128, LC)


def _unpack_b(db):
    t = jnp.einsum("jgchp,gh->jgcp", db.reshape(NLC, 8, GCH, 8, NSTATE), jnp.eye(8, dtype=db.dtype))
    return t.transpose(0, 1, 3, 2).reshape(GROUPS, NSTATE, GCH)


def _pack_c(cc):
    t = cc.reshape(NLC, 8, GCH, NSTATE).transpose(0, 1, 3, 2)
    return jnp.einsum("jgpc,gh->jgphc", t, jnp.eye(8, dtype=cc.dtype)).reshape(NLC, LC, 128)


def _unpack_c(dc):
    t = jnp.einsum("jgphc,gh->jgpc", dc.reshape(NLC, 8, NSTATE, 8, GCH), jnp.eye(8, dtype=dc.dtype))
    return t.transpose(0, 1, 3, 2).reshape(GROUPS, GCH, NSTATE)


SMALL = ("norm_mix", "q_norm", "k_norm", "attn_sinks", "lam_re", "lam_im", "log_dt", "ssm_b_re", "ssm_b_im",
         "ssm_c_re", "ssm_c_im", "ssm_d", "attn_branch_norm", "ssm_branch_norm", "norm_ffn")


def _pack_small(arrs, rows=None):
    flat = jnp.concatenate([a.reshape(-1).astype(F32) for a in arrs])
    n = flat.shape[0]
    if rows is None:
        rows = -(-n // (128 * 256)) * 256
    return jnp.pad(flat, (0, rows * 128 - n)).reshape(rows, 128)


def _unpack_small(pack, like):
    flat = pack.reshape(-1)
    out, off = [], 0
    for a in like:
        out.append(flat[off:off + a.size].reshape(a.shape))
        off += a.size
    return out


def local_step(x, tgt, meta_full, p, w):
    seq = x.shape[0]
    hp = jnp.concatenate([jnp.zeros((META0, D), F32), meta_full, x], axis=0)
    qn_t = jnp.tile(p["q_norm"].reshape(1, HEAD), (1, NQ))
    kn_t = jnp.tile(p["k_norm"].reshape(1, HEAD), (1, NKV))
    seg = jnp.arange(256) // HEAD
    bd = (seg[:, None] == seg[None, :]).astype(BF)
    nm = p["norm_mix"].reshape(1, D)
    sinks = p["attn_sinks"].reshape(NQ)
    dsk = p["ssm_d"].reshape(1, D)
    abn, sbn, gf = (p[k].reshape(1, D) for k in ("attn_branch_norm", "ssm_branch_norm", "norm_ffn"))

    disc_in = (p["lam_re"], p["lam_im"], p["log_dt"], p["ssm_b_re"], p["ssm_b_im"])
    (ar, ai, bbr, bbi), disc_vjp = jax.vjp(_ssm_discretize, *disc_in)
    pw = _scan_powers(ar, ai)
    brp, bip = _pack_b(bbr).astype(BF), _pack_b(bbi).astype(BF)
    crp, cip = _pack_c(p["ssm_c_re"]).astype(BF), _pack_c(p["ssm_c_im"]).astype(BF)

    xn, qr, kr, qn, kn, v, u, gg = in_proj_fwd(hp, w["w_in"], nm, qn_t, kn_t, bd)
    attn = attention_fwd(sinks, qn, kn, v)
    y, sr, si = ssm_fwd(u, pw, brp, bip, crp, cip, dsk)
    zb, zzb, mgb, h1 = mid_fwd(y, attn, gg, hp, w["w_glu"], w["w_out"], abn, sbn)
    hnb, dgub, actb, dh2b, dh1, dh1b, dgf, loss = ffn_fwd_bwd(h1, tgt, w["w_ffn_in"], w["w_ffn_out"], gf)
    dy, dattn, dggb, dzzb, dabn, dsbn = mid_bwd(dh1b, y, attn, gg, zzb, w["w_glu"], w["w_out"], abn, sbn)
    dub, da, dbr, dbi, dcr, dci, dd = ssm_bwd(dy, u, sr, si, pw, brp, bip, crp, cip, dsk)
    dqn, dkn, dv, dsink = attention_bwd(sinks, qn, kn, v, dattn)
    dproj, dh0, dnm, dqw, dkw = in_proj_bwd(dqn, dkn, dv, dub, dggb, qr, kr, hp, dh1, w["w_in"], nm, qn_t, kn_t, bd)

    big = [weight_grad("grad_w_in", xn, dproj, 1152), weight_grad("grad_w_glu", zb, dzzb, 1024),
           weight_grad("grad_w_out", mgb, dh1b, 1024), weight_grad("grad_w_ffn_in", hnb, dgub, 1408),
           weight_grad("grad_w_ffn_out", actb, dh2b, 512)]

    dar = jnp.sum(da[0:8], axis=0).reshape(GROUPS, NSTATE)
    dai = jnp.sum(da[8:16], axis=0).reshape(GROUPS, NSTATE)
    dlr, dli, dldt, dbre, dbim = disc_vjp((dar, dai, _unpack_b(dbr), _unpack_b(dbi)))
    small = {
        "norm_mix": dnm, "q_norm": dqw.reshape(NQ, HEAD).sum(0), "k_norm": dkw.reshape(NKV, HEAD).sum(0),
        "attn_sinks": dsink[0, 0:NQ], "lam_re": dlr, "lam_im": dli, "log_dt": dldt,
        "ssm_b_re": dbre, "ssm_b_im": dbim, "ssm_c_re": _unpack_c(dcr), "ssm_c_im": _unpack_c(dci),
        "ssm_d": dd, "attn_branch_norm": dabn, "ssm_branch_norm": dsbn, "norm_ffn": dgf,
    }
    return loss[0, 0], dh0[PAD:PAD + seq], dh0[META0:PAD], small, big


def kernel(x, meta_tokens, norm_mix, w_in, q_norm, k_norm, attn_sinks, lam_re, lam_im, log_dt, ssm_b_re, ssm_b_im, ssm_c_re, ssm_c_im, ssm_d, w_glu, attn_branch_norm, ssm_branch_norm, w_out, norm_ffn, w_ffn_in, w_ffn_out, loss_target, m_meta_tokens, m_norm_mix, m_w_in, m_q_norm, m_k_norm, m_attn_sinks, m_lam_re, m_lam_im, m_log_dt, m_ssm_b_re, m_ssm_b_im, m_ssm_c_re, m_ssm_c_im, m_ssm_d, m_w_glu, m_attn_branch_norm, m_ssm_branch_norm, m_w_out, m_norm_ffn, m_w_ffn_in, m_w_ffn_out, v_meta_tokens, v_norm_mix, v_w_in, v_q_norm, v_k_norm, v_attn_sinks, v_lam_re, v_lam_im, v_log_dt, v_ssm_b_re, v_ssm_b_im, v_ssm_c_re, v_ssm_c_im, v_ssm_d, v_w_glu, v_attn_branch_norm, v_ssm_branch_norm, v_w_out, v_norm_ffn, v_w_ffn_in, v_w_ffn_out):
    args = dict(locals())
    names = ["meta_tokens", "norm_mix", "w_in", "q_norm", "k_norm", "attn_sinks", "lam_re", "lam_im", "log_dt",
             "ssm_b_re", "ssm_b_im", "ssm_c_re", "ssm_c_im", "ssm_d", "w_glu", "attn_branch_norm",
             "ssm_branch_norm", "w_out", "norm_ffn", "w_ffn_in", "w_ffn_out"]
    big_names = [s[0] for s in SHARDED]
    xs, ys, cs = _place()
    chip = 2 * xs + ys

    shards = [args[n][0].astype(BF) for n in big_names]
    *full, meta_full = gather_weights(shards, meta_tokens)
    w = dict(zip(big_names, full))

    p = {n: args[n][0] for n in SMALL}
    loss, dx, dmeta, gsmall, gbig = local_step(x[0], loss_target[0], meta_full, p, w)
    loss = lax.psum(loss, ("x", "y", "c"))

    place = jnp.stack([chip, cs]).astype(jnp.int32)
    axes = [s[3] for s in SHARDED]
    got = pair_exchange(gbig)
    halves = [pair_sum("pair_sum_" + n, place, g, b, ax) for n, g, b, ax in zip(big_names, gbig, got, axes)]
    parts = scatter_grads(halves)
    mine = [sum4("sum4_" + n, place, pt, h, ax) for n, pt, h, ax in zip(big_names, parts, halves, axes)]
    gfull = join_halves(mine)
    small_list = [gsmall[n].reshape(args[n].shape) for n in SMALL] + [dmeta]
    red = allreduce_small(_pack_small(small_list))
    small_red = _unpack_small(red, small_list)
    gmeta = lax.dynamic_slice_in_dim(small_red[-1], chip * (D // 4), D // 4, axis=1)

    out = {}
    for n, g in zip(big_names, gfull):
        d, nm_, nv_ = adamw("adamw_" + n, args[n][0], g, args["m_" + n][0], args["v_" + n][0])
        out[n] = tuple(t[None] for t in (g, d, nm_, nv_))
    d, nm_, nv_ = adamw("adamw_meta", meta_tokens, gmeta, m_meta_tokens, v_meta_tokens)
    out["meta_tokens"] = (gmeta, d, nm_, nv_)
    rows = red.shape[0]
    wp = _pack_small([args[n] for n in SMALL], rows)
    mp = _pack_small([args["m_" + n] for n in SMALL], rows)
    vp = _pack_small([args["v_" + n] for n in SMALL], rows)
    dp, nmp, nvp = adamw("adamw_small", wp, red, mp, vp)
    like = [args[n] for n in SMALL]
    for n, g_, d_, m_, v_ in zip(SMALL, small_red[:-1], _unpack_small(dp, like),
                                 _unpack_small(nmp, like), _unpack_small(nvp, like)):
        out[n] = (g_, d_, m_, v_)

    res = [loss, dx[None]]
    for k in range(4):
        res += [out[n][k] for n in names]
    return tuple(res)
```

```python
import functools
import math

import jax
import jax.numpy as jnp
from jax import lax
from jax.experimental import pallas as pl
from jax.experimental.pallas import tpu as pltpu

F32 = jnp.float32
BF = jnp.bfloat16

D = 1024
N_META = 16
HEAD = 64
NQ = 16
NKV = 4
QW = NQ * HEAD
KVW = NKV * HEAD
WIN = 128
GROUPS = 64
GCH = 16
NSTATE = 64
SW = GROUPS * NSTATE
DFF = 2816
IN_COLS = QW + 2 * KVW + D + 2 * D
PAD = 256
META0 = PAD - N_META
EPS = 1e-6
NEG = -1e30

TM = 256
TS = 128
LC = 512
NLC = SW // LC

LR, B1, B2, AEPS, WD, STEP = 0.001, 0.9, 0.999, 1e-08, 0.01, 10
BC1 = 1.0 - B1 ** STEP
BC2 = 1.0 - B2 ** STEP

MESH = pl.DeviceIdType.MESH
ANY = pl.BlockSpec(memory_space=pl.ANY)


def _cp(sem=None, vmem_mb=48):
    return pltpu.CompilerParams(dimension_semantics=sem, vmem_limit_bytes=vmem_mb << 20)


def _full(shape):
    n = len(shape)
    return pl.BlockSpec(shape, lambda *a: (0,) * n)


def _rows(tm, width):
    return pl.BlockSpec((tm, width), lambda i: (i, 0))


def _dot(a, b):
    return jnp.dot(a, b, preferred_element_type=F32)


def _dot_nt(a, b):
    return lax.dot_general(a, b, (((1,), (1,)), ((), ())), preferred_element_type=F32)


def _dot_tn(a, b):
    return lax.dot_general(a, b, (((0,), (0,)), ((), ())), preferred_element_type=F32)


def _sigmoid(x):
    return 1.0 / (1.0 + jnp.exp(-x))


def _seg_mean(x, bd):
    outs = []
    for j in range(x.shape[1] // 256):
        c = x[:, 256 * j:256 * (j + 1)]
        hi = c.astype(BF)
        lo = (c - hi.astype(F32)).astype(BF)
        outs.append(_dot(hi, bd) + _dot(lo, bd))
    out = outs[0] if len(outs) == 1 else jnp.concatenate(outs, axis=1)
    return out * (1.0 / HEAD)


_GC = math.sqrt(2.0 / math.pi)


def _gelu(x):
    t = jnp.tanh(_GC * (x + 0.044715 * x * x * x))
    return 0.5 * x * (1.0 + t), t


def _gelu_grad(x, t):
    return 0.5 * (1.0 + t) + 0.5 * x * (1.0 - t * t) * _GC * (1.0 + 3.0 * 0.044715 * x * x)


def _seq_specs():
    assert TM == PAD
    return [pl.BlockSpec((TM, D), lambda i: (jnp.maximum(i - 1, 0), 0)), _full((PAD, D))]


def _seq_tile(x_ref, m_ref):
    return jnp.where(pl.program_id(0) == 0, m_ref[...], x_ref[...])


def in_proj_fwd(x, mpad, w_in, nm, qn_t, kn_t, bd):
    lp = x.shape[0] + PAD

    def body(x_ref, m_ref, w_ref, nm_ref, qn_ref, kn_ref, bd_ref,
             xn_o, qr_o, kr_o, qn_o, kn_o, v_o, u_o, gg_o):
        h = _seq_tile(x_ref, m_ref)
        r = lax.rsqrt(jnp.mean(h * h, axis=-1, keepdims=True) + EPS)
        xb = ((h * r) * nm_ref[...]).astype(BF)
        xn_o[...] = xb
        bdv = bd_ref[...]
        q = _dot(xb, w_ref[:, 0:QW])
        qr_o[...] = q
        qn_o[...] = (q * lax.rsqrt(_seg_mean(q * q, bdv) + EPS) * qn_ref[...]).astype(BF)
        k = _dot(xb, w_ref[:, QW:QW + KVW])
        kr_o[...] = k
        kn_o[...] = (k * lax.rsqrt(_seg_mean(k * k, bdv) + EPS) * kn_ref[...]).astype(BF)
        v_o[...] = _dot(xb, w_ref[:, QW + KVW:QW + 2 * KVW]).astype(BF)
        u_o[...] = _dot(xb, w_ref[:, QW + 2 * KVW:QW + 2 * KVW + D])
        gg_o[...] = _dot(xb, w_ref[:, QW + 2 * KVW + D:IN_COLS])

    outs = [(D, BF), (QW, F32), (KVW, F32), (QW, BF), (KVW, BF), (KVW, BF), (D, F32), (2 * D, F32)]
    return pl.pallas_call(
        body, name="in_proj_fwd", grid=(lp // TM,),
        in_specs=_seq_specs() + [_full((D, IN_COLS)), _full((1, D)), _full((1, QW)), _full((1, KVW)),
                                 _full((256, 256))],
        out_specs=[_rows(TM, w) for w, _ in outs],
        out_shape=[jax.ShapeDtypeStruct((lp, w), t) for w, t in outs],
        compiler_params=_cp(("arbitrary",)),
    )(x, mpad, w_in, nm, qn_t, kn_t, bd)


def in_proj_bwd(dqn, dkn, dv, du, dgg, qr, kr, x, mpad, dh1, w_in, nm, qn_t, kn_t, bd):
    lp = x.shape[0] + PAD

    def body(dqn_ref, dkn_ref, dv_ref, du_ref, dgg_ref, qr_ref, kr_ref, x_ref, m_ref, dh1_ref,
             w_ref, nm_ref, qn_ref, kn_ref, bd_ref,
             dproj_o, dx_o, dm_o, dnm_o, dqw_o, dkw_o):
        i = pl.program_id(0)

        @pl.when(i == 0)
        def _():
            dnm_o[...] = jnp.zeros_like(dnm_o)
            dqw_o[...] = jnp.zeros_like(dqw_o)
            dkw_o[...] = jnp.zeros_like(dkw_o)

        bdv = bd_ref[...]

        def norm_bwd(x, dy, g, acc_ref):
            rr = lax.rsqrt(_seg_mean(x * x, bdv) + EPS)
            xh = x * rr
            acc_ref[...] += jnp.sum(dy * xh, axis=0, keepdims=True)
            dxh = dy * g
            return rr * (dxh - xh * _seg_mean(dxh * xh, bdv))

        dq = norm_bwd(qr_ref[...], dqn_ref[...], qn_ref[...], dqw_o)
        dk = norm_bwd(kr_ref[...], dkn_ref[...], kn_ref[...], dkw_o)
        dproj_o[:, 0:QW] = dq.astype(BF)
        dproj_o[:, QW:QW + KVW] = dk.astype(BF)
        dproj_o[:, QW + KVW:QW + 2 * KVW] = dv_ref[...].astype(BF)
        dproj_o[:, QW + 2 * KVW:QW + 2 * KVW + D] = du_ref[...]
        dproj_o[:, QW + 2 * KVW + D:IN_COLS] = dgg_ref[...]
        dxn = jnp.zeros((TM, D), F32)
        for c0 in range(0, IN_COLS, 512):
            dxn += _dot_nt(dproj_o[:, c0:c0 + 512], w_ref[:, c0:c0 + 512])
        h = _seq_tile(x_ref, m_ref)
        r = lax.rsqrt(jnp.mean(h * h, axis=-1, keepdims=True) + EPS)
        xh = h * r
        dnm_o[...] += jnp.sum(dxn * xh, axis=0, keepdims=True)
        dxh = dxn * nm_ref[...]
        dh0 = dh1_ref[...] + r * (dxh - xh * jnp.mean(dxh * xh, axis=-1, keepdims=True))
        dx_o[...] = dh0

        @pl.when(i == 0)
        def _():
            dm_o[...] = dh0

    return pl.pallas_call(
        body, name="in_proj_bwd", grid=(lp // TM,),
        in_specs=[_rows(TM, QW), _rows(TM, KVW), _rows(TM, KVW), _rows(TM, D), _rows(TM, 2 * D),
                  _rows(TM, QW), _rows(TM, KVW)] + _seq_specs() + [_rows(TM, D),
                  _full((D, IN_COLS)), _full((1, D)), _full((1, QW)), _full((1, KVW)), _full((256, 256))],
        out_specs=[_rows(TM, IN_COLS), _seq_specs()[0], _full((PAD, D)), _full((1, D)), _full((1, QW)),
                   _full((1, KVW))],
        out_shape=[jax.ShapeDtypeStruct((lp, IN_COLS), BF), jax.ShapeDtypeStruct((lp - PAD, D), F32),
                   jax.ShapeDtypeStruct((PAD, D), F32),
                   jax.ShapeDtypeStruct((1, D), F32), jax.ShapeDtypeStruct((1, QW), F32),
                   jax.ShapeDtypeStruct((1, KVW), F32)],
        compiler_params=_cp(("arbitrary",)),
    )(dqn, dkn, dv, du, dgg, qr, kr, x, mpad, dh1, w_in, nm, qn_t, kn_t, bd)


def _att_masks(b):
    qi = lax.broadcasted_iota(jnp.int32, (4 * WIN, WIN), 0) & (WIN - 1)
    kj = lax.broadcasted_iota(jnp.int32, (4 * WIN, WIN), 1)
    mask_c = (kj <= qi) & (b * WIN + kj >= META0)
    mask_p = (kj > qi) & ((b - 1) * WIN + kj >= PAD)
    mask_m = (kj >= WIN - N_META) & (b * WIN + qi >= PAD)
    return mask_m, mask_p, mask_c


def _dup_half(ref, kp, pos, lo):
    pair = ref[:, 128 * kp:128 * kp + 128].astype(F32)
    rolled = pltpu.roll(pair, 64, axis=1)
    keep = lo if pos == 0 else jnp.logical_not(lo)
    return jnp.where(keep, pair, rolled).astype(BF)


def _stack_heads(ref, kv, lo):
    parts = []
    for pp in range(2):
        pair = ref[:, 256 * kv + 128 * pp:256 * kv + 128 * pp + 128]
        zero = jnp.zeros_like(pair)
        parts.append(jnp.where(lo, pair, zero))
        parts.append(jnp.where(lo, zero, pair))
    return jnp.concatenate(parts, axis=0)


def _unstack_heads(x4, lo):
    a = jnp.where(lo, x4[0:WIN], x4[WIN:2 * WIN])
    b = jnp.where(lo, x4[2 * WIN:3 * WIN], x4[3 * WIN:4 * WIN])
    return a, b


def _sink_col(sink_ref, kv):
    row = lax.broadcasted_iota(jnp.int32, (4 * WIN, 1), 0) >> 7
    col = jnp.full((4 * WIN, 1), sink_ref[4 * kv + 3], F32)
    for r in range(3):
        col = jnp.where(row == r, sink_ref[4 * kv + r], col)
    return col


def _att_probs(q4, kds, masks, sink4):
    scale = HEAD ** -0.5
    ss = [jnp.where(mk, _dot_nt(q4, kd) * scale, NEG) for kd, mk in zip(kds, masks)]
    m = jnp.maximum(sink4, jnp.max(jnp.maximum(jnp.maximum(ss[0], ss[1]), ss[2]), axis=-1, keepdims=True))
    es = [jnp.exp(s - m) for s in ss]
    esink = jnp.exp(sink4 - m)
    den = esink + jnp.sum(es[0] + es[1] + es[2], axis=-1, keepdims=True)
    inv = 1.0 / den
    return [e * inv for e in es], esink * inv


def _kv_specs():
    return [pl.BlockSpec((WIN, KVW), lambda b: (1, 0)),
            pl.BlockSpec((WIN, KVW), lambda b: (jnp.maximum(b - 1, 0), 0)),
            pl.BlockSpec((WIN, KVW), lambda b: (b, 0))]


def attention_fwd(sinks, qn, kn, v):
    lp = qn.shape[0]

    def body(sink_ref, q_ref, km_ref, kp_ref, kc_ref, vm_ref, vp_ref, vc_ref, o_ref):
        b = pl.program_id(0)
        masks = _att_masks(b)
        lo = lax.broadcasted_iota(jnp.int32, (WIN, WIN), 1) < HEAD
        for kv in range(NKV):
            kp, pos = kv // 2, kv % 2
            kds = [_dup_half(r, kp, pos, lo) for r in (km_ref, kp_ref, kc_ref)]
            vds = [_dup_half(r, kp, pos, lo) for r in (vm_ref, vp_ref, vc_ref)]
            q4 = _stack_heads(q_ref, kv, lo)
            ps, _ = _att_probs(q4, kds, masks, _sink_col(sink_ref, kv))
            o4 = _dot(ps[0].astype(BF), vds[0]) + _dot(ps[1].astype(BF), vds[1]) + _dot(ps[2].astype(BF), vds[2])
            oa, ob = _unstack_heads(o4, lo)
            o_ref[:, 256 * kv:256 * kv + 128] = oa
            o_ref[:, 256 * kv + 128:256 * kv + 256] = ob

    return pl.pallas_call(
        body, name="attention_fwd", grid=(lp // WIN,),
        in_specs=[pl.BlockSpec(memory_space=pltpu.SMEM), _rows(WIN, QW)] + _kv_specs() + _kv_specs(),
        out_specs=_rows(WIN, QW),
        out_shape=jax.ShapeDtypeStruct((lp, QW), F32),
        compiler_params=_cp(("arbitrary",)),
    )(sinks, qn, kn, kn, kn, v, v, v)


def attention_bwd(sinks, qn, kn, v, do):
    lp = qn.shape[0]
    nb = lp // WIN

    def body(sink_ref, q_ref, km_ref, kp_ref, kc_ref, vm_ref, vp_ref, vc_ref, do_ref,
             dq_ref, dk_hbm, dv_hbm, dsink_ref, dk_acc, dv_acc):
        b = pl.program_id(0)

        @pl.when(b == 0)
        def _():
            dk_acc[...] = jnp.zeros_like(dk_acc)
            dv_acc[...] = jnp.zeros_like(dv_acc)
            dsink_ref[...] = jnp.zeros_like(dsink_ref)

        masks = _att_masks(b)
        lo = lax.broadcasted_iota(jnp.int32, (WIN, WIN), 1) < HEAD
        lane8 = lax.broadcasted_iota(jnp.int32, (8, 128), 1)
        row4 = lax.broadcasted_iota(jnp.int32, (4 * WIN, 1), 0) >> 7
        starts = [WIN, pl.multiple_of(jnp.maximum(b - 1, 0) * WIN, WIN), pl.multiple_of(b * WIN, WIN)]
        scale = HEAD ** -0.5
        for kv in range(NKV):
            kp, pos = kv // 2, kv % 2
            keep = lo if pos == 0 else jnp.logical_not(lo)
            kds = [_dup_half(r, kp, pos, lo) for r in (km_ref, kp_ref, kc_ref)]
            vds = [_dup_half(r, kp, pos, lo) for r in (vm_ref, vp_ref, vc_ref)]
            q4 = _stack_heads(q_ref, kv, lo)
            do4 = _stack_heads(do_ref, kv, lo)
            ps, psink = _att_probs(q4, kds, masks, _sink_col(sink_ref, kv))
            dps = [_dot_nt(do4, vd) for vd in vds]
            delta = jnp.sum(ps[0] * dps[0] + ps[1] * dps[1] + ps[2] * dps[2], axis=-1, keepdims=True)
            dss = [(p * (dp - delta) * scale).astype(BF) for p, dp in zip(ps, dps)]
            dsk = psink * delta
            for r in range(4):
                val = -jnp.sum(jnp.where(row4 == r, dsk, 0.0), axis=0, keepdims=True)
                dsink_ref[...] += jnp.where(lane8 == 4 * kv + r, val, 0.0)
            dq4 = _dot(dss[0], kds[0]) + _dot(dss[1], kds[1]) + _dot(dss[2], kds[2])
            dqa, dqb = _unstack_heads(dq4, lo)
            dq_ref[:, 256 * kv:256 * kv + 128] = dqa
            dq_ref[:, 256 * kv + 128:256 * kv + 256] = dqb
            for x in range(3):
                dkd = _dot_tn(dss[x], q4)
                dvd = _dot_tn(ps[x].astype(BF), do4)
                dkd = jnp.where(keep, dkd + pltpu.roll(dkd, 64, axis=1), 0.0)
                dvd = jnp.where(keep, dvd + pltpu.roll(dvd, 64, axis=1), 0.0)
                dk_acc[pl.ds(starts[x], WIN), 128 * kp:128 * kp + 128] += dkd
                dv_acc[pl.ds(starts[x], WIN), 128 * kp:128 * kp + 128] += dvd

        @pl.when(b == nb - 1)
        def _():
            pltpu.sync_copy(dk_acc, dk_hbm)
            pltpu.sync_copy(dv_acc, dv_hbm)

    return pl.pallas_call(
        body, name="attention_bwd", grid=(nb,),
        in_specs=[pl.BlockSpec(memory_space=pltpu.SMEM), _rows(WIN, QW)] + _kv_specs() + _kv_specs()
                 + [_rows(WIN, QW)],
        out_specs=[_rows(WIN, QW), ANY, ANY, _full((8, 128))],
        out_shape=[jax.ShapeDtypeStruct((lp, QW), F32), jax.ShapeDtypeStruct((lp, KVW), F32),
                   jax.ShapeDtypeStruct((lp, KVW), F32), jax.ShapeDtypeStruct((8, 128), F32)],
        scratch_shapes=[pltpu.VMEM((lp, KVW), F32), pltpu.VMEM((lp, KVW), F32)],
        compiler_params=_cp(("arbitrary",)),
    )(sinks, qn, kn, kn, kn, v, v, v, do)


PW_ROWS = 64


def _scan_tile(xr, xi, pw_ref, base, ls, reverse):
    for s, k in enumerate((1, 2, 4)):
        shift = 8 - k if reverse else k
        sr = pltpu.roll(xr, shift, axis=0)
        si = pltpu.roll(xi, shift, axis=0)
        akr = pw_ref[base + 16 * s:base + 16 * s + 8, ls]
        aki = pw_ref[base + 16 * s + 8:base + 16 * s + 16, ls]
        xr, xi = xr + akr * sr - aki * si, xi + akr * si + aki * sr
    return xr, xi


def ssm_fwd(u, pw, br, bi, cr, ci, dsk):
    lp = u.shape[0]

    def body(u_ref, pw_ref, br_ref, bi_ref, cr_ref, ci_ref, d_ref, y_ref, sr_ref, si_ref, car_r, car_i):
        i = pl.program_id(0)

        @pl.when(i == 0)
        def _():
            car_r[...] = jnp.zeros_like(car_r)
            car_i[...] = jnp.zeros_like(car_i)

        u = u_ref[...]
        ub = u.astype(BF)
        for j in range(NLC):
            uj = ub[:, 128 * j:128 * j + 128]
            sr_ref[:, LC * j:LC * j + LC] = _dot(uj, br_ref[j])
            si_ref[:, LC * j:LC * j + LC] = _dot(uj, bi_ref[j])
        for j in range(NLC):
            ls = slice(LC * j, LC * j + LC)

            def group(g, carry):
                cr_, ci_ = carry
                rows = pl.ds(pl.multiple_of(g * 8, 8), 8)
                xr, xi = _scan_tile(sr_ref[rows, ls], si_ref[rows, ls], pw_ref, 0, ls, False)
                pr, pi = pw_ref[48:56, ls], pw_ref[56:64, ls]
                xr, xi = xr + pr * cr_ - pi * ci_, xi + pr * ci_ + pi * cr_
                sr_ref[rows, ls] = xr
                si_ref[rows, ls] = xi
                return xr[7:8, :], xi[7:8, :]

            cr_, ci_ = lax.fori_loop(0, TS // 8, group, (car_r[:, ls], car_i[:, ls]))
            car_r[:, ls] = cr_
            car_i[:, ls] = ci_
        for j in range(NLC):
            ls = slice(LC * j, LC * j + LC)
            y_ref[:, 128 * j:128 * j + 128] = (_dot(sr_ref[:, ls].astype(BF), cr_ref[j])
                                               - _dot(si_ref[:, ls].astype(BF), ci_ref[j]))
        y_ref[...] += d_ref[...] * u

    return pl.pallas_call(
        body, name="ssm_fwd", grid=(lp // TS,),
        in_specs=[_rows(TS, D), _full((2 * PW_ROWS, SW)), _full((NLC, 128, LC)), _full((NLC, 128, LC)),
                  _full((NLC, LC, 128)), _full((NLC, LC, 128)), _full((1, D))],
        out_specs=[_rows(TS, D), _rows(TS, SW), _rows(TS, SW)],
        out_shape=[jax.ShapeDtypeStruct((lp, D), F32), jax.ShapeDtypeStruct((lp, SW), F32),
                   jax.ShapeDtypeStruct((lp, SW), F32)],
        scratch_shapes=[pltpu.VMEM((1, SW), F32), pltpu.VMEM((1, SW), F32)],
        compiler_params=_cp(("arbitrary",)),
    )(u, pw, br, bi, cr, ci, dsk)


def ssm_bwd(dy, u, sr, si, pw, br, bi, cr, ci, dsk):
    lp = u.shape[0]
    nt = lp // TS

    def rev(i):
        return (nt - 1 - i, 0)

    def prev8(i):
        return (jnp.maximum((nt - 1 - i) * (TS // 8) - 1, 0), 0)

    def body(dy_ref, u_ref, sr_ref, si_ref, pr8_ref, pi8_ref, pw_ref, br_ref, bi_ref, cr_ref, ci_ref, d_ref,
             du_ref, da_ref, dbr_ref, dbi_ref, dcr_ref, dci_ref, dd_ref, gr_s, gi_s, car_r, car_i):
        i = pl.program_id(0)

        @pl.when(i == 0)
        def _():
            car_r[...] = jnp.zeros_like(car_r)
            car_i[...] = jnp.zeros_like(car_i)
            for ref in (da_ref, dbr_ref, dbi_ref, dcr_ref, dci_ref, dd_ref):
                ref[...] = jnp.zeros_like(ref)

        first = (i == nt - 1).astype(F32)
        dy = dy_ref[...]
        u = u_ref[...]
        dyb = dy.astype(BF)
        ub = u.astype(BF)
        dd_ref[...] += jnp.sum(dy * u, axis=0, keepdims=True)
        for j in range(NLC):
            ls = slice(LC * j, LC * j + LC)
            dyj = dyb[:, 128 * j:128 * j + 128]
            gr_s[:, ls] = _dot_nt(dyj, cr_ref[j])
            gi_s[:, ls] = -_dot_nt(dyj, ci_ref[j])
            dcr_ref[j] += _dot_tn(sr_ref[:, ls].astype(BF), dyj)
            dci_ref[j] -= _dot_tn(si_ref[:, ls].astype(BF), dyj)
        rid = lax.broadcasted_iota(jnp.int32, (8, LC), 0)
        for j in range(NLC):
            ls = slice(LC * j, LC * j + LC)

            def one_group(rows, spr, spi, carry):
                cr_, ci_ = carry
                gr, gi = _scan_tile(gr_s[rows, ls], gi_s[rows, ls], pw_ref, PW_ROWS, ls, True)
                pr, pi = pw_ref[PW_ROWS + 48:PW_ROWS + 56, ls], pw_ref[PW_ROWS + 56:PW_ROWS + 64, ls]
                gr, gi = gr + pr * cr_ - pi * ci_, gi + pr * ci_ + pi * cr_
                gr_s[rows, ls] = gr
                gi_s[rows, ls] = gi
                s_r = jnp.where(rid >= 1, pltpu.roll(sr_ref[rows, ls], 1, axis=0), spr[7:8, :])
                s_i = jnp.where(rid >= 1, pltpu.roll(si_ref[rows, ls], 1, axis=0), spi[7:8, :])
                da_ref[0:8, ls] += gr * s_r + gi * s_i
                da_ref[8:16, ls] += gi * s_r - gr * s_i
                return gr[0:1, :], gi[0:1, :]

            def group(k, carry):
                g = TS // 8 - 1 - k
                rows = pl.ds(pl.multiple_of(g * 8, 8), 8)
                before = pl.ds(pl.multiple_of(g * 8 - 8, 8), 8)
                return one_group(rows, sr_ref[before, ls], si_ref[before, ls], carry)

            carry = lax.fori_loop(0, TS // 8 - 1, group, (car_r[:, ls], car_i[:, ls]))
            keep = 1.0 - first
            cr_, ci_ = one_group(pl.ds(0, 8), pr8_ref[:, ls] * keep, pi8_ref[:, ls] * keep, carry)
            car_r[:, ls] = cr_
            car_i[:, ls] = ci_
        for j in range(NLC):
            ls = slice(LC * j, LC * j + LC)
            uj = ub[:, 128 * j:128 * j + 128]
            grb = gr_s[:, ls].astype(BF)
            gib = gi_s[:, ls].astype(BF)
            du_ref[:, 128 * j:128 * j + 128] = (_dot_nt(grb, br_ref[j]) + _dot_nt(gib, bi_ref[j])
                                                + d_ref[:, 128 * j:128 * j + 128] * dy[:, 128 * j:128 * j + 128]
                                                ).astype(BF)
            dbr_ref[j] += _dot_tn(uj, grb)
            dbi_ref[j] += _dot_tn(uj, gib)

    return pl.pallas_call(
        body, name="ssm_bwd", grid=(nt,),
        in_specs=[pl.BlockSpec((TS, D), rev), pl.BlockSpec((TS, D), rev), pl.BlockSpec((TS, SW), rev),
                  pl.BlockSpec((TS, SW), rev), pl.BlockSpec((8, SW), prev8), pl.BlockSpec((8, SW), prev8),
                  _full((2 * PW_ROWS, SW)), _full((NLC, 128, LC)), _full((NLC, 128, LC)),
                  _full((NLC, LC, 128)), _full((NLC, LC, 128)), _full((1, D))],
        out_specs=[pl.BlockSpec((TS, D), rev), _full((16, SW)), _full((NLC, 128, LC)), _full((NLC, 128, LC)),
                   _full((NLC, LC, 128)), _full((NLC, LC, 128)), _full((1, D))],
        out_shape=[jax.ShapeDtypeStruct((lp, D), BF), jax.ShapeDtypeStruct((16, SW), F32),
                   jax.ShapeDtypeStruct((NLC, 128, LC), F32), jax.ShapeDtypeStruct((NLC, 128, LC), F32),
                   jax.ShapeDtypeStruct((NLC, LC, 128), F32), jax.ShapeDtypeStruct((NLC, LC, 128), F32),
                   jax.ShapeDtypeStruct((1, D), F32)],
        scratch_shapes=[pltpu.VMEM((TS, SW), F32), pltpu.VMEM((TS, SW), F32),
                        pltpu.VMEM((1, SW), F32), pltpu.VMEM((1, SW), F32)],
        compiler_params=_cp(("arbitrary",)),
    )(dy, u, sr, si, sr, si, pw, br, bi, cr, ci, dsk)


def _merge_parts(attn, zz, gg, abn, sbn):
    za, zb = zz[:, 0:D], zz[:, D:2 * D]
    sgz = _sigmoid(zb)
    ssm = za * sgz
    ra = lax.rsqrt(jnp.mean(attn * attn, axis=-1, keepdims=True) + EPS)
    rs = lax.rsqrt(jnp.mean(ssm * ssm, axis=-1, keepdims=True) + EPS)
    ah, sh = attn * ra, ssm * rs
    sga, sgs = _sigmoid(gg[:, 0:D]), _sigmoid(gg[:, D:2 * D])
    return za, sgz, ra, rs, ah, sh, sga, sgs, ah * abn, sh * sbn


def mid_fwd(y, attn, gg, x, mpad, w_glu, w_out, abn, sbn):
    lp = y.shape[0]

    def body(y_ref, a_ref, gg_ref, x_ref, m_ref, wg_ref, wo_ref, abn_ref, sbn_ref, z_o, zz_o, mg_o, h1_o):
        z, _ = _gelu(y_ref[...])
        zb = z.astype(BF)
        z_o[...] = zb
        zz = _dot(zb, wg_ref[...])
        zz_o[...] = zz.astype(BF)
        _, _, _, _, _, _, sga, sgs, an, sn = _merge_parts(a_ref[...], zz, gg_ref[...], abn_ref[...], sbn_ref[...])
        mb = (sga * an + sgs * sn).astype(BF)
        mg_o[...] = mb
        h1_o[...] = _seq_tile(x_ref, m_ref) + _dot(mb, wo_ref[...])

    return pl.pallas_call(
        body, name="mid_fwd", grid=(lp // TM,),
        in_specs=[_rows(TM, D), _rows(TM, D), _rows(TM, 2 * D)] + _seq_specs() + [_full((D, 2 * D)), _full((D, D)),
                  _full((1, D)), _full((1, D))],
        out_specs=[_rows(TM, D), _rows(TM, 2 * D), _rows(TM, D), _rows(TM, D)],
        out_shape=[jax.ShapeDtypeStruct((lp, D), BF), jax.ShapeDtypeStruct((lp, 2 * D), BF),
                   jax.ShapeDtypeStruct((lp, D), BF), jax.ShapeDtypeStruct((lp, D), F32)],
        compiler_params=_cp(("arbitrary",)),
    )(y, attn, gg, x, mpad, w_glu, w_out, abn, sbn)


def mid_bwd(dh1b, y, attn, gg, zzb, w_glu, w_out, abn, sbn):
    lp = y.shape[0]

    def body(dh_ref, y_ref, a_ref, gg_ref, zz_ref, wg_ref, wo_ref, abn_ref, sbn_ref,
             dy_o, dattn_o, dgg_o, dzz_o, dabn_o, dsbn_o):
        i = pl.program_id(0)

        @pl.when(i == 0)
        def _():
            dabn_o[...] = jnp.zeros_like(dabn_o)
            dsbn_o[...] = jnp.zeros_like(dsbn_o)

        dm = _dot_nt(dh_ref[...], wo_ref[...])
        abn, sbn = abn_ref[...], sbn_ref[...]
        za, sgz, ra, rs, ah, sh, sga, sgs, an, sn = _merge_parts(
            a_ref[...], zz_ref[...].astype(F32), gg_ref[...], abn, sbn)
        dgg_o[:, 0:D] = (dm * an * sga * (1.0 - sga)).astype(BF)
        dgg_o[:, D:2 * D] = (dm * sn * sgs * (1.0 - sgs)).astype(BF)
        da = dm * sga
        dabn_o[...] += jnp.sum(da * ah, axis=0, keepdims=True)
        dah = da * abn
        dattn_o[...] = (ra * (dah - ah * jnp.mean(dah * ah, axis=-1, keepdims=True))).astype(BF)
        ds = dm * sgs
        dsbn_o[...] += jnp.sum(ds * sh, axis=0, keepdims=True)
        dsh = ds * sbn
        dssm = rs * (dsh - sh * jnp.mean(dsh * sh, axis=-1, keepdims=True))
        dzz_o[:, 0:D] = (dssm * sgz).astype(BF)
        dzz_o[:, D:2 * D] = (dssm * za * sgz * (1.0 - sgz)).astype(BF)
        dz = _dot_nt(dzz_o[...], wg_ref[...])
        yv = y_ref[...]
        _, t = _gelu(yv)
        dy_o[...] = dz * _gelu_grad(yv, t)

    return pl.pallas_call(
        body, name="mid_bwd", grid=(lp // TM,),
        in_specs=[_rows(TM, D), _rows(TM, D), _rows(TM, D), _rows(TM, 2 * D), _rows(TM, 2 * D),
                  _full((D, 2 * D)), _full((D, D)), _full((1, D)), _full((1, D))],
        out_specs=[_rows(TM, D), _rows(TM, D), _rows(TM, 2 * D), _rows(TM, 2 * D), _full((1, D)), _full((1, D))],
        out_shape=[jax.ShapeDtypeStruct((lp, D), F32), jax.ShapeDtypeStruct((lp, D), BF),
                   jax.ShapeDtypeStruct((lp, 2 * D), BF), jax.ShapeDtypeStruct((lp, 2 * D), BF),
                   jax.ShapeDtypeStruct((1, D), F32), jax.ShapeDtypeStruct((1, D), F32)],
        compiler_params=_cp(("arbitrary",)),
    )(dh1b, y, attn, gg, zzb, w_glu, w_out, abn, sbn)


def ffn_fwd_bwd(h1, tgt, w_fi, w_fo, gf):
    lp = h1.shape[0]
    tf = 128

    def body(h_ref, t_ref, wi_ref, wo_ref, gf_ref, hn_o, dgu_o, act_o, dh2_o, dh1_o, dh1b_o, dgf_o, loss_o):
        i = pl.program_id(0)

        @pl.when(i == 0)
        def _():
            dgf_o[...] = jnp.zeros_like(dgf_o)
            loss_o[...] = jnp.zeros_like(loss_o)

        h = h_ref[...]
        r = lax.rsqrt(jnp.mean(h * h, axis=-1, keepdims=True) + EPS)
        xh = h * r
        hb = (xh * gf_ref[...]).astype(BF)
        hn_o[...] = hb
        gate = _dot(hb, wi_ref[:, 0:DFF])
        up = _dot(hb, wi_ref[:, DFF:2 * DFF])
        sg = _sigmoid(gate)
        sl = gate * sg
        ab = (sl * up).astype(BF)
        act_o[...] = ab
        h2 = h + _dot(ab, wo_ref[...])
        real = (i >= PAD // tf).astype(F32)
        e = (h2 - t_ref[...]) * real
        loss_o[...] += 0.5 * jnp.sum(jnp.sum(e * e, axis=-1, keepdims=True), axis=0, keepdims=True) * (1.0 / D)
        dh2 = e * (1.0 / D)
        db = dh2.astype(BF)
        dh2_o[...] = db
        dact = _dot_nt(db, wo_ref[...])
        dgate = (dact * up * (sg * (1.0 + gate * (1.0 - sg)))).astype(BF)
        dup = (dact * sl).astype(BF)
        dgu_o[:, 0:DFF] = dgate
        dgu_o[:, DFF:2 * DFF] = dup
        dhn = _dot_nt(dgate, wi_ref[:, 0:DFF]) + _dot_nt(dup, wi_ref[:, DFF:2 * DFF])
        dgf_o[...] += jnp.sum(dhn * xh, axis=0, keepdims=True)
        dxh = dhn * gf_ref[...]
        dh1 = dh2 + r * (dxh - xh * jnp.mean(dxh * xh, axis=-1, keepdims=True))
        dh1_o[...] = dh1
        dh1b_o[...] = dh1.astype(BF)

    return pl.pallas_call(
        body, name="ffn_fwd_bwd", grid=(lp // tf,),
        in_specs=[_rows(tf, D), pl.BlockSpec((tf, D), lambda i: (jnp.maximum(i - PAD // tf, 0), 0)),
                  _full((D, 2 * DFF)), _full((DFF, D)), _full((1, D))],
        out_specs=[_rows(tf, D), _rows(tf, 2 * DFF), _rows(tf, DFF), _rows(tf, D), _rows(tf, D), _rows(tf, D),
                   _full((1, D)), _full((1, 1))],
        out_shape=[jax.ShapeDtypeStruct((lp, D), BF), jax.ShapeDtypeStruct((lp, 2 * DFF), BF),
                   jax.ShapeDtypeStruct((lp, DFF), BF), jax.ShapeDtypeStruct((lp, D), BF),
                   jax.ShapeDtypeStruct((lp, D), F32), jax.ShapeDtypeStruct((lp, D), BF),
                   jax.ShapeDtypeStruct((1, D), F32), jax.ShapeDtypeStruct((1, 1), F32)],
        compiler_params=_cp(("arbitrary",), vmem_mb=56),
    )(h1, tgt, w_fi, w_fo, gf)


def weight_grad(name, a, b, tn):
    lp, k = a.shape
    n = b.shape[1]
    tr = next(t for t in (768, 512, 384, 256, 128) if lp % t == 0)

    def body(a_ref, b_ref, o_ref):
        @pl.when(pl.program_id(1) == 0)
        def _():
            o_ref[...] = jnp.zeros_like(o_ref)

        o_ref[...] += _dot_tn(a_ref[...], b_ref[...])

    return pl.pallas_call(
        body, name=name, grid=(n // tn, lp // tr),
        in_specs=[pl.BlockSpec((tr, k), lambda j, m: (m, 0)), pl.BlockSpec((tr, tn), lambda j, m: (m, j))],
        out_specs=pl.BlockSpec((k, tn), lambda j, m: (0, j)),
        out_shape=jax.ShapeDtypeStruct((k, n), F32),
        compiler_params=_cp(("arbitrary", "arbitrary")),
    )(a, b)


def _adamw_math(w, g, m, v):
    m = B1 * m + (1.0 - B1) * g
    v = B2 * v + (1.0 - B2) * (g * g)
    delta = -LR * ((m / BC1) / (jnp.sqrt(v / BC2) + AEPS) + WD * w)
    return delta, m, v


def _row_tile(r):
    return next((t for t in (256, 128, 64, 32, 16, 8) if r % t == 0), r)


def adamw(name, w, g, m, v):
    r, c = w.shape
    tr = _row_tile(r)

    def body(w_ref, g_ref, m_ref, v_ref, d_o, m_o, v_o):
        d_o[...], m_o[...], v_o[...] = _adamw_math(w_ref[...], g_ref[...], m_ref[...], v_ref[...])

    spec = pl.BlockSpec((tr, c), lambda i: (i, 0))
    return pl.pallas_call(
        body, name=name, grid=(r // tr,), in_specs=[spec] * 4, out_specs=[spec] * 3,
        out_shape=[jax.ShapeDtypeStruct((r, c), F32)] * 3,
        compiler_params=_cp(("arbitrary",)),
    )(w, g, m, v)


def adamw_small(ws, gs, ms, vs):
    n = len(ws)
    steps = 8

    def spec(a):
        if a.ndim == 4:
            return pl.BlockSpec((1, a.shape[1] // steps) + a.shape[2:], lambda i: (0, i, 0, 0))
        nd = a.ndim
        return pl.BlockSpec(a.shape, lambda i: (0,) * nd)

    def body(*refs):
        w, g, m, v, d_o, m_o, v_o = (refs[k * n:(k + 1) * n] for k in range(7))
        for t in range(n):
            d_o[t][...], m_o[t][...], v_o[t][...] = _adamw_math(w[t][...], g[t][...], m[t][...], v[t][...])

    specs = [spec(a) for a in ws]
    res = pl.pallas_call(
        body, name="adamw_small", grid=(steps,), in_specs=specs * 4, out_specs=specs * 3,
        out_shape=[jax.ShapeDtypeStruct(a.shape, F32) for a in ws] * 3,
        compiler_params=_cp(("arbitrary",)),
    )(*ws, *gs, *ms, *vs)
    return res[:n], res[n:2 * n], res[2 * n:]


def pair_sum(name, place, full, got, ax):
    r, c = got.shape
    tr = _row_tile(r)

    def body(s_ref, a_ref, b_ref, o_ref):
        o_ref[...] = (a_ref[...] + b_ref[...]).astype(BF)

    if ax == 1:
        mine = pl.BlockSpec((tr, c), lambda i, s: (s[1] * (r // tr) + i, 0))
    else:
        mine = pl.BlockSpec((tr, c), lambda i, s: (i, s[1]))
    spec = pl.BlockSpec((tr, c), lambda i, s: (i, 0))
    return pl.pallas_call(
        body, name=name,
        grid_spec=pltpu.PrefetchScalarGridSpec(num_scalar_prefetch=1, grid=(r // tr,), in_specs=[mine, spec],
                                               out_specs=spec),
        out_shape=jax.ShapeDtypeStruct((r, c), BF),
        compiler_params=_cp(("arbitrary",)),
    )(place, full, got)


def sum4(name, place, parts, half, ax):
    _, r, c = parts.shape
    tr = _row_tile(r)

    def body(s_ref, p_ref, h_ref, o_ref):
        o_ref[...] = ((p_ref[0].astype(F32) + p_ref[1].astype(F32))
                      + (p_ref[2].astype(F32) + h_ref[...].astype(F32)))

    if ax == 1:
        own = pl.BlockSpec((tr, c), lambda i, s: (i, s[0]))
        out = pl.BlockSpec((tr, c), lambda i, s: (s[1] * (r // tr) + i, 0))
        shape = (2 * r, c)
    else:
        own = pl.BlockSpec((tr, c), lambda i, s: (s[0] * (r // tr) + i, 0))
        out = pl.BlockSpec((tr, c), lambda i, s: (i, s[1]))
        shape = (r, 2 * c)
    return pl.pallas_call(
        body, name=name,
        grid_spec=pltpu.PrefetchScalarGridSpec(
            num_scalar_prefetch=1, grid=(r // tr,),
            in_specs=[pl.BlockSpec((3, tr, c), lambda i, s: (0, i, 0)), own], out_specs=out),
        out_shape=jax.ShapeDtypeStruct(shape, F32),
        compiler_params=_cp(("arbitrary",)),
    )(place, parts, half)


def _place():
    x, y, c = lax.axis_index("x"), lax.axis_index("y"), lax.axis_index("c")
    return x, y, c


def _other_chips(x, y):
    return [(1 - x, y), (x, 1 - y), (1 - x, 1 - y)]


SHARDED = (("w_in", D, IN_COLS, 1), ("w_glu", D, 2 * D, 1), ("w_out", D, D, 0),
           ("w_ffn_in", D, 2 * DFF, 1), ("w_ffn_out", DFF, D, 0))


def _shard_of(ref, axis, k, size):
    if axis == 0:
        return ref.at[pl.ds(k * size, size), :]
    return ref.at[:, pl.ds(k * size, size)]


def gather_weights(shards, meta):
    items = [(s, r, c, ax) for s, (_, r, c, ax) in zip(shards, SHARDED)] + [(meta, N_META, D, 1)]
    n = len(items)

    def body(*refs):
        ins, outs = refs[:n], refs[n:2 * n]
        ici_send, ici_recv, d2d_send, d2d_recv, local_sems = refs[2 * n:]
        x, y, c = _place()
        mine = 2 * x + y
        chips = _other_chips(x, y)

        def piece(t, k, h):
            _, r, cc, ax = items[t]
            if ax == 0:
                return outs[t].at[pl.ds(k * (r // 4) + h * (r // 8), r // 8), :]
            return outs[t].at[pl.ds(h * (r // 2), r // 2), pl.ds(k * (cc // 4), cc // 4)]

        def src_half(t, h):
            rs = ins[t].shape[0]
            return ins[t].at[pl.ds(h * (rs // 2), rs // 2), :]

        local, first, passed = [], [], []
        for t in range(n):
            _, r, cc, ax = items[t]
            size = (r if ax == 0 else cc) // 4
            loc = pltpu.make_async_remote_copy(
                src_ref=ins[t], dst_ref=_shard_of(outs[t], ax, mine, size), send_sem=local_sems.at[t],
                recv_sem=local_sems.at[n + t], device_id=(x, y, 1 - c), device_id_type=MESH)
            loc.start()
            local.append(loc)
            for j, chip in enumerate(chips):
                cp = pltpu.make_async_remote_copy(
                    src_ref=src_half(t, c), dst_ref=piece(t, mine, c),
                    send_sem=ici_send.at[3 * t + j], recv_sem=ici_recv.at[3 * t + j],
                    device_id=(*chip, c), device_id_type=MESH)
                cp.start()
                first.append(cp)
        for t in range(n):
            for j, (px, py) in enumerate(chips):
                got = piece(t, 2 * px + py, c)
                pltpu.make_async_remote_copy(
                    src_ref=src_half(t, c), dst_ref=got, send_sem=ici_send.at[3 * t + j],
                    recv_sem=ici_recv.at[3 * t + j], device_id=(px, py, c), device_id_type=MESH).wait_recv()
                cp = pltpu.make_async_remote_copy(
                    src_ref=got, dst_ref=got, send_sem=d2d_send.at[3 * t + j], recv_sem=d2d_recv.at[3 * t + j],
                    device_id=(x, y, 1 - c), device_id_type=MESH)
                cp.start()
                passed.append(cp)
        for t in range(n):
            for j, (px, py) in enumerate(chips):
                other = piece(t, 2 * px + py, 1 - c)
                pltpu.make_async_remote_copy(
                    src_ref=other, dst_ref=other, send_sem=d2d_send.at[3 * t + j], recv_sem=d2d_recv.at[3 * t + j],
                    device_id=(x, y, 1 - c), device_id_type=MESH).wait_recv()
        for cp in first + passed:
            cp.wait_send()
        for loc in local:
            loc.wait()

    return pl.pallas_call(
        body, name="gather_weights",
        in_specs=[ANY] * n, out_specs=[ANY] * n,
        out_shape=[jax.ShapeDtypeStruct((r, c), s.dtype) for s, r, c, _ in items],
        scratch_shapes=[pltpu.SemaphoreType.DMA((3 * n,)), pltpu.SemaphoreType.DMA((3 * n,)),
                        pltpu.SemaphoreType.DMA((3 * n,)), pltpu.SemaphoreType.DMA((3 * n,)),
                        pltpu.SemaphoreType.DMA((2 * n,))],
        compiler_params=pltpu.CompilerParams(has_side_effects=True),
    )(*shards, meta)


def _half_of(ref, ax, h):
    r, c = ref.shape
    if ax == 1:
        return ref.at[pl.ds(h * (r // 2), r // 2), :]
    return ref.at[:, pl.ds(h * (c // 2), c // 2)]


def _half_shape(r, c, ax):
    return (r // 2, c) if ax == 1 else (r, c // 2)


def pair_exchange(grads):
    n = len(grads)

    def body(*refs):
        ins, got = refs[:n], refs[n:2 * n]
        send_sems, recv_sems = refs[2 * n:]
        x, y, c = _place()
        cps = []
        for t, (_, _, _, ax) in enumerate(SHARDED):
            cp = pltpu.make_async_remote_copy(
                src_ref=_half_of(ins[t], ax, 1 - c), dst_ref=got[t], send_sem=send_sems.at[t],
                recv_sem=recv_sems.at[t], device_id=(x, y, 1 - c), device_id_type=MESH)
            cp.start()
            cps.append(cp)
        for cp in cps:
            cp.wait()

    return pl.pallas_call(
        body, name="pair_exchange",
        in_specs=[ANY] * n, out_specs=[ANY] * n,
        out_shape=[jax.ShapeDtypeStruct(_half_shape(r, c, ax), F32) for _, r, c, ax in SHARDED],
        scratch_shapes=[pltpu.SemaphoreType.DMA((n,)), pltpu.SemaphoreType.DMA((n,))],
        compiler_params=pltpu.CompilerParams(has_side_effects=True),
    )(*grads)


def scatter_grads(halves):
    n = len(halves)

    def shapes():
        out = []
        for _, r, c, ax in SHARDED:
            hr, hc = _half_shape(r, c, ax)
            out.append((hr // 4, hc) if ax == 0 else (hr, hc // 4))
        return out

    def body(*refs):
        ins, outs = refs[:n], refs[n:2 * n]
        send_sems, recv_sems = refs[2 * n:]
        x, y, c = _place()
        started = []
        for t, (src, dst, (_, _, _, ax)) in enumerate(zip(ins, outs, SHARDED)):
            size = src.shape[ax] // 4
            for j, (px, py) in enumerate(_other_chips(x, y)):
                cp = pltpu.make_async_remote_copy(
                    src_ref=_shard_of(src, ax, 2 * px + py, size), dst_ref=dst.at[j],
                    send_sem=send_sems.at[3 * t + j], recv_sem=recv_sems.at[3 * t + j],
                    device_id=(px, py, c), device_id_type=MESH)
                cp.start()
                started.append(cp)
        for cp in started:
            cp.wait()

    return pl.pallas_call(
        body, name="scatter_grads",
        in_specs=[ANY] * n, out_specs=[ANY] * n,
        out_shape=[jax.ShapeDtypeStruct((3,) + s, BF) for s in shapes()],
        scratch_shapes=[pltpu.SemaphoreType.DMA((3 * n,)), pltpu.SemaphoreType.DMA((3 * n,))],
        compiler_params=pltpu.CompilerParams(has_side_effects=True),
    )(*halves)


def join_halves(shards):
    n = len(shards)

    def body(*refs):
        ins, outs = refs[:n], refs[n:2 * n]
        send_sems, recv_sems = refs[2 * n:]
        x, y, c = _place()
        cps = []
        for t, (_, _, _, ax) in enumerate(SHARDED):
            cp = pltpu.make_async_remote_copy(
                src_ref=_half_of(ins[t], ax, c), dst_ref=_half_of(outs[t], ax, c), send_sem=send_sems.at[t],
                recv_sem=recv_sems.at[t], device_id=(x, y, 1 - c), device_id_type=MESH)
            cp.start()
            cps.append(cp)
        for t, cp in enumerate(cps):
            cp.wait_send()
            theirs = _half_of(outs[t], SHARDED[t][3], 1 - c)
            pltpu.make_async_remote_copy(
                src_ref=theirs, dst_ref=theirs, send_sem=send_sems.at[t], recv_sem=recv_sems.at[t],
                device_id=(x, y, 1 - c), device_id_type=MESH).wait_recv()

    return pl.pallas_call(
        body, name="join_halves",
        in_specs=[ANY] * n, out_specs=[ANY] * n,
        out_shape=[jax.ShapeDtypeStruct(s.shape, F32) for s in shards],
        input_output_aliases={t: t for t in range(n)},
        scratch_shapes=[pltpu.SemaphoreType.DMA((n,)), pltpu.SemaphoreType.DMA((n,))],
        compiler_params=pltpu.CompilerParams(has_side_effects=True),
    )(*shards)


def allreduce_small(pack):
    rows = pack.shape[0]
    hr = rows // 2

    def body(p_ref, o_ref, sib_ref, parts_ref, send_sems, recv_sems):
        x, y, c = _place()
        mine = 2 * x + y
        sibling = (x, y, 1 - c)
        swap = pltpu.make_async_remote_copy(
            src_ref=p_ref, dst_ref=sib_ref, send_sem=send_sems.at[0], recv_sem=recv_sems.at[0],
            device_id=sibling, device_id_type=MESH)
        swap.start()
        swap.wait()
        half = pl.ds(pl.multiple_of(c * hr, 8), hr)
        parts_ref[mine] = p_ref[half, :] + sib_ref[half, :]
        cps = []
        for j, chip in enumerate(_other_chips(x, y)):
            cp = pltpu.make_async_remote_copy(
                src_ref=parts_ref.at[mine], dst_ref=parts_ref.at[mine], send_sem=send_sems.at[1 + j],
                recv_sem=recv_sems.at[1 + j], device_id=(*chip, c), device_id_type=MESH)
            cp.start()
            cps.append(cp)
        for j, (px, py) in enumerate(_other_chips(x, y)):
            cps[j].wait_send()
            theirs = parts_ref.at[2 * px + py]
            pltpu.make_async_remote_copy(
                src_ref=theirs, dst_ref=theirs, send_sem=send_sems.at[1 + j], recv_sem=recv_sems.at[1 + j],
                device_id=(px, py, c), device_id_type=MESH).wait_recv()
        o_ref[half, :] = (parts_ref[0] + parts_ref[1]) + (parts_ref[2] + parts_ref[3])
        back = pltpu.make_async_remote_copy(
            src_ref=o_ref.at[half, :], dst_ref=o_ref.at[half, :], send_sem=send_sems.at[4],
            recv_sem=recv_sems.at[4], device_id=sibling, device_id_type=MESH)
        back.start()
        back.wait_send()
        other = pl.ds(pl.multiple_of((1 - c) * hr, 8), hr)
        pltpu.make_async_remote_copy(
            src_ref=o_ref.at[other, :], dst_ref=o_ref.at[other, :], send_sem=send_sems.at[4],
            recv_sem=recv_sems.at[4], device_id=sibling, device_id_type=MESH).wait_recv()

    return pl.pallas_call(
        body, name="allreduce_small",
        in_specs=[pl.BlockSpec(memory_space=pltpu.VMEM)], out_specs=pl.BlockSpec(memory_space=pltpu.VMEM),
        out_shape=jax.ShapeDtypeStruct((rows, 128), F32),
        scratch_shapes=[pltpu.VMEM((rows, 128), F32), pltpu.VMEM((4, hr, 128), F32),
                        pltpu.SemaphoreType.DMA((5,)), pltpu.SemaphoreType.DMA((5,))],
        compiler_params=pltpu.CompilerParams(has_side_effects=True, vmem_limit_bytes=40 << 20),
    )(pack)


def _ssm_discretize(lam_re, lam_im, log_dt, b_re, b_im):
    dt = jnp.exp(log_dt)[:, None]
    mag = jnp.exp(lam_re * dt)
    ar, ai = mag * jnp.cos(lam_im * dt), mag * jnp.sin(lam_im * dt)
    den = lam_re * lam_re + lam_im * lam_im
    nr, ni = ar - 1.0, ai
    fr, fi = (nr * lam_re + ni * lam_im) / den, (ni * lam_re - nr * lam_im) / den
    bbr = fr[..., None] * b_re - fi[..., None] * b_im
    bbi = fr[..., None] * b_im + fi[..., None] * b_re
    return ar, ai, bbr, bbi


def _cmul(a, b):
    return a[0] * b[0] - a[1] * b[1], a[0] * b[1] + a[1] * b[0]


def _scan_powers(ar, ai):
    a = (ar.reshape(1, SW), ai.reshape(1, SW))
    pows = [a]
    for _ in range(7):
        pows.append(_cmul(pows[-1], a))
    fr = jnp.concatenate([p[0] for p in pows], axis=0)
    fi = jnp.concatenate([p[1] for p in pows], axis=0)
    rid = jnp.arange(8)[:, None]
    fwd, rev = [], []
    for k in (1, 2, 4):
        pr, pi = pows[k - 1]
        fwd += [jnp.where(rid >= k, pr, 0.0), jnp.where(rid >= k, pi, 0.0)]
        rev += [jnp.where(rid < 8 - k, pr, 0.0), jnp.where(rid < 8 - k, -pi, 0.0)]
    return jnp.concatenate(fwd + [fr, fi] + rev + [fr[::-1], -fi[::-1]], axis=0)


def _pack_b(bb):
    t = bb.reshape(NLC, 8, NSTATE, GCH).transpose(0, 1, 3, 2)
    return jnp.einsum("jgcp,gh->jgchp", t, jnp.eye(8, dtype=bb.dtype)).reshape(NLC, 128, LC)


def _unpack_b(db):
    t = jnp.einsum("jgchp,gh->jgcp", db.reshape(NLC, 8, GCH, 8, NSTATE), jnp.eye(8, dtype=db.dtype))
    return t.transpose(0, 1, 3, 2).reshape(GROUPS, NSTATE, GCH)


def _pack_c(cc):
    t = cc.reshape(NLC, 8, GCH, NSTATE).transpose(0, 1, 3, 2)
    return jnp.einsum("jgpc,gh->jgphc", t, jnp.eye(8, dtype=cc.dtype)).reshape(NLC, LC, 128)


def _unpack_c(dc):
    t = jnp.einsum("jgphc,gh->jgpc", dc.reshape(NLC, 8, NSTATE, 8, GCH), jnp.eye(8, dtype=dc.dtype))
    return t.transpose(0, 1, 3, 2).reshape(GROUPS, GCH, NSTATE)


SMALL = ("norm_mix", "q_norm", "k_norm", "attn_sinks", "lam_re", "lam_im", "log_dt", "ssm_b_re", "ssm_b_im",
         "ssm_c_re", "ssm_c_im", "ssm_d", "attn_branch_norm", "ssm_branch_norm", "norm_ffn")


def _pack_small(arrs, rows=None):
    flat = jnp.concatenate([a.reshape(-1).astype(F32) for a in arrs])
    n = flat.shape[0]
    if rows is None:
        rows = -(-n // (128 * 256)) * 256
    return jnp.pad(flat, (0, rows * 128 - n)).reshape(rows, 128)


def _unpack_small(pack, like):
    flat = pack.reshape(-1)
    out, off = [], 0
    for a in like:
        out.append(flat[off:off + a.size].reshape(a.shape))
        off += a.size
    return out


def local_step(x, tgt, meta_full, p, w):
    mpad = jnp.concatenate([jnp.zeros((META0, D), F32), meta_full], axis=0)
    qn_t = jnp.tile(p["q_norm"].reshape(1, HEAD), (1, NQ))
    kn_t = jnp.tile(p["k_norm"].reshape(1, HEAD), (1, NKV))
    seg = jnp.arange(256) // HEAD
    bd = (seg[:, None] == seg[None, :]).astype(BF)
    nm = p["norm_mix"].reshape(1, D)
    sinks = p["attn_sinks"].reshape(NQ)
    dsk = p["ssm_d"].reshape(1, D)
    abn, sbn, gf = (p[k].reshape(1, D) for k in ("attn_branch_norm", "ssm_branch_norm", "norm_ffn"))

    disc_in = (p["lam_re"], p["lam_im"], p["log_dt"], p["ssm_b_re"], p["ssm_b_im"])
    (ar, ai, bbr, bbi), disc_vjp = jax.vjp(_ssm_discretize, *disc_in)
    pw = _scan_powers(ar, ai)
    brp, bip = _pack_b(bbr).astype(BF), _pack_b(bbi).astype(BF)
    crp, cip = _pack_c(p["ssm_c_re"]).astype(BF), _pack_c(p["ssm_c_im"]).astype(BF)

    xn, qr, kr, qn, kn, v, u, gg = in_proj_fwd(x, mpad, w["w_in"], nm, qn_t, kn_t, bd)
    attn = attention_fwd(sinks, qn, kn, v)
    y, sr, si = ssm_fwd(u, pw, brp, bip, crp, cip, dsk)
    zb, zzb, mgb, h1 = mid_fwd(y, attn, gg, x, mpad, w["w_glu"], w["w_out"], abn, sbn)
    hnb, dgub, actb, dh2b, dh1, dh1b, dgf, loss = ffn_fwd_bwd(h1, tgt, w["w_ffn_in"], w["w_ffn_out"], gf)
    dy, dattn, dggb, dzzb, dabn, dsbn = mid_bwd(dh1b, y, attn, gg, zzb, w["w_glu"], w["w_out"], abn, sbn)
    dub, da, dbr, dbi, dcr, dci, dd = ssm_bwd(dy, u, sr, si, pw, brp, bip, crp, cip, dsk)
    dqn, dkn, dv, dsink = attention_bwd(sinks, qn, kn, v, dattn)
    dproj, dx, dmpad, dnm, dqw, dkw = in_proj_bwd(dqn, dkn, dv, dub, dggb, qr, kr, x, mpad, dh1, w["w_in"], nm, qn_t,
                                                  kn_t, bd)

    big = [weight_grad("grad_w_in", xn, dproj, 1152), weight_grad("grad_w_glu", zb, dzzb, 1024),
           weight_grad("grad_w_out", mgb, dh1b, 1024), weight_grad("grad_w_ffn_in", hnb, dgub, 1408),
           weight_grad("grad_w_ffn_out", actb, dh2b, 512)]

    dar = jnp.sum(da[0:8], axis=0).reshape(GROUPS, NSTATE)
    dai = jnp.sum(da[8:16], axis=0).reshape(GROUPS, NSTATE)
    dlr, dli, dldt, dbre, dbim = disc_vjp((dar, dai, _unpack_b(dbr), _unpack_b(dbi)))
    small = {
        "norm_mix": dnm, "q_norm": dqw.reshape(NQ, HEAD).sum(0), "k_norm": dkw.reshape(NKV, HEAD).sum(0),
        "attn_sinks": dsink[0, 0:NQ], "lam_re": dlr, "lam_im": dli, "log_dt": dldt,
        "ssm_b_re": dbre, "ssm_b_im": dbim, "ssm_c_re": _unpack_c(dcr), "ssm_c_im": _unpack_c(dci),
        "ssm_d": dd, "attn_branch_norm": dabn, "ssm_branch_norm": dsbn, "norm_ffn": dgf,
    }
    return loss[0, 0], dx, dmpad[META0:PAD], small, big


def kernel(x, meta_tokens, norm_mix, w_in, q_norm, k_norm, attn_sinks, lam_re, lam_im, log_dt, ssm_b_re, ssm_b_im, ssm_c_re, ssm_c_im, ssm_d, w_glu, attn_branch_norm, ssm_branch_norm, w_out, norm_ffn, w_ffn_in, w_ffn_out, loss_target, m_meta_tokens, m_norm_mix, m_w_in, m_q_norm, m_k_norm, m_attn_sinks, m_lam_re, m_lam_im, m_log_dt, m_ssm_b_re, m_ssm_b_im, m_ssm_c_re, m_ssm_c_im, m_ssm_d, m_w_glu, m_attn_branch_norm, m_ssm_branch_norm, m_w_out, m_norm_ffn, m_w_ffn_in, m_w_ffn_out, v_meta_tokens, v_norm_mix, v_w_in, v_q_norm, v_k_norm, v_attn_sinks, v_lam_re, v_lam_im, v_log_dt, v_ssm_b_re, v_ssm_b_im, v_ssm_c_re, v_ssm_c_im, v_ssm_d, v_w_glu, v_attn_branch_norm, v_ssm_branch_norm, v_w_out, v_norm_ffn, v_w_ffn_in, v_w_ffn_out):
    args = dict(locals())
    names = ["meta_tokens", "norm_mix", "w_in", "q_norm", "k_norm", "attn_sinks", "lam_re", "lam_im", "log_dt",
             "ssm_b_re", "ssm_b_im", "ssm_c_re", "ssm_c_im", "ssm_d", "w_glu", "attn_branch_norm",
             "ssm_branch_norm", "w_out", "norm_ffn", "w_ffn_in", "w_ffn_out"]
    big_names = [s[0] for s in SHARDED]
    xs, ys, cs = _place()
    chip = 2 * xs + ys

    shards = [args[n][0].astype(BF) for n in big_names]
    *full, meta_full = gather_weights(shards, meta_tokens)
    w = dict(zip(big_names, full))

    p = {n: args[n][0] for n in SMALL}
    loss, dx, dmeta, gsmall, gbig = local_step(x[0], loss_target[0], meta_full, p, w)
    loss = lax.psum(loss, ("x", "y", "c"))

    place = jnp.stack([chip, cs]).astype(jnp.int32)
    axes = [s[3] for s in SHARDED]
    got = pair_exchange(gbig)
    halves = [pair_sum("pair_sum_" + n, place, g, b, ax) for n, g, b, ax in zip(big_names, gbig, got, axes)]
    parts = scatter_grads(halves)
    mine = [sum4("sum4_" + n, place, pt, h, ax) for n, pt, h, ax in zip(big_names, parts, halves, axes)]
    gfull = join_halves(mine)
    small_list = [gsmall[n].reshape(args[n].shape) for n in SMALL] + [dmeta]
    red = allreduce_small(_pack_small(small_list))
    small_red = _unpack_small(red, small_list)
    gmeta = lax.dynamic_slice_in_dim(small_red[-1], chip * (D // 4), D // 4, axis=1)

    out = {}
    for n, g in zip(big_names, gfull):
        d, nm_, nv_ = adamw("adamw_" + n, args[n][0], g, args["m_" + n][0], args["v_" + n][0])
        out[n] = tuple(t[None] for t in (g, d, nm_, nv_))
    d, nm_, nv_ = adamw("adamw_meta", meta_tokens, gmeta, m_meta_tokens, v_meta_tokens)
    out["meta_tokens"] = (gmeta, d, nm_, nv_)
    ds, nms, nvs = adamw_small([args[n] for n in SMALL], small_red[:-1], [args["m_" + n] for n in SMALL],
                               [args["v_" + n] for n in SMALL])
    for n, g_, d_, m_, v_ in zip(SMALL, small_red[:-1], ds, nms, nvs):
        out[n] = (g_, d_, m_, v_)

    res = [loss, dx[None]]
    for k in range(4):
        res += [out[n][k] for n in names]
    return tuple(res)
```

```python
import functools
import math

import jax
import jax.numpy as jnp
from jax import lax
from jax.experimental import pallas as pl
from jax.experimental.pallas import tpu as pltpu

F32 = jnp.float32
BF = jnp.bfloat16

D = 1024
N_META = 16
HEAD = 64
NQ = 16
NKV = 4
QW = NQ * HEAD
KVW = NKV * HEAD
WIN = 128
GROUPS = 64
GCH = 16
NSTATE = 64
SW = GROUPS * NSTATE
DFF = 2816
IN_COLS = QW + 2 * KVW + D + 2 * D
PAD = 256
META0 = PAD - N_META
EPS = 1e-6
NEG = -1e30

TM = 256
TS = 128
LC = 512
NLC = SW // LC

LR, B1, B2, AEPS, WD, STEP = 0.001, 0.9, 0.999, 1e-08, 0.01, 10
BC1 = 1.0 - B1 ** STEP
BC2 = 1.0 - B2 ** STEP

MESH = pl.DeviceIdType.MESH
ANY = pl.BlockSpec(memory_space=pl.ANY)


def _cp(sem=None, vmem_mb=48):
    return pltpu.CompilerParams(dimension_semantics=sem, vmem_limit_bytes=vmem_mb << 20)


def _full(shape):
    n = len(shape)
    return pl.BlockSpec(shape, lambda *a: (0,) * n)


def _const(shape):
    n = len(shape)
    return pl.BlockSpec(shape, lambda *a: (0,) * n, pipeline_mode=pl.Buffered(1))


def _rows(tm, width):
    return pl.BlockSpec((tm, width), lambda i: (i, 0))


def _dot(a, b):
    return jnp.dot(a, b, preferred_element_type=F32)


def _dot_nt(a, b):
    return lax.dot_general(a, b, (((1,), (1,)), ((), ())), preferred_element_type=F32)


def _dot_tn(a, b):
    return lax.dot_general(a, b, (((0,), (0,)), ((), ())), preferred_element_type=F32)


def _sigmoid(x):
    return 0.5 * jnp.tanh(0.5 * x) + 0.5


def _seg_mean(x, bd):
    outs = []
    for j in range(x.shape[1] // 256):
        c = x[:, 256 * j:256 * (j + 1)]
        hi = c.astype(BF)
        lo = (c - hi.astype(F32)).astype(BF)
        outs.append(_dot(hi, bd) + _dot(lo, bd))
    out = outs[0] if len(outs) == 1 else jnp.concatenate(outs, axis=1)
    return out * (1.0 / HEAD)


_GC = math.sqrt(2.0 / math.pi)


def _gelu(x):
    t = jnp.tanh(_GC * (x + 0.044715 * x * x * x))
    return 0.5 * x * (1.0 + t), t


def _gelu_grad(x, t):
    return 0.5 * (1.0 + t) + 0.5 * x * (1.0 - t * t) * _GC * (1.0 + 3.0 * 0.044715 * x * x)


def _seq_specs():
    assert TM == PAD
    return [pl.BlockSpec((TM, D), lambda i: (jnp.maximum(i - 1, 0), 0)), _full((PAD, D))]


def _seq_tile(x_ref, m_ref):
    return jnp.where(pl.program_id(0) == 0, m_ref[...], x_ref[...])


def in_proj_fwd(x, mpad, w_in, nm, qn_t, kn_t, bd):
    lp = x.shape[0] + PAD

    def body(x_ref, m_ref, w_ref, nm_ref, qn_ref, kn_ref, bd_ref,
             xn_o, qr_o, kr_o, qn_o, kn_o, v_o, u_o, gg_o):
        h = _seq_tile(x_ref, m_ref)
        r = lax.rsqrt(jnp.mean(h * h, axis=-1, keepdims=True) + EPS)
        xb = ((h * r) * nm_ref[...]).astype(BF)
        xn_o[...] = xb
        bdv = bd_ref[...]
        q = _dot(xb, w_ref[:, 0:QW])
        qr_o[...] = q
        qn_o[...] = (q * lax.rsqrt(_seg_mean(q * q, bdv) + EPS) * qn_ref[...]).astype(BF)
        k = _dot(xb, w_ref[:, QW:QW + KVW])
        kr_o[...] = k
        kn_o[...] = (k * lax.rsqrt(_seg_mean(k * k, bdv) + EPS) * kn_ref[...]).astype(BF)
        v_o[...] = _dot(xb, w_ref[:, QW + KVW:QW + 2 * KVW]).astype(BF)
        u_o[...] = _dot(xb, w_ref[:, QW + 2 * KVW:QW + 2 * KVW + D])
        gg_o[...] = _dot(xb, w_ref[:, QW + 2 * KVW + D:IN_COLS])

    outs = [(D, BF), (QW, F32), (KVW, F32), (QW, BF), (KVW, BF), (KVW, BF), (D, F32), (2 * D, F32)]
    return pl.pallas_call(
        body, name="in_proj_fwd", grid=(lp // TM,),
        in_specs=_seq_specs() + [_full((D, IN_COLS)), _full((1, D)), _full((1, QW)), _full((1, KVW)),
                                 _full((256, 256))],
        out_specs=[_rows(TM, w) for w, _ in outs],
        out_shape=[jax.ShapeDtypeStruct((lp, w), t) for w, t in outs],
        compiler_params=_cp(("arbitrary",)),
    )(x, mpad, w_in, nm, qn_t, kn_t, bd)


def in_proj_bwd(dqn, dkn, dv, du, dgg, qr, kr, x, mpad, dh1, w_in, nm, qn_t, kn_t, bd):
    lp = x.shape[0] + PAD

    def body(dqn_ref, dkn_ref, dv_ref, du_ref, dgg_ref, qr_ref, kr_ref, x_ref, m_ref, dh1_ref,
             w_ref, nm_ref, qn_ref, kn_ref, bd_ref,
             dproj_o, dx_o, dm_o, dnm_o, dqw_o, dkw_o):
        i = pl.program_id(0)

        @pl.when(i == 0)
        def _():
            dnm_o[...] = jnp.zeros_like(dnm_o)
            dqw_o[...] = jnp.zeros_like(dqw_o)
            dkw_o[...] = jnp.zeros_like(dkw_o)

        bdv = bd_ref[...]

        def norm_bwd(x, dy, g, acc_ref):
            rr = lax.rsqrt(_seg_mean(x * x, bdv) + EPS)
            xh = x * rr
            acc_ref[...] += jnp.sum(dy * xh, axis=0, keepdims=True)
            dxh = dy * g
            return rr * (dxh - xh * _seg_mean(dxh * xh, bdv))

        dq = norm_bwd(qr_ref[...], dqn_ref[...], qn_ref[...], dqw_o)
        dk = norm_bwd(kr_ref[...], dkn_ref[...], kn_ref[...], dkw_o)
        dproj_o[:, 0:QW] = dq.astype(BF)
        dproj_o[:, QW:QW + KVW] = dk.astype(BF)
        dproj_o[:, QW + KVW:QW + 2 * KVW] = dv_ref[...].astype(BF)
        dproj_o[:, QW + 2 * KVW:QW + 2 * KVW + D] = du_ref[...]
        dproj_o[:, QW + 2 * KVW + D:IN_COLS] = dgg_ref[...]
        dxn = jnp.zeros((TM, D), F32)
        for c0 in range(0, IN_COLS, 512):
            dxn += _dot_nt(dproj_o[:, c0:c0 + 512], w_ref[:, c0:c0 + 512])
        h = _seq_tile(x_ref, m_ref)
        r = lax.rsqrt(jnp.mean(h * h, axis=-1, keepdims=True) + EPS)
        xh = h * r
        dnm_o[...] += jnp.sum(dxn * xh, axis=0, keepdims=True)
        dxh = dxn * nm_ref[...]
        dh0 = dh1_ref[...] + r * (dxh - xh * jnp.mean(dxh * xh, axis=-1, keepdims=True))
        dx_o[...] = dh0

        @pl.when(i == 0)
        def _():
            dm_o[...] = dh0

    return pl.pallas_call(
        body, name="in_proj_bwd", grid=(lp // TM,),
        in_specs=[_rows(TM, QW), _rows(TM, KVW), _rows(TM, KVW), _rows(TM, D), _rows(TM, 2 * D),
                  _rows(TM, QW), _rows(TM, KVW)] + _seq_specs() + [_rows(TM, D),
                  _full((D, IN_COLS)), _full((1, D)), _full((1, QW)), _full((1, KVW)), _full((256, 256))],
        out_specs=[_rows(TM, IN_COLS), _seq_specs()[0], _full((PAD, D)), _full((1, D)), _full((1, QW)),
                   _full((1, KVW))],
        out_shape=[jax.ShapeDtypeStruct((lp, IN_COLS), BF), jax.ShapeDtypeStruct((lp - PAD, D), F32),
                   jax.ShapeDtypeStruct((PAD, D), F32),
                   jax.ShapeDtypeStruct((1, D), F32), jax.ShapeDtypeStruct((1, QW), F32),
                   jax.ShapeDtypeStruct((1, KVW), F32)],
        compiler_params=_cp(("arbitrary",)),
    )(dqn, dkn, dv, du, dgg, qr, kr, x, mpad, dh1, w_in, nm, qn_t, kn_t, bd)


def _att_masks(b):
    qi = lax.broadcasted_iota(jnp.int32, (4 * WIN, WIN), 0) & (WIN - 1)
    kj = lax.broadcasted_iota(jnp.int32, (4 * WIN, WIN), 1)
    mask_c = (kj <= qi) & (b * WIN + kj >= META0)
    mask_p = (kj > qi) & ((b - 1) * WIN + kj >= PAD)
    mask_m = (kj >= WIN - N_META) & (b * WIN + qi >= PAD)
    return mask_m, mask_p, mask_c


def _dup_half(ref, kp, pos, lo):
    pair = ref[:, 128 * kp:128 * kp + 128].astype(F32)
    rolled = pltpu.roll(pair, 64, axis=1)
    keep = lo if pos == 0 else jnp.logical_not(lo)
    return jnp.where(keep, pair, rolled).astype(BF)


def _stack_heads(ref, kv, lo):
    parts = []
    for pp in range(2):
        pair = ref[:, 256 * kv + 128 * pp:256 * kv + 128 * pp + 128]
        zero = jnp.zeros_like(pair)
        parts.append(jnp.where(lo, pair, zero))
        parts.append(jnp.where(lo, zero, pair))
    return jnp.concatenate(parts, axis=0)


def _unstack_heads(x4, lo):
    a = jnp.where(lo, x4[0:WIN], x4[WIN:2 * WIN])
    b = jnp.where(lo, x4[2 * WIN:3 * WIN], x4[3 * WIN:4 * WIN])
    return a, b


def _sink_col(sink_ref, kv):
    row = lax.broadcasted_iota(jnp.int32, (4 * WIN, 1), 0) >> 7
    col = jnp.full((4 * WIN, 1), sink_ref[4 * kv + 3], F32)
    for r in range(3):
        col = jnp.where(row == r, sink_ref[4 * kv + r], col)
    return col


def _att_probs(q4, kds, masks, sink4):
    ss = [jnp.where(mk, _dot_nt(q4, kd), NEG) for kd, mk in zip(kds, masks)]
    m = jnp.maximum(sink4, jnp.max(jnp.maximum(jnp.maximum(ss[0], ss[1]), ss[2]), axis=-1, keepdims=True))
    es = [jnp.exp(s - m) for s in ss]
    esink = jnp.exp(sink4 - m)
    den = esink + jnp.sum(es[0] + es[1] + es[2], axis=-1, keepdims=True)
    inv = 1.0 / den
    return [e * inv for e in es], esink * inv


def _kv_specs():
    return [pl.BlockSpec((WIN, KVW), lambda b: (1, 0)),
            pl.BlockSpec((WIN, KVW), lambda b: (jnp.maximum(b - 1, 0), 0)),
            pl.BlockSpec((WIN, KVW), lambda b: (b, 0))]


def attention_fwd(sinks, qn, kn, v):
    lp = qn.shape[0]

    def body(sink_ref, q_ref, km_ref, kp_ref, kc_ref, vm_ref, vp_ref, vc_ref, o_ref):
        b = pl.program_id(0)
        masks = _att_masks(b)
        lo = lax.broadcasted_iota(jnp.int32, (WIN, WIN), 1) < HEAD
        for kv in range(NKV):
            kp, pos = kv // 2, kv % 2
            kds = [_dup_half(r, kp, pos, lo) for r in (km_ref, kp_ref, kc_ref)]
            vds = [_dup_half(r, kp, pos, lo) for r in (vm_ref, vp_ref, vc_ref)]
            q4 = _stack_heads(q_ref, kv, lo)
            ps, _ = _att_probs(q4, kds, masks, _sink_col(sink_ref, kv))
            o4 = _dot(ps[0].astype(BF), vds[0]) + _dot(ps[1].astype(BF), vds[1]) + _dot(ps[2].astype(BF), vds[2])
            oa, ob = _unstack_heads(o4, lo)
            o_ref[:, 256 * kv:256 * kv + 128] = oa
            o_ref[:, 256 * kv + 128:256 * kv + 256] = ob

    return pl.pallas_call(
        body, name="attention_fwd", grid=(lp // WIN,),
        in_specs=[pl.BlockSpec(memory_space=pltpu.SMEM), _rows(WIN, QW)] + _kv_specs() + _kv_specs(),
        out_specs=_rows(WIN, QW),
        out_shape=jax.ShapeDtypeStruct((lp, QW), F32),
        compiler_params=_cp(("arbitrary",)),
    )(sinks, qn, kn, kn, kn, v, v, v)


def attention_bwd(sinks, qn, kn, v, do):
    lp = qn.shape[0]
    nb = lp // WIN

    def body(sink_ref, q_ref, km_ref, kp_ref, kc_ref, vm_ref, vp_ref, vc_ref, do_ref,
             dq_ref, dk_hbm, dv_hbm, dsink_ref, dk_acc, dv_acc):
        b = pl.program_id(0)

        @pl.when(b == 0)
        def _():
            dk_acc[...] = jnp.zeros_like(dk_acc)
            dv_acc[...] = jnp.zeros_like(dv_acc)
            dsink_ref[...] = jnp.zeros_like(dsink_ref)

        masks = _att_masks(b)
        lo = lax.broadcasted_iota(jnp.int32, (WIN, WIN), 1) < HEAD
        lane8 = lax.broadcasted_iota(jnp.int32, (8, 128), 1)
        row4 = lax.broadcasted_iota(jnp.int32, (4 * WIN, 1), 0) >> 7
        starts = [WIN, pl.multiple_of(jnp.maximum(b - 1, 0) * WIN, WIN), pl.multiple_of(b * WIN, WIN)]
        for kv in range(NKV):
            kp, pos = kv // 2, kv % 2
            keep = lo if pos == 0 else jnp.logical_not(lo)
            kds = [_dup_half(r, kp, pos, lo) for r in (km_ref, kp_ref, kc_ref)]
            vds = [_dup_half(r, kp, pos, lo) for r in (vm_ref, vp_ref, vc_ref)]
            q4 = _stack_heads(q_ref, kv, lo)
            do4 = _stack_heads(do_ref, kv, lo)
            ps, psink = _att_probs(q4, kds, masks, _sink_col(sink_ref, kv))
            dps = [_dot_nt(do4, vd) for vd in vds]
            delta = jnp.sum(ps[0] * dps[0] + ps[1] * dps[1] + ps[2] * dps[2], axis=-1, keepdims=True)
            dss = [(p * (dp - delta)).astype(BF) for p, dp in zip(ps, dps)]
            dsk = psink * delta
            for r in range(4):
                val = -jnp.sum(jnp.where(row4 == r, dsk, 0.0), axis=0, keepdims=True)
                dsink_ref[...] += jnp.where(lane8 == 4 * kv + r, val, 0.0)
            dq4 = _dot(dss[0], kds[0]) + _dot(dss[1], kds[1]) + _dot(dss[2], kds[2])
            dqa, dqb = _unstack_heads(dq4, lo)
            dq_ref[:, 256 * kv:256 * kv + 128] = dqa
            dq_ref[:, 256 * kv + 128:256 * kv + 256] = dqb
            for x in range(3):
                dkd = _dot_tn(dss[x], q4)
                dvd = _dot_tn(ps[x].astype(BF), do4)
                dkd = jnp.where(keep, dkd + pltpu.roll(dkd, 64, axis=1), 0.0)
                dvd = jnp.where(keep, dvd + pltpu.roll(dvd, 64, axis=1), 0.0)
                dk_acc[pl.ds(starts[x], WIN), 128 * kp:128 * kp + 128] += dkd
                dv_acc[pl.ds(starts[x], WIN), 128 * kp:128 * kp + 128] += dvd

        @pl.when(b == nb - 1)
        def _():
            pltpu.sync_copy(dk_acc, dk_hbm)
            pltpu.sync_copy(dv_acc, dv_hbm)

    return pl.pallas_call(
        body, name="attention_bwd", grid=(nb,),
        in_specs=[pl.BlockSpec(memory_space=pltpu.SMEM), _rows(WIN, QW)] + _kv_specs() + _kv_specs()
                 + [_rows(WIN, QW)],
        out_specs=[_rows(WIN, QW), ANY, ANY, _full((8, 128))],
        out_shape=[jax.ShapeDtypeStruct((lp, QW), F32), jax.ShapeDtypeStruct((lp, KVW), F32),
                   jax.ShapeDtypeStruct((lp, KVW), F32), jax.ShapeDtypeStruct((8, 128), F32)],
        scratch_shapes=[pltpu.VMEM((lp, KVW), F32), pltpu.VMEM((lp, KVW), F32)],
        compiler_params=_cp(("arbitrary",)),
    )(sinks, qn, kn, kn, kn, v, v, v, do)


PW_ROWS = 64


def _scan_tile(xr, xi, pw_ref, base, ls, reverse):
    for s, k in enumerate((1, 2, 4)):
        shift = 8 - k if reverse else k
        sr = pltpu.roll(xr, shift, axis=0)
        si = pltpu.roll(xi, shift, axis=0)
        akr = pw_ref[base + 16 * s:base + 16 * s + 8, ls]
        aki = pw_ref[base + 16 * s + 8:base + 16 * s + 16, ls]
        xr, xi = xr + akr * sr - aki * si, xi + akr * si + aki * sr
    return xr, xi


def ssm_fwd(u, pw, br, bi, cr, ci, dsk):
    lp = u.shape[0]

    def body(u_ref, pw_ref, br_ref, bi_ref, cr_ref, ci_ref, d_ref, y_ref, sr_ref, si_ref, car_r, car_i):
        i = pl.program_id(0)

        @pl.when(i == 0)
        def _():
            car_r[...] = jnp.zeros_like(car_r)
            car_i[...] = jnp.zeros_like(car_i)

        u = u_ref[...]
        ub = u.astype(BF)
        for j in range(NLC):
            uj = ub[:, 128 * j:128 * j + 128]
            sr_ref[:, LC * j:LC * j + LC] = _dot(uj, br_ref[j])
            si_ref[:, LC * j:LC * j + LC] = _dot(uj, bi_ref[j])
        for j in range(NLC):
            ls = slice(LC * j, LC * j + LC)

            def group(g, carry):
                cr_, ci_ = carry
                rows = pl.ds(pl.multiple_of(g * 8, 8), 8)
                xr, xi = _scan_tile(sr_ref[rows, ls], si_ref[rows, ls], pw_ref, 0, ls, False)
                pr, pi = pw_ref[48:56, ls], pw_ref[56:64, ls]
                xr, xi = xr + pr * cr_ - pi * ci_, xi + pr * ci_ + pi * cr_
                sr_ref[rows, ls] = xr
                si_ref[rows, ls] = xi
                return xr[7:8, :], xi[7:8, :]

            cr_, ci_ = lax.fori_loop(0, TS // 8, group, (car_r[:, ls], car_i[:, ls]))
            car_r[:, ls] = cr_
            car_i[:, ls] = ci_
        for j in range(NLC):
            ls = slice(LC * j, LC * j + LC)
            y_ref[:, 128 * j:128 * j + 128] = (_dot(sr_ref[:, ls].astype(BF), cr_ref[j])
                                               - _dot(si_ref[:, ls].astype(BF), ci_ref[j]))
        y_ref[...] += d_ref[...] * u

    return pl.pallas_call(
        body, name="ssm_fwd", grid=(lp // TS,),
        in_specs=[_rows(TS, D), _full((2 * PW_ROWS, SW)), _full((NLC, 128, LC)), _full((NLC, 128, LC)),
                  _full((NLC, LC, 128)), _full((NLC, LC, 128)), _full((1, D))],
        out_specs=[_rows(TS, D), _rows(TS, SW), _rows(TS, SW)],
        out_shape=[jax.ShapeDtypeStruct((lp, D), F32), jax.ShapeDtypeStruct((lp, SW), F32),
                   jax.ShapeDtypeStruct((lp, SW), F32)],
        scratch_shapes=[pltpu.VMEM((1, SW), F32), pltpu.VMEM((1, SW), F32)],
        compiler_params=_cp(("arbitrary",)),
    )(u, pw, br, bi, cr, ci, dsk)


def ssm_bwd(dy, u, sr, si, pw, br, bi, cr, ci, dsk):
    lp = u.shape[0]
    nt = lp // TS

    def rev(i):
        return (nt - 1 - i, 0)

    def prev8(i):
        return (jnp.maximum((nt - 1 - i) * (TS // 8) - 1, 0), 0)

    def body(dy_ref, u_ref, sr_ref, si_ref, pr8_ref, pi8_ref, pw_ref, br_ref, bi_ref, cr_ref, ci_ref, d_ref,
             du_ref, da_ref, dbr_ref, dbi_ref, dcr_ref, dci_ref, dd_ref, gr_s, gi_s, car_r, car_i):
        i = pl.program_id(0)

        @pl.when(i == 0)
        def _():
            car_r[...] = jnp.zeros_like(car_r)
            car_i[...] = jnp.zeros_like(car_i)
            for ref in (da_ref, dbr_ref, dbi_ref, dcr_ref, dci_ref, dd_ref):
                ref[...] = jnp.zeros_like(ref)

        first = (i == nt - 1).astype(F32)
        dy = dy_ref[...]
        u = u_ref[...]
        dyb = dy.astype(BF)
        ub = u.astype(BF)
        dd_ref[...] += jnp.sum(dy * u, axis=0, keepdims=True)
        for j in range(NLC):
            ls = slice(LC * j, LC * j + LC)
            dyj = dyb[:, 128 * j:128 * j + 128]
            gr_s[:, ls] = _dot_nt(dyj, cr_ref[j])
            gi_s[:, ls] = -_dot_nt(dyj, ci_ref[j])
            dcr_ref[j] += _dot_tn(sr_ref[:, ls].astype(BF), dyj)
            dci_ref[j] -= _dot_tn(si_ref[:, ls].astype(BF), dyj)
        rid = lax.broadcasted_iota(jnp.int32, (8, LC), 0)
        for j in range(NLC):
            ls = slice(LC * j, LC * j + LC)

            def one_group(rows, spr, spi, carry):
                cr_, ci_ = carry
                gr, gi = _scan_tile(gr_s[rows, ls], gi_s[rows, ls], pw_ref, PW_ROWS, ls, True)
                pr, pi = pw_ref[PW_ROWS + 48:PW_ROWS + 56, ls], pw_ref[PW_ROWS + 56:PW_ROWS + 64, ls]
                gr, gi = gr + pr * cr_ - pi * ci_, gi + pr * ci_ + pi * cr_
                gr_s[rows, ls] = gr
                gi_s[rows, ls] = gi
                s_r = jnp.where(rid >= 1, pltpu.roll(sr_ref[rows, ls], 1, axis=0), spr[7:8, :])
                s_i = jnp.where(rid >= 1, pltpu.roll(si_ref[rows, ls], 1, axis=0), spi[7:8, :])
                da_ref[0:8, ls] += gr * s_r + gi * s_i
                da_ref[8:16, ls] += gi * s_r - gr * s_i
                return gr[0:1, :], gi[0:1, :]

            def group(k, carry):
                g = TS // 8 - 1 - k
                rows = pl.ds(pl.multiple_of(g * 8, 8), 8)
                before = pl.ds(pl.multiple_of(g * 8 - 8, 8), 8)
                return one_group(rows, sr_ref[before, ls], si_ref[before, ls], carry)

            carry = lax.fori_loop(0, TS // 8 - 1, group, (car_r[:, ls], car_i[:, ls]))
            keep = 1.0 - first
            cr_, ci_ = one_group(pl.ds(0, 8), pr8_ref[:, ls] * keep, pi8_ref[:, ls] * keep, carry)
            car_r[:, ls] = cr_
            car_i[:, ls] = ci_
        for j in range(NLC):
            ls = slice(LC * j, LC * j + LC)
            uj = ub[:, 128 * j:128 * j + 128]
            grb = gr_s[:, ls].astype(BF)
            gib = gi_s[:, ls].astype(BF)
            du_ref[:, 128 * j:128 * j + 128] = (_dot_nt(grb, br_ref[j]) + _dot_nt(gib, bi_ref[j])
                                                + d_ref[:, 128 * j:128 * j + 128] * dy[:, 128 * j:128 * j + 128]
                                                ).astype(BF)
            dbr_ref[j] += _dot_tn(uj, grb)
            dbi_ref[j] += _dot_tn(uj, gib)

    return pl.pallas_call(
        body, name="ssm_bwd", grid=(nt,),
        in_specs=[pl.BlockSpec((TS, D), rev), pl.BlockSpec((TS, D), rev), pl.BlockSpec((TS, SW), rev),
                  pl.BlockSpec((TS, SW), rev), pl.BlockSpec((8, SW), prev8), pl.BlockSpec((8, SW), prev8),
                  _full((2 * PW_ROWS, SW)), _full((NLC, 128, LC)), _full((NLC, 128, LC)),
                  _full((NLC, LC, 128)), _full((NLC, LC, 128)), _full((1, D))],
        out_specs=[pl.BlockSpec((TS, D), rev), _full((16, SW)), _full((NLC, 128, LC)), _full((NLC, 128, LC)),
                   _full((NLC, LC, 128)), _full((NLC, LC, 128)), _full((1, D))],
        out_shape=[jax.ShapeDtypeStruct((lp, D), BF), jax.ShapeDtypeStruct((16, SW), F32),
                   jax.ShapeDtypeStruct((NLC, 128, LC), F32), jax.ShapeDtypeStruct((NLC, 128, LC), F32),
                   jax.ShapeDtypeStruct((NLC, LC, 128), F32), jax.ShapeDtypeStruct((NLC, LC, 128), F32),
                   jax.ShapeDtypeStruct((1, D), F32)],
        scratch_shapes=[pltpu.VMEM((TS, SW), F32), pltpu.VMEM((TS, SW), F32),
                        pltpu.VMEM((1, SW), F32), pltpu.VMEM((1, SW), F32)],
        compiler_params=_cp(("arbitrary",)),
    )(dy, u, sr, si, sr, si, pw, br, bi, cr, ci, dsk)


def _merge_parts(attn, zz, gg, abn, sbn):
    za, zb = zz[:, 0:D], zz[:, D:2 * D]
    sgz = _sigmoid(zb)
    ssm = za * sgz
    ra = lax.rsqrt(jnp.mean(attn * attn, axis=-1, keepdims=True) + EPS)
    rs = lax.rsqrt(jnp.mean(ssm * ssm, axis=-1, keepdims=True) + EPS)
    ah, sh = attn * ra, ssm * rs
    sga, sgs = _sigmoid(gg[:, 0:D]), _sigmoid(gg[:, D:2 * D])
    return za, sgz, ra, rs, ah, sh, sga, sgs, ah * abn, sh * sbn


def mid_fwd(y, attn, gg, x, mpad, w_glu, w_out, abn, sbn):
    lp = y.shape[0]

    def body(y_ref, a_ref, gg_ref, x_ref, m_ref, wg_ref, wo_ref, abn_ref, sbn_ref, z_o, zz_o, mg_o, h1_o):
        z, _ = _gelu(y_ref[...])
        zb = z.astype(BF)
        z_o[...] = zb
        zz = _dot(zb, wg_ref[...])
        zz_o[...] = zz.astype(BF)
        _, _, _, _, _, _, sga, sgs, an, sn = _merge_parts(a_ref[...], zz, gg_ref[...], abn_ref[...], sbn_ref[...])
        mb = (sga * an + sgs * sn).astype(BF)
        mg_o[...] = mb
        h1_o[...] = _seq_tile(x_ref, m_ref) + _dot(mb, wo_ref[...])

    return pl.pallas_call(
        body, name="mid_fwd", grid=(lp // TM,),
        in_specs=[_rows(TM, D), _rows(TM, D), _rows(TM, 2 * D)] + _seq_specs() + [_full((D, 2 * D)), _full((D, D)),
                  _full((1, D)), _full((1, D))],
        out_specs=[_rows(TM, D), _rows(TM, 2 * D), _rows(TM, D), _rows(TM, D)],
        out_shape=[jax.ShapeDtypeStruct((lp, D), BF), jax.ShapeDtypeStruct((lp, 2 * D), BF),
                   jax.ShapeDtypeStruct((lp, D), BF), jax.ShapeDtypeStruct((lp, D), F32)],
        compiler_params=_cp(("arbitrary",)),
    )(y, attn, gg, x, mpad, w_glu, w_out, abn, sbn)


def mid_bwd(dh1b, y, attn, gg, zzb, w_glu, w_out, abn, sbn):
    lp = y.shape[0]

    def body(dh_ref, y_ref, a_ref, gg_ref, zz_ref, wg_ref, wo_ref, abn_ref, sbn_ref,
             dy_o, dattn_o, dgg_o, dzz_o, dabn_o, dsbn_o):
        i = pl.program_id(0)

        @pl.when(i == 0)
        def _():
            dabn_o[...] = jnp.zeros_like(dabn_o)
            dsbn_o[...] = jnp.zeros_like(dsbn_o)

        dm = _dot_nt(dh_ref[...], wo_ref[...])
        abn, sbn = abn_ref[...], sbn_ref[...]
        za, sgz, ra, rs, ah, sh, sga, sgs, an, sn = _merge_parts(
            a_ref[...], zz_ref[...].astype(F32), gg_ref[...], abn, sbn)
        dgg_o[:, 0:D] = (dm * an * sga * (1.0 - sga)).astype(BF)
        dgg_o[:, D:2 * D] = (dm * sn * sgs * (1.0 - sgs)).astype(BF)
        da = dm * sga
        dabn_o[...] += jnp.sum(da * ah, axis=0, keepdims=True)
        dah = da * abn
        dattn_o[...] = (ra * (dah - ah * jnp.mean(dah * ah, axis=-1, keepdims=True))).astype(BF)
        ds = dm * sgs
        dsbn_o[...] += jnp.sum(ds * sh, axis=0, keepdims=True)
        dsh = ds * sbn
        dssm = rs * (dsh - sh * jnp.mean(dsh * sh, axis=-1, keepdims=True))
        dzz_o[:, 0:D] = (dssm * sgz).astype(BF)
        dzz_o[:, D:2 * D] = (dssm * za * sgz * (1.0 - sgz)).astype(BF)
        dz = _dot_nt(dzz_o[...], wg_ref[...])
        yv = y_ref[...]
        _, t = _gelu(yv)
        dy_o[...] = dz * _gelu_grad(yv, t)

    return pl.pallas_call(
        body, name="mid_bwd", grid=(lp // TM,),
        in_specs=[_rows(TM, D), _rows(TM, D), _rows(TM, D), _rows(TM, 2 * D), _rows(TM, 2 * D),
                  _full((D, 2 * D)), _full((D, D)), _full((1, D)), _full((1, D))],
        out_specs=[_rows(TM, D), _rows(TM, D), _rows(TM, 2 * D), _rows(TM, 2 * D), _full((1, D)), _full((1, D))],
        out_shape=[jax.ShapeDtypeStruct((lp, D), F32), jax.ShapeDtypeStruct((lp, D), BF),
                   jax.ShapeDtypeStruct((lp, 2 * D), BF), jax.ShapeDtypeStruct((lp, 2 * D), BF),
                   jax.ShapeDtypeStruct((1, D), F32), jax.ShapeDtypeStruct((1, D), F32)],
        compiler_params=_cp(("arbitrary",)),
    )(dh1b, y, attn, gg, zzb, w_glu, w_out, abn, sbn)


def ffn_fwd_bwd(h1, tgt, w_fi, w_fo, gf):
    lp = h1.shape[0]
    tf = 256
    nch = 2
    cw = DFF // nch

    def body(h_ref, t_ref, wi_ref, wo_ref, gf_ref, hn_o, dgu_o, act_o, dh2_o, dh1_o, dh1b_o, dgf_o, loss_o,
             gate_s, up_s):
        i = pl.program_id(0)

        @pl.when(i == 0)
        def _():
            dgf_o[...] = jnp.zeros_like(dgf_o)
            loss_o[...] = jnp.zeros_like(loss_o)

        h = h_ref[...]
        r = lax.rsqrt(jnp.mean(h * h, axis=-1, keepdims=True) + EPS)
        xh = h * r
        hb = (xh * gf_ref[...]).astype(BF)
        hn_o[...] = hb
        h2 = h
        for c in range(nch):
            gate = _dot(hb, wi_ref[:, cw * c:cw * (c + 1)])
            up = _dot(hb, wi_ref[:, DFF + cw * c:DFF + cw * (c + 1)])
            gate_s[:, cw * c:cw * (c + 1)] = gate
            up_s[:, cw * c:cw * (c + 1)] = up
            ab = (gate * _sigmoid(gate) * up).astype(BF)
            act_o[:, cw * c:cw * (c + 1)] = ab
            h2 = h2 + _dot(ab, wo_ref[cw * c:cw * (c + 1), :])
        real = (i >= PAD // tf).astype(F32)
        e = (h2 - t_ref[...]) * real
        loss_o[...] += 0.5 * jnp.sum(jnp.sum(e * e, axis=-1, keepdims=True), axis=0, keepdims=True) * (1.0 / D)
        dh2 = e * (1.0 / D)
        db = dh2.astype(BF)
        dh2_o[...] = db
        dhn = jnp.zeros((tf, D), F32)
        for c in range(nch):
            gate = gate_s[:, cw * c:cw * (c + 1)]
            up = up_s[:, cw * c:cw * (c + 1)]
            sg = _sigmoid(gate)
            dact = _dot_nt(db, wo_ref[cw * c:cw * (c + 1), :])
            dgate = (dact * up * (sg * (1.0 + gate * (1.0 - sg)))).astype(BF)
            dup = (dact * (gate * sg)).astype(BF)
            dgu_o[:, cw * c:cw * (c + 1)] = dgate
            dgu_o[:, DFF + cw * c:DFF + cw * (c + 1)] = dup
            dhn += (_dot_nt(dgate, wi_ref[:, cw * c:cw * (c + 1)])
                    + _dot_nt(dup, wi_ref[:, DFF + cw * c:DFF + cw * (c + 1)]))
        dgf_o[...] += jnp.sum(dhn * xh, axis=0, keepdims=True)
        dxh = dhn * gf_ref[...]
        dh1 = dh2 + r * (dxh - xh * jnp.mean(dxh * xh, axis=-1, keepdims=True))
        dh1_o[...] = dh1
        dh1b_o[...] = dh1.astype(BF)

    return pl.pallas_call(
        body, name="ffn_fwd_bwd", grid=(lp // tf,),
        in_specs=[_rows(tf, D), pl.BlockSpec((tf, D), lambda i: (jnp.maximum(i - PAD // tf, 0), 0)),
                  _const((D, 2 * DFF)), _const((DFF, D)), _full((1, D))],
        out_specs=[_rows(tf, D), _rows(tf, 2 * DFF), _rows(tf, DFF), _rows(tf, D), _rows(tf, D), _rows(tf, D),
                   _full((1, D)), _full((1, 1))],
        out_shape=[jax.ShapeDtypeStruct((lp, D), BF), jax.ShapeDtypeStruct((lp, 2 * DFF), BF),
                   jax.ShapeDtypeStruct((lp, DFF), BF), jax.ShapeDtypeStruct((lp, D), BF),
                   jax.ShapeDtypeStruct((lp, D), F32), jax.ShapeDtypeStruct((lp, D), BF),
                   jax.ShapeDtypeStruct((1, D), F32), jax.ShapeDtypeStruct((1, 1), F32)],
        scratch_shapes=[pltpu.VMEM((tf, DFF), F32), pltpu.VMEM((tf, DFF), F32)],
        compiler_params=_cp(("arbitrary",), vmem_mb=58),
    )(h1, tgt, w_fi, w_fo, gf)


def weight_grad(name, a, b, tn):
    lp, k = a.shape
    n = b.shape[1]
    tr = next(t for t in (768, 512, 384, 256, 128) if lp % t == 0)

    def body(a_ref, b_ref, o_ref):
        @pl.when(pl.program_id(1) == 0)
        def _():
            o_ref[...] = jnp.zeros_like(o_ref)

        o_ref[...] += _dot_tn(a_ref[...], b_ref[...])

    return pl.pallas_call(
        body, name=name, grid=(n // tn, lp // tr),
        in_specs=[pl.BlockSpec((tr, k), lambda j, m: (m, 0)), pl.BlockSpec((tr, tn), lambda j, m: (m, j))],
        out_specs=pl.BlockSpec((k, tn), lambda j, m: (0, j)),
        out_shape=jax.ShapeDtypeStruct((k, n), F32),
        compiler_params=_cp(("arbitrary", "arbitrary")),
    )(a, b)


def _adamw_math(w, g, m, v):
    m = B1 * m + (1.0 - B1) * g
    v = B2 * v + (1.0 - B2) * (g * g)
    delta = -LR * ((m / BC1) / (jnp.sqrt(v / BC2) + AEPS) + WD * w)
    return delta, m, v


def _row_tile(r):
    return next((t for t in (256, 128, 64, 32, 16, 8) if r % t == 0), r)


def adamw(name, w, g, m, v):
    r, c = w.shape
    tr = _row_tile(r)

    def body(w_ref, g_ref, m_ref, v_ref, d_o, m_o, v_o):
        d_o[...], m_o[...], v_o[...] = _adamw_math(w_ref[...], g_ref[...], m_ref[...], v_ref[...])

    spec = pl.BlockSpec((tr, c), lambda i: (i, 0))
    return pl.pallas_call(
        body, name=name, grid=(r // tr,), in_specs=[spec] * 4, out_specs=[spec] * 3,
        out_shape=[jax.ShapeDtypeStruct((r, c), F32)] * 3,
        compiler_params=_cp(("arbitrary",)),
    )(w, g, m, v)


def adamw_small(ws, gs, ms, vs):
    n = len(ws)
    steps = 8

    def spec(a):
        if a.ndim == 4:
            return pl.BlockSpec((1, a.shape[1] // steps) + a.shape[2:], lambda i: (0, i, 0, 0))
        nd = a.ndim
        return pl.BlockSpec(a.shape, lambda i: (0,) * nd)

    def body(*refs):
        w, g, m, v, d_o, m_o, v_o = (refs[k * n:(k + 1) * n] for k in range(7))
        for t in range(n):
            d_o[t][...], m_o[t][...], v_o[t][...] = _adamw_math(w[t][...], g[t][...], m[t][...], v[t][...])

    specs = [spec(a) for a in ws]
    res = pl.pallas_call(
        body, name="adamw_small", grid=(steps,), in_specs=specs * 4, out_specs=specs * 3,
        out_shape=[jax.ShapeDtypeStruct(a.shape, F32) for a in ws] * 3,
        compiler_params=_cp(("arbitrary",)),
    )(*ws, *gs, *ms, *vs)
    return res[:n], res[n:2 * n], res[2 * n:]


def pair_sum(name, place, full, got, ax):
    r, c = got.shape
    tr = _row_tile(r)

    def body(s_ref, a_ref, b_ref, o_ref):
        o_ref[...] = (a_ref[...] + b_ref[...]).astype(BF)

    if ax == 1:
        mine = pl.BlockSpec((tr, c), lambda i, s: (s[1] * (r // tr) + i, 0))
    else:
        mine = pl.BlockSpec((tr, c), lambda i, s: (i, s[1]))
    spec = pl.BlockSpec((tr, c), lambda i, s: (i, 0))
    return pl.pallas_call(
        body, name=name,
        grid_spec=pltpu.PrefetchScalarGridSpec(num_scalar_prefetch=1, grid=(r // tr,), in_specs=[mine, spec],
                                               out_specs=spec),
        out_shape=jax.ShapeDtypeStruct((r, c), BF),
        compiler_params=_cp(("arbitrary",)),
    )(place, full, got)


def sum4(name, place, parts, half, ax):
    _, r, c = parts.shape
    tr = _row_tile(r)

    def body(s_ref, p_ref, h_ref, o_ref):
        o_ref[...] = ((p_ref[0].astype(F32) + p_ref[1].astype(F32))
                      + (p_ref[2].astype(F32) + h_ref[...].astype(F32)))

    if ax == 1:
        own = pl.BlockSpec((tr, c), lambda i, s: (i, s[0]))
        out = pl.BlockSpec((tr, c), lambda i, s: (s[1] * (r // tr) + i, 0))
        shape = (2 * r, c)
    else:
        own = pl.BlockSpec((tr, c), lambda i, s: (s[0] * (r // tr) + i, 0))
        out = pl.BlockSpec((tr, c), lambda i, s: (i, s[1]))
        shape = (r, 2 * c)
    return pl.pallas_call(
        body, name=name,
        grid_spec=pltpu.PrefetchScalarGridSpec(
            num_scalar_prefetch=1, grid=(r // tr,),
            in_specs=[pl.BlockSpec((3, tr, c), lambda i, s: (0, i, 0)), own], out_specs=out),
        out_shape=jax.ShapeDtypeStruct(shape, F32),
        compiler_params=_cp(("arbitrary",)),
    )(place, parts, half)


def _place():
    x, y, c = lax.axis_index("x"), lax.axis_index("y"), lax.axis_index("c")
    return x, y, c


def _other_chips(x, y):
    return [(1 - x, y), (x, 1 - y), (1 - x, 1 - y)]


SHARDED = (("w_in", D, IN_COLS, 1), ("w_glu", D, 2 * D, 1), ("w_out", D, D, 0),
           ("w_ffn_in", D, 2 * DFF, 1), ("w_ffn_out", DFF, D, 0))


def _shard_of(ref, axis, k, size):
    if axis == 0:
        return ref.at[pl.ds(k * size, size), :]
    return ref.at[:, pl.ds(k * size, size)]


def gather_weights(shards, meta):
    items = [(s, r, c, ax) for s, (_, r, c, ax) in zip(shards, SHARDED)] + [(meta, N_META, D, 1)]
    n = len(items)

    def body(*refs):
        ins, outs = refs[:n], refs[n:2 * n]
        ici_send, ici_recv, d2d_send, d2d_recv, local_sems = refs[2 * n:]
        x, y, c = _place()
        mine = 2 * x + y
        chips = _other_chips(x, y)

        def piece(t, k, h):
            _, r, cc, ax = items[t]
            if ax == 0:
                return outs[t].at[pl.ds(k * (r // 4) + h * (r // 8), r // 8), :]
            return outs[t].at[pl.ds(h * (r // 2), r // 2), pl.ds(k * (cc // 4), cc // 4)]

        def src_half(t, h):
            rs = ins[t].shape[0]
            return ins[t].at[pl.ds(h * (rs // 2), rs // 2), :]

        local, first, passed = [], [], []
        for t in range(n):
            _, r, cc, ax = items[t]
            size = (r if ax == 0 else cc) // 4
            loc = pltpu.make_async_remote_copy(
                src_ref=ins[t], dst_ref=_shard_of(outs[t], ax, mine, size), send_sem=local_sems.at[t],
                recv_sem=local_sems.at[n + t], device_id=(x, y, 1 - c), device_id_type=MESH)
            loc.start()
            local.append(loc)
            for j, chip in enumerate(chips):
                cp = pltpu.make_async_remote_copy(
                    src_ref=src_half(t, c), dst_ref=piece(t, mine, c),
                    send_sem=ici_send.at[3 * t + j], recv_sem=ici_recv.at[3 * t + j],
                    device_id=(*chip, c), device_id_type=MESH)
                cp.start()
                first.append(cp)
        for t in range(n):
            for j, (px, py) in enumerate(chips):
                got = piece(t, 2 * px + py, c)
                pltpu.make_async_remote_copy(
                    src_ref=src_half(t, c), dst_ref=got, send_sem=ici_send.at[3 * t + j],
                    recv_sem=ici_recv.at[3 * t + j], device_id=(px, py, c), device_id_type=MESH).wait_recv()
                cp = pltpu.make_async_remote_copy(
                    src_ref=got, dst_ref=got, send_sem=d2d_send.at[3 * t + j], recv_sem=d2d_recv.at[3 * t + j],
                    device_id=(x, y, 1 - c), device_id_type=MESH)
                cp.start()
                passed.append(cp)
        for t in range(n):
            for j, (px, py) in enumerate(chips):
                other = piece(t, 2 * px + py, 1 - c)
                pltpu.make_async_remote_copy(
                    src_ref=other, dst_ref=other, send_sem=d2d_send.at[3 * t + j], recv_sem=d2d_recv.at[3 * t + j],
                    device_id=(x, y, 1 - c), device_id_type=MESH).wait_recv()
        for cp in first + passed:
            cp.wait_send()
        for loc in local:
            loc.wait()

    return pl.pallas_call(
        body, name="gather_weights",
        in_specs=[ANY] * n, out_specs=[ANY] * n,
        out_shape=[jax.ShapeDtypeStruct((r, c), s.dtype) for s, r, c, _ in items],
        scratch_shapes=[pltpu.SemaphoreType.DMA((3 * n,)), pltpu.SemaphoreType.DMA((3 * n,)),
                        pltpu.SemaphoreType.DMA((3 * n,)), pltpu.SemaphoreType.DMA((3 * n,)),
                        pltpu.SemaphoreType.DMA((2 * n,))],
        compiler_params=pltpu.CompilerParams(has_side_effects=True),
    )(*shards, meta)


def _half_of(ref, ax, h):
    r, c = ref.shape
    if ax == 1:
        return ref.at[pl.ds(h * (r // 2), r // 2), :]
    return ref.at[:, pl.ds(h * (c // 2), c // 2)]


def _half_shape(r, c, ax):
    return (r // 2, c) if ax == 1 else (r, c // 2)


def pair_exchange(grads):
    n = len(grads)

    def body(*refs):
        ins, got = refs[:n], refs[n:2 * n]
        send_sems, recv_sems = refs[2 * n:]
        x, y, c = _place()
        cps = []
        for t, (_, _, _, ax) in enumerate(SHARDED):
            cp = pltpu.make_async_remote_copy(
                src_ref=_half_of(ins[t], ax, 1 - c), dst_ref=got[t], send_sem=send_sems.at[t],
                recv_sem=recv_sems.at[t], device_id=(x, y, 1 - c), device_id_type=MESH)
            cp.start()
            cps.append(cp)
        for cp in cps:
            cp.wait()

    return pl.pallas_call(
        body, name="pair_exchange",
        in_specs=[ANY] * n, out_specs=[ANY] * n,
        out_shape=[jax.ShapeDtypeStruct(_half_shape(r, c, ax), F32) for _, r, c, ax in SHARDED],
        scratch_shapes=[pltpu.SemaphoreType.DMA((n,)), pltpu.SemaphoreType.DMA((n,))],
        compiler_params=pltpu.CompilerParams(has_side_effects=True),
    )(*grads)


def scatter_grads(halves):
    n = len(halves)

    def shapes():
        out = []
        for _, r, c, ax in SHARDED:
            hr, hc = _half_shape(r, c, ax)
            out.append((hr // 4, hc) if ax == 0 else (hr, hc // 4))
        return out

    def body(*refs):
        ins, outs = refs[:n], refs[n:2 * n]
        send_sems, recv_sems = refs[2 * n:]
        x, y, c = _place()
        started = []
        for t, (src, dst, (_, _, _, ax)) in enumerate(zip(ins, outs, SHARDED)):
            size = src.shape[ax] // 4
            for j, (px, py) in enumerate(_other_chips(x, y)):
                cp = pltpu.make_async_remote_copy(
                    src_ref=_shard_of(src, ax, 2 * px + py, size), dst_ref=dst.at[j],
                    send_sem=send_sems.at[3 * t + j], recv_sem=recv_sems.at[3 * t + j],
                    device_id=(px, py, c), device_id_type=MESH)
                cp.start()
                started.append(cp)
        for cp in started:
            cp.wait()

    return pl.pallas_call(
        body, name="scatter_grads",
        in_specs=[ANY] * n, out_specs=[ANY] * n,
        out_shape=[jax.ShapeDtypeStruct((3,) + s, BF) for s in shapes()],
        scratch_shapes=[pltpu.SemaphoreType.DMA((3 * n,)), pltpu.SemaphoreType.DMA((3 * n,))],
        compiler_params=pltpu.CompilerParams(has_side_effects=True),
    )(*halves)


def join_halves(shards):
    n = len(shards)

    def body(*refs):
        ins, outs = refs[:n], refs[n:2 * n]
        send_sems, recv_sems = refs[2 * n:]
        x, y, c = _place()
        cps = []
        for t, (_, _, _, ax) in enumerate(SHARDED):
            cp = pltpu.make_async_remote_copy(
                src_ref=_half_of(ins[t], ax, c), dst_ref=_half_of(outs[t], ax, c), send_sem=send_sems.at[t],
                recv_sem=recv_sems.at[t], device_id=(x, y, 1 - c), device_id_type=MESH)
            cp.start()
            cps.append(cp)
        for t, cp in enumerate(cps):
            cp.wait_send()
            theirs = _half_of(outs[t], SHARDED[t][3], 1 - c)
            pltpu.make_async_remote_copy(
                src_ref=theirs, dst_ref=theirs, send_sem=send_sems.at[t], recv_sem=recv_sems.at[t],
                device_id=(x, y, 1 - c), device_id_type=MESH).wait_recv()

    return pl.pallas_call(
        body, name="join_halves",
        in_specs=[ANY] * n, out_specs=[ANY] * n,
        out_shape=[jax.ShapeDtypeStruct(s.shape, F32) for s in shards],
        input_output_aliases={t: t for t in range(n)},
        scratch_shapes=[pltpu.SemaphoreType.DMA((n,)), pltpu.SemaphoreType.DMA((n,))],
        compiler_params=pltpu.CompilerParams(has_side_effects=True),
    )(*shards)


def allreduce_small(pack):
    rows = pack.shape[0]
    hr = rows // 2

    def body(p_ref, o_ref, sib_ref, parts_ref, send_sems, recv_sems):
        x, y, c = _place()
        mine = 2 * x + y
        sibling = (x, y, 1 - c)
        swap = pltpu.make_async_remote_copy(
            src_ref=p_ref, dst_ref=sib_ref, send_sem=send_sems.at[0], recv_sem=recv_sems.at[0],
            device_id=sibling, device_id_type=MESH)
        swap.start()
        swap.wait()
        half = pl.ds(pl.multiple_of(c * hr, 8), hr)
        parts_ref[mine] = p_ref[half, :] + sib_ref[half, :]
        cps = []
        for j, chip in enumerate(_other_chips(x, y)):
            cp = pltpu.make_async_remote_copy(
                src_ref=parts_ref.at[mine], dst_ref=parts_ref.at[mine], send_sem=send_sems.at[1 + j],
                recv_sem=recv_sems.at[1 + j], device_id=(*chip, c), device_id_type=MESH)
            cp.start()
            cps.append(cp)
        for j, (px, py) in enumerate(_other_chips(x, y)):
            cps[j].wait_send()
            theirs = parts_ref.at[2 * px + py]
            pltpu.make_async_remote_copy(
                src_ref=theirs, dst_ref=theirs, send_sem=send_sems.at[1 + j], recv_sem=recv_sems.at[1 + j],
                device_id=(px, py, c), device_id_type=MESH).wait_recv()
        o_ref[half, :] = (parts_ref[0] + parts_ref[1]) + (parts_ref[2] + parts_ref[3])
        back = pltpu.make_async_remote_copy(
            src_ref=o_ref.at[half, :], dst_ref=o_ref.at[half, :], send_sem=send_sems.at[4],
            recv_sem=recv_sems.at[4], device_id=sibling, device_id_type=MESH)
        back.start()
        back.wait_send()
        other = pl.ds(pl.multiple_of((1 - c) * hr, 8), hr)
        pltpu.make_async_remote_copy(
            src_ref=o_ref.at[other, :], dst_ref=o_ref.at[other, :], send_sem=send_sems.at[4],
            recv_sem=recv_sems.at[4], device_id=sibling, device_id_type=MESH).wait_recv()

    return pl.pallas_call(
        body, name="allreduce_small",
        in_specs=[pl.BlockSpec(memory_space=pltpu.VMEM)], out_specs=pl.BlockSpec(memory_space=pltpu.VMEM),
        out_shape=jax.ShapeDtypeStruct((rows, 128), F32),
        scratch_shapes=[pltpu.VMEM((rows, 128), F32), pltpu.VMEM((4, hr, 128), F32),
                        pltpu.SemaphoreType.DMA((5,)), pltpu.SemaphoreType.DMA((5,))],
        compiler_params=pltpu.CompilerParams(has_side_effects=True, vmem_limit_bytes=40 << 20),
    )(pack)


def _ssm_discretize(lam_re, lam_im, log_dt, b_re, b_im):
    dt = jnp.exp(log_dt)[:, None]
    mag = jnp.exp(lam_re * dt)
    ar, ai = mag * jnp.cos(lam_im * dt), mag * jnp.sin(lam_im * dt)
    den = lam_re * lam_re + lam_im * lam_im
    nr, ni = ar - 1.0, ai
    fr, fi = (nr * lam_re + ni * lam_im) / den, (ni * lam_re - nr * lam_im) / den
    bbr = fr[..., None] * b_re - fi[..., None] * b_im
    bbi = fr[..., None] * b_im + fi[..., None] * b_re
    return ar, ai, bbr, bbi


def _cmul(a, b):
    return a[0] * b[0] - a[1] * b[1], a[0] * b[1] + a[1] * b[0]


def _scan_powers(ar, ai):
    a = (ar.reshape(1, SW), ai.reshape(1, SW))
    pows = [a]
    for _ in range(7):
        pows.append(_cmul(pows[-1], a))
    fr = jnp.concatenate([p[0] for p in pows], axis=0)
    fi = jnp.concatenate([p[1] for p in pows], axis=0)
    rid = jnp.arange(8)[:, None]
    fwd, rev = [], []
    for k in (1, 2, 4):
        pr, pi = pows[k - 1]
        fwd += [jnp.where(rid >= k, pr, 0.0), jnp.where(rid >= k, pi, 0.0)]
        rev += [jnp.where(rid < 8 - k, pr, 0.0), jnp.where(rid < 8 - k, -pi, 0.0)]
    return jnp.concatenate(fwd + [fr, fi] + rev + [fr[::-1], -fi[::-1]], axis=0)


def _pack_b(bb):
    t = bb.reshape(NLC, 8, NSTATE, GCH).transpose(0, 1, 3, 2)
    return jnp.einsum("jgcp,gh->jgchp", t, jnp.eye(8, dtype=bb.dtype)).reshape(NLC, 128, LC)


def _unpack_b(db):
    t = jnp.einsum("jgchp,gh->jgcp", db.reshape(NLC, 8, GCH, 8, NSTATE), jnp.eye(8, dtype=db.dtype))
    return t.transpose(0, 1, 3, 2).reshape(GROUPS, NSTATE, GCH)


def _pack_c(cc):
    t = cc.reshape(NLC, 8, GCH, NSTATE).transpose(0, 1, 3, 2)
    return jnp.einsum("jgpc,gh->jgphc", t, jnp.eye(8, dtype=cc.dtype)).reshape(NLC, LC, 128)


def _unpack_c(dc):
    t = jnp.einsum("jgphc,gh->jgpc", dc.reshape(NLC, 8, NSTATE, 8, GCH), jnp.eye(8, dtype=dc.dtype))
    return t.transpose(0, 1, 3, 2).reshape(GROUPS, GCH, NSTATE)


SMALL = ("norm_mix", "q_norm", "k_norm", "attn_sinks", "lam_re", "lam_im", "log_dt", "ssm_b_re", "ssm_b_im",
         "ssm_c_re", "ssm_c_im", "ssm_d", "attn_branch_norm", "ssm_branch_norm", "norm_ffn")


def _pack_small(arrs, rows=None):
    flat = jnp.concatenate([a.reshape(-1).astype(F32) for a in arrs])
    n = flat.shape[0]
    if rows is None:
        rows = -(-n // (128 * 256)) * 256
    return jnp.pad(flat, (0, rows * 128 - n)).reshape(rows, 128)


def _unpack_small(pack, like):
    flat = pack.reshape(-1)
    out, off = [], 0
    for a in like:
        out.append(flat[off:off + a.size].reshape(a.shape))
        off += a.size
    return out


def local_step(x, tgt, meta_full, p, w):
    mpad = jnp.concatenate([jnp.zeros((META0, D), F32), meta_full], axis=0)
    qn_t = jnp.tile(p["q_norm"].reshape(1, HEAD), (1, NQ)) * HEAD ** -0.5
    kn_t = jnp.tile(p["k_norm"].reshape(1, HEAD), (1, NKV))
    seg = jnp.arange(256) // HEAD
    bd = (seg[:, None] == seg[None, :]).astype(BF)
    nm = p["norm_mix"].reshape(1, D)
    sinks = p["attn_sinks"].reshape(NQ)
    dsk = p["ssm_d"].reshape(1, D)
    abn, sbn, gf = (p[k].reshape(1, D) for k in ("attn_branch_norm", "ssm_branch_norm", "norm_ffn"))

    disc_in = (p["lam_re"], p["lam_im"], p["log_dt"], p["ssm_b_re"], p["ssm_b_im"])
    (ar, ai, bbr, bbi), disc_vjp = jax.vjp(_ssm_discretize, *disc_in)
    pw = _scan_powers(ar, ai)
    brp, bip = _pack_b(bbr).astype(BF), _pack_b(bbi).astype(BF)
    crp, cip = _pack_c(p["ssm_c_re"]).astype(BF), _pack_c(p["ssm_c_im"]).astype(BF)

    xn, qr, kr, qn, kn, v, u, gg = in_proj_fwd(x, mpad, w["w_in"], nm, qn_t, kn_t, bd)
    attn = attention_fwd(sinks, qn, kn, v)
    y, sr, si = ssm_fwd(u, pw, brp, bip, crp, cip, dsk)
    zb, zzb, mgb, h1 = mid_fwd(y, attn, gg, x, mpad, w["w_glu"], w["w_out"], abn, sbn)
    hnb, dgub, actb, dh2b, dh1, dh1b, dgf, loss = ffn_fwd_bwd(h1, tgt, w["w_ffn_in"], w["w_ffn_out"], gf)
    dy, dattn, dggb, dzzb, dabn, dsbn = mid_bwd(dh1b, y, attn, gg, zzb, w["w_glu"], w["w_out"], abn, sbn)
    dub, da, dbr, dbi, dcr, dci, dd = ssm_bwd(dy, u, sr, si, pw, brp, bip, crp, cip, dsk)
    dqn, dkn, dv, dsink = attention_bwd(sinks, qn, kn, v, dattn)
    dproj, dx, dmpad, dnm, dqw, dkw = in_proj_bwd(dqn, dkn, dv, dub, dggb, qr, kr, x, mpad, dh1, w["w_in"], nm, qn_t,
                                                  kn_t, bd)

    big = [weight_grad("grad_w_in", xn, dproj, 1152), weight_grad("grad_w_glu", zb, dzzb, 1024),
           weight_grad("grad_w_out", mgb, dh1b, 1024), weight_grad("grad_w_ffn_in", hnb, dgub, 1408),
           weight_grad("grad_w_ffn_out", actb, dh2b, 512)]

    dar = jnp.sum(da[0:8], axis=0).reshape(GROUPS, NSTATE)
    dai = jnp.sum(da[8:16], axis=0).reshape(GROUPS, NSTATE)
    dlr, dli, dldt, dbre, dbim = disc_vjp((dar, dai, _unpack_b(dbr), _unpack_b(dbi)))
    small = {
        "norm_mix": dnm, "q_norm": dqw.reshape(NQ, HEAD).sum(0) * HEAD ** -0.5, "k_norm": dkw.reshape(NKV, HEAD).sum(0),
        "attn_sinks": dsink[0, 0:NQ], "lam_re": dlr, "lam_im": dli, "log_dt": dldt,
        "ssm_b_re": dbre, "ssm_b_im": dbim, "ssm_c_re": _unpack_c(dcr), "ssm_c_im": _unpack_c(dci),
        "ssm_d": dd, "attn_branch_norm": dabn, "ssm_branch_norm": dsbn, "norm_ffn": dgf,
    }
    return loss[0, 0], dx, dmpad[META0:PAD], small, big


def kernel(x, meta_tokens, norm_mix, w_in, q_norm, k_norm, attn_sinks, lam_re, lam_im, log_dt, ssm_b_re, ssm_b_im, ssm_c_re, ssm_c_im, ssm_d, w_glu, attn_branch_norm, ssm_branch_norm, w_out, norm_ffn, w_ffn_in, w_ffn_out, loss_target, m_meta_tokens, m_norm_mix, m_w_in, m_q_norm, m_k_norm, m_attn_sinks, m_lam_re, m_lam_im, m_log_dt, m_ssm_b_re, m_ssm_b_im, m_ssm_c_re, m_ssm_c_im, m_ssm_d, m_w_glu, m_attn_branch_norm, m_ssm_branch_norm, m_w_out, m_norm_ffn, m_w_ffn_in, m_w_ffn_out, v_meta_tokens, v_norm_mix, v_w_in, v_q_norm, v_k_norm, v_attn_sinks, v_lam_re, v_lam_im, v_log_dt, v_ssm_b_re, v_ssm_b_im, v_ssm_c_re, v_ssm_c_im, v_ssm_d, v_w_glu, v_attn_branch_norm, v_ssm_branch_norm, v_w_out, v_norm_ffn, v_w_ffn_in, v_w_ffn_out):
    args = dict(locals())
    names = ["meta_tokens", "norm_mix", "w_in", "q_norm", "k_norm", "attn_sinks", "lam_re", "lam_im", "log_dt",
             "ssm_b_re", "ssm_b_im", "ssm_c_re", "ssm_c_im", "ssm_d", "w_glu", "attn_branch_norm",
             "ssm_branch_norm", "w_out", "norm_ffn", "w_ffn_in", "w_ffn_out"]
    big_names = [s[0] for s in SHARDED]
    xs, ys, cs = _place()
    chip = 2 * xs + ys

    shards = [args[n][0].astype(BF) for n in big_names]
    *full, meta_full = gather_weights(shards, meta_tokens)
    w = dict(zip(big_names, full))

    p = {n: args[n][0] for n in SMALL}
    loss, dx, dmeta, gsmall, gbig = local_step(x[0], loss_target[0], meta_full, p, w)
    loss = lax.psum(loss, ("x", "y", "c"))

    place = jnp.stack([chip, cs]).astype(jnp.int32)
    axes = [s[3] for s in SHARDED]
    got = pair_exchange(gbig)
    halves = [pair_sum("pair_sum_" + n, place, g, b, ax) for n, g, b, ax in zip(big_names, gbig, got, axes)]
    parts = scatter_grads(halves)
    mine = [sum4("sum4_" + n, place, pt, h, ax) for n, pt, h, ax in zip(big_names, parts, halves, axes)]
    gfull = join_halves(mine)
    small_list = [gsmall[n].reshape(args[n].shape) for n in SMALL] + [dmeta]
    red = allreduce_small(_pack_small(small_list))
    small_red = _unpack_small(red, small_list)
    gmeta = lax.dynamic_slice_in_dim(small_red[-1], chip * (D // 4), D // 4, axis=1)

    out = {}
    for n, g in zip(big_names, gfull):
        d, nm_, nv_ = adamw("adamw_" + n, args[n][0], g, args["m_" + n][0], args["v_" + n][0])
        out[n] = tuple(t[None] for t in (g, d, nm_, nv_))
    d, nm_, nv_ = adamw("adamw_meta", meta_tokens, gmeta, m_meta_tokens, v_meta_tokens)
    out["meta_tokens"] = (gmeta, d, nm_, nv_)
    ds, nms, nvs = adamw_small([args[n] for n in SMALL], small_red[:-1], [args["m_" + n] for n in SMALL],
                               [args["v_" + n] for n in SMALL])
    for n, g_, d_, m_, v_ in zip(SMALL, small_red[:-1], ds, nms, nvs):
        out[n] = (g_, d_, m_, v_)

    res = [loss, dx[None]]
    for k in range(4):
        res += [out[n][k] for n in names]
    return tuple(res)
```

```python
import functools
import math

import jax
import jax.numpy as jnp
from jax import lax
from jax.experimental import pallas as pl
from jax.experimental.pallas import tpu as pltpu

F32 = jnp.float32
BF = jnp.bfloat16

D = 1024
N_META = 16
HEAD = 64
NQ = 16
NKV = 4
QW = NQ * HEAD
KVW = NKV * HEAD
WIN = 128
GROUPS = 64
GCH = 16
NSTATE = 64
SW = GROUPS * NSTATE
DFF = 2816
IN_COLS = QW + 2 * KVW + D + 2 * D
PAD = 256
META0 = PAD - N_META
EPS = 1e-6
NEG = -1e30

TM = 256
TS = 128
LC = 512
NLC = SW // LC

LR, B1, B2, AEPS, WD, STEP = 0.001, 0.9, 0.999, 1e-08, 0.01, 10
BC1 = 1.0 - B1 ** STEP
BC2 = 1.0 - B2 ** STEP

MESH = pl.DeviceIdType.MESH
ANY = pl.BlockSpec(memory_space=pl.ANY)


def _cp(sem=None, vmem_mb=48):
    return pltpu.CompilerParams(dimension_semantics=sem, vmem_limit_bytes=vmem_mb << 20)


def _full(shape):
    n = len(shape)
    return pl.BlockSpec(shape, lambda *a: (0,) * n)


def _const(shape):
    n = len(shape)
    return pl.BlockSpec(shape, lambda *a: (0,) * n, pipeline_mode=pl.Buffered(1))


def _rows(tm, width):
    return pl.BlockSpec((tm, width), lambda i: (i, 0))


def _dot(a, b):
    return jnp.dot(a, b, preferred_element_type=F32)


def _dot_nt(a, b):
    return lax.dot_general(a, b, (((1,), (1,)), ((), ())), preferred_element_type=F32)


def _dot_tn(a, b):
    return lax.dot_general(a, b, (((0,), (0,)), ((), ())), preferred_element_type=F32)


def _sigmoid(x):
    return 0.5 * jnp.tanh(0.5 * x) + 0.5


def _seg_mean(x, bd):
    outs = []
    for j in range(x.shape[1] // 256):
        c = x[:, 256 * j:256 * (j + 1)]
        hi = c.astype(BF)
        lo = (c - hi.astype(F32)).astype(BF)
        outs.append(_dot(hi, bd) + _dot(lo, bd))
    out = outs[0] if len(outs) == 1 else jnp.concatenate(outs, axis=1)
    return out * (1.0 / HEAD)


_GC = math.sqrt(2.0 / math.pi)


def _gelu(x):
    t = jnp.tanh(_GC * (x + 0.044715 * x * x * x))
    return 0.5 * x * (1.0 + t), t


def _gelu_grad(x, t):
    return 0.5 * (1.0 + t) + 0.5 * x * (1.0 - t * t) * _GC * (1.0 + 3.0 * 0.044715 * x * x)


def _seq_specs():
    assert TM == PAD
    return [pl.BlockSpec((TM, D), lambda i: (jnp.maximum(i - 1, 0), 0)), _full((PAD, D))]


def _seq_tile(x_ref, m_ref):
    return jnp.where(pl.program_id(0) == 0, m_ref[...], x_ref[...])


def in_proj_fwd(x, mpad, w_in, nm, qn_t, kn_t, bd):
    lp = x.shape[0] + PAD

    def body(x_ref, m_ref, w_ref, nm_ref, qn_ref, kn_ref, bd_ref,
             xn_o, qr_o, kr_o, qn_o, kn_o, v_o, u_o, gg_o):
        h = _seq_tile(x_ref, m_ref)
        r = lax.rsqrt(jnp.mean(h * h, axis=-1, keepdims=True) + EPS)
        xb = ((h * r) * nm_ref[...]).astype(BF)
        xn_o[...] = xb
        bdv = bd_ref[...]
        q = _dot(xb, w_ref[:, 0:QW])
        qr_o[...] = q
        qn_o[...] = (q * lax.rsqrt(_seg_mean(q * q, bdv) + EPS) * qn_ref[...]).astype(BF)
        k = _dot(xb, w_ref[:, QW:QW + KVW])
        kr_o[...] = k
        kn_o[...] = (k * lax.rsqrt(_seg_mean(k * k, bdv) + EPS) * kn_ref[...]).astype(BF)
        v_o[...] = _dot(xb, w_ref[:, QW + KVW:QW + 2 * KVW]).astype(BF)
        u_o[...] = _dot(xb, w_ref[:, QW + 2 * KVW:QW + 2 * KVW + D])
        gg_o[...] = _dot(xb, w_ref[:, QW + 2 * KVW + D:IN_COLS])

    outs = [(D, BF), (QW, F32), (KVW, F32), (QW, BF), (KVW, BF), (KVW, BF), (D, F32), (2 * D, F32)]
    return pl.pallas_call(
        body, name="in_proj_fwd", grid=(lp // TM,),
        in_specs=_seq_specs() + [_full((D, IN_COLS)), _full((1, D)), _full((1, QW)), _full((1, KVW)),
                                 _full((256, 256))],
        out_specs=[_rows(TM, w) for w, _ in outs],
        out_shape=[jax.ShapeDtypeStruct((lp, w), t) for w, t in outs],
        compiler_params=_cp(("arbitrary",)),
    )(x, mpad, w_in, nm, qn_t, kn_t, bd)


def in_proj_bwd(dqn, dkn, dv, du, dgg, qr, kr, x, mpad, dh1, w_in, nm, qn_t, kn_t, bd):
    lp = x.shape[0] + PAD

    def body(dqn_ref, dkn_ref, dv_ref, du_ref, dgg_ref, qr_ref, kr_ref, x_ref, m_ref, dh1_ref,
             w_ref, nm_ref, qn_ref, kn_ref, bd_ref,
             dproj_o, dx_o, dm_o, dnm_o, dqw_o, dkw_o):
        i = pl.program_id(0)

        @pl.when(i == 0)
        def _():
            dnm_o[...] = jnp.zeros_like(dnm_o)
            dqw_o[...] = jnp.zeros_like(dqw_o)
            dkw_o[...] = jnp.zeros_like(dkw_o)

        bdv = bd_ref[...]

        def norm_bwd(x, dy, g, acc_ref):
            rr = lax.rsqrt(_seg_mean(x * x, bdv) + EPS)
            xh = x * rr
            acc_ref[...] += jnp.sum(dy * xh, axis=0, keepdims=True)
            dxh = dy * g
            return rr * (dxh - xh * _seg_mean(dxh * xh, bdv))

        dq = norm_bwd(qr_ref[...], dqn_ref[...], qn_ref[...], dqw_o)
        dk = norm_bwd(kr_ref[...], dkn_ref[...], kn_ref[...], dkw_o)
        dproj_o[:, 0:QW] = dq.astype(BF)
        dproj_o[:, QW:QW + KVW] = dk.astype(BF)
        dproj_o[:, QW + KVW:QW + 2 * KVW] = dv_ref[...].astype(BF)
        dproj_o[:, QW + 2 * KVW:QW + 2 * KVW + D] = du_ref[...]
        dproj_o[:, QW + 2 * KVW + D:IN_COLS] = dgg_ref[...]
        dxn = jnp.zeros((TM, D), F32)
        for c0 in range(0, IN_COLS, 512):
            dxn += _dot_nt(dproj_o[:, c0:c0 + 512], w_ref[:, c0:c0 + 512])
        h = _seq_tile(x_ref, m_ref)
        r = lax.rsqrt(jnp.mean(h * h, axis=-1, keepdims=True) + EPS)
        xh = h * r
        dnm_o[...] += jnp.sum(dxn * xh, axis=0, keepdims=True)
        dxh = dxn * nm_ref[...]
        dh0 = dh1_ref[...] + r * (dxh - xh * jnp.mean(dxh * xh, axis=-1, keepdims=True))
        dx_o[...] = dh0

        @pl.when(i == 0)
        def _():
            dm_o[...] = dh0

    return pl.pallas_call(
        body, name="in_proj_bwd", grid=(lp // TM,),
        in_specs=[_rows(TM, QW), _rows(TM, KVW), _rows(TM, KVW), _rows(TM, D), _rows(TM, 2 * D),
                  _rows(TM, QW), _rows(TM, KVW)] + _seq_specs() + [_rows(TM, D),
                  _full((D, IN_COLS)), _full((1, D)), _full((1, QW)), _full((1, KVW)), _full((256, 256))],
        out_specs=[_rows(TM, IN_COLS), _seq_specs()[0], _full((PAD, D)), _full((1, D)), _full((1, QW)),
                   _full((1, KVW))],
        out_shape=[jax.ShapeDtypeStruct((lp, IN_COLS), BF), jax.ShapeDtypeStruct((lp - PAD, D), F32),
                   jax.ShapeDtypeStruct((PAD, D), F32),
                   jax.ShapeDtypeStruct((1, D), F32), jax.ShapeDtypeStruct((1, QW), F32),
                   jax.ShapeDtypeStruct((1, KVW), F32)],
        compiler_params=_cp(("arbitrary",)),
    )(dqn, dkn, dv, du, dgg, qr, kr, x, mpad, dh1, w_in, nm, qn_t, kn_t, bd)


def _att_masks(b):
    qi = lax.broadcasted_iota(jnp.int32, (4 * WIN, WIN), 0) & (WIN - 1)
    kj = lax.broadcasted_iota(jnp.int32, (4 * WIN, WIN), 1)
    mask_c = (kj <= qi) & (b * WIN + kj >= META0)
    mask_p = (kj > qi) & ((b - 1) * WIN + kj >= PAD)
    mask_m = (kj >= WIN - N_META) & (b * WIN + qi >= PAD)
    return mask_m, mask_p, mask_c


def _dup_half(ref, kp, pos, lo):
    pair = ref[:, 128 * kp:128 * kp + 128].astype(F32)
    rolled = pltpu.roll(pair, 64, axis=1)
    keep = lo if pos == 0 else jnp.logical_not(lo)
    return jnp.where(keep, pair, rolled).astype(BF)


def _stack_heads(ref, kv, lo):
    parts = []
    for pp in range(2):
        pair = ref[:, 256 * kv + 128 * pp:256 * kv + 128 * pp + 128]
        zero = jnp.zeros_like(pair)
        parts.append(jnp.where(lo, pair, zero))
        parts.append(jnp.where(lo, zero, pair))
    return jnp.concatenate(parts, axis=0)


def _unstack_heads(x4, lo):
    a = jnp.where(lo, x4[0:WIN], x4[WIN:2 * WIN])
    b = jnp.where(lo, x4[2 * WIN:3 * WIN], x4[3 * WIN:4 * WIN])
    return a, b


def _sink_col(sink_ref, kv):
    row = lax.broadcasted_iota(jnp.int32, (4 * WIN, 1), 0) >> 7
    col = jnp.full((4 * WIN, 1), sink_ref[4 * kv + 3], F32)
    for r in range(3):
        col = jnp.where(row == r, sink_ref[4 * kv + r], col)
    return col


def _att_probs(q4, kds, masks, sink4):
    ss = [jnp.where(mk, _dot_nt(q4, kd), NEG) for kd, mk in zip(kds, masks)]
    m = jnp.maximum(sink4, jnp.max(jnp.maximum(jnp.maximum(ss[0], ss[1]), ss[2]), axis=-1, keepdims=True))
    es = [jnp.exp(s - m) for s in ss]
    esink = jnp.exp(sink4 - m)
    den = esink + jnp.sum(es[0] + es[1] + es[2], axis=-1, keepdims=True)
    inv = 1.0 / den
    return [e * inv for e in es], esink * inv


def _kv_specs():
    return [pl.BlockSpec((WIN, KVW), lambda b: (1, 0)),
            pl.BlockSpec((WIN, KVW), lambda b: (jnp.maximum(b - 1, 0), 0)),
            pl.BlockSpec((WIN, KVW), lambda b: (b, 0))]


def attention_fwd(sinks, qn, kn, v):
    lp = qn.shape[0]

    def body(sink_ref, q_ref, km_ref, kp_ref, kc_ref, vm_ref, vp_ref, vc_ref, o_ref):
        b = pl.program_id(0)
        masks = _att_masks(b)
        lo = lax.broadcasted_iota(jnp.int32, (WIN, WIN), 1) < HEAD
        for kv in range(NKV):
            kp, pos = kv // 2, kv % 2
            kds = [_dup_half(r, kp, pos, lo) for r in (km_ref, kp_ref, kc_ref)]
            vds = [_dup_half(r, kp, pos, lo) for r in (vm_ref, vp_ref, vc_ref)]
            q4 = _stack_heads(q_ref, kv, lo)
            ps, _ = _att_probs(q4, kds, masks, _sink_col(sink_ref, kv))
            o4 = _dot(ps[0].astype(BF), vds[0]) + _dot(ps[1].astype(BF), vds[1]) + _dot(ps[2].astype(BF), vds[2])
            oa, ob = _unstack_heads(o4, lo)
            o_ref[:, 256 * kv:256 * kv + 128] = oa
            o_ref[:, 256 * kv + 128:256 * kv + 256] = ob

    return pl.pallas_call(
        body, name="attention_fwd", grid=(lp // WIN,),
        in_specs=[pl.BlockSpec(memory_space=pltpu.SMEM), _rows(WIN, QW)] + _kv_specs() + _kv_specs(),
        out_specs=_rows(WIN, QW),
        out_shape=jax.ShapeDtypeStruct((lp, QW), F32),
        compiler_params=_cp(("arbitrary",)),
    )(sinks, qn, kn, kn, kn, v, v, v)


def attention_bwd(sinks, qn, kn, v, do):
    lp = qn.shape[0]
    nb = lp // WIN

    def body(sink_ref, q_ref, km_ref, kp_ref, kc_ref, vm_ref, vp_ref, vc_ref, do_ref,
             dq_ref, dk_hbm, dv_hbm, dsink_ref, dk_acc, dv_acc):
        b = pl.program_id(0)

        @pl.when(b == 0)
        def _():
            dk_acc[...] = jnp.zeros_like(dk_acc)
            dv_acc[...] = jnp.zeros_like(dv_acc)
            dsink_ref[...] = jnp.zeros_like(dsink_ref)

        masks = _att_masks(b)
        lo = lax.broadcasted_iota(jnp.int32, (WIN, WIN), 1) < HEAD
        lane8 = lax.broadcasted_iota(jnp.int32, (8, 128), 1)
        row4 = lax.broadcasted_iota(jnp.int32, (4 * WIN, 1), 0) >> 7
        starts = [WIN, pl.multiple_of(jnp.maximum(b - 1, 0) * WIN, WIN), pl.multiple_of(b * WIN, WIN)]
        for kv in range(NKV):
            kp, pos = kv // 2, kv % 2
            keep = lo if pos == 0 else jnp.logical_not(lo)
            kds = [_dup_half(r, kp, pos, lo) for r in (km_ref, kp_ref, kc_ref)]
            vds = [_dup_half(r, kp, pos, lo) for r in (vm_ref, vp_ref, vc_ref)]
            q4 = _stack_heads(q_ref, kv, lo)
            do4 = _stack_heads(do_ref, kv, lo)
            ps, psink = _att_probs(q4, kds, masks, _sink_col(sink_ref, kv))
            dps = [_dot_nt(do4, vd) for vd in vds]
            delta = jnp.sum(ps[0] * dps[0] + ps[1] * dps[1] + ps[2] * dps[2], axis=-1, keepdims=True)
            dss = [(p * (dp - delta)).astype(BF) for p, dp in zip(ps, dps)]
            dsk = psink * delta
            for r in range(4):
                val = -jnp.sum(jnp.where(row4 == r, dsk, 0.0), axis=0, keepdims=True)
                dsink_ref[...] += jnp.where(lane8 == 4 * kv + r, val, 0.0)
            dq4 = _dot(dss[0], kds[0]) + _dot(dss[1], kds[1]) + _dot(dss[2], kds[2])
            dqa, dqb = _unstack_heads(dq4, lo)
            dq_ref[:, 256 * kv:256 * kv + 128] = dqa
            dq_ref[:, 256 * kv + 128:256 * kv + 256] = dqb
            for x in range(3):
                dkd = _dot_tn(dss[x], q4)
                dvd = _dot_tn(ps[x].astype(BF), do4)
                dkd = jnp.where(keep, dkd + pltpu.roll(dkd, 64, axis=1), 0.0)
                dvd = jnp.where(keep, dvd + pltpu.roll(dvd, 64, axis=1), 0.0)
                dk_acc[pl.ds(starts[x], WIN), 128 * kp:128 * kp + 128] += dkd
                dv_acc[pl.ds(starts[x], WIN), 128 * kp:128 * kp + 128] += dvd

        @pl.when(b == nb - 1)
        def _():
            pltpu.sync_copy(dk_acc, dk_hbm)
            pltpu.sync_copy(dv_acc, dv_hbm)

    return pl.pallas_call(
        body, name="attention_bwd", grid=(nb,),
        in_specs=[pl.BlockSpec(memory_space=pltpu.SMEM), _rows(WIN, QW)] + _kv_specs() + _kv_specs()
                 + [_rows(WIN, QW)],
        out_specs=[_rows(WIN, QW), ANY, ANY, _full((8, 128))],
        out_shape=[jax.ShapeDtypeStruct((lp, QW), F32), jax.ShapeDtypeStruct((lp, KVW), F32),
                   jax.ShapeDtypeStruct((lp, KVW), F32), jax.ShapeDtypeStruct((8, 128), F32)],
        scratch_shapes=[pltpu.VMEM((lp, KVW), F32), pltpu.VMEM((lp, KVW), F32)],
        compiler_params=_cp(("arbitrary",)),
    )(sinks, qn, kn, kn, kn, v, v, v, do)


PW_ROWS = 64


def _scan_tile(xr, xi, pw_ref, base, ls, reverse):
    for s, k in enumerate((1, 2, 4)):
        shift = 8 - k if reverse else k
        sr = pltpu.roll(xr, shift, axis=0)
        si = pltpu.roll(xi, shift, axis=0)
        akr = pw_ref[base + 16 * s:base + 16 * s + 8, ls]
        aki = pw_ref[base + 16 * s + 8:base + 16 * s + 16, ls]
        xr, xi = xr + akr * sr - aki * si, xi + akr * si + aki * sr
    return xr, xi


def ssm_fwd(u, pw, br, bi, cr, ci, dsk):
    lp = u.shape[0]

    def body(u_ref, pw_ref, br_ref, bi_ref, cr_ref, ci_ref, d_ref, y_ref, sr_ref, si_ref, car_r, car_i):
        i = pl.program_id(0)

        @pl.when(i == 0)
        def _():
            car_r[...] = jnp.zeros_like(car_r)
            car_i[...] = jnp.zeros_like(car_i)

        u = u_ref[...]
        ub = u.astype(BF)
        for j in range(NLC):
            uj = ub[:, 128 * j:128 * j + 128]
            sr_ref[:, LC * j:LC * j + LC] = _dot(uj, br_ref[j])
            si_ref[:, LC * j:LC * j + LC] = _dot(uj, bi_ref[j])
        for j in range(NLC):
            ls = slice(LC * j, LC * j + LC)

            def group(g, carry):
                cr_, ci_ = carry
                rows = pl.ds(pl.multiple_of(g * 8, 8), 8)
                xr, xi = _scan_tile(sr_ref[rows, ls], si_ref[rows, ls], pw_ref, 0, ls, False)
                pr, pi = pw_ref[48:56, ls], pw_ref[56:64, ls]
                xr, xi = xr + pr * cr_ - pi * ci_, xi + pr * ci_ + pi * cr_
                sr_ref[rows, ls] = xr
                si_ref[rows, ls] = xi
                return xr[7:8, :], xi[7:8, :]

            cr_, ci_ = lax.fori_loop(0, TS // 8, group, (car_r[:, ls], car_i[:, ls]))
            car_r[:, ls] = cr_
            car_i[:, ls] = ci_
        for j in range(NLC):
            ls = slice(LC * j, LC * j + LC)
            y_ref[:, 128 * j:128 * j + 128] = (_dot(sr_ref[:, ls].astype(BF), cr_ref[j])
                                               - _dot(si_ref[:, ls].astype(BF), ci_ref[j]))
        y_ref[...] += d_ref[...] * u

    return pl.pallas_call(
        body, name="ssm_fwd", grid=(lp // TS,),
        in_specs=[_rows(TS, D), _full((2 * PW_ROWS, SW)), _full((NLC, 128, LC)), _full((NLC, 128, LC)),
                  _full((NLC, LC, 128)), _full((NLC, LC, 128)), _full((1, D))],
        out_specs=[_rows(TS, D), _rows(TS, SW), _rows(TS, SW)],
        out_shape=[jax.ShapeDtypeStruct((lp, D), F32), jax.ShapeDtypeStruct((lp, SW), F32),
                   jax.ShapeDtypeStruct((lp, SW), F32)],
        scratch_shapes=[pltpu.VMEM((1, SW), F32), pltpu.VMEM((1, SW), F32)],
        compiler_params=_cp(("arbitrary",)),
    )(u, pw, br, bi, cr, ci, dsk)


def ssm_bwd(dy, u, sr, si, pw, br, bi, cr, ci, dsk):
    lp = u.shape[0]
    nt = lp // TS

    def rev(i):
        return (nt - 1 - i, 0)

    def prev8(i):
        return (jnp.maximum((nt - 1 - i) * (TS // 8) - 1, 0), 0)

    def body(dy_ref, u_ref, sr_ref, si_ref, pr8_ref, pi8_ref, pw_ref, br_ref, bi_ref, cr_ref, ci_ref, d_ref,
             du_ref, da_ref, dbr_ref, dbi_ref, dcr_ref, dci_ref, dd_ref, gr_s, gi_s, car_r, car_i):
        i = pl.program_id(0)

        @pl.when(i == 0)
        def _():
            car_r[...] = jnp.zeros_like(car_r)
            car_i[...] = jnp.zeros_like(car_i)
            for ref in (da_ref, dbr_ref, dbi_ref, dcr_ref, dci_ref, dd_ref):
                ref[...] = jnp.zeros_like(ref)

        first = (i == nt - 1).astype(F32)
        dy = dy_ref[...]
        u = u_ref[...]
        dyb = dy.astype(BF)
        ub = u.astype(BF)
        dd_ref[...] += jnp.sum(dy * u, axis=0, keepdims=True)
        for j in range(NLC):
            ls = slice(LC * j, LC * j + LC)
            dyj = dyb[:, 128 * j:128 * j + 128]
            gr_s[:, ls] = _dot_nt(dyj, cr_ref[j])
            gi_s[:, ls] = -_dot_nt(dyj, ci_ref[j])
            dcr_ref[j] += _dot_tn(sr_ref[:, ls].astype(BF), dyj)
            dci_ref[j] -= _dot_tn(si_ref[:, ls].astype(BF), dyj)
        rid = lax.broadcasted_iota(jnp.int32, (8, LC), 0)
        for j in range(NLC):
            ls = slice(LC * j, LC * j + LC)

            def one_group(rows, spr, spi, carry):
                cr_, ci_ = carry
                gr, gi = _scan_tile(gr_s[rows, ls], gi_s[rows, ls], pw_ref, PW_ROWS, ls, True)
                pr, pi = pw_ref[PW_ROWS + 48:PW_ROWS + 56, ls], pw_ref[PW_ROWS + 56:PW_ROWS + 64, ls]
                gr, gi = gr + pr * cr_ - pi * ci_, gi + pr * ci_ + pi * cr_
                gr_s[rows, ls] = gr
                gi_s[rows, ls] = gi
                s_r = jnp.where(rid >= 1, pltpu.roll(sr_ref[rows, ls], 1, axis=0), spr[7:8, :])
                s_i = jnp.where(rid >= 1, pltpu.roll(si_ref[rows, ls], 1, axis=0), spi[7:8, :])
                da_ref[0:8, ls] += gr * s_r + gi * s_i
                da_ref[8:16, ls] += gi * s_r - gr * s_i
                return gr[0:1, :], gi[0:1, :]

            def group(k, carry):
                g = TS // 8 - 1 - k
                rows = pl.ds(pl.multiple_of(g * 8, 8), 8)
                before = pl.ds(pl.multiple_of(g * 8 - 8, 8), 8)
                return one_group(rows, sr_ref[before, ls], si_ref[before, ls], carry)

            carry = lax.fori_loop(0, TS // 8 - 1, group, (car_r[:, ls], car_i[:, ls]))
            keep = 1.0 - first
            cr_, ci_ = one_group(pl.ds(0, 8), pr8_ref[:, ls] * keep, pi8_ref[:, ls] * keep, carry)
            car_r[:, ls] = cr_
            car_i[:, ls] = ci_
        for j in range(NLC):
            ls = slice(LC * j, LC * j + LC)
            uj = ub[:, 128 * j:128 * j + 128]
            grb = gr_s[:, ls].astype(BF)
            gib = gi_s[:, ls].astype(BF)
            du_ref[:, 128 * j:128 * j + 128] = (_dot_nt(grb, br_ref[j]) + _dot_nt(gib, bi_ref[j])
                                                + d_ref[:, 128 * j:128 * j + 128] * dy[:, 128 * j:128 * j + 128]
                                                ).astype(BF)
            dbr_ref[j] += _dot_tn(uj, grb)
            dbi_ref[j] += _dot_tn(uj, gib)

    return pl.pallas_call(
        body, name="ssm_bwd", grid=(nt,),
        in_specs=[pl.BlockSpec((TS, D), rev), pl.BlockSpec((TS, D), rev), pl.BlockSpec((TS, SW), rev),
                  pl.BlockSpec((TS, SW), rev), pl.BlockSpec((8, SW), prev8), pl.BlockSpec((8, SW), prev8),
                  _full((2 * PW_ROWS, SW)), _full((NLC, 128, LC)), _full((NLC, 128, LC)),
                  _full((NLC, LC, 128)), _full((NLC, LC, 128)), _full((1, D))],
        out_specs=[pl.BlockSpec((TS, D), rev), _full((16, SW)), _full((NLC, 128, LC)), _full((NLC, 128, LC)),
                   _full((NLC, LC, 128)), _full((NLC, LC, 128)), _full((1, D))],
        out_shape=[jax.ShapeDtypeStruct((lp, D), BF), jax.ShapeDtypeStruct((16, SW), F32),
                   jax.ShapeDtypeStruct((NLC, 128, LC), F32), jax.ShapeDtypeStruct((NLC, 128, LC), F32),
                   jax.ShapeDtypeStruct((NLC, LC, 128), F32), jax.ShapeDtypeStruct((NLC, LC, 128), F32),
                   jax.ShapeDtypeStruct((1, D), F32)],
        scratch_shapes=[pltpu.VMEM((TS, SW), F32), pltpu.VMEM((TS, SW), F32),
                        pltpu.VMEM((1, SW), F32), pltpu.VMEM((1, SW), F32)],
        compiler_params=_cp(("arbitrary",)),
    )(dy, u, sr, si, sr, si, pw, br, bi, cr, ci, dsk)


def _merge_parts(attn, zz, gg, abn, sbn):
    za, zb = zz[:, 0:D], zz[:, D:2 * D]
    sgz = _sigmoid(zb)
    ssm = za * sgz
    ra = lax.rsqrt(jnp.mean(attn * attn, axis=-1, keepdims=True) + EPS)
    rs = lax.rsqrt(jnp.mean(ssm * ssm, axis=-1, keepdims=True) + EPS)
    ah, sh = attn * ra, ssm * rs
    sga, sgs = _sigmoid(gg[:, 0:D]), _sigmoid(gg[:, D:2 * D])
    return za, sgz, ra, rs, ah, sh, sga, sgs, ah * abn, sh * sbn


def mid_fwd(y, attn, gg, x, mpad, w_glu, w_out, abn, sbn):
    lp = y.shape[0]

    def body(y_ref, a_ref, gg_ref, x_ref, m_ref, wg_ref, wo_ref, abn_ref, sbn_ref, z_o, zz_o, mg_o, h1_o):
        z, _ = _gelu(y_ref[...])
        zb = z.astype(BF)
        z_o[...] = zb
        zz = _dot(zb, wg_ref[...])
        zz_o[...] = zz.astype(BF)
        _, _, _, _, _, _, sga, sgs, an, sn = _merge_parts(a_ref[...], zz, gg_ref[...], abn_ref[...], sbn_ref[...])
        mb = (sga * an + sgs * sn).astype(BF)
        mg_o[...] = mb
        h1_o[...] = _seq_tile(x_ref, m_ref) + _dot(mb, wo_ref[...])

    return pl.pallas_call(
        body, name="mid_fwd", grid=(lp // TM,),
        in_specs=[_rows(TM, D), _rows(TM, D), _rows(TM, 2 * D)] + _seq_specs() + [_full((D, 2 * D)), _full((D, D)),
                  _full((1, D)), _full((1, D))],
        out_specs=[_rows(TM, D), _rows(TM, 2 * D), _rows(TM, D), _rows(TM, D)],
        out_shape=[jax.ShapeDtypeStruct((lp, D), BF), jax.ShapeDtypeStruct((lp, 2 * D), BF),
                   jax.ShapeDtypeStruct((lp, D), BF), jax.ShapeDtypeStruct((lp, D), F32)],
        compiler_params=_cp(("arbitrary",)),
    )(y, attn, gg, x, mpad, w_glu, w_out, abn, sbn)


def mid_bwd(dh1b, y, attn, gg, zzb, w_glu, w_out, abn, sbn):
    lp = y.shape[0]

    def body(dh_ref, y_ref, a_ref, gg_ref, zz_ref, wg_ref, wo_ref, abn_ref, sbn_ref,
             dy_o, dattn_o, dgg_o, dzz_o, dabn_o, dsbn_o):
        i = pl.program_id(0)

        @pl.when(i == 0)
        def _():
            dabn_o[...] = jnp.zeros_like(dabn_o)
            dsbn_o[...] = jnp.zeros_like(dsbn_o)

        dm = _dot_nt(dh_ref[...], wo_ref[...])
        abn, sbn = abn_ref[...], sbn_ref[...]
        za, sgz, ra, rs, ah, sh, sga, sgs, an, sn = _merge_parts(
            a_ref[...], zz_ref[...].astype(F32), gg_ref[...], abn, sbn)
        dgg_o[:, 0:D] = (dm * an * sga * (1.0 - sga)).astype(BF)
        dgg_o[:, D:2 * D] = (dm * sn * sgs * (1.0 - sgs)).astype(BF)
        da = dm * sga
        dabn_o[...] += jnp.sum(da * ah, axis=0, keepdims=True)
        dah = da * abn
        dattn_o[...] = (ra * (dah - ah * jnp.mean(dah * ah, axis=-1, keepdims=True))).astype(BF)
        ds = dm * sgs
        dsbn_o[...] += jnp.sum(ds * sh, axis=0, keepdims=True)
        dsh = ds * sbn
        dssm = rs * (dsh - sh * jnp.mean(dsh * sh, axis=-1, keepdims=True))
        dzz_o[:, 0:D] = (dssm * sgz).astype(BF)
        dzz_o[:, D:2 * D] = (dssm * za * sgz * (1.0 - sgz)).astype(BF)
        dz = _dot_nt(dzz_o[...], wg_ref[...])
        yv = y_ref[...]
        _, t = _gelu(yv)
        dy_o[...] = dz * _gelu_grad(yv, t)

    return pl.pallas_call(
        body, name="mid_bwd", grid=(lp // TM,),
        in_specs=[_rows(TM, D), _rows(TM, D), _rows(TM, D), _rows(TM, 2 * D), _rows(TM, 2 * D),
                  _full((D, 2 * D)), _full((D, D)), _full((1, D)), _full((1, D))],
        out_specs=[_rows(TM, D), _rows(TM, D), _rows(TM, 2 * D), _rows(TM, 2 * D), _full((1, D)), _full((1, D))],
        out_shape=[jax.ShapeDtypeStruct((lp, D), F32), jax.ShapeDtypeStruct((lp, D), BF),
                   jax.ShapeDtypeStruct((lp, 2 * D), BF), jax.ShapeDtypeStruct((lp, 2 * D), BF),
                   jax.ShapeDtypeStruct((1, D), F32), jax.ShapeDtypeStruct((1, D), F32)],
        compiler_params=_cp(("arbitrary",)),
    )(dh1b, y, attn, gg, zzb, w_glu, w_out, abn, sbn)


def ffn_fwd_bwd(h1, tgt, w_fi, w_fo, gf):
    lp = h1.shape[0]
    tf = 256
    nch = 2
    cw = DFF // nch

    def body(h_ref, t_ref, wi_ref, wo_ref, gf_ref, hn_o, dgu_o, act_o, dh2_o, dh1_o, dh1b_o, dgf_o, loss_o,
             gate_s, up_s):
        i = pl.program_id(0)

        @pl.when(i == 0)
        def _():
            dgf_o[...] = jnp.zeros_like(dgf_o)
            loss_o[...] = jnp.zeros_like(loss_o)

        h = h_ref[...]
        r = lax.rsqrt(jnp.mean(h * h, axis=-1, keepdims=True) + EPS)
        xh = h * r
        hb = (xh * gf_ref[...]).astype(BF)
        hn_o[...] = hb
        h2 = h
        for c in range(nch):
            gate = _dot(hb, wi_ref[:, cw * c:cw * (c + 1)])
            up = _dot(hb, wi_ref[:, DFF + cw * c:DFF + cw * (c + 1)])
            gate_s[:, cw * c:cw * (c + 1)] = gate
            up_s[:, cw * c:cw * (c + 1)] = up
            ab = (gate * _sigmoid(gate) * up).astype(BF)
            act_o[:, cw * c:cw * (c + 1)] = ab
            h2 = h2 + _dot(ab, wo_ref[cw * c:cw * (c + 1), :])
        real = (i >= PAD // tf).astype(F32)
        e = (h2 - t_ref[...]) * real
        loss_o[...] += 0.5 * jnp.sum(jnp.sum(e * e, axis=-1, keepdims=True), axis=0, keepdims=True) * (1.0 / D)
        dh2 = e * (1.0 / D)
        db = dh2.astype(BF)
        dh2_o[...] = db
        dhn = jnp.zeros((tf, D), F32)
        for c in range(nch):
            gate = gate_s[:, cw * c:cw * (c + 1)]
            up = up_s[:, cw * c:cw * (c + 1)]
            sg = _sigmoid(gate)
            dact = _dot_nt(db, wo_ref[cw * c:cw * (c + 1), :])
            dgate = (dact * up * (sg * (1.0 + gate * (1.0 - sg)))).astype(BF)
            dup = (dact * (gate * sg)).astype(BF)
            dgu_o[:, cw * c:cw * (c + 1)] = dgate
            dgu_o[:, DFF + cw * c:DFF + cw * (c + 1)] = dup
            dhn += (_dot_nt(dgate, wi_ref[:, cw * c:cw * (c + 1)])
                    + _dot_nt(dup, wi_ref[:, DFF + cw * c:DFF + cw * (c + 1)]))
        dgf_o[...] += jnp.sum(dhn * xh, axis=0, keepdims=True)
        dxh = dhn * gf_ref[...]
        dh1 = dh2 + r * (dxh - xh * jnp.mean(dxh * xh, axis=-1, keepdims=True))
        dh1_o[...] = dh1
        dh1b_o[...] = dh1.astype(BF)

    return pl.pallas_call(
        body, name="ffn_fwd_bwd", grid=(lp // tf,),
        in_specs=[_rows(tf, D), pl.BlockSpec((tf, D), lambda i: (jnp.maximum(i - PAD // tf, 0), 0)),
                  _const((D, 2 * DFF)), _const((DFF, D)), _full((1, D))],
        out_specs=[_rows(tf, D), _rows(tf, 2 * DFF), _rows(tf, DFF), _rows(tf, D), _rows(tf, D), _rows(tf, D),
                   _full((1, D)), _full((1, 1))],
        out_shape=[jax.ShapeDtypeStruct((lp, D), BF), jax.ShapeDtypeStruct((lp, 2 * DFF), BF),
                   jax.ShapeDtypeStruct((lp, DFF), BF), jax.ShapeDtypeStruct((lp, D), BF),
                   jax.ShapeDtypeStruct((lp, D), F32), jax.ShapeDtypeStruct((lp, D), BF),
                   jax.ShapeDtypeStruct((1, D), F32), jax.ShapeDtypeStruct((1, 1), F32)],
        scratch_shapes=[pltpu.VMEM((tf, DFF), F32), pltpu.VMEM((tf, DFF), F32)],
        compiler_params=_cp(("arbitrary",), vmem_mb=58),
    )(h1, tgt, w_fi, w_fo, gf)


def weight_grad(name, a, b, tn):
    lp, k = a.shape
    n = b.shape[1]
    tr = next(t for t in (768, 512, 384, 256, 128) if lp % t == 0)

    def body(a_ref, b_ref, o_ref):
        @pl.when(pl.program_id(1) == 0)
        def _():
            o_ref[...] = jnp.zeros_like(o_ref)

        o_ref[...] += _dot_tn(a_ref[...], b_ref[...])

    return pl.pallas_call(
        body, name=name, grid=(n // tn, lp // tr),
        in_specs=[pl.BlockSpec((tr, k), lambda j, m: (m, 0)), pl.BlockSpec((tr, tn), lambda j, m: (m, j))],
        out_specs=pl.BlockSpec((k, tn), lambda j, m: (0, j)),
        out_shape=jax.ShapeDtypeStruct((k, n), F32),
        compiler_params=_cp(("arbitrary", "arbitrary")),
    )(a, b)


def _adamw_math(w, g, m, v):
    m = B1 * m + (1.0 - B1) * g
    v = B2 * v + (1.0 - B2) * (g * g)
    delta = -LR * ((m / BC1) / (jnp.sqrt(v / BC2) + AEPS) + WD * w)
    return delta, m, v


def _row_tile(r):
    return next((t for t in (256, 128, 64, 32, 16, 8) if r % t == 0), r)


def adamw(name, w, g, m, v):
    r, c = w.shape
    tr = _row_tile(r)

    def body(w_ref, g_ref, m_ref, v_ref, d_o, m_o, v_o):
        d_o[...], m_o[...], v_o[...] = _adamw_math(w_ref[...], g_ref[...], m_ref[...], v_ref[...])

    spec = pl.BlockSpec((tr, c), lambda i: (i, 0))
    return pl.pallas_call(
        body, name=name, grid=(r // tr,), in_specs=[spec] * 4, out_specs=[spec] * 3,
        out_shape=[jax.ShapeDtypeStruct((r, c), F32)] * 3,
        compiler_params=_cp(("arbitrary",)),
    )(w, g, m, v)


def adamw_small(ws, gs, ms, vs):
    n = len(ws)
    steps = 8

    def spec(a):
        if a.ndim == 4:
            return pl.BlockSpec((1, a.shape[1] // steps) + a.shape[2:], lambda i: (0, i, 0, 0))
        nd = a.ndim
        return pl.BlockSpec(a.shape, lambda i: (0,) * nd)

    def body(*refs):
        w, g, m, v, d_o, m_o, v_o = (refs[k * n:(k + 1) * n] for k in range(7))
        for t in range(n):
            d_o[t][...], m_o[t][...], v_o[t][...] = _adamw_math(w[t][...], g[t][...], m[t][...], v[t][...])

    specs = [spec(a) for a in ws]
    res = pl.pallas_call(
        body, name="adamw_small", grid=(steps,), in_specs=specs * 4, out_specs=specs * 3,
        out_shape=[jax.ShapeDtypeStruct(a.shape, F32) for a in ws] * 3,
        compiler_params=_cp(("arbitrary",)),
    )(*ws, *gs, *ms, *vs)
    return res[:n], res[n:2 * n], res[2 * n:]


def pair_sum(name, place, full, got, ax):
    r, c = got.shape
    tr = _row_tile(r)

    def body(s_ref, a_ref, b_ref, o_ref):
        o_ref[...] = (a_ref[...] + b_ref[...]).astype(BF)

    if ax == 1:
        mine = pl.BlockSpec((tr, c), lambda i, s: (s[1] * (r // tr) + i, 0))
    else:
        mine = pl.BlockSpec((tr, c), lambda i, s: (i, s[1]))
    spec = pl.BlockSpec((tr, c), lambda i, s: (i, 0))
    return pl.pallas_call(
        body, name=name,
        grid_spec=pltpu.PrefetchScalarGridSpec(num_scalar_prefetch=1, grid=(r // tr,), in_specs=[mine, spec],
                                               out_specs=spec),
        out_shape=jax.ShapeDtypeStruct((r, c), BF),
        compiler_params=_cp(("arbitrary",)),
    )(place, full, got)


def sum4(name, place, parts, half, ax):
    _, r, c = parts.shape
    tr = _row_tile(r)

    def body(s_ref, p_ref, h_ref, o_ref):
        o_ref[...] = ((p_ref[0].astype(F32) + p_ref[1].astype(F32))
                      + (p_ref[2].astype(F32) + h_ref[...].astype(F32)))

    if ax == 1:
        own = pl.BlockSpec((tr, c), lambda i, s: (i, s[0]))
        out = pl.BlockSpec((tr, c), lambda i, s: (s[1] * (r // tr) + i, 0))
        shape = (2 * r, c)
    else:
        own = pl.BlockSpec((tr, c), lambda i, s: (s[0] * (r // tr) + i, 0))
        out = pl.BlockSpec((tr, c), lambda i, s: (i, s[1]))
        shape = (r, 2 * c)
    return pl.pallas_call(
        body, name=name,
        grid_spec=pltpu.PrefetchScalarGridSpec(
            num_scalar_prefetch=1, grid=(r // tr,),
            in_specs=[pl.BlockSpec((3, tr, c), lambda i, s: (0, i, 0)), own], out_specs=out),
        out_shape=jax.ShapeDtypeStruct(shape, F32),
        compiler_params=_cp(("arbitrary",)),
    )(place, parts, half)


def _place():
    x, y, c = lax.axis_index("x"), lax.axis_index("y"), lax.axis_index("c")
    return x, y, c


def _other_chips(x, y):
    return [(1 - x, y), (x, 1 - y), (1 - x, 1 - y)]


SHARDED = (("w_in", D, IN_COLS, 1), ("w_glu", D, 2 * D, 1), ("w_out", D, D, 0),
           ("w_ffn_in", D, 2 * DFF, 1), ("w_ffn_out", DFF, D, 0))


def _shard_of(ref, axis, k, size):
    if axis == 0:
        return ref.at[pl.ds(k * size, size), :]
    return ref.at[:, pl.ds(k * size, size)]


def _gather_piece(full, dims, k, h):
    r, cc, ax = dims
    if ax == 0:
        return full.at[pl.ds(k * (r // 4) + h * (r // 8), r // 8), :]
    return full.at[pl.ds(h * (r // 2), r // 2), pl.ds(k * (cc // 4), cc // 4)]


def _rows_half(shard, h):
    rs = shard.shape[0]
    return shard.at[pl.ds(h * (rs // 2), rs // 2), :]


def gather_weights(shards, dims):
    items = [(s,) + tuple(d) for s, d in zip(shards, dims)]
    n = len(items)

    def body(*refs):
        ins, outs = refs[:n], refs[n:2 * n]
        ici_send, ici_recv, d2d_send, d2d_recv, local_sems = refs[2 * n:]
        x, y, c = _place()
        mine = 2 * x + y
        chips = _other_chips(x, y)

        def piece(t, k, h):
            return _gather_piece(outs[t], dims[t], k, h)

        def src_half(t, h):
            return _rows_half(ins[t], h)

        local, first, passed = [], [], []
        for t in range(n):
            _, r, cc, ax = items[t]
            size = (r if ax == 0 else cc) // 4
            loc = pltpu.make_async_remote_copy(
                src_ref=ins[t], dst_ref=_shard_of(outs[t], ax, mine, size), send_sem=local_sems.at[t],
                recv_sem=local_sems.at[n + t], device_id=(x, y, 1 - c), device_id_type=MESH)
            loc.start()
            local.append(loc)
            for j, chip in enumerate(chips):
                cp = pltpu.make_async_remote_copy(
                    src_ref=src_half(t, c), dst_ref=piece(t, mine, c),
                    send_sem=ici_send.at[3 * t + j], recv_sem=ici_recv.at[3 * t + j],
                    device_id=(*chip, c), device_id_type=MESH)
                cp.start()
                first.append(cp)
        for t in range(n):
            for j, (px, py) in enumerate(chips):
                got = piece(t, 2 * px + py, c)
                pltpu.make_async_remote_copy(
                    src_ref=src_half(t, c), dst_ref=got, send_sem=ici_send.at[3 * t + j],
                    recv_sem=ici_recv.at[3 * t + j], device_id=(px, py, c), device_id_type=MESH).wait_recv()
                cp = pltpu.make_async_remote_copy(
                    src_ref=got, dst_ref=got, send_sem=d2d_send.at[3 * t + j], recv_sem=d2d_recv.at[3 * t + j],
                    device_id=(x, y, 1 - c), device_id_type=MESH)
                cp.start()
                passed.append(cp)
        for t in range(n):
            for j, (px, py) in enumerate(chips):
                other = piece(t, 2 * px + py, 1 - c)
                pltpu.make_async_remote_copy(
                    src_ref=other, dst_ref=other, send_sem=d2d_send.at[3 * t + j], recv_sem=d2d_recv.at[3 * t + j],
                    device_id=(x, y, 1 - c), device_id_type=MESH).wait_recv()
        for cp in first + passed:
            cp.wait_send()
        for loc in local:
            loc.wait()

    return pl.pallas_call(
        body, name="gather_weights",
        in_specs=[ANY] * n, out_specs=[ANY] * n,
        out_shape=[jax.ShapeDtypeStruct((r, c), s.dtype) for s, r, c, _ in items],
        scratch_shapes=[pltpu.SemaphoreType.DMA((3 * n,)), pltpu.SemaphoreType.DMA((3 * n,)),
                        pltpu.SemaphoreType.DMA((3 * n,)), pltpu.SemaphoreType.DMA((3 * n,)),
                        pltpu.SemaphoreType.DMA((2 * n,))],
        compiler_params=pltpu.CompilerParams(has_side_effects=True),
    )(*shards)


SEM = pl.BlockSpec(memory_space=pltpu.SEMAPHORE)
HBM = pl.BlockSpec(memory_space=pltpu.HBM)
EFFECT = pltpu.SideEffectType.DATAFLOW_SIDE_EFFECTING


def _in_hbm(a):
    return pltpu.with_memory_space_constraint(a, pltpu.HBM)


def _split_copy_start(name, copies, ncopies, srcs, lands, after):
    ns, nl = len(srcs), len(lands)

    def body(*refs):
        ins, land_refs = refs[:ns], refs[ns:ns + nl]
        send_sems, recv_sems = refs[ns + nl + 1], refs[ns + nl + 2]
        token = refs[-1]
        for k, (src, dst, dev) in enumerate(copies(ins, land_refs)):
            pltpu.make_async_remote_copy(src_ref=src, dst_ref=dst, send_sem=send_sems.at[k],
                                         recv_sem=recv_sems.at[k], device_id=dev, device_id_type=MESH).start()
        token[...] = jnp.zeros_like(token)

    res = pl.pallas_call(
        body, name=name,
        in_specs=[HBM] * (ns + nl) + [ANY],
        out_specs=(SEM, SEM) + (HBM,) * (ns + nl) + (pl.BlockSpec(memory_space=pltpu.VMEM),),
        out_shape=(pltpu.SemaphoreType.DMA((ncopies,)), pltpu.SemaphoreType.DMA((ncopies,)))
        + tuple(pltpu.HBM(a.shape, a.dtype) for a in list(srcs) + list(lands))
        + (jax.ShapeDtypeStruct((8, 128), F32),),
        input_output_aliases={t: 2 + t for t in range(ns + nl)},
        compiler_params=pltpu.CompilerParams(has_side_effects=EFFECT),
    )(*[_in_hbm(a) for a in srcs], *[_in_hbm(a) for a in lands], after)
    return res[0], res[1], list(res[2:2 + ns]), list(res[2 + ns:2 + ns + nl]), res[-1]


def _split_copy_wait(name, copies, arrivals, send_sems, recv_sems, srcs, lands, after):
    ns, nl = len(srcs), len(lands)

    def body(*refs):
        ins, land_refs = refs[:ns], refs[ns:ns + nl]
        send_sems_ref, recv_sems_ref = refs[ns + nl], refs[ns + nl + 1]
        mine = copies(ins, land_refs)
        for k, ((src, dst, dev), got) in enumerate(zip(mine, arrivals(ins, land_refs))):
            pltpu.make_async_remote_copy(src_ref=src, dst_ref=dst, send_sem=send_sems_ref.at[k],
                                         recv_sem=recv_sems_ref.at[k], device_id=dev,
                                         device_id_type=MESH).wait_send()
            pltpu.make_async_remote_copy(src_ref=got, dst_ref=got, send_sem=send_sems_ref.at[k],
                                         recv_sem=recv_sems_ref.at[k], device_id=dev,
                                         device_id_type=MESH).wait_recv()

    res = pl.pallas_call(
        body, name=name,
        in_specs=[HBM] * (ns + nl) + [SEM, SEM, ANY],
        out_specs=(HBM,) * (ns + nl),
        out_shape=tuple(pltpu.HBM(a.shape, a.dtype) for a in list(srcs) + list(lands)),
        input_output_aliases={t: t for t in range(ns + nl)},
        compiler_params=pltpu.CompilerParams(has_side_effects=EFFECT),
    )(*srcs, *lands, send_sems, recv_sems, after)
    return list(res[:ns]), list(res[ns:])


def _gather_copies(dims):
    def copies(ins, lands):
        x, y, c = _place()
        mine = 2 * x + y
        out = []
        for t, d in enumerate(dims):
            size = (d[0] if d[2] == 0 else d[1]) // 4
            for chip in _other_chips(x, y):
                out.append((_rows_half(ins[t], c), _gather_piece(lands[t], d, mine, c), (*chip, c)))
            out.append((ins[t], _shard_of(lands[t], d[2], mine, size), (x, y, 1 - c)))
        return out

    def arrivals(ins, lands):
        x, y, c = _place()
        mine = 2 * x + y
        out = []
        for t, d in enumerate(dims):
            size = (d[0] if d[2] == 0 else d[1]) // 4
            for px, py in _other_chips(x, y):
                out.append(_gather_piece(lands[t], d, 2 * px + py, c))
            out.append(_shard_of(lands[t], d[2], mine, size))
        return out

    return copies, arrivals


def gather_forward(fulls, dims):
    n = len(fulls)

    def body(*refs):
        ins, outs = refs[:n], refs[n:2 * n]
        send_sems, recv_sems = refs[2 * n:]
        x, y, c = _place()
        cps = []
        for t in range(n):
            for j, (px, py) in enumerate(_other_chips(x, y)):
                cp = pltpu.make_async_remote_copy(
                    src_ref=_gather_piece(ins[t], dims[t], 2 * px + py, c),
                    dst_ref=_gather_piece(outs[t], dims[t], 2 * px + py, c),
                    send_sem=send_sems.at[3 * t + j], recv_sem=recv_sems.at[3 * t + j],
                    device_id=(x, y, 1 - c), device_id_type=MESH)
                cp.start()
                cps.append(cp)
        for t in range(n):
            for j, (px, py) in enumerate(_other_chips(x, y)):
                cps[3 * t + j].wait_send()
                other = _gather_piece(outs[t], dims[t], 2 * px + py, 1 - c)
                pltpu.make_async_remote_copy(
                    src_ref=other, dst_ref=other, send_sem=send_sems.at[3 * t + j], recv_sem=recv_sems.at[3 * t + j],
                    device_id=(x, y, 1 - c), device_id_type=MESH).wait_recv()

    return pl.pallas_call(
        body, name="gather_forward",
        in_specs=[ANY] * n, out_specs=[ANY] * n,
        out_shape=[jax.ShapeDtypeStruct(a.shape, a.dtype) for a in fulls],
        input_output_aliases={t: t for t in range(n)},
        scratch_shapes=[pltpu.SemaphoreType.DMA((3 * n,)), pltpu.SemaphoreType.DMA((3 * n,))],
        compiler_params=pltpu.CompilerParams(has_side_effects=True),
    )(*fulls)


def _half_of(ref, ax, h):
    r, c = ref.shape
    if ax == 1:
        return ref.at[pl.ds(h * (r // 2), r // 2), :]
    return ref.at[:, pl.ds(h * (c // 2), c // 2)]


def _half_shape(r, c, ax):
    return (r // 2, c) if ax == 1 else (r, c // 2)


def pair_exchange(name, grads, idxs):
    n = len(grads)
    which = [SHARDED[i] for i in idxs]

    def body(*refs):
        ins, got = refs[:n], refs[n:2 * n]
        send_sems, recv_sems = refs[2 * n:]
        x, y, c = _place()
        cps = []
        for t, (_, _, _, ax) in enumerate(which):
            cp = pltpu.make_async_remote_copy(
                src_ref=_half_of(ins[t], ax, 1 - c), dst_ref=got[t], send_sem=send_sems.at[t],
                recv_sem=recv_sems.at[t], device_id=(x, y, 1 - c), device_id_type=MESH)
            cp.start()
            cps.append(cp)
        for cp in cps:
            cp.wait()

    return pl.pallas_call(
        body, name=name,
        in_specs=[ANY] * n, out_specs=[ANY] * n,
        out_shape=[jax.ShapeDtypeStruct(_half_shape(r, c, ax), F32) for _, r, c, ax in which],
        scratch_shapes=[pltpu.SemaphoreType.DMA((n,)), pltpu.SemaphoreType.DMA((n,))],
        compiler_params=pltpu.CompilerParams(has_side_effects=True),
    )(*grads)


def _piece_shapes(idxs):
    out = []
    for _, r, c, ax in [SHARDED[i] for i in idxs]:
        hr, hc = _half_shape(r, c, ax)
        out.append((hr // 4, hc) if ax == 0 else (hr, hc // 4))
    return out


def _scatter_copies(idxs):
    which = [SHARDED[i] for i in idxs]

    def copies(ins, lands):
        x, y, c = _place()
        out = []
        for t, (_, _, _, ax) in enumerate(which):
            size = ins[t].shape[ax] // 4
            for j, (px, py) in enumerate(_other_chips(x, y)):
                out.append((_shard_of(ins[t], ax, 2 * px + py, size), lands[t].at[j], (px, py, c)))
        return out

    def arrivals(ins, lands):
        return [lands[t].at[j] for t in range(len(which)) for j in range(3)]

    return copies, arrivals


def scatter_grads(halves, idxs):
    n = len(halves)
    which = [SHARDED[i] for i in idxs]

    def shapes():
        return _piece_shapes(idxs)

    def body(*refs):
        ins, outs = refs[:n], refs[n:2 * n]
        send_sems, recv_sems = refs[2 * n:]
        x, y, c = _place()
        started = []
        for t, (src, dst, (_, _, _, ax)) in enumerate(zip(ins, outs, which)):
            size = src.shape[ax] // 4
            for j, (px, py) in enumerate(_other_chips(x, y)):
                cp = pltpu.make_async_remote_copy(
                    src_ref=_shard_of(src, ax, 2 * px + py, size), dst_ref=dst.at[j],
                    send_sem=send_sems.at[3 * t + j], recv_sem=recv_sems.at[3 * t + j],
                    device_id=(px, py, c), device_id_type=MESH)
                cp.start()
                started.append(cp)
        for cp in started:
            cp.wait()

    return pl.pallas_call(
        body, name="scatter_grads",
        in_specs=[ANY] * n, out_specs=[ANY] * n,
        out_shape=[jax.ShapeDtypeStruct((3,) + s, BF) for s in shapes()],
        scratch_shapes=[pltpu.SemaphoreType.DMA((3 * n,)), pltpu.SemaphoreType.DMA((3 * n,))],
        compiler_params=pltpu.CompilerParams(has_side_effects=True),
    )(*halves)


def join_halves(shards):
    n = len(shards)

    def body(*refs):
        ins, outs = refs[:n], refs[n:2 * n]
        send_sems, recv_sems = refs[2 * n:]
        x, y, c = _place()
        cps = []
        for t, (_, _, _, ax) in enumerate(SHARDED):
            cp = pltpu.make_async_remote_copy(
                src_ref=_half_of(ins[t], ax, c), dst_ref=_half_of(outs[t], ax, c), send_sem=send_sems.at[t],
                recv_sem=recv_sems.at[t], device_id=(x, y, 1 - c), device_id_type=MESH)
            cp.start()
            cps.append(cp)
        for t, cp in enumerate(cps):
            cp.wait_send()
            theirs = _half_of(outs[t], SHARDED[t][3], 1 - c)
            pltpu.make_async_remote_copy(
                src_ref=theirs, dst_ref=theirs, send_sem=send_sems.at[t], recv_sem=recv_sems.at[t],
                device_id=(x, y, 1 - c), device_id_type=MESH).wait_recv()

    return pl.pallas_call(
        body, name="join_halves",
        in_specs=[ANY] * n, out_specs=[ANY] * n,
        out_shape=[jax.ShapeDtypeStruct(s.shape, F32) for s in shards],
        input_output_aliases={t: t for t in range(n)},
        scratch_shapes=[pltpu.SemaphoreType.DMA((n,)), pltpu.SemaphoreType.DMA((n,))],
        compiler_params=pltpu.CompilerParams(has_side_effects=True),
    )(*shards)


def allreduce_small(pack):
    rows = pack.shape[0]
    hr = rows // 2

    def body(p_ref, o_ref, sib_ref, parts_ref, send_sems, recv_sems):
        x, y, c = _place()
        mine = 2 * x + y
        sibling = (x, y, 1 - c)
        swap = pltpu.make_async_remote_copy(
            src_ref=p_ref, dst_ref=sib_ref, send_sem=send_sems.at[0], recv_sem=recv_sems.at[0],
            device_id=sibling, device_id_type=MESH)
        swap.start()
        swap.wait()
        half = pl.ds(pl.multiple_of(c * hr, 8), hr)
        parts_ref[mine] = p_ref[half, :] + sib_ref[half, :]
        cps = []
        for j, chip in enumerate(_other_chips(x, y)):
            cp = pltpu.make_async_remote_copy(
                src_ref=parts_ref.at[mine], dst_ref=parts_ref.at[mine], send_sem=send_sems.at[1 + j],
                recv_sem=recv_sems.at[1 + j], device_id=(*chip, c), device_id_type=MESH)
            cp.start()
            cps.append(cp)
        for j, (px, py) in enumerate(_other_chips(x, y)):
            cps[j].wait_send()
            theirs = parts_ref.at[2 * px + py]
            pltpu.make_async_remote_copy(
                src_ref=theirs, dst_ref=theirs, send_sem=send_sems.at[1 + j], recv_sem=recv_sems.at[1 + j],
                device_id=(px, py, c), device_id_type=MESH).wait_recv()
        o_ref[half, :] = (parts_ref[0] + parts_ref[1]) + (parts_ref[2] + parts_ref[3])
        back = pltpu.make_async_remote_copy(
            src_ref=o_ref.at[half, :], dst_ref=o_ref.at[half, :], send_sem=send_sems.at[4],
            recv_sem=recv_sems.at[4], device_id=sibling, device_id_type=MESH)
        back.start()
        back.wait_send()
        other = pl.ds(pl.multiple_of((1 - c) * hr, 8), hr)
        pltpu.make_async_remote_copy(
            src_ref=o_ref.at[other, :], dst_ref=o_ref.at[other, :], send_sem=send_sems.at[4],
            recv_sem=recv_sems.at[4], device_id=sibling, device_id_type=MESH).wait_recv()

    return pl.pallas_call(
        body, name="allreduce_small",
        in_specs=[pl.BlockSpec(memory_space=pltpu.VMEM)], out_specs=pl.BlockSpec(memory_space=pltpu.VMEM),
        out_shape=jax.ShapeDtypeStruct((rows, 128), F32),
        scratch_shapes=[pltpu.VMEM((rows, 128), F32), pltpu.VMEM((4, hr, 128), F32),
                        pltpu.SemaphoreType.DMA((5,)), pltpu.SemaphoreType.DMA((5,))],
        compiler_params=pltpu.CompilerParams(has_side_effects=True, vmem_limit_bytes=40 << 20),
    )(pack)


def _ssm_discretize(lam_re, lam_im, log_dt, b_re, b_im):
    dt = jnp.exp(log_dt)[:, None]
    mag = jnp.exp(lam_re * dt)
    ar, ai = mag * jnp.cos(lam_im * dt), mag * jnp.sin(lam_im * dt)
    den = lam_re * lam_re + lam_im * lam_im
    nr, ni = ar - 1.0, ai
    fr, fi = (nr * lam_re + ni * lam_im) / den, (ni * lam_re - nr * lam_im) / den
    bbr = fr[..., None] * b_re - fi[..., None] * b_im
    bbi = fr[..., None] * b_im + fi[..., None] * b_re
    return ar, ai, bbr, bbi


def _cmul(a, b):
    return a[0] * b[0] - a[1] * b[1], a[0] * b[1] + a[1] * b[0]


def _scan_powers(ar, ai):
    a = (ar.reshape(1, SW), ai.reshape(1, SW))
    pows = [a]
    for _ in range(7):
        pows.append(_cmul(pows[-1], a))
    fr = jnp.concatenate([p[0] for p in pows], axis=0)
    fi = jnp.concatenate([p[1] for p in pows], axis=0)
    rid = jnp.arange(8)[:, None]
    fwd, rev = [], []
    for k in (1, 2, 4):
        pr, pi = pows[k - 1]
        fwd += [jnp.where(rid >= k, pr, 0.0), jnp.where(rid >= k, pi, 0.0)]
        rev += [jnp.where(rid < 8 - k, pr, 0.0), jnp.where(rid < 8 - k, -pi, 0.0)]
    return jnp.concatenate(fwd + [fr, fi] + rev + [fr[::-1], -fi[::-1]], axis=0)


def _pack_b(bb):
    t = bb.reshape(NLC, 8, NSTATE, GCH).transpose(0, 1, 3, 2)
    return jnp.einsum("jgcp,gh->jgchp", t, jnp.eye(8, dtype=bb.dtype)).reshape(NLC, 128, LC)


def _unpack_b(db):
    t = jnp.einsum("jgchp,gh->jgcp", db.reshape(NLC, 8, GCH, 8, NSTATE), jnp.eye(8, dtype=db.dtype))
    return t.transpose(0, 1, 3, 2).reshape(GROUPS, NSTATE, GCH)


def _pack_c(cc):
    t = cc.reshape(NLC, 8, GCH, NSTATE).transpose(0, 1, 3, 2)
    return jnp.einsum("jgpc,gh->jgphc", t, jnp.eye(8, dtype=cc.dtype)).reshape(NLC, LC, 128)


def _unpack_c(dc):
    t = jnp.einsum("jgphc,gh->jgpc", dc.reshape(NLC, 8, NSTATE, 8, GCH), jnp.eye(8, dtype=dc.dtype))
    return t.transpose(0, 1, 3, 2).reshape(GROUPS, GCH, NSTATE)


SMALL = ("norm_mix", "q_norm", "k_norm", "attn_sinks", "lam_re", "lam_im", "log_dt", "ssm_b_re", "ssm_b_im",
         "ssm_c_re", "ssm_c_im", "ssm_d", "attn_branch_norm", "ssm_branch_norm", "norm_ffn")


def _pack_small(arrs, rows=None):
    flat = jnp.concatenate([a.reshape(-1).astype(F32) for a in arrs])
    n = flat.shape[0]
    if rows is None:
        rows = -(-n // (128 * 256)) * 256
    return jnp.pad(flat, (0, rows * 128 - n)).reshape(rows, 128)


def _unpack_small(pack, like):
    flat = pack.reshape(-1)
    out, off = [], 0
    for a in like:
        out.append(flat[off:off + a.size].reshape(a.shape))
        off += a.size
    return out


def _tie(a, token):
    return a if token is None else a + token[0, 0]


class _NoExchange:
    def __init__(self, w):
        self.w = w
        self.grads = {}

    def begin(self, after):
        return None

    def rest_weights(self, after):
        return self.w

    def start(self, group, idxs, grads):
        self.grads.update(zip(idxs, grads))
        return None

    def finish(self, group, after):
        pass


class _Exchange:
    def __init__(self, rest_shards, place):
        self.rest_shards = rest_shards
        self.place = place
        self.rest_dims = [s[1:] for s in SHARDED[1:]]
        self.pending = {}
        self.halves = {}
        self.parts = {}

    def begin(self, after):
        copies, arrivals = _gather_copies(self.rest_dims)
        lands = [lax.empty((r, c), BF) for r, c, _ in self.rest_dims]
        send, recv, srcs, lands, token = _split_copy_start(
            "gather_start", copies, 4 * len(lands), self.rest_shards, lands, after)
        self.pending["gather"] = (copies, arrivals, send, recv, srcs, lands)
        return token

    def rest_weights(self, after):
        copies, arrivals, send, recv, srcs, lands = self.pending.pop("gather")
        _, lands = _split_copy_wait("gather_wait", copies, arrivals, send, recv, srcs, lands, after)
        fulls = gather_forward(lands, self.rest_dims)
        return dict(zip([s[0] for s in SHARDED[1:]], fulls))

    def _halves(self, group, idxs, grads):
        got = pair_exchange("pair_exchange_" + group, grads, idxs)
        return [pair_sum("pair_sum_" + SHARDED[i][0], self.place, g, b, SHARDED[i][3])
                for i, g, b in zip(idxs, grads, got)]

    def start(self, group, idxs, grads):
        halves = self._halves(group, idxs, grads)
        if group == "in":
            for i, h, pt in zip(idxs, halves, scatter_grads(halves, idxs)):
                self.halves[i], self.parts[i] = h, pt
            return None
        copies, arrivals = _scatter_copies(idxs)
        lands = [lax.empty((3,) + sh, BF) for sh in _piece_shapes(idxs)]
        send, recv, srcs, lands, token = _split_copy_start(
            "scatter_start_" + group, copies, 3 * len(idxs), halves, lands, halves[0])
        self.pending[group] = (copies, arrivals, send, recv, srcs, lands, idxs)
        return token

    def finish(self, group, after):
        if group not in self.pending:
            return
        copies, arrivals, send, recv, srcs, lands, idxs = self.pending.pop(group)
        srcs, lands = _split_copy_wait("scatter_wait_" + group, copies, arrivals, send, recv, srcs, lands, after)
        for i, h, pt in zip(idxs, srcs, lands):
            self.halves[i], self.parts[i] = h, pt


def local_step(x, tgt, meta_full, p, w_in, ex):
    mpad = jnp.concatenate([jnp.zeros((META0, D), F32), meta_full], axis=0)
    qn_t = jnp.tile(p["q_norm"].reshape(1, HEAD), (1, NQ)) * HEAD ** -0.5
    kn_t = jnp.tile(p["k_norm"].reshape(1, HEAD), (1, NKV))
    seg = jnp.arange(256) // HEAD
    bd = (seg[:, None] == seg[None, :]).astype(BF)
    nm = p["norm_mix"].reshape(1, D)
    sinks = p["attn_sinks"].reshape(NQ)
    dsk = p["ssm_d"].reshape(1, D)
    abn, sbn, gf = (p[k].reshape(1, D) for k in ("attn_branch_norm", "ssm_branch_norm", "norm_ffn"))

    disc_in = (p["lam_re"], p["lam_im"], p["log_dt"], p["ssm_b_re"], p["ssm_b_im"])
    (ar, ai, bbr, bbi), disc_vjp = jax.vjp(_ssm_discretize, *disc_in)
    pw = _scan_powers(ar, ai)
    brp, bip = _pack_b(bbr).astype(BF), _pack_b(bbi).astype(BF)
    crp, cip = _pack_c(p["ssm_c_re"]).astype(BF), _pack_c(p["ssm_c_im"]).astype(BF)

    token = ex.begin(w_in)
    xn, qr, kr, qn, kn, v, u, gg = in_proj_fwd(x, mpad, w_in, _tie(nm, token), qn_t, kn_t, bd)
    attn = attention_fwd(sinks, qn, kn, v)
    y, sr, si = ssm_fwd(u, pw, brp, bip, crp, cip, dsk)
    w = ex.rest_weights(y)
    zb, zzb, mgb, h1 = mid_fwd(y, attn, gg, x, mpad, w["w_glu"], w["w_out"], abn, sbn)
    hnb, dgub, actb, dh2b, dh1, dh1b, dgf, loss = ffn_fwd_bwd(h1, tgt, w["w_ffn_in"], w["w_ffn_out"], gf)
    token = ex.start("ffn", (3, 4), [weight_grad("grad_w_ffn_in", hnb, dgub, 1408),
                                     weight_grad("grad_w_ffn_out", actb, dh2b, 512)])
    dy, dattn, dggb, dzzb, dabn, dsbn = mid_bwd(dh1b, y, attn, gg, zzb, w["w_glu"], w["w_out"], _tie(abn, token),
                                                sbn)
    grads_mid = [weight_grad("grad_w_glu", zb, dzzb, 1024), weight_grad("grad_w_out", mgb, dh1b, 1024)]
    dub, da, dbr, dbi, dcr, dci, dd = ssm_bwd(dy, u, sr, si, pw, brp, bip, crp, cip, dsk)
    ex.finish("ffn", dub)
    token = ex.start("mid", (1, 2), grads_mid)
    dqn, dkn, dv, dsink = attention_bwd(_tie(sinks, token), qn, kn, v, dattn)
    dproj, dx, dmpad, dnm, dqw, dkw = in_proj_bwd(dqn, dkn, dv, dub, dggb, qr, kr, x, mpad, dh1, w_in, nm, qn_t,
                                                  kn_t, bd)
    ex.finish("mid", dproj)
    ex.start("in", (0,), [weight_grad("grad_w_in", xn, dproj, 1152)])

    dar = jnp.sum(da[0:8], axis=0).reshape(GROUPS, NSTATE)
    dai = jnp.sum(da[8:16], axis=0).reshape(GROUPS, NSTATE)
    dlr, dli, dldt, dbre, dbim = disc_vjp((dar, dai, _unpack_b(dbr), _unpack_b(dbi)))
    small = {
        "norm_mix": dnm, "q_norm": dqw.reshape(NQ, HEAD).sum(0) * HEAD ** -0.5, "k_norm": dkw.reshape(NKV, HEAD).sum(0),
        "attn_sinks": dsink[0, 0:NQ], "lam_re": dlr, "lam_im": dli, "log_dt": dldt,
        "ssm_b_re": dbre, "ssm_b_im": dbim, "ssm_c_re": _unpack_c(dcr), "ssm_c_im": _unpack_c(dci),
        "ssm_d": dd, "attn_branch_norm": dabn, "ssm_branch_norm": dsbn, "norm_ffn": dgf,
    }
    return loss[0, 0], dx, dmpad[META0:PAD], small


def kernel(x, meta_tokens, norm_mix, w_in, q_norm, k_norm, attn_sinks, lam_re, lam_im, log_dt, ssm_b_re, ssm_b_im, ssm_c_re, ssm_c_im, ssm_d, w_glu, attn_branch_norm, ssm_branch_norm, w_out, norm_ffn, w_ffn_in, w_ffn_out, loss_target, m_meta_tokens, m_norm_mix, m_w_in, m_q_norm, m_k_norm, m_attn_sinks, m_lam_re, m_lam_im, m_log_dt, m_ssm_b_re, m_ssm_b_im, m_ssm_c_re, m_ssm_c_im, m_ssm_d, m_w_glu, m_attn_branch_norm, m_ssm_branch_norm, m_w_out, m_norm_ffn, m_w_ffn_in, m_w_ffn_out, v_meta_tokens, v_norm_mix, v_w_in, v_q_norm, v_k_norm, v_attn_sinks, v_lam_re, v_lam_im, v_log_dt, v_ssm_b_re, v_ssm_b_im, v_ssm_c_re, v_ssm_c_im, v_ssm_d, v_w_glu, v_attn_branch_norm, v_ssm_branch_norm, v_w_out, v_norm_ffn, v_w_ffn_in, v_w_ffn_out):
    args = dict(locals())
    names = ["meta_tokens", "norm_mix", "w_in", "q_norm", "k_norm", "attn_sinks", "lam_re", "lam_im", "log_dt",
             "ssm_b_re", "ssm_b_im", "ssm_c_re", "ssm_c_im", "ssm_d", "w_glu", "attn_branch_norm",
             "ssm_branch_norm", "w_out", "norm_ffn", "w_ffn_in", "w_ffn_out"]
    big_names = [s[0] for s in SHARDED]
    xs, ys, cs = _place()
    chip = 2 * xs + ys

    shards = [args[n][0].astype(BF) for n in big_names]
    w_in_full, meta_full = gather_weights([shards[0], meta_tokens], [SHARDED[0][1:], (N_META, D, 1)])
    place = jnp.stack([chip, cs]).astype(jnp.int32)
    ex = _Exchange(shards[1:], place)

    p = {n: args[n][0] for n in SMALL}
    loss, dx, dmeta, gsmall = local_step(x[0], loss_target[0], meta_full, p, w_in_full, ex)
    loss = lax.psum(loss, ("x", "y", "c"))

    mine = [sum4("sum4_" + SHARDED[i][0], place, ex.parts[i], ex.halves[i], SHARDED[i][3]) for i in range(5)]
    gfull = join_halves(mine)
    small_list = [gsmall[n].reshape(args[n].shape) for n in SMALL] + [dmeta]
    red = allreduce_small(_pack_small(small_list))
    small_red = _unpack_small(red, small_list)
    gmeta = lax.dynamic_slice_in_dim(small_red[-1], chip * (D // 4), D // 4, axis=1)

    out = {}
    for n, g in zip(big_names, gfull):
        d, nm_, nv_ = adamw("adamw_" + n, args[n][0], g, args["m_" + n][0], args["v_" + n][0])
        out[n] = tuple(t[None] for t in (g, d, nm_, nv_))
    d, nm_, nv_ = adamw("adamw_meta", meta_tokens, gmeta, m_meta_tokens, v_meta_tokens)
    out["meta_tokens"] = (gmeta, d, nm_, nv_)
    ds, nms, nvs = adamw_small([args[n] for n in SMALL], small_red[:-1], [args["m_" + n] for n in SMALL],
                               [args["v_" + n] for n in SMALL])
    for n, g_, d_, m_, v_ in zip(SMALL, small_red[:-1], ds, nms, nvs):
        out[n] = (g_, d_, m_, v_)

    res = [loss, dx[None]]
    for k in range(4):
        res += [out[n][k] for n in names]
    return tuple(res)
```

```python
import functools
import math

import jax
import jax.numpy as jnp
from jax import lax
from jax.experimental import pallas as pl
from jax.experimental.pallas import tpu as pltpu

F32 = jnp.float32
BF = jnp.bfloat16

D = 1024
N_META = 16
HEAD = 64
NQ = 16
NKV = 4
QW = NQ * HEAD
KVW = NKV * HEAD
WIN = 128
GROUPS = 64
GCH = 16
NSTATE = 64
SW = GROUPS * NSTATE
DFF = 2816
IN_COLS = QW + 2 * KVW + D + 2 * D
PAD = 256
META0 = PAD - N_META
EPS = 1e-6
NEG = -1e30

TM = 256
TS = 128
LC = 512
NLC = SW // LC

LR, B1, B2, AEPS, WD, STEP = 0.001, 0.9, 0.999, 1e-08, 0.01, 10
BC1 = 1.0 - B1 ** STEP
BC2 = 1.0 - B2 ** STEP

MESH = pl.DeviceIdType.MESH
ANY = pl.BlockSpec(memory_space=pl.ANY)


def _cp(sem=None, vmem_mb=48):
    return pltpu.CompilerParams(dimension_semantics=sem, vmem_limit_bytes=vmem_mb << 20)


def _full(shape):
    n = len(shape)
    return pl.BlockSpec(shape, lambda *a: (0,) * n)


def _const(shape):
    n = len(shape)
    return pl.BlockSpec(shape, lambda *a: (0,) * n, pipeline_mode=pl.Buffered(1))


def _rows(tm, width):
    return pl.BlockSpec((tm, width), lambda i: (i, 0))


def _dot(a, b):
    return jnp.dot(a, b, preferred_element_type=F32)


def _dot_nt(a, b):
    return lax.dot_general(a, b, (((1,), (1,)), ((), ())), preferred_element_type=F32)


def _dot_tn(a, b):
    return lax.dot_general(a, b, (((0,), (0,)), ((), ())), preferred_element_type=F32)


def _sigmoid(x):
    return 0.5 * jnp.tanh(0.5 * x) + 0.5


def _seg_mean(x, bd):
    outs = []
    for j in range(x.shape[1] // 256):
        c = x[:, 256 * j:256 * (j + 1)]
        hi = c.astype(BF)
        lo = (c - hi.astype(F32)).astype(BF)
        outs.append(_dot(hi, bd) + _dot(lo, bd))
    out = outs[0] if len(outs) == 1 else jnp.concatenate(outs, axis=1)
    return out * (1.0 / HEAD)


_GC = math.sqrt(2.0 / math.pi)


def _gelu(x):
    t = jnp.tanh(_GC * (x + 0.044715 * x * x * x))
    return 0.5 * x * (1.0 + t), t


def _gelu_grad(x, t):
    return 0.5 * (1.0 + t) + 0.5 * x * (1.0 - t * t) * _GC * (1.0 + 3.0 * 0.044715 * x * x)


def _seq_specs():
    assert TM == PAD
    return [pl.BlockSpec((TM, D), lambda i: (jnp.maximum(i - 1, 0), 0)), _full((PAD, D))]


def _seq_tile(x_ref, m_ref):
    return jnp.where(pl.program_id(0) == 0, m_ref[...], x_ref[...])


def in_proj_fwd(x, mpad, w_in, nm, qn_t, kn_t, bd):
    lp = x.shape[0] + PAD

    def body(x_ref, m_ref, w_ref, nm_ref, qn_ref, kn_ref, bd_ref,
             xn_o, qr_o, kr_o, qn_o, kn_o, v_o, u_o, gg_o):
        h = _seq_tile(x_ref, m_ref)
        r = lax.rsqrt(jnp.mean(h * h, axis=-1, keepdims=True) + EPS)
        xb = ((h * r) * nm_ref[...]).astype(BF)
        xn_o[...] = xb
        bdv = bd_ref[...]
        q = _dot(xb, w_ref[:, 0:QW])
        qr_o[...] = q
        qn_o[...] = (q * lax.rsqrt(_seg_mean(q * q, bdv) + EPS) * qn_ref[...]).astype(BF)
        k = _dot(xb, w_ref[:, QW:QW + KVW])
        kr_o[...] = k
        kn_o[...] = (k * lax.rsqrt(_seg_mean(k * k, bdv) + EPS) * kn_ref[...]).astype(BF)
        v_o[...] = _dot(xb, w_ref[:, QW + KVW:QW + 2 * KVW]).astype(BF)
        u_o[...] = _dot(xb, w_ref[:, QW + 2 * KVW:QW + 2 * KVW + D])
        gg_o[...] = _dot(xb, w_ref[:, QW + 2 * KVW + D:IN_COLS])

    outs = [(D, BF), (QW, F32), (KVW, F32), (QW, BF), (KVW, BF), (KVW, BF), (D, F32), (2 * D, F32)]
    return pl.pallas_call(
        body, name="in_proj_fwd", grid=(lp // TM,),
        in_specs=_seq_specs() + [_full((D, IN_COLS)), _full((1, D)), _full((1, QW)), _full((1, KVW)),
                                 _full((256, 256))],
        out_specs=[_rows(TM, w) for w, _ in outs],
        out_shape=[jax.ShapeDtypeStruct((lp, w), t) for w, t in outs],
        compiler_params=_cp(("arbitrary",)),
    )(x, mpad, w_in, nm, qn_t, kn_t, bd)


def in_proj_bwd(dqn, dkn, dv, du, dgg, qr, kr, x, mpad, dh1, w_in, nm, qn_t, kn_t, bd):
    lp = x.shape[0] + PAD

    def body(dqn_ref, dkn_ref, dv_ref, du_ref, dgg_ref, qr_ref, kr_ref, x_ref, m_ref, dh1_ref,
             w_ref, nm_ref, qn_ref, kn_ref, bd_ref,
             dproj_o, dx_o, dm_o, dnm_o, dqw_o, dkw_o):
        i = pl.program_id(0)

        @pl.when(i == 0)
        def _():
            dnm_o[...] = jnp.zeros_like(dnm_o)
            dqw_o[...] = jnp.zeros_like(dqw_o)
            dkw_o[...] = jnp.zeros_like(dkw_o)

        bdv = bd_ref[...]

        def norm_bwd(x, dy, g, acc_ref):
            rr = lax.rsqrt(_seg_mean(x * x, bdv) + EPS)
            xh = x * rr
            acc_ref[...] += jnp.sum(dy * xh, axis=0, keepdims=True)
            dxh = dy * g
            return rr * (dxh - xh * _seg_mean(dxh * xh, bdv))

        dq = norm_bwd(qr_ref[...], dqn_ref[...], qn_ref[...], dqw_o)
        dk = norm_bwd(kr_ref[...], dkn_ref[...], kn_ref[...], dkw_o)
        dproj_o[:, 0:QW] = dq.astype(BF)
        dproj_o[:, QW:QW + KVW] = dk.astype(BF)
        dproj_o[:, QW + KVW:QW + 2 * KVW] = dv_ref[...].astype(BF)
        dproj_o[:, QW + 2 * KVW:QW + 2 * KVW + D] = du_ref[...]
        dproj_o[:, QW + 2 * KVW + D:IN_COLS] = dgg_ref[...]
        dxn = jnp.zeros((TM, D), F32)
        for c0 in range(0, IN_COLS, 512):
            dxn += _dot_nt(dproj_o[:, c0:c0 + 512], w_ref[:, c0:c0 + 512])
        h = _seq_tile(x_ref, m_ref)
        r = lax.rsqrt(jnp.mean(h * h, axis=-1, keepdims=True) + EPS)
        xh = h * r
        dnm_o[...] += jnp.sum(dxn * xh, axis=0, keepdims=True)
        dxh = dxn * nm_ref[...]
        dh0 = dh1_ref[...] + r * (dxh - xh * jnp.mean(dxh * xh, axis=-1, keepdims=True))
        dx_o[...] = dh0

        @pl.when(i == 0)
        def _():
            dm_o[...] = dh0

    return pl.pallas_call(
        body, name="in_proj_bwd", grid=(lp // TM,),
        in_specs=[_rows(TM, QW), _rows(TM, KVW), _rows(TM, KVW), _rows(TM, D), _rows(TM, 2 * D),
                  _rows(TM, QW), _rows(TM, KVW)] + _seq_specs() + [_rows(TM, D),
                  _full((D, IN_COLS)), _full((1, D)), _full((1, QW)), _full((1, KVW)), _full((256, 256))],
        out_specs=[_rows(TM, IN_COLS), _seq_specs()[0], _full((PAD, D)), _full((1, D)), _full((1, QW)),
                   _full((1, KVW))],
        out_shape=[jax.ShapeDtypeStruct((lp, IN_COLS), BF), jax.ShapeDtypeStruct((lp - PAD, D), F32),
                   jax.ShapeDtypeStruct((PAD, D), F32),
                   jax.ShapeDtypeStruct((1, D), F32), jax.ShapeDtypeStruct((1, QW), F32),
                   jax.ShapeDtypeStruct((1, KVW), F32)],
        compiler_params=_cp(("arbitrary",)),
    )(dqn, dkn, dv, du, dgg, qr, kr, x, mpad, dh1, w_in, nm, qn_t, kn_t, bd)


def _att_masks(b):
    qi = lax.broadcasted_iota(jnp.int32, (4 * WIN, WIN), 0) & (WIN - 1)
    kj = lax.broadcasted_iota(jnp.int32, (4 * WIN, WIN), 1)
    mask_c = (kj <= qi) & (b * WIN + kj >= META0)
    mask_p = (kj > qi) & ((b - 1) * WIN + kj >= PAD)
    mask_m = (kj >= WIN - N_META) & (b * WIN + qi >= PAD)
    return mask_m, mask_p, mask_c


def _dup_half(ref, kp, pos, lo):
    pair = ref[:, 128 * kp:128 * kp + 128].astype(F32)
    rolled = pltpu.roll(pair, 64, axis=1)
    keep = lo if pos == 0 else jnp.logical_not(lo)
    return jnp.where(keep, pair, rolled).astype(BF)


def _stack_heads(ref, kv, lo):
    parts = []
    for pp in range(2):
        pair = ref[:, 256 * kv + 128 * pp:256 * kv + 128 * pp + 128]
        zero = jnp.zeros_like(pair)
        parts.append(jnp.where(lo, pair, zero))
        parts.append(jnp.where(lo, zero, pair))
    return jnp.concatenate(parts, axis=0)


def _unstack_heads(x4, lo):
    a = jnp.where(lo, x4[0:WIN], x4[WIN:2 * WIN])
    b = jnp.where(lo, x4[2 * WIN:3 * WIN], x4[3 * WIN:4 * WIN])
    return a, b


def _sink_col(sink_ref, kv):
    row = lax.broadcasted_iota(jnp.int32, (4 * WIN, 1), 0) >> 7
    col = jnp.full((4 * WIN, 1), sink_ref[4 * kv + 3], F32)
    for r in range(3):
        col = jnp.where(row == r, sink_ref[4 * kv + r], col)
    return col


def _att_probs(q4, kds, masks, sink4):
    ss = [jnp.where(mk, _dot_nt(q4, kd), NEG) for kd, mk in zip(kds, masks)]
    m = jnp.maximum(sink4, jnp.max(jnp.maximum(jnp.maximum(ss[0], ss[1]), ss[2]), axis=-1, keepdims=True))
    es = [jnp.exp(s - m) for s in ss]
    esink = jnp.exp(sink4 - m)
    den = esink + jnp.sum(es[0] + es[1] + es[2], axis=-1, keepdims=True)
    inv = 1.0 / den
    return [e * inv for e in es], esink * inv


def _kv_specs():
    return [pl.BlockSpec((WIN, KVW), lambda b: (1, 0)),
            pl.BlockSpec((WIN, KVW), lambda b: (jnp.maximum(b - 1, 0), 0)),
            pl.BlockSpec((WIN, KVW), lambda b: (b, 0))]


def attention_fwd(sinks, qn, kn, v):
    lp = qn.shape[0]

    def body(sink_ref, q_ref, km_ref, kp_ref, kc_ref, vm_ref, vp_ref, vc_ref, o_ref):
        b = pl.program_id(0)
        masks = _att_masks(b)
        lo = lax.broadcasted_iota(jnp.int32, (WIN, WIN), 1) < HEAD
        for kv in range(NKV):
            kp, pos = kv // 2, kv % 2
            kds = [_dup_half(r, kp, pos, lo) for r in (km_ref, kp_ref, kc_ref)]
            vds = [_dup_half(r, kp, pos, lo) for r in (vm_ref, vp_ref, vc_ref)]
            q4 = _stack_heads(q_ref, kv, lo)
            ps, _ = _att_probs(q4, kds, masks, _sink_col(sink_ref, kv))
            o4 = _dot(ps[0].astype(BF), vds[0]) + _dot(ps[1].astype(BF), vds[1]) + _dot(ps[2].astype(BF), vds[2])
            oa, ob = _unstack_heads(o4, lo)
            o_ref[:, 256 * kv:256 * kv + 128] = oa
            o_ref[:, 256 * kv + 128:256 * kv + 256] = ob

    return pl.pallas_call(
        body, name="attention_fwd", grid=(lp // WIN,),
        in_specs=[pl.BlockSpec(memory_space=pltpu.SMEM), _rows(WIN, QW)] + _kv_specs() + _kv_specs(),
        out_specs=_rows(WIN, QW),
        out_shape=jax.ShapeDtypeStruct((lp, QW), F32),
        compiler_params=_cp(("arbitrary",)),
    )(sinks, qn, kn, kn, kn, v, v, v)


def attention_bwd(sinks, qn, kn, v, do):
    lp = qn.shape[0]
    nb = lp // WIN

    def body(sink_ref, q_ref, km_ref, kp_ref, kc_ref, vm_ref, vp_ref, vc_ref, do_ref,
             dq_ref, dk_hbm, dv_hbm, dsink_ref, dk_acc, dv_acc):
        b = pl.program_id(0)

        @pl.when(b == 0)
        def _():
            dk_acc[...] = jnp.zeros_like(dk_acc)
            dv_acc[...] = jnp.zeros_like(dv_acc)
            dsink_ref[...] = jnp.zeros_like(dsink_ref)

        masks = _att_masks(b)
        lo = lax.broadcasted_iota(jnp.int32, (WIN, WIN), 1) < HEAD
        lane8 = lax.broadcasted_iota(jnp.int32, (8, 128), 1)
        row4 = lax.broadcasted_iota(jnp.int32, (4 * WIN, 1), 0) >> 7
        starts = [WIN, pl.multiple_of(jnp.maximum(b - 1, 0) * WIN, WIN), pl.multiple_of(b * WIN, WIN)]
        for kv in range(NKV):
            kp, pos = kv // 2, kv % 2
            keep = lo if pos == 0 else jnp.logical_not(lo)
            kds = [_dup_half(r, kp, pos, lo) for r in (km_ref, kp_ref, kc_ref)]
            vds = [_dup_half(r, kp, pos, lo) for r in (vm_ref, vp_ref, vc_ref)]
            q4 = _stack_heads(q_ref, kv, lo)
            do4 = _stack_heads(do_ref, kv, lo)
            ps, psink = _att_probs(q4, kds, masks, _sink_col(sink_ref, kv))
            dps = [_dot_nt(do4, vd) for vd in vds]
            delta = jnp.sum(ps[0] * dps[0] + ps[1] * dps[1] + ps[2] * dps[2], axis=-1, keepdims=True)
            dss = [(p * (dp - delta)).astype(BF) for p, dp in zip(ps, dps)]
            dsk = psink * delta
            for r in range(4):
                val = -jnp.sum(jnp.where(row4 == r, dsk, 0.0), axis=0, keepdims=True)
                dsink_ref[...] += jnp.where(lane8 == 4 * kv + r, val, 0.0)
            dq4 = _dot(dss[0], kds[0]) + _dot(dss[1], kds[1]) + _dot(dss[2], kds[2])
            dqa, dqb = _unstack_heads(dq4, lo)
            dq_ref[:, 256 * kv:256 * kv + 128] = dqa
            dq_ref[:, 256 * kv + 128:256 * kv + 256] = dqb
            for x in range(3):
                dkd = _dot_tn(dss[x], q4)
                dvd = _dot_tn(ps[x].astype(BF), do4)
                dkd = jnp.where(keep, dkd + pltpu.roll(dkd, 64, axis=1), 0.0)
                dvd = jnp.where(keep, dvd + pltpu.roll(dvd, 64, axis=1), 0.0)
                dk_acc[pl.ds(starts[x], WIN), 128 * kp:128 * kp + 128] += dkd
                dv_acc[pl.ds(starts[x], WIN), 128 * kp:128 * kp + 128] += dvd

        @pl.when(b == nb - 1)
        def _():
            pltpu.sync_copy(dk_acc, dk_hbm)
            pltpu.sync_copy(dv_acc, dv_hbm)

    return pl.pallas_call(
        body, name="attention_bwd", grid=(nb,),
        in_specs=[pl.BlockSpec(memory_space=pltpu.SMEM), _rows(WIN, QW)] + _kv_specs() + _kv_specs()
                 + [_rows(WIN, QW)],
        out_specs=[_rows(WIN, QW), ANY, ANY, _full((8, 128))],
        out_shape=[jax.ShapeDtypeStruct((lp, QW), F32), jax.ShapeDtypeStruct((lp, KVW), F32),
                   jax.ShapeDtypeStruct((lp, KVW), F32), jax.ShapeDtypeStruct((8, 128), F32)],
        scratch_shapes=[pltpu.VMEM((lp, KVW), F32), pltpu.VMEM((lp, KVW), F32)],
        compiler_params=_cp(("arbitrary",)),
    )(sinks, qn, kn, kn, kn, v, v, v, do)


PW_ROWS = 64
SEG = TS // 8


def _segment_order():
    rho = jnp.arange(TS)
    token = SEG * (rho % 8) + rho // 8
    p = (token[:, None] == jnp.arange(TS)[None, :]).astype(BF)
    return p, p.T


def _reorder(p, x):
    hi = x.astype(BF)
    r1 = x - hi.astype(F32)
    mid = r1.astype(BF)
    lo = (r1 - mid.astype(F32)).astype(BF)
    return _dot(p, hi) + _dot(p, mid) + _dot(p, lo)


def _segment_scan(xr_ref, xi_ref, pw_ref, base, ls, c_r, c_i, reverse, visit=None):
    ar, ai = pw_ref[base:base + 8, ls], pw_ref[base + 8:base + 16, ls]
    order = list(range(SEG - 1, -1, -1)) if reverse else list(range(SEG))
    s_r = s_i = jnp.zeros_like(ar)
    for i in order:
        rows = pl.ds(8 * i, 8)
        s_r, s_i = ar * s_r - ai * s_i + xr_ref[rows, ls], ar * s_i + ai * s_r + xi_ref[rows, ls]
        xr_ref[rows, ls] = s_r
        xi_ref[rows, ls] = s_i
    rid = lax.broadcasted_iota(jnp.int32, s_r.shape, 0)
    if reverse:
        e_r = jnp.where(rid < 7, pltpu.roll(s_r, 7, axis=0), c_r)
        e_i = jnp.where(rid < 7, pltpu.roll(s_i, 7, axis=0), c_i)
    else:
        e_r = jnp.where(rid >= 1, pltpu.roll(s_r, 1, axis=0), c_r)
        e_i = jnp.where(rid >= 1, pltpu.roll(s_i, 1, axis=0), c_i)
    for n, k in enumerate((1, 2, 4)):
        shift = 8 - k if reverse else k
        t_r, t_i = pltpu.roll(e_r, shift, axis=0), pltpu.roll(e_i, shift, axis=0)
        kr = pw_ref[base + 16 + 16 * n:base + 24 + 16 * n, ls]
        ki = pw_ref[base + 24 + 16 * n:base + 32 + 16 * n, ls]
        e_r, e_i = e_r + kr * t_r - ki * t_i, e_i + kr * t_i + ki * t_r
    last = 0 if reverse else 7
    a16r, a16i = pw_ref[base + 16 + last:base + 17 + last, ls], pw_ref[base + 24 + last:base + 25 + last, ls]
    lr, li, fr, fi = (t[last:last + 1] for t in (e_r, e_i, s_r, s_i))
    out = (a16r * lr - a16i * li + fr, a16r * li + a16i * lr + fi)
    d_r, d_i = e_r, e_i
    for i in order:
        rows = pl.ds(8 * i, 8)
        d_r, d_i = ar * d_r - ai * d_i, ar * d_i + ai * d_r
        f_r, f_i = xr_ref[rows, ls] + d_r, xi_ref[rows, ls] + d_i
        xr_ref[rows, ls] = f_r
        xi_ref[rows, ls] = f_i
        if visit is not None:
            visit(i, f_r, f_i)
    return out


def ssm_fwd(u, pw, br, bi, cr, ci, dsk, perm):
    lp = u.shape[0]

    def body(u_ref, pw_ref, br_ref, bi_ref, cr_ref, ci_ref, d_ref, p_ref, pt_ref,
             y_ref, sr_ref, si_ref, car_r, car_i, yp_s):
        i = pl.program_id(0)

        @pl.when(i == 0)
        def _():
            car_r[...] = jnp.zeros_like(car_r)
            car_i[...] = jnp.zeros_like(car_i)

        u = u_ref[...]
        ub = _dot(p_ref[...], u.astype(BF)).astype(BF)
        for j in range(NLC):
            uj = ub[:, 128 * j:128 * j + 128]
            sr_ref[:, LC * j:LC * j + LC] = _dot(uj, br_ref[j])
            si_ref[:, LC * j:LC * j + LC] = _dot(uj, bi_ref[j])
        for j in range(NLC):
            ls = slice(LC * j, LC * j + LC)
            car_r[:, ls], car_i[:, ls] = _segment_scan(sr_ref, si_ref, pw_ref, 0, ls, car_r[:, ls], car_i[:, ls],
                                                       False)
        for j in range(NLC):
            ls = slice(LC * j, LC * j + LC)
            yp_s[:, 128 * j:128 * j + 128] = (_dot(sr_ref[:, ls].astype(BF), cr_ref[j])
                                              - _dot(si_ref[:, ls].astype(BF), ci_ref[j]))
        y_ref[...] = _reorder(pt_ref[...], yp_s[...]) + d_ref[...] * u

    p, pt = perm
    return pl.pallas_call(
        body, name="ssm_fwd", grid=(lp // TS,),
        in_specs=[_rows(TS, D), _full((2 * PW_ROWS, SW)), _full((NLC, 128, LC)), _full((NLC, 128, LC)),
                  _full((NLC, LC, 128)), _full((NLC, LC, 128)), _full((1, D)), _full((TS, TS)), _full((TS, TS))],
        out_specs=[_rows(TS, D), _rows(TS, SW), _rows(TS, SW)],
        out_shape=[jax.ShapeDtypeStruct((lp, D), F32), jax.ShapeDtypeStruct((lp, SW), F32),
                   jax.ShapeDtypeStruct((lp, SW), F32)],
        scratch_shapes=[pltpu.VMEM((1, SW), F32), pltpu.VMEM((1, SW), F32), pltpu.VMEM((TS, D), F32)],
        compiler_params=_cp(("arbitrary",)),
    )(u, pw, br, bi, cr, ci, dsk, p, pt)


def ssm_bwd(dy, u, sr, si, pw, br, bi, cr, ci, dsk, perm):
    lp = u.shape[0]
    nt = lp // TS

    def rev(i):
        return (nt - 1 - i, 0)

    def prev8(i):
        return (jnp.maximum((nt - 1 - i) * (TS // 8) - 1, 0), 0)

    def body(dy_ref, u_ref, sr_ref, si_ref, pr8_ref, pi8_ref, pw_ref, br_ref, bi_ref, cr_ref, ci_ref, d_ref,
             p_ref, pt_ref, du_ref, da_ref, dbr_ref, dbi_ref, dcr_ref, dci_ref, dd_ref,
             gr_s, gi_s, car_r, car_i, dup_s):
        i = pl.program_id(0)

        @pl.when(i == 0)
        def _():
            car_r[...] = jnp.zeros_like(car_r)
            car_i[...] = jnp.zeros_like(car_i)
            for ref in (da_ref, dbr_ref, dbi_ref, dcr_ref, dci_ref, dd_ref):
                ref[...] = jnp.zeros_like(ref)

        keep = 1.0 - (i == nt - 1).astype(F32)
        dy = dy_ref[...]
        u = u_ref[...]
        dd_ref[...] += jnp.sum(dy * u, axis=0, keepdims=True)
        pm = p_ref[...]
        dyb = _dot(pm, dy.astype(BF)).astype(BF)
        ub = _dot(pm, u.astype(BF)).astype(BF)
        for j in range(NLC):
            ls = slice(LC * j, LC * j + LC)
            dyj = dyb[:, 128 * j:128 * j + 128]
            gr_s[:, ls] = _dot_nt(dyj, cr_ref[j])
            gi_s[:, ls] = -_dot_nt(dyj, ci_ref[j])
            dcr_ref[j] += _dot_tn(dyj, sr_ref[:, ls].astype(BF))
            dci_ref[j] -= _dot_tn(dyj, si_ref[:, ls].astype(BF))
        rid = lax.broadcasted_iota(jnp.int32, (8, LC), 0)
        for j in range(NLC):
            ls = slice(LC * j, LC * j + LC)

            def accumulate(i, g_r, g_i):
                if i >= 1:
                    rows = pl.ds(8 * (i - 1), 8)
                    p_r, p_i = sr_ref[rows, ls], si_ref[rows, ls]
                else:
                    rows = pl.ds(8 * (SEG - 1), 8)
                    p_r = jnp.where(rid >= 1, pltpu.roll(sr_ref[rows, ls], 1, axis=0), pr8_ref[7:8, ls] * keep)
                    p_i = jnp.where(rid >= 1, pltpu.roll(si_ref[rows, ls], 1, axis=0), pi8_ref[7:8, ls] * keep)
                da_ref[0:8, ls] += g_r * p_r + g_i * p_i
                da_ref[8:16, ls] += g_i * p_r - g_r * p_i

            car_r[:, ls], car_i[:, ls] = _segment_scan(gr_s, gi_s, pw_ref, PW_ROWS, ls, car_r[:, ls], car_i[:, ls],
                                                       True, accumulate)
        for j in range(NLC):
            ls = slice(LC * j, LC * j + LC)
            uj = ub[:, 128 * j:128 * j + 128]
            grb = gr_s[:, ls].astype(BF)
            gib = gi_s[:, ls].astype(BF)
            dup_s[:, 128 * j:128 * j + 128] = _dot_nt(grb, br_ref[j]) + _dot_nt(gib, bi_ref[j])
            dbr_ref[j] += _dot_tn(uj, grb)
            dbi_ref[j] += _dot_tn(uj, gib)
        du_ref[...] = (_reorder(pt_ref[...], dup_s[...]) + d_ref[...] * dy).astype(BF)

    p, pt = perm
    return pl.pallas_call(
        body, name="ssm_bwd", grid=(nt,),
        in_specs=[pl.BlockSpec((TS, D), rev), pl.BlockSpec((TS, D), rev), pl.BlockSpec((TS, SW), rev),
                  pl.BlockSpec((TS, SW), rev), pl.BlockSpec((8, SW), prev8), pl.BlockSpec((8, SW), prev8),
                  _full((2 * PW_ROWS, SW)), _full((NLC, 128, LC)), _full((NLC, 128, LC)),
                  _full((NLC, LC, 128)), _full((NLC, LC, 128)), _full((1, D)), _full((TS, TS)), _full((TS, TS))],
        out_specs=[pl.BlockSpec((TS, D), rev), _full((16, SW)), _full((NLC, 128, LC)), _full((NLC, 128, LC)),
                   _full((NLC, 128, LC)), _full((NLC, 128, LC)), _full((1, D))],
        out_shape=[jax.ShapeDtypeStruct((lp, D), BF), jax.ShapeDtypeStruct((16, SW), F32),
                   jax.ShapeDtypeStruct((NLC, 128, LC), F32), jax.ShapeDtypeStruct((NLC, 128, LC), F32),
                   jax.ShapeDtypeStruct((NLC, 128, LC), F32), jax.ShapeDtypeStruct((NLC, 128, LC), F32),
                   jax.ShapeDtypeStruct((1, D), F32)],
        scratch_shapes=[pltpu.VMEM((TS, SW), F32), pltpu.VMEM((TS, SW), F32),
                        pltpu.VMEM((1, SW), F32), pltpu.VMEM((1, SW), F32), pltpu.VMEM((TS, D), F32)],
        compiler_params=_cp(("arbitrary",)),
    )(dy, u, sr, si, sr, si, pw, br, bi, cr, ci, dsk, p, pt)


def _merge_parts(attn, zz, gg, abn, sbn):
    za, zb = zz[:, 0:D], zz[:, D:2 * D]
    sgz = _sigmoid(zb)
    ssm = za * sgz
    ra = lax.rsqrt(jnp.mean(attn * attn, axis=-1, keepdims=True) + EPS)
    rs = lax.rsqrt(jnp.mean(ssm * ssm, axis=-1, keepdims=True) + EPS)
    ah, sh = attn * ra, ssm * rs
    sga, sgs = _sigmoid(gg[:, 0:D]), _sigmoid(gg[:, D:2 * D])
    return za, sgz, ra, rs, ah, sh, sga, sgs, ah * abn, sh * sbn


def mid_fwd(y, attn, gg, x, mpad, w_glu, w_out, abn, sbn):
    lp = y.shape[0]

    def body(y_ref, a_ref, gg_ref, x_ref, m_ref, wg_ref, wo_ref, abn_ref, sbn_ref, z_o, zz_o, mg_o, h1_o):
        z, _ = _gelu(y_ref[...])
        zb = z.astype(BF)
        z_o[...] = zb
        zz = _dot(zb, wg_ref[...])
        zz_o[...] = zz.astype(BF)
        _, _, _, _, _, _, sga, sgs, an, sn = _merge_parts(a_ref[...], zz, gg_ref[...], abn_ref[...], sbn_ref[...])
        mb = (sga * an + sgs * sn).astype(BF)
        mg_o[...] = mb
        h1_o[...] = _seq_tile(x_ref, m_ref) + _dot(mb, wo_ref[...])

    return pl.pallas_call(
        body, name="mid_fwd", grid=(lp // TM,),
        in_specs=[_rows(TM, D), _rows(TM, D), _rows(TM, 2 * D)] + _seq_specs() + [_full((D, 2 * D)), _full((D, D)),
                  _full((1, D)), _full((1, D))],
        out_specs=[_rows(TM, D), _rows(TM, 2 * D), _rows(TM, D), _rows(TM, D)],
        out_shape=[jax.ShapeDtypeStruct((lp, D), BF), jax.ShapeDtypeStruct((lp, 2 * D), BF),
                   jax.ShapeDtypeStruct((lp, D), BF), jax.ShapeDtypeStruct((lp, D), F32)],
        compiler_params=_cp(("arbitrary",)),
    )(y, attn, gg, x, mpad, w_glu, w_out, abn, sbn)


def mid_bwd(dh1b, y, attn, gg, zzb, w_glu, w_out, abn, sbn):
    lp = y.shape[0]

    def body(dh_ref, y_ref, a_ref, gg_ref, zz_ref, wg_ref, wo_ref, abn_ref, sbn_ref,
             dy_o, dattn_o, dgg_o, dzz_o, dabn_o, dsbn_o):
        i = pl.program_id(0)

        @pl.when(i == 0)
        def _():
            dabn_o[...] = jnp.zeros_like(dabn_o)
            dsbn_o[...] = jnp.zeros_like(dsbn_o)

        dm = _dot_nt(dh_ref[...], wo_ref[...])
        abn, sbn = abn_ref[...], sbn_ref[...]
        za, sgz, ra, rs, ah, sh, sga, sgs, an, sn = _merge_parts(
            a_ref[...], zz_ref[...].astype(F32), gg_ref[...], abn, sbn)
        dgg_o[:, 0:D] = (dm * an * sga * (1.0 - sga)).astype(BF)
        dgg_o[:, D:2 * D] = (dm * sn * sgs * (1.0 - sgs)).astype(BF)
        da = dm * sga
        dabn_o[...] += jnp.sum(da * ah, axis=0, keepdims=True)
        dah = da * abn
        dattn_o[...] = (ra * (dah - ah * jnp.mean(dah * ah, axis=-1, keepdims=True))).astype(BF)
        ds = dm * sgs
        dsbn_o[...] += jnp.sum(ds * sh, axis=0, keepdims=True)
        dsh = ds * sbn
        dssm = rs * (dsh - sh * jnp.mean(dsh * sh, axis=-1, keepdims=True))
        dzz_o[:, 0:D] = (dssm * sgz).astype(BF)
        dzz_o[:, D:2 * D] = (dssm * za * sgz * (1.0 - sgz)).astype(BF)
        dz = _dot_nt(dzz_o[...], wg_ref[...])
        yv = y_ref[...]
        _, t = _gelu(yv)
        dy_o[...] = dz * _gelu_grad(yv, t)

    return pl.pallas_call(
        body, name="mid_bwd", grid=(lp // TM,),
        in_specs=[_rows(TM, D), _rows(TM, D), _rows(TM, D), _rows(TM, 2 * D), _rows(TM, 2 * D),
                  _full((D, 2 * D)), _full((D, D)), _full((1, D)), _full((1, D))],
        out_specs=[_rows(TM, D), _rows(TM, D), _rows(TM, 2 * D), _rows(TM, 2 * D), _full((1, D)), _full((1, D))],
        out_shape=[jax.ShapeDtypeStruct((lp, D), F32), jax.ShapeDtypeStruct((lp, D), BF),
                   jax.ShapeDtypeStruct((lp, 2 * D), BF), jax.ShapeDtypeStruct((lp, 2 * D), BF),
                   jax.ShapeDtypeStruct((1, D), F32), jax.ShapeDtypeStruct((1, D), F32)],
        compiler_params=_cp(("arbitrary",)),
    )(dh1b, y, attn, gg, zzb, w_glu, w_out, abn, sbn)


def ffn_fwd_bwd(h1, tgt, w_fi, w_fo, gf):
    lp = h1.shape[0]
    tf = 256
    nch = 2
    cw = DFF // nch

    def body(h_ref, t_ref, wi_ref, wo_ref, gf_ref, hn_o, dgu_o, act_o, dh2_o, dh1_o, dh1b_o, dgf_o, loss_o,
             gate_s, up_s):
        i = pl.program_id(0)

        @pl.when(i == 0)
        def _():
            dgf_o[...] = jnp.zeros_like(dgf_o)
            loss_o[...] = jnp.zeros_like(loss_o)

        h = h_ref[...]
        r = lax.rsqrt(jnp.mean(h * h, axis=-1, keepdims=True) + EPS)
        xh = h * r
        hb = (xh * gf_ref[...]).astype(BF)
        hn_o[...] = hb
        h2 = h
        for c in range(nch):
            gate = _dot(hb, wi_ref[:, cw * c:cw * (c + 1)])
            up = _dot(hb, wi_ref[:, DFF + cw * c:DFF + cw * (c + 1)])
            gate_s[:, cw * c:cw * (c + 1)] = gate
            up_s[:, cw * c:cw * (c + 1)] = up
            ab = (gate * _sigmoid(gate) * up).astype(BF)
            act_o[:, cw * c:cw * (c + 1)] = ab
            h2 = h2 + _dot(ab, wo_ref[cw * c:cw * (c + 1), :])
        real = (i >= PAD // tf).astype(F32)
        e = (h2 - t_ref[...]) * real
        loss_o[...] += 0.5 * jnp.sum(jnp.sum(e * e, axis=-1, keepdims=True), axis=0, keepdims=True) * (1.0 / D)
        dh2 = e * (1.0 / D)
        db = dh2.astype(BF)
        dh2_o[...] = db
        dhn = jnp.zeros((tf, D), F32)
        for c in range(nch):
            gate = gate_s[:, cw * c:cw * (c + 1)]
            up = up_s[:, cw * c:cw * (c + 1)]
            sg = _sigmoid(gate)
            dact = _dot_nt(db, wo_ref[cw * c:cw * (c + 1), :])
            dgate = (dact * up * (sg * (1.0 + gate * (1.0 - sg)))).astype(BF)
            dup = (dact * (gate * sg)).astype(BF)
            dgu_o[:, cw * c:cw * (c + 1)] = dgate
            dgu_o[:, DFF + cw * c:DFF + cw * (c + 1)] = dup
            dhn += (_dot_nt(dgate, wi_ref[:, cw * c:cw * (c + 1)])
                    + _dot_nt(dup, wi_ref[:, DFF + cw * c:DFF + cw * (c + 1)]))
        dgf_o[...] += jnp.sum(dhn * xh, axis=0, keepdims=True)
        dxh = dhn * gf_ref[...]
        dh1 = dh2 + r * (dxh - xh * jnp.mean(dxh * xh, axis=-1, keepdims=True))
        dh1_o[...] = dh1
        dh1b_o[...] = dh1.astype(BF)

    return pl.pallas_call(
        body, name="ffn_fwd_bwd", grid=(lp // tf,),
        in_specs=[_rows(tf, D), pl.BlockSpec((tf, D), lambda i: (jnp.maximum(i - PAD // tf, 0), 0)),
                  _const((D, 2 * DFF)), _const((DFF, D)), _full((1, D))],
        out_specs=[_rows(tf, D), _rows(tf, 2 * DFF), _rows(tf, DFF), _rows(tf, D), _rows(tf, D), _rows(tf, D),
                   _full((1, D)), _full((1, 1))],
        out_shape=[jax.ShapeDtypeStruct((lp, D), BF), jax.ShapeDtypeStruct((lp, 2 * DFF), BF),
                   jax.ShapeDtypeStruct((lp, DFF), BF), jax.ShapeDtypeStruct((lp, D), BF),
                   jax.ShapeDtypeStruct((lp, D), F32), jax.ShapeDtypeStruct((lp, D), BF),
                   jax.ShapeDtypeStruct((1, D), F32), jax.ShapeDtypeStruct((1, 1), F32)],
        scratch_shapes=[pltpu.VMEM((tf, DFF), F32), pltpu.VMEM((tf, DFF), F32)],
        compiler_params=_cp(("arbitrary",), vmem_mb=58),
    )(h1, tgt, w_fi, w_fo, gf)


def weight_grad(name, a, b, tn):
    lp, k = a.shape
    n = b.shape[1]
    tr = next(t for t in (768, 512, 384, 256, 128) if lp % t == 0)

    def body(a_ref, b_ref, o_ref):
        @pl.when(pl.program_id(1) == 0)
        def _():
            o_ref[...] = jnp.zeros_like(o_ref)

        o_ref[...] += _dot_tn(a_ref[...], b_ref[...])

    return pl.pallas_call(
        body, name=name, grid=(n // tn, lp // tr),
        in_specs=[pl.BlockSpec((tr, k), lambda j, m: (m, 0)), pl.BlockSpec((tr, tn), lambda j, m: (m, j))],
        out_specs=pl.BlockSpec((k, tn), lambda j, m: (0, j)),
        out_shape=jax.ShapeDtypeStruct((k, n), F32),
        compiler_params=_cp(("arbitrary", "arbitrary")),
    )(a, b)


def _adamw_math(w, g, m, v):
    m = B1 * m + (1.0 - B1) * g
    v = B2 * v + (1.0 - B2) * (g * g)
    delta = -LR * ((m / BC1) / (jnp.sqrt(v / BC2) + AEPS) + WD * w)
    return delta, m, v


def _row_tile(r):
    return next((t for t in (256, 128, 64, 32, 16, 8) if r % t == 0), r)


def adamw(name, w, g, m, v):
    r, c = w.shape
    tr = _row_tile(r)

    def body(w_ref, g_ref, m_ref, v_ref, d_o, m_o, v_o):
        d_o[...], m_o[...], v_o[...] = _adamw_math(w_ref[...], g_ref[...], m_ref[...], v_ref[...])

    spec = pl.BlockSpec((tr, c), lambda i: (i, 0))
    return pl.pallas_call(
        body, name=name, grid=(r // tr,), in_specs=[spec] * 4, out_specs=[spec] * 3,
        out_shape=[jax.ShapeDtypeStruct((r, c), F32)] * 3,
        compiler_params=_cp(("arbitrary",)),
    )(w, g, m, v)


def adamw_small(ws, gs, ms, vs):
    n = len(ws)
    steps = 8

    def spec(a):
        if a.ndim == 4:
            return pl.BlockSpec((1, a.shape[1] // steps) + a.shape[2:], lambda i: (0, i, 0, 0))
        nd = a.ndim
        return pl.BlockSpec(a.shape, lambda i: (0,) * nd)

    def body(*refs):
        w, g, m, v, d_o, m_o, v_o = (refs[k * n:(k + 1) * n] for k in range(7))
        for t in range(n):
            d_o[t][...], m_o[t][...], v_o[t][...] = _adamw_math(w[t][...], g[t][...], m[t][...], v[t][...])

    specs = [spec(a) for a in ws]
    res = pl.pallas_call(
        body, name="adamw_small", grid=(steps,), in_specs=specs * 4, out_specs=specs * 3,
        out_shape=[jax.ShapeDtypeStruct(a.shape, F32) for a in ws] * 3,
        compiler_params=_cp(("arbitrary",)),
    )(*ws, *gs, *ms, *vs)
    return res[:n], res[n:2 * n], res[2 * n:]


def pair_sum(name, place, full, got, ax):
    r, c = got.shape
    tr = _row_tile(r)

    def body(s_ref, a_ref, b_ref, o_ref):
        o_ref[...] = (a_ref[...] + b_ref[...]).astype(BF)

    if ax == 1:
        mine = pl.BlockSpec((tr, c), lambda i, s: (s[1] * (r // tr) + i, 0))
    else:
        mine = pl.BlockSpec((tr, c), lambda i, s: (i, s[1]))
    spec = pl.BlockSpec((tr, c), lambda i, s: (i, 0))
    return pl.pallas_call(
        body, name=name,
        grid_spec=pltpu.PrefetchScalarGridSpec(num_scalar_prefetch=1, grid=(r // tr,), in_specs=[mine, spec],
                                               out_specs=spec),
        out_shape=jax.ShapeDtypeStruct((r, c), BF),
        compiler_params=_cp(("arbitrary",)),
    )(place, full, got)


def sum4(name, place, parts, half, ax):
    _, r, c = parts.shape
    tr = _row_tile(r)

    def body(s_ref, p_ref, h_ref, o_ref):
        o_ref[...] = ((p_ref[0].astype(F32) + p_ref[1].astype(F32))
                      + (p_ref[2].astype(F32) + h_ref[...].astype(F32)))

    if ax == 1:
        own = pl.BlockSpec((tr, c), lambda i, s: (i, s[0]))
        out = pl.BlockSpec((tr, c), lambda i, s: (s[1] * (r // tr) + i, 0))
        shape = (2 * r, c)
    else:
        own = pl.BlockSpec((tr, c), lambda i, s: (s[0] * (r // tr) + i, 0))
        out = pl.BlockSpec((tr, c), lambda i, s: (i, s[1]))
        shape = (r, 2 * c)
    return pl.pallas_call(
        body, name=name,
        grid_spec=pltpu.PrefetchScalarGridSpec(
            num_scalar_prefetch=1, grid=(r // tr,),
            in_specs=[pl.BlockSpec((3, tr, c), lambda i, s: (0, i, 0)), own], out_specs=out),
        out_shape=jax.ShapeDtypeStruct(shape, F32),
        compiler_params=_cp(("arbitrary",)),
    )(place, parts, half)


def _place():
    x, y, c = lax.axis_index("x"), lax.axis_index("y"), lax.axis_index("c")
    return x, y, c


def _other_chips(x, y):
    return [(1 - x, y), (x, 1 - y), (1 - x, 1 - y)]


SHARDED = (("w_in", D, IN_COLS, 1), ("w_glu", D, 2 * D, 1), ("w_out", D, D, 0),
           ("w_ffn_in", D, 2 * DFF, 1), ("w_ffn_out", DFF, D, 0))


def _shard_of(ref, axis, k, size):
    if axis == 0:
        return ref.at[pl.ds(k * size, size), :]
    return ref.at[:, pl.ds(k * size, size)]


def _gather_piece(full, dims, k, h):
    r, cc, ax = dims
    if ax == 0:
        return full.at[pl.ds(k * (r // 4) + h * (r // 8), r // 8), :]
    return full.at[pl.ds(h * (r // 2), r // 2), pl.ds(k * (cc // 4), cc // 4)]


def _rows_half(shard, h):
    rs = shard.shape[0]
    return shard.at[pl.ds(h * (rs // 2), rs // 2), :]


def gather_weights(shards, dims):
    items = [(s,) + tuple(d) for s, d in zip(shards, dims)]
    n = len(items)

    def body(*refs):
        ins, outs = refs[:n], refs[n:2 * n]
        ici_send, ici_recv, d2d_send, d2d_recv, local_sems = refs[2 * n:]
        x, y, c = _place()
        mine = 2 * x + y
        chips = _other_chips(x, y)

        def piece(t, k, h):
            return _gather_piece(outs[t], dims[t], k, h)

        def src_half(t, h):
            return _rows_half(ins[t], h)

        local, first, passed = [], [], []
        for t in range(n):
            _, r, cc, ax = items[t]
            size = (r if ax == 0 else cc) // 4
            loc = pltpu.make_async_remote_copy(
                src_ref=ins[t], dst_ref=_shard_of(outs[t], ax, mine, size), send_sem=local_sems.at[t],
                recv_sem=local_sems.at[n + t], device_id=(x, y, 1 - c), device_id_type=MESH)
            loc.start()
            local.append(loc)
            for j, chip in enumerate(chips):
                cp = pltpu.make_async_remote_copy(
                    src_ref=src_half(t, c), dst_ref=piece(t, mine, c),
                    send_sem=ici_send.at[3 * t + j], recv_sem=ici_recv.at[3 * t + j],
                    device_id=(*chip, c), device_id_type=MESH)
                cp.start()
                first.append(cp)
        for t in range(n):
            for j, (px, py) in enumerate(chips):
                got = piece(t, 2 * px + py, c)
                pltpu.make_async_remote_copy(
                    src_ref=src_half(t, c), dst_ref=got, send_sem=ici_send.at[3 * t + j],
                    recv_sem=ici_recv.at[3 * t + j], device_id=(px, py, c), device_id_type=MESH).wait_recv()
                cp = pltpu.make_async_remote_copy(
                    src_ref=got, dst_ref=got, send_sem=d2d_send.at[3 * t + j], recv_sem=d2d_recv.at[3 * t + j],
                    device_id=(x, y, 1 - c), device_id_type=MESH)
                cp.start()
                passed.append(cp)
        for t in range(n):
            for j, (px, py) in enumerate(chips):
                other = piece(t, 2 * px + py, 1 - c)
                pltpu.make_async_remote_copy(
                    src_ref=other, dst_ref=other, send_sem=d2d_send.at[3 * t + j], recv_sem=d2d_recv.at[3 * t + j],
                    device_id=(x, y, 1 - c), device_id_type=MESH).wait_recv()
        for cp in first + passed:
            cp.wait_send()
        for loc in local:
            loc.wait()

    return pl.pallas_call(
        body, name="gather_weights",
        in_specs=[ANY] * n, out_specs=[ANY] * n,
        out_shape=[jax.ShapeDtypeStruct((r, c), s.dtype) for s, r, c, _ in items],
        scratch_shapes=[pltpu.SemaphoreType.DMA((3 * n,)), pltpu.SemaphoreType.DMA((3 * n,)),
                        pltpu.SemaphoreType.DMA((3 * n,)), pltpu.SemaphoreType.DMA((3 * n,)),
                        pltpu.SemaphoreType.DMA((2 * n,))],
        compiler_params=pltpu.CompilerParams(has_side_effects=True),
    )(*shards)


SEM = pl.BlockSpec(memory_space=pltpu.SEMAPHORE)
HBM = pl.BlockSpec(memory_space=pltpu.HBM)
EFFECT = pltpu.SideEffectType.DATAFLOW_SIDE_EFFECTING


def _in_hbm(a):
    return pltpu.with_memory_space_constraint(a, pltpu.HBM)


def _split_copy_start(name, copies, ncopies, srcs, lands, after):
    ns, nl = len(srcs), len(lands)

    def body(*refs):
        ins, land_refs = refs[:ns], refs[ns:ns + nl]
        send_sems, recv_sems = refs[ns + nl + 1], refs[ns + nl + 2]
        token = refs[-1]
        for k, (src, dst, dev) in enumerate(copies(ins, land_refs)):
            pltpu.make_async_remote_copy(src_ref=src, dst_ref=dst, send_sem=send_sems.at[k],
                                         recv_sem=recv_sems.at[k], device_id=dev, device_id_type=MESH).start()
        token[...] = jnp.zeros_like(token)

    res = pl.pallas_call(
        body, name=name,
        in_specs=[HBM] * (ns + nl) + [ANY],
        out_specs=(SEM, SEM) + (HBM,) * (ns + nl) + (pl.BlockSpec(memory_space=pltpu.VMEM),),
        out_shape=(pltpu.SemaphoreType.DMA((ncopies,)), pltpu.SemaphoreType.DMA((ncopies,)))
        + tuple(pltpu.HBM(a.shape, a.dtype) for a in list(srcs) + list(lands))
        + (jax.ShapeDtypeStruct((8, 128), F32),),
        input_output_aliases={t: 2 + t for t in range(ns + nl)},
        compiler_params=pltpu.CompilerParams(has_side_effects=EFFECT),
    )(*[_in_hbm(a) for a in srcs], *[_in_hbm(a) for a in lands], after)
    return res[0], res[1], list(res[2:2 + ns]), list(res[2 + ns:2 + ns + nl]), res[-1]


def _split_copy_wait(name, copies, arrivals, send_sems, recv_sems, srcs, lands, after):
    ns, nl = len(srcs), len(lands)

    def body(*refs):
        ins, land_refs = refs[:ns], refs[ns:ns + nl]
        send_sems_ref, recv_sems_ref = refs[ns + nl], refs[ns + nl + 1]
        mine = copies(ins, land_refs)
        for k, ((src, dst, dev), got) in enumerate(zip(mine, arrivals(ins, land_refs))):
            pltpu.make_async_remote_copy(src_ref=src, dst_ref=dst, send_sem=send_sems_ref.at[k],
                                         recv_sem=recv_sems_ref.at[k], device_id=dev,
                                         device_id_type=MESH).wait_send()
            pltpu.make_async_remote_copy(src_ref=got, dst_ref=got, send_sem=send_sems_ref.at[k],
                                         recv_sem=recv_sems_ref.at[k], device_id=dev,
                                         device_id_type=MESH).wait_recv()

    res = pl.pallas_call(
        body, name=name,
        in_specs=[HBM] * (ns + nl) + [SEM, SEM, ANY],
        out_specs=(HBM,) * (ns + nl),
        out_shape=tuple(pltpu.HBM(a.shape, a.dtype) for a in list(srcs) + list(lands)),
        input_output_aliases={t: t for t in range(ns + nl)},
        compiler_params=pltpu.CompilerParams(has_side_effects=EFFECT),
    )(*srcs, *lands, send_sems, recv_sems, after)
    return list(res[:ns]), list(res[ns:])


def _gather_copies(dims):
    def copies(ins, lands):
        x, y, c = _place()
        mine = 2 * x + y
        out = []
        for t, d in enumerate(dims):
            size = (d[0] if d[2] == 0 else d[1]) // 4
            for chip in _other_chips(x, y):
                out.append((_rows_half(ins[t], c), _gather_piece(lands[t], d, mine, c), (*chip, c)))
            out.append((ins[t], _shard_of(lands[t], d[2], mine, size), (x, y, 1 - c)))
        return out

    def arrivals(ins, lands):
        x, y, c = _place()
        mine = 2 * x + y
        out = []
        for t, d in enumerate(dims):
            size = (d[0] if d[2] == 0 else d[1]) // 4
            for px, py in _other_chips(x, y):
                out.append(_gather_piece(lands[t], d, 2 * px + py, c))
            out.append(_shard_of(lands[t], d[2], mine, size))
        return out

    return copies, arrivals


def gather_forward(fulls, dims):
    n = len(fulls)

    def body(*refs):
        ins, outs = refs[:n], refs[n:2 * n]
        send_sems, recv_sems = refs[2 * n:]
        x, y, c = _place()
        cps = []
        for t in range(n):
            for j, (px, py) in enumerate(_other_chips(x, y)):
                cp = pltpu.make_async_remote_copy(
                    src_ref=_gather_piece(ins[t], dims[t], 2 * px + py, c),
                    dst_ref=_gather_piece(outs[t], dims[t], 2 * px + py, c),
                    send_sem=send_sems.at[3 * t + j], recv_sem=recv_sems.at[3 * t + j],
                    device_id=(x, y, 1 - c), device_id_type=MESH)
                cp.start()
                cps.append(cp)
        for t in range(n):
            for j, (px, py) in enumerate(_other_chips(x, y)):
                cps[3 * t + j].wait_send()
                other = _gather_piece(outs[t], dims[t], 2 * px + py, 1 - c)
                pltpu.make_async_remote_copy(
                    src_ref=other, dst_ref=other, send_sem=send_sems.at[3 * t + j], recv_sem=recv_sems.at[3 * t + j],
                    device_id=(x, y, 1 - c), device_id_type=MESH).wait_recv()

    return pl.pallas_call(
        body, name="gather_forward",
        in_specs=[ANY] * n, out_specs=[ANY] * n,
        out_shape=[jax.ShapeDtypeStruct(a.shape, a.dtype) for a in fulls],
        input_output_aliases={t: t for t in range(n)},
        scratch_shapes=[pltpu.SemaphoreType.DMA((3 * n,)), pltpu.SemaphoreType.DMA((3 * n,))],
        compiler_params=pltpu.CompilerParams(has_side_effects=True),
    )(*fulls)


def _half_of(ref, ax, h):
    r, c = ref.shape
    if ax == 1:
        return ref.at[pl.ds(h * (r // 2), r // 2), :]
    return ref.at[:, pl.ds(h * (c // 2), c // 2)]


def _half_shape(r, c, ax):
    return (r // 2, c) if ax == 1 else (r, c // 2)


def pair_exchange(name, grads, idxs):
    n = len(grads)
    which = [SHARDED[i] for i in idxs]

    def body(*refs):
        ins, got = refs[:n], refs[n:2 * n]
        send_sems, recv_sems = refs[2 * n:]
        x, y, c = _place()
        cps = []
        for t, (_, _, _, ax) in enumerate(which):
            cp = pltpu.make_async_remote_copy(
                src_ref=_half_of(ins[t], ax, 1 - c), dst_ref=got[t], send_sem=send_sems.at[t],
                recv_sem=recv_sems.at[t], device_id=(x, y, 1 - c), device_id_type=MESH)
            cp.start()
            cps.append(cp)
        for cp in cps:
            cp.wait()

    return pl.pallas_call(
        body, name=name,
        in_specs=[ANY] * n, out_specs=[ANY] * n,
        out_shape=[jax.ShapeDtypeStruct(_half_shape(r, c, ax), F32) for _, r, c, ax in which],
        scratch_shapes=[pltpu.SemaphoreType.DMA((n,)), pltpu.SemaphoreType.DMA((n,))],
        compiler_params=pltpu.CompilerParams(has_side_effects=True),
    )(*grads)


def _piece_shapes(idxs):
    out = []
    for _, r, c, ax in [SHARDED[i] for i in idxs]:
        hr, hc = _half_shape(r, c, ax)
        out.append((hr // 4, hc) if ax == 0 else (hr, hc // 4))
    return out


def _scatter_copies(idxs):
    which = [SHARDED[i] for i in idxs]

    def copies(ins, lands):
        x, y, c = _place()
        out = []
        for t, (_, _, _, ax) in enumerate(which):
            size = ins[t].shape[ax] // 4
            for j, (px, py) in enumerate(_other_chips(x, y)):
                out.append((_shard_of(ins[t], ax, 2 * px + py, size), lands[t].at[j], (px, py, c)))
        return out

    def arrivals(ins, lands):
        return [lands[t].at[j] for t in range(len(which)) for j in range(3)]

    return copies, arrivals


def scatter_grads(halves, idxs):
    n = len(halves)
    which = [SHARDED[i] for i in idxs]

    def shapes():
        return _piece_shapes(idxs)

    def body(*refs):
        ins, outs = refs[:n], refs[n:2 * n]
        send_sems, recv_sems = refs[2 * n:]
        x, y, c = _place()
        started = []
        for t, (src, dst, (_, _, _, ax)) in enumerate(zip(ins, outs, which)):
            size = src.shape[ax] // 4
            for j, (px, py) in enumerate(_other_chips(x, y)):
                cp = pltpu.make_async_remote_copy(
                    src_ref=_shard_of(src, ax, 2 * px + py, size), dst_ref=dst.at[j],
                    send_sem=send_sems.at[3 * t + j], recv_sem=recv_sems.at[3 * t + j],
                    device_id=(px, py, c), device_id_type=MESH)
                cp.start()
                started.append(cp)
        for cp in started:
            cp.wait()

    return pl.pallas_call(
        body, name="scatter_grads",
        in_specs=[ANY] * n, out_specs=[ANY] * n,
        out_shape=[jax.ShapeDtypeStruct((3,) + s, BF) for s in shapes()],
        scratch_shapes=[pltpu.SemaphoreType.DMA((3 * n,)), pltpu.SemaphoreType.DMA((3 * n,))],
        compiler_params=pltpu.CompilerParams(has_side_effects=True),
    )(*halves)


def join_halves(shards):
    n = len(shards)

    def body(*refs):
        ins, outs = refs[:n], refs[n:2 * n]
        send_sems, recv_sems = refs[2 * n:]
        x, y, c = _place()
        cps = []
        for t, (_, _, _, ax) in enumerate(SHARDED):
            cp = pltpu.make_async_remote_copy(
                src_ref=_half_of(ins[t], ax, c), dst_ref=_half_of(outs[t], ax, c), send_sem=send_sems.at[t],
                recv_sem=recv_sems.at[t], device_id=(x, y, 1 - c), device_id_type=MESH)
            cp.start()
            cps.append(cp)
        for t, cp in enumerate(cps):
            cp.wait_send()
            theirs = _half_of(outs[t], SHARDED[t][3], 1 - c)
            pltpu.make_async_remote_copy(
                src_ref=theirs, dst_ref=theirs, send_sem=send_sems.at[t], recv_sem=recv_sems.at[t],
                device_id=(x, y, 1 - c), device_id_type=MESH).wait_recv()

    return pl.pallas_call(
        body, name="join_halves",
        in_specs=[ANY] * n, out_specs=[ANY] * n,
        out_shape=[jax.ShapeDtypeStruct(s.shape, F32) for s in shards],
        input_output_aliases={t: t for t in range(n)},
        scratch_shapes=[pltpu.SemaphoreType.DMA((n,)), pltpu.SemaphoreType.DMA((n,))],
        compiler_params=pltpu.CompilerParams(has_side_effects=True),
    )(*shards)


def allreduce_small(pack):
    rows = pack.shape[0]
    hr = rows // 2

    def body(p_ref, o_ref, sib_ref, parts_ref, send_sems, recv_sems):
        x, y, c = _place()
        mine = 2 * x + y
        sibling = (x, y, 1 - c)
        swap = pltpu.make_async_remote_copy(
            src_ref=p_ref, dst_ref=sib_ref, send_sem=send_sems.at[0], recv_sem=recv_sems.at[0],
            device_id=sibling, device_id_type=MESH)
        swap.start()
        swap.wait()
        half = pl.ds(pl.multiple_of(c * hr, 8), hr)
        parts_ref[mine] = p_ref[half, :] + sib_ref[half, :]
        cps = []
        for j, chip in enumerate(_other_chips(x, y)):
            cp = pltpu.make_async_remote_copy(
                src_ref=parts_ref.at[mine], dst_ref=parts_ref.at[mine], send_sem=send_sems.at[1 + j],
                recv_sem=recv_sems.at[1 + j], device_id=(*chip, c), device_id_type=MESH)
            cp.start()
            cps.append(cp)
        for j, (px, py) in enumerate(_other_chips(x, y)):
            cps[j].wait_send()
            theirs = parts_ref.at[2 * px + py]
            pltpu.make_async_remote_copy(
                src_ref=theirs, dst_ref=theirs, send_sem=send_sems.at[1 + j], recv_sem=recv_sems.at[1 + j],
                device_id=(px, py, c), device_id_type=MESH).wait_recv()
        o_ref[half, :] = (parts_ref[0] + parts_ref[1]) + (parts_ref[2] + parts_ref[3])
        back = pltpu.make_async_remote_copy(
            src_ref=o_ref.at[half, :], dst_ref=o_ref.at[half, :], send_sem=send_sems.at[4],
            recv_sem=recv_sems.at[4], device_id=sibling, device_id_type=MESH)
        back.start()
        back.wait_send()
        other = pl.ds(pl.multiple_of((1 - c) * hr, 8), hr)
        pltpu.make_async_remote_copy(
            src_ref=o_ref.at[other, :], dst_ref=o_ref.at[other, :], send_sem=send_sems.at[4],
            recv_sem=recv_sems.at[4], device_id=sibling, device_id_type=MESH).wait_recv()

    return pl.pallas_call(
        body, name="allreduce_small",
        in_specs=[pl.BlockSpec(memory_space=pltpu.VMEM)], out_specs=pl.BlockSpec(memory_space=pltpu.VMEM),
        out_shape=jax.ShapeDtypeStruct((rows, 128), F32),
        scratch_shapes=[pltpu.VMEM((rows, 128), F32), pltpu.VMEM((4, hr, 128), F32),
                        pltpu.SemaphoreType.DMA((5,)), pltpu.SemaphoreType.DMA((5,))],
        compiler_params=pltpu.CompilerParams(has_side_effects=True, vmem_limit_bytes=40 << 20),
    )(pack)


def _ssm_discretize(lam_re, lam_im, log_dt, b_re, b_im):
    dt = jnp.exp(log_dt)[:, None]
    mag = jnp.exp(lam_re * dt)
    ar, ai = mag * jnp.cos(lam_im * dt), mag * jnp.sin(lam_im * dt)
    den = lam_re * lam_re + lam_im * lam_im
    nr, ni = ar - 1.0, ai
    fr, fi = (nr * lam_re + ni * lam_im) / den, (ni * lam_re - nr * lam_im) / den
    bbr = fr[..., None] * b_re - fi[..., None] * b_im
    bbi = fr[..., None] * b_im + fi[..., None] * b_re
    return ar, ai, bbr, bbi


def _cmul(a, b):
    return a[0] * b[0] - a[1] * b[1], a[0] * b[1] + a[1] * b[0]


def _scan_powers(ar, ai):
    a = (ar.reshape(1, SW), ai.reshape(1, SW))
    big = a
    for _ in range(SEG.bit_length() - 1):
        big = _cmul(big, big)
    assert 1 << (SEG.bit_length() - 1) == SEG
    pows = {1: big}
    pows[2] = _cmul(big, big)
    pows[4] = _cmul(pows[2], pows[2])
    rid = jnp.arange(8)[:, None]
    ones = jnp.ones((8, 1), F32)
    fwd, rev = [ones * a[0], ones * a[1]], [ones * a[0], -ones * a[1]]
    for k in (1, 2, 4):
        pr, pi = pows[k]
        fwd += [jnp.where(rid >= k, pr, 0.0), jnp.where(rid >= k, pi, 0.0)]
        rev += [jnp.where(rid < 8 - k, pr, 0.0), jnp.where(rid < 8 - k, -pi, 0.0)]
    return jnp.concatenate(fwd + rev, axis=0)


def _pack_b(bb):
    t = bb.reshape(NLC, 8, NSTATE, GCH).transpose(0, 1, 3, 2)
    return jnp.einsum("jgcp,gh->jgchp", t, jnp.eye(8, dtype=bb.dtype)).reshape(NLC, 128, LC)


def _unpack_b(db):
    t = jnp.einsum("jgchp,gh->jgcp", db.reshape(NLC, 8, GCH, 8, NSTATE), jnp.eye(8, dtype=db.dtype))
    return t.transpose(0, 1, 3, 2).reshape(GROUPS, NSTATE, GCH)


def _pack_c(cc):
    t = cc.reshape(NLC, 8, GCH, NSTATE).transpose(0, 1, 3, 2)
    return jnp.einsum("jgpc,gh->jgphc", t, jnp.eye(8, dtype=cc.dtype)).reshape(NLC, LC, 128)


def _unpack_c(dc):
    t = jnp.einsum("jgphc,gh->jgpc", dc.reshape(NLC, 8, NSTATE, 8, GCH), jnp.eye(8, dtype=dc.dtype))
    return t.transpose(0, 1, 3, 2).reshape(GROUPS, GCH, NSTATE)


SMALL = ("norm_mix", "q_norm", "k_norm", "attn_sinks", "lam_re", "lam_im", "log_dt", "ssm_b_re", "ssm_b_im",
         "ssm_c_re", "ssm_c_im", "ssm_d", "attn_branch_norm", "ssm_branch_norm", "norm_ffn")


def _pack_small(arrs, rows=None):
    flat = jnp.concatenate([a.reshape(-1).astype(F32) for a in arrs])
    n = flat.shape[0]
    if rows is None:
        rows = -(-n // (128 * 256)) * 256
    return jnp.pad(flat, (0, rows * 128 - n)).reshape(rows, 128)


def _unpack_small(pack, like):
    flat = pack.reshape(-1)
    out, off = [], 0
    for a in like:
        out.append(flat[off:off + a.size].reshape(a.shape))
        off += a.size
    return out


def _tie(a, token):
    return a if token is None else a + token[0, 0]


class _NoExchange:
    def __init__(self, w):
        self.w = w
        self.grads = {}

    def begin(self, after):
        return None

    def rest_weights(self, after):
        return self.w

    def start(self, group, idxs, grads):
        self.grads.update(zip(idxs, grads))
        return None

    def finish(self, group, after):
        pass


class _Exchange:
    def __init__(self, rest_shards, place):
        self.rest_shards = rest_shards
        self.place = place
        self.rest_dims = [s[1:] for s in SHARDED[1:]]
        self.pending = {}
        self.halves = {}
        self.parts = {}

    def begin(self, after):
        copies, arrivals = _gather_copies(self.rest_dims)
        lands = [lax.empty((r, c), BF) for r, c, _ in self.rest_dims]
        send, recv, srcs, lands, token = _split_copy_start(
            "gather_start", copies, 4 * len(lands), self.rest_shards, lands, after)
        self.pending["gather"] = (copies, arrivals, send, recv, srcs, lands)
        return token

    def rest_weights(self, after):
        copies, arrivals, send, recv, srcs, lands = self.pending.pop("gather")
        _, lands = _split_copy_wait("gather_wait", copies, arrivals, send, recv, srcs, lands, after)
        fulls = gather_forward(lands, self.rest_dims)
        return dict(zip([s[0] for s in SHARDED[1:]], fulls))

    def _halves(self, group, idxs, grads):
        got = pair_exchange("pair_exchange_" + group, grads, idxs)
        return [pair_sum("pair_sum_" + SHARDED[i][0], self.place, g, b, SHARDED[i][3])
                for i, g, b in zip(idxs, grads, got)]

    def start(self, group, idxs, grads):
        halves = self._halves(group, idxs, grads)
        if group == "in":
            for i, h, pt in zip(idxs, halves, scatter_grads(halves, idxs)):
                self.halves[i], self.parts[i] = h, pt
            return None
        copies, arrivals = _scatter_copies(idxs)
        lands = [lax.empty((3,) + sh, BF) for sh in _piece_shapes(idxs)]
        send, recv, srcs, lands, token = _split_copy_start(
            "scatter_start_" + group, copies, 3 * len(idxs), halves, lands, halves[0])
        self.pending[group] = (copies, arrivals, send, recv, srcs, lands, idxs)
        return token

    def finish(self, group, after):
        if group not in self.pending:
            return
        copies, arrivals, send, recv, srcs, lands, idxs = self.pending.pop(group)
        srcs, lands = _split_copy_wait("scatter_wait_" + group, copies, arrivals, send, recv, srcs, lands, after)
        for i, h, pt in zip(idxs, srcs, lands):
            self.halves[i], self.parts[i] = h, pt


def local_step(x, tgt, meta_full, p, w_in, ex):
    mpad = jnp.concatenate([jnp.zeros((META0, D), F32), meta_full], axis=0)
    qn_t = jnp.tile(p["q_norm"].reshape(1, HEAD), (1, NQ)) * HEAD ** -0.5
    kn_t = jnp.tile(p["k_norm"].reshape(1, HEAD), (1, NKV))
    seg = jnp.arange(256) // HEAD
    bd = (seg[:, None] == seg[None, :]).astype(BF)
    nm = p["norm_mix"].reshape(1, D)
    sinks = p["attn_sinks"].reshape(NQ)
    dsk = p["ssm_d"].reshape(1, D)
    abn, sbn, gf = (p[k].reshape(1, D) for k in ("attn_branch_norm", "ssm_branch_norm", "norm_ffn"))

    disc_in = (p["lam_re"], p["lam_im"], p["log_dt"], p["ssm_b_re"], p["ssm_b_im"])
    (ar, ai, bbr, bbi), disc_vjp = jax.vjp(_ssm_discretize, *disc_in)
    pw = _scan_powers(ar, ai)
    brp, bip = _pack_b(bbr).astype(BF), _pack_b(bbi).astype(BF)
    crp, cip = _pack_c(p["ssm_c_re"]).astype(BF), _pack_c(p["ssm_c_im"]).astype(BF)

    token = ex.begin(w_in)
    xn, qr, kr, qn, kn, v, u, gg = in_proj_fwd(x, mpad, w_in, _tie(nm, token), qn_t, kn_t, bd)
    attn = attention_fwd(sinks, qn, kn, v)
    perm = _segment_order()
    y, sr, si = ssm_fwd(u, pw, brp, bip, crp, cip, dsk, perm)
    w = ex.rest_weights(y)
    zb, zzb, mgb, h1 = mid_fwd(y, attn, gg, x, mpad, w["w_glu"], w["w_out"], abn, sbn)
    hnb, dgub, actb, dh2b, dh1, dh1b, dgf, loss = ffn_fwd_bwd(h1, tgt, w["w_ffn_in"], w["w_ffn_out"], gf)
    token = ex.start("ffn", (3, 4), [weight_grad("grad_w_ffn_in", hnb, dgub, 1408),
                                     weight_grad("grad_w_ffn_out", actb, dh2b, 512)])
    dy, dattn, dggb, dzzb, dabn, dsbn = mid_bwd(dh1b, y, attn, gg, zzb, w["w_glu"], w["w_out"], _tie(abn, token),
                                                sbn)
    grads_mid = [weight_grad("grad_w_glu", zb, dzzb, 1024), weight_grad("grad_w_out", mgb, dh1b, 1024)]
    dub, da, dbr, dbi, dcr, dci, dd = ssm_bwd(dy, u, sr, si, pw, brp, bip, crp, cip, dsk, perm)
    ex.finish("ffn", dub)
    token = ex.start("mid", (1, 2), grads_mid)
    dqn, dkn, dv, dsink = attention_bwd(_tie(sinks, token), qn, kn, v, dattn)
    dproj, dx, dmpad, dnm, dqw, dkw = in_proj_bwd(dqn, dkn, dv, dub, dggb, qr, kr, x, mpad, dh1, w_in, nm, qn_t,
                                                  kn_t, bd)
    ex.finish("mid", dproj)
    ex.start("in", (0,), [weight_grad("grad_w_in", xn, dproj, 1152)])

    dar = jnp.sum(da[0:8], axis=0).reshape(GROUPS, NSTATE)
    dai = jnp.sum(da[8:16], axis=0).reshape(GROUPS, NSTATE)
    dlr, dli, dldt, dbre, dbim = disc_vjp((dar, dai, _unpack_b(dbr), _unpack_b(dbi)))
    small = {
        "norm_mix": dnm, "q_norm": dqw.reshape(NQ, HEAD).sum(0) * HEAD ** -0.5, "k_norm": dkw.reshape(NKV, HEAD).sum(0),
        "attn_sinks": dsink[0, 0:NQ], "lam_re": dlr, "lam_im": dli, "log_dt": dldt,
        "ssm_b_re": dbre, "ssm_b_im": dbim, "ssm_c_re": _unpack_c(jnp.swapaxes(dcr, 1, 2)), "ssm_c_im": _unpack_c(jnp.swapaxes(dci, 1, 2)),
        "ssm_d": dd, "attn_branch_norm": dabn, "ssm_branch_norm": dsbn, "norm_ffn": dgf,
    }
    return loss[0, 0], dx, dmpad[META0:PAD], small


def kernel(x, meta_tokens, norm_mix, w_in, q_norm, k_norm, attn_sinks, lam_re, lam_im, log_dt, ssm_b_re, ssm_b_im, ssm_c_re, ssm_c_im, ssm_d, w_glu, attn_branch_norm, ssm_branch_norm, w_out, norm_ffn, w_ffn_in, w_ffn_out, loss_target, m_meta_tokens, m_norm_mix, m_w_in, m_q_norm, m_k_norm, m_attn_sinks, m_lam_re, m_lam_im, m_log_dt, m_ssm_b_re, m_ssm_b_im, m_ssm_c_re, m_ssm_c_im, m_ssm_d, m_w_glu, m_attn_branch_norm, m_ssm_branch_norm, m_w_out, m_norm_ffn, m_w_ffn_in, m_w_ffn_out, v_meta_tokens, v_norm_mix, v_w_in, v_q_norm, v_k_norm, v_attn_sinks, v_lam_re, v_lam_im, v_log_dt, v_ssm_b_re, v_ssm_b_im, v_ssm_c_re, v_ssm_c_im, v_ssm_d, v_w_glu, v_attn_branch_norm, v_ssm_branch_norm, v_w_out, v_norm_ffn, v_w_ffn_in, v_w_ffn_out):
    args = dict(locals())
    names = ["meta_tokens", "norm_mix", "w_in", "q_norm", "k_norm", "attn_sinks", "lam_re", "lam_im", "log_dt",
             "ssm_b_re", "ssm_b_im", "ssm_c_re", "ssm_c_im", "ssm_d", "w_glu", "attn_branch_norm",
             "ssm_branch_norm", "w_out", "norm_ffn", "w_ffn_in", "w_ffn_out"]
    big_names = [s[0] for s in SHARDED]
    xs, ys, cs = _place()
    chip = 2 * xs + ys

    shards = [args[n][0].astype(BF) for n in big_names]
    w_in_full, meta_full = gather_weights([shards[0], meta_tokens], [SHARDED[0][1:], (N_META, D, 1)])
    place = jnp.stack([chip, cs]).astype(jnp.int32)
    ex = _Exchange(shards[1:], place)

    p = {n: args[n][0] for n in SMALL}
    loss, dx, dmeta, gsmall = local_step(x[0], loss_target[0], meta_full, p, w_in_full, ex)
    loss = lax.psum(loss, ("x", "y", "c"))

    mine = [sum4("sum4_" + SHARDED[i][0], place, ex.parts[i], ex.halves[i], SHARDED[i][3]) for i in range(5)]
    gfull = join_halves(mine)
    small_list = [gsmall[n].reshape(args[n].shape) for n in SMALL] + [dmeta]
    red = allreduce_small(_pack_small(small_list))
    small_red = _unpack_small(red, small_list)
    gmeta = lax.dynamic_slice_in_dim(small_red[-1], chip * (D // 4), D // 4, axis=1)

    out = {}
    for n, g in zip(big_names, gfull):
        d, nm_, nv_ = adamw("adamw_" + n, args[n][0], g, args["m_" + n][0], args["v_" + n][0])
        out[n] = tuple(t[None] for t in (g, d, nm_, nv_))
    d, nm_, nv_ = adamw("adamw_meta", meta_tokens, gmeta, m_meta_tokens, v_meta_tokens)
    out["meta_tokens"] = (gmeta, d, nm_, nv_)
    ds, nms, nvs = adamw_small([args[n] for n in SMALL], small_red[:-1], [args["m_" + n] for n in SMALL],
                               [args["v_" + n] for n in SMALL])
    for n, g_, d_, m_, v_ in zip(SMALL, small_red[:-1], ds, nms, nvs):
        out[n] = (g_, d_, m_, v_)

    res = [loss, dx[None]]
    for k in range(4):
        res += [out[n][k] for n in names]
    return tuple(res)
```

```python
import functools
import math

import jax
import jax.numpy as jnp
from jax import lax
from jax.experimental import pallas as pl
from jax.experimental.pallas import tpu as pltpu

F32 = jnp.float32
BF = jnp.bfloat16

D = 1024
N_META = 16
HEAD = 64
NQ = 16
NKV = 4
QW = NQ * HEAD
KVW = NKV * HEAD
WIN = 128
GROUPS = 64
GCH = 16
NSTATE = 64
SW = GROUPS * NSTATE
DFF = 2816
IN_COLS = QW + 2 * KVW + D + 2 * D
PAD = 256
META0 = PAD - N_META
EPS = 1e-6
NEG = -1e30

TM = 256
TS = 256
LC = 512
NLC = SW // LC

LR, B1, B2, AEPS, WD, STEP = 0.001, 0.9, 0.999, 1e-08, 0.01, 10
BC1 = 1.0 - B1 ** STEP
BC2 = 1.0 - B2 ** STEP

MESH = pl.DeviceIdType.MESH
ANY = pl.BlockSpec(memory_space=pl.ANY)


def _cp(sem=None, vmem_mb=48):
    return pltpu.CompilerParams(dimension_semantics=sem, vmem_limit_bytes=vmem_mb << 20)


def _full(shape):
    n = len(shape)
    return pl.BlockSpec(shape, lambda *a: (0,) * n)


def _const(shape):
    n = len(shape)
    return pl.BlockSpec(shape, lambda *a: (0,) * n, pipeline_mode=pl.Buffered(1))


def _rows(tm, width):
    return pl.BlockSpec((tm, width), lambda i: (i, 0))


def _dot(a, b):
    return jnp.dot(a, b, preferred_element_type=F32)


def _dot_nt(a, b):
    return lax.dot_general(a, b, (((1,), (1,)), ((), ())), preferred_element_type=F32)


def _dot_tn(a, b):
    return lax.dot_general(a, b, (((0,), (0,)), ((), ())), preferred_element_type=F32)


def _sigmoid(x):
    return 0.5 * jnp.tanh(0.5 * x) + 0.5


def _seg_mean(x, bd):
    outs = []
    for j in range(x.shape[1] // 256):
        c = x[:, 256 * j:256 * (j + 1)]
        hi = c.astype(BF)
        lo = (c - hi.astype(F32)).astype(BF)
        outs.append(_dot(hi, bd) + _dot(lo, bd))
    out = outs[0] if len(outs) == 1 else jnp.concatenate(outs, axis=1)
    return out * (1.0 / HEAD)


_GC = math.sqrt(2.0 / math.pi)


def _gelu(x):
    t = jnp.tanh(_GC * (x + 0.044715 * x * x * x))
    return 0.5 * x * (1.0 + t), t


def _gelu_grad(x, t):
    return 0.5 * (1.0 + t) + 0.5 * x * (1.0 - t * t) * _GC * (1.0 + 3.0 * 0.044715 * x * x)


def _seq_specs():
    assert TM == PAD
    return [pl.BlockSpec((TM, D), lambda i: (jnp.maximum(i - 1, 0), 0)), _full((PAD, D))]


def _seq_tile(x_ref, m_ref):
    return jnp.where(pl.program_id(0) == 0, m_ref[...], x_ref[...])


def in_proj_fwd(x, mpad, w_in, nm, qn_t, kn_t, bd):
    lp = x.shape[0] + PAD

    def body(x_ref, m_ref, w_ref, nm_ref, qn_ref, kn_ref, bd_ref,
             xn_o, qr_o, kr_o, qn_o, kn_o, v_o, u_o, gg_o):
        h = _seq_tile(x_ref, m_ref)
        r = lax.rsqrt(jnp.mean(h * h, axis=-1, keepdims=True) + EPS)
        xb = ((h * r) * nm_ref[...]).astype(BF)
        xn_o[...] = xb
        bdv = bd_ref[...]
        q = _dot(xb, w_ref[:, 0:QW])
        qr_o[...] = q
        qn_o[...] = (q * lax.rsqrt(_seg_mean(q * q, bdv) + EPS) * qn_ref[...]).astype(BF)
        k = _dot(xb, w_ref[:, QW:QW + KVW])
        kr_o[...] = k
        kn_o[...] = (k * lax.rsqrt(_seg_mean(k * k, bdv) + EPS) * kn_ref[...]).astype(BF)
        v_o[...] = _dot(xb, w_ref[:, QW + KVW:QW + 2 * KVW]).astype(BF)
        u_o[...] = _dot(xb, w_ref[:, QW + 2 * KVW:QW + 2 * KVW + D])
        gg_o[...] = _dot(xb, w_ref[:, QW + 2 * KVW + D:IN_COLS])

    outs = [(D, BF), (QW, F32), (KVW, F32), (QW, BF), (KVW, BF), (KVW, BF), (D, F32), (2 * D, F32)]
    return pl.pallas_call(
        body, name="in_proj_fwd", grid=(lp // TM,),
        in_specs=_seq_specs() + [_full((D, IN_COLS)), _full((1, D)), _full((1, QW)), _full((1, KVW)),
                                 _full((256, 256))],
        out_specs=[_rows(TM, w) for w, _ in outs],
        out_shape=[jax.ShapeDtypeStruct((lp, w), t) for w, t in outs],
        compiler_params=_cp(("arbitrary",)),
    )(x, mpad, w_in, nm, qn_t, kn_t, bd)


def in_proj_bwd(dqn, dkn, dv, du, dgg, qr, kr, x, mpad, dh1, w_in, nm, qn_t, kn_t, bd):
    lp = x.shape[0] + PAD

    def body(dqn_ref, dkn_ref, dv_ref, du_ref, dgg_ref, qr_ref, kr_ref, x_ref, m_ref, dh1_ref,
             w_ref, nm_ref, qn_ref, kn_ref, bd_ref,
             dproj_o, dx_o, dm_o, dnm_o, dqw_o, dkw_o):
        i = pl.program_id(0)

        @pl.when(i == 0)
        def _():
            dnm_o[...] = jnp.zeros_like(dnm_o)
            dqw_o[...] = jnp.zeros_like(dqw_o)
            dkw_o[...] = jnp.zeros_like(dkw_o)

        bdv = bd_ref[...]

        def norm_bwd(x, dy, g, acc_ref):
            rr = lax.rsqrt(_seg_mean(x * x, bdv) + EPS)
            xh = x * rr
            acc_ref[...] += jnp.sum(dy * xh, axis=0, keepdims=True)
            dxh = dy * g
            return rr * (dxh - xh * _seg_mean(dxh * xh, bdv))

        dq = norm_bwd(qr_ref[...], dqn_ref[...], qn_ref[...], dqw_o)
        dk = norm_bwd(kr_ref[...], dkn_ref[...], kn_ref[...], dkw_o)
        dproj_o[:, 0:QW] = dq.astype(BF)
        dproj_o[:, QW:QW + KVW] = dk.astype(BF)
        dproj_o[:, QW + KVW:QW + 2 * KVW] = dv_ref[...].astype(BF)
        dproj_o[:, QW + 2 * KVW:QW + 2 * KVW + D] = du_ref[...]
        dproj_o[:, QW + 2 * KVW + D:IN_COLS] = dgg_ref[...]
        dxn = jnp.zeros((TM, D), F32)
        for c0 in range(0, IN_COLS, 512):
            dxn += _dot_nt(dproj_o[:, c0:c0 + 512], w_ref[:, c0:c0 + 512])
        h = _seq_tile(x_ref, m_ref)
        r = lax.rsqrt(jnp.mean(h * h, axis=-1, keepdims=True) + EPS)
        xh = h * r
        dnm_o[...] += jnp.sum(dxn * xh, axis=0, keepdims=True)
        dxh = dxn * nm_ref[...]
        dh0 = dh1_ref[...] + r * (dxh - xh * jnp.mean(dxh * xh, axis=-1, keepdims=True))
        dx_o[...] = dh0

        @pl.when(i == 0)
        def _():
            dm_o[...] = dh0

    return pl.pallas_call(
        body, name="in_proj_bwd", grid=(lp // TM,),
        in_specs=[_rows(TM, QW), _rows(TM, KVW), _rows(TM, KVW), _rows(TM, D), _rows(TM, 2 * D),
                  _rows(TM, QW), _rows(TM, KVW)] + _seq_specs() + [_rows(TM, D),
                  _full((D, IN_COLS)), _full((1, D)), _full((1, QW)), _full((1, KVW)), _full((256, 256))],
        out_specs=[_rows(TM, IN_COLS), _seq_specs()[0], _full((PAD, D)), _full((1, D)), _full((1, QW)),
                   _full((1, KVW))],
        out_shape=[jax.ShapeDtypeStruct((lp, IN_COLS), BF), jax.ShapeDtypeStruct((lp - PAD, D), F32),
                   jax.ShapeDtypeStruct((PAD, D), F32),
                   jax.ShapeDtypeStruct((1, D), F32), jax.ShapeDtypeStruct((1, QW), F32),
                   jax.ShapeDtypeStruct((1, KVW), F32)],
        compiler_params=_cp(("arbitrary",)),
    )(dqn, dkn, dv, du, dgg, qr, kr, x, mpad, dh1, w_in, nm, qn_t, kn_t, bd)


def _att_masks(b):
    qi = lax.broadcasted_iota(jnp.int32, (4 * WIN, WIN), 0) & (WIN - 1)
    kj = lax.broadcasted_iota(jnp.int32, (4 * WIN, WIN), 1)
    upper = kj > qi
    valid = (upper & ((b - 1) * WIN + kj >= PAD)) | (jnp.logical_not(upper) & (b * WIN + kj >= META0))
    mask_m = (kj >= WIN - N_META) & (b * WIN + qi >= PAD)
    return mask_m, upper, valid


def _dup_half(ref, kp, pos, lo):
    pair = ref[:, 128 * kp:128 * kp + 128].astype(F32)
    rolled = pltpu.roll(pair, 64, axis=1)
    keep = lo if pos == 0 else jnp.logical_not(lo)
    return jnp.where(keep, pair, rolled).astype(BF)


def _stack_heads(ref, kv, lo):
    parts = []
    for pp in range(2):
        pair = ref[:, 256 * kv + 128 * pp:256 * kv + 128 * pp + 128]
        zero = jnp.zeros_like(pair)
        parts.append(jnp.where(lo, pair, zero))
        parts.append(jnp.where(lo, zero, pair))
    return jnp.concatenate(parts, axis=0)


def _unstack_heads(x4, lo):
    a = jnp.where(lo, x4[0:WIN], x4[WIN:2 * WIN])
    b = jnp.where(lo, x4[2 * WIN:3 * WIN], x4[3 * WIN:4 * WIN])
    return a, b


def _sink_col(sink_ref, kv):
    row = lax.broadcasted_iota(jnp.int32, (4 * WIN, 1), 0) >> 7
    col = jnp.full((4 * WIN, 1), sink_ref[4 * kv + 3], F32)
    for r in range(3):
        col = jnp.where(row == r, sink_ref[4 * kv + r], col)
    return col


def _att_probs(q4, kds, masks, sink4):
    mask_m, upper, valid = masks
    s_m = jnp.where(mask_m, _dot_nt(q4, kds[0]), NEG)
    s_pc = jnp.where(valid, jnp.where(upper, _dot_nt(q4, kds[1]), _dot_nt(q4, kds[2])), NEG)
    m = jnp.maximum(sink4, jnp.max(jnp.maximum(s_m, s_pc), axis=-1, keepdims=True))
    e_m, e_pc = jnp.exp(s_m - m), jnp.exp(s_pc - m)
    esink = jnp.exp(sink4 - m)
    inv = 1.0 / (esink + jnp.sum(e_m + e_pc, axis=-1, keepdims=True))
    return e_m * inv, e_pc * inv, esink * inv


def _split_pc(x, upper):
    zero = jnp.zeros_like(x)
    return jnp.where(upper, x, zero).astype(BF), jnp.where(upper, zero, x).astype(BF)


def _kv_specs():
    return [pl.BlockSpec((WIN, KVW), lambda b: (1, 0)),
            pl.BlockSpec((WIN, KVW), lambda b: (jnp.maximum(b - 1, 0), 0)),
            pl.BlockSpec((WIN, KVW), lambda b: (b, 0))]


def attention_fwd(sinks, qn, kn, v):
    lp = qn.shape[0]

    def body(sink_ref, q_ref, km_ref, kp_ref, kc_ref, vm_ref, vp_ref, vc_ref, o_ref):
        b = pl.program_id(0)
        masks = _att_masks(b)
        lo = lax.broadcasted_iota(jnp.int32, (WIN, WIN), 1) < HEAD
        for kv in range(NKV):
            kp, pos = kv // 2, kv % 2
            kds = [_dup_half(r, kp, pos, lo) for r in (km_ref, kp_ref, kc_ref)]
            vds = [_dup_half(r, kp, pos, lo) for r in (vm_ref, vp_ref, vc_ref)]
            q4 = _stack_heads(q_ref, kv, lo)
            p_m, p_pc, _ = _att_probs(q4, kds, masks, _sink_col(sink_ref, kv))
            p_p, p_c = _split_pc(p_pc, masks[1])
            o4 = _dot(p_m.astype(BF), vds[0]) + _dot(p_p, vds[1]) + _dot(p_c, vds[2])
            oa, ob = _unstack_heads(o4, lo)
            o_ref[:, 256 * kv:256 * kv + 128] = oa
            o_ref[:, 256 * kv + 128:256 * kv + 256] = ob

    return pl.pallas_call(
        body, name="attention_fwd", grid=(lp // WIN,),
        in_specs=[pl.BlockSpec(memory_space=pltpu.SMEM), _rows(WIN, QW)] + _kv_specs() + _kv_specs(),
        out_specs=_rows(WIN, QW),
        out_shape=jax.ShapeDtypeStruct((lp, QW), F32),
        compiler_params=_cp(("arbitrary",)),
    )(sinks, qn, kn, kn, kn, v, v, v)


def attention_bwd(sinks, qn, kn, v, do):
    lp = qn.shape[0]
    nb = lp // WIN

    def body(sink_ref, q_ref, km_ref, kp_ref, kc_ref, vm_ref, vp_ref, vc_ref, do_ref,
             dq_ref, dk_hbm, dv_hbm, dsink_ref, dk_acc, dv_acc):
        b = pl.program_id(0)

        @pl.when(b == 0)
        def _():
            dk_acc[...] = jnp.zeros_like(dk_acc)
            dv_acc[...] = jnp.zeros_like(dv_acc)
            dsink_ref[...] = jnp.zeros_like(dsink_ref)

        masks = _att_masks(b)
        lo = lax.broadcasted_iota(jnp.int32, (WIN, WIN), 1) < HEAD
        lane8 = lax.broadcasted_iota(jnp.int32, (8, 128), 1)
        row4 = lax.broadcasted_iota(jnp.int32, (4 * WIN, 1), 0) >> 7
        starts = [WIN, pl.multiple_of(jnp.maximum(b - 1, 0) * WIN, WIN), pl.multiple_of(b * WIN, WIN)]
        for kv in range(NKV):
            kp, pos = kv // 2, kv % 2
            keep = lo if pos == 0 else jnp.logical_not(lo)
            kds = [_dup_half(r, kp, pos, lo) for r in (km_ref, kp_ref, kc_ref)]
            vds = [_dup_half(r, kp, pos, lo) for r in (vm_ref, vp_ref, vc_ref)]
            q4 = _stack_heads(q_ref, kv, lo)
            do4 = _stack_heads(do_ref, kv, lo)
            p_m, p_pc, psink = _att_probs(q4, kds, masks, _sink_col(sink_ref, kv))
            upper = masks[1]
            dp_m = _dot_nt(do4, vds[0])
            dp_pc = jnp.where(upper, _dot_nt(do4, vds[1]), _dot_nt(do4, vds[2]))
            delta = jnp.sum(p_m * dp_m + p_pc * dp_pc, axis=-1, keepdims=True)
            dss = [(p_m * (dp_m - delta)).astype(BF)] + list(_split_pc(p_pc * (dp_pc - delta), upper))
            ps = [p_m.astype(BF)] + list(_split_pc(p_pc, upper))
            dsk = psink * delta
            for r in range(4):
                val = -jnp.sum(jnp.where(row4 == r, dsk, 0.0), axis=0, keepdims=True)
                dsink_ref[...] += jnp.where(lane8 == 4 * kv + r, val, 0.0)
            dq4 = _dot(dss[0], kds[0]) + _dot(dss[1], kds[1]) + _dot(dss[2], kds[2])
            dqa, dqb = _unstack_heads(dq4, lo)
            dq_ref[:, 256 * kv:256 * kv + 128] = dqa
            dq_ref[:, 256 * kv + 128:256 * kv + 256] = dqb
            for x in range(3):
                dkd = _dot_tn(dss[x], q4)
                dvd = _dot_tn(ps[x], do4)
                dkd = jnp.where(keep, dkd + pltpu.roll(dkd, 64, axis=1), 0.0)
                dvd = jnp.where(keep, dvd + pltpu.roll(dvd, 64, axis=1), 0.0)
                dk_acc[pl.ds(starts[x], WIN), 128 * kp:128 * kp + 128] += dkd
                dv_acc[pl.ds(starts[x], WIN), 128 * kp:128 * kp + 128] += dvd

        @pl.when(b == nb - 1)
        def _():
            pltpu.sync_copy(dk_acc, dk_hbm)
            pltpu.sync_copy(dv_acc, dv_hbm)

    return pl.pallas_call(
        body, name="attention_bwd", grid=(nb,),
        in_specs=[pl.BlockSpec(memory_space=pltpu.SMEM), _rows(WIN, QW)] + _kv_specs() + _kv_specs()
                 + [_rows(WIN, QW)],
        out_specs=[_rows(WIN, QW), ANY, ANY, _full((8, 128))],
        out_shape=[jax.ShapeDtypeStruct((lp, QW), F32), jax.ShapeDtypeStruct((lp, KVW), F32),
                   jax.ShapeDtypeStruct((lp, KVW), F32), jax.ShapeDtypeStruct((8, 128), F32)],
        scratch_shapes=[pltpu.VMEM((lp, KVW), F32), pltpu.VMEM((lp, KVW), F32)],
        compiler_params=_cp(("arbitrary",)),
    )(sinks, qn, kn, kn, kn, v, v, v, do)


PW_ROWS = 64
SEG = TS // 8


def _segment_order():
    rho = jnp.arange(TS)
    token = SEG * (rho % 8) + rho // 8
    p = (token[:, None] == jnp.arange(TS)[None, :]).astype(BF)
    return p, p.T


def _reorder(p, x):
    hi = x.astype(BF)
    r1 = x - hi.astype(F32)
    mid = r1.astype(BF)
    lo = (r1 - mid.astype(F32)).astype(BF)
    return _dot(p, hi) + _dot(p, mid) + _dot(p, lo)


def _segment_scan(xr_ref, xi_ref, pw_ref, base, ls, c_r, c_i, reverse, visit=None):
    ar, ai = pw_ref[base:base + 8, ls], pw_ref[base + 8:base + 16, ls]
    order = list(range(SEG - 1, -1, -1)) if reverse else list(range(SEG))
    s_r = s_i = jnp.zeros_like(ar)
    for i in order:
        rows = pl.ds(8 * i, 8)
        s_r, s_i = ar * s_r - ai * s_i + xr_ref[rows, ls], ar * s_i + ai * s_r + xi_ref[rows, ls]
        xr_ref[rows, ls] = s_r
        xi_ref[rows, ls] = s_i
    rid = lax.broadcasted_iota(jnp.int32, s_r.shape, 0)
    if reverse:
        e_r = jnp.where(rid < 7, pltpu.roll(s_r, 7, axis=0), c_r)
        e_i = jnp.where(rid < 7, pltpu.roll(s_i, 7, axis=0), c_i)
    else:
        e_r = jnp.where(rid >= 1, pltpu.roll(s_r, 1, axis=0), c_r)
        e_i = jnp.where(rid >= 1, pltpu.roll(s_i, 1, axis=0), c_i)
    for n, k in enumerate((1, 2, 4)):
        shift = 8 - k if reverse else k
        t_r, t_i = pltpu.roll(e_r, shift, axis=0), pltpu.roll(e_i, shift, axis=0)
        kr = pw_ref[base + 16 + 16 * n:base + 24 + 16 * n, ls]
        ki = pw_ref[base + 24 + 16 * n:base + 32 + 16 * n, ls]
        e_r, e_i = e_r + kr * t_r - ki * t_i, e_i + kr * t_i + ki * t_r
    last = 0 if reverse else 7
    a16r, a16i = pw_ref[base + 16 + last:base + 17 + last, ls], pw_ref[base + 24 + last:base + 25 + last, ls]
    lr, li, fr, fi = (t[last:last + 1] for t in (e_r, e_i, s_r, s_i))
    out = (a16r * lr - a16i * li + fr, a16r * li + a16i * lr + fi)
    d_r, d_i = e_r, e_i
    extra = None
    for i in order:
        rows = pl.ds(8 * i, 8)
        d_r, d_i = ar * d_r - ai * d_i, ar * d_i + ai * d_r
        f_r, f_i = xr_ref[rows, ls] + d_r, xi_ref[rows, ls] + d_i
        xr_ref[rows, ls] = f_r
        xi_ref[rows, ls] = f_i
        if visit is not None:
            extra = visit(i, f_r, f_i, extra)
    return out if visit is None else (out, extra)


def ssm_fwd(u, pw, br, bi, cr, ci, dsk, perm):
    lp = u.shape[0]

    def body(u_ref, pw_ref, br_ref, bi_ref, cr_ref, ci_ref, d_ref, p_ref, pt_ref,
             y_ref, sr_ref, si_ref, car_r, car_i, yp_s):
        i = pl.program_id(0)

        @pl.when(i == 0)
        def _():
            car_r[...] = jnp.zeros_like(car_r)
            car_i[...] = jnp.zeros_like(car_i)

        u = u_ref[...]
        ub = _dot(p_ref[...], u.astype(BF)).astype(BF)
        for j in range(NLC):
            uj = ub[:, 128 * j:128 * j + 128]
            sr_ref[:, LC * j:LC * j + LC] = _dot(uj, br_ref[j])
            si_ref[:, LC * j:LC * j + LC] = _dot(uj, bi_ref[j])
        for j in range(NLC):
            ls = slice(LC * j, LC * j + LC)
            car_r[:, ls], car_i[:, ls] = _segment_scan(sr_ref, si_ref, pw_ref, 0, ls, car_r[:, ls], car_i[:, ls],
                                                       False)
        for j in range(NLC):
            ls = slice(LC * j, LC * j + LC)
            yp_s[:, 128 * j:128 * j + 128] = (_dot(sr_ref[:, ls].astype(BF), cr_ref[j])
                                              - _dot(si_ref[:, ls].astype(BF), ci_ref[j]))
        y_ref[...] = _reorder(pt_ref[...], yp_s[...]) + d_ref[...] * u

    p, pt = perm
    return pl.pallas_call(
        body, name="ssm_fwd", grid=(lp // TS,),
        in_specs=[_rows(TS, D), _full((2 * PW_ROWS, SW)), _full((NLC, 128, LC)), _full((NLC, 128, LC)),
                  _full((NLC, LC, 128)), _full((NLC, LC, 128)), _full((1, D)), _full((TS, TS)), _full((TS, TS))],
        out_specs=[_rows(TS, D), _rows(TS, SW), _rows(TS, SW)],
        out_shape=[jax.ShapeDtypeStruct((lp, D), F32), jax.ShapeDtypeStruct((lp, SW), F32),
                   jax.ShapeDtypeStruct((lp, SW), F32)],
        scratch_shapes=[pltpu.VMEM((1, SW), F32), pltpu.VMEM((1, SW), F32), pltpu.VMEM((TS, D), F32)],
        compiler_params=_cp(("arbitrary",)),
    )(u, pw, br, bi, cr, ci, dsk, p, pt)


def ssm_bwd(dy, u, sr, si, pw, br, bi, cr, ci, dsk, perm):
    lp = u.shape[0]
    nt = lp // TS

    def rev(i):
        return (nt - 1 - i, 0)

    def prev8(i):
        return (jnp.maximum((nt - 1 - i) * (TS // 8) - 1, 0), 0)

    def body(dy_ref, u_ref, sr_ref, si_ref, pr8_ref, pi8_ref, pw_ref, br_ref, bi_ref, cr_ref, ci_ref, d_ref,
             p_ref, pt_ref, du_ref, da_ref, dbr_ref, dbi_ref, dcr_ref, dci_ref, dd_ref,
             gr_s, gi_s, car_r, car_i, dup_s):
        i = pl.program_id(0)

        @pl.when(i == 0)
        def _():
            car_r[...] = jnp.zeros_like(car_r)
            car_i[...] = jnp.zeros_like(car_i)
            for ref in (da_ref, dbr_ref, dbi_ref, dcr_ref, dci_ref, dd_ref):
                ref[...] = jnp.zeros_like(ref)

        keep = 1.0 - (i == nt - 1).astype(F32)
        dy = dy_ref[...]
        u = u_ref[...]
        dd_ref[...] += jnp.sum(dy * u, axis=0, keepdims=True)
        pm = p_ref[...]
        dyb = _dot(pm, dy.astype(BF)).astype(BF)
        ub = _dot(pm, u.astype(BF)).astype(BF)
        for j in range(NLC):
            ls = slice(LC * j, LC * j + LC)
            dyj = dyb[:, 128 * j:128 * j + 128]
            gr_s[:, ls] = _dot_nt(dyj, cr_ref[j])
            gi_s[:, ls] = -_dot_nt(dyj, ci_ref[j])
            dcr_ref[j] += _dot_tn(dyj, sr_ref[:, ls].astype(BF))
            dci_ref[j] -= _dot_tn(dyj, si_ref[:, ls].astype(BF))
        rid = lax.broadcasted_iota(jnp.int32, (8, LC), 0)
        for j in range(NLC):
            ls = slice(LC * j, LC * j + LC)

            def accumulate(i, g_r, g_i, acc):
                if i >= 1:
                    rows = pl.ds(8 * (i - 1), 8)
                    p_r, p_i = sr_ref[rows, ls], si_ref[rows, ls]
                else:
                    rows = pl.ds(8 * (SEG - 1), 8)
                    p_r = jnp.where(rid >= 1, pltpu.roll(sr_ref[rows, ls], 1, axis=0), pr8_ref[7:8, ls] * keep)
                    p_i = jnp.where(rid >= 1, pltpu.roll(si_ref[rows, ls], 1, axis=0), pi8_ref[7:8, ls] * keep)
                t_r, t_i = g_r * p_r + g_i * p_i, g_i * p_r - g_r * p_i
                return (t_r, t_i) if acc is None else (acc[0] + t_r, acc[1] + t_i)

            (car_r[:, ls], car_i[:, ls]), (t_r, t_i) = _segment_scan(
                gr_s, gi_s, pw_ref, PW_ROWS, ls, car_r[:, ls], car_i[:, ls], True, accumulate)
            da_ref[0:8, ls] += t_r
            da_ref[8:16, ls] += t_i
        for j in range(NLC):
            ls = slice(LC * j, LC * j + LC)
            uj = ub[:, 128 * j:128 * j + 128]
            grb = gr_s[:, ls].astype(BF)
            gib = gi_s[:, ls].astype(BF)
            dup_s[:, 128 * j:128 * j + 128] = _dot_nt(grb, br_ref[j]) + _dot_nt(gib, bi_ref[j])
            dbr_ref[j] += _dot_tn(uj, grb)
            dbi_ref[j] += _dot_tn(uj, gib)
        du_ref[...] = (_reorder(pt_ref[...], dup_s[...]) + d_ref[...] * dy).astype(BF)

    p, pt = perm
    return pl.pallas_call(
        body, name="ssm_bwd", grid=(nt,),
        in_specs=[pl.BlockSpec((TS, D), rev), pl.BlockSpec((TS, D), rev), pl.BlockSpec((TS, SW), rev),
                  pl.BlockSpec((TS, SW), rev), pl.BlockSpec((8, SW), prev8), pl.BlockSpec((8, SW), prev8),
                  _full((2 * PW_ROWS, SW)), _full((NLC, 128, LC)), _full((NLC, 128, LC)),
                  _full((NLC, LC, 128)), _full((NLC, LC, 128)), _full((1, D)), _full((TS, TS)), _full((TS, TS))],
        out_specs=[pl.BlockSpec((TS, D), rev), _full((16, SW)), _full((NLC, 128, LC)), _full((NLC, 128, LC)),
                   _full((NLC, 128, LC)), _full((NLC, 128, LC)), _full((1, D))],
        out_shape=[jax.ShapeDtypeStruct((lp, D), BF), jax.ShapeDtypeStruct((16, SW), F32),
                   jax.ShapeDtypeStruct((NLC, 128, LC), F32), jax.ShapeDtypeStruct((NLC, 128, LC), F32),
                   jax.ShapeDtypeStruct((NLC, 128, LC), F32), jax.ShapeDtypeStruct((NLC, 128, LC), F32),
                   jax.ShapeDtypeStruct((1, D), F32)],
        scratch_shapes=[pltpu.VMEM((TS, SW), F32), pltpu.VMEM((TS, SW), F32),
                        pltpu.VMEM((1, SW), F32), pltpu.VMEM((1, SW), F32), pltpu.VMEM((TS, D), F32)],
        compiler_params=_cp(("arbitrary",)),
    )(dy, u, sr, si, sr, si, pw, br, bi, cr, ci, dsk, p, pt)


def _merge_parts(attn, zz, gg, abn, sbn):
    za, zb = zz[:, 0:D], zz[:, D:2 * D]
    sgz = _sigmoid(zb)
    ssm = za * sgz
    ra = lax.rsqrt(jnp.mean(attn * attn, axis=-1, keepdims=True) + EPS)
    rs = lax.rsqrt(jnp.mean(ssm * ssm, axis=-1, keepdims=True) + EPS)
    ah, sh = attn * ra, ssm * rs
    sga, sgs = _sigmoid(gg[:, 0:D]), _sigmoid(gg[:, D:2 * D])
    return za, sgz, ra, rs, ah, sh, sga, sgs, ah * abn, sh * sbn


def mid_fwd(y, attn, gg, x, mpad, w_glu, w_out, abn, sbn):
    lp = y.shape[0]

    def body(y_ref, a_ref, gg_ref, x_ref, m_ref, wg_ref, wo_ref, abn_ref, sbn_ref, z_o, zz_o, mg_o, h1_o):
        z, _ = _gelu(y_ref[...])
        zb = z.astype(BF)
        z_o[...] = zb
        zz = _dot(zb, wg_ref[...])
        zz_o[...] = zz.astype(BF)
        _, _, _, _, _, _, sga, sgs, an, sn = _merge_parts(a_ref[...], zz, gg_ref[...], abn_ref[...], sbn_ref[...])
        mb = (sga * an + sgs * sn).astype(BF)
        mg_o[...] = mb
        h1_o[...] = _seq_tile(x_ref, m_ref) + _dot(mb, wo_ref[...])

    return pl.pallas_call(
        body, name="mid_fwd", grid=(lp // TM,),
        in_specs=[_rows(TM, D), _rows(TM, D), _rows(TM, 2 * D)] + _seq_specs() + [_full((D, 2 * D)), _full((D, D)),
                  _full((1, D)), _full((1, D))],
        out_specs=[_rows(TM, D), _rows(TM, 2 * D), _rows(TM, D), _rows(TM, D)],
        out_shape=[jax.ShapeDtypeStruct((lp, D), BF), jax.ShapeDtypeStruct((lp, 2 * D), BF),
                   jax.ShapeDtypeStruct((lp, D), BF), jax.ShapeDtypeStruct((lp, D), F32)],
        compiler_params=_cp(("arbitrary",)),
    )(y, attn, gg, x, mpad, w_glu, w_out, abn, sbn)


def mid_bwd(dh1b, y, attn, gg, zzb, w_glu, w_out, abn, sbn):
    lp = y.shape[0]

    def body(dh_ref, y_ref, a_ref, gg_ref, zz_ref, wg_ref, wo_ref, abn_ref, sbn_ref,
             dy_o, dattn_o, dgg_o, dzz_o, dabn_o, dsbn_o):
        i = pl.program_id(0)

        @pl.when(i == 0)
        def _():
            dabn_o[...] = jnp.zeros_like(dabn_o)
            dsbn_o[...] = jnp.zeros_like(dsbn_o)

        dm = _dot_nt(dh_ref[...], wo_ref[...])
        abn, sbn = abn_ref[...], sbn_ref[...]
        za, sgz, ra, rs, ah, sh, sga, sgs, an, sn = _merge_parts(
            a_ref[...], zz_ref[...].astype(F32), gg_ref[...], abn, sbn)
        dgg_o[:, 0:D] = (dm * an * sga * (1.0 - sga)).astype(BF)
        dgg_o[:, D:2 * D] = (dm * sn * sgs * (1.0 - sgs)).astype(BF)
        da = dm * sga
        dabn_o[...] += jnp.sum(da * ah, axis=0, keepdims=True)
        dah = da * abn
        dattn_o[...] = (ra * (dah - ah * jnp.mean(dah * ah, axis=-1, keepdims=True))).astype(BF)
        ds = dm * sgs
        dsbn_o[...] += jnp.sum(ds * sh, axis=0, keepdims=True)
        dsh = ds * sbn
        dssm = rs * (dsh - sh * jnp.mean(dsh * sh, axis=-1, keepdims=True))
        dzz_o[:, 0:D] = (dssm * sgz).astype(BF)
        dzz_o[:, D:2 * D] = (dssm * za * sgz * (1.0 - sgz)).astype(BF)
        dz = _dot_nt(dzz_o[...], wg_ref[...])
        yv = y_ref[...]
        _, t = _gelu(yv)
        dy_o[...] = dz * _gelu_grad(yv, t)

    return pl.pallas_call(
        body, name="mid_bwd", grid=(lp // TM,),
        in_specs=[_rows(TM, D), _rows(TM, D), _rows(TM, D), _rows(TM, 2 * D), _rows(TM, 2 * D),
                  _full((D, 2 * D)), _full((D, D)), _full((1, D)), _full((1, D))],
        out_specs=[_rows(TM, D), _rows(TM, D), _rows(TM, 2 * D), _rows(TM, 2 * D), _full((1, D)), _full((1, D))],
        out_shape=[jax.ShapeDtypeStruct((lp, D), F32), jax.ShapeDtypeStruct((lp, D), BF),
                   jax.ShapeDtypeStruct((lp, 2 * D), BF), jax.ShapeDtypeStruct((lp, 2 * D), BF),
                   jax.ShapeDtypeStruct((1, D), F32), jax.ShapeDtypeStruct((1, D), F32)],
        compiler_params=_cp(("arbitrary",)),
    )(dh1b, y, attn, gg, zzb, w_glu, w_out, abn, sbn)


def ffn_fwd_bwd(h1, tgt, w_fi, w_fo, gf):
    lp = h1.shape[0]
    tf = 256
    nch = 2
    cw = DFF // nch

    def body(h_ref, t_ref, wi_ref, wo_ref, gf_ref, hn_o, dgu_o, act_o, dh2_o, dh1_o, dh1b_o, dgf_o, loss_o,
             gate_s, up_s):
        i = pl.program_id(0)

        @pl.when(i == 0)
        def _():
            dgf_o[...] = jnp.zeros_like(dgf_o)
            loss_o[...] = jnp.zeros_like(loss_o)

        h = h_ref[...]
        r = lax.rsqrt(jnp.mean(h * h, axis=-1, keepdims=True) + EPS)
        xh = h * r
        hb = (xh * gf_ref[...]).astype(BF)
        hn_o[...] = hb
        h2 = h
        for c in range(nch):
            gate = _dot(hb, wi_ref[:, cw * c:cw * (c + 1)])
            up = _dot(hb, wi_ref[:, DFF + cw * c:DFF + cw * (c + 1)])
            gate_s[:, cw * c:cw * (c + 1)] = gate
            up_s[:, cw * c:cw * (c + 1)] = up
            ab = (gate * _sigmoid(gate) * up).astype(BF)
            act_o[:, cw * c:cw * (c + 1)] = ab
            h2 = h2 + _dot(ab, wo_ref[cw * c:cw * (c + 1), :])
        real = (i >= PAD // tf).astype(F32)
        e = (h2 - t_ref[...]) * real
        loss_o[...] += 0.5 * jnp.sum(jnp.sum(e * e, axis=-1, keepdims=True), axis=0, keepdims=True) * (1.0 / D)
        dh2 = e * (1.0 / D)
        db = dh2.astype(BF)
        dh2_o[...] = db
        dhn = jnp.zeros((tf, D), F32)
        for c in range(nch):
            gate = gate_s[:, cw * c:cw * (c + 1)]
            up = up_s[:, cw * c:cw * (c + 1)]
            sg = _sigmoid(gate)
            dact = _dot_nt(db, wo_ref[cw * c:cw * (c + 1), :])
            dgate = (dact * up * (sg * (1.0 + gate * (1.0 - sg)))).astype(BF)
            dup = (dact * (gate * sg)).astype(BF)
            dgu_o[:, cw * c:cw * (c + 1)] = dgate
            dgu_o[:, DFF + cw * c:DFF + cw * (c + 1)] = dup
            dhn += (_dot_nt(dgate, wi_ref[:, cw * c:cw * (c + 1)])
                    + _dot_nt(dup, wi_ref[:, DFF + cw * c:DFF + cw * (c + 1)]))
        dgf_o[...] += jnp.sum(dhn * xh, axis=0, keepdims=True)
        dxh = dhn * gf_ref[...]
        dh1 = dh2 + r * (dxh - xh * jnp.mean(dxh * xh, axis=-1, keepdims=True))
        dh1_o[...] = dh1
        dh1b_o[...] = dh1.astype(BF)

    return pl.pallas_call(
        body, name="ffn_fwd_bwd", grid=(lp // tf,),
        in_specs=[_rows(tf, D), pl.BlockSpec((tf, D), lambda i: (jnp.maximum(i - PAD // tf, 0), 0)),
                  _const((D, 2 * DFF)), _const((DFF, D)), _full((1, D))],
        out_specs=[_rows(tf, D), _rows(tf, 2 * DFF), _rows(tf, DFF), _rows(tf, D), _rows(tf, D), _rows(tf, D),
                   _full((1, D)), _full((1, 1))],
        out_shape=[jax.ShapeDtypeStruct((lp, D), BF), jax.ShapeDtypeStruct((lp, 2 * DFF), BF),
                   jax.ShapeDtypeStruct((lp, DFF), BF), jax.ShapeDtypeStruct((lp, D), BF),
                   jax.ShapeDtypeStruct((lp, D), F32), jax.ShapeDtypeStruct((lp, D), BF),
                   jax.ShapeDtypeStruct((1, D), F32), jax.ShapeDtypeStruct((1, 1), F32)],
        scratch_shapes=[pltpu.VMEM((tf, DFF), F32), pltpu.VMEM((tf, DFF), F32)],
        compiler_params=_cp(("arbitrary",), vmem_mb=58),
    )(h1, tgt, w_fi, w_fo, gf)


def weight_grad(name, a, b, tn):
    lp, k = a.shape
    n = b.shape[1]
    tr = next(t for t in (768, 512, 384, 256, 128) if lp % t == 0)

    def body(a_ref, b_ref, o_ref):
        @pl.when(pl.program_id(1) == 0)
        def _():
            o_ref[...] = jnp.zeros_like(o_ref)

        o_ref[...] += _dot_tn(a_ref[...], b_ref[...])

    return pl.pallas_call(
        body, name=name, grid=(n // tn, lp // tr),
        in_specs=[pl.BlockSpec((tr, k), lambda j, m: (m, 0)), pl.BlockSpec((tr, tn), lambda j, m: (m, j))],
        out_specs=pl.BlockSpec((k, tn), lambda j, m: (0, j)),
        out_shape=jax.ShapeDtypeStruct((k, n), F32),
        compiler_params=_cp(("arbitrary", "arbitrary")),
    )(a, b)


def _adamw_math(w, g, m, v):
    m = B1 * m + (1.0 - B1) * g
    v = B2 * v + (1.0 - B2) * (g * g)
    delta = -LR * ((m / BC1) / (jnp.sqrt(v / BC2) + AEPS) + WD * w)
    return delta, m, v


def _row_tile(r):
    return next((t for t in (256, 128, 64, 32, 16, 8) if r % t == 0), r)


def adamw(name, w, g, m, v):
    r, c = w.shape
    tr = _row_tile(r)

    def body(w_ref, g_ref, m_ref, v_ref, d_o, m_o, v_o):
        d_o[...], m_o[...], v_o[...] = _adamw_math(w_ref[...], g_ref[...], m_ref[...], v_ref[...])

    spec = pl.BlockSpec((tr, c), lambda i: (i, 0))
    return pl.pallas_call(
        body, name=name, grid=(r // tr,), in_specs=[spec] * 4, out_specs=[spec] * 3,
        out_shape=[jax.ShapeDtypeStruct((r, c), F32)] * 3,
        compiler_params=_cp(("arbitrary",)),
    )(w, g, m, v)


def adamw_small(ws, gs, ms, vs):
    n = len(ws)
    steps = 8

    def spec(a):
        if a.ndim == 4:
            return pl.BlockSpec((1, a.shape[1] // steps) + a.shape[2:], lambda i: (0, i, 0, 0))
        nd = a.ndim
        return pl.BlockSpec(a.shape, lambda i: (0,) * nd)

    def body(*refs):
        w, g, m, v, d_o, m_o, v_o = (refs[k * n:(k + 1) * n] for k in range(7))
        for t in range(n):
            d_o[t][...], m_o[t][...], v_o[t][...] = _adamw_math(w[t][...], g[t][...], m[t][...], v[t][...])

    specs = [spec(a) for a in ws]
    res = pl.pallas_call(
        body, name="adamw_small", grid=(steps,), in_specs=specs * 4, out_specs=specs * 3,
        out_shape=[jax.ShapeDtypeStruct(a.shape, F32) for a in ws] * 3,
        compiler_params=_cp(("arbitrary",)),
    )(*ws, *gs, *ms, *vs)
    return res[:n], res[n:2 * n], res[2 * n:]


def pair_sum(name, place, full, got, ax):
    r, c = got.shape
    tr = _row_tile(r)

    def body(s_ref, a_ref, b_ref, o_ref):
        o_ref[...] = (a_ref[...] + b_ref[...]).astype(BF)

    if ax == 1:
        mine = pl.BlockSpec((tr, c), lambda i, s: (s[1] * (r // tr) + i, 0))
    else:
        mine = pl.BlockSpec((tr, c), lambda i, s: (i, s[1]))
    spec = pl.BlockSpec((tr, c), lambda i, s: (i, 0))
    return pl.pallas_call(
        body, name=name,
        grid_spec=pltpu.PrefetchScalarGridSpec(num_scalar_prefetch=1, grid=(r // tr,), in_specs=[mine, spec],
                                               out_specs=spec),
        out_shape=jax.ShapeDtypeStruct((r, c), BF),
        compiler_params=_cp(("arbitrary",)),
    )(place, full, got)


def sum4(name, place, parts, half, ax):
    _, r, c = parts.shape
    tr = _row_tile(r)

    def body(s_ref, p_ref, h_ref, o_ref):
        o_ref[...] = ((p_ref[0].astype(F32) + p_ref[1].astype(F32))
                      + (p_ref[2].astype(F32) + h_ref[...].astype(F32)))

    if ax == 1:
        own = pl.BlockSpec((tr, c), lambda i, s: (i, s[0]))
        out = pl.BlockSpec((tr, c), lambda i, s: (s[1] * (r // tr) + i, 0))
        shape = (2 * r, c)
    else:
        own = pl.BlockSpec((tr, c), lambda i, s: (s[0] * (r // tr) + i, 0))
        out = pl.BlockSpec((tr, c), lambda i, s: (i, s[1]))
        shape = (r, 2 * c)
    return pl.pallas_call(
        body, name=name,
        grid_spec=pltpu.PrefetchScalarGridSpec(
            num_scalar_prefetch=1, grid=(r // tr,),
            in_specs=[pl.BlockSpec((3, tr, c), lambda i, s: (0, i, 0)), own], out_specs=out),
        out_shape=jax.ShapeDtypeStruct(shape, F32),
        compiler_params=_cp(("arbitrary",)),
    )(place, parts, half)


def _place():
    x, y, c = lax.axis_index("x"), lax.axis_index("y"), lax.axis_index("c")
    return x, y, c


def _other_chips(x, y):
    return [(1 - x, y), (x, 1 - y), (1 - x, 1 - y)]


SHARDED = (("w_in", D, IN_COLS, 1), ("w_glu", D, 2 * D, 1), ("w_out", D, D, 0),
           ("w_ffn_in", D, 2 * DFF, 1), ("w_ffn_out", DFF, D, 0))


def _shard_of(ref, axis, k, size):
    if axis == 0:
        return ref.at[pl.ds(k * size, size), :]
    return ref.at[:, pl.ds(k * size, size)]


def _gather_piece(full, dims, k, h):
    r, cc, ax = dims
    if ax == 0:
        return full.at[pl.ds(k * (r // 4) + h * (r // 8), r // 8), :]
    return full.at[pl.ds(h * (r // 2), r // 2), pl.ds(k * (cc // 4), cc // 4)]


def _rows_half(shard, h):
    rs = shard.shape[0]
    return shard.at[pl.ds(h * (rs // 2), rs // 2), :]


def gather_weights(shards, dims):
    items = [(s,) + tuple(d) for s, d in zip(shards, dims)]
    n = len(items)

    def body(*refs):
        ins, outs = refs[:n], refs[n:2 * n]
        ici_send, ici_recv, d2d_send, d2d_recv, local_sems = refs[2 * n:]
        x, y, c = _place()
        mine = 2 * x + y
        chips = _other_chips(x, y)

        def piece(t, k, h):
            return _gather_piece(outs[t], dims[t], k, h)

        def src_half(t, h):
            return _rows_half(ins[t], h)

        local, first, passed = [], [], []
        for t in range(n):
            _, r, cc, ax = items[t]
            size = (r if ax == 0 else cc) // 4
            loc = pltpu.make_async_remote_copy(
                src_ref=ins[t], dst_ref=_shard_of(outs[t], ax, mine, size), send_sem=local_sems.at[t],
                recv_sem=local_sems.at[n + t], device_id=(x, y, 1 - c), device_id_type=MESH)
            loc.start()
            local.append(loc)
            for j, chip in enumerate(chips):
                cp = pltpu.make_async_remote_copy(
                    src_ref=src_half(t, c), dst_ref=piece(t, mine, c),
                    send_sem=ici_send.at[3 * t + j], recv_sem=ici_recv.at[3 * t + j],
                    device_id=(*chip, c), device_id_type=MESH)
                cp.start()
                first.append(cp)
        for t in range(n):
            for j, (px, py) in enumerate(chips):
                got = piece(t, 2 * px + py, c)
                pltpu.make_async_remote_copy(
                    src_ref=src_half(t, c), dst_ref=got, send_sem=ici_send.at[3 * t + j],
                    recv_sem=ici_recv.at[3 * t + j], device_id=(px, py, c), device_id_type=MESH).wait_recv()
                cp = pltpu.make_async_remote_copy(
                    src_ref=got, dst_ref=got, send_sem=d2d_send.at[3 * t + j], recv_sem=d2d_recv.at[3 * t + j],
                    device_id=(x, y, 1 - c), device_id_type=MESH)
                cp.start()
                passed.append(cp)
        for t in range(n):
            for j, (px, py) in enumerate(chips):
                other = piece(t, 2 * px + py, 1 - c)
                pltpu.make_async_remote_copy(
                    src_ref=other, dst_ref=other, send_sem=d2d_send.at[3 * t + j], recv_sem=d2d_recv.at[3 * t + j],
                    device_id=(x, y, 1 - c), device_id_type=MESH).wait_recv()
        for cp in first + passed:
            cp.wait_send()
        for loc in local:
            loc.wait()

    return pl.pallas_call(
        body, name="gather_weights",
        in_specs=[ANY] * n, out_specs=[ANY] * n,
        out_shape=[jax.ShapeDtypeStruct((r, c), s.dtype) for s, r, c, _ in items],
        scratch_shapes=[pltpu.SemaphoreType.DMA((3 * n,)), pltpu.SemaphoreType.DMA((3 * n,)),
                        pltpu.SemaphoreType.DMA((3 * n,)), pltpu.SemaphoreType.DMA((3 * n,)),
                        pltpu.SemaphoreType.DMA((2 * n,))],
        compiler_params=pltpu.CompilerParams(has_side_effects=True),
    )(*shards)


SEM = pl.BlockSpec(memory_space=pltpu.SEMAPHORE)
HBM = pl.BlockSpec(memory_space=pltpu.HBM)
EFFECT = pltpu.SideEffectType.DATAFLOW_SIDE_EFFECTING


def _in_hbm(a):
    return pltpu.with_memory_space_constraint(a, pltpu.HBM)


def _split_copy_start(name, copies, ncopies, srcs, lands, after):
    ns, nl = len(srcs), len(lands)

    def body(*refs):
        ins, land_refs = refs[:ns], refs[ns:ns + nl]
        send_sems, recv_sems = refs[ns + nl + 1], refs[ns + nl + 2]
        token = refs[-1]
        for k, (src, dst, dev) in enumerate(copies(ins, land_refs)):
            pltpu.make_async_remote_copy(src_ref=src, dst_ref=dst, send_sem=send_sems.at[k],
                                         recv_sem=recv_sems.at[k], device_id=dev, device_id_type=MESH).start()
        token[...] = jnp.zeros_like(token)

    res = pl.pallas_call(
        body, name=name,
        in_specs=[HBM] * (ns + nl) + [ANY],
        out_specs=(SEM, SEM) + (HBM,) * (ns + nl) + (pl.BlockSpec(memory_space=pltpu.VMEM),),
        out_shape=(pltpu.SemaphoreType.DMA((ncopies,)), pltpu.SemaphoreType.DMA((ncopies,)))
        + tuple(pltpu.HBM(a.shape, a.dtype) for a in list(srcs) + list(lands))
        + (jax.ShapeDtypeStruct((8, 128), F32),),
        input_output_aliases={t: 2 + t for t in range(ns + nl)},
        compiler_params=pltpu.CompilerParams(has_side_effects=EFFECT),
    )(*[_in_hbm(a) for a in srcs], *[_in_hbm(a) for a in lands], after)
    return res[0], res[1], list(res[2:2 + ns]), list(res[2 + ns:2 + ns + nl]), res[-1]


def _split_copy_wait(name, copies, arrivals, send_sems, recv_sems, srcs, lands, after):
    ns, nl = len(srcs), len(lands)

    def body(*refs):
        ins, land_refs = refs[:ns], refs[ns:ns + nl]
        send_sems_ref, recv_sems_ref = refs[ns + nl], refs[ns + nl + 1]
        mine = copies(ins, land_refs)
        for k, ((src, dst, dev), got) in enumerate(zip(mine, arrivals(ins, land_refs))):
            pltpu.make_async_remote_copy(src_ref=src, dst_ref=dst, send_sem=send_sems_ref.at[k],
                                         recv_sem=recv_sems_ref.at[k], device_id=dev,
                                         device_id_type=MESH).wait_send()
            pltpu.make_async_remote_copy(src_ref=got, dst_ref=got, send_sem=send_sems_ref.at[k],
                                         recv_sem=recv_sems_ref.at[k], device_id=dev,
                                         device_id_type=MESH).wait_recv()

    res = pl.pallas_call(
        body, name=name,
        in_specs=[HBM] * (ns + nl) + [SEM, SEM, ANY],
        out_specs=(HBM,) * (ns + nl),
        out_shape=tuple(pltpu.HBM(a.shape, a.dtype) for a in list(srcs) + list(lands)),
        input_output_aliases={t: t for t in range(ns + nl)},
        compiler_params=pltpu.CompilerParams(has_side_effects=EFFECT),
    )(*srcs, *lands, send_sems, recv_sems, after)
    return list(res[:ns]), list(res[ns:])


def _gather_copies(dims):
    def copies(ins, lands):
        x, y, c = _place()
        mine = 2 * x + y
        out = []
        for t, d in enumerate(dims):
            size = (d[0] if d[2] == 0 else d[1]) // 4
            for chip in _other_chips(x, y):
                out.append((_rows_half(ins[t], c), _gather_piece(lands[t], d, mine, c), (*chip, c)))
            out.append((ins[t], _shard_of(lands[t], d[2], mine, size), (x, y, 1 - c)))
        return out

    def arrivals(ins, lands):
        x, y, c = _place()
        mine = 2 * x + y
        out = []
        for t, d in enumerate(dims):
            size = (d[0] if d[2] == 0 else d[1]) // 4
            for px, py in _other_chips(x, y):
                out.append(_gather_piece(lands[t], d, 2 * px + py, c))
            out.append(_shard_of(lands[t], d[2], mine, size))
        return out

    return copies, arrivals


def gather_forward(fulls, dims):
    n = len(fulls)

    def body(*refs):
        ins, outs = refs[:n], refs[n:2 * n]
        send_sems, recv_sems = refs[2 * n:]
        x, y, c = _place()
        cps = []
        for t in range(n):
            for j, (px, py) in enumerate(_other_chips(x, y)):
                cp = pltpu.make_async_remote_copy(
                    src_ref=_gather_piece(ins[t], dims[t], 2 * px + py, c),
                    dst_ref=_gather_piece(outs[t], dims[t], 2 * px + py, c),
                    send_sem=send_sems.at[3 * t + j], recv_sem=recv_sems.at[3 * t + j],
                    device_id=(x, y, 1 - c), device_id_type=MESH)
                cp.start()
                cps.append(cp)
        for t in range(n):
            for j, (px, py) in enumerate(_other_chips(x, y)):
                cps[3 * t + j].wait_send()
                other = _gather_piece(outs[t], dims[t], 2 * px + py, 1 - c)
                pltpu.make_async_remote_copy(
                    src_ref=other, dst_ref=other, send_sem=send_sems.at[3 * t + j], recv_sem=recv_sems.at[3 * t + j],
                    device_id=(x, y, 1 - c), device_id_type=MESH).wait_recv()

    return pl.pallas_call(
        body, name="gather_forward",
        in_specs=[ANY] * n, out_specs=[ANY] * n,
        out_shape=[jax.ShapeDtypeStruct(a.shape, a.dtype) for a in fulls],
        input_output_aliases={t: t for t in range(n)},
        scratch_shapes=[pltpu.SemaphoreType.DMA((3 * n,)), pltpu.SemaphoreType.DMA((3 * n,))],
        compiler_params=pltpu.CompilerParams(has_side_effects=True),
    )(*fulls)


def _half_of(ref, ax, h):
    r, c = ref.shape
    if ax == 1:
        return ref.at[pl.ds(h * (r // 2), r // 2), :]
    return ref.at[:, pl.ds(h * (c // 2), c // 2)]


def _half_shape(r, c, ax):
    return (r // 2, c) if ax == 1 else (r, c // 2)


def pair_exchange(name, grads, idxs):
    n = len(grads)
    which = [SHARDED[i] for i in idxs]

    def body(*refs):
        ins, got = refs[:n], refs[n:2 * n]
        send_sems, recv_sems = refs[2 * n:]
        x, y, c = _place()
        cps = []
        for t, (_, _, _, ax) in enumerate(which):
            cp = pltpu.make_async_remote_copy(
                src_ref=_half_of(ins[t], ax, 1 - c), dst_ref=got[t], send_sem=send_sems.at[t],
                recv_sem=recv_sems.at[t], device_id=(x, y, 1 - c), device_id_type=MESH)
            cp.start()
            cps.append(cp)
        for cp in cps:
            cp.wait()

    return pl.pallas_call(
        body, name=name,
        in_specs=[ANY] * n, out_specs=[ANY] * n,
        out_shape=[jax.ShapeDtypeStruct(_half_shape(r, c, ax), F32) for _, r, c, ax in which],
        scratch_shapes=[pltpu.SemaphoreType.DMA((n,)), pltpu.SemaphoreType.DMA((n,))],
        compiler_params=pltpu.CompilerParams(has_side_effects=True),
    )(*grads)


def _piece_shapes(idxs):
    out = []
    for _, r, c, ax in [SHARDED[i] for i in idxs]:
        hr, hc = _half_shape(r, c, ax)
        out.append((hr // 4, hc) if ax == 0 else (hr, hc // 4))
    return out


def _scatter_copies(idxs):
    which = [SHARDED[i] for i in idxs]

    def copies(ins, lands):
        x, y, c = _place()
        out = []
        for t, (_, _, _, ax) in enumerate(which):
            size = ins[t].shape[ax] // 4
            for j, (px, py) in enumerate(_other_chips(x, y)):
                out.append((_shard_of(ins[t], ax, 2 * px + py, size), lands[t].at[j], (px, py, c)))
        return out

    def arrivals(ins, lands):
        return [lands[t].at[j] for t in range(len(which)) for j in range(3)]

    return copies, arrivals


def scatter_grads(halves, idxs):
    n = len(halves)
    which = [SHARDED[i] for i in idxs]

    def shapes():
        return _piece_shapes(idxs)

    def body(*refs):
        ins, outs = refs[:n], refs[n:2 * n]
        send_sems, recv_sems = refs[2 * n:]
        x, y, c = _place()
        started = []
        for t, (src, dst, (_, _, _, ax)) in enumerate(zip(ins, outs, which)):
            size = src.shape[ax] // 4
            for j, (px, py) in enumerate(_other_chips(x, y)):
                cp = pltpu.make_async_remote_copy(
                    src_ref=_shard_of(src, ax, 2 * px + py, size), dst_ref=dst.at[j],
                    send_sem=send_sems.at[3 * t + j], recv_sem=recv_sems.at[3 * t + j],
                    device_id=(px, py, c), device_id_type=MESH)
                cp.start()
                started.append(cp)
        for cp in started:
            cp.wait()

    return pl.pallas_call(
        body, name="scatter_grads",
        in_specs=[ANY] * n, out_specs=[ANY] * n,
        out_shape=[jax.ShapeDtypeStruct((3,) + s, BF) for s in shapes()],
        scratch_shapes=[pltpu.SemaphoreType.DMA((3 * n,)), pltpu.SemaphoreType.DMA((3 * n,))],
        compiler_params=pltpu.CompilerParams(has_side_effects=True),
    )(*halves)


def join_halves(name, shards, idxs):
    n = len(shards)
    which = [SHARDED[i] for i in idxs]

    def body(*refs):
        ins, outs = refs[:n], refs[n:2 * n]
        send_sems, recv_sems = refs[2 * n:]
        x, y, c = _place()
        cps = []
        for t, (_, _, _, ax) in enumerate(which):
            cp = pltpu.make_async_remote_copy(
                src_ref=_half_of(ins[t], ax, c), dst_ref=_half_of(outs[t], ax, c), send_sem=send_sems.at[t],
                recv_sem=recv_sems.at[t], device_id=(x, y, 1 - c), device_id_type=MESH)
            cp.start()
            cps.append(cp)
        for t, cp in enumerate(cps):
            cp.wait_send()
            theirs = _half_of(outs[t], which[t][3], 1 - c)
            pltpu.make_async_remote_copy(
                src_ref=theirs, dst_ref=theirs, send_sem=send_sems.at[t], recv_sem=recv_sems.at[t],
                device_id=(x, y, 1 - c), device_id_type=MESH).wait_recv()

    return pl.pallas_call(
        body, name=name,
        in_specs=[ANY] * n, out_specs=[ANY] * n,
        out_shape=[jax.ShapeDtypeStruct(s.shape, F32) for s in shards],
        input_output_aliases={t: t for t in range(n)},
        scratch_shapes=[pltpu.SemaphoreType.DMA((n,)), pltpu.SemaphoreType.DMA((n,))],
        compiler_params=pltpu.CompilerParams(has_side_effects=True),
    )(*shards)


def allreduce_small(pack):
    rows = pack.shape[0]
    hr = rows // 2

    def body(p_ref, o_ref, sib_ref, parts_ref, send_sems, recv_sems):
        x, y, c = _place()
        mine = 2 * x + y
        sibling = (x, y, 1 - c)
        swap = pltpu.make_async_remote_copy(
            src_ref=p_ref, dst_ref=sib_ref, send_sem=send_sems.at[0], recv_sem=recv_sems.at[0],
            device_id=sibling, device_id_type=MESH)
        swap.start()
        swap.wait()
        half = pl.ds(pl.multiple_of(c * hr, 8), hr)
        parts_ref[mine] = p_ref[half, :] + sib_ref[half, :]
        cps = []
        for j, chip in enumerate(_other_chips(x, y)):
            cp = pltpu.make_async_remote_copy(
                src_ref=parts_ref.at[mine], dst_ref=parts_ref.at[mine], send_sem=send_sems.at[1 + j],
                recv_sem=recv_sems.at[1 + j], device_id=(*chip, c), device_id_type=MESH)
            cp.start()
            cps.append(cp)
        for j, (px, py) in enumerate(_other_chips(x, y)):
            cps[j].wait_send()
            theirs = parts_ref.at[2 * px + py]
            pltpu.make_async_remote_copy(
                src_ref=theirs, dst_ref=theirs, send_sem=send_sems.at[1 + j], recv_sem=recv_sems.at[1 + j],
                device_id=(px, py, c), device_id_type=MESH).wait_recv()
        o_ref[half, :] = (parts_ref[0] + parts_ref[1]) + (parts_ref[2] + parts_ref[3])
        back = pltpu.make_async_remote_copy(
            src_ref=o_ref.at[half, :], dst_ref=o_ref.at[half, :], send_sem=send_sems.at[4],
            recv_sem=recv_sems.at[4], device_id=sibling, device_id_type=MESH)
        back.start()
        back.wait_send()
        other = pl.ds(pl.multiple_of((1 - c) * hr, 8), hr)
        pltpu.make_async_remote_copy(
            src_ref=o_ref.at[other, :], dst_ref=o_ref.at[other, :], send_sem=send_sems.at[4],
            recv_sem=recv_sems.at[4], device_id=sibling, device_id_type=MESH).wait_recv()

    return pl.pallas_call(
        body, name="allreduce_small",
        in_specs=[pl.BlockSpec(memory_space=pltpu.VMEM)], out_specs=pl.BlockSpec(memory_space=pltpu.VMEM),
        out_shape=jax.ShapeDtypeStruct((rows, 128), F32),
        scratch_shapes=[pltpu.VMEM((rows, 128), F32), pltpu.VMEM((4, hr, 128), F32),
                        pltpu.SemaphoreType.DMA((5,)), pltpu.SemaphoreType.DMA((5,))],
        compiler_params=pltpu.CompilerParams(has_side_effects=True, vmem_limit_bytes=40 << 20),
    )(pack)


def _ssm_discretize(lam_re, lam_im, log_dt, b_re, b_im):
    dt = jnp.exp(log_dt)[:, None]
    mag = jnp.exp(lam_re * dt)
    ar, ai = mag * jnp.cos(lam_im * dt), mag * jnp.sin(lam_im * dt)
    den = lam_re * lam_re + lam_im * lam_im
    nr, ni = ar - 1.0, ai
    fr, fi = (nr * lam_re + ni * lam_im) / den, (ni * lam_re - nr * lam_im) / den
    bbr = fr[..., None] * b_re - fi[..., None] * b_im
    bbi = fr[..., None] * b_im + fi[..., None] * b_re
    return ar, ai, bbr, bbi


def _cmul(a, b):
    return a[0] * b[0] - a[1] * b[1], a[0] * b[1] + a[1] * b[0]


def _scan_powers(ar, ai):
    a = (ar.reshape(1, SW), ai.reshape(1, SW))
    big = a
    for _ in range(SEG.bit_length() - 1):
        big = _cmul(big, big)
    assert 1 << (SEG.bit_length() - 1) == SEG
    pows = {1: big}
    pows[2] = _cmul(big, big)
    pows[4] = _cmul(pows[2], pows[2])
    rid = jnp.arange(8)[:, None]
    ones = jnp.ones((8, 1), F32)
    fwd, rev = [ones * a[0], ones * a[1]], [ones * a[0], -ones * a[1]]
    for k in (1, 2, 4):
        pr, pi = pows[k]
        fwd += [jnp.where(rid >= k, pr, 0.0), jnp.where(rid >= k, pi, 0.0)]
        rev += [jnp.where(rid < 8 - k, pr, 0.0), jnp.where(rid < 8 - k, -pi, 0.0)]
    return jnp.concatenate(fwd + rev, axis=0)


def _pack_b(bb):
    t = bb.reshape(NLC, 8, NSTATE, GCH).transpose(0, 1, 3, 2)
    return jnp.einsum("jgcp,gh->jgchp", t, jnp.eye(8, dtype=bb.dtype)).reshape(NLC, 128, LC)


def _unpack_b(db):
    t = jnp.einsum("jgchp,gh->jgcp", db.reshape(NLC, 8, GCH, 8, NSTATE), jnp.eye(8, dtype=db.dtype))
    return t.transpose(0, 1, 3, 2).reshape(GROUPS, NSTATE, GCH)


def _pack_c(cc):
    t = cc.reshape(NLC, 8, GCH, NSTATE).transpose(0, 1, 3, 2)
    return jnp.einsum("jgpc,gh->jgphc", t, jnp.eye(8, dtype=cc.dtype)).reshape(NLC, LC, 128)


def _unpack_c(dc):
    t = jnp.einsum("jgphc,gh->jgpc", dc.reshape(NLC, 8, NSTATE, 8, GCH), jnp.eye(8, dtype=dc.dtype))
    return t.transpose(0, 1, 3, 2).reshape(GROUPS, GCH, NSTATE)


SMALL = ("norm_mix", "q_norm", "k_norm", "attn_sinks", "lam_re", "lam_im", "log_dt", "ssm_b_re", "ssm_b_im",
         "ssm_c_re", "ssm_c_im", "ssm_d", "attn_branch_norm", "ssm_branch_norm", "norm_ffn")


def _pack_small(arrs, rows=None):
    flat = jnp.concatenate([a.reshape(-1).astype(F32) for a in arrs])
    n = flat.shape[0]
    if rows is None:
        rows = -(-n // (128 * 256)) * 256
    return jnp.pad(flat, (0, rows * 128 - n)).reshape(rows, 128)


def _unpack_small(pack, like):
    flat = pack.reshape(-1)
    out, off = [], 0
    for a in like:
        out.append(flat[off:off + a.size].reshape(a.shape))
        off += a.size
    return out


def _tie(a, token):
    return a if token is None else a + token[0, 0]


class _NoExchange:
    def __init__(self, w):
        self.w = w
        self.grads = {}

    def begin(self, after):
        return None

    def rest_weights(self, after):
        return self.w

    def start(self, group, idxs, grads):
        self.grads.update(zip(idxs, grads))
        return None

    def finish(self, group, after):
        pass


class _Exchange:
    def __init__(self, rest_shards, place):
        self.rest_shards = rest_shards
        self.place = place
        self.rest_dims = [s[1:] for s in SHARDED[1:]]
        self.pending = {}
        self.halves = {}
        self.parts = {}

    def begin(self, after):
        copies, arrivals = _gather_copies(self.rest_dims)
        lands = [lax.empty((r, c), BF) for r, c, _ in self.rest_dims]
        send, recv, srcs, lands, token = _split_copy_start(
            "gather_start", copies, 4 * len(lands), self.rest_shards, lands, after)
        self.pending["gather"] = (copies, arrivals, send, recv, srcs, lands)
        return token

    def rest_weights(self, after):
        copies, arrivals, send, recv, srcs, lands = self.pending.pop("gather")
        _, lands = _split_copy_wait("gather_wait", copies, arrivals, send, recv, srcs, lands, after)
        fulls = gather_forward(lands, self.rest_dims)
        return dict(zip([s[0] for s in SHARDED[1:]], fulls))

    def _halves(self, group, idxs, grads):
        got = pair_exchange("pair_exchange_" + group, grads, idxs)
        return [pair_sum("pair_sum_" + SHARDED[i][0], self.place, g, b, SHARDED[i][3])
                for i, g, b in zip(idxs, grads, got)]

    def start(self, group, idxs, grads):
        halves = self._halves(group, idxs, grads)
        copies, arrivals = _scatter_copies(idxs)
        lands = [lax.empty((3,) + sh, BF) for sh in _piece_shapes(idxs)]
        send, recv, srcs, lands, token = _split_copy_start(
            "scatter_start_" + group, copies, 3 * len(idxs), halves, lands, halves[0])
        self.pending[group] = (copies, arrivals, send, recv, srcs, lands, idxs)
        return token

    def finish(self, group, after):
        if group not in self.pending:
            return
        copies, arrivals, send, recv, srcs, lands, idxs = self.pending.pop(group)
        srcs, lands = _split_copy_wait("scatter_wait_" + group, copies, arrivals, send, recv, srcs, lands, after)
        for i, h, pt in zip(idxs, srcs, lands):
            self.halves[i], self.parts[i] = h, pt


def local_step(x, tgt, meta_full, p, w_in, ex):
    mpad = jnp.concatenate([jnp.zeros((META0, D), F32), meta_full], axis=0)
    qn_t = jnp.tile(p["q_norm"].reshape(1, HEAD), (1, NQ)) * HEAD ** -0.5
    kn_t = jnp.tile(p["k_norm"].reshape(1, HEAD), (1, NKV))
    seg = jnp.arange(256) // HEAD
    bd = (seg[:, None] == seg[None, :]).astype(BF)
    nm = p["norm_mix"].reshape(1, D)
    sinks = p["attn_sinks"].reshape(NQ)
    dsk = p["ssm_d"].reshape(1, D)
    abn, sbn, gf = (p[k].reshape(1, D) for k in ("attn_branch_norm", "ssm_branch_norm", "norm_ffn"))

    disc_in = (p["lam_re"], p["lam_im"], p["log_dt"], p["ssm_b_re"], p["ssm_b_im"])
    (ar, ai, bbr, bbi), disc_vjp = jax.vjp(_ssm_discretize, *disc_in)
    pw = _scan_powers(ar, ai)
    brp, bip = _pack_b(bbr).astype(BF), _pack_b(bbi).astype(BF)
    crp, cip = _pack_c(p["ssm_c_re"]).astype(BF), _pack_c(p["ssm_c_im"]).astype(BF)

    token = ex.begin(w_in)
    xn, qr, kr, qn, kn, v, u, gg = in_proj_fwd(x, mpad, w_in, _tie(nm, token), qn_t, kn_t, bd)
    attn = attention_fwd(sinks, qn, kn, v)
    perm = _segment_order()
    y, sr, si = ssm_fwd(u, pw, brp, bip, crp, cip, dsk, perm)
    w = ex.rest_weights(y)
    zb, zzb, mgb, h1 = mid_fwd(y, attn, gg, x, mpad, w["w_glu"], w["w_out"], abn, sbn)
    hnb, dgub, actb, dh2b, dh1, dh1b, dgf, loss = ffn_fwd_bwd(h1, tgt, w["w_ffn_in"], w["w_ffn_out"], gf)
    token = ex.start("ffn", (3, 4), [weight_grad("grad_w_ffn_in", hnb, dgub, 1408),
                                     weight_grad("grad_w_ffn_out", actb, dh2b, 512)])
    dy, dattn, dggb, dzzb, dabn, dsbn = mid_bwd(dh1b, y, attn, gg, zzb, w["w_glu"], w["w_out"], _tie(abn, token),
                                                sbn)
    grads_mid = [weight_grad("grad_w_glu", zb, dzzb, 1024), weight_grad("grad_w_out", mgb, dh1b, 1024)]
    dub, da, dbr, dbi, dcr, dci, dd = ssm_bwd(dy, u, sr, si, pw, brp, bip, crp, cip, dsk, perm)
    ex.finish("ffn", dub)
    token = ex.start("mid", (1, 2), grads_mid)
    dqn, dkn, dv, dsink = attention_bwd(_tie(sinks, token), qn, kn, v, dattn)
    dproj, dx, dmpad, dnm, dqw, dkw = in_proj_bwd(dqn, dkn, dv, dub, dggb, qr, kr, x, mpad, dh1, w_in, nm, qn_t,
                                                  kn_t, bd)
    ex.finish("mid", dproj)
    ex.start("in", (0,), [weight_grad("grad_w_in", xn, dproj, 1152)])

    dar = jnp.sum(da[0:8], axis=0).reshape(GROUPS, NSTATE)
    dai = jnp.sum(da[8:16], axis=0).reshape(GROUPS, NSTATE)
    dlr, dli, dldt, dbre, dbim = disc_vjp((dar, dai, _unpack_b(dbr), _unpack_b(dbi)))
    small = {
        "norm_mix": dnm, "q_norm": dqw.reshape(NQ, HEAD).sum(0) * HEAD ** -0.5, "k_norm": dkw.reshape(NKV, HEAD).sum(0),
        "attn_sinks": dsink[0, 0:NQ], "lam_re": dlr, "lam_im": dli, "log_dt": dldt,
        "ssm_b_re": dbre, "ssm_b_im": dbim, "ssm_c_re": _unpack_c(jnp.swapaxes(dcr, 1, 2)), "ssm_c_im": _unpack_c(jnp.swapaxes(dci, 1, 2)),
        "ssm_d": dd, "attn_branch_norm": dabn, "ssm_branch_norm": dsbn, "norm_ffn": dgf,
    }
    return loss[0, 0], dx, dmpad[META0:PAD], small


def kernel(x, meta_tokens, norm_mix, w_in, q_norm, k_norm, attn_sinks, lam_re, lam_im, log_dt, ssm_b_re, ssm_b_im, ssm_c_re, ssm_c_im, ssm_d, w_glu, attn_branch_norm, ssm_branch_norm, w_out, norm_ffn, w_ffn_in, w_ffn_out, loss_target, m_meta_tokens, m_norm_mix, m_w_in, m_q_norm, m_k_norm, m_attn_sinks, m_lam_re, m_lam_im, m_log_dt, m_ssm_b_re, m_ssm_b_im, m_ssm_c_re, m_ssm_c_im, m_ssm_d, m_w_glu, m_attn_branch_norm, m_ssm_branch_norm, m_w_out, m_norm_ffn, m_w_ffn_in, m_w_ffn_out, v_meta_tokens, v_norm_mix, v_w_in, v_q_norm, v_k_norm, v_attn_sinks, v_lam_re, v_lam_im, v_log_dt, v_ssm_b_re, v_ssm_b_im, v_ssm_c_re, v_ssm_c_im, v_ssm_d, v_w_glu, v_attn_branch_norm, v_ssm_branch_norm, v_w_out, v_norm_ffn, v_w_ffn_in, v_w_ffn_out):
    args = dict(locals())
    names = ["meta_tokens", "norm_mix", "w_in", "q_norm", "k_norm", "attn_sinks", "lam_re", "lam_im", "log_dt",
             "ssm_b_re", "ssm_b_im", "ssm_c_re", "ssm_c_im", "ssm_d", "w_glu", "attn_branch_norm",
             "ssm_branch_norm", "w_out", "norm_ffn", "w_ffn_in", "w_ffn_out"]
    big_names = [s[0] for s in SHARDED]
    xs, ys, cs = _place()
    chip = 2 * xs + ys

    shards = [args[n][0].astype(BF) for n in big_names]
    w_in_full, meta_full = gather_weights([shards[0], meta_tokens], [SHARDED[0][1:], (N_META, D, 1)])
    place = jnp.stack([chip, cs]).astype(jnp.int32)
    ex = _Exchange(shards[1:], place)

    p = {n: args[n][0] for n in SMALL}
    loss, dx, dmeta, gsmall = local_step(x[0], loss_target[0], meta_full, p, w_in_full, ex)
    loss = lax.psum(loss, ("x", "y", "c"))

    out = {}

    def finish(idxs, name):
        mine = [sum4("sum4_" + SHARDED[i][0], place, ex.parts[i], ex.halves[i], SHARDED[i][3]) for i in idxs]
        for i, g in zip(idxs, join_halves(name, mine, idxs)):
            n = SHARDED[i][0]
            d, nm_, nv_ = adamw("adamw_" + n, args[n][0], g, args["m_" + n][0], args["v_" + n][0])
            out[n] = tuple(t[None] for t in (g, d, nm_, nv_))

    finish((1, 2, 3, 4), "join_halves_rest")
    small_list = [gsmall[n].reshape(args[n].shape) for n in SMALL] + [dmeta]
    red = allreduce_small(_pack_small(small_list))
    small_red = _unpack_small(red, small_list)
    gmeta = lax.dynamic_slice_in_dim(small_red[-1], chip * (D // 4), D // 4, axis=1)
    d, nm_, nv_ = adamw("adamw_meta", meta_tokens, gmeta, m_meta_tokens, v_meta_tokens)
    out["meta_tokens"] = (gmeta, d, nm_, nv_)
    ds, nms, nvs = adamw_small([args[n] for n in SMALL], small_red[:-1], [args["m_" + n] for n in SMALL],
                               [args["v_" + n] for n in SMALL])
    for n, g_, d_, m_, v_ in zip(SMALL, small_red[:-1], ds, nms, nvs):
        out[n] = (g_, d_, m_, v_)
    ex.finish("in", ds[0])
    finish((0,), "join_halves_in")

    res = [loss, dx[None]]
    for k in range(4):
        res += [out[n][k] for n in names]
    return tuple(res)
```

```python
import functools
import math

import jax
import jax.numpy as jnp
from jax import lax
from jax.experimental import pallas as pl
from jax.experimental.pallas import tpu as pltpu

F32 = jnp.float32
BF = jnp.bfloat16

D = 1024
N_META = 16
HEAD = 64
NQ = 16
NKV = 4
QW = NQ * HEAD
KVW = NKV * HEAD
WIN = 128
GROUPS = 64
GCH = 16
NSTATE = 64
SW = GROUPS * NSTATE
DFF = 2816
IN_COLS = QW + 2 * KVW + D + 2 * D
PAD = 256
META0 = PAD - N_META
EPS = 1e-6
NEG = -1e30

TM = 256
TS = 256
LC = 512
NLC = SW // LC

LR, B1, B2, AEPS, WD, STEP = 0.001, 0.9, 0.999, 1e-08, 0.01, 10
BC1 = 1.0 - B1 ** STEP
BC2 = 1.0 - B2 ** STEP

MESH = pl.DeviceIdType.MESH
ANY = pl.BlockSpec(memory_space=pl.ANY)


def _cp(sem=None, vmem_mb=48):
    return pltpu.CompilerParams(dimension_semantics=sem, vmem_limit_bytes=vmem_mb << 20)


def _full(shape):
    n = len(shape)
    return pl.BlockSpec(shape, lambda *a: (0,) * n)


def _const(shape):
    n = len(shape)
    return pl.BlockSpec(shape, lambda *a: (0,) * n, pipeline_mode=pl.Buffered(1))


def _rows(tm, width):
    return pl.BlockSpec((tm, width), lambda i: (i, 0))


def _dot(a, b):
    return jnp.dot(a, b, preferred_element_type=F32)


def _dot_nt(a, b):
    return lax.dot_general(a, b, (((1,), (1,)), ((), ())), preferred_element_type=F32)


def _dot_tn(a, b):
    return lax.dot_general(a, b, (((0,), (0,)), ((), ())), preferred_element_type=F32)


def _sigmoid(x):
    return 0.5 * jnp.tanh(0.5 * x) + 0.5


def _seg_mean(x, bd):
    outs = []
    for j in range(x.shape[1] // 256):
        c = x[:, 256 * j:256 * (j + 1)]
        hi = c.astype(BF)
        lo = (c - hi.astype(F32)).astype(BF)
        outs.append(_dot(hi, bd) + _dot(lo, bd))
    out = outs[0] if len(outs) == 1 else jnp.concatenate(outs, axis=1)
    return out * (1.0 / HEAD)


_GC = math.sqrt(2.0 / math.pi)


def _gelu(x):
    t = jnp.tanh(_GC * (x + 0.044715 * x * x * x))
    return 0.5 * x * (1.0 + t), t


def _gelu_grad(x, t):
    return 0.5 * (1.0 + t) + 0.5 * x * (1.0 - t * t) * _GC * (1.0 + 3.0 * 0.044715 * x * x)


def _seq_specs():
    assert TM == PAD
    return [pl.BlockSpec((TM, D), lambda i: (jnp.maximum(i - 1, 0), 0)), _full((PAD, D))]


def _seq_tile(x_ref, m_ref):
    return jnp.where(pl.program_id(0) == 0, m_ref[...], x_ref[...])


def in_proj_fwd(x, mpad, w_in, nm, qn_t, kn_t, bd):
    lp = x.shape[0] + PAD

    def body(x_ref, m_ref, w_ref, nm_ref, qn_ref, kn_ref, bd_ref,
             xn_o, qr_o, kr_o, qn_o, kn_o, v_o, u_o, gg_o):
        h = _seq_tile(x_ref, m_ref)
        r = lax.rsqrt(jnp.mean(h * h, axis=-1, keepdims=True) + EPS)
        xb = ((h * r) * nm_ref[...]).astype(BF)
        xn_o[...] = xb
        bdv = bd_ref[...]
        q = _dot(xb, w_ref[:, 0:QW])
        qr_o[...] = q
        qn_o[...] = (q * lax.rsqrt(_seg_mean(q * q, bdv) + EPS) * qn_ref[...]).astype(BF)
        k = _dot(xb, w_ref[:, QW:QW + KVW])
        kr_o[...] = k
        kn_o[...] = (k * lax.rsqrt(_seg_mean(k * k, bdv) + EPS) * kn_ref[...]).astype(BF)
        v_o[...] = _dot(xb, w_ref[:, QW + KVW:QW + 2 * KVW]).astype(BF)
        u_o[...] = _dot(xb, w_ref[:, QW + 2 * KVW:QW + 2 * KVW + D])
        gg_o[...] = _dot(xb, w_ref[:, QW + 2 * KVW + D:IN_COLS])

    outs = [(D, BF), (QW, F32), (KVW, F32), (QW, BF), (KVW, BF), (KVW, BF), (D, F32), (2 * D, F32)]
    return pl.pallas_call(
        body, name="in_proj_fwd", grid=(lp // TM,),
        in_specs=_seq_specs() + [_full((D, IN_COLS)), _full((1, D)), _full((1, QW)), _full((1, KVW)),
                                 _full((256, 256))],
        out_specs=[_rows(TM, w) for w, _ in outs],
        out_shape=[jax.ShapeDtypeStruct((lp, w), t) for w, t in outs],
        compiler_params=_cp(("arbitrary",)),
    )(x, mpad, w_in, nm, qn_t, kn_t, bd)


def in_proj_bwd(dqn, dkn, dv, du, dgg, qr, kr, x, mpad, dh1, w_in, nm, qn_t, kn_t, bd):
    lp = x.shape[0] + PAD

    def body(dqn_ref, dkn_ref, dv_ref, du_ref, dgg_ref, qr_ref, kr_ref, x_ref, m_ref, dh1_ref,
             w_ref, nm_ref, qn_ref, kn_ref, bd_ref,
             dproj_o, dx_o, dm_o, dnm_o, dqw_o, dkw_o):
        i = pl.program_id(0)

        @pl.when(i == 0)
        def _():
            dnm_o[...] = jnp.zeros_like(dnm_o)
            dqw_o[...] = jnp.zeros_like(dqw_o)
            dkw_o[...] = jnp.zeros_like(dkw_o)

        bdv = bd_ref[...]

        def norm_bwd(x, dy, g, acc_ref):
            rr = lax.rsqrt(_seg_mean(x * x, bdv) + EPS)
            xh = x * rr
            acc_ref[...] += jnp.sum(dy * xh, axis=0, keepdims=True)
            dxh = dy * g
            return rr * (dxh - xh * _seg_mean(dxh * xh, bdv))

        dq = norm_bwd(qr_ref[...], dqn_ref[...], qn_ref[...], dqw_o)
        dk = norm_bwd(kr_ref[...], dkn_ref[...], kn_ref[...], dkw_o)
        dproj_o[:, 0:QW] = dq.astype(BF)
        dproj_o[:, QW:QW + KVW] = dk.astype(BF)
        dproj_o[:, QW + KVW:QW + 2 * KVW] = dv_ref[...].astype(BF)
        dproj_o[:, QW + 2 * KVW:QW + 2 * KVW + D] = du_ref[...]
        dproj_o[:, QW + 2 * KVW + D:IN_COLS] = dgg_ref[...]
        dxn = _dot_nt(dproj_o[...], w_ref[...])
        h = _seq_tile(x_ref, m_ref)
        r = lax.rsqrt(jnp.mean(h * h, axis=-1, keepdims=True) + EPS)
        xh = h * r
        dnm_o[...] += jnp.sum(dxn * xh, axis=0, keepdims=True)
        dxh = dxn * nm_ref[...]
        dh0 = dh1_ref[...] + r * (dxh - xh * jnp.mean(dxh * xh, axis=-1, keepdims=True))
        dx_o[...] = dh0

        @pl.when(i == 0)
        def _():
            dm_o[...] = dh0

    return pl.pallas_call(
        body, name="in_proj_bwd", grid=(lp // TM,),
        in_specs=[_rows(TM, QW), _rows(TM, KVW), _rows(TM, KVW), _rows(TM, D), _rows(TM, 2 * D),
                  _rows(TM, QW), _rows(TM, KVW)] + _seq_specs() + [_rows(TM, D),
                  _full((D, IN_COLS)), _full((1, D)), _full((1, QW)), _full((1, KVW)), _full((256, 256))],
        out_specs=[_rows(TM, IN_COLS), _seq_specs()[0], _full((PAD, D)), _full((1, D)), _full((1, QW)),
                   _full((1, KVW))],
        out_shape=[jax.ShapeDtypeStruct((lp, IN_COLS), BF), jax.ShapeDtypeStruct((lp - PAD, D), F32),
                   jax.ShapeDtypeStruct((PAD, D), F32),
                   jax.ShapeDtypeStruct((1, D), F32), jax.ShapeDtypeStruct((1, QW), F32),
                   jax.ShapeDtypeStruct((1, KVW), F32)],
        compiler_params=_cp(("arbitrary",)),
    )(dqn, dkn, dv, du, dgg, qr, kr, x, mpad, dh1, w_in, nm, qn_t, kn_t, bd)


def _att_bias(b):
    qi = lax.broadcasted_iota(jnp.int32, (WIN, WIN), 0)
    kj = lax.broadcasted_iota(jnp.int32, (WIN, WIN), 1)
    upper = kj > qi
    valid = (upper & ((b - 1) * WIN + kj >= PAD)) | (jnp.logical_not(upper) & (b * WIN + kj >= META0))
    bias_pc = jnp.where(valid, 0.0, NEG)
    kj1 = lax.broadcasted_iota(jnp.int32, (1, WIN), 1)
    bias_m = jnp.where((kj1 >= WIN - N_META) & (b * WIN >= PAD), 0.0, NEG)
    return bias_m, bias_pc, upper


def _dup_half(ref, kp, pos, lo):
    pair = ref[:, 128 * kp:128 * kp + 128].astype(F32)
    rolled = pltpu.roll(pair, 64, axis=1)
    keep = lo if pos == 0 else jnp.logical_not(lo)
    return jnp.where(keep, pair, rolled).astype(BF)


def _stack_heads(ref, kv, lo):
    parts = []
    for pp in range(2):
        pair = ref[:, 256 * kv + 128 * pp:256 * kv + 128 * pp + 128]
        zero = jnp.zeros_like(pair)
        parts.append(jnp.where(lo, pair, zero))
        parts.append(jnp.where(lo, zero, pair))
    return jnp.concatenate(parts, axis=0)


def _unstack_heads(x4, lo):
    a = jnp.where(lo, x4[0:WIN], x4[WIN:2 * WIN])
    b = jnp.where(lo, x4[2 * WIN:3 * WIN], x4[3 * WIN:4 * WIN])
    return a, b


def _split_pc(x, upper):
    zero = jnp.zeros_like(x)
    return jnp.where(upper, x, zero).astype(BF), jnp.where(upper, zero, x).astype(BF)


def _head_softmax(s_m, s_p, s_c, bias, sink):
    bias_m, bias_pc, upper = bias
    sm = s_m + bias_m
    spc = jnp.where(upper, s_p, s_c) + bias_pc
    m = jnp.maximum(jnp.max(jnp.maximum(sm, spc), axis=-1, keepdims=True), sink)
    em, epc = jnp.exp(sm - m), jnp.exp(spc - m)
    esink = jnp.exp(sink - m)
    inv = 1.0 / (esink + jnp.sum(em + epc, axis=-1, keepdims=True))
    return em, epc, inv, esink


def _kv_specs():
    return [pl.BlockSpec((WIN, KVW), lambda b: (1, 0)),
            pl.BlockSpec((WIN, KVW), lambda b: (jnp.maximum(b - 1, 0), 0)),
            pl.BlockSpec((WIN, KVW), lambda b: (b, 0))]


def attention_fwd(sinks, qn, kn, v):
    lp = qn.shape[0]

    def body(sink_ref, q_ref, km_ref, kp_ref, kc_ref, vm_ref, vp_ref, vc_ref, o_ref):
        b = pl.program_id(0)
        bias = _att_bias(b)
        lo = lax.broadcasted_iota(jnp.int32, (WIN, WIN), 1) < HEAD
        for kv in range(NKV):
            kp, pos = kv // 2, kv % 2
            kds = [_dup_half(r, kp, pos, lo) for r in (km_ref, kp_ref, kc_ref)]
            vds = [_dup_half(r, kp, pos, lo) for r in (vm_ref, vp_ref, vc_ref)]
            q4 = _stack_heads(q_ref, kv, lo)
            ss = [_dot_nt(q4, kd) for kd in kds]
            es, invs = ([], [], []), []
            for h in range(4):
                rs = slice(WIN * h, WIN * h + WIN)
                em, epc, inv, _ = _head_softmax(ss[0][rs], ss[1][rs], ss[2][rs], bias, sink_ref[4 * kv + h])
                e_p, e_c = _split_pc(epc, bias[2])
                for lst, e in zip(es, (em.astype(BF), e_p, e_c)):
                    lst.append(e)
                invs.append(inv)
            e_m, e_p, e_c = (jnp.concatenate(lst, axis=0) for lst in es)
            o4 = (_dot(e_m, vds[0]) + _dot(e_p, vds[1]) + _dot(e_c, vds[2])) * jnp.concatenate(invs, axis=0)
            oa, ob = _unstack_heads(o4, lo)
            o_ref[:, 256 * kv:256 * kv + 128] = oa
            o_ref[:, 256 * kv + 128:256 * kv + 256] = ob

    return pl.pallas_call(
        body, name="attention_fwd", grid=(lp // WIN,),
        in_specs=[pl.BlockSpec(memory_space=pltpu.SMEM), _rows(WIN, QW)] + _kv_specs() + _kv_specs(),
        out_specs=_rows(WIN, QW),
        out_shape=jax.ShapeDtypeStruct((lp, QW), F32),
        compiler_params=_cp(("arbitrary",)),
    )(sinks, qn, kn, kn, kn, v, v, v)


def attention_bwd(sinks, qn, kn, v, do):
    lp = qn.shape[0]
    nb = lp // WIN

    def body(sink_ref, q_ref, km_ref, kp_ref, kc_ref, vm_ref, vp_ref, vc_ref, do_ref,
             dq_ref, dk_hbm, dv_hbm, dsink_ref, dk_acc, dv_acc):
        b = pl.program_id(0)

        @pl.when(b == 0)
        def _():
            dk_acc[...] = jnp.zeros_like(dk_acc)
            dv_acc[...] = jnp.zeros_like(dv_acc)
            dsink_ref[...] = jnp.zeros_like(dsink_ref)

        bias = _att_bias(b)
        upper = bias[2]
        lo = lax.broadcasted_iota(jnp.int32, (WIN, WIN), 1) < HEAD
        lane8 = lax.broadcasted_iota(jnp.int32, (8, 128), 1)
        starts = [WIN, pl.multiple_of(jnp.maximum(b - 1, 0) * WIN, WIN), pl.multiple_of(b * WIN, WIN)]
        for kv in range(NKV):
            kp, pos = kv // 2, kv % 2
            keep = lo if pos == 0 else jnp.logical_not(lo)
            kds = [_dup_half(r, kp, pos, lo) for r in (km_ref, kp_ref, kc_ref)]
            vds = [_dup_half(r, kp, pos, lo) for r in (vm_ref, vp_ref, vc_ref)]
            q4 = _stack_heads(q_ref, kv, lo)
            do4 = _stack_heads(do_ref, kv, lo)
            ss = [_dot_nt(q4, kd) for kd in kds]
            dps = [_dot_nt(do4, vd) for vd in vds]
            ds_l, p_l = ([], [], []), ([], [], [])
            for h in range(4):
                rs = slice(WIN * h, WIN * h + WIN)
                em, epc, inv, esink = _head_softmax(ss[0][rs], ss[1][rs], ss[2][rs], bias, sink_ref[4 * kv + h])
                p_m, p_pc = em * inv, epc * inv
                dp_m = dps[0][rs]
                dp_pc = jnp.where(upper, dps[1][rs], dps[2][rs])
                delta = jnp.sum(p_m * dp_m + p_pc * dp_pc, axis=-1, keepdims=True)
                val = -jnp.sum(esink * inv * delta, axis=0, keepdims=True)
                dsink_ref[...] += jnp.where(lane8 == 4 * kv + h, val, 0.0)
                for lst, t in zip(ds_l, ((p_m * (dp_m - delta)).astype(BF),) + _split_pc(p_pc * (dp_pc - delta), upper)):
                    lst.append(t)
                for lst, t in zip(p_l, (p_m.astype(BF),) + _split_pc(p_pc, upper)):
                    lst.append(t)
            dss = [jnp.concatenate(lst, axis=0) for lst in ds_l]
            ps = [jnp.concatenate(lst, axis=0) for lst in p_l]
            dq4 = _dot(dss[0], kds[0]) + _dot(dss[1], kds[1]) + _dot(dss[2], kds[2])
            dqa, dqb = _unstack_heads(dq4, lo)
            dq_ref[:, 256 * kv:256 * kv + 128] = dqa
            dq_ref[:, 256 * kv + 128:256 * kv + 256] = dqb
            for x in range(3):
                dkd = _dot_tn(dss[x], q4)
                dvd = _dot_tn(ps[x], do4)
                dkd = jnp.where(keep, dkd + pltpu.roll(dkd, 64, axis=1), 0.0)
                dvd = jnp.where(keep, dvd + pltpu.roll(dvd, 64, axis=1), 0.0)
                dk_acc[pl.ds(starts[x], WIN), 128 * kp:128 * kp + 128] += dkd
                dv_acc[pl.ds(starts[x], WIN), 128 * kp:128 * kp + 128] += dvd

        @pl.when(b == nb - 1)
        def _():
            pltpu.sync_copy(dk_acc, dk_hbm)
            pltpu.sync_copy(dv_acc, dv_hbm)

    return pl.pallas_call(
        body, name="attention_bwd", grid=(nb,),
        in_specs=[pl.BlockSpec(memory_space=pltpu.SMEM), _rows(WIN, QW)] + _kv_specs() + _kv_specs()
                 + [_rows(WIN, QW)],
        out_specs=[_rows(WIN, QW), ANY, ANY, _full((8, 128))],
        out_shape=[jax.ShapeDtypeStruct((lp, QW), F32), jax.ShapeDtypeStruct((lp, KVW), F32),
                   jax.ShapeDtypeStruct((lp, KVW), F32), jax.ShapeDtypeStruct((8, 128), F32)],
        scratch_shapes=[pltpu.VMEM((lp, KVW), F32), pltpu.VMEM((lp, KVW), F32)],
        compiler_params=_cp(("arbitrary",)),
    )(sinks, qn, kn, kn, kn, v, v, v, do)


PW_ROWS = 64
SEG = TS // 8


def _segment_order():
    rho = jnp.arange(TS)
    token = SEG * (rho % 8) + rho // 8
    p = (token[:, None] == jnp.arange(TS)[None, :]).astype(BF)
    return p, p.T


def _reorder(p, x):
    hi = x.astype(BF)
    r1 = x - hi.astype(F32)
    mid = r1.astype(BF)
    lo = (r1 - mid.astype(F32)).astype(BF)
    return _dot(p, hi) + _dot(p, mid) + _dot(p, lo)


def _segment_scan(xr_ref, xi_ref, pw_ref, base, ls, c_r, c_i, reverse, visit=None):
    ar, ai = pw_ref[base:base + 8, ls], pw_ref[base + 8:base + 16, ls]
    order = list(range(SEG - 1, -1, -1)) if reverse else list(range(SEG))
    s_r = s_i = jnp.zeros_like(ar)
    for i in order:
        rows = pl.ds(8 * i, 8)
        s_r, s_i = ar * s_r - ai * s_i + xr_ref[rows, ls], ar * s_i + ai * s_r + xi_ref[rows, ls]
        xr_ref[rows, ls] = s_r
        xi_ref[rows, ls] = s_i
    rid = lax.broadcasted_iota(jnp.int32, s_r.shape, 0)
    if reverse:
        e_r = jnp.where(rid < 7, pltpu.roll(s_r, 7, axis=0), c_r)
        e_i = jnp.where(rid < 7, pltpu.roll(s_i, 7, axis=0), c_i)
    else:
        e_r = jnp.where(rid >= 1, pltpu.roll(s_r, 1, axis=0), c_r)
        e_i = jnp.where(rid >= 1, pltpu.roll(s_i, 1, axis=0), c_i)
    for n, k in enumerate((1, 2, 4)):
        shift = 8 - k if reverse else k
        t_r, t_i = pltpu.roll(e_r, shift, axis=0), pltpu.roll(e_i, shift, axis=0)
        kr = pw_ref[base + 16 + 16 * n:base + 24 + 16 * n, ls]
        ki = pw_ref[base + 24 + 16 * n:base + 32 + 16 * n, ls]
        e_r, e_i = e_r + kr * t_r - ki * t_i, e_i + kr * t_i + ki * t_r
    last = 0 if reverse else 7
    a16r, a16i = pw_ref[base + 16 + last:base + 17 + last, ls], pw_ref[base + 24 + last:base + 25 + last, ls]
    lr, li, fr, fi = (t[last:last + 1] for t in (e_r, e_i, s_r, s_i))
    out = (a16r * lr - a16i * li + fr, a16r * li + a16i * lr + fi)
    d_r, d_i = e_r, e_i
    extra = None
    for i in order:
        rows = pl.ds(8 * i, 8)
        d_r, d_i = ar * d_r - ai * d_i, ar * d_i + ai * d_r
        f_r, f_i = xr_ref[rows, ls] + d_r, xi_ref[rows, ls] + d_i
        xr_ref[rows, ls] = f_r
        xi_ref[rows, ls] = f_i
        if visit is not None:
            extra = visit(i, f_r, f_i, extra)
    return out if visit is None else (out, extra)


def ssm_fwd(u, pw, br, bi, cr, ci, dsk, perm):
    lp = u.shape[0]

    def body(u_ref, pw_ref, br_ref, bi_ref, cr_ref, ci_ref, d_ref, p_ref, pt_ref,
             y_ref, sr_ref, si_ref, car_r, car_i, yp_s):
        i = pl.program_id(0)

        @pl.when(i == 0)
        def _():
            car_r[...] = jnp.zeros_like(car_r)
            car_i[...] = jnp.zeros_like(car_i)

        u = u_ref[...]
        ub = _dot(p_ref[...], u.astype(BF)).astype(BF)
        for j in range(NLC):
            uj = ub[:, 128 * j:128 * j + 128]
            sr_ref[:, LC * j:LC * j + LC] = _dot(uj, br_ref[j])
            si_ref[:, LC * j:LC * j + LC] = _dot(uj, bi_ref[j])
        for j in range(NLC):
            ls = slice(LC * j, LC * j + LC)
            car_r[:, ls], car_i[:, ls] = _segment_scan(sr_ref, si_ref, pw_ref, 0, ls, car_r[:, ls], car_i[:, ls],
                                                       False)
        for j in range(NLC):
            ls = slice(LC * j, LC * j + LC)
            yp_s[:, 128 * j:128 * j + 128] = (_dot(sr_ref[:, ls].astype(BF), cr_ref[j])
                                              - _dot(si_ref[:, ls].astype(BF), ci_ref[j]))
        y_ref[...] = _reorder(pt_ref[...], yp_s[...]) + d_ref[...] * u

    p, pt = perm
    return pl.pallas_call(
        body, name="ssm_fwd", grid=(lp // TS,),
        in_specs=[_rows(TS, D), _full((2 * PW_ROWS, SW)), _full((NLC, 128, LC)), _full((NLC, 128, LC)),
                  _full((NLC, LC, 128)), _full((NLC, LC, 128)), _full((1, D)), _full((TS, TS)), _full((TS, TS))],
        out_specs=[_rows(TS, D), _rows(TS, SW), _rows(TS, SW)],
        out_shape=[jax.ShapeDtypeStruct((lp, D), F32), jax.ShapeDtypeStruct((lp, SW), F32),
                   jax.ShapeDtypeStruct((lp, SW), F32)],
        scratch_shapes=[pltpu.VMEM((1, SW), F32), pltpu.VMEM((1, SW), F32), pltpu.VMEM((TS, D), F32)],
        compiler_params=_cp(("arbitrary",)),
    )(u, pw, br, bi, cr, ci, dsk, p, pt)


def ssm_bwd(dy, u, sr, si, pw, br, bi, cr, ci, dsk, perm):
    lp = u.shape[0]
    nt = lp // TS

    def rev(i):
        return (nt - 1 - i, 0)

    def prev8(i):
        return (jnp.maximum((nt - 1 - i) * (TS // 8) - 1, 0), 0)

    def body(dy_ref, u_ref, sr_ref, si_ref, pr8_ref, pi8_ref, pw_ref, br_ref, bi_ref, cr_ref, ci_ref, d_ref,
             p_ref, pt_ref, du_ref, da_ref, dbr_ref, dbi_ref, dcr_ref, dci_ref, dd_ref,
             gr_s, gi_s, car_r, car_i, dup_s):
        i = pl.program_id(0)

        @pl.when(i == 0)
        def _():
            car_r[...] = jnp.zeros_like(car_r)
            car_i[...] = jnp.zeros_like(car_i)
            for ref in (da_ref, dbr_ref, dbi_ref, dcr_ref, dci_ref, dd_ref):
                ref[...] = jnp.zeros_like(ref)

        keep = 1.0 - (i == nt - 1).astype(F32)
        dy = dy_ref[...]
        u = u_ref[...]
        dd_ref[...] += jnp.sum(dy * u, axis=0, keepdims=True)
        pm = p_ref[...]
        dyb = _dot(pm, dy.astype(BF)).astype(BF)
        ub = _dot(pm, u.astype(BF)).astype(BF)
        for j in range(NLC):
            ls = slice(LC * j, LC * j + LC)
            dyj = dyb[:, 128 * j:128 * j + 128]
            gr_s[:, ls] = _dot_nt(dyj, cr_ref[j])
            gi_s[:, ls] = -_dot_nt(dyj, ci_ref[j])
            dcr_ref[j] += _dot_tn(dyj, sr_ref[:, ls].astype(BF))
            dci_ref[j] -= _dot_tn(dyj, si_ref[:, ls].astype(BF))
        rid = lax.broadcasted_iota(jnp.int32, (8, LC), 0)
        for j in range(NLC):
            ls = slice(LC * j, LC * j + LC)

            def accumulate(i, g_r, g_i, acc):
                if i >= 1:
                    rows = pl.ds(8 * (i - 1), 8)
                    p_r, p_i = sr_ref[rows, ls], si_ref[rows, ls]
                else:
                    rows = pl.ds(8 * (SEG - 1), 8)
                    p_r = jnp.where(rid >= 1, pltpu.roll(sr_ref[rows, ls], 1, axis=0), pr8_ref[7:8, ls] * keep)
                    p_i = jnp.where(rid >= 1, pltpu.roll(si_ref[rows, ls], 1, axis=0), pi8_ref[7:8, ls] * keep)
                t_r, t_i = g_r * p_r + g_i * p_i, g_i * p_r - g_r * p_i
                return (t_r, t_i) if acc is None else (acc[0] + t_r, acc[1] + t_i)

            (car_r[:, ls], car_i[:, ls]), (t_r, t_i) = _segment_scan(
                gr_s, gi_s, pw_ref, PW_ROWS, ls, car_r[:, ls], car_i[:, ls], True, accumulate)
            da_ref[0:8, ls] += t_r
            da_ref[8:16, ls] += t_i
        for j in range(NLC):
            ls = slice(LC * j, LC * j + LC)
            uj = ub[:, 128 * j:128 * j + 128]
            grb = gr_s[:, ls].astype(BF)
            gib = gi_s[:, ls].astype(BF)
            dup_s[:, 128 * j:128 * j + 128] = _dot_nt(grb, br_ref[j]) + _dot_nt(gib, bi_ref[j])
            dbr_ref[j] += _dot_tn(uj, grb)
            dbi_ref[j] += _dot_tn(uj, gib)
        du_ref[...] = (_reorder(pt_ref[...], dup_s[...]) + d_ref[...] * dy).astype(BF)

    p, pt = perm
    return pl.pallas_call(
        body, name="ssm_bwd", grid=(nt,),
        in_specs=[pl.BlockSpec((TS, D), rev), pl.BlockSpec((TS, D), rev), pl.BlockSpec((TS, SW), rev),
                  pl.BlockSpec((TS, SW), rev), pl.BlockSpec((8, SW), prev8), pl.BlockSpec((8, SW), prev8),
                  _full((2 * PW_ROWS, SW)), _full((NLC, 128, LC)), _full((NLC, 128, LC)),
                  _full((NLC, LC, 128)), _full((NLC, LC, 128)), _full((1, D)), _full((TS, TS)), _full((TS, TS))],
        out_specs=[pl.BlockSpec((TS, D), rev), _full((16, SW)), _full((NLC, 128, LC)), _full((NLC, 128, LC)),
                   _full((NLC, 128, LC)), _full((NLC, 128, LC)), _full((1, D))],
        out_shape=[jax.ShapeDtypeStruct((lp, D), BF), jax.ShapeDtypeStruct((16, SW), F32),
                   jax.ShapeDtypeStruct((NLC, 128, LC), F32), jax.ShapeDtypeStruct((NLC, 128, LC), F32),
                   jax.ShapeDtypeStruct((NLC, 128, LC), F32), jax.ShapeDtypeStruct((NLC, 128, LC), F32),
                   jax.ShapeDtypeStruct((1, D), F32)],
        scratch_shapes=[pltpu.VMEM((TS, SW), F32), pltpu.VMEM((TS, SW), F32),
                        pltpu.VMEM((1, SW), F32), pltpu.VMEM((1, SW), F32), pltpu.VMEM((TS, D), F32)],
        compiler_params=_cp(("arbitrary",)),
    )(dy, u, sr, si, sr, si, pw, br, bi, cr, ci, dsk, p, pt)


def _merge_parts(attn, zz, gg, abn, sbn):
    za, zb = zz[:, 0:D], zz[:, D:2 * D]
    sgz = _sigmoid(zb)
    ssm = za * sgz
    ra = lax.rsqrt(jnp.mean(attn * attn, axis=-1, keepdims=True) + EPS)
    rs = lax.rsqrt(jnp.mean(ssm * ssm, axis=-1, keepdims=True) + EPS)
    ah, sh = attn * ra, ssm * rs
    sga, sgs = _sigmoid(gg[:, 0:D]), _sigmoid(gg[:, D:2 * D])
    return za, sgz, ra, rs, ah, sh, sga, sgs, ah * abn, sh * sbn


def mid_fwd(y, attn, gg, x, mpad, w_glu, w_out, abn, sbn):
    lp = y.shape[0]

    def body(y_ref, a_ref, gg_ref, x_ref, m_ref, wg_ref, wo_ref, abn_ref, sbn_ref, z_o, zz_o, mg_o, h1_o):
        z, _ = _gelu(y_ref[...])
        zb = z.astype(BF)
        z_o[...] = zb
        zz = _dot(zb, wg_ref[...])
        zz_o[...] = zz.astype(BF)
        _, _, _, _, _, _, sga, sgs, an, sn = _merge_parts(a_ref[...], zz, gg_ref[...], abn_ref[...], sbn_ref[...])
        mb = (sga * an + sgs * sn).astype(BF)
        mg_o[...] = mb
        h1_o[...] = _seq_tile(x_ref, m_ref) + _dot(mb, wo_ref[...])

    return pl.pallas_call(
        body, name="mid_fwd", grid=(lp // TM,),
        in_specs=[_rows(TM, D), _rows(TM, D), _rows(TM, 2 * D)] + _seq_specs() + [_full((D, 2 * D)), _full((D, D)),
                  _full((1, D)), _full((1, D))],
        out_specs=[_rows(TM, D), _rows(TM, 2 * D), _rows(TM, D), _rows(TM, D)],
        out_shape=[jax.ShapeDtypeStruct((lp, D), BF), jax.ShapeDtypeStruct((lp, 2 * D), BF),
                   jax.ShapeDtypeStruct((lp, D), BF), jax.ShapeDtypeStruct((lp, D), F32)],
        compiler_params=_cp(("arbitrary",)),
    )(y, attn, gg, x, mpad, w_glu, w_out, abn, sbn)


def mid_bwd(dh1b, y, attn, gg, zzb, w_glu, w_out, abn, sbn):
    lp = y.shape[0]

    def body(dh_ref, y_ref, a_ref, gg_ref, zz_ref, wg_ref, wo_ref, abn_ref, sbn_ref,
             dy_o, dattn_o, dgg_o, dzz_o, dabn_o, dsbn_o):
        i = pl.program_id(0)

        @pl.when(i == 0)
        def _():
            dabn_o[...] = jnp.zeros_like(dabn_o)
            dsbn_o[...] = jnp.zeros_like(dsbn_o)

        dm = _dot_nt(dh_ref[...], wo_ref[...])
        abn, sbn = abn_ref[...], sbn_ref[...]
        za, sgz, ra, rs, ah, sh, sga, sgs, an, sn = _merge_parts(
            a_ref[...], zz_ref[...].astype(F32), gg_ref[...], abn, sbn)
        dgg_o[:, 0:D] = (dm * an * sga * (1.0 - sga)).astype(BF)
        dgg_o[:, D:2 * D] = (dm * sn * sgs * (1.0 - sgs)).astype(BF)
        da = dm * sga
        dabn_o[...] += jnp.sum(da * ah, axis=0, keepdims=True)
        dah = da * abn
        dattn_o[...] = (ra * (dah - ah * jnp.mean(dah * ah, axis=-1, keepdims=True))).astype(BF)
        ds = dm * sgs
        dsbn_o[...] += jnp.sum(ds * sh, axis=0, keepdims=True)
        dsh = ds * sbn
        dssm = rs * (dsh - sh * jnp.mean(dsh * sh, axis=-1, keepdims=True))
        dzz_o[:, 0:D] = (dssm * sgz).astype(BF)
        dzz_o[:, D:2 * D] = (dssm * za * sgz * (1.0 - sgz)).astype(BF)
        dz = _dot_nt(dzz_o[...], wg_ref[...])
        yv = y_ref[...]
        _, t = _gelu(yv)
        dy_o[...] = dz * _gelu_grad(yv, t)

    return pl.pallas_call(
        body, name="mid_bwd", grid=(lp // TM,),
        in_specs=[_rows(TM, D), _rows(TM, D), _rows(TM, D), _rows(TM, 2 * D), _rows(TM, 2 * D),
                  _full((D, 2 * D)), _full((D, D)), _full((1, D)), _full((1, D))],
        out_specs=[_rows(TM, D), _rows(TM, D), _rows(TM, 2 * D), _rows(TM, 2 * D), _full((1, D)), _full((1, D))],
        out_shape=[jax.ShapeDtypeStruct((lp, D), F32), jax.ShapeDtypeStruct((lp, D), BF),
                   jax.ShapeDtypeStruct((lp, 2 * D), BF), jax.ShapeDtypeStruct((lp, 2 * D), BF),
                   jax.ShapeDtypeStruct((1, D), F32), jax.ShapeDtypeStruct((1, D), F32)],
        compiler_params=_cp(("arbitrary",)),
    )(dh1b, y, attn, gg, zzb, w_glu, w_out, abn, sbn)


def ffn_fwd_bwd(h1, tgt, w_fi, w_fo, gf):
    lp = h1.shape[0]
    tf = 256
    nch = 2
    cw = DFF // nch

    def body(h_ref, t_ref, wi_ref, wo_ref, gf_ref, hn_o, dgu_o, act_o, dh2_o, dh1_o, dh1b_o, dgf_o, loss_o,
             gate_s, up_s):
        i = pl.program_id(0)

        @pl.when(i == 0)
        def _():
            dgf_o[...] = jnp.zeros_like(dgf_o)
            loss_o[...] = jnp.zeros_like(loss_o)

        h = h_ref[...]
        r = lax.rsqrt(jnp.mean(h * h, axis=-1, keepdims=True) + EPS)
        xh = h * r
        hb = (xh * gf_ref[...]).astype(BF)
        hn_o[...] = hb
        h2 = h
        for c in range(nch):
            gate = _dot(hb, wi_ref[:, cw * c:cw * (c + 1)])
            up = _dot(hb, wi_ref[:, DFF + cw * c:DFF + cw * (c + 1)])
            gate_s[:, cw * c:cw * (c + 1)] = gate
            up_s[:, cw * c:cw * (c + 1)] = up
            ab = (gate * _sigmoid(gate) * up).astype(BF)
            act_o[:, cw * c:cw * (c + 1)] = ab
            h2 = h2 + _dot(ab, wo_ref[cw * c:cw * (c + 1), :])
        real = (i >= PAD // tf).astype(F32)
        e = (h2 - t_ref[...]) * real
        loss_o[...] += 0.5 * jnp.sum(jnp.sum(e * e, axis=-1, keepdims=True), axis=0, keepdims=True) * (1.0 / D)
        dh2 = e * (1.0 / D)
        db = dh2.astype(BF)
        dh2_o[...] = db
        dhn = jnp.zeros((tf, D), F32)
        for c in range(nch):
            gate = gate_s[:, cw * c:cw * (c + 1)]
            up = up_s[:, cw * c:cw * (c + 1)]
            sg = _sigmoid(gate)
            dact = _dot_nt(db, wo_ref[cw * c:cw * (c + 1), :])
            dgate = (dact * up * (sg * (1.0 + gate * (1.0 - sg)))).astype(BF)
            dup = (dact * (gate * sg)).astype(BF)
            dgu_o[:, cw * c:cw * (c + 1)] = dgate
            dgu_o[:, DFF + cw * c:DFF + cw * (c + 1)] = dup
            dhn += (_dot_nt(dgate, wi_ref[:, cw * c:cw * (c + 1)])
                    + _dot_nt(dup, wi_ref[:, DFF + cw * c:DFF + cw * (c + 1)]))
        dgf_o[...] += jnp.sum(dhn * xh, axis=0, keepdims=True)
        dxh = dhn * gf_ref[...]
        dh1 = dh2 + r * (dxh - xh * jnp.mean(dxh * xh, axis=-1, keepdims=True))
        dh1_o[...] = dh1
        dh1b_o[...] = dh1.astype(BF)

    return pl.pallas_call(
        body, name="ffn_fwd_bwd", grid=(lp // tf,),
        in_specs=[_rows(tf, D), pl.BlockSpec((tf, D), lambda i: (jnp.maximum(i - PAD // tf, 0), 0)),
                  _const((D, 2 * DFF)), _const((DFF, D)), _full((1, D))],
        out_specs=[_rows(tf, D), _rows(tf, 2 * DFF), _rows(tf, DFF), _rows(tf, D), _rows(tf, D), _rows(tf, D),
                   _full((1, D)), _full((1, 1))],
        out_shape=[jax.ShapeDtypeStruct((lp, D), BF), jax.ShapeDtypeStruct((lp, 2 * DFF), BF),
                   jax.ShapeDtypeStruct((lp, DFF), BF), jax.ShapeDtypeStruct((lp, D), BF),
                   jax.ShapeDtypeStruct((lp, D), F32), jax.ShapeDtypeStruct((lp, D), BF),
                   jax.ShapeDtypeStruct((1, D), F32), jax.ShapeDtypeStruct((1, 1), F32)],
        scratch_shapes=[pltpu.VMEM((tf, DFF), F32), pltpu.VMEM((tf, DFF), F32)],
        compiler_params=_cp(("arbitrary",), vmem_mb=58),
    )(h1, tgt, w_fi, w_fo, gf)


def weight_grad(name, a, b, tn):
    lp, k = a.shape
    n = b.shape[1]
    tr = next(t for t in (2112, 1056, 768, 512, 384, 256, 128) if lp % t == 0 and t * k * 2 <= (5 << 20))

    def body(a_ref, b_ref, o_ref):
        @pl.when(pl.program_id(1) == 0)
        def _():
            o_ref[...] = jnp.zeros_like(o_ref)

        o_ref[...] += _dot_tn(a_ref[...], b_ref[...])

    return pl.pallas_call(
        body, name=name, grid=(n // tn, lp // tr),
        in_specs=[pl.BlockSpec((tr, k), lambda j, m: (m, 0)), pl.BlockSpec((tr, tn), lambda j, m: (m, j))],
        out_specs=pl.BlockSpec((k, tn), lambda j, m: (0, j)),
        out_shape=jax.ShapeDtypeStruct((k, n), F32),
        compiler_params=_cp(("arbitrary", "arbitrary")),
    )(a, b)


def _adamw_math(w, g, m, v):
    m = B1 * m + (1.0 - B1) * g
    v = B2 * v + (1.0 - B2) * (g * g)
    delta = -LR * ((m / BC1) / (jnp.sqrt(v / BC2) + AEPS) + WD * w)
    return delta, m, v


def _row_tile(r):
    return next((t for t in (256, 128, 64, 32, 16, 8) if r % t == 0), r)


def adamw(name, w, g, m, v):
    r, c = w.shape
    tr = _row_tile(r)

    def body(w_ref, g_ref, m_ref, v_ref, d_o, m_o, v_o):
        d_o[...], m_o[...], v_o[...] = _adamw_math(w_ref[...], g_ref[...], m_ref[...], v_ref[...])

    spec = pl.BlockSpec((tr, c), lambda i: (i, 0))
    return pl.pallas_call(
        body, name=name, grid=(r // tr,), in_specs=[spec] * 4, out_specs=[spec] * 3,
        out_shape=[jax.ShapeDtypeStruct((r, c), F32)] * 3,
        compiler_params=_cp(("arbitrary",)),
    )(w, g, m, v)


def adamw_small(ws, gs, ms, vs):
    n = len(ws)
    steps = 8

    def spec(a):
        if a.ndim == 4:
            return pl.BlockSpec((1, a.shape[1] // steps) + a.shape[2:], lambda i: (0, i, 0, 0))
        nd = a.ndim
        return pl.BlockSpec(a.shape, lambda i: (0,) * nd)

    def body(*refs):
        w, g, m, v, d_o, m_o, v_o = (refs[k * n:(k + 1) * n] for k in range(7))
        for t in range(n):
            d_o[t][...], m_o[t][...], v_o[t][...] = _adamw_math(w[t][...], g[t][...], m[t][...], v[t][...])

    specs = [spec(a) for a in ws]
    res = pl.pallas_call(
        body, name="adamw_small", grid=(steps,), in_specs=specs * 4, out_specs=specs * 3,
        out_shape=[jax.ShapeDtypeStruct(a.shape, F32) for a in ws] * 3,
        compiler_params=_cp(("arbitrary",)),
    )(*ws, *gs, *ms, *vs)
    return res[:n], res[n:2 * n], res[2 * n:]


def pair_sum(name, place, full, got, ax):
    r, c = got.shape
    tr = _row_tile(r)

    def body(s_ref, a_ref, b_ref, o_ref):
        o_ref[...] = (a_ref[...] + b_ref[...]).astype(BF)

    if ax == 1:
        mine = pl.BlockSpec((tr, c), lambda i, s: (s[1] * (r // tr) + i, 0))
    else:
        mine = pl.BlockSpec((tr, c), lambda i, s: (i, s[1]))
    spec = pl.BlockSpec((tr, c), lambda i, s: (i, 0))
    return pl.pallas_call(
        body, name=name,
        grid_spec=pltpu.PrefetchScalarGridSpec(num_scalar_prefetch=1, grid=(r // tr,), in_specs=[mine, spec],
                                               out_specs=spec),
        out_shape=jax.ShapeDtypeStruct((r, c), BF),
        compiler_params=_cp(("arbitrary",)),
    )(place, full, got)


def sum4(name, place, parts, half, ax):
    _, r, c = parts.shape
    tr = _row_tile(r)

    def body(s_ref, p_ref, h_ref, o_ref):
        o_ref[...] = ((p_ref[0].astype(F32) + p_ref[1].astype(F32))
                      + (p_ref[2].astype(F32) + h_ref[...].astype(F32)))

    if ax == 1:
        own = pl.BlockSpec((tr, c), lambda i, s: (i, s[0]))
        out = pl.BlockSpec((tr, c), lambda i, s: (s[1] * (r // tr) + i, 0))
        shape = (2 * r, c)
    else:
        own = pl.BlockSpec((tr, c), lambda i, s: (s[0] * (r // tr) + i, 0))
        out = pl.BlockSpec((tr, c), lambda i, s: (i, s[1]))
        shape = (r, 2 * c)
    return pl.pallas_call(
        body, name=name,
        grid_spec=pltpu.PrefetchScalarGridSpec(
            num_scalar_prefetch=1, grid=(r // tr,),
            in_specs=[pl.BlockSpec((3, tr, c), lambda i, s: (0, i, 0)), own], out_specs=out),
        out_shape=jax.ShapeDtypeStruct(shape, F32),
        compiler_params=_cp(("arbitrary",)),
    )(place, parts, half)


def _place():
    x, y, c = lax.axis_index("x"), lax.axis_index("y"), lax.axis_index("c")
    return x, y, c


def _other_chips(x, y):
    return [(1 - x, y), (x, 1 - y), (1 - x, 1 - y)]


SHARDED = (("w_in", D, IN_COLS, 1), ("w_glu", D, 2 * D, 1), ("w_out", D, D, 0),
           ("w_ffn_in", D, 2 * DFF, 1), ("w_ffn_out", DFF, D, 0))


def _shard_of(ref, axis, k, size):
    if axis == 0:
        return ref.at[pl.ds(k * size, size), :]
    return ref.at[:, pl.ds(k * size, size)]


def _gather_piece(full, dims, k, h):
    r, cc, ax = dims
    if ax == 0:
        return full.at[pl.ds(k * (r // 4) + h * (r // 8), r // 8), :]
    return full.at[pl.ds(h * (r // 2), r // 2), pl.ds(k * (cc // 4), cc // 4)]


def _rows_half(shard, h):
    rs = shard.shape[0]
    return shard.at[pl.ds(h * (rs // 2), rs // 2), :]


def gather_weights(shards, dims):
    items = [(s,) + tuple(d) for s, d in zip(shards, dims)]
    n = len(items)

    def body(*refs):
        ins, outs = refs[:n], refs[n:2 * n]
        ici_send, ici_recv, d2d_send, d2d_recv, local_sems = refs[2 * n:]
        x, y, c = _place()
        mine = 2 * x + y
        chips = _other_chips(x, y)

        def piece(t, k, h):
            return _gather_piece(outs[t], dims[t], k, h)

        def src_half(t, h):
            return _rows_half(ins[t], h)

        local, first, passed = [], [], []
        for t in range(n):
            _, r, cc, ax = items[t]
            size = (r if ax == 0 else cc) // 4
            loc = pltpu.make_async_remote_copy(
                src_ref=ins[t], dst_ref=_shard_of(outs[t], ax, mine, size), send_sem=local_sems.at[t],
                recv_sem=local_sems.at[n + t], device_id=(x, y, 1 - c), device_id_type=MESH)
            loc.start()
            local.append(loc)
            for j, chip in enumerate(chips):
                cp = pltpu.make_async_remote_copy(
                    src_ref=src_half(t, c), dst_ref=piece(t, mine, c),
                    send_sem=ici_send.at[3 * t + j], recv_sem=ici_recv.at[3 * t + j],
                    device_id=(*chip, c), device_id_type=MESH)
                cp.start()
                first.append(cp)
        for t in range(n):
            for j, (px, py) in enumerate(chips):
                got = piece(t, 2 * px + py, c)
                pltpu.make_async_remote_copy(
                    src_ref=src_half(t, c), dst_ref=got, send_sem=ici_send.at[3 * t + j],
                    recv_sem=ici_recv.at[3 * t + j], device_id=(px, py, c), device_id_type=MESH).wait_recv()
                cp = pltpu.make_async_remote_copy(
                    src_ref=got, dst_ref=got, send_sem=d2d_send.at[3 * t + j], recv_sem=d2d_recv.at[3 * t + j],
                    device_id=(x, y, 1 - c), device_id_type=MESH)
                cp.start()
                passed.append(cp)
        for t in range(n):
            for j, (px, py) in enumerate(chips):
                other = piece(t, 2 * px + py, 1 - c)
                pltpu.make_async_remote_copy(
                    src_ref=other, dst_ref=other, send_sem=d2d_send.at[3 * t + j], recv_sem=d2d_recv.at[3 * t + j],
                    device_id=(x, y, 1 - c), device_id_type=MESH).wait_recv()
        for cp in first + passed:
            cp.wait_send()
        for loc in local:
            loc.wait()

    return pl.pallas_call(
        body, name="gather_weights",
        in_specs=[ANY] * n, out_specs=[ANY] * n,
        out_shape=[jax.ShapeDtypeStruct((r, c), s.dtype) for s, r, c, _ in items],
        scratch_shapes=[pltpu.SemaphoreType.DMA((3 * n,)), pltpu.SemaphoreType.DMA((3 * n,)),
                        pltpu.SemaphoreType.DMA((3 * n,)), pltpu.SemaphoreType.DMA((3 * n,)),
                        pltpu.SemaphoreType.DMA((2 * n,))],
        compiler_params=pltpu.CompilerParams(has_side_effects=True),
    )(*shards)


SEM = pl.BlockSpec(memory_space=pltpu.SEMAPHORE)
HBM = pl.BlockSpec(memory_space=pltpu.HBM)
EFFECT = pltpu.SideEffectType.DATAFLOW_SIDE_EFFECTING


def _in_hbm(a):
    return pltpu.with_memory_space_constraint(a, pltpu.HBM)


def _split_copy_start(name, copies, ncopies, srcs, lands, after):
    ns, nl = len(srcs), len(lands)

    def body(*refs):
        ins, land_refs = refs[:ns], refs[ns:ns + nl]
        send_sems, recv_sems = refs[ns + nl + 1], refs[ns + nl + 2]
        token = refs[-1]
        for k, (src, dst, dev) in enumerate(copies(ins, land_refs)):
            pltpu.make_async_remote_copy(src_ref=src, dst_ref=dst, send_sem=send_sems.at[k],
                                         recv_sem=recv_sems.at[k], device_id=dev, device_id_type=MESH).start()
        token[...] = jnp.zeros_like(token)

    res = pl.pallas_call(
        body, name=name,
        in_specs=[HBM] * (ns + nl) + [ANY],
        out_specs=(SEM, SEM) + (HBM,) * (ns + nl) + (pl.BlockSpec(memory_space=pltpu.VMEM),),
        out_shape=(pltpu.SemaphoreType.DMA((ncopies,)), pltpu.SemaphoreType.DMA((ncopies,)))
        + tuple(pltpu.HBM(a.shape, a.dtype) for a in list(srcs) + list(lands))
        + (jax.ShapeDtypeStruct((8, 128), F32),),
        input_output_aliases={t: 2 + t for t in range(ns + nl)},
        compiler_params=pltpu.CompilerParams(has_side_effects=EFFECT),
    )(*[_in_hbm(a) for a in srcs], *[_in_hbm(a) for a in lands], after)
    return res[0], res[1], list(res[2:2 + ns]), list(res[2 + ns:2 + ns + nl]), res[-1]


def _split_copy_wait(name, copies, arrivals, send_sems, recv_sems, srcs, lands, after):
    ns, nl = len(srcs), len(lands)

    def body(*refs):
        ins, land_refs = refs[:ns], refs[ns:ns + nl]
        send_sems_ref, recv_sems_ref = refs[ns + nl], refs[ns + nl + 1]
        mine = copies(ins, land_refs)
        for k, ((src, dst, dev), got) in enumerate(zip(mine, arrivals(ins, land_refs))):
            pltpu.make_async_remote_copy(src_ref=src, dst_ref=dst, send_sem=send_sems_ref.at[k],
                                         recv_sem=recv_sems_ref.at[k], device_id=dev,
                                         device_id_type=MESH).wait_send()
            pltpu.make_async_remote_copy(src_ref=got, dst_ref=got, send_sem=send_sems_ref.at[k],
                                         recv_sem=recv_sems_ref.at[k], device_id=dev,
                                         device_id_type=MESH).wait_recv()

    res = pl.pallas_call(
        body, name=name,
        in_specs=[HBM] * (ns + nl) + [SEM, SEM, ANY],
        out_specs=(HBM,) * (ns + nl),
        out_shape=tuple(pltpu.HBM(a.shape, a.dtype) for a in list(srcs) + list(lands)),
        input_output_aliases={t: t for t in range(ns + nl)},
        compiler_params=pltpu.CompilerParams(has_side_effects=EFFECT),
    )(*srcs, *lands, send_sems, recv_sems, after)
    return list(res[:ns]), list(res[ns:])


def _gather_copies(dims):
    def copies(ins, lands):
        x, y, c = _place()
        mine = 2 * x + y
        out = []
        for t, d in enumerate(dims):
            size = (d[0] if d[2] == 0 else d[1]) // 4
            for chip in _other_chips(x, y):
                out.append((_rows_half(ins[t], c), _gather_piece(lands[t], d, mine, c), (*chip, c)))
            out.append((ins[t], _shard_of(lands[t], d[2], mine, size), (x, y, 1 - c)))
        return out

    def arrivals(ins, lands):
        x, y, c = _place()
        mine = 2 * x + y
        out = []
        for t, d in enumerate(dims):
            size = (d[0] if d[2] == 0 else d[1]) // 4
            for px, py in _other_chips(x, y):
                out.append(_gather_piece(lands[t], d, 2 * px + py, c))
            out.append(_shard_of(lands[t], d[2], mine, size))
        return out

    return copies, arrivals


def gather_forward(fulls, dims):
    n = len(fulls)

    def body(*refs):
        ins, outs = refs[:n], refs[n:2 * n]
        send_sems, recv_sems = refs[2 * n:]
        x, y, c = _place()
        cps = []
        for t in range(n):
            for j, (px, py) in enumerate(_other_chips(x, y)):
                cp = pltpu.make_async_remote_copy(
                    src_ref=_gather_piece(ins[t], dims[t], 2 * px + py, c),
                    dst_ref=_gather_piece(outs[t], dims[t], 2 * px + py, c),
                    send_sem=send_sems.at[3 * t + j], recv_sem=recv_sems.at[3 * t + j],
                    device_id=(x, y, 1 - c), device_id_type=MESH)
                cp.start()
                cps.append(cp)
        for t in range(n):
            for j, (px, py) in enumerate(_other_chips(x, y)):
                cps[3 * t + j].wait_send()
                other = _gather_piece(outs[t], dims[t], 2 * px + py, 1 - c)
                pltpu.make_async_remote_copy(
                    src_ref=other, dst_ref=other, send_sem=send_sems.at[3 * t + j], recv_sem=recv_sems.at[3 * t + j],
                    device_id=(x, y, 1 - c), device_id_type=MESH).wait_recv()

    return pl.pallas_call(
        body, name="gather_forward",
        in_specs=[ANY] * n, out_specs=[ANY] * n,
        out_shape=[jax.ShapeDtypeStruct(a.shape, a.dtype) for a in fulls],
        input_output_aliases={t: t for t in range(n)},
        scratch_shapes=[pltpu.SemaphoreType.DMA((3 * n,)), pltpu.SemaphoreType.DMA((3 * n,))],
        compiler_params=pltpu.CompilerParams(has_side_effects=True),
    )(*fulls)


def _half_of(ref, ax, h):
    r, c = ref.shape
    if ax == 1:
        return ref.at[pl.ds(h * (r // 2), r // 2), :]
    return ref.at[:, pl.ds(h * (c // 2), c // 2)]


def _half_shape(r, c, ax):
    return (r // 2, c) if ax == 1 else (r, c // 2)


def pair_exchange(name, grads, idxs):
    n = len(grads)
    which = [SHARDED[i] for i in idxs]

    def body(*refs):
        ins, got = refs[:n], refs[n:2 * n]
        send_sems, recv_sems = refs[2 * n:]
        x, y, c = _place()
        cps = []
        for t, (_, _, _, ax) in enumerate(which):
            cp = pltpu.make_async_remote_copy(
                src_ref=_half_of(ins[t], ax, 1 - c), dst_ref=got[t], send_sem=send_sems.at[t],
                recv_sem=recv_sems.at[t], device_id=(x, y, 1 - c), device_id_type=MESH)
            cp.start()
            cps.append(cp)
        for cp in cps:
            cp.wait()

    return pl.pallas_call(
        body, name=name,
        in_specs=[ANY] * n, out_specs=[ANY] * n,
        out_shape=[jax.ShapeDtypeStruct(_half_shape(r, c, ax), F32) for _, r, c, ax in which],
        scratch_shapes=[pltpu.SemaphoreType.DMA((n,)), pltpu.SemaphoreType.DMA((n,))],
        compiler_params=pltpu.CompilerParams(has_side_effects=True),
    )(*grads)


def _piece_shapes(idxs):
    out = []
    for _, r, c, ax in [SHARDED[i] for i in idxs]:
        hr, hc = _half_shape(r, c, ax)
        out.append((hr // 4, hc) if ax == 0 else (hr, hc // 4))
    return out


def _scatter_copies(idxs):
    which = [SHARDED[i] for i in idxs]

    def copies(ins, lands):
        x, y, c = _place()
        out = []
        for t, (_, _, _, ax) in enumerate(which):
            size = ins[t].shape[ax] // 4
            for j, (px, py) in enumerate(_other_chips(x, y)):
                out.append((_shard_of(ins[t], ax, 2 * px + py, size), lands[t].at[j], (px, py, c)))
        return out

    def arrivals(ins, lands):
        return [lands[t].at[j] for t in range(len(which)) for j in range(3)]

    return copies, arrivals


def scatter_grads(halves, idxs):
    n = len(halves)
    which = [SHARDED[i] for i in idxs]

    def shapes():
        return _piece_shapes(idxs)

    def body(*refs):
        ins, outs = refs[:n], refs[n:2 * n]
        send_sems, recv_sems = refs[2 * n:]
        x, y, c = _place()
        started = []
        for t, (src, dst, (_, _, _, ax)) in enumerate(zip(ins, outs, which)):
            size = src.shape[ax] // 4
            for j, (px, py) in enumerate(_other_chips(x, y)):
                cp = pltpu.make_async_remote_copy(
                    src_ref=_shard_of(src, ax, 2 * px + py, size), dst_ref=dst.at[j],
                    send_sem=send_sems.at[3 * t + j], recv_sem=recv_sems.at[3 * t + j],
                    device_id=(px, py, c), device_id_type=MESH)
                cp.start()
                started.append(cp)
        for cp in started:
            cp.wait()

    return pl.pallas_call(
        body, name="scatter_grads",
        in_specs=[ANY] * n, out_specs=[ANY] * n,
        out_shape=[jax.ShapeDtypeStruct((3,) + s, BF) for s in shapes()],
        scratch_shapes=[pltpu.SemaphoreType.DMA((3 * n,)), pltpu.SemaphoreType.DMA((3 * n,))],
        compiler_params=pltpu.CompilerParams(has_side_effects=True),
    )(*halves)


def join_halves(name, shards, idxs):
    n = len(shards)
    which = [SHARDED[i] for i in idxs]

    def body(*refs):
        ins, outs = refs[:n], refs[n:2 * n]
        send_sems, recv_sems = refs[2 * n:]
        x, y, c = _place()
        cps = []
        for t, (_, _, _, ax) in enumerate(which):
            cp = pltpu.make_async_remote_copy(
                src_ref=_half_of(ins[t], ax, c), dst_ref=_half_of(outs[t], ax, c), send_sem=send_sems.at[t],
                recv_sem=recv_sems.at[t], device_id=(x, y, 1 - c), device_id_type=MESH)
            cp.start()
            cps.append(cp)
        for t, cp in enumerate(cps):
            cp.wait_send()
            theirs = _half_of(outs[t], which[t][3], 1 - c)
            pltpu.make_async_remote_copy(
                src_ref=theirs, dst_ref=theirs, send_sem=send_sems.at[t], recv_sem=recv_sems.at[t],
                device_id=(x, y, 1 - c), device_id_type=MESH).wait_recv()

    return pl.pallas_call(
        body, name=name,
        in_specs=[ANY] * n, out_specs=[ANY] * n,
        out_shape=[jax.ShapeDtypeStruct(s.shape, F32) for s in shards],
        input_output_aliases={t: t for t in range(n)},
        scratch_shapes=[pltpu.SemaphoreType.DMA((n,)), pltpu.SemaphoreType.DMA((n,))],
        compiler_params=pltpu.CompilerParams(has_side_effects=True),
    )(*shards)


def allreduce_small(pack):
    rows = pack.shape[0]
    hr = rows // 2

    def body(p_ref, o_ref, sib_ref, parts_ref, send_sems, recv_sems):
        x, y, c = _place()
        mine = 2 * x + y
        sibling = (x, y, 1 - c)
        swap = pltpu.make_async_remote_copy(
            src_ref=p_ref, dst_ref=sib_ref, send_sem=send_sems.at[0], recv_sem=recv_sems.at[0],
            device_id=sibling, device_id_type=MESH)
        swap.start()
        swap.wait()
        half = pl.ds(pl.multiple_of(c * hr, 8), hr)
        parts_ref[mine] = p_ref[half, :] + sib_ref[half, :]
        cps = []
        for j, chip in enumerate(_other_chips(x, y)):
            cp = pltpu.make_async_remote_copy(
                src_ref=parts_ref.at[mine], dst_ref=parts_ref.at[mine], send_sem=send_sems.at[1 + j],
                recv_sem=recv_sems.at[1 + j], device_id=(*chip, c), device_id_type=MESH)
            cp.start()
            cps.append(cp)
        for j, (px, py) in enumerate(_other_chips(x, y)):
            cps[j].wait_send()
            theirs = parts_ref.at[2 * px + py]
            pltpu.make_async_remote_copy(
                src_ref=theirs, dst_ref=theirs, send_sem=send_sems.at[1 + j], recv_sem=recv_sems.at[1 + j],
                device_id=(px, py, c), device_id_type=MESH).wait_recv()
        o_ref[half, :] = (parts_ref[0] + parts_ref[1]) + (parts_ref[2] + parts_ref[3])
        back = pltpu.make_async_remote_copy(
            src_ref=o_ref.at[half, :], dst_ref=o_ref.at[half, :], send_sem=send_sems.at[4],
            recv_sem=recv_sems.at[4], device_id=sibling, device_id_type=MESH)
        back.start()
        back.wait_send()
        other = pl.ds(pl.multiple_of((1 - c) * hr, 8), hr)
        pltpu.make_async_remote_copy(
            src_ref=o_ref.at[other, :], dst_ref=o_ref.at[other, :], send_sem=send_sems.at[4],
            recv_sem=recv_sems.at[4], device_id=sibling, device_id_type=MESH).wait_recv()

    return pl.pallas_call(
        body, name="allreduce_small",
        in_specs=[pl.BlockSpec(memory_space=pltpu.VMEM)], out_specs=pl.BlockSpec(memory_space=pltpu.VMEM),
        out_shape=jax.ShapeDtypeStruct((rows, 128), F32),
        scratch_shapes=[pltpu.VMEM((rows, 128), F32), pltpu.VMEM((4, hr, 128), F32),
                        pltpu.SemaphoreType.DMA((5,)), pltpu.SemaphoreType.DMA((5,))],
        compiler_params=pltpu.CompilerParams(has_side_effects=True, vmem_limit_bytes=40 << 20),
    )(pack)


def _ssm_discretize(lam_re, lam_im, log_dt, b_re, b_im):
    dt = jnp.exp(log_dt)[:, None]
    mag = jnp.exp(lam_re * dt)
    ar, ai = mag * jnp.cos(lam_im * dt), mag * jnp.sin(lam_im * dt)
    den = lam_re * lam_re + lam_im * lam_im
    nr, ni = ar - 1.0, ai
    fr, fi = (nr * lam_re + ni * lam_im) / den, (ni * lam_re - nr * lam_im) / den
    bbr = fr[..., None] * b_re - fi[..., None] * b_im
    bbi = fr[..., None] * b_im + fi[..., None] * b_re
    return ar, ai, bbr, bbi


def _cmul(a, b):
    return a[0] * b[0] - a[1] * b[1], a[0] * b[1] + a[1] * b[0]


def _scan_powers(ar, ai):
    a = (ar.reshape(1, SW), ai.reshape(1, SW))
    big = a
    for _ in range(SEG.bit_length() - 1):
        big = _cmul(big, big)
    assert 1 << (SEG.bit_length() - 1) == SEG
    pows = {1: big}
    pows[2] = _cmul(big, big)
    pows[4] = _cmul(pows[2], pows[2])
    rid = jnp.arange(8)[:, None]
    ones = jnp.ones((8, 1), F32)
    fwd, rev = [ones * a[0], ones * a[1]], [ones * a[0], -ones * a[1]]
    for k in (1, 2, 4):
        pr, pi = pows[k]
        fwd += [jnp.where(rid >= k, pr, 0.0), jnp.where(rid >= k, pi, 0.0)]
        rev += [jnp.where(rid < 8 - k, pr, 0.0), jnp.where(rid < 8 - k, -pi, 0.0)]
    return jnp.concatenate(fwd + rev, axis=0)


def _pack_b(bb):
    t = bb.reshape(NLC, 8, NSTATE, GCH).transpose(0, 1, 3, 2)
    return jnp.einsum("jgcp,gh->jgchp", t, jnp.eye(8, dtype=bb.dtype)).reshape(NLC, 128, LC)


def _unpack_b(db):
    t = jnp.einsum("jgchp,gh->jgcp", db.reshape(NLC, 8, GCH, 8, NSTATE), jnp.eye(8, dtype=db.dtype))
    return t.transpose(0, 1, 3, 2).reshape(GROUPS, NSTATE, GCH)


def _pack_c(cc):
    t = cc.reshape(NLC, 8, GCH, NSTATE).transpose(0, 1, 3, 2)
    return jnp.einsum("jgpc,gh->jgphc", t, jnp.eye(8, dtype=cc.dtype)).reshape(NLC, LC, 128)


def _unpack_c(dc):
    t = jnp.einsum("jgphc,gh->jgpc", dc.reshape(NLC, 8, NSTATE, 8, GCH), jnp.eye(8, dtype=dc.dtype))
    return t.transpose(0, 1, 3, 2).reshape(GROUPS, GCH, NSTATE)


SMALL = ("norm_mix", "q_norm", "k_norm", "attn_sinks", "lam_re", "lam_im", "log_dt", "ssm_b_re", "ssm_b_im",
         "ssm_c_re", "ssm_c_im", "ssm_d", "attn_branch_norm", "ssm_branch_norm", "norm_ffn")


def _pack_small(arrs, rows=None):
    flat = jnp.concatenate([a.reshape(-1).astype(F32) for a in arrs])
    n = flat.shape[0]
    if rows is None:
        rows = -(-n // (128 * 256)) * 256
    return jnp.pad(flat, (0, rows * 128 - n)).reshape(rows, 128)


def _unpack_small(pack, like):
    flat = pack.reshape(-1)
    out, off = [], 0
    for a in like:
        out.append(flat[off:off + a.size].reshape(a.shape))
        off += a.size
    return out


def _tie(a, token):
    return a if token is None else a + token[0, 0]


class _NoExchange:
    def __init__(self, w):
        self.w = w
        self.grads = {}

    def begin(self, after):
        return None

    def rest_weights(self, after):
        return self.w

    def start(self, group, idxs, grads):
        self.grads.update(zip(idxs, grads))
        return None

    def finish(self, group, after):
        pass


class _Exchange:
    def __init__(self, rest_shards, place):
        self.rest_shards = rest_shards
        self.place = place
        self.rest_dims = [s[1:] for s in SHARDED[1:]]
        self.pending = {}
        self.halves = {}
        self.parts = {}

    def begin(self, after):
        copies, arrivals = _gather_copies(self.rest_dims)
        lands = [lax.empty((r, c), BF) for r, c, _ in self.rest_dims]
        send, recv, srcs, lands, token = _split_copy_start(
            "gather_start", copies, 4 * len(lands), self.rest_shards, lands, after)
        self.pending["gather"] = (copies, arrivals, send, recv, srcs, lands)
        return token

    def rest_weights(self, after):
        copies, arrivals, send, recv, srcs, lands = self.pending.pop("gather")
        _, lands = _split_copy_wait("gather_wait", copies, arrivals, send, recv, srcs, lands, after)
        fulls = gather_forward(lands, self.rest_dims)
        return dict(zip([s[0] for s in SHARDED[1:]], fulls))

    def _halves(self, group, idxs, grads):
        got = pair_exchange("pair_exchange_" + group, grads, idxs)
        return [pair_sum("pair_sum_" + SHARDED[i][0], self.place, g, b, SHARDED[i][3])
                for i, g, b in zip(idxs, grads, got)]

    def start(self, group, idxs, grads):
        halves = self._halves(group, idxs, grads)
        if group == "in":
            for i, h, pt in zip(idxs, halves, scatter_grads(halves, idxs)):
                self.halves[i], self.parts[i] = h, pt
            return None
        copies, arrivals = _scatter_copies(idxs)
        lands = [lax.empty((3,) + sh, BF) for sh in _piece_shapes(idxs)]
        send, recv, srcs, lands, token = _split_copy_start(
            "scatter_start_" + group, copies, 3 * len(idxs), halves, lands, halves[0])
        self.pending[group] = (copies, arrivals, send, recv, srcs, lands, idxs)
        return token

    def finish(self, group, after):
        if group not in self.pending:
            return
        copies, arrivals, send, recv, srcs, lands, idxs = self.pending.pop(group)
        srcs, lands = _split_copy_wait("scatter_wait_" + group, copies, arrivals, send, recv, srcs, lands, after)
        for i, h, pt in zip(idxs, srcs, lands):
            self.halves[i], self.parts[i] = h, pt


def local_step(x, tgt, meta_full, p, w_in, ex):
    mpad = jnp.concatenate([jnp.zeros((META0, D), F32), meta_full], axis=0)
    qn_t = jnp.tile(p["q_norm"].reshape(1, HEAD), (1, NQ)) * HEAD ** -0.5
    kn_t = jnp.tile(p["k_norm"].reshape(1, HEAD), (1, NKV))
    seg = jnp.arange(256) // HEAD
    bd = (seg[:, None] == seg[None, :]).astype(BF)
    nm = p["norm_mix"].reshape(1, D)
    sinks = p["attn_sinks"].reshape(NQ)
    dsk = p["ssm_d"].reshape(1, D)
    abn, sbn, gf = (p[k].reshape(1, D) for k in ("attn_branch_norm", "ssm_branch_norm", "norm_ffn"))

    disc_in = (p["lam_re"], p["lam_im"], p["log_dt"], p["ssm_b_re"], p["ssm_b_im"])
    (ar, ai, bbr, bbi), disc_vjp = jax.vjp(_ssm_discretize, *disc_in)
    pw = _scan_powers(ar, ai)
    brp, bip = _pack_b(bbr).astype(BF), _pack_b(bbi).astype(BF)
    crp, cip = _pack_c(p["ssm_c_re"]).astype(BF), _pack_c(p["ssm_c_im"]).astype(BF)

    token = ex.begin(w_in)
    xn, qr, kr, qn, kn, v, u, gg = in_proj_fwd(x, mpad, w_in, _tie(nm, token), qn_t, kn_t, bd)
    attn = attention_fwd(sinks, qn, kn, v)
    perm = _segment_order()
    y, sr, si = ssm_fwd(u, pw, brp, bip, crp, cip, dsk, perm)
    w = ex.rest_weights(y)
    zb, zzb, mgb, h1 = mid_fwd(y, attn, gg, x, mpad, w["w_glu"], w["w_out"], abn, sbn)
    hnb, dgub, actb, dh2b, dh1, dh1b, dgf, loss = ffn_fwd_bwd(h1, tgt, w["w_ffn_in"], w["w_ffn_out"], gf)
    token = ex.start("ffn", (3, 4), [weight_grad("grad_w_ffn_in", hnb, dgub, 1408),
                                     weight_grad("grad_w_ffn_out", actb, dh2b, 512)])
    dy, dattn, dggb, dzzb, dabn, dsbn = mid_bwd(dh1b, y, attn, gg, zzb, w["w_glu"], w["w_out"], _tie(abn, token),
                                                sbn)
    grads_mid = [weight_grad("grad_w_glu", zb, dzzb, 1024), weight_grad("grad_w_out", mgb, dh1b, 1024)]
    dub, da, dbr, dbi, dcr, dci, dd = ssm_bwd(dy, u, sr, si, pw, brp, bip, crp, cip, dsk, perm)
    ex.finish("ffn", dub)
    token = ex.start("mid", (1, 2), grads_mid)
    dqn, dkn, dv, dsink = attention_bwd(_tie(sinks, token), qn, kn, v, dattn)
    dproj, dx, dmpad, dnm, dqw, dkw = in_proj_bwd(dqn, dkn, dv, dub, dggb, qr, kr, x, mpad, dh1, w_in, nm, qn_t,
                                                  kn_t, bd)
    ex.finish("mid", dproj)
    ex.start("in", (0,), [weight_grad("grad_w_in", xn, dproj, 1152)])

    dar = jnp.sum(da[0:8], axis=0).reshape(GROUPS, NSTATE)
    dai = jnp.sum(da[8:16], axis=0).reshape(GROUPS, NSTATE)
    dlr, dli, dldt, dbre, dbim = disc_vjp((dar, dai, _unpack_b(dbr), _unpack_b(dbi)))
    small = {
        "norm_mix": dnm, "q_norm": dqw.reshape(NQ, HEAD).sum(0) * HEAD ** -0.5, "k_norm": dkw.reshape(NKV, HEAD).sum(0),
        "attn_sinks": dsink[0, 0:NQ], "lam_re": dlr, "lam_im": dli, "log_dt": dldt,
        "ssm_b_re": dbre, "ssm_b_im": dbim, "ssm_c_re": _unpack_c(jnp.swapaxes(dcr, 1, 2)), "ssm_c_im": _unpack_c(jnp.swapaxes(dci, 1, 2)),
        "ssm_d": dd, "attn_branch_norm": dabn, "ssm_branch_norm": dsbn, "norm_ffn": dgf,
    }
    return loss[0, 0], dx, dmpad[META0:PAD], small


def kernel(x, meta_tokens, norm_mix, w_in, q_norm, k_norm, attn_sinks, lam_re, lam_im, log_dt, ssm_b_re, ssm_b_im, ssm_c_re, ssm_c_im, ssm_d, w_glu, attn_branch_norm, ssm_branch_norm, w_out, norm_ffn, w_ffn_in, w_ffn_out, loss_target, m_meta_tokens, m_norm_mix, m_w_in, m_q_norm, m_k_norm, m_attn_sinks, m_lam_re, m_lam_im, m_log_dt, m_ssm_b_re, m_ssm_b_im, m_ssm_c_re, m_ssm_c_im, m_ssm_d, m_w_glu, m_attn_branch_norm, m_ssm_branch_norm, m_w_out, m_norm_ffn, m_w_ffn_in, m_w_ffn_out, v_meta_tokens, v_norm_mix, v_w_in, v_q_norm, v_k_norm, v_attn_sinks, v_lam_re, v_lam_im, v_log_dt, v_ssm_b_re, v_ssm_b_im, v_ssm_c_re, v_ssm_c_im, v_ssm_d, v_w_glu, v_attn_branch_norm, v_ssm_branch_norm, v_w_out, v_norm_ffn, v_w_ffn_in, v_w_ffn_out):
    args = dict(locals())
    names = ["meta_tokens", "norm_mix", "w_in", "q_norm", "k_norm", "attn_sinks", "lam_re", "lam_im", "log_dt",
             "ssm_b_re", "ssm_b_im", "ssm_c_re", "ssm_c_im", "ssm_d", "w_glu", "attn_branch_norm",
             "ssm_branch_norm", "w_out", "norm_ffn", "w_ffn_in", "w_ffn_out"]
    big_names = [s[0] for s in SHARDED]
    xs, ys, cs = _place()
    chip = 2 * xs + ys

    shards = [args[n][0].astype(BF) for n in big_names]
    w_in_full, meta_full = gather_weights([shards[0], meta_tokens], [SHARDED[0][1:], (N_META, D, 1)])
    place = jnp.stack([chip, cs]).astype(jnp.int32)
    ex = _Exchange(shards[1:], place)

    p = {n: args[n][0] for n in SMALL}
    loss, dx, dmeta, gsmall = local_step(x[0], loss_target[0], meta_full, p, w_in_full, ex)
    loss = lax.psum(loss, ("x", "y", "c"))

    out = {}

    def finish(idxs, name):
        mine = [sum4("sum4_" + SHARDED[i][0], place, ex.parts[i], ex.halves[i], SHARDED[i][3]) for i in idxs]
        for i, g in zip(idxs, join_halves(name, mine, idxs)):
            n = SHARDED[i][0]
            d, nm_, nv_ = adamw("adamw_" + n, args[n][0], g, args["m_" + n][0], args["v_" + n][0])
            out[n] = tuple(t[None] for t in (g, d, nm_, nv_))

    finish((0, 1, 2, 3, 4), "join_halves")
    small_list = [gsmall[n].reshape(args[n].shape) for n in SMALL] + [dmeta]
    red = allreduce_small(_pack_small(small_list))
    small_red = _unpack_small(red, small_list)
    gmeta = lax.dynamic_slice_in_dim(small_red[-1], chip * (D // 4), D // 4, axis=1)
    d, nm_, nv_ = adamw("adamw_meta", meta_tokens, gmeta, m_meta_tokens, v_meta_tokens)
    out["meta_tokens"] = (gmeta, d, nm_, nv_)
    ds, nms, nvs = adamw_small([args[n] for n in SMALL], small_red[:-1], [args["m_" + n] for n in SMALL],
                               [args["v_" + n] for n in SMALL])
    for n, g_, d_, m_, v_ in zip(SMALL, small_red[:-1], ds, nms, nvs):
        out[n] = (g_, d_, m_, v_)

    res = [loss, dx[None]]
    for k in range(4):
        res += [out[n][k] for n in names]
    return tuple(res)
```

```python
import functools
import math

import jax
import jax.numpy as jnp
from jax import lax
from jax.experimental import pallas as pl
from jax.experimental.pallas import tpu as pltpu

F32 = jnp.float32
BF = jnp.bfloat16

D = 1024
N_META = 16
HEAD = 64
NQ = 16
NKV = 4
QW = NQ * HEAD
KVW = NKV * HEAD
WIN = 128
GROUPS = 64
GCH = 16
NSTATE = 64
SW = GROUPS * NSTATE
DFF = 2816
IN_COLS = QW + 2 * KVW + D + 2 * D
PAD = 256
META0 = PAD - N_META
EPS = 1e-6
NEG = -1e30

TM = 256
TS = 256
LC = 512
NLC = SW // LC

LR, B1, B2, AEPS, WD, STEP = 0.001, 0.9, 0.999, 1e-08, 0.01, 10
BC1 = 1.0 - B1 ** STEP
BC2 = 1.0 - B2 ** STEP

MESH = pl.DeviceIdType.MESH
ANY = pl.BlockSpec(memory_space=pl.ANY)


def _cp(sem=None, vmem_mb=48):
    return pltpu.CompilerParams(dimension_semantics=sem, vmem_limit_bytes=vmem_mb << 20)


def _full(shape):
    n = len(shape)
    return pl.BlockSpec(shape, lambda *a: (0,) * n)


def _const(shape):
    n = len(shape)
    return pl.BlockSpec(shape, lambda *a: (0,) * n, pipeline_mode=pl.Buffered(1))


def _rows(tm, width):
    return pl.BlockSpec((tm, width), lambda i: (i, 0))


def _dot(a, b):
    return jnp.dot(a, b, preferred_element_type=F32)


def _dot_nt(a, b):
    return lax.dot_general(a, b, (((1,), (1,)), ((), ())), preferred_element_type=F32)


def _dot_tn(a, b):
    return lax.dot_general(a, b, (((0,), (0,)), ((), ())), preferred_element_type=F32)


def _sigmoid(x):
    return 0.5 * jnp.tanh(0.5 * x) + 0.5


def _seg_mean(x, bd):
    outs = []
    for j in range(x.shape[1] // 256):
        c = x[:, 256 * j:256 * (j + 1)]
        hi = c.astype(BF)
        lo = (c - hi.astype(F32)).astype(BF)
        outs.append(_dot(hi, bd) + _dot(lo, bd))
    out = outs[0] if len(outs) == 1 else jnp.concatenate(outs, axis=1)
    return out * (1.0 / HEAD)


_GC = math.sqrt(2.0 / math.pi)


def _gelu(x):
    t = jnp.tanh(_GC * (x + 0.044715 * x * x * x))
    return 0.5 * x * (1.0 + t), t


def _gelu_grad(x, t):
    return 0.5 * (1.0 + t) + 0.5 * x * (1.0 - t * t) * _GC * (1.0 + 3.0 * 0.044715 * x * x)


def _seq_specs():
    assert TM == PAD
    return [pl.BlockSpec((TM, D), lambda i: (jnp.maximum(i - 1, 0), 0)), _full((PAD, D))]


def _seq_tile(x_ref, m_ref):
    return jnp.where(pl.program_id(0) == 0, m_ref[...], x_ref[...])


def in_proj_fwd(x, mpad, w_in, nm, qn_t, kn_t, bd):
    lp = x.shape[0] + PAD

    def body(x_ref, m_ref, w_ref, nm_ref, qn_ref, kn_ref, bd_ref,
             xn_o, qr_o, kr_o, qn_o, kn_o, v_o, u_o, gg_o):
        h = _seq_tile(x_ref, m_ref)
        r = lax.rsqrt(jnp.mean(h * h, axis=-1, keepdims=True) + EPS)
        xb = ((h * r) * nm_ref[...]).astype(BF)
        xn_o[...] = xb
        bdv = bd_ref[...]
        q = _dot(xb, w_ref[:, 0:QW])
        qr_o[...] = q
        qn_o[...] = (q * lax.rsqrt(_seg_mean(q * q, bdv) + EPS) * qn_ref[...]).astype(BF)
        k = _dot(xb, w_ref[:, QW:QW + KVW])
        kr_o[...] = k
        kn_o[...] = (k * lax.rsqrt(_seg_mean(k * k, bdv) + EPS) * kn_ref[...]).astype(BF)
        v_o[...] = _dot(xb, w_ref[:, QW + KVW:QW + 2 * KVW]).astype(BF)
        u_o[...] = _dot(xb, w_ref[:, QW + 2 * KVW:QW + 2 * KVW + D])
        gg_o[...] = _dot(xb, w_ref[:, QW + 2 * KVW + D:IN_COLS])

    outs = [(D, BF), (QW, F32), (KVW, F32), (QW, BF), (KVW, BF), (KVW, BF), (D, F32), (2 * D, F32)]
    return pl.pallas_call(
        body, name="in_proj_fwd", grid=(lp // TM,),
        in_specs=_seq_specs() + [_full((D, IN_COLS)), _full((1, D)), _full((1, QW)), _full((1, KVW)),
                                 _full((256, 256))],
        out_specs=[_rows(TM, w) for w, _ in outs],
        out_shape=[jax.ShapeDtypeStruct((lp, w), t) for w, t in outs],
        compiler_params=_cp(("arbitrary",)),
    )(x, mpad, w_in, nm, qn_t, kn_t, bd)


def in_proj_bwd(dqn, dkn, dv, du, dgg, qr, kr, x, mpad, dh1, w_in, nm, qn_t, kn_t, bd):
    lp = x.shape[0] + PAD

    def body(dqn_ref, dkn_ref, dv_ref, du_ref, dgg_ref, qr_ref, kr_ref, x_ref, m_ref, dh1_ref,
             w_ref, nm_ref, qn_ref, kn_ref, bd_ref,
             dproj_o, dx_o, dm_o, dnm_o, dqw_o, dkw_o):
        i = pl.program_id(0)

        @pl.when(i == 0)
        def _():
            dnm_o[...] = jnp.zeros_like(dnm_o)
            dqw_o[...] = jnp.zeros_like(dqw_o)
            dkw_o[...] = jnp.zeros_like(dkw_o)

        bdv = bd_ref[...]

        def norm_bwd(x, dy, g, acc_ref):
            rr = lax.rsqrt(_seg_mean(x * x, bdv) + EPS)
            xh = x * rr
            acc_ref[...] += jnp.sum(dy * xh, axis=0, keepdims=True)
            dxh = dy * g
            return rr * (dxh - xh * _seg_mean(dxh * xh, bdv))

        dq = norm_bwd(qr_ref[...], dqn_ref[...], qn_ref[...], dqw_o)
        dk = norm_bwd(kr_ref[...], dkn_ref[...], kn_ref[...], dkw_o)
        dproj_o[:, 0:QW] = dq.astype(BF)
        dproj_o[:, QW:QW + KVW] = dk.astype(BF)
        dproj_o[:, QW + KVW:QW + 2 * KVW] = dv_ref[...].astype(BF)
        dproj_o[:, QW + 2 * KVW:QW + 2 * KVW + D] = du_ref[...]
        dproj_o[:, QW + 2 * KVW + D:IN_COLS] = dgg_ref[...]
        dxn = _dot_nt(dproj_o[...], w_ref[...])
        h = _seq_tile(x_ref, m_ref)
        r = lax.rsqrt(jnp.mean(h * h, axis=-1, keepdims=True) + EPS)
        xh = h * r
        dnm_o[...] += jnp.sum(dxn * xh, axis=0, keepdims=True)
        dxh = dxn * nm_ref[...]
        dh0 = dh1_ref[...] + r * (dxh - xh * jnp.mean(dxh * xh, axis=-1, keepdims=True))
        dx_o[...] = dh0

        @pl.when(i == 0)
        def _():
            dm_o[...] = dh0

    return pl.pallas_call(
        body, name="in_proj_bwd", grid=(lp // TM,),
        in_specs=[_rows(TM, QW), _rows(TM, KVW), _rows(TM, KVW), _rows(TM, D), _rows(TM, 2 * D),
                  _rows(TM, QW), _rows(TM, KVW)] + _seq_specs() + [_rows(TM, D),
                  _full((D, IN_COLS)), _full((1, D)), _full((1, QW)), _full((1, KVW)), _full((256, 256))],
        out_specs=[_rows(TM, IN_COLS), _seq_specs()[0], _full((PAD, D)), _full((1, D)), _full((1, QW)),
                   _full((1, KVW))],
        out_shape=[jax.ShapeDtypeStruct((lp, IN_COLS), BF), jax.ShapeDtypeStruct((lp - PAD, D), F32),
                   jax.ShapeDtypeStruct((PAD, D), F32),
                   jax.ShapeDtypeStruct((1, D), F32), jax.ShapeDtypeStruct((1, QW), F32),
                   jax.ShapeDtypeStruct((1, KVW), F32)],
        compiler_params=_cp(("arbitrary",)),
    )(dqn, dkn, dv, du, dgg, qr, kr, x, mpad, dh1, w_in, nm, qn_t, kn_t, bd)


def _att_bias(b):
    qi = lax.broadcasted_iota(jnp.int32, (WIN, WIN), 0)
    kj = lax.broadcasted_iota(jnp.int32, (WIN, WIN), 1)
    upper = kj > qi
    valid = (upper & ((b - 1) * WIN + kj >= PAD)) | (jnp.logical_not(upper) & (b * WIN + kj >= META0))
    bias_pc = jnp.where(valid, 0.0, NEG)
    kj1 = lax.broadcasted_iota(jnp.int32, (1, WIN), 1)
    bias_m = jnp.where((kj1 >= WIN - N_META) & (b * WIN >= PAD), 0.0, NEG)
    return bias_m, bias_pc, upper


def _dup_half(ref, kp, pos, lo):
    pair = ref[:, 128 * kp:128 * kp + 128].astype(F32)
    rolled = pltpu.roll(pair, 64, axis=1)
    keep = lo if pos == 0 else jnp.logical_not(lo)
    return jnp.where(keep, pair, rolled).astype(BF)


def _stack_heads(ref, kv, lo):
    parts = []
    for pp in range(2):
        pair = ref[:, 256 * kv + 128 * pp:256 * kv + 128 * pp + 128]
        zero = jnp.zeros_like(pair)
        parts.append(jnp.where(lo, pair, zero))
        parts.append(jnp.where(lo, zero, pair))
    return jnp.concatenate(parts, axis=0)


def _unstack_heads(x4, lo):
    a = jnp.where(lo, x4[0:WIN], x4[WIN:2 * WIN])
    b = jnp.where(lo, x4[2 * WIN:3 * WIN], x4[3 * WIN:4 * WIN])
    return a, b


def _split_pc(x, upper):
    zero = jnp.zeros_like(x)
    return jnp.where(upper, x, zero).astype(BF), jnp.where(upper, zero, x).astype(BF)


def _head_softmax(s_m, s_p, s_c, bias, sink):
    bias_m, bias_pc, upper = bias
    sm = s_m + bias_m
    spc = jnp.where(upper, s_p, s_c) + bias_pc
    m = jnp.maximum(jnp.max(jnp.maximum(sm, spc), axis=-1, keepdims=True), sink)
    em, epc = jnp.exp(sm - m), jnp.exp(spc - m)
    esink = jnp.exp(sink - m)
    inv = 1.0 / (esink + jnp.sum(em + epc, axis=-1, keepdims=True))
    return em, epc, inv, esink


def _kv_specs():
    return [pl.BlockSpec((WIN, KVW), lambda b: (1, 0)),
            pl.BlockSpec((WIN, KVW), lambda b: (jnp.maximum(b - 1, 0), 0)),
            pl.BlockSpec((WIN, KVW), lambda b: (b, 0))]


def attention_fwd(sinks, qn, kn, v):
    lp = qn.shape[0]

    def body(sink_ref, q_ref, km_ref, kp_ref, kc_ref, vm_ref, vp_ref, vc_ref, o_ref):
        b = pl.program_id(0)
        bias = _att_bias(b)
        lo = lax.broadcasted_iota(jnp.int32, (WIN, WIN), 1) < HEAD
        for kv in range(NKV):
            kp, pos = kv // 2, kv % 2
            kds = [_dup_half(r, kp, pos, lo) for r in (km_ref, kp_ref, kc_ref)]
            vds = [_dup_half(r, kp, pos, lo) for r in (vm_ref, vp_ref, vc_ref)]
            q4 = _stack_heads(q_ref, kv, lo)
            ss = [_dot_nt(q4, kd) for kd in kds]
            es, invs = ([], [], []), []
            for h in range(4):
                rs = slice(WIN * h, WIN * h + WIN)
                em, epc, inv, _ = _head_softmax(ss[0][rs], ss[1][rs], ss[2][rs], bias, sink_ref[4 * kv + h])
                e_p, e_c = _split_pc(epc, bias[2])
                for lst, e in zip(es, (em.astype(BF), e_p, e_c)):
                    lst.append(e)
                invs.append(inv)
            e_m, e_p, e_c = (jnp.concatenate(lst, axis=0) for lst in es)
            o4 = (_dot(e_m, vds[0]) + _dot(e_p, vds[1]) + _dot(e_c, vds[2])) * jnp.concatenate(invs, axis=0)
            oa, ob = _unstack_heads(o4, lo)
            o_ref[:, 256 * kv:256 * kv + 128] = oa
            o_ref[:, 256 * kv + 128:256 * kv + 256] = ob

    return pl.pallas_call(
        body, name="attention_fwd", grid=(lp // WIN,),
        in_specs=[pl.BlockSpec(memory_space=pltpu.SMEM), _rows(WIN, QW)] + _kv_specs() + _kv_specs(),
        out_specs=_rows(WIN, QW),
        out_shape=jax.ShapeDtypeStruct((lp, QW), F32),
        compiler_params=_cp(("arbitrary",)),
    )(sinks, qn, kn, kn, kn, v, v, v)


def attention_bwd(sinks, qn, kn, v, do):
    lp = qn.shape[0]
    nb = lp // WIN

    def body(sink_ref, q_ref, km_ref, kp_ref, kc_ref, vm_ref, vp_ref, vc_ref, do_ref,
             dq_ref, dk_hbm, dv_hbm, dsink_ref, dk_acc, dv_acc):
        b = pl.program_id(0)

        @pl.when(b == 0)
        def _():
            dk_acc[...] = jnp.zeros_like(dk_acc)
            dv_acc[...] = jnp.zeros_like(dv_acc)
            dsink_ref[...] = jnp.zeros_like(dsink_ref)

        bias = _att_bias(b)
        upper = bias[2]
        lo = lax.broadcasted_iota(jnp.int32, (WIN, WIN), 1) < HEAD
        lane8 = lax.broadcasted_iota(jnp.int32, (8, 128), 1)
        starts = [WIN, pl.multiple_of(jnp.maximum(b - 1, 0) * WIN, WIN), pl.multiple_of(b * WIN, WIN)]
        for kv in range(NKV):
            kp, pos = kv // 2, kv % 2
            keep = lo if pos == 0 else jnp.logical_not(lo)
            kds = [_dup_half(r, kp, pos, lo) for r in (km_ref, kp_ref, kc_ref)]
            vds = [_dup_half(r, kp, pos, lo) for r in (vm_ref, vp_ref, vc_ref)]
            q4 = _stack_heads(q_ref, kv, lo)
            do4 = _stack_heads(do_ref, kv, lo)
            ss = [_dot_nt(q4, kd) for kd in kds]
            dps = [_dot_nt(do4, vd) for vd in vds]
            ds_l, p_l = ([], [], []), ([], [], [])
            for h in range(4):
                rs = slice(WIN * h, WIN * h + WIN)
                em, epc, inv, esink = _head_softmax(ss[0][rs], ss[1][rs], ss[2][rs], bias, sink_ref[4 * kv + h])
                p_m, p_pc = em * inv, epc * inv
                dp_m = dps[0][rs]
                dp_pc = jnp.where(upper, dps[1][rs], dps[2][rs])
                delta = jnp.sum(p_m * dp_m + p_pc * dp_pc, axis=-1, keepdims=True)
                val = -jnp.sum(esink * inv * delta, axis=0, keepdims=True)
                dsink_ref[...] += jnp.where(lane8 == 4 * kv + h, val, 0.0)
                for lst, t in zip(ds_l, ((p_m * (dp_m - delta)).astype(BF),) + _split_pc(p_pc * (dp_pc - delta), upper)):
                    lst.append(t)
                for lst, t in zip(p_l, (p_m.astype(BF),) + _split_pc(p_pc, upper)):
                    lst.append(t)
            dss = [jnp.concatenate(lst, axis=0) for lst in ds_l]
            ps = [jnp.concatenate(lst, axis=0) for lst in p_l]
            dq4 = _dot(dss[0], kds[0]) + _dot(dss[1], kds[1]) + _dot(dss[2], kds[2])
            dqa, dqb = _unstack_heads(dq4, lo)
            dq_ref[:, 256 * kv:256 * kv + 128] = dqa
            dq_ref[:, 256 * kv + 128:256 * kv + 256] = dqb
            for x in range(3):
                dkd = _dot_tn(dss[x], q4)
                dvd = _dot_tn(ps[x], do4)
                dkd = jnp.where(keep, dkd + pltpu.roll(dkd, 64, axis=1), 0.0)
                dvd = jnp.where(keep, dvd + pltpu.roll(dvd, 64, axis=1), 0.0)
                dk_acc[pl.ds(starts[x], WIN), 128 * kp:128 * kp + 128] += dkd
                dv_acc[pl.ds(starts[x], WIN), 128 * kp:128 * kp + 128] += dvd

        @pl.when(b == nb - 1)
        def _():
            pltpu.sync_copy(dk_acc, dk_hbm)
            pltpu.sync_copy(dv_acc, dv_hbm)

    return pl.pallas_call(
        body, name="attention_bwd", grid=(nb,),
        in_specs=[pl.BlockSpec(memory_space=pltpu.SMEM), _rows(WIN, QW)] + _kv_specs() + _kv_specs()
                 + [_rows(WIN, QW)],
        out_specs=[_rows(WIN, QW), ANY, ANY, _full((8, 128))],
        out_shape=[jax.ShapeDtypeStruct((lp, QW), F32), jax.ShapeDtypeStruct((lp, KVW), F32),
                   jax.ShapeDtypeStruct((lp, KVW), F32), jax.ShapeDtypeStruct((8, 128), F32)],
        scratch_shapes=[pltpu.VMEM((lp, KVW), F32), pltpu.VMEM((lp, KVW), F32)],
        compiler_params=_cp(("arbitrary",)),
    )(sinks, qn, kn, kn, kn, v, v, v, do)


PW_ROWS = 64
SEG = TS // 8


def _segment_order():
    rho = jnp.arange(TS)
    token = SEG * (rho % 8) + rho // 8
    p = (token[:, None] == jnp.arange(TS)[None, :]).astype(BF)
    return p, p.T


def _reorder(p, x):
    hi = x.astype(BF)
    r1 = x - hi.astype(F32)
    mid = r1.astype(BF)
    lo = (r1 - mid.astype(F32)).astype(BF)
    return _dot(p, hi) + _dot(p, mid) + _dot(p, lo)


def _segment_scan(xr_ref, xi_ref, pw_ref, base, ls, c_r, c_i, reverse, visit=None):
    ar, ai = pw_ref[base:base + 8, ls], pw_ref[base + 8:base + 16, ls]
    order = list(range(SEG - 1, -1, -1)) if reverse else list(range(SEG))
    s_r = s_i = jnp.zeros_like(ar)
    for i in order:
        rows = pl.ds(8 * i, 8)
        s_r, s_i = ar * s_r - ai * s_i + xr_ref[rows, ls], ar * s_i + ai * s_r + xi_ref[rows, ls]
        xr_ref[rows, ls] = s_r
        xi_ref[rows, ls] = s_i
    rid = lax.broadcasted_iota(jnp.int32, s_r.shape, 0)
    if reverse:
        e_r = jnp.where(rid < 7, pltpu.roll(s_r, 7, axis=0), c_r)
        e_i = jnp.where(rid < 7, pltpu.roll(s_i, 7, axis=0), c_i)
    else:
        e_r = jnp.where(rid >= 1, pltpu.roll(s_r, 1, axis=0), c_r)
        e_i = jnp.where(rid >= 1, pltpu.roll(s_i, 1, axis=0), c_i)
    for n, k in enumerate((1, 2, 4)):
        shift = 8 - k if reverse else k
        t_r, t_i = pltpu.roll(e_r, shift, axis=0), pltpu.roll(e_i, shift, axis=0)
        kr = pw_ref[base + 16 + 16 * n:base + 24 + 16 * n, ls]
        ki = pw_ref[base + 24 + 16 * n:base + 32 + 16 * n, ls]
        e_r, e_i = e_r + kr * t_r - ki * t_i, e_i + kr * t_i + ki * t_r
    last = 0 if reverse else 7
    a16r, a16i = pw_ref[base + 16 + last:base + 17 + last, ls], pw_ref[base + 24 + last:base + 25 + last, ls]
    lr, li, fr, fi = (t[last:last + 1] for t in (e_r, e_i, s_r, s_i))
    out = (a16r * lr - a16i * li + fr, a16r * li + a16i * lr + fi)
    d_r, d_i = e_r, e_i
    extra = None
    for i in order:
        rows = pl.ds(8 * i, 8)
        d_r, d_i = ar * d_r - ai * d_i, ar * d_i + ai * d_r
        f_r, f_i = xr_ref[rows, ls] + d_r, xi_ref[rows, ls] + d_i
        xr_ref[rows, ls] = f_r
        xi_ref[rows, ls] = f_i
        if visit is not None:
            extra = visit(i, f_r, f_i, extra)
    return out if visit is None else (out, extra)


def ssm_fwd(u, pw, br, bi, cr, ci, dsk, perm):
    lp = u.shape[0]

    def body(u_ref, pw_ref, br_ref, bi_ref, cr_ref, ci_ref, d_ref, p_ref, pt_ref,
             y_ref, sr_ref, si_ref, car_r, car_i, yp_s):
        i = pl.program_id(0)

        @pl.when(i == 0)
        def _():
            car_r[...] = jnp.zeros_like(car_r)
            car_i[...] = jnp.zeros_like(car_i)

        u = u_ref[...]
        ub = _dot(p_ref[...], u.astype(BF)).astype(BF)
        for j in range(NLC):
            uj = ub[:, 128 * j:128 * j + 128]
            sr_ref[:, LC * j:LC * j + LC] = _dot(uj, br_ref[j])
            si_ref[:, LC * j:LC * j + LC] = _dot(uj, bi_ref[j])
        for j in range(NLC):
            ls = slice(LC * j, LC * j + LC)
            car_r[:, ls], car_i[:, ls] = _segment_scan(sr_ref, si_ref, pw_ref, 0, ls, car_r[:, ls], car_i[:, ls],
                                                       False)
        for j in range(NLC):
            ls = slice(LC * j, LC * j + LC)
            yp_s[:, 128 * j:128 * j + 128] = (_dot(sr_ref[:, ls].astype(BF), cr_ref[j])
                                              - _dot(si_ref[:, ls].astype(BF), ci_ref[j]))
        y_ref[...] = _reorder(pt_ref[...], yp_s[...]) + d_ref[...] * u

    p, pt = perm
    return pl.pallas_call(
        body, name="ssm_fwd", grid=(lp // TS,),
        in_specs=[_rows(TS, D), _full((2 * PW_ROWS, SW)), _full((NLC, 128, LC)), _full((NLC, 128, LC)),
                  _full((NLC, LC, 128)), _full((NLC, LC, 128)), _full((1, D)), _full((TS, TS)), _full((TS, TS))],
        out_specs=[_rows(TS, D), _rows(TS, SW), _rows(TS, SW)],
        out_shape=[jax.ShapeDtypeStruct((lp, D), F32), jax.ShapeDtypeStruct((lp, SW), F32),
                   jax.ShapeDtypeStruct((lp, SW), F32)],
        scratch_shapes=[pltpu.VMEM((1, SW), F32), pltpu.VMEM((1, SW), F32), pltpu.VMEM((TS, D), F32)],
        compiler_params=_cp(("arbitrary",)),
    )(u, pw, br, bi, cr, ci, dsk, p, pt)


def ssm_bwd(dy, u, sr, si, pw, br, bi, cr, ci, dsk, perm):
    lp = u.shape[0]
    nt = lp // TS

    def rev(i):
        return (nt - 1 - i, 0)

    def prev8(i):
        return (jnp.maximum((nt - 1 - i) * (TS // 8) - 1, 0), 0)

    def body(dy_ref, u_ref, sr_ref, si_ref, pr8_ref, pi8_ref, pw_ref, br_ref, bi_ref, cr_ref, ci_ref, d_ref,
             p_ref, pt_ref, du_ref, da_ref, dbr_ref, dbi_ref, dcr_ref, dci_ref, dd_ref,
             gr_s, gi_s, car_r, car_i, dup_s):
        i = pl.program_id(0)

        @pl.when(i == 0)
        def _():
            car_r[...] = jnp.zeros_like(car_r)
            car_i[...] = jnp.zeros_like(car_i)
            for ref in (da_ref, dbr_ref, dbi_ref, dcr_ref, dci_ref, dd_ref):
                ref[...] = jnp.zeros_like(ref)

        keep = 1.0 - (i == nt - 1).astype(F32)
        dy = dy_ref[...]
        u = u_ref[...]
        dd_ref[...] += jnp.sum(dy * u, axis=0, keepdims=True)
        pm = p_ref[...]
        dyb = _dot(pm, dy.astype(BF)).astype(BF)
        ub = _dot(pm, u.astype(BF)).astype(BF)
        for j in range(NLC):
            ls = slice(LC * j, LC * j + LC)
            dyj = dyb[:, 128 * j:128 * j + 128]
            gr_s[:, ls] = _dot_nt(dyj, cr_ref[j])
            gi_s[:, ls] = -_dot_nt(dyj, ci_ref[j])
            dcr_ref[j] += _dot_tn(dyj, sr_ref[:, ls].astype(BF))
            dci_ref[j] -= _dot_tn(dyj, si_ref[:, ls].astype(BF))
        rid = lax.broadcasted_iota(jnp.int32, (8, LC), 0)
        for j in range(NLC):
            ls = slice(LC * j, LC * j + LC)

            def accumulate(i, g_r, g_i, acc):
                if i >= 1:
                    rows = pl.ds(8 * (i - 1), 8)
                    p_r, p_i = sr_ref[rows, ls], si_ref[rows, ls]
                else:
                    rows = pl.ds(8 * (SEG - 1), 8)
                    p_r = jnp.where(rid >= 1, pltpu.roll(sr_ref[rows, ls], 1, axis=0), pr8_ref[7:8, ls] * keep)
                    p_i = jnp.where(rid >= 1, pltpu.roll(si_ref[rows, ls], 1, axis=0), pi8_ref[7:8, ls] * keep)
                t_r, t_i = g_r * p_r + g_i * p_i, g_i * p_r - g_r * p_i
                return (t_r, t_i) if acc is None else (acc[0] + t_r, acc[1] + t_i)

            (car_r[:, ls], car_i[:, ls]), (t_r, t_i) = _segment_scan(
                gr_s, gi_s, pw_ref, PW_ROWS, ls, car_r[:, ls], car_i[:, ls], True, accumulate)
            da_ref[0:8, ls] += t_r
            da_ref[8:16, ls] += t_i
        for j in range(NLC):
            ls = slice(LC * j, LC * j + LC)
            uj = ub[:, 128 * j:128 * j + 128]
            grb = gr_s[:, ls].astype(BF)
            gib = gi_s[:, ls].astype(BF)
            dup_s[:, 128 * j:128 * j + 128] = _dot_nt(grb, br_ref[j]) + _dot_nt(gib, bi_ref[j])
            dbr_ref[j] += _dot_tn(uj, grb)
            dbi_ref[j] += _dot_tn(uj, gib)
        du_ref[...] = (_reorder(pt_ref[...], dup_s[...]) + d_ref[...] * dy).astype(BF)

    p, pt = perm
    return pl.pallas_call(
        body, name="ssm_bwd", grid=(nt,),
        in_specs=[pl.BlockSpec((TS, D), rev), pl.BlockSpec((TS, D), rev), pl.BlockSpec((TS, SW), rev),
                  pl.BlockSpec((TS, SW), rev), pl.BlockSpec((8, SW), prev8), pl.BlockSpec((8, SW), prev8),
                  _full((2 * PW_ROWS, SW)), _full((NLC, 128, LC)), _full((NLC, 128, LC)),
                  _full((NLC, LC, 128)), _full((NLC, LC, 128)), _full((1, D)), _full((TS, TS)), _full((TS, TS))],
        out_specs=[pl.BlockSpec((TS, D), rev), _full((16, SW)), _full((NLC, 128, LC)), _full((NLC, 128, LC)),
                   _full((NLC, 128, LC)), _full((NLC, 128, LC)), _full((1, D))],
        out_shape=[jax.ShapeDtypeStruct((lp, D), BF), jax.ShapeDtypeStruct((16, SW), F32),
                   jax.ShapeDtypeStruct((NLC, 128, LC), F32), jax.ShapeDtypeStruct((NLC, 128, LC), F32),
                   jax.ShapeDtypeStruct((NLC, 128, LC), F32), jax.ShapeDtypeStruct((NLC, 128, LC), F32),
                   jax.ShapeDtypeStruct((1, D), F32)],
        scratch_shapes=[pltpu.VMEM((TS, SW), F32), pltpu.VMEM((TS, SW), F32),
                        pltpu.VMEM((1, SW), F32), pltpu.VMEM((1, SW), F32), pltpu.VMEM((TS, D), F32)],
        compiler_params=_cp(("arbitrary",)),
    )(dy, u, sr, si, sr, si, pw, br, bi, cr, ci, dsk, p, pt)


def _merge_parts(attn, zz, gg, abn, sbn):
    za, zb = zz[:, 0:D], zz[:, D:2 * D]
    sgz = _sigmoid(zb)
    ssm = za * sgz
    ra = lax.rsqrt(jnp.mean(attn * attn, axis=-1, keepdims=True) + EPS)
    rs = lax.rsqrt(jnp.mean(ssm * ssm, axis=-1, keepdims=True) + EPS)
    ah, sh = attn * ra, ssm * rs
    sga, sgs = _sigmoid(gg[:, 0:D]), _sigmoid(gg[:, D:2 * D])
    return za, sgz, ra, rs, ah, sh, sga, sgs, ah * abn, sh * sbn


def mid_fwd(y, attn, gg, x, mpad, w_glu, w_out, abn, sbn):
    lp = y.shape[0]

    def body(y_ref, a_ref, gg_ref, x_ref, m_ref, wg_ref, wo_ref, abn_ref, sbn_ref, z_o, zz_o, mg_o, h1_o):
        z, _ = _gelu(y_ref[...])
        zb = z.astype(BF)
        z_o[...] = zb
        zz = _dot(zb, wg_ref[...])
        zz_o[...] = zz.astype(BF)
        _, _, _, _, _, _, sga, sgs, an, sn = _merge_parts(a_ref[...], zz, gg_ref[...], abn_ref[...], sbn_ref[...])
        mb = (sga * an + sgs * sn).astype(BF)
        mg_o[...] = mb
        h1_o[...] = _seq_tile(x_ref, m_ref) + _dot(mb, wo_ref[...])

    return pl.pallas_call(
        body, name="mid_fwd", grid=(lp // TM,),
        in_specs=[_rows(TM, D), _rows(TM, D), _rows(TM, 2 * D)] + _seq_specs() + [_full((D, 2 * D)), _full((D, D)),
                  _full((1, D)), _full((1, D))],
        out_specs=[_rows(TM, D), _rows(TM, 2 * D), _rows(TM, D), _rows(TM, D)],
        out_shape=[jax.ShapeDtypeStruct((lp, D), BF), jax.ShapeDtypeStruct((lp, 2 * D), BF),
                   jax.ShapeDtypeStruct((lp, D), BF), jax.ShapeDtypeStruct((lp, D), F32)],
        compiler_params=_cp(("arbitrary",)),
    )(y, attn, gg, x, mpad, w_glu, w_out, abn, sbn)


def mid_bwd(dh1b, y, attn, gg, zzb, w_glu, w_out, abn, sbn):
    lp = y.shape[0]

    def body(dh_ref, y_ref, a_ref, gg_ref, zz_ref, wg_ref, wo_ref, abn_ref, sbn_ref,
             dy_o, dattn_o, dgg_o, dzz_o, dabn_o, dsbn_o):
        i = pl.program_id(0)

        @pl.when(i == 0)
        def _():
            dabn_o[...] = jnp.zeros_like(dabn_o)
            dsbn_o[...] = jnp.zeros_like(dsbn_o)

        dm = _dot_nt(dh_ref[...], wo_ref[...])
        abn, sbn = abn_ref[...], sbn_ref[...]
        za, sgz, ra, rs, ah, sh, sga, sgs, an, sn = _merge_parts(
            a_ref[...], zz_ref[...].astype(F32), gg_ref[...], abn, sbn)
        dgg_o[:, 0:D] = (dm * an * sga * (1.0 - sga)).astype(BF)
        dgg_o[:, D:2 * D] = (dm * sn * sgs * (1.0 - sgs)).astype(BF)
        da = dm * sga
        dabn_o[...] += jnp.sum(da * ah, axis=0, keepdims=True)
        dah = da * abn
        dattn_o[...] = (ra * (dah - ah * jnp.mean(dah * ah, axis=-1, keepdims=True))).astype(BF)
        ds = dm * sgs
        dsbn_o[...] += jnp.sum(ds * sh, axis=0, keepdims=True)
        dsh = ds * sbn
        dssm = rs * (dsh - sh * jnp.mean(dsh * sh, axis=-1, keepdims=True))
        dzz_o[:, 0:D] = (dssm * sgz).astype(BF)
        dzz_o[:, D:2 * D] = (dssm * za * sgz * (1.0 - sgz)).astype(BF)
        dz = _dot_nt(dzz_o[...], wg_ref[...])
        yv = y_ref[...]
        _, t = _gelu(yv)
        dy_o[...] = dz * _gelu_grad(yv, t)

    return pl.pallas_call(
        body, name="mid_bwd", grid=(lp // TM,),
        in_specs=[_rows(TM, D), _rows(TM, D), _rows(TM, D), _rows(TM, 2 * D), _rows(TM, 2 * D),
                  _full((D, 2 * D)), _full((D, D)), _full((1, D)), _full((1, D))],
        out_specs=[_rows(TM, D), _rows(TM, D), _rows(TM, 2 * D), _rows(TM, 2 * D), _full((1, D)), _full((1, D))],
        out_shape=[jax.ShapeDtypeStruct((lp, D), F32), jax.ShapeDtypeStruct((lp, D), BF),
                   jax.ShapeDtypeStruct((lp, 2 * D), BF), jax.ShapeDtypeStruct((lp, 2 * D), BF),
                   jax.ShapeDtypeStruct((1, D), F32), jax.ShapeDtypeStruct((1, D), F32)],
        compiler_params=_cp(("arbitrary",)),
    )(dh1b, y, attn, gg, zzb, w_glu, w_out, abn, sbn)


def ffn_fwd_bwd(h1, tgt, w_fi, w_fo, gf):
    lp = h1.shape[0]
    tf = 256
    bounds = (0, 1536, DFF)

    def body(h_ref, t_ref, wi_ref, wo_ref, gf_ref, hn_o, dgu_o, act_o, dh2_o, dh1_o, dh1b_o, dgf_o, loss_o,
             gate_s, up_s):
        i = pl.program_id(0)

        @pl.when(i == 0)
        def _():
            dgf_o[...] = jnp.zeros_like(dgf_o)
            loss_o[...] = jnp.zeros_like(loss_o)

        h = h_ref[...]
        r = lax.rsqrt(jnp.mean(h * h, axis=-1, keepdims=True) + EPS)
        xh = h * r
        hb = (xh * gf_ref[...]).astype(BF)
        hn_o[...] = hb
        h2 = h
        for c0, c1 in zip(bounds[:-1], bounds[1:]):
            gate = _dot(hb, wi_ref[:, c0:c1])
            up = _dot(hb, wi_ref[:, DFF + c0:DFF + c1])
            gate_s[:, c0:c1] = gate
            up_s[:, c0:c1] = up
            ab = (gate * _sigmoid(gate) * up).astype(BF)
            act_o[:, c0:c1] = ab
            h2 = h2 + _dot(ab, wo_ref[c0:c1, :])
        real = (i >= PAD // tf).astype(F32)
        e = (h2 - t_ref[...]) * real
        loss_o[...] += 0.5 * jnp.sum(jnp.sum(e * e, axis=-1, keepdims=True), axis=0, keepdims=True) * (1.0 / D)
        dh2 = e * (1.0 / D)
        db = dh2.astype(BF)
        dh2_o[...] = db
        dhn = jnp.zeros((tf, D), F32)
        for c0, c1 in zip(bounds[:-1], bounds[1:]):
            gate = gate_s[:, c0:c1]
            up = up_s[:, c0:c1]
            sg = _sigmoid(gate)
            dact = _dot_nt(db, wo_ref[c0:c1, :])
            dgate = (dact * up * (sg * (1.0 + gate * (1.0 - sg)))).astype(BF)
            dup = (dact * (gate * sg)).astype(BF)
            dgu_o[:, c0:c1] = dgate
            dgu_o[:, DFF + c0:DFF + c1] = dup
            dhn += _dot_nt(dgate, wi_ref[:, c0:c1]) + _dot_nt(dup, wi_ref[:, DFF + c0:DFF + c1])
        dgf_o[...] += jnp.sum(dhn * xh, axis=0, keepdims=True)
        dxh = dhn * gf_ref[...]
        dh1 = dh2 + r * (dxh - xh * jnp.mean(dxh * xh, axis=-1, keepdims=True))
        dh1_o[...] = dh1
        dh1b_o[...] = dh1.astype(BF)

    return pl.pallas_call(
        body, name="ffn_fwd_bwd", grid=(lp // tf,),
        in_specs=[_rows(tf, D), pl.BlockSpec((tf, D), lambda i: (jnp.maximum(i - PAD // tf, 0), 0)),
                  _const((D, 2 * DFF)), _const((DFF, D)), _full((1, D))],
        out_specs=[_rows(tf, D), _rows(tf, 2 * DFF), _rows(tf, DFF), _rows(tf, D), _rows(tf, D), _rows(tf, D),
                   _full((1, D)), _full((1, 1))],
        out_shape=[jax.ShapeDtypeStruct((lp, D), BF), jax.ShapeDtypeStruct((lp, 2 * DFF), BF),
                   jax.ShapeDtypeStruct((lp, DFF), BF), jax.ShapeDtypeStruct((lp, D), BF),
                   jax.ShapeDtypeStruct((lp, D), F32), jax.ShapeDtypeStruct((lp, D), BF),
                   jax.ShapeDtypeStruct((1, D), F32), jax.ShapeDtypeStruct((1, 1), F32)],
        scratch_shapes=[pltpu.VMEM((tf, DFF), F32), pltpu.VMEM((tf, DFF), F32)],
        compiler_params=_cp(("arbitrary",), vmem_mb=58),
    )(h1, tgt, w_fi, w_fo, gf)


def weight_grad(name, a, b, tn, with_bf16=False):
    lp, k = a.shape
    n = b.shape[1]
    tr = next(t for t in (2112, 1056, 768, 512, 384, 256, 128) if lp % t == 0 and t * k * 2 <= (5 << 20))

    def body(a_ref, b_ref, o_ref, *ob_ref):
        @pl.when(pl.program_id(1) == 0)
        def _():
            o_ref[...] = jnp.zeros_like(o_ref)

        o_ref[...] += _dot_tn(a_ref[...], b_ref[...])
        if with_bf16:
            @pl.when(pl.program_id(1) == lp // tr - 1)
            def _():
                ob_ref[0][...] = o_ref[...].astype(BF)

    spec = pl.BlockSpec((k, tn), lambda j, m: (0, j))
    return pl.pallas_call(
        body, name=name, grid=(n // tn, lp // tr),
        in_specs=[pl.BlockSpec((tr, k), lambda j, m: (m, 0)), pl.BlockSpec((tr, tn), lambda j, m: (m, j))],
        out_specs=[spec, spec] if with_bf16 else spec,
        out_shape=([jax.ShapeDtypeStruct((k, n), F32), jax.ShapeDtypeStruct((k, n), BF)] if with_bf16
                   else jax.ShapeDtypeStruct((k, n), F32)),
        compiler_params=_cp(("arbitrary", "arbitrary")),
    )(a, b)


def _adamw_math(w, g, m, v):
    m = B1 * m + (1.0 - B1) * g
    v = B2 * v + (1.0 - B2) * (g * g)
    delta = -LR * ((m / BC1) / (jnp.sqrt(v / BC2) + AEPS) + WD * w)
    return delta, m, v


def _row_tile(r):
    return next((t for t in (256, 128, 64, 32, 16, 8) if r % t == 0), r)


def adamw(name, w, g, m, v):
    r, c = w.shape
    tr = _row_tile(r)

    def body(w_ref, g_ref, m_ref, v_ref, d_o, m_o, v_o):
        d_o[...], m_o[...], v_o[...] = _adamw_math(w_ref[...], g_ref[...], m_ref[...], v_ref[...])

    spec = pl.BlockSpec((tr, c), lambda i: (i, 0))
    return pl.pallas_call(
        body, name=name, grid=(r // tr,), in_specs=[spec] * 4, out_specs=[spec] * 3,
        out_shape=[jax.ShapeDtypeStruct((r, c), F32)] * 3,
        compiler_params=_cp(("arbitrary",)),
    )(w, g, m, v)


def adamw_small(ws, gs, ms, vs):
    n = len(ws)
    steps = 8

    def spec(a):
        if a.ndim == 4:
            return pl.BlockSpec((1, a.shape[1] // steps) + a.shape[2:], lambda i: (0, i, 0, 0))
        nd = a.ndim
        return pl.BlockSpec(a.shape, lambda i: (0,) * nd)

    def body(*refs):
        w, g, m, v, d_o, m_o, v_o = (refs[k * n:(k + 1) * n] for k in range(7))
        for t in range(n):
            d_o[t][...], m_o[t][...], v_o[t][...] = _adamw_math(w[t][...], g[t][...], m[t][...], v[t][...])

    specs = [spec(a) for a in ws]
    res = pl.pallas_call(
        body, name="adamw_small", grid=(steps,), in_specs=specs * 4, out_specs=specs * 3,
        out_shape=[jax.ShapeDtypeStruct(a.shape, F32) for a in ws] * 3,
        compiler_params=_cp(("arbitrary",)),
    )(*ws, *gs, *ms, *vs)
    return res[:n], res[n:2 * n], res[2 * n:]


def pair_sum(name, place, full, got, ax):
    r, c = got.shape
    tr = _row_tile(r)

    def body(s_ref, a_ref, b_ref, o_ref):
        o_ref[...] = (a_ref[...] + b_ref[...]).astype(BF)

    if ax == 1:
        mine = pl.BlockSpec((tr, c), lambda i, s: (s[1] * (r // tr) + i, 0))
    else:
        mine = pl.BlockSpec((tr, c), lambda i, s: (i, s[1]))
    spec = pl.BlockSpec((tr, c), lambda i, s: (i, 0))
    return pl.pallas_call(
        body, name=name,
        grid_spec=pltpu.PrefetchScalarGridSpec(num_scalar_prefetch=1, grid=(r // tr,), in_specs=[mine, spec],
                                               out_specs=spec),
        out_shape=jax.ShapeDtypeStruct((r, c), BF),
        compiler_params=_cp(("arbitrary",)),
    )(place, full, got)


def sum4(name, place, parts, half, ax):
    _, r, c = parts.shape
    tr = _row_tile(r)

    def body(s_ref, p_ref, h_ref, o_ref):
        o_ref[...] = ((p_ref[0].astype(F32) + p_ref[1].astype(F32))
                      + (p_ref[2].astype(F32) + h_ref[...].astype(F32)))

    if ax == 1:
        own = pl.BlockSpec((tr, c), lambda i, s: (i, s[0]))
        out = pl.BlockSpec((tr, c), lambda i, s: (s[1] * (r // tr) + i, 0))
        shape = (2 * r, c)
    else:
        own = pl.BlockSpec((tr, c), lambda i, s: (s[0] * (r // tr) + i, 0))
        out = pl.BlockSpec((tr, c), lambda i, s: (i, s[1]))
        shape = (r, 2 * c)
    return pl.pallas_call(
        body, name=name,
        grid_spec=pltpu.PrefetchScalarGridSpec(
            num_scalar_prefetch=1, grid=(r // tr,),
            in_specs=[pl.BlockSpec((3, tr, c), lambda i, s: (0, i, 0)), own], out_specs=out),
        out_shape=jax.ShapeDtypeStruct(shape, F32),
        compiler_params=_cp(("arbitrary",)),
    )(place, parts, half)


def sum8(name, place, parts, full, ax):
    _, r, c = parts.shape
    tr = _row_tile(r)

    def body(s_ref, p_ref, g_ref, o_ref):
        p = [p_ref[k].astype(F32) for k in range(7)]
        o_ref[...] = ((p[0] + p[1]) + (p[2] + p[3])) + ((p[4] + p[5]) + (p[6] + g_ref[...]))

    if ax == 1:
        own = pl.BlockSpec((tr, c), lambda i, s: (s[1] * (r // tr) + i, s[0]))
        out = pl.BlockSpec((tr, c), lambda i, s: (s[1] * (r // tr) + i, 0))
        shape = (2 * r, c)
    else:
        own = pl.BlockSpec((tr, c), lambda i, s: (s[0] * (r // tr) + i, s[1]))
        out = pl.BlockSpec((tr, c), lambda i, s: (i, s[1]))
        shape = (r, 2 * c)
    return pl.pallas_call(
        body, name=name,
        grid_spec=pltpu.PrefetchScalarGridSpec(
            num_scalar_prefetch=1, grid=(r // tr,),
            in_specs=[pl.BlockSpec((7, tr, c), lambda i, s: (0, i, 0)), own], out_specs=out),
        out_shape=jax.ShapeDtypeStruct(shape, F32),
        compiler_params=_cp(("arbitrary",)),
    )(place, parts, full)


def _place():
    x, y, c = lax.axis_index("x"), lax.axis_index("y"), lax.axis_index("c")
    return x, y, c


def _other_chips(x, y):
    return [(1 - x, y), (x, 1 - y), (1 - x, 1 - y)]


SHARDED = (("w_in", D, IN_COLS, 1), ("w_glu", D, 2 * D, 1), ("w_out", D, D, 0),
           ("w_ffn_in", D, 2 * DFF, 1), ("w_ffn_out", DFF, D, 0))


def _shard_of(ref, axis, k, size):
    if axis == 0:
        return ref.at[pl.ds(k * size, size), :]
    return ref.at[:, pl.ds(k * size, size)]


def _gather_piece(full, dims, k, h):
    r, cc, ax = dims
    if ax == 0:
        return full.at[pl.ds(k * (r // 4) + h * (r // 8), r // 8), :]
    return full.at[pl.ds(h * (r // 2), r // 2), pl.ds(k * (cc // 4), cc // 4)]


def _rows_half(shard, h):
    rs = shard.shape[0]
    return shard.at[pl.ds(h * (rs // 2), rs // 2), :]


def gather_weights(shards, dims):
    items = [(s,) + tuple(d) for s, d in zip(shards, dims)]
    n = len(items)

    def body(*refs):
        ins, outs = refs[:n], refs[n:2 * n]
        ici_send, ici_recv, d2d_send, d2d_recv, local_sems = refs[2 * n:]
        x, y, c = _place()
        mine = 2 * x + y
        chips = _other_chips(x, y)

        def piece(t, k, h):
            return _gather_piece(outs[t], dims[t], k, h)

        def src_half(t, h):
            return _rows_half(ins[t], h)

        local, first, passed = [], [], []
        for t in range(n):
            _, r, cc, ax = items[t]
            size = (r if ax == 0 else cc) // 4
            loc = pltpu.make_async_remote_copy(
                src_ref=ins[t], dst_ref=_shard_of(outs[t], ax, mine, size), send_sem=local_sems.at[t],
                recv_sem=local_sems.at[n + t], device_id=(x, y, 1 - c), device_id_type=MESH)
            loc.start()
            local.append(loc)
            for j, chip in enumerate(chips):
                cp = pltpu.make_async_remote_copy(
                    src_ref=src_half(t, c), dst_ref=piece(t, mine, c),
                    send_sem=ici_send.at[3 * t + j], recv_sem=ici_recv.at[3 * t + j],
                    device_id=(*chip, c), device_id_type=MESH)
                cp.start()
                first.append(cp)
        for t in range(n):
            for j, (px, py) in enumerate(chips):
                got = piece(t, 2 * px + py, c)
                pltpu.make_async_remote_copy(
                    src_ref=src_half(t, c), dst_ref=got, send_sem=ici_send.at[3 * t + j],
                    recv_sem=ici_recv.at[3 * t + j], device_id=(px, py, c), device_id_type=MESH).wait_recv()
                cp = pltpu.make_async_remote_copy(
                    src_ref=got, dst_ref=got, send_sem=d2d_send.at[3 * t + j], recv_sem=d2d_recv.at[3 * t + j],
                    device_id=(x, y, 1 - c), device_id_type=MESH)
                cp.start()
                passed.append(cp)
        for t in range(n):
            for j, (px, py) in enumerate(chips):
                other = piece(t, 2 * px + py, 1 - c)
                pltpu.make_async_remote_copy(
                    src_ref=other, dst_ref=other, send_sem=d2d_send.at[3 * t + j], recv_sem=d2d_recv.at[3 * t + j],
                    device_id=(x, y, 1 - c), device_id_type=MESH).wait_recv()
        for cp in first + passed:
            cp.wait_send()
        for loc in local:
            loc.wait()

    return pl.pallas_call(
        body, name="gather_weights",
        in_specs=[ANY] * n, out_specs=[ANY] * n,
        out_shape=[jax.ShapeDtypeStruct((r, c), s.dtype) for s, r, c, _ in items],
        scratch_shapes=[pltpu.SemaphoreType.DMA((3 * n,)), pltpu.SemaphoreType.DMA((3 * n,)),
                        pltpu.SemaphoreType.DMA((3 * n,)), pltpu.SemaphoreType.DMA((3 * n,)),
                        pltpu.SemaphoreType.DMA((2 * n,))],
        compiler_params=pltpu.CompilerParams(has_side_effects=True),
    )(*shards)


SEM = pl.BlockSpec(memory_space=pltpu.SEMAPHORE)
HBM = pl.BlockSpec(memory_space=pltpu.HBM)
EFFECT = pltpu.SideEffectType.DATAFLOW_SIDE_EFFECTING


def _in_hbm(a):
    return pltpu.with_memory_space_constraint(a, pltpu.HBM)


def _split_copy_start(name, copies, ncopies, srcs, lands, after):
    ns, nl = len(srcs), len(lands)

    def body(*refs):
        ins, land_refs = refs[:ns], refs[ns:ns + nl]
        send_sems, recv_sems = refs[ns + nl + 1], refs[ns + nl + 2]
        token = refs[-1]
        for k, (src, dst, dev) in enumerate(copies(ins, land_refs)):
            pltpu.make_async_remote_copy(src_ref=src, dst_ref=dst, send_sem=send_sems.at[k],
                                         recv_sem=recv_sems.at[k], device_id=dev, device_id_type=MESH).start()
        token[...] = jnp.zeros_like(token)

    res = pl.pallas_call(
        body, name=name,
        in_specs=[HBM] * (ns + nl) + [ANY],
        out_specs=(SEM, SEM) + (HBM,) * (ns + nl) + (pl.BlockSpec(memory_space=pltpu.VMEM),),
        out_shape=(pltpu.SemaphoreType.DMA((ncopies,)), pltpu.SemaphoreType.DMA((ncopies,)))
        + tuple(pltpu.HBM(a.shape, a.dtype) for a in list(srcs) + list(lands))
        + (jax.ShapeDtypeStruct((8, 128), F32),),
        input_output_aliases={t: 2 + t for t in range(ns + nl)},
        compiler_params=pltpu.CompilerParams(has_side_effects=EFFECT),
    )(*[_in_hbm(a) for a in srcs], *[_in_hbm(a) for a in lands], after)
    return res[0], res[1], list(res[2:2 + ns]), list(res[2 + ns:2 + ns + nl]), res[-1]


def _split_copy_wait(name, copies, arrivals, send_sems, recv_sems, srcs, lands, after):
    ns, nl = len(srcs), len(lands)

    def body(*refs):
        ins, land_refs = refs[:ns], refs[ns:ns + nl]
        send_sems_ref, recv_sems_ref = refs[ns + nl], refs[ns + nl + 1]
        mine = copies(ins, land_refs)
        for k, ((src, dst, dev), got) in enumerate(zip(mine, arrivals(ins, land_refs))):
            pltpu.make_async_remote_copy(src_ref=src, dst_ref=dst, send_sem=send_sems_ref.at[k],
                                         recv_sem=recv_sems_ref.at[k], device_id=dev,
                                         device_id_type=MESH).wait_send()
            pltpu.make_async_remote_copy(src_ref=got, dst_ref=got, send_sem=send_sems_ref.at[k],
                                         recv_sem=recv_sems_ref.at[k], device_id=dev,
                                         device_id_type=MESH).wait_recv()

    res = pl.pallas_call(
        body, name=name,
        in_specs=[HBM] * (ns + nl) + [SEM, SEM, ANY],
        out_specs=(HBM,) * (ns + nl),
        out_shape=tuple(pltpu.HBM(a.shape, a.dtype) for a in list(srcs) + list(lands)),
        input_output_aliases={t: t for t in range(ns + nl)},
        compiler_params=pltpu.CompilerParams(has_side_effects=EFFECT),
    )(*srcs, *lands, send_sems, recv_sems, after)
    return list(res[:ns]), list(res[ns:])


def _gather_copies(dims):
    def copies(ins, lands):
        x, y, c = _place()
        mine = 2 * x + y
        out = []
        for t, d in enumerate(dims):
            size = (d[0] if d[2] == 0 else d[1]) // 4
            for chip in _other_chips(x, y):
                out.append((_rows_half(ins[t], c), _gather_piece(lands[t], d, mine, c), (*chip, c)))
            out.append((ins[t], _shard_of(lands[t], d[2], mine, size), (x, y, 1 - c)))
        return out

    def arrivals(ins, lands):
        x, y, c = _place()
        mine = 2 * x + y
        out = []
        for t, d in enumerate(dims):
            size = (d[0] if d[2] == 0 else d[1]) // 4
            for px, py in _other_chips(x, y):
                out.append(_gather_piece(lands[t], d, 2 * px + py, c))
            out.append(_shard_of(lands[t], d[2], mine, size))
        return out

    return copies, arrivals


def gather_forward(fulls, dims):
    n = len(fulls)

    def body(*refs):
        ins, outs = refs[:n], refs[n:2 * n]
        send_sems, recv_sems = refs[2 * n:]
        x, y, c = _place()
        cps = []
        for t in range(n):
            for j, (px, py) in enumerate(_other_chips(x, y)):
                cp = pltpu.make_async_remote_copy(
                    src_ref=_gather_piece(ins[t], dims[t], 2 * px + py, c),
                    dst_ref=_gather_piece(outs[t], dims[t], 2 * px + py, c),
                    send_sem=send_sems.at[3 * t + j], recv_sem=recv_sems.at[3 * t + j],
                    device_id=(x, y, 1 - c), device_id_type=MESH)
                cp.start()
                cps.append(cp)
        for t in range(n):
            for j, (px, py) in enumerate(_other_chips(x, y)):
                cps[3 * t + j].wait_send()
                other = _gather_piece(outs[t], dims[t], 2 * px + py, 1 - c)
                pltpu.make_async_remote_copy(
                    src_ref=other, dst_ref=other, send_sem=send_sems.at[3 * t + j], recv_sem=recv_sems.at[3 * t + j],
                    device_id=(x, y, 1 - c), device_id_type=MESH).wait_recv()

    return pl.pallas_call(
        body, name="gather_forward",
        in_specs=[ANY] * n, out_specs=[ANY] * n,
        out_shape=[jax.ShapeDtypeStruct(a.shape, a.dtype) for a in fulls],
        input_output_aliases={t: t for t in range(n)},
        scratch_shapes=[pltpu.SemaphoreType.DMA((3 * n,)), pltpu.SemaphoreType.DMA((3 * n,))],
        compiler_params=pltpu.CompilerParams(has_side_effects=True),
    )(*fulls)


def _half_of(ref, ax, h):
    r, c = ref.shape
    if ax == 1:
        return ref.at[pl.ds(h * (r // 2), r // 2), :]
    return ref.at[:, pl.ds(h * (c // 2), c // 2)]


def _half_shape(r, c, ax):
    return (r // 2, c) if ax == 1 else (r, c // 2)


def pair_exchange(name, grads, idxs):
    n = len(grads)
    which = [SHARDED[i] for i in idxs]

    def body(*refs):
        ins, got = refs[:n], refs[n:2 * n]
        send_sems, recv_sems = refs[2 * n:]
        x, y, c = _place()
        cps = []
        for t, (_, _, _, ax) in enumerate(which):
            cp = pltpu.make_async_remote_copy(
                src_ref=_half_of(ins[t], ax, 1 - c), dst_ref=got[t], send_sem=send_sems.at[t],
                recv_sem=recv_sems.at[t], device_id=(x, y, 1 - c), device_id_type=MESH)
            cp.start()
            cps.append(cp)
        for cp in cps:
            cp.wait()

    return pl.pallas_call(
        body, name=name,
        in_specs=[ANY] * n, out_specs=[ANY] * n,
        out_shape=[jax.ShapeDtypeStruct(_half_shape(r, c, ax), F32) for _, r, c, ax in which],
        scratch_shapes=[pltpu.SemaphoreType.DMA((n,)), pltpu.SemaphoreType.DMA((n,))],
        compiler_params=pltpu.CompilerParams(has_side_effects=True),
    )(*grads)


def _piece_shapes(idxs):
    out = []
    for _, r, c, ax in [SHARDED[i] for i in idxs]:
        hr, hc = _half_shape(r, c, ax)
        out.append((hr // 4, hc) if ax == 0 else (hr, hc // 4))
    return out


def _scatter_copies(idxs):
    which = [SHARDED[i] for i in idxs]

    def copies(ins, lands):
        x, y, c = _place()
        out = []
        for t, (_, _, _, ax) in enumerate(which):
            size = ins[t].shape[ax] // 4
            for j, (px, py) in enumerate(_other_chips(x, y)):
                out.append((_shard_of(ins[t], ax, 2 * px + py, size), lands[t].at[j], (px, py, c)))
        return out

    def arrivals(ins, lands):
        return [lands[t].at[j] for t in range(len(which)) for j in range(3)]

    return copies, arrivals


def _piece_of(full, ax, k, h):
    r, c = full.shape
    if ax == 1:
        return full.at[pl.ds(h * (r // 2), r // 2), pl.ds(k * (c // 4), c // 4)]
    return full.at[pl.ds(k * (r // 4), r // 4), pl.ds(h * (c // 2), c // 2)]


def _scatter8_copies(idxs):
    which = [SHARDED[i] for i in idxs]

    def copies(ins, lands):
        x, y, c = _place()
        out = []
        for t, (_, _, _, ax) in enumerate(which):
            for j, (px, py) in enumerate(_other_chips(x, y)):
                for d in range(2):
                    out.append((_piece_of(ins[t], ax, 2 * px + py, c ^ d), lands[t].at[2 * j + d], (px, py, c ^ d)))
            out.append((_piece_of(ins[t], ax, 2 * x + y, 1 - c), lands[t].at[6], (x, y, 1 - c)))
        return out

    def arrivals(ins, lands):
        return [lands[t].at[k] for t in range(len(which)) for k in range(7)]

    return copies, arrivals


def scatter_grads(halves, idxs):
    n = len(halves)
    which = [SHARDED[i] for i in idxs]

    def shapes():
        return _piece_shapes(idxs)

    def body(*refs):
        ins, outs = refs[:n], refs[n:2 * n]
        send_sems, recv_sems = refs[2 * n:]
        x, y, c = _place()
        started = []
        for t, (src, dst, (_, _, _, ax)) in enumerate(zip(ins, outs, which)):
            size = src.shape[ax] // 4
            for j, (px, py) in enumerate(_other_chips(x, y)):
                cp = pltpu.make_async_remote_copy(
                    src_ref=_shard_of(src, ax, 2 * px + py, size), dst_ref=dst.at[j],
                    send_sem=send_sems.at[3 * t + j], recv_sem=recv_sems.at[3 * t + j],
                    device_id=(px, py, c), device_id_type=MESH)
                cp.start()
                started.append(cp)
        for cp in started:
            cp.wait()

    return pl.pallas_call(
        body, name="scatter_grads",
        in_specs=[ANY] * n, out_specs=[ANY] * n,
        out_shape=[jax.ShapeDtypeStruct((3,) + s, BF) for s in shapes()],
        scratch_shapes=[pltpu.SemaphoreType.DMA((3 * n,)), pltpu.SemaphoreType.DMA((3 * n,))],
        compiler_params=pltpu.CompilerParams(has_side_effects=True),
    )(*halves)


def join_halves(name, shards, idxs):
    n = len(shards)
    which = [SHARDED[i] for i in idxs]

    def body(*refs):
        ins, outs = refs[:n], refs[n:2 * n]
        send_sems, recv_sems = refs[2 * n:]
        x, y, c = _place()
        cps = []
        for t, (_, _, _, ax) in enumerate(which):
            cp = pltpu.make_async_remote_copy(
                src_ref=_half_of(ins[t], ax, c), dst_ref=_half_of(outs[t], ax, c), send_sem=send_sems.at[t],
                recv_sem=recv_sems.at[t], device_id=(x, y, 1 - c), device_id_type=MESH)
            cp.start()
            cps.append(cp)
        for t, cp in enumerate(cps):
            cp.wait_send()
            theirs = _half_of(outs[t], which[t][3], 1 - c)
            pltpu.make_async_remote_copy(
                src_ref=theirs, dst_ref=theirs, send_sem=send_sems.at[t], recv_sem=recv_sems.at[t],
                device_id=(x, y, 1 - c), device_id_type=MESH).wait_recv()

    return pl.pallas_call(
        body, name=name,
        in_specs=[ANY] * n, out_specs=[ANY] * n,
        out_shape=[jax.ShapeDtypeStruct(s.shape, F32) for s in shards],
        input_output_aliases={t: t for t in range(n)},
        scratch_shapes=[pltpu.SemaphoreType.DMA((n,)), pltpu.SemaphoreType.DMA((n,))],
        compiler_params=pltpu.CompilerParams(has_side_effects=True),
    )(*shards)


def allreduce_small(pack):
    rows = pack.shape[0]
    hr = rows // 2

    def body(p_ref, o_ref, sib_ref, parts_ref, send_sems, recv_sems):
        x, y, c = _place()
        mine = 2 * x + y
        sibling = (x, y, 1 - c)
        swap = pltpu.make_async_remote_copy(
            src_ref=p_ref, dst_ref=sib_ref, send_sem=send_sems.at[0], recv_sem=recv_sems.at[0],
            device_id=sibling, device_id_type=MESH)
        swap.start()
        swap.wait()
        half = pl.ds(pl.multiple_of(c * hr, 8), hr)
        parts_ref[mine] = p_ref[half, :] + sib_ref[half, :]
        cps = []
        for j, chip in enumerate(_other_chips(x, y)):
            cp = pltpu.make_async_remote_copy(
                src_ref=parts_ref.at[mine], dst_ref=parts_ref.at[mine], send_sem=send_sems.at[1 + j],
                recv_sem=recv_sems.at[1 + j], device_id=(*chip, c), device_id_type=MESH)
            cp.start()
            cps.append(cp)
        for j, (px, py) in enumerate(_other_chips(x, y)):
            cps[j].wait_send()
            theirs = parts_ref.at[2 * px + py]
            pltpu.make_async_remote_copy(
                src_ref=theirs, dst_ref=theirs, send_sem=send_sems.at[1 + j], recv_sem=recv_sems.at[1 + j],
                device_id=(px, py, c), device_id_type=MESH).wait_recv()
        o_ref[half, :] = (parts_ref[0] + parts_ref[1]) + (parts_ref[2] + parts_ref[3])
        back = pltpu.make_async_remote_copy(
            src_ref=o_ref.at[half, :], dst_ref=o_ref.at[half, :], send_sem=send_sems.at[4],
            recv_sem=recv_sems.at[4], device_id=sibling, device_id_type=MESH)
        back.start()
        back.wait_send()
        other = pl.ds(pl.multiple_of((1 - c) * hr, 8), hr)
        pltpu.make_async_remote_copy(
            src_ref=o_ref.at[other, :], dst_ref=o_ref.at[other, :], send_sem=send_sems.at[4],
            recv_sem=recv_sems.at[4], device_id=sibling, device_id_type=MESH).wait_recv()

    return pl.pallas_call(
        body, name="allreduce_small",
        in_specs=[pl.BlockSpec(memory_space=pltpu.VMEM)], out_specs=pl.BlockSpec(memory_space=pltpu.VMEM),
        out_shape=jax.ShapeDtypeStruct((rows, 128), F32),
        scratch_shapes=[pltpu.VMEM((rows, 128), F32), pltpu.VMEM((4, hr, 128), F32),
                        pltpu.SemaphoreType.DMA((5,)), pltpu.SemaphoreType.DMA((5,))],
        compiler_params=pltpu.CompilerParams(has_side_effects=True, vmem_limit_bytes=40 << 20),
    )(pack)


def _ssm_discretize(lam_re, lam_im, log_dt, b_re, b_im):
    dt = jnp.exp(log_dt)[:, None]
    mag = jnp.exp(lam_re * dt)
    ar, ai = mag * jnp.cos(lam_im * dt), mag * jnp.sin(lam_im * dt)
    den = lam_re * lam_re + lam_im * lam_im
    nr, ni = ar - 1.0, ai
    fr, fi = (nr * lam_re + ni * lam_im) / den, (ni * lam_re - nr * lam_im) / den
    bbr = fr[..., None] * b_re - fi[..., None] * b_im
    bbi = fr[..., None] * b_im + fi[..., None] * b_re
    return ar, ai, bbr, bbi


def _cmul(a, b):
    return a[0] * b[0] - a[1] * b[1], a[0] * b[1] + a[1] * b[0]


def _scan_powers(ar, ai):
    a = (ar.reshape(1, SW), ai.reshape(1, SW))
    big = a
    for _ in range(SEG.bit_length() - 1):
        big = _cmul(big, big)
    assert 1 << (SEG.bit_length() - 1) == SEG
    pows = {1: big}
    pows[2] = _cmul(big, big)
    pows[4] = _cmul(pows[2], pows[2])
    rid = jnp.arange(8)[:, None]
    ones = jnp.ones((8, 1), F32)
    fwd, rev = [ones * a[0], ones * a[1]], [ones * a[0], -ones * a[1]]
    for k in (1, 2, 4):
        pr, pi = pows[k]
        fwd += [jnp.where(rid >= k, pr, 0.0), jnp.where(rid >= k, pi, 0.0)]
        rev += [jnp.where(rid < 8 - k, pr, 0.0), jnp.where(rid < 8 - k, -pi, 0.0)]
    return jnp.concatenate(fwd + rev, axis=0)


def _pack_b(bb):
    t = bb.reshape(NLC, 8, NSTATE, GCH).transpose(0, 1, 3, 2)
    return jnp.einsum("jgcp,gh->jgchp", t, jnp.eye(8, dtype=bb.dtype)).reshape(NLC, 128, LC)


def _unpack_b(db):
    t = jnp.einsum("jgchp,gh->jgcp", db.reshape(NLC, 8, GCH, 8, NSTATE), jnp.eye(8, dtype=db.dtype))
    return t.transpose(0, 1, 3, 2).reshape(GROUPS, NSTATE, GCH)


def _pack_c(cc):
    t = cc.reshape(NLC, 8, GCH, NSTATE).transpose(0, 1, 3, 2)
    return jnp.einsum("jgpc,gh->jgphc", t, jnp.eye(8, dtype=cc.dtype)).reshape(NLC, LC, 128)


def _unpack_c(dc):
    t = jnp.einsum("jgphc,gh->jgpc", dc.reshape(NLC, 8, NSTATE, 8, GCH), jnp.eye(8, dtype=dc.dtype))
    return t.transpose(0, 1, 3, 2).reshape(GROUPS, GCH, NSTATE)


SMALL = ("norm_mix", "q_norm", "k_norm", "attn_sinks", "lam_re", "lam_im", "log_dt", "ssm_b_re", "ssm_b_im",
         "ssm_c_re", "ssm_c_im", "ssm_d", "attn_branch_norm", "ssm_branch_norm", "norm_ffn")


def _pack_small(arrs, rows=None):
    flat = jnp.concatenate([a.reshape(-1).astype(F32) for a in arrs])
    n = flat.shape[0]
    if rows is None:
        rows = -(-n // (128 * 256)) * 256
    return jnp.pad(flat, (0, rows * 128 - n)).reshape(rows, 128)


def _unpack_small(pack, like):
    flat = pack.reshape(-1)
    out, off = [], 0
    for a in like:
        out.append(flat[off:off + a.size].reshape(a.shape))
        off += a.size
    return out


def _tie(a, token):
    return a if token is None else a + token[0, 0]


class _NoExchange:
    def __init__(self, w):
        self.w = w
        self.grads = {}

    def begin(self, after):
        return None

    def rest_weights(self, after):
        return self.w

    def start(self, group, idxs, grads):
        self.grads.update(zip(idxs, [g for g, _ in grads]))
        return None

    def finish(self, group, after):
        pass


class _Exchange:
    def __init__(self, rest_shards, place):
        self.rest_shards = rest_shards
        self.place = place
        self.rest_dims = [s[1:] for s in SHARDED[1:]]
        self.pending = {}
        self.halves = {}
        self.full = {}
        self.parts = {}

    def begin(self, after):
        copies, arrivals = _gather_copies(self.rest_dims)
        lands = [lax.empty((r, c), BF) for r, c, _ in self.rest_dims]
        send, recv, srcs, lands, token = _split_copy_start(
            "gather_start", copies, 4 * len(lands), self.rest_shards, lands, after)
        self.pending["gather"] = (copies, arrivals, send, recv, srcs, lands)
        return token

    def rest_weights(self, after):
        copies, arrivals, send, recv, srcs, lands = self.pending.pop("gather")
        _, lands = _split_copy_wait("gather_wait", copies, arrivals, send, recv, srcs, lands, after)
        fulls = gather_forward(lands, self.rest_dims)
        return dict(zip([s[0] for s in SHARDED[1:]], fulls))

    def _halves(self, group, idxs, grads):
        got = pair_exchange("pair_exchange_" + group, grads, idxs)
        return [pair_sum("pair_sum_" + SHARDED[i][0], self.place, g, b, SHARDED[i][3])
                for i, g, b in zip(idxs, grads, got)]

    def start(self, group, idxs, grads):
        if grads[0][1] is None:
            halves = self._halves(group, idxs, [g for g, _ in grads])
            for i, h, pt in zip(idxs, halves, scatter_grads(halves, idxs)):
                self.halves[i], self.parts[i] = h, pt
            return None
        copies, arrivals = _scatter8_copies(idxs)
        lands = [lax.empty((7,) + sh, BF) for sh in _piece_shapes(idxs)]
        send, recv, srcs, lands, token = _split_copy_start(
            "scatter_start_" + group, copies, 7 * len(idxs), [g16 for _, g16 in grads], lands, grads[0][1])
        self.pending[group] = (copies, arrivals, send, recv, srcs, lands, idxs)
        for i, (g32, _) in zip(idxs, grads):
            self.full[i] = g32
        return token

    def finish(self, group, after):
        if group not in self.pending:
            return
        copies, arrivals, send, recv, srcs, lands, idxs = self.pending.pop(group)
        _, lands = _split_copy_wait("scatter_wait_" + group, copies, arrivals, send, recv, srcs, lands, after)
        for i, pt in zip(idxs, lands):
            self.parts[i] = pt


def local_step(x, tgt, meta_full, p, w_in, ex):
    mpad = jnp.concatenate([jnp.zeros((META0, D), F32), meta_full], axis=0)
    qn_t = jnp.tile(p["q_norm"].reshape(1, HEAD), (1, NQ)) * HEAD ** -0.5
    kn_t = jnp.tile(p["k_norm"].reshape(1, HEAD), (1, NKV))
    seg = jnp.arange(256) // HEAD
    bd = (seg[:, None] == seg[None, :]).astype(BF)
    nm = p["norm_mix"].reshape(1, D)
    sinks = p["attn_sinks"].reshape(NQ)
    dsk = p["ssm_d"].reshape(1, D)
    abn, sbn, gf = (p[k].reshape(1, D) for k in ("attn_branch_norm", "ssm_branch_norm", "norm_ffn"))

    disc_in = (p["lam_re"], p["lam_im"], p["log_dt"], p["ssm_b_re"], p["ssm_b_im"])
    (ar, ai, bbr, bbi), disc_vjp = jax.vjp(_ssm_discretize, *disc_in)
    pw = _scan_powers(ar, ai)
    brp, bip = _pack_b(bbr).astype(BF), _pack_b(bbi).astype(BF)
    crp, cip = _pack_c(p["ssm_c_re"]).astype(BF), _pack_c(p["ssm_c_im"]).astype(BF)

    token = ex.begin(w_in)
    xn, qr, kr, qn, kn, v, u, gg = in_proj_fwd(x, mpad, w_in, _tie(nm, token), qn_t, kn_t, bd)
    attn = attention_fwd(sinks, qn, kn, v)
    perm = _segment_order()
    y, sr, si = ssm_fwd(u, pw, brp, bip, crp, cip, dsk, perm)
    w = ex.rest_weights(y)
    zb, zzb, mgb, h1 = mid_fwd(y, attn, gg, x, mpad, w["w_glu"], w["w_out"], abn, sbn)
    hnb, dgub, actb, dh2b, dh1, dh1b, dgf, loss = ffn_fwd_bwd(h1, tgt, w["w_ffn_in"], w["w_ffn_out"], gf)
    token = ex.start("ffn", (3, 4), [weight_grad("grad_w_ffn_in", hnb, dgub, 512, True),
                                     weight_grad("grad_w_ffn_out", actb, dh2b, 512, True)])
    dy, dattn, dggb, dzzb, dabn, dsbn = mid_bwd(dh1b, y, attn, gg, zzb, w["w_glu"], w["w_out"], _tie(abn, token),
                                                sbn)
    grads_mid = [weight_grad("grad_w_glu", zb, dzzb, 1024, True), weight_grad("grad_w_out", mgb, dh1b, 1024, True)]
    dub, da, dbr, dbi, dcr, dci, dd = ssm_bwd(dy, u, sr, si, pw, brp, bip, crp, cip, dsk, perm)
    ex.finish("ffn", dub)
    token = ex.start("mid", (1, 2), grads_mid)
    dqn, dkn, dv, dsink = attention_bwd(_tie(sinks, token), qn, kn, v, dattn)
    dproj, dx, dmpad, dnm, dqw, dkw = in_proj_bwd(dqn, dkn, dv, dub, dggb, qr, kr, x, mpad, dh1, w_in, nm, qn_t,
                                                  kn_t, bd)
    ex.finish("mid", dproj)
    ex.start("in", (0,), [(weight_grad("grad_w_in", xn, dproj, 1536), None)])

    dar = jnp.sum(da[0:8], axis=0).reshape(GROUPS, NSTATE)
    dai = jnp.sum(da[8:16], axis=0).reshape(GROUPS, NSTATE)
    dlr, dli, dldt, dbre, dbim = disc_vjp((dar, dai, _unpack_b(dbr), _unpack_b(dbi)))
    small = {
        "norm_mix": dnm, "q_norm": dqw.reshape(NQ, HEAD).sum(0) * HEAD ** -0.5, "k_norm": dkw.reshape(NKV, HEAD).sum(0),
        "attn_sinks": dsink[0, 0:NQ], "lam_re": dlr, "lam_im": dli, "log_dt": dldt,
        "ssm_b_re": dbre, "ssm_b_im": dbim, "ssm_c_re": _unpack_c(jnp.swapaxes(dcr, 1, 2)), "ssm_c_im": _unpack_c(jnp.swapaxes(dci, 1, 2)),
        "ssm_d": dd, "attn_branch_norm": dabn, "ssm_branch_norm": dsbn, "norm_ffn": dgf,
    }
    return loss[0, 0], dx, dmpad[META0:PAD], small


def kernel(x, meta_tokens, norm_mix, w_in, q_norm, k_norm, attn_sinks, lam_re, lam_im, log_dt, ssm_b_re, ssm_b_im, ssm_c_re, ssm_c_im, ssm_d, w_glu, attn_branch_norm, ssm_branch_norm, w_out, norm_ffn, w_ffn_in, w_ffn_out, loss_target, m_meta_tokens, m_norm_mix, m_w_in, m_q_norm, m_k_norm, m_attn_sinks, m_lam_re, m_lam_im, m_log_dt, m_ssm_b_re, m_ssm_b_im, m_ssm_c_re, m_ssm_c_im, m_ssm_d, m_w_glu, m_attn_branch_norm, m_ssm_branch_norm, m_w_out, m_norm_ffn, m_w_ffn_in, m_w_ffn_out, v_meta_tokens, v_norm_mix, v_w_in, v_q_norm, v_k_norm, v_attn_sinks, v_lam_re, v_lam_im, v_log_dt, v_ssm_b_re, v_ssm_b_im, v_ssm_c_re, v_ssm_c_im, v_ssm_d, v_w_glu, v_attn_branch_norm, v_ssm_branch_norm, v_w_out, v_norm_ffn, v_w_ffn_in, v_w_ffn_out):
    args = dict(locals())
    names = ["meta_tokens", "norm_mix", "w_in", "q_norm", "k_norm", "attn_sinks", "lam_re", "lam_im", "log_dt",
             "ssm_b_re", "ssm_b_im", "ssm_c_re", "ssm_c_im", "ssm_d", "w_glu", "attn_branch_norm",
             "ssm_branch_norm", "w_out", "norm_ffn", "w_ffn_in", "w_ffn_out"]
    big_names = [s[0] for s in SHARDED]
    xs, ys, cs = _place()
    chip = 2 * xs + ys

    shards = [args[n][0].astype(BF) for n in big_names]
    w_in_full, meta_full = gather_weights([shards[0], meta_tokens], [SHARDED[0][1:], (N_META, D, 1)])
    place = jnp.stack([chip, cs]).astype(jnp.int32)
    ex = _Exchange(shards[1:], place)

    p = {n: args[n][0] for n in SMALL}
    loss, dx, dmeta, gsmall = local_step(x[0], loss_target[0], meta_full, p, w_in_full, ex)

    out = {}

    def finish(idxs, name):
        mine = [sum8("sum8_" + SHARDED[i][0], place, ex.parts[i], ex.full[i], SHARDED[i][3]) if i in ex.full
                else sum4("sum4_" + SHARDED[i][0], place, ex.parts[i], ex.halves[i], SHARDED[i][3]) for i in idxs]
        for i, g in zip(idxs, join_halves(name, mine, idxs)):
            n = SHARDED[i][0]
            d, nm_, nv_ = adamw("adamw_" + n, args[n][0], g, args["m_" + n][0], args["v_" + n][0])
            out[n] = tuple(t[None] for t in (g, d, nm_, nv_))

    finish((0, 1, 2, 3, 4), "join_halves")
    small_list = [gsmall[n].reshape(args[n].shape) for n in SMALL] + [dmeta, loss.reshape(1, 1)]
    red = allreduce_small(_pack_small(small_list))
    *small_red, dmeta_sum, loss = _unpack_small(red, small_list)
    loss = loss[0, 0]
    gmeta = lax.dynamic_slice_in_dim(dmeta_sum, chip * (D // 4), D // 4, axis=1)
    d, nm_, nv_ = adamw("adamw_meta", meta_tokens, gmeta, m_meta_tokens, v_meta_tokens)
    out["meta_tokens"] = (gmeta, d, nm_, nv_)
    ds, nms, nvs = adamw_small([args[n] for n in SMALL], small_red, [args["m_" + n] for n in SMALL],
                               [args["v_" + n] for n in SMALL])
    for n, g_, d_, m_, v_ in zip(SMALL, small_red, ds, nms, nvs):
        out[n] = (g_, d_, m_, v_)

    res = [loss, dx[None]]
    for k in range(4):
        res += [out[n][k] for n in names]
    return tuple(res)
```

```python
import functools
import math

import jax
import jax.numpy as jnp
from jax import lax
from jax.experimental import pallas as pl
from jax.experimental.pallas import tpu as pltpu

F32 = jnp.float32
BF = jnp.bfloat16

D = 1024
N_META = 16
HEAD = 64
NQ = 16
NKV = 4
QW = NQ * HEAD
KVW = NKV * HEAD
WIN = 128
GROUPS = 64
GCH = 16
NSTATE = 64
SW = GROUPS * NSTATE
DFF = 2816
IN_COLS = QW + 2 * KVW + D + 2 * D
PAD = 256
META0 = PAD - N_META
EPS = 1e-6
NEG = -1e30

TM = 256
TS = 256
LC = 512
NLC = SW // LC

LR, B1, B2, AEPS, WD, STEP = 0.001, 0.9, 0.999, 1e-08, 0.01, 10
BC1 = 1.0 - B1 ** STEP
BC2 = 1.0 - B2 ** STEP

MESH = pl.DeviceIdType.MESH
ANY = pl.BlockSpec(memory_space=pl.ANY)


def _cp(sem=None, vmem_mb=48):
    return pltpu.CompilerParams(dimension_semantics=sem, vmem_limit_bytes=vmem_mb << 20)


def _full(shape):
    n = len(shape)
    return pl.BlockSpec(shape, lambda *a: (0,) * n)


def _const(shape):
    n = len(shape)
    return pl.BlockSpec(shape, lambda *a: (0,) * n, pipeline_mode=pl.Buffered(1))


def _rows(tm, width):
    return pl.BlockSpec((tm, width), lambda i: (i, 0))


def _dot(a, b):
    return jnp.dot(a, b, preferred_element_type=F32)


def _dot_nt(a, b):
    return lax.dot_general(a, b, (((1,), (1,)), ((), ())), preferred_element_type=F32)


def _dot_tn(a, b):
    return lax.dot_general(a, b, (((0,), (0,)), ((), ())), preferred_element_type=F32)


def _sigmoid(x):
    return 0.5 * jnp.tanh(0.5 * x) + 0.5


def _seg_mean(x, bd):
    outs = []
    for j in range(x.shape[1] // 256):
        c = x[:, 256 * j:256 * (j + 1)]
        hi = c.astype(BF)
        lo = (c - hi.astype(F32)).astype(BF)
        outs.append(_dot(hi, bd) + _dot(lo, bd))
    out = outs[0] if len(outs) == 1 else jnp.concatenate(outs, axis=1)
    return out * (1.0 / HEAD)


_GC = math.sqrt(2.0 / math.pi)


def _gelu(x):
    t = jnp.tanh(_GC * (x + 0.044715 * x * x * x))
    return 0.5 * x * (1.0 + t), t


def _gelu_grad(x, t):
    return 0.5 * (1.0 + t) + 0.5 * x * (1.0 - t * t) * _GC * (1.0 + 3.0 * 0.044715 * x * x)


def _seq_specs():
    assert TM == PAD
    return [pl.BlockSpec((TM, D), lambda i: (jnp.maximum(i - 1, 0), 0)), _full((PAD, D))]


def _seq_tile(x_ref, m_ref):
    return jnp.where(pl.program_id(0) == 0, m_ref[...], x_ref[...])


def in_proj_fwd(x, mpad, w_in, nm, qn_t, kn_t, bd):
    lp = x.shape[0] + PAD

    def body(x_ref, m_ref, w_ref, nm_ref, qn_ref, kn_ref, bd_ref,
             xn_o, qr_o, kr_o, qn_o, kn_o, v_o, u_o, gg_o):
        h = _seq_tile(x_ref, m_ref)
        r = lax.rsqrt(jnp.mean(h * h, axis=-1, keepdims=True) + EPS)
        xb = ((h * r) * nm_ref[...]).astype(BF)
        xn_o[...] = xb
        bdv = bd_ref[...]
        q = _dot(xb, w_ref[:, 0:QW])
        qr_o[...] = q
        qn_o[...] = (q * lax.rsqrt(_seg_mean(q * q, bdv) + EPS) * qn_ref[...]).astype(BF)
        k = _dot(xb, w_ref[:, QW:QW + KVW])
        kr_o[...] = k
        kn_o[...] = (k * lax.rsqrt(_seg_mean(k * k, bdv) + EPS) * kn_ref[...]).astype(BF)
        v_o[...] = _dot(xb, w_ref[:, QW + KVW:QW + 2 * KVW]).astype(BF)
        u_o[...] = _dot(xb, w_ref[:, QW + 2 * KVW:QW + 2 * KVW + D])
        gg_o[...] = _dot(xb, w_ref[:, QW + 2 * KVW + D:IN_COLS])

    outs = [(D, BF), (QW, F32), (KVW, F32), (QW, BF), (KVW, BF), (KVW, BF), (D, F32), (2 * D, F32)]
    return pl.pallas_call(
        body, name="in_proj_fwd", grid=(lp // TM,),
        in_specs=_seq_specs() + [_full((D, IN_COLS)), _full((1, D)), _full((1, QW)), _full((1, KVW)),
                                 _full((256, 256))],
        out_specs=[_rows(TM, w) for w, _ in outs],
        out_shape=[jax.ShapeDtypeStruct((lp, w), t) for w, t in outs],
        compiler_params=_cp(("arbitrary",)),
    )(x, mpad, w_in, nm, qn_t, kn_t, bd)


def in_proj_bwd(dqn, dkn, dv, du, dgg, qr, kr, x, mpad, dh1, w_in, nm, qn_t, kn_t, bd):
    lp = x.shape[0] + PAD

    def body(dqn_ref, dkn_ref, dv_ref, du_ref, dgg_ref, qr_ref, kr_ref, x_ref, m_ref, dh1_ref,
             w_ref, nm_ref, qn_ref, kn_ref, bd_ref,
             dproj_o, dx_o, dm_o, dnm_o, dqw_o, dkw_o):
        i = pl.program_id(0)

        @pl.when(i == 0)
        def _():
            dnm_o[...] = jnp.zeros_like(dnm_o)
            dqw_o[...] = jnp.zeros_like(dqw_o)
            dkw_o[...] = jnp.zeros_like(dkw_o)

        bdv = bd_ref[...]

        def norm_bwd(x, dy, g, acc_ref):
            rr = lax.rsqrt(_seg_mean(x * x, bdv) + EPS)
            xh = x * rr
            acc_ref[...] += jnp.sum(dy * xh, axis=0, keepdims=True)
            dxh = dy * g
            return rr * (dxh - xh * _seg_mean(dxh * xh, bdv))

        dq = norm_bwd(qr_ref[...], dqn_ref[...], qn_ref[...], dqw_o)
        dk = norm_bwd(kr_ref[...], dkn_ref[...], kn_ref[...], dkw_o)
        dproj_o[:, 0:QW] = dq.astype(BF)
        dproj_o[:, QW:QW + KVW] = dk.astype(BF)
        dproj_o[:, QW + KVW:QW + 2 * KVW] = dv_ref[...].astype(BF)
        dproj_o[:, QW + 2 * KVW:QW + 2 * KVW + D] = du_ref[...]
        dproj_o[:, QW + 2 * KVW + D:IN_COLS] = dgg_ref[...]
        dxn = _dot_nt(dproj_o[...], w_ref[...])
        h = _seq_tile(x_ref, m_ref)
        r = lax.rsqrt(jnp.mean(h * h, axis=-1, keepdims=True) + EPS)
        xh = h * r
        dnm_o[...] += jnp.sum(dxn * xh, axis=0, keepdims=True)
        dxh = dxn * nm_ref[...]
        dh0 = dh1_ref[...] + r * (dxh - xh * jnp.mean(dxh * xh, axis=-1, keepdims=True))
        dx_o[...] = dh0

        @pl.when(i == 0)
        def _():
            dm_o[...] = dh0

    return pl.pallas_call(
        body, name="in_proj_bwd", grid=(lp // TM,),
        in_specs=[_rows(TM, QW), _rows(TM, KVW), _rows(TM, KVW), _rows(TM, D), _rows(TM, 2 * D),
                  _rows(TM, QW), _rows(TM, KVW)] + _seq_specs() + [_rows(TM, D),
                  _full((D, IN_COLS)), _full((1, D)), _full((1, QW)), _full((1, KVW)), _full((256, 256))],
        out_specs=[_rows(TM, IN_COLS), _seq_specs()[0], _full((PAD, D)), _full((1, D)), _full((1, QW)),
                   _full((1, KVW))],
        out_shape=[jax.ShapeDtypeStruct((lp, IN_COLS), BF), jax.ShapeDtypeStruct((lp - PAD, D), F32),
                   jax.ShapeDtypeStruct((PAD, D), F32),
                   jax.ShapeDtypeStruct((1, D), F32), jax.ShapeDtypeStruct((1, QW), F32),
                   jax.ShapeDtypeStruct((1, KVW), F32)],
        compiler_params=_cp(("arbitrary",)),
    )(dqn, dkn, dv, du, dgg, qr, kr, x, mpad, dh1, w_in, nm, qn_t, kn_t, bd)


def _att_bias(b):
    qi = lax.broadcasted_iota(jnp.int32, (WIN, WIN), 0)
    kj = lax.broadcasted_iota(jnp.int32, (WIN, WIN), 1)
    upper = kj > qi
    valid = (upper & ((b - 1) * WIN + kj >= PAD)) | (jnp.logical_not(upper) & (b * WIN + kj >= META0))
    bias_pc = jnp.where(valid, 0.0, NEG)
    kj1 = lax.broadcasted_iota(jnp.int32, (1, WIN), 1)
    bias_m = jnp.where((kj1 >= WIN - N_META) & (b * WIN >= PAD), 0.0, NEG)
    return bias_m, bias_pc, upper


def _dup_half(ref, kp, pos, lo):
    pair = ref[:, 128 * kp:128 * kp + 128].astype(F32)
    rolled = pltpu.roll(pair, 64, axis=1)
    keep = lo if pos == 0 else jnp.logical_not(lo)
    return jnp.where(keep, pair, rolled).astype(BF)


def _stack_heads(ref, kv, lo):
    parts = []
    for pp in range(2):
        pair = ref[:, 256 * kv + 128 * pp:256 * kv + 128 * pp + 128]
        zero = jnp.zeros_like(pair)
        parts.append(jnp.where(lo, pair, zero))
        parts.append(jnp.where(lo, zero, pair))
    return jnp.concatenate(parts, axis=0)


def _unstack_heads(x4, lo):
    a = jnp.where(lo, x4[0:WIN], x4[WIN:2 * WIN])
    b = jnp.where(lo, x4[2 * WIN:3 * WIN], x4[3 * WIN:4 * WIN])
    return a, b


def _split_pc(x, upper):
    zero = jnp.zeros_like(x)
    return jnp.where(upper, x, zero).astype(BF), jnp.where(upper, zero, x).astype(BF)


def _head_softmax(s_m, s_p, s_c, bias, sink):
    bias_m, bias_pc, upper = bias
    sm = s_m + bias_m
    spc = jnp.where(upper, s_p, s_c) + bias_pc
    m = jnp.maximum(jnp.max(jnp.maximum(sm, spc), axis=-1, keepdims=True), sink)
    em, epc = jnp.exp(sm - m), jnp.exp(spc - m)
    esink = jnp.exp(sink - m)
    inv = 1.0 / (esink + jnp.sum(em + epc, axis=-1, keepdims=True))
    return em, epc, inv, esink


def _kv_specs():
    return [pl.BlockSpec((WIN, KVW), lambda b: (1, 0)),
            pl.BlockSpec((WIN, KVW), lambda b: (jnp.maximum(b - 1, 0), 0)),
            pl.BlockSpec((WIN, KVW), lambda b: (b, 0))]


def attention_fwd(sinks, qn, kn, v):
    lp = qn.shape[0]

    def body(sink_ref, q_ref, km_ref, kp_ref, kc_ref, vm_ref, vp_ref, vc_ref, o_ref):
        b = pl.program_id(0)
        bias = _att_bias(b)
        lo = lax.broadcasted_iota(jnp.int32, (WIN, WIN), 1) < HEAD
        for kv in range(NKV):
            kp, pos = kv // 2, kv % 2
            kds = [_dup_half(r, kp, pos, lo) for r in (km_ref, kp_ref, kc_ref)]
            vds = [_dup_half(r, kp, pos, lo) for r in (vm_ref, vp_ref, vc_ref)]
            q4 = _stack_heads(q_ref, kv, lo)
            ss = [_dot_nt(q4, kd) for kd in kds]
            es, invs = ([], [], []), []
            for h in range(4):
                rs = slice(WIN * h, WIN * h + WIN)
                em, epc, inv, _ = _head_softmax(ss[0][rs], ss[1][rs], ss[2][rs], bias, sink_ref[4 * kv + h])
                e_p, e_c = _split_pc(epc, bias[2])
                for lst, e in zip(es, (em.astype(BF), e_p, e_c)):
                    lst.append(e)
                invs.append(inv)
            e_m, e_p, e_c = (jnp.concatenate(lst, axis=0) for lst in es)
            o4 = (_dot(e_m, vds[0]) + _dot(e_p, vds[1]) + _dot(e_c, vds[2])) * jnp.concatenate(invs, axis=0)
            oa, ob = _unstack_heads(o4, lo)
            o_ref[:, 256 * kv:256 * kv + 128] = oa
            o_ref[:, 256 * kv + 128:256 * kv + 256] = ob

    return pl.pallas_call(
        body, name="attention_fwd", grid=(lp // WIN,),
        in_specs=[pl.BlockSpec(memory_space=pltpu.SMEM), _rows(WIN, QW)] + _kv_specs() + _kv_specs(),
        out_specs=_rows(WIN, QW),
        out_shape=jax.ShapeDtypeStruct((lp, QW), F32),
        compiler_params=_cp(("arbitrary",)),
    )(sinks, qn, kn, kn, kn, v, v, v)


def attention_bwd(sinks, qn, kn, v, do):
    lp = qn.shape[0]
    nb = lp // WIN

    def body(sink_ref, q_ref, km_ref, kp_ref, kc_ref, vm_ref, vp_ref, vc_ref, do_ref,
             dq_ref, dk_hbm, dv_hbm, dsink_ref, dk_acc, dv_acc):
        b = pl.program_id(0)

        @pl.when(b == 0)
        def _():
            dk_acc[...] = jnp.zeros_like(dk_acc)
            dv_acc[...] = jnp.zeros_like(dv_acc)
            dsink_ref[...] = jnp.zeros_like(dsink_ref)

        bias = _att_bias(b)
        upper = bias[2]
        lo = lax.broadcasted_iota(jnp.int32, (WIN, WIN), 1) < HEAD
        lane8 = lax.broadcasted_iota(jnp.int32, (8, 128), 1)
        starts = [WIN, pl.multiple_of(jnp.maximum(b - 1, 0) * WIN, WIN), pl.multiple_of(b * WIN, WIN)]
        for kv in range(NKV):
            kp, pos = kv // 2, kv % 2
            keep = lo if pos == 0 else jnp.logical_not(lo)
            kds = [_dup_half(r, kp, pos, lo) for r in (km_ref, kp_ref, kc_ref)]
            vds = [_dup_half(r, kp, pos, lo) for r in (vm_ref, vp_ref, vc_ref)]
            q4 = _stack_heads(q_ref, kv, lo)
            do4 = _stack_heads(do_ref, kv, lo)
            ss = [_dot_nt(q4, kd) for kd in kds]
            dps = [_dot_nt(do4, vd) for vd in vds]
            ds_l, p_l = ([], [], []), ([], [], [])
            for h in range(4):
                rs = slice(WIN * h, WIN * h + WIN)
                em, epc, inv, esink = _head_softmax(ss[0][rs], ss[1][rs], ss[2][rs], bias, sink_ref[4 * kv + h])
                p_m, p_pc = em * inv, epc * inv
                dp_m = dps[0][rs]
                dp_pc = jnp.where(upper, dps[1][rs], dps[2][rs])
                delta = jnp.sum(p_m * dp_m + p_pc * dp_pc, axis=-1, keepdims=True)
                val = -jnp.sum(esink * inv * delta, axis=0, keepdims=True)
                dsink_ref[...] += jnp.where(lane8 == 4 * kv + h, val, 0.0)
                for lst, t in zip(ds_l, ((p_m * (dp_m - delta)).astype(BF),) + _split_pc(p_pc * (dp_pc - delta), upper)):
                    lst.append(t)
                for lst, t in zip(p_l, (p_m.astype(BF),) + _split_pc(p_pc, upper)):
                    lst.append(t)
            dss = [jnp.concatenate(lst, axis=0) for lst in ds_l]
            ps = [jnp.concatenate(lst, axis=0) for lst in p_l]
            dq4 = _dot(dss[0], kds[0]) + _dot(dss[1], kds[1]) + _dot(dss[2], kds[2])
            dqa, dqb = _unstack_heads(dq4, lo)
            dq_ref[:, 256 * kv:256 * kv + 128] = dqa
            dq_ref[:, 256 * kv + 128:256 * kv + 256] = dqb
            for x in range(3):
                dkd = _dot_tn(dss[x], q4)
                dvd = _dot_tn(ps[x], do4)
                dkd = jnp.where(keep, dkd + pltpu.roll(dkd, 64, axis=1), 0.0)
                dvd = jnp.where(keep, dvd + pltpu.roll(dvd, 64, axis=1), 0.0)
                dk_acc[pl.ds(starts[x], WIN), 128 * kp:128 * kp + 128] += dkd
                dv_acc[pl.ds(starts[x], WIN), 128 * kp:128 * kp + 128] += dvd

        @pl.when(b == nb - 1)
        def _():
            pltpu.sync_copy(dk_acc, dk_hbm)
            pltpu.sync_copy(dv_acc, dv_hbm)

    return pl.pallas_call(
        body, name="attention_bwd", grid=(nb,),
        in_specs=[pl.BlockSpec(memory_space=pltpu.SMEM), _rows(WIN, QW)] + _kv_specs() + _kv_specs()
                 + [_rows(WIN, QW)],
        out_specs=[_rows(WIN, QW), ANY, ANY, _full((8, 128))],
        out_shape=[jax.ShapeDtypeStruct((lp, QW), F32), jax.ShapeDtypeStruct((lp, KVW), F32),
                   jax.ShapeDtypeStruct((lp, KVW), F32), jax.ShapeDtypeStruct((8, 128), F32)],
        scratch_shapes=[pltpu.VMEM((lp, KVW), F32), pltpu.VMEM((lp, KVW), F32)],
        compiler_params=_cp(("arbitrary",)),
    )(sinks, qn, kn, kn, kn, v, v, v, do)


PW_ROWS = 64
SEG = TS // 8


def _segment_order():
    rho = jnp.arange(TS)
    token = SEG * (rho % 8) + rho // 8
    p = (token[:, None] == jnp.arange(TS)[None, :]).astype(BF)
    return p, p.T


def _reorder(p, x):
    hi = x.astype(BF)
    r1 = x - hi.astype(F32)
    mid = r1.astype(BF)
    lo = (r1 - mid.astype(F32)).astype(BF)
    return _dot(p, hi) + _dot(p, mid) + _dot(p, lo)


def _segment_scan(xr_ref, xi_ref, pw_ref, base, ls, c_r, c_i, reverse, visit=None):
    ar, ai = pw_ref[base:base + 8, ls], pw_ref[base + 8:base + 16, ls]
    order = list(range(SEG - 1, -1, -1)) if reverse else list(range(SEG))
    s_r = s_i = jnp.zeros_like(ar)
    for i in order:
        rows = pl.ds(8 * i, 8)
        s_r, s_i = ar * s_r - ai * s_i + xr_ref[rows, ls], ar * s_i + ai * s_r + xi_ref[rows, ls]
        xr_ref[rows, ls] = s_r
        xi_ref[rows, ls] = s_i
    rid = lax.broadcasted_iota(jnp.int32, s_r.shape, 0)
    if reverse:
        e_r = jnp.where(rid < 7, pltpu.roll(s_r, 7, axis=0), c_r)
        e_i = jnp.where(rid < 7, pltpu.roll(s_i, 7, axis=0), c_i)
    else:
        e_r = jnp.where(rid >= 1, pltpu.roll(s_r, 1, axis=0), c_r)
        e_i = jnp.where(rid >= 1, pltpu.roll(s_i, 1, axis=0), c_i)
    for n, k in enumerate((1, 2, 4)):
        shift = 8 - k if reverse else k
        t_r, t_i = pltpu.roll(e_r, shift, axis=0), pltpu.roll(e_i, shift, axis=0)
        kr = pw_ref[base + 16 + 16 * n:base + 24 + 16 * n, ls]
        ki = pw_ref[base + 24 + 16 * n:base + 32 + 16 * n, ls]
        e_r, e_i = e_r + kr * t_r - ki * t_i, e_i + kr * t_i + ki * t_r
    last = 0 if reverse else 7
    a16r, a16i = pw_ref[base + 16 + last:base + 17 + last, ls], pw_ref[base + 24 + last:base + 25 + last, ls]
    lr, li, fr, fi = (t[last:last + 1] for t in (e_r, e_i, s_r, s_i))
    out = (a16r * lr - a16i * li + fr, a16r * li + a16i * lr + fi)
    d_r, d_i = e_r, e_i
    extra = None
    for i in order:
        rows = pl.ds(8 * i, 8)
        d_r, d_i = ar * d_r - ai * d_i, ar * d_i + ai * d_r
        f_r, f_i = xr_ref[rows, ls] + d_r, xi_ref[rows, ls] + d_i
        xr_ref[rows, ls] = f_r
        xi_ref[rows, ls] = f_i
        if visit is not None:
            extra = visit(i, f_r, f_i, extra)
    return out if visit is None else (out, extra)


def ssm_fwd(u, pw, br, bi, cr, ci, dsk, perm):
    lp = u.shape[0]

    def body(u_ref, pw_ref, br_ref, bi_ref, cr_ref, ci_ref, d_ref, p_ref, pt_ref,
             y_ref, sr_ref, si_ref, car_r, car_i, yp_s):
        i = pl.program_id(0)

        @pl.when(i == 0)
        def _():
            car_r[...] = jnp.zeros_like(car_r)
            car_i[...] = jnp.zeros_like(car_i)

        u = u_ref[...]
        ub = _dot(p_ref[...], u.astype(BF)).astype(BF)
        for j in range(NLC):
            uj = ub[:, 128 * j:128 * j + 128]
            sr_ref[:, LC * j:LC * j + LC] = _dot(uj, br_ref[j])
            si_ref[:, LC * j:LC * j + LC] = _dot(uj, bi_ref[j])
        for j in range(NLC):
            ls = slice(LC * j, LC * j + LC)
            car_r[:, ls], car_i[:, ls] = _segment_scan(sr_ref, si_ref, pw_ref, 0, ls, car_r[:, ls], car_i[:, ls],
                                                       False)
        for j in range(NLC):
            ls = slice(LC * j, LC * j + LC)
            yp_s[:, 128 * j:128 * j + 128] = (_dot(sr_ref[:, ls].astype(BF), cr_ref[j])
                                              - _dot(si_ref[:, ls].astype(BF), ci_ref[j]))
        y_ref[...] = _reorder(pt_ref[...], yp_s[...]) + d_ref[...] * u

    p, pt = perm
    return pl.pallas_call(
        body, name="ssm_fwd", grid=(lp // TS,),
        in_specs=[_rows(TS, D), _full((2 * PW_ROWS, SW)), _full((NLC, 128, LC)), _full((NLC, 128, LC)),
                  _full((NLC, LC, 128)), _full((NLC, LC, 128)), _full((1, D)), _full((TS, TS)), _full((TS, TS))],
        out_specs=[_rows(TS, D), _rows(TS, SW), _rows(TS, SW)],
        out_shape=[jax.ShapeDtypeStruct((lp, D), F32), jax.ShapeDtypeStruct((lp, SW), F32),
                   jax.ShapeDtypeStruct((lp, SW), F32)],
        scratch_shapes=[pltpu.VMEM((1, SW), F32), pltpu.VMEM((1, SW), F32), pltpu.VMEM((TS, D), F32)],
        compiler_params=_cp(("arbitrary",)),
    )(u, pw, br, bi, cr, ci, dsk, p, pt)


def ssm_bwd(dy, u, sr, si, pw, br, bi, cr, ci, dsk, perm):
    lp = u.shape[0]
    nt = lp // TS

    def rev(i):
        return (nt - 1 - i, 0)

    def prev8(i):
        return (jnp.maximum((nt - 1 - i) * (TS // 8) - 1, 0), 0)

    def body(dy_ref, u_ref, sr_ref, si_ref, pr8_ref, pi8_ref, pw_ref, br_ref, bi_ref, cr_ref, ci_ref, d_ref,
             p_ref, pt_ref, du_ref, da_ref, dbr_ref, dbi_ref, dcr_ref, dci_ref, dd_ref,
             gr_s, gi_s, car_r, car_i, dup_s):
        i = pl.program_id(0)

        @pl.when(i == 0)
        def _():
            car_r[...] = jnp.zeros_like(car_r)
            car_i[...] = jnp.zeros_like(car_i)
            for ref in (da_ref, dbr_ref, dbi_ref, dcr_ref, dci_ref, dd_ref):
                ref[...] = jnp.zeros_like(ref)

        keep = 1.0 - (i == nt - 1).astype(F32)
        dy = dy_ref[...]
        u = u_ref[...]
        dd_ref[...] += jnp.sum(dy * u, axis=0, keepdims=True)
        pm = p_ref[...]
        dyb = _dot(pm, dy.astype(BF)).astype(BF)
        ub = _dot(pm, u.astype(BF)).astype(BF)
        for j in range(NLC):
            ls = slice(LC * j, LC * j + LC)
            dyj = dyb[:, 128 * j:128 * j + 128]
            gr_s[:, ls] = _dot_nt(dyj, cr_ref[j])
            gi_s[:, ls] = -_dot_nt(dyj, ci_ref[j])
            dcr_ref[j] += _dot_tn(dyj, sr_ref[:, ls].astype(BF))
            dci_ref[j] -= _dot_tn(dyj, si_ref[:, ls].astype(BF))
        rid = lax.broadcasted_iota(jnp.int32, (8, LC), 0)
        for j in range(NLC):
            ls = slice(LC * j, LC * j + LC)

            def accumulate(i, g_r, g_i, acc):
                if i >= 1:
                    rows = pl.ds(8 * (i - 1), 8)
                    p_r, p_i = sr_ref[rows, ls], si_ref[rows, ls]
                else:
                    rows = pl.ds(8 * (SEG - 1), 8)
                    p_r = jnp.where(rid >= 1, pltpu.roll(sr_ref[rows, ls], 1, axis=0), pr8_ref[7:8, ls] * keep)
                    p_i = jnp.where(rid >= 1, pltpu.roll(si_ref[rows, ls], 1, axis=0), pi8_ref[7:8, ls] * keep)
                t_r, t_i = g_r * p_r + g_i * p_i, g_i * p_r - g_r * p_i
                return (t_r, t_i) if acc is None else (acc[0] + t_r, acc[1] + t_i)

            (car_r[:, ls], car_i[:, ls]), (t_r, t_i) = _segment_scan(
                gr_s, gi_s, pw_ref, PW_ROWS, ls, car_r[:, ls], car_i[:, ls], True, accumulate)
            da_ref[0:8, ls] += t_r
            da_ref[8:16, ls] += t_i
        for j in range(NLC):
            ls = slice(LC * j, LC * j + LC)
            uj = ub[:, 128 * j:128 * j + 128]
            grb = gr_s[:, ls].astype(BF)
            gib = gi_s[:, ls].astype(BF)
            dup_s[:, 128 * j:128 * j + 128] = _dot_nt(grb, br_ref[j]) + _dot_nt(gib, bi_ref[j])
            dbr_ref[j] += _dot_tn(uj, grb)
            dbi_ref[j] += _dot_tn(uj, gib)
        du_ref[...] = (_reorder(pt_ref[...], dup_s[...]) + d_ref[...] * dy).astype(BF)

    p, pt = perm
    return pl.pallas_call(
        body, name="ssm_bwd", grid=(nt,),
        in_specs=[pl.BlockSpec((TS, D), rev), pl.BlockSpec((TS, D), rev), pl.BlockSpec((TS, SW), rev),
                  pl.BlockSpec((TS, SW), rev), pl.BlockSpec((8, SW), prev8), pl.BlockSpec((8, SW), prev8),
                  _full((2 * PW_ROWS, SW)), _full((NLC, 128, LC)), _full((NLC, 128, LC)),
                  _full((NLC, LC, 128)), _full((NLC, LC, 128)), _full((1, D)), _full((TS, TS)), _full((TS, TS))],
        out_specs=[pl.BlockSpec((TS, D), rev), _full((16, SW)), _full((NLC, 128, LC)), _full((NLC, 128, LC)),
                   _full((NLC, 128, LC)), _full((NLC, 128, LC)), _full((1, D))],
        out_shape=[jax.ShapeDtypeStruct((lp, D), BF), jax.ShapeDtypeStruct((16, SW), F32),
                   jax.ShapeDtypeStruct((NLC, 128, LC), F32), jax.ShapeDtypeStruct((NLC, 128, LC), F32),
                   jax.ShapeDtypeStruct((NLC, 128, LC), F32), jax.ShapeDtypeStruct((NLC, 128, LC), F32),
                   jax.ShapeDtypeStruct((1, D), F32)],
        scratch_shapes=[pltpu.VMEM((TS, SW), F32), pltpu.VMEM((TS, SW), F32),
                        pltpu.VMEM((1, SW), F32), pltpu.VMEM((1, SW), F32), pltpu.VMEM((TS, D), F32)],
        compiler_params=_cp(("arbitrary",)),
    )(dy, u, sr, si, sr, si, pw, br, bi, cr, ci, dsk, p, pt)


def _merge_parts(attn, zz, gg, abn, sbn):
    za, zb = zz[:, 0:D], zz[:, D:2 * D]
    sgz = _sigmoid(zb)
    ssm = za * sgz
    ra = lax.rsqrt(jnp.mean(attn * attn, axis=-1, keepdims=True) + EPS)
    rs = lax.rsqrt(jnp.mean(ssm * ssm, axis=-1, keepdims=True) + EPS)
    ah, sh = attn * ra, ssm * rs
    sga, sgs = _sigmoid(gg[:, 0:D]), _sigmoid(gg[:, D:2 * D])
    return za, sgz, ra, rs, ah, sh, sga, sgs, ah * abn, sh * sbn


def mid_fwd(y, attn, gg, x, mpad, w_glu, w_out, abn, sbn):
    lp = y.shape[0]

    def body(y_ref, a_ref, gg_ref, x_ref, m_ref, wg_ref, wo_ref, abn_ref, sbn_ref, z_o, zz_o, mg_o, h1_o):
        z, _ = _gelu(y_ref[...])
        zb = z.astype(BF)
        z_o[...] = zb
        zz = _dot(zb, wg_ref[...])
        zz_o[...] = zz.astype(BF)
        _, _, _, _, _, _, sga, sgs, an, sn = _merge_parts(a_ref[...], zz, gg_ref[...], abn_ref[...], sbn_ref[...])
        mb = (sga * an + sgs * sn).astype(BF)
        mg_o[...] = mb
        h1_o[...] = _seq_tile(x_ref, m_ref) + _dot(mb, wo_ref[...])

    return pl.pallas_call(
        body, name="mid_fwd", grid=(lp // TM,),
        in_specs=[_rows(TM, D), _rows(TM, D), _rows(TM, 2 * D)] + _seq_specs() + [_full((D, 2 * D)), _full((D, D)),
                  _full((1, D)), _full((1, D))],
        out_specs=[_rows(TM, D), _rows(TM, 2 * D), _rows(TM, D), _rows(TM, D)],
        out_shape=[jax.ShapeDtypeStruct((lp, D), BF), jax.ShapeDtypeStruct((lp, 2 * D), BF),
                   jax.ShapeDtypeStruct((lp, D), BF), jax.ShapeDtypeStruct((lp, D), F32)],
        compiler_params=_cp(("arbitrary",)),
    )(y, attn, gg, x, mpad, w_glu, w_out, abn, sbn)


def mid_bwd(dh1b, y, attn, gg, zzb, w_glu, w_out, abn, sbn):
    lp = y.shape[0]

    def body(dh_ref, y_ref, a_ref, gg_ref, zz_ref, wg_ref, wo_ref, abn_ref, sbn_ref,
             dy_o, dattn_o, dgg_o, dzz_o, dabn_o, dsbn_o):
        i = pl.program_id(0)

        @pl.when(i == 0)
        def _():
            dabn_o[...] = jnp.zeros_like(dabn_o)
            dsbn_o[...] = jnp.zeros_like(dsbn_o)

        dm = _dot_nt(dh_ref[...], wo_ref[...])
        abn, sbn = abn_ref[...], sbn_ref[...]
        za, sgz, ra, rs, ah, sh, sga, sgs, an, sn = _merge_parts(
            a_ref[...], zz_ref[...].astype(F32), gg_ref[...], abn, sbn)
        dgg_o[:, 0:D] = (dm * an * sga * (1.0 - sga)).astype(BF)
        dgg_o[:, D:2 * D] = (dm * sn * sgs * (1.0 - sgs)).astype(BF)
        da = dm * sga
        dabn_o[...] += jnp.sum(da * ah, axis=0, keepdims=True)
        dah = da * abn
        dattn_o[...] = (ra * (dah - ah * jnp.mean(dah * ah, axis=-1, keepdims=True))).astype(BF)
        ds = dm * sgs
        dsbn_o[...] += jnp.sum(ds * sh, axis=0, keepdims=True)
        dsh = ds * sbn
        dssm = rs * (dsh - sh * jnp.mean(dsh * sh, axis=-1, keepdims=True))
        dzz_o[:, 0:D] = (dssm * sgz).astype(BF)
        dzz_o[:, D:2 * D] = (dssm * za * sgz * (1.0 - sgz)).astype(BF)
        dz = _dot_nt(dzz_o[...], wg_ref[...])
        yv = y_ref[...]
        _, t = _gelu(yv)
        dy_o[...] = dz * _gelu_grad(yv, t)

    return pl.pallas_call(
        body, name="mid_bwd", grid=(lp // TM,),
        in_specs=[_rows(TM, D), _rows(TM, D), _rows(TM, D), _rows(TM, 2 * D), _rows(TM, 2 * D),
                  _full((D, 2 * D)), _full((D, D)), _full((1, D)), _full((1, D))],
        out_specs=[_rows(TM, D), _rows(TM, D), _rows(TM, 2 * D), _rows(TM, 2 * D), _full((1, D)), _full((1, D))],
        out_shape=[jax.ShapeDtypeStruct((lp, D), F32), jax.ShapeDtypeStruct((lp, D), BF),
                   jax.ShapeDtypeStruct((lp, 2 * D), BF), jax.ShapeDtypeStruct((lp, 2 * D), BF),
                   jax.ShapeDtypeStruct((1, D), F32), jax.ShapeDtypeStruct((1, D), F32)],
        compiler_params=_cp(("arbitrary",)),
    )(dh1b, y, attn, gg, zzb, w_glu, w_out, abn, sbn)


def ffn_fwd_bwd(h1, tgt, w_fi, w_fo, gf):
    lp = h1.shape[0]
    tf = 256
    bounds = (0, 1536, DFF)

    def body(h_ref, t_ref, wi_ref, wo_ref, gf_ref, hn_o, dgu_o, act_o, dh2_o, dh1_o, dh1b_o, dgf_o, loss_o,
             gate_s, up_s):
        i = pl.program_id(0)

        @pl.when(i == 0)
        def _():
            dgf_o[...] = jnp.zeros_like(dgf_o)
            loss_o[...] = jnp.zeros_like(loss_o)

        h = h_ref[...]
        r = lax.rsqrt(jnp.mean(h * h, axis=-1, keepdims=True) + EPS)
        xh = h * r
        hb = (xh * gf_ref[...]).astype(BF)
        hn_o[...] = hb
        h2 = h
        for c0, c1 in zip(bounds[:-1], bounds[1:]):
            gate = _dot(hb, wi_ref[:, c0:c1])
            up = _dot(hb, wi_ref[:, DFF + c0:DFF + c1])
            gate_s[:, c0:c1] = gate
            up_s[:, c0:c1] = up
            ab = (gate * _sigmoid(gate) * up).astype(BF)
            act_o[:, c0:c1] = ab
            h2 = h2 + _dot(ab, wo_ref[c0:c1, :])
        real = (i >= PAD // tf).astype(F32)
        e = (h2 - t_ref[...]) * real
        loss_o[...] += 0.5 * jnp.sum(jnp.sum(e * e, axis=-1, keepdims=True), axis=0, keepdims=True) * (1.0 / D)
        dh2 = e * (1.0 / D)
        db = dh2.astype(BF)
        dh2_o[...] = db
        dhn = jnp.zeros((tf, D), F32)
        for c0, c1 in zip(bounds[:-1], bounds[1:]):
            gate = gate_s[:, c0:c1]
            up = up_s[:, c0:c1]
            sg = _sigmoid(gate)
            dact = _dot_nt(db, wo_ref[c0:c1, :])
            dgate = (dact * up * (sg * (1.0 + gate * (1.0 - sg)))).astype(BF)
            dup = (dact * (gate * sg)).astype(BF)
            dgu_o[:, c0:c1] = dgate
            dgu_o[:, DFF + c0:DFF + c1] = dup
            dhn += _dot_nt(dgate, wi_ref[:, c0:c1]) + _dot_nt(dup, wi_ref[:, DFF + c0:DFF + c1])
        dgf_o[...] += jnp.sum(dhn * xh, axis=0, keepdims=True)
        dxh = dhn * gf_ref[...]
        dh1 = dh2 + r * (dxh - xh * jnp.mean(dxh * xh, axis=-1, keepdims=True))
        dh1_o[...] = dh1
        dh1b_o[...] = dh1.astype(BF)

    return pl.pallas_call(
        body, name="ffn_fwd_bwd", grid=(lp // tf,),
        in_specs=[_rows(tf, D), pl.BlockSpec((tf, D), lambda i: (jnp.maximum(i - PAD // tf, 0), 0)),
                  _const((D, 2 * DFF)), _const((DFF, D)), _full((1, D))],
        out_specs=[_rows(tf, D), _rows(tf, 2 * DFF), _rows(tf, DFF), _rows(tf, D), _rows(tf, D), _rows(tf, D),
                   _full((1, D)), _full((1, 1))],
        out_shape=[jax.ShapeDtypeStruct((lp, D), BF), jax.ShapeDtypeStruct((lp, 2 * DFF), BF),
                   jax.ShapeDtypeStruct((lp, DFF), BF), jax.ShapeDtypeStruct((lp, D), BF),
                   jax.ShapeDtypeStruct((lp, D), F32), jax.ShapeDtypeStruct((lp, D), BF),
                   jax.ShapeDtypeStruct((1, D), F32), jax.ShapeDtypeStruct((1, 1), F32)],
        scratch_shapes=[pltpu.VMEM((tf, DFF), F32), pltpu.VMEM((tf, DFF), F32)],
        compiler_params=_cp(("arbitrary",), vmem_mb=58),
    )(h1, tgt, w_fi, w_fo, gf)


def weight_grad(name, a, b, tn, with_bf16=False):
    lp, k = a.shape
    n = b.shape[1]
    tr = next(t for t in (2112, 1056, 768, 512, 384, 256, 128) if lp % t == 0 and t * k * 2 <= (5 << 20))

    def body(a_ref, b_ref, o_ref, *ob_ref):
        @pl.when(pl.program_id(1) == 0)
        def _():
            o_ref[...] = jnp.zeros_like(o_ref)

        o_ref[...] += _dot_tn(a_ref[...], b_ref[...])
        if with_bf16:
            @pl.when(pl.program_id(1) == lp // tr - 1)
            def _():
                ob_ref[0][...] = o_ref[...].astype(BF)

    spec = pl.BlockSpec((k, tn), lambda j, m: (0, j))
    return pl.pallas_call(
        body, name=name, grid=(n // tn, lp // tr),
        in_specs=[pl.BlockSpec((tr, k), lambda j, m: (m, 0)), pl.BlockSpec((tr, tn), lambda j, m: (m, j))],
        out_specs=[spec, spec] if with_bf16 else spec,
        out_shape=([jax.ShapeDtypeStruct((k, n), F32), jax.ShapeDtypeStruct((k, n), BF)] if with_bf16
                   else jax.ShapeDtypeStruct((k, n), F32)),
        compiler_params=_cp(("arbitrary", "arbitrary")),
    )(a, b)


def _adamw_math(w, g, m, v):
    m = B1 * m + (1.0 - B1) * g
    v = B2 * v + (1.0 - B2) * (g * g)
    delta = -LR * ((m / BC1) / (jnp.sqrt(v / BC2) + AEPS) + WD * w)
    return delta, m, v


def _row_tile(r):
    return next((t for t in (256, 128, 64, 32, 16, 8) if r % t == 0), r)


def adamw(name, w, g, m, v):
    r, c = w.shape
    tr = _row_tile(r)

    def body(w_ref, g_ref, m_ref, v_ref, d_o, m_o, v_o):
        d_o[...], m_o[...], v_o[...] = _adamw_math(w_ref[...], g_ref[...], m_ref[...], v_ref[...])

    spec = pl.BlockSpec((tr, c), lambda i: (i, 0))
    return pl.pallas_call(
        body, name=name, grid=(r // tr,), in_specs=[spec] * 4, out_specs=[spec] * 3,
        out_shape=[jax.ShapeDtypeStruct((r, c), F32)] * 3,
        compiler_params=_cp(("arbitrary",)),
    )(w, g, m, v)


def adamw_small(ws, gs, ms, vs):
    n = len(ws)
    steps = 8

    def spec(a):
        if a.ndim == 4:
            return pl.BlockSpec((1, a.shape[1] // steps) + a.shape[2:], lambda i: (0, i, 0, 0))
        nd = a.ndim
        return pl.BlockSpec(a.shape, lambda i: (0,) * nd)

    def body(*refs):
        w, g, m, v, d_o, m_o, v_o = (refs[k * n:(k + 1) * n] for k in range(7))
        for t in range(n):
            d_o[t][...], m_o[t][...], v_o[t][...] = _adamw_math(w[t][...], g[t][...], m[t][...], v[t][...])

    specs = [spec(a) for a in ws]
    res = pl.pallas_call(
        body, name="adamw_small", grid=(steps,), in_specs=specs * 4, out_specs=specs * 3,
        out_shape=[jax.ShapeDtypeStruct(a.shape, F32) for a in ws] * 3,
        compiler_params=_cp(("arbitrary",)),
    )(*ws, *gs, *ms, *vs)
    return res[:n], res[n:2 * n], res[2 * n:]


def pair_sum(name, place, full, got, ax):
    r, c = got.shape
    tr = _row_tile(r)

    def body(s_ref, a_ref, b_ref, o_ref):
        o_ref[...] = (a_ref[...] + b_ref[...]).astype(BF)

    if ax == 1:
        mine = pl.BlockSpec((tr, c), lambda i, s: (s[1] * (r // tr) + i, 0))
    else:
        mine = pl.BlockSpec((tr, c), lambda i, s: (i, s[1]))
    spec = pl.BlockSpec((tr, c), lambda i, s: (i, 0))
    return pl.pallas_call(
        body, name=name,
        grid_spec=pltpu.PrefetchScalarGridSpec(num_scalar_prefetch=1, grid=(r // tr,), in_specs=[mine, spec],
                                               out_specs=spec),
        out_shape=jax.ShapeDtypeStruct((r, c), BF),
        compiler_params=_cp(("arbitrary",)),
    )(place, full, got)


def sum4(name, place, parts, half, ax):
    _, r, c = parts.shape
    tr = _row_tile(r)

    def body(s_ref, p_ref, h_ref, o_ref):
        o_ref[...] = ((p_ref[0].astype(F32) + p_ref[1].astype(F32))
                      + (p_ref[2].astype(F32) + h_ref[...].astype(F32)))

    if ax == 1:
        own = pl.BlockSpec((tr, c), lambda i, s: (i, s[0]))
        out = pl.BlockSpec((tr, c), lambda i, s: (s[1] * (r // tr) + i, 0))
        shape = (2 * r, c)
    else:
        own = pl.BlockSpec((tr, c), lambda i, s: (s[0] * (r // tr) + i, 0))
        out = pl.BlockSpec((tr, c), lambda i, s: (i, s[1]))
        shape = (r, 2 * c)
    return pl.pallas_call(
        body, name=name,
        grid_spec=pltpu.PrefetchScalarGridSpec(
            num_scalar_prefetch=1, grid=(r // tr,),
            in_specs=[pl.BlockSpec((3, tr, c), lambda i, s: (0, i, 0)), own], out_specs=out),
        out_shape=jax.ShapeDtypeStruct(shape, F32),
        compiler_params=_cp(("arbitrary",)),
    )(place, parts, half)


def sum8(name, place, parts, full, ax):
    _, r, c = parts.shape
    tr = _row_tile(r)

    def body(s_ref, p_ref, g_ref, o_ref):
        p = [p_ref[k].astype(F32) for k in range(7)]
        o_ref[...] = ((p[0] + p[1]) + (p[2] + p[3])) + ((p[4] + p[5]) + (p[6] + g_ref[...]))

    if ax == 1:
        own = pl.BlockSpec((tr, c), lambda i, s: (s[1] * (r // tr) + i, s[0]))
        out = pl.BlockSpec((tr, c), lambda i, s: (s[1] * (r // tr) + i, 0))
        shape = (2 * r, c)
    else:
        own = pl.BlockSpec((tr, c), lambda i, s: (s[0] * (r // tr) + i, s[1]))
        out = pl.BlockSpec((tr, c), lambda i, s: (i, s[1]))
        shape = (r, 2 * c)
    return pl.pallas_call(
        body, name=name,
        grid_spec=pltpu.PrefetchScalarGridSpec(
            num_scalar_prefetch=1, grid=(r // tr,),
            in_specs=[pl.BlockSpec((7, tr, c), lambda i, s: (0, i, 0)), own], out_specs=out),
        out_shape=jax.ShapeDtypeStruct(shape, F32),
        compiler_params=_cp(("arbitrary",)),
    )(place, parts, full)


def _place():
    x, y, c = lax.axis_index("x"), lax.axis_index("y"), lax.axis_index("c")
    return x, y, c


def _other_chips(x, y):
    return [(1 - x, y), (x, 1 - y), (1 - x, 1 - y)]


SHARDED = (("w_in", D, IN_COLS, 1), ("w_glu", D, 2 * D, 1), ("w_out", D, D, 0),
           ("w_ffn_in", D, 2 * DFF, 1), ("w_ffn_out", DFF, D, 0))


def _shard_of(ref, axis, k, size):
    if axis == 0:
        return ref.at[pl.ds(k * size, size), :]
    return ref.at[:, pl.ds(k * size, size)]


def _gather_piece(full, dims, k, h):
    r, cc, ax = dims
    if ax == 0:
        return full.at[pl.ds(k * (r // 4) + h * (r // 8), r // 8), :]
    return full.at[pl.ds(h * (r // 2), r // 2), pl.ds(k * (cc // 4), cc // 4)]


def _rows_half(shard, h):
    rs = shard.shape[0]
    return shard.at[pl.ds(h * (rs // 2), rs // 2), :]


def gather_weights(shards, dims):
    items = [(s,) + tuple(d) for s, d in zip(shards, dims)]
    n = len(items)

    def body(*refs):
        ins, outs = refs[:n], refs[n:2 * n]
        ici_send, ici_recv, d2d_send, d2d_recv, local_sems = refs[2 * n:]
        x, y, c = _place()
        mine = 2 * x + y
        chips = _other_chips(x, y)

        def piece(t, k, h):
            return _gather_piece(outs[t], dims[t], k, h)

        def src_half(t, h):
            return _rows_half(ins[t], h)

        local, first, passed = [], [], []
        for t in range(n):
            _, r, cc, ax = items[t]
            size = (r if ax == 0 else cc) // 4
            loc = pltpu.make_async_remote_copy(
                src_ref=ins[t], dst_ref=_shard_of(outs[t], ax, mine, size), send_sem=local_sems.at[t],
                recv_sem=local_sems.at[n + t], device_id=(x, y, 1 - c), device_id_type=MESH)
            loc.start()
            local.append(loc)
            for j, chip in enumerate(chips):
                cp = pltpu.make_async_remote_copy(
                    src_ref=src_half(t, c), dst_ref=piece(t, mine, c),
                    send_sem=ici_send.at[3 * t + j], recv_sem=ici_recv.at[3 * t + j],
                    device_id=(*chip, c), device_id_type=MESH)
                cp.start()
                first.append(cp)
        for t in range(n):
            for j, (px, py) in enumerate(chips):
                got = piece(t, 2 * px + py, c)
                pltpu.make_async_remote_copy(
                    src_ref=src_half(t, c), dst_ref=got, send_sem=ici_send.at[3 * t + j],
                    recv_sem=ici_recv.at[3 * t + j], device_id=(px, py, c), device_id_type=MESH).wait_recv()
                cp = pltpu.make_async_remote_copy(
                    src_ref=got, dst_ref=got, send_sem=d2d_send.at[3 * t + j], recv_sem=d2d_recv.at[3 * t + j],
                    device_id=(x, y, 1 - c), device_id_type=MESH)
                cp.start()
                passed.append(cp)
        for t in range(n):
            for j, (px, py) in enumerate(chips):
                other = piece(t, 2 * px + py, 1 - c)
                pltpu.make_async_remote_copy(
                    src_ref=other, dst_ref=other, send_sem=d2d_send.at[3 * t + j], recv_sem=d2d_recv.at[3 * t + j],
                    device_id=(x, y, 1 - c), device_id_type=MESH).wait_recv()
        for cp in first + passed:
            cp.wait_send()
        for loc in local:
            loc.wait()

    return pl.pallas_call(
        body, name="gather_weights",
        in_specs=[ANY] * n, out_specs=[ANY] * n,
        out_shape=[jax.ShapeDtypeStruct((r, c), s.dtype) for s, r, c, _ in items],
        scratch_shapes=[pltpu.SemaphoreType.DMA((3 * n,)), pltpu.SemaphoreType.DMA((3 * n,)),
                        pltpu.SemaphoreType.DMA((3 * n,)), pltpu.SemaphoreType.DMA((3 * n,)),
                        pltpu.SemaphoreType.DMA((2 * n,))],
        compiler_params=pltpu.CompilerParams(has_side_effects=True),
    )(*shards)


SEM = pl.BlockSpec(memory_space=pltpu.SEMAPHORE)
HBM = pl.BlockSpec(memory_space=pltpu.HBM)
EFFECT = pltpu.SideEffectType.DATAFLOW_SIDE_EFFECTING


def _in_hbm(a):
    return pltpu.with_memory_space_constraint(a, pltpu.HBM)


def _split_copy_start(name, copies, ncopies, srcs, lands, after):
    ns, nl = len(srcs), len(lands)

    def body(*refs):
        ins, land_refs = refs[:ns], refs[ns:ns + nl]
        send_sems, recv_sems = refs[ns + nl + 1], refs[ns + nl + 2]
        token = refs[-1]
        for k, (src, dst, dev) in enumerate(copies(ins, land_refs)):
            pltpu.make_async_remote_copy(src_ref=src, dst_ref=dst, send_sem=send_sems.at[k],
                                         recv_sem=recv_sems.at[k], device_id=dev, device_id_type=MESH).start()
        token[...] = jnp.zeros_like(token)

    res = pl.pallas_call(
        body, name=name,
        in_specs=[HBM] * (ns + nl) + [ANY],
        out_specs=(SEM, SEM) + (HBM,) * (ns + nl) + (pl.BlockSpec(memory_space=pltpu.VMEM),),
        out_shape=(pltpu.SemaphoreType.DMA((ncopies,)), pltpu.SemaphoreType.DMA((ncopies,)))
        + tuple(pltpu.HBM(a.shape, a.dtype) for a in list(srcs) + list(lands))
        + (jax.ShapeDtypeStruct((8, 128), F32),),
        input_output_aliases={t: 2 + t for t in range(ns + nl)},
        compiler_params=pltpu.CompilerParams(has_side_effects=EFFECT),
    )(*[_in_hbm(a) for a in srcs], *[_in_hbm(a) for a in lands], after)
    return res[0], res[1], list(res[2:2 + ns]), list(res[2 + ns:2 + ns + nl]), res[-1]


def _split_copy_wait(name, copies, arrivals, send_sems, recv_sems, srcs, lands, after):
    ns, nl = len(srcs), len(lands)

    def body(*refs):
        ins, land_refs = refs[:ns], refs[ns:ns + nl]
        send_sems_ref, recv_sems_ref = refs[ns + nl], refs[ns + nl + 1]
        mine = copies(ins, land_refs)
        for k, ((src, dst, dev), got) in enumerate(zip(mine, arrivals(ins, land_refs))):
            pltpu.make_async_remote_copy(src_ref=src, dst_ref=dst, send_sem=send_sems_ref.at[k],
                                         recv_sem=recv_sems_ref.at[k], device_id=dev,
                                         device_id_type=MESH).wait_send()
            pltpu.make_async_remote_copy(src_ref=got, dst_ref=got, send_sem=send_sems_ref.at[k],
                                         recv_sem=recv_sems_ref.at[k], device_id=dev,
                                         device_id_type=MESH).wait_recv()

    res = pl.pallas_call(
        body, name=name,
        in_specs=[HBM] * (ns + nl) + [SEM, SEM, ANY],
        out_specs=(HBM,) * (ns + nl),
        out_shape=tuple(pltpu.HBM(a.shape, a.dtype) for a in list(srcs) + list(lands)),
        input_output_aliases={t: t for t in range(ns + nl)},
        compiler_params=pltpu.CompilerParams(has_side_effects=EFFECT),
    )(*srcs, *lands, send_sems, recv_sems, after)
    return list(res[:ns]), list(res[ns:])


def _gather_copies(dims):
    def copies(ins, lands):
        x, y, c = _place()
        mine = 2 * x + y
        out = []
        for t, d in enumerate(dims):
            size = (d[0] if d[2] == 0 else d[1]) // 4
            for px, py in _other_chips(x, y):
                for dc in range(2):
                    out.append((_rows_half(ins[t], c), _gather_piece(lands[t], d, mine, c), (px, py, c ^ dc)))
            out.append((ins[t], _shard_of(lands[t], d[2], mine, size), (x, y, 1 - c)))
        return out

    def arrivals(ins, lands):
        x, y, c = _place()
        mine = 2 * x + y
        out = []
        for t, d in enumerate(dims):
            size = (d[0] if d[2] == 0 else d[1]) // 4
            for px, py in _other_chips(x, y):
                for dc in range(2):
                    out.append(_gather_piece(lands[t], d, 2 * px + py, c ^ dc))
            out.append(_shard_of(lands[t], d[2], mine, size))
        return out

    return copies, arrivals


def _half_of(ref, ax, h):
    r, c = ref.shape
    if ax == 1:
        return ref.at[pl.ds(h * (r // 2), r // 2), :]
    return ref.at[:, pl.ds(h * (c // 2), c // 2)]


def _half_shape(r, c, ax):
    return (r // 2, c) if ax == 1 else (r, c // 2)


def pair_exchange(name, grads, idxs):
    n = len(grads)
    which = [SHARDED[i] for i in idxs]

    def body(*refs):
        ins, got = refs[:n], refs[n:2 * n]
        send_sems, recv_sems = refs[2 * n:]
        x, y, c = _place()
        cps = []
        for t, (_, _, _, ax) in enumerate(which):
            cp = pltpu.make_async_remote_copy(
                src_ref=_half_of(ins[t], ax, 1 - c), dst_ref=got[t], send_sem=send_sems.at[t],
                recv_sem=recv_sems.at[t], device_id=(x, y, 1 - c), device_id_type=MESH)
            cp.start()
            cps.append(cp)
        for cp in cps:
            cp.wait()

    return pl.pallas_call(
        body, name=name,
        in_specs=[ANY] * n, out_specs=[ANY] * n,
        out_shape=[jax.ShapeDtypeStruct(_half_shape(r, c, ax), F32) for _, r, c, ax in which],
        scratch_shapes=[pltpu.SemaphoreType.DMA((n,)), pltpu.SemaphoreType.DMA((n,))],
        compiler_params=pltpu.CompilerParams(has_side_effects=True),
    )(*grads)


def _piece_shapes(idxs):
    out = []
    for _, r, c, ax in [SHARDED[i] for i in idxs]:
        hr, hc = _half_shape(r, c, ax)
        out.append((hr // 4, hc) if ax == 0 else (hr, hc // 4))
    return out


def _piece_of(full, ax, k, h):
    r, c = full.shape
    if ax == 1:
        return full.at[pl.ds(h * (r // 2), r // 2), pl.ds(k * (c // 4), c // 4)]
    return full.at[pl.ds(k * (r // 4), r // 4), pl.ds(h * (c // 2), c // 2)]


def _scatter8_copies(idxs):
    which = [SHARDED[i] for i in idxs]

    def copies(ins, lands):
        x, y, c = _place()
        out = []
        for t, (_, _, _, ax) in enumerate(which):
            for j, (px, py) in enumerate(_other_chips(x, y)):
                for d in range(2):
                    out.append((_piece_of(ins[t], ax, 2 * px + py, c ^ d), lands[t].at[2 * j + d], (px, py, c ^ d)))
            out.append((_piece_of(ins[t], ax, 2 * x + y, 1 - c), lands[t].at[6], (x, y, 1 - c)))
        return out

    def arrivals(ins, lands):
        return [lands[t].at[k] for t in range(len(which)) for k in range(7)]

    return copies, arrivals


def scatter_grads(halves, idxs):
    n = len(halves)
    which = [SHARDED[i] for i in idxs]

    def shapes():
        return _piece_shapes(idxs)

    def body(*refs):
        ins, outs = refs[:n], refs[n:2 * n]
        send_sems, recv_sems = refs[2 * n:]
        x, y, c = _place()
        started = []
        for t, (src, dst, (_, _, _, ax)) in enumerate(zip(ins, outs, which)):
            size = src.shape[ax] // 4
            for j, (px, py) in enumerate(_other_chips(x, y)):
                cp = pltpu.make_async_remote_copy(
                    src_ref=_shard_of(src, ax, 2 * px + py, size), dst_ref=dst.at[j],
                    send_sem=send_sems.at[3 * t + j], recv_sem=recv_sems.at[3 * t + j],
                    device_id=(px, py, c), device_id_type=MESH)
                cp.start()
                started.append(cp)
        for cp in started:
            cp.wait()

    return pl.pallas_call(
        body, name="scatter_grads",
        in_specs=[ANY] * n, out_specs=[ANY] * n,
        out_shape=[jax.ShapeDtypeStruct((3,) + s, BF) for s in shapes()],
        scratch_shapes=[pltpu.SemaphoreType.DMA((3 * n,)), pltpu.SemaphoreType.DMA((3 * n,))],
        compiler_params=pltpu.CompilerParams(has_side_effects=True),
    )(*halves)


def join_halves(name, shards, idxs):
    n = len(shards)
    which = [SHARDED[i] for i in idxs]

    def body(*refs):
        ins, outs = refs[:n], refs[n:2 * n]
        send_sems, recv_sems = refs[2 * n:]
        x, y, c = _place()
        cps = []
        for t, (_, _, _, ax) in enumerate(which):
            cp = pltpu.make_async_remote_copy(
                src_ref=_half_of(ins[t], ax, c), dst_ref=_half_of(outs[t], ax, c), send_sem=send_sems.at[t],
                recv_sem=recv_sems.at[t], device_id=(x, y, 1 - c), device_id_type=MESH)
            cp.start()
            cps.append(cp)
        for t, cp in enumerate(cps):
            cp.wait_send()
            theirs = _half_of(outs[t], which[t][3], 1 - c)
            pltpu.make_async_remote_copy(
                src_ref=theirs, dst_ref=theirs, send_sem=send_sems.at[t], recv_sem=recv_sems.at[t],
                device_id=(x, y, 1 - c), device_id_type=MESH).wait_recv()

    return pl.pallas_call(
        body, name=name,
        in_specs=[ANY] * n, out_specs=[ANY] * n,
        out_shape=[jax.ShapeDtypeStruct(s.shape, F32) for s in shards],
        input_output_aliases={t: t for t in range(n)},
        scratch_shapes=[pltpu.SemaphoreType.DMA((n,)), pltpu.SemaphoreType.DMA((n,))],
        compiler_params=pltpu.CompilerParams(has_side_effects=True),
    )(*shards)


def allreduce_small(pack):
    rows = pack.shape[0]
    hr = rows // 2

    def body(p_ref, o_ref, sib_ref, parts_ref, send_sems, recv_sems):
        x, y, c = _place()
        mine = 2 * x + y
        sibling = (x, y, 1 - c)
        swap = pltpu.make_async_remote_copy(
            src_ref=p_ref, dst_ref=sib_ref, send_sem=send_sems.at[0], recv_sem=recv_sems.at[0],
            device_id=sibling, device_id_type=MESH)
        swap.start()
        swap.wait()
        half = pl.ds(pl.multiple_of(c * hr, 8), hr)
        parts_ref[mine] = p_ref[half, :] + sib_ref[half, :]
        cps = []
        for j, chip in enumerate(_other_chips(x, y)):
            cp = pltpu.make_async_remote_copy(
                src_ref=parts_ref.at[mine], dst_ref=parts_ref.at[mine], send_sem=send_sems.at[1 + j],
                recv_sem=recv_sems.at[1 + j], device_id=(*chip, c), device_id_type=MESH)
            cp.start()
            cps.append(cp)
        for j, (px, py) in enumerate(_other_chips(x, y)):
            cps[j].wait_send()
            theirs = parts_ref.at[2 * px + py]
            pltpu.make_async_remote_copy(
                src_ref=theirs, dst_ref=theirs, send_sem=send_sems.at[1 + j], recv_sem=recv_sems.at[1 + j],
                device_id=(px, py, c), device_id_type=MESH).wait_recv()
        o_ref[half, :] = (parts_ref[0] + parts_ref[1]) + (parts_ref[2] + parts_ref[3])
        back = pltpu.make_async_remote_copy(
            src_ref=o_ref.at[half, :], dst_ref=o_ref.at[half, :], send_sem=send_sems.at[4],
            recv_sem=recv_sems.at[4], device_id=sibling, device_id_type=MESH)
        back.start()
        back.wait_send()
        other = pl.ds(pl.multiple_of((1 - c) * hr, 8), hr)
        pltpu.make_async_remote_copy(
            src_ref=o_ref.at[other, :], dst_ref=o_ref.at[other, :], send_sem=send_sems.at[4],
            recv_sem=recv_sems.at[4], device_id=sibling, device_id_type=MESH).wait_recv()

    return pl.pallas_call(
        body, name="allreduce_small",
        in_specs=[pl.BlockSpec(memory_space=pltpu.VMEM)], out_specs=pl.BlockSpec(memory_space=pltpu.VMEM),
        out_shape=jax.ShapeDtypeStruct((rows, 128), F32),
        scratch_shapes=[pltpu.VMEM((rows, 128), F32), pltpu.VMEM((4, hr, 128), F32),
                        pltpu.SemaphoreType.DMA((5,)), pltpu.SemaphoreType.DMA((5,))],
        compiler_params=pltpu.CompilerParams(has_side_effects=True, vmem_limit_bytes=40 << 20),
    )(pack)


def _ssm_discretize(lam_re, lam_im, log_dt, b_re, b_im):
    dt = jnp.exp(log_dt)[:, None]
    mag = jnp.exp(lam_re * dt)
    ar, ai = mag * jnp.cos(lam_im * dt), mag * jnp.sin(lam_im * dt)
    den = lam_re * lam_re + lam_im * lam_im
    nr, ni = ar - 1.0, ai
    fr, fi = (nr * lam_re + ni * lam_im) / den, (ni * lam_re - nr * lam_im) / den
    bbr = fr[..., None] * b_re - fi[..., None] * b_im
    bbi = fr[..., None] * b_im + fi[..., None] * b_re
    return ar, ai, bbr, bbi


def _cmul(a, b):
    return a[0] * b[0] - a[1] * b[1], a[0] * b[1] + a[1] * b[0]


def _scan_powers(ar, ai):
    a = (ar.reshape(1, SW), ai.reshape(1, SW))
    big = a
    for _ in range(SEG.bit_length() - 1):
        big = _cmul(big, big)
    assert 1 << (SEG.bit_length() - 1) == SEG
    pows = {1: big}
    pows[2] = _cmul(big, big)
    pows[4] = _cmul(pows[2], pows[2])
    rid = jnp.arange(8)[:, None]
    ones = jnp.ones((8, 1), F32)
    fwd, rev = [ones * a[0], ones * a[1]], [ones * a[0], -ones * a[1]]
    for k in (1, 2, 4):
        pr, pi = pows[k]
        fwd += [jnp.where(rid >= k, pr, 0.0), jnp.where(rid >= k, pi, 0.0)]
        rev += [jnp.where(rid < 8 - k, pr, 0.0), jnp.where(rid < 8 - k, -pi, 0.0)]
    return jnp.concatenate(fwd + rev, axis=0)


def _pack_b(bb):
    t = bb.reshape(NLC, 8, NSTATE, GCH).transpose(0, 1, 3, 2)
    return jnp.einsum("jgcp,gh->jgchp", t, jnp.eye(8, dtype=bb.dtype)).reshape(NLC, 128, LC)


def _unpack_b(db):
    t = jnp.einsum("jgchp,gh->jgcp", db.reshape(NLC, 8, GCH, 8, NSTATE), jnp.eye(8, dtype=db.dtype))
    return t.transpose(0, 1, 3, 2).reshape(GROUPS, NSTATE, GCH)


def _pack_c(cc):
    t = cc.reshape(NLC, 8, GCH, NSTATE).transpose(0, 1, 3, 2)
    return jnp.einsum("jgpc,gh->jgphc", t, jnp.eye(8, dtype=cc.dtype)).reshape(NLC, LC, 128)


def _unpack_c(dc):
    t = jnp.einsum("jgphc,gh->jgpc", dc.reshape(NLC, 8, NSTATE, 8, GCH), jnp.eye(8, dtype=dc.dtype))
    return t.transpose(0, 1, 3, 2).reshape(GROUPS, GCH, NSTATE)


SMALL = ("norm_mix", "q_norm", "k_norm", "attn_sinks", "lam_re", "lam_im", "log_dt", "ssm_b_re", "ssm_b_im",
         "ssm_c_re", "ssm_c_im", "ssm_d", "attn_branch_norm", "ssm_branch_norm", "norm_ffn")


def _pack_small(arrs, rows=None):
    flat = jnp.concatenate([a.reshape(-1).astype(F32) for a in arrs])
    n = flat.shape[0]
    if rows is None:
        rows = -(-n // (128 * 256)) * 256
    return jnp.pad(flat, (0, rows * 128 - n)).reshape(rows, 128)


def _unpack_small(pack, like):
    flat = pack.reshape(-1)
    out, off = [], 0
    for a in like:
        out.append(flat[off:off + a.size].reshape(a.shape))
        off += a.size
    return out


def _tie(a, token):
    return a if token is None else a + token[0, 0]


class _NoExchange:
    def __init__(self, w):
        self.w = w
        self.grads = {}

    def begin(self, after):
        return None

    def rest_weights(self, after):
        return self.w

    def start(self, group, idxs, grads):
        self.grads.update(zip(idxs, [g for g, _ in grads]))
        return None

    def finish(self, group, after):
        pass


class _Exchange:
    def __init__(self, rest_shards, place):
        self.rest_shards = rest_shards
        self.place = place
        self.rest_dims = [s[1:] for s in SHARDED[1:]]
        self.pending = {}
        self.halves = {}
        self.full = {}
        self.parts = {}

    def begin(self, after):
        copies, arrivals = _gather_copies(self.rest_dims)
        lands = [lax.empty((r, c), BF) for r, c, _ in self.rest_dims]
        send, recv, srcs, lands, token = _split_copy_start(
            "gather_start", copies, 7 * len(lands), self.rest_shards, lands, after)
        self.pending["gather"] = (copies, arrivals, send, recv, srcs, lands)
        return token

    def rest_weights(self, after):
        copies, arrivals, send, recv, srcs, lands = self.pending.pop("gather")
        _, fulls = _split_copy_wait("gather_wait", copies, arrivals, send, recv, srcs, lands, after)
        return dict(zip([s[0] for s in SHARDED[1:]], fulls))

    def _halves(self, group, idxs, grads):
        got = pair_exchange("pair_exchange_" + group, grads, idxs)
        return [pair_sum("pair_sum_" + SHARDED[i][0], self.place, g, b, SHARDED[i][3])
                for i, g, b in zip(idxs, grads, got)]

    def start(self, group, idxs, grads):
        if grads[0][1] is None:
            halves = self._halves(group, idxs, [g for g, _ in grads])
            for i, h, pt in zip(idxs, halves, scatter_grads(halves, idxs)):
                self.halves[i], self.parts[i] = h, pt
            return None
        copies, arrivals = _scatter8_copies(idxs)
        lands = [lax.empty((7,) + sh, BF) for sh in _piece_shapes(idxs)]
        send, recv, srcs, lands, token = _split_copy_start(
            "scatter_start_" + group, copies, 7 * len(idxs), [g16 for _, g16 in grads], lands, grads[0][1])
        self.pending[group] = (copies, arrivals, send, recv, srcs, lands, idxs)
        for i, (g32, _) in zip(idxs, grads):
            self.full[i] = g32
        return token

    def finish(self, group, after):
        if group not in self.pending:
            return
        copies, arrivals, send, recv, srcs, lands, idxs = self.pending.pop(group)
        _, lands = _split_copy_wait("scatter_wait_" + group, copies, arrivals, send, recv, srcs, lands, after)
        for i, pt in zip(idxs, lands):
            self.parts[i] = pt


def local_step(x, tgt, meta_full, p, w_in, ex):
    mpad = jnp.concatenate([jnp.zeros((META0, D), F32), meta_full], axis=0)
    qn_t = jnp.tile(p["q_norm"].reshape(1, HEAD), (1, NQ)) * HEAD ** -0.5
    kn_t = jnp.tile(p["k_norm"].reshape(1, HEAD), (1, NKV))
    seg = jnp.arange(256) // HEAD
    bd = (seg[:, None] == seg[None, :]).astype(BF)
    nm = p["norm_mix"].reshape(1, D)
    sinks = p["attn_sinks"].reshape(NQ)
    dsk = p["ssm_d"].reshape(1, D)
    abn, sbn, gf = (p[k].reshape(1, D) for k in ("attn_branch_norm", "ssm_branch_norm", "norm_ffn"))

    disc_in = (p["lam_re"], p["lam_im"], p["log_dt"], p["ssm_b_re"], p["ssm_b_im"])
    (ar, ai, bbr, bbi), disc_vjp = jax.vjp(_ssm_discretize, *disc_in)
    pw = _scan_powers(ar, ai)
    brp, bip = _pack_b(bbr).astype(BF), _pack_b(bbi).astype(BF)
    crp, cip = _pack_c(p["ssm_c_re"]).astype(BF), _pack_c(p["ssm_c_im"]).astype(BF)

    token = ex.begin(w_in)
    xn, qr, kr, qn, kn, v, u, gg = in_proj_fwd(x, mpad, w_in, _tie(nm, token), qn_t, kn_t, bd)
    attn = attention_fwd(sinks, qn, kn, v)
    perm = _segment_order()
    y, sr, si = ssm_fwd(u, pw, brp, bip, crp, cip, dsk, perm)
    w = ex.rest_weights(y)
    zb, zzb, mgb, h1 = mid_fwd(y, attn, gg, x, mpad, w["w_glu"], w["w_out"], abn, sbn)
    hnb, dgub, actb, dh2b, dh1, dh1b, dgf, loss = ffn_fwd_bwd(h1, tgt, w["w_ffn_in"], w["w_ffn_out"], gf)
    token = ex.start("ffn", (3, 4), [weight_grad("grad_w_ffn_in", hnb, dgub, 1408, True),
                                     weight_grad("grad_w_ffn_out", actb, dh2b, 512, True)])
    dy, dattn, dggb, dzzb, dabn, dsbn = mid_bwd(dh1b, y, attn, gg, zzb, w["w_glu"], w["w_out"], _tie(abn, token),
                                                sbn)
    grads_mid = [weight_grad("grad_w_glu", zb, dzzb, 1024, True), weight_grad("grad_w_out", mgb, dh1b, 1024, True)]
    dub, da, dbr, dbi, dcr, dci, dd = ssm_bwd(dy, u, sr, si, pw, brp, bip, crp, cip, dsk, perm)
    ex.finish("ffn", dub)
    token = ex.start("mid", (1, 2), grads_mid)
    dqn, dkn, dv, dsink = attention_bwd(_tie(sinks, token), qn, kn, v, dattn)
    dproj, dx, dmpad, dnm, dqw, dkw = in_proj_bwd(dqn, dkn, dv, dub, dggb, qr, kr, x, mpad, dh1, w_in, nm, qn_t,
                                                  kn_t, bd)
    ex.finish("mid", dproj)
    ex.start("in", (0,), [(weight_grad("grad_w_in", xn, dproj, 1536), None)])

    dar = jnp.sum(da[0:8], axis=0).reshape(GROUPS, NSTATE)
    dai = jnp.sum(da[8:16], axis=0).reshape(GROUPS, NSTATE)
    dlr, dli, dldt, dbre, dbim = disc_vjp((dar, dai, _unpack_b(dbr), _unpack_b(dbi)))
    small = {
        "norm_mix": dnm, "q_norm": dqw.reshape(NQ, HEAD).sum(0) * HEAD ** -0.5, "k_norm": dkw.reshape(NKV, HEAD).sum(0),
        "attn_sinks": dsink[0, 0:NQ], "lam_re": dlr, "lam_im": dli, "log_dt": dldt,
        "ssm_b_re": dbre, "ssm_b_im": dbim, "ssm_c_re": _unpack_c(jnp.swapaxes(dcr, 1, 2)), "ssm_c_im": _unpack_c(jnp.swapaxes(dci, 1, 2)),
        "ssm_d": dd, "attn_branch_norm": dabn, "ssm_branch_norm": dsbn, "norm_ffn": dgf,
    }
    return loss[0, 0], dx, dmpad[META0:PAD], small


def kernel(x, meta_tokens, norm_mix, w_in, q_norm, k_norm, attn_sinks, lam_re, lam_im, log_dt, ssm_b_re, ssm_b_im, ssm_c_re, ssm_c_im, ssm_d, w_glu, attn_branch_norm, ssm_branch_norm, w_out, norm_ffn, w_ffn_in, w_ffn_out, loss_target, m_meta_tokens, m_norm_mix, m_w_in, m_q_norm, m_k_norm, m_attn_sinks, m_lam_re, m_lam_im, m_log_dt, m_ssm_b_re, m_ssm_b_im, m_ssm_c_re, m_ssm_c_im, m_ssm_d, m_w_glu, m_attn_branch_norm, m_ssm_branch_norm, m_w_out, m_norm_ffn, m_w_ffn_in, m_w_ffn_out, v_meta_tokens, v_norm_mix, v_w_in, v_q_norm, v_k_norm, v_attn_sinks, v_lam_re, v_lam_im, v_log_dt, v_ssm_b_re, v_ssm_b_im, v_ssm_c_re, v_ssm_c_im, v_ssm_d, v_w_glu, v_attn_branch_norm, v_ssm_branch_norm, v_w_out, v_norm_ffn, v_w_ffn_in, v_w_ffn_out):
    args = dict(locals())
    names = ["meta_tokens", "norm_mix", "w_in", "q_norm", "k_norm", "attn_sinks", "lam_re", "lam_im", "log_dt",
             "ssm_b_re", "ssm_b_im", "ssm_c_re", "ssm_c_im", "ssm_d", "w_glu", "attn_branch_norm",
             "ssm_branch_norm", "w_out", "norm_ffn", "w_ffn_in", "w_ffn_out"]
    big_names = [s[0] for s in SHARDED]
    xs, ys, cs = _place()
    chip = 2 * xs + ys

    shards = [args[n][0].astype(BF) for n in big_names]
    w_in_full, meta_full = gather_weights([shards[0], meta_tokens], [SHARDED[0][1:], (N_META, D, 1)])
    place = jnp.stack([chip, cs]).astype(jnp.int32)
    ex = _Exchange(shards[1:], place)

    p = {n: args[n][0] for n in SMALL}
    loss, dx, dmeta, gsmall = local_step(x[0], loss_target[0], meta_full, p, w_in_full, ex)

    out = {}

    def finish(idxs, name):
        mine = [sum8("sum8_" + SHARDED[i][0], place, ex.parts[i], ex.full[i], SHARDED[i][3]) if i in ex.full
                else sum4("sum4_" + SHARDED[i][0], place, ex.parts[i], ex.halves[i], SHARDED[i][3]) for i in idxs]
        for i, g in zip(idxs, join_halves(name, mine, idxs)):
            n = SHARDED[i][0]
            d, nm_, nv_ = adamw("adamw_" + n, args[n][0], g, args["m_" + n][0], args["v_" + n][0])
            out[n] = tuple(t[None] for t in (g, d, nm_, nv_))

    finish((0, 1, 2, 3, 4), "join_halves")
    small_list = [gsmall[n].reshape(args[n].shape) for n in SMALL] + [dmeta, loss.reshape(1, 1)]
    red = allreduce_small(_pack_small(small_list))
    *small_red, dmeta_sum, loss = _unpack_small(red, small_list)
    loss = loss[0, 0]
    gmeta = lax.dynamic_slice_in_dim(dmeta_sum, chip * (D // 4), D // 4, axis=1)
    d, nm_, nv_ = adamw("adamw_meta", meta_tokens, gmeta, m_meta_tokens, v_meta_tokens)
    out["meta_tokens"] = (gmeta, d, nm_, nv_)
    ds, nms, nvs = adamw_small([args[n] for n in SMALL], small_red, [args["m_" + n] for n in SMALL],
                               [args["v_" + n] for n in SMALL])
    for n, g_, d_, m_, v_ in zip(SMALL, small_red, ds, nms, nvs):
        out[n] = (g_, d_, m_, v_)

    res = [loss, dx[None]]
    for k in range(4):
        res += [out[n][k] for n in names]
    return tuple(res)
```

```python
import functools
import math

import jax
import jax.numpy as jnp
from jax import lax
from jax.experimental import pallas as pl
from jax.experimental.pallas import tpu as pltpu

F32 = jnp.float32
BF = jnp.bfloat16

D = 1024
N_META = 16
HEAD = 64
NQ = 16
NKV = 4
QW = NQ * HEAD
KVW = NKV * HEAD
WIN = 128
GROUPS = 64
GCH = 16
NSTATE = 64
SW = GROUPS * NSTATE
DFF = 2816
IN_COLS = QW + 2 * KVW + D + 2 * D
PAD = 256
META0 = PAD - N_META
EPS = 1e-6
NEG = -1e30

TM = 256
TS = 256
LC = 512
NLC = SW // LC

LR, B1, B2, AEPS, WD, STEP = 0.001, 0.9, 0.999, 1e-08, 0.01, 10
BC1 = 1.0 - B1 ** STEP
BC2 = 1.0 - B2 ** STEP

MESH = pl.DeviceIdType.MESH
ANY = pl.BlockSpec(memory_space=pl.ANY)


def _cp(sem=None, vmem_mb=48):
    return pltpu.CompilerParams(dimension_semantics=sem, vmem_limit_bytes=vmem_mb << 20)


def _full(shape):
    n = len(shape)
    return pl.BlockSpec(shape, lambda *a: (0,) * n)


def _const(shape):
    n = len(shape)
    return pl.BlockSpec(shape, lambda *a: (0,) * n, pipeline_mode=pl.Buffered(1))


def _rows(tm, width):
    return pl.BlockSpec((tm, width), lambda i: (i, 0))


def _dot(a, b):
    return jnp.dot(a, b, preferred_element_type=F32)


def _dot_nt(a, b):
    return lax.dot_general(a, b, (((1,), (1,)), ((), ())), preferred_element_type=F32)


def _dot_tn(a, b):
    return lax.dot_general(a, b, (((0,), (0,)), ((), ())), preferred_element_type=F32)


def _sigmoid(x):
    return 0.5 * jnp.tanh(0.5 * x) + 0.5


def _seg_mean(x, bd):
    outs = []
    for j in range(x.shape[1] // 256):
        c = x[:, 256 * j:256 * (j + 1)]
        hi = c.astype(BF)
        lo = (c - hi.astype(F32)).astype(BF)
        outs.append(_dot(hi, bd) + _dot(lo, bd))
    out = outs[0] if len(outs) == 1 else jnp.concatenate(outs, axis=1)
    return out * (1.0 / HEAD)


_GC = math.sqrt(2.0 / math.pi)


def _gelu(x):
    t = jnp.tanh(_GC * (x + 0.044715 * x * x * x))
    return 0.5 * x * (1.0 + t), t


def _gelu_grad(x, t):
    return 0.5 * (1.0 + t) + 0.5 * x * (1.0 - t * t) * _GC * (1.0 + 3.0 * 0.044715 * x * x)


def _seq_specs():
    assert TM == PAD
    return [pl.BlockSpec((TM, D), lambda i: (jnp.maximum(i - 1, 0), 0)), _full((PAD, D))]


def _seq_tile(x_ref, m_ref):
    return jnp.where(pl.program_id(0) == 0, m_ref[...], x_ref[...])


def in_proj_fwd(x, mpad, w_in, nm, qn_t, kn_t, bd):
    lp = x.shape[0] + PAD

    def body(x_ref, m_ref, w_ref, nm_ref, qn_ref, kn_ref, bd_ref,
             xn_o, qr_o, kr_o, qn_o, kn_o, v_o, u_o, gg_o):
        h = _seq_tile(x_ref, m_ref)
        r = lax.rsqrt(jnp.mean(h * h, axis=-1, keepdims=True) + EPS)
        xb = ((h * r) * nm_ref[...]).astype(BF)
        xn_o[...] = xb
        bdv = bd_ref[...]
        q = _dot(xb, w_ref[:, 0:QW])
        qr_o[...] = q
        qn_o[...] = (q * lax.rsqrt(_seg_mean(q * q, bdv) + EPS) * qn_ref[...]).astype(BF)
        k = _dot(xb, w_ref[:, QW:QW + KVW])
        kr_o[...] = k
        kn_o[...] = (k * lax.rsqrt(_seg_mean(k * k, bdv) + EPS) * kn_ref[...]).astype(BF)
        v_o[...] = _dot(xb, w_ref[:, QW + KVW:QW + 2 * KVW]).astype(BF)
        u_o[...] = _dot(xb, w_ref[:, QW + 2 * KVW:QW + 2 * KVW + D])
        gg_o[...] = _dot(xb, w_ref[:, QW + 2 * KVW + D:IN_COLS])

    outs = [(D, BF), (QW, F32), (KVW, F32), (QW, BF), (KVW, BF), (KVW, BF), (D, F32), (2 * D, F32)]
    return pl.pallas_call(
        body, name="in_proj_fwd", grid=(lp // TM,),
        in_specs=_seq_specs() + [_full((D, IN_COLS)), _full((1, D)), _full((1, QW)), _full((1, KVW)),
                                 _full((256, 256))],
        out_specs=[_rows(TM, w) for w, _ in outs],
        out_shape=[jax.ShapeDtypeStruct((lp, w), t) for w, t in outs],
        compiler_params=_cp(("arbitrary",)),
    )(x, mpad, w_in, nm, qn_t, kn_t, bd)


def in_proj_bwd(dqn, dkn, dv, du, dgg, qr, kr, x, mpad, dh1, w_in, nm, qn_t, kn_t, bd):
    lp = x.shape[0] + PAD

    def body(dqn_ref, dkn_ref, dv_ref, du_ref, dgg_ref, qr_ref, kr_ref, x_ref, m_ref, dh1_ref,
             w_ref, nm_ref, qn_ref, kn_ref, bd_ref,
             dproj_o, dx_o, dm_o, dnm_o, dqw_o, dkw_o):
        i = pl.program_id(0)

        @pl.when(i == 0)
        def _():
            dnm_o[...] = jnp.zeros_like(dnm_o)
            dqw_o[...] = jnp.zeros_like(dqw_o)
            dkw_o[...] = jnp.zeros_like(dkw_o)

        bdv = bd_ref[...]

        def norm_bwd(x, dy, g, acc_ref):
            rr = lax.rsqrt(_seg_mean(x * x, bdv) + EPS)
            xh = x * rr
            acc_ref[...] += jnp.sum(dy * xh, axis=0, keepdims=True)
            dxh = dy * g
            return rr * (dxh - xh * _seg_mean(dxh * xh, bdv))

        dq = norm_bwd(qr_ref[...], dqn_ref[...], qn_ref[...], dqw_o)
        dk = norm_bwd(kr_ref[...], dkn_ref[...], kn_ref[...], dkw_o)
        dproj_o[:, 0:QW] = dq.astype(BF)
        dproj_o[:, QW:QW + KVW] = dk.astype(BF)
        dproj_o[:, QW + KVW:QW + 2 * KVW] = dv_ref[...].astype(BF)
        dproj_o[:, QW + 2 * KVW:QW + 2 * KVW + D] = du_ref[...]
        dproj_o[:, QW + 2 * KVW + D:IN_COLS] = dgg_ref[...]
        dxn = _dot_nt(dproj_o[...], w_ref[...])
        h = _seq_tile(x_ref, m_ref)
        r = lax.rsqrt(jnp.mean(h * h, axis=-1, keepdims=True) + EPS)
        xh = h * r
        dnm_o[...] += jnp.sum(dxn * xh, axis=0, keepdims=True)
        dxh = dxn * nm_ref[...]
        dh0 = dh1_ref[...] + r * (dxh - xh * jnp.mean(dxh * xh, axis=-1, keepdims=True))
        dx_o[...] = dh0

        @pl.when(i == 0)
        def _():
            dm_o[...] = dh0

    return pl.pallas_call(
        body, name="in_proj_bwd", grid=(lp // TM,),
        in_specs=[_rows(TM, QW), _rows(TM, KVW), _rows(TM, KVW), _rows(TM, D), _rows(TM, 2 * D),
                  _rows(TM, QW), _rows(TM, KVW)] + _seq_specs() + [_rows(TM, D),
                  _full((D, IN_COLS)), _full((1, D)), _full((1, QW)), _full((1, KVW)), _full((256, 256))],
        out_specs=[_rows(TM, IN_COLS), _seq_specs()[0], _full((PAD, D)), _full((1, D)), _full((1, QW)),
                   _full((1, KVW))],
        out_shape=[jax.ShapeDtypeStruct((lp, IN_COLS), BF), jax.ShapeDtypeStruct((lp - PAD, D), F32),
                   jax.ShapeDtypeStruct((PAD, D), F32),
                   jax.ShapeDtypeStruct((1, D), F32), jax.ShapeDtypeStruct((1, QW), F32),
                   jax.ShapeDtypeStruct((1, KVW), F32)],
        compiler_params=_cp(("arbitrary",)),
    )(dqn, dkn, dv, du, dgg, qr, kr, x, mpad, dh1, w_in, nm, qn_t, kn_t, bd)


def _att_bias(b):
    qi = lax.broadcasted_iota(jnp.int32, (WIN, WIN), 0)
    kj = lax.broadcasted_iota(jnp.int32, (WIN, WIN), 1)
    upper = kj > qi
    valid = (upper & ((b - 1) * WIN + kj >= PAD)) | (jnp.logical_not(upper) & (b * WIN + kj >= META0))
    bias_pc = jnp.where(valid, 0.0, NEG)
    kj1 = lax.broadcasted_iota(jnp.int32, (1, WIN), 1)
    bias_m = jnp.where((kj1 >= WIN - N_META) & (b * WIN >= PAD), 0.0, NEG)
    return bias_m, bias_pc, upper


def _dup_half(ref, kp, pos, lo):
    pair = ref[:, 128 * kp:128 * kp + 128].astype(F32)
    rolled = pltpu.roll(pair, 64, axis=1)
    keep = lo if pos == 0 else jnp.logical_not(lo)
    return jnp.where(keep, pair, rolled).astype(BF)


def _stack_heads(ref, kv, lo):
    parts = []
    for pp in range(2):
        pair = ref[:, 256 * kv + 128 * pp:256 * kv + 128 * pp + 128]
        zero = jnp.zeros_like(pair)
        parts.append(jnp.where(lo, pair, zero))
        parts.append(jnp.where(lo, zero, pair))
    return jnp.concatenate(parts, axis=0)


def _unstack_heads(x4, lo):
    a = jnp.where(lo, x4[0:WIN], x4[WIN:2 * WIN])
    b = jnp.where(lo, x4[2 * WIN:3 * WIN], x4[3 * WIN:4 * WIN])
    return a, b


def _split_pc(x, upper):
    zero = jnp.zeros_like(x)
    return jnp.where(upper, x, zero).astype(BF), jnp.where(upper, zero, x).astype(BF)


def _head_softmax(s_m, s_p, s_c, bias, sink):
    bias_m, bias_pc, upper = bias
    sm = s_m + bias_m
    spc = jnp.where(upper, s_p, s_c) + bias_pc
    m = jnp.maximum(jnp.max(jnp.maximum(sm, spc), axis=-1, keepdims=True), sink)
    em, epc = jnp.exp(sm - m), jnp.exp(spc - m)
    esink = jnp.exp(sink - m)
    inv = 1.0 / (esink + jnp.sum(em + epc, axis=-1, keepdims=True))
    return em, epc, inv, esink


def _kv_specs():
    return [pl.BlockSpec((WIN, KVW), lambda b: (1, 0)),
            pl.BlockSpec((WIN, KVW), lambda b: (jnp.maximum(b - 1, 0), 0)),
            pl.BlockSpec((WIN, KVW), lambda b: (b, 0))]


def attention_fwd(sinks, qn, kn, v):
    lp = qn.shape[0]

    def body(sink_ref, q_ref, km_ref, kp_ref, kc_ref, vm_ref, vp_ref, vc_ref, o_ref):
        b = pl.program_id(0)
        bias = _att_bias(b)
        lo = lax.broadcasted_iota(jnp.int32, (WIN, WIN), 1) < HEAD
        for kv in range(NKV):
            kp, pos = kv // 2, kv % 2
            kds = [_dup_half(r, kp, pos, lo) for r in (km_ref, kp_ref, kc_ref)]
            vds = [_dup_half(r, kp, pos, lo) for r in (vm_ref, vp_ref, vc_ref)]
            q4 = _stack_heads(q_ref, kv, lo)
            ss = [_dot_nt(q4, kd) for kd in kds]
            es, invs = ([], [], []), []
            for h in range(4):
                rs = slice(WIN * h, WIN * h + WIN)
                em, epc, inv, _ = _head_softmax(ss[0][rs], ss[1][rs], ss[2][rs], bias, sink_ref[4 * kv + h])
                e_p, e_c = _split_pc(epc, bias[2])
                for lst, e in zip(es, (em.astype(BF), e_p, e_c)):
                    lst.append(e)
                invs.append(inv)
            e_m, e_p, e_c = (jnp.concatenate(lst, axis=0) for lst in es)
            o4 = (_dot(e_m, vds[0]) + _dot(e_p, vds[1]) + _dot(e_c, vds[2])) * jnp.concatenate(invs, axis=0)
            oa, ob = _unstack_heads(o4, lo)
            o_ref[:, 256 * kv:256 * kv + 128] = oa
            o_ref[:, 256 * kv + 128:256 * kv + 256] = ob

    return pl.pallas_call(
        body, name="attention_fwd", grid=(lp // WIN,),
        in_specs=[pl.BlockSpec(memory_space=pltpu.SMEM), _rows(WIN, QW)] + _kv_specs() + _kv_specs(),
        out_specs=_rows(WIN, QW),
        out_shape=jax.ShapeDtypeStruct((lp, QW), F32),
        compiler_params=_cp(("arbitrary",)),
    )(sinks, qn, kn, kn, kn, v, v, v)


def attention_bwd(sinks, qn, kn, v, do):
    lp = qn.shape[0]
    nb = lp // WIN

    def body(sink_ref, q_ref, km_ref, kp_ref, kc_ref, vm_ref, vp_ref, vc_ref, do_ref,
             dq_ref, dk_hbm, dv_hbm, dsink_ref, dk_acc, dv_acc):
        b = pl.program_id(0)

        @pl.when(b == 0)
        def _():
            dk_acc[...] = jnp.zeros_like(dk_acc)
            dv_acc[...] = jnp.zeros_like(dv_acc)
            dsink_ref[...] = jnp.zeros_like(dsink_ref)

        bias = _att_bias(b)
        upper = bias[2]
        lo = lax.broadcasted_iota(jnp.int32, (WIN, WIN), 1) < HEAD
        lane8 = lax.broadcasted_iota(jnp.int32, (8, 128), 1)
        starts = [WIN, pl.multiple_of(jnp.maximum(b - 1, 0) * WIN, WIN), pl.multiple_of(b * WIN, WIN)]
        for kv in range(NKV):
            kp, pos = kv // 2, kv % 2
            keep = lo if pos == 0 else jnp.logical_not(lo)
            kds = [_dup_half(r, kp, pos, lo) for r in (km_ref, kp_ref, kc_ref)]
            vds = [_dup_half(r, kp, pos, lo) for r in (vm_ref, vp_ref, vc_ref)]
            q4 = _stack_heads(q_ref, kv, lo)
            do4 = _stack_heads(do_ref, kv, lo)
            ss = [_dot_nt(q4, kd) for kd in kds]
            dps = [_dot_nt(do4, vd) for vd in vds]
            ds_l, p_l = ([], [], []), ([], [], [])
            for h in range(4):
                rs = slice(WIN * h, WIN * h + WIN)
                em, epc, inv, esink = _head_softmax(ss[0][rs], ss[1][rs], ss[2][rs], bias, sink_ref[4 * kv + h])
                p_m, p_pc = em * inv, epc * inv
                dp_m = dps[0][rs]
                dp_pc = jnp.where(upper, dps[1][rs], dps[2][rs])
                delta = jnp.sum(p_m * dp_m + p_pc * dp_pc, axis=-1, keepdims=True)
                val = -jnp.sum(esink * inv * delta, axis=0, keepdims=True)
                dsink_ref[...] += jnp.where(lane8 == 4 * kv + h, val, 0.0)
                for lst, t in zip(ds_l, ((p_m * (dp_m - delta)).astype(BF),) + _split_pc(p_pc * (dp_pc - delta), upper)):
                    lst.append(t)
                for lst, t in zip(p_l, (p_m.astype(BF),) + _split_pc(p_pc, upper)):
                    lst.append(t)
            dss = [jnp.concatenate(lst, axis=0) for lst in ds_l]
            ps = [jnp.concatenate(lst, axis=0) for lst in p_l]
            dq4 = _dot(dss[0], kds[0]) + _dot(dss[1], kds[1]) + _dot(dss[2], kds[2])
            dqa, dqb = _unstack_heads(dq4, lo)
            dq_ref[:, 256 * kv:256 * kv + 128] = dqa
            dq_ref[:, 256 * kv + 128:256 * kv + 256] = dqb
            for x in range(3):
                dkd = _dot_tn(dss[x], q4)
                dvd = _dot_tn(ps[x], do4)
                dkd = jnp.where(keep, dkd + pltpu.roll(dkd, 64, axis=1), 0.0)
                dvd = jnp.where(keep, dvd + pltpu.roll(dvd, 64, axis=1), 0.0)
                dk_acc[pl.ds(starts[x], WIN), 128 * kp:128 * kp + 128] += dkd
                dv_acc[pl.ds(starts[x], WIN), 128 * kp:128 * kp + 128] += dvd

        @pl.when(b == nb - 1)
        def _():
            pltpu.sync_copy(dk_acc, dk_hbm)
            pltpu.sync_copy(dv_acc, dv_hbm)

    return pl.pallas_call(
        body, name="attention_bwd", grid=(nb,),
        in_specs=[pl.BlockSpec(memory_space=pltpu.SMEM), _rows(WIN, QW)] + _kv_specs() + _kv_specs()
                 + [_rows(WIN, QW)],
        out_specs=[_rows(WIN, QW), ANY, ANY, _full((8, 128))],
        out_shape=[jax.ShapeDtypeStruct((lp, QW), F32), jax.ShapeDtypeStruct((lp, KVW), F32),
                   jax.ShapeDtypeStruct((lp, KVW), F32), jax.ShapeDtypeStruct((8, 128), F32)],
        scratch_shapes=[pltpu.VMEM((lp, KVW), F32), pltpu.VMEM((lp, KVW), F32)],
        compiler_params=_cp(("arbitrary",)),
    )(sinks, qn, kn, kn, kn, v, v, v, do)


PW_ROWS = 64
SEG = TS // 8


def _segment_order():
    rho = jnp.arange(TS)
    token = SEG * (rho % 8) + rho // 8
    p = (token[:, None] == jnp.arange(TS)[None, :]).astype(BF)
    return p, p.T


def _reorder(p, x):
    hi = x.astype(BF)
    r1 = x - hi.astype(F32)
    mid = r1.astype(BF)
    lo = (r1 - mid.astype(F32)).astype(BF)
    return _dot(p, hi) + _dot(p, mid) + _dot(p, lo)


def _segment_scan(xr_ref, xi_ref, pw_ref, base, ls, c_r, c_i, reverse, visit=None):
    ar, ai = pw_ref[base:base + 8, ls], pw_ref[base + 8:base + 16, ls]
    order = list(range(SEG - 1, -1, -1)) if reverse else list(range(SEG))
    s_r = s_i = jnp.zeros_like(ar)
    for i in order:
        rows = pl.ds(8 * i, 8)
        s_r, s_i = ar * s_r - ai * s_i + xr_ref[rows, ls], ar * s_i + ai * s_r + xi_ref[rows, ls]
        xr_ref[rows, ls] = s_r
        xi_ref[rows, ls] = s_i
    rid = lax.broadcasted_iota(jnp.int32, s_r.shape, 0)
    if reverse:
        e_r = jnp.where(rid < 7, pltpu.roll(s_r, 7, axis=0), c_r)
        e_i = jnp.where(rid < 7, pltpu.roll(s_i, 7, axis=0), c_i)
    else:
        e_r = jnp.where(rid >= 1, pltpu.roll(s_r, 1, axis=0), c_r)
        e_i = jnp.where(rid >= 1, pltpu.roll(s_i, 1, axis=0), c_i)
    for n, k in enumerate((1, 2, 4)):
        shift = 8 - k if reverse else k
        t_r, t_i = pltpu.roll(e_r, shift, axis=0), pltpu.roll(e_i, shift, axis=0)
        kr = pw_ref[base + 16 + 16 * n:base + 24 + 16 * n, ls]
        ki = pw_ref[base + 24 + 16 * n:base + 32 + 16 * n, ls]
        e_r, e_i = e_r + kr * t_r - ki * t_i, e_i + kr * t_i + ki * t_r
    last = 0 if reverse else 7
    a16r, a16i = pw_ref[base + 16 + last:base + 17 + last, ls], pw_ref[base + 24 + last:base + 25 + last, ls]
    lr, li, fr, fi = (t[last:last + 1] for t in (e_r, e_i, s_r, s_i))
    out = (a16r * lr - a16i * li + fr, a16r * li + a16i * lr + fi)
    d_r, d_i = e_r, e_i
    extra = None
    for i in order:
        rows = pl.ds(8 * i, 8)
        d_r, d_i = ar * d_r - ai * d_i, ar * d_i + ai * d_r
        f_r, f_i = xr_ref[rows, ls] + d_r, xi_ref[rows, ls] + d_i
        xr_ref[rows, ls] = f_r
        xi_ref[rows, ls] = f_i
        if visit is not None:
            extra = visit(i, f_r, f_i, extra)
    return out if visit is None else (out, extra)


def ssm_fwd(u, pw, br, bi, cr, ci, dsk, perm):
    lp = u.shape[0]

    def body(u_ref, pw_ref, br_ref, bi_ref, cr_ref, ci_ref, d_ref, p_ref, pt_ref,
             y_ref, sr_ref, si_ref, car_r, car_i, yp_s):
        i = pl.program_id(0)

        @pl.when(i == 0)
        def _():
            car_r[...] = jnp.zeros_like(car_r)
            car_i[...] = jnp.zeros_like(car_i)

        u = u_ref[...]
        ub = _dot(p_ref[...], u.astype(BF)).astype(BF)
        for j in range(NLC):
            uj = ub[:, 128 * j:128 * j + 128]
            sr_ref[:, LC * j:LC * j + LC] = _dot(uj, br_ref[j])
            si_ref[:, LC * j:LC * j + LC] = _dot(uj, bi_ref[j])
        for j in range(NLC):
            ls = slice(LC * j, LC * j + LC)
            car_r[:, ls], car_i[:, ls] = _segment_scan(sr_ref, si_ref, pw_ref, 0, ls, car_r[:, ls], car_i[:, ls],
                                                       False)
        for j in range(NLC):
            ls = slice(LC * j, LC * j + LC)
            yp_s[:, 128 * j:128 * j + 128] = (_dot(sr_ref[:, ls].astype(BF), cr_ref[j])
                                              - _dot(si_ref[:, ls].astype(BF), ci_ref[j]))
        y_ref[...] = _reorder(pt_ref[...], yp_s[...]) + d_ref[...] * u

    p, pt = perm
    return pl.pallas_call(
        body, name="ssm_fwd", grid=(lp // TS,),
        in_specs=[_rows(TS, D), _full((2 * PW_ROWS, SW)), _full((NLC, 128, LC)), _full((NLC, 128, LC)),
                  _full((NLC, LC, 128)), _full((NLC, LC, 128)), _full((1, D)), _full((TS, TS)), _full((TS, TS))],
        out_specs=[_rows(TS, D), _rows(TS, SW), _rows(TS, SW)],
        out_shape=[jax.ShapeDtypeStruct((lp, D), F32), jax.ShapeDtypeStruct((lp, SW), F32),
                   jax.ShapeDtypeStruct((lp, SW), F32)],
        scratch_shapes=[pltpu.VMEM((1, SW), F32), pltpu.VMEM((1, SW), F32), pltpu.VMEM((TS, D), F32)],
        compiler_params=_cp(("arbitrary",)),
    )(u, pw, br, bi, cr, ci, dsk, p, pt)


def ssm_bwd(dy, u, sr, si, pw, br, bi, cr, ci, dsk, perm):
    lp = u.shape[0]
    nt = lp // TS

    def rev(i):
        return (nt - 1 - i, 0)

    def prev8(i):
        return (jnp.maximum((nt - 1 - i) * (TS // 8) - 1, 0), 0)

    def body(dy_ref, u_ref, sr_ref, si_ref, pr8_ref, pi8_ref, pw_ref, br_ref, bi_ref, cr_ref, ci_ref, d_ref,
             p_ref, pt_ref, du_ref, da_ref, dbr_ref, dbi_ref, dcr_ref, dci_ref, dd_ref,
             gr_s, gi_s, car_r, car_i, dup_s):
        i = pl.program_id(0)

        @pl.when(i == 0)
        def _():
            car_r[...] = jnp.zeros_like(car_r)
            car_i[...] = jnp.zeros_like(car_i)
            for ref in (da_ref, dbr_ref, dbi_ref, dcr_ref, dci_ref, dd_ref):
                ref[...] = jnp.zeros_like(ref)

        keep = 1.0 - (i == nt - 1).astype(F32)
        dy = dy_ref[...]
        u = u_ref[...]
        dd_ref[...] += jnp.sum(dy * u, axis=0, keepdims=True)
        pm = p_ref[...]
        dyb = _dot(pm, dy.astype(BF)).astype(BF)
        ub = _dot(pm, u.astype(BF)).astype(BF)
        for j in range(NLC):
            ls = slice(LC * j, LC * j + LC)
            dyj = dyb[:, 128 * j:128 * j + 128]
            gr_s[:, ls] = _dot_nt(dyj, cr_ref[j])
            gi_s[:, ls] = -_dot_nt(dyj, ci_ref[j])
            dcr_ref[j] += _dot_tn(dyj, sr_ref[:, ls].astype(BF))
            dci_ref[j] -= _dot_tn(dyj, si_ref[:, ls].astype(BF))
        rid = lax.broadcasted_iota(jnp.int32, (8, LC), 0)
        for j in range(NLC):
            ls = slice(LC * j, LC * j + LC)

            def accumulate(i, g_r, g_i, acc):
                if i >= 1:
                    rows = pl.ds(8 * (i - 1), 8)
                    p_r, p_i = sr_ref[rows, ls], si_ref[rows, ls]
                else:
                    rows = pl.ds(8 * (SEG - 1), 8)
                    p_r = jnp.where(rid >= 1, pltpu.roll(sr_ref[rows, ls], 1, axis=0), pr8_ref[7:8, ls] * keep)
                    p_i = jnp.where(rid >= 1, pltpu.roll(si_ref[rows, ls], 1, axis=0), pi8_ref[7:8, ls] * keep)
                t_r, t_i = g_r * p_r + g_i * p_i, g_i * p_r - g_r * p_i
                return (t_r, t_i) if acc is None else (acc[0] + t_r, acc[1] + t_i)

            (car_r[:, ls], car_i[:, ls]), (t_r, t_i) = _segment_scan(
                gr_s, gi_s, pw_ref, PW_ROWS, ls, car_r[:, ls], car_i[:, ls], True, accumulate)
            da_ref[0:8, ls] += t_r
            da_ref[8:16, ls] += t_i
        for j in range(NLC):
            ls = slice(LC * j, LC * j + LC)
            uj = ub[:, 128 * j:128 * j + 128]
            grb = gr_s[:, ls].astype(BF)
            gib = gi_s[:, ls].astype(BF)
            dup_s[:, 128 * j:128 * j + 128] = _dot_nt(grb, br_ref[j]) + _dot_nt(gib, bi_ref[j])
            dbr_ref[j] += _dot_tn(uj, grb)
            dbi_ref[j] += _dot_tn(uj, gib)
        du_ref[...] = (_reorder(pt_ref[...], dup_s[...]) + d_ref[...] * dy).astype(BF)

    p, pt = perm
    return pl.pallas_call(
        body, name="ssm_bwd", grid=(nt,),
        in_specs=[pl.BlockSpec((TS, D), rev), pl.BlockSpec((TS, D), rev), pl.BlockSpec((TS, SW), rev),
                  pl.BlockSpec((TS, SW), rev), pl.BlockSpec((8, SW), prev8), pl.BlockSpec((8, SW), prev8),
                  _full((2 * PW_ROWS, SW)), _full((NLC, 128, LC)), _full((NLC, 128, LC)),
                  _full((NLC, LC, 128)), _full((NLC, LC, 128)), _full((1, D)), _full((TS, TS)), _full((TS, TS))],
        out_specs=[pl.BlockSpec((TS, D), rev), _full((16, SW)), _full((NLC, 128, LC)), _full((NLC, 128, LC)),
                   _full((NLC, 128, LC)), _full((NLC, 128, LC)), _full((1, D))],
        out_shape=[jax.ShapeDtypeStruct((lp, D), BF), jax.ShapeDtypeStruct((16, SW), F32),
                   jax.ShapeDtypeStruct((NLC, 128, LC), F32), jax.ShapeDtypeStruct((NLC, 128, LC), F32),
                   jax.ShapeDtypeStruct((NLC, 128, LC), F32), jax.ShapeDtypeStruct((NLC, 128, LC), F32),
                   jax.ShapeDtypeStruct((1, D), F32)],
        scratch_shapes=[pltpu.VMEM((TS, SW), F32), pltpu.VMEM((TS, SW), F32),
                        pltpu.VMEM((1, SW), F32), pltpu.VMEM((1, SW), F32), pltpu.VMEM((TS, D), F32)],
        compiler_params=_cp(("arbitrary",)),
    )(dy, u, sr, si, sr, si, pw, br, bi, cr, ci, dsk, p, pt)


def _merge_parts(attn, zz, gg, abn, sbn):
    za, zb = zz[:, 0:D], zz[:, D:2 * D]
    sgz = _sigmoid(zb)
    ssm = za * sgz
    ra = lax.rsqrt(jnp.mean(attn * attn, axis=-1, keepdims=True) + EPS)
    rs = lax.rsqrt(jnp.mean(ssm * ssm, axis=-1, keepdims=True) + EPS)
    ah, sh = attn * ra, ssm * rs
    sga, sgs = _sigmoid(gg[:, 0:D]), _sigmoid(gg[:, D:2 * D])
    return za, sgz, ra, rs, ah, sh, sga, sgs, ah * abn, sh * sbn


def mid_fwd(y, attn, gg, x, mpad, w_glu, w_out, abn, sbn):
    lp = y.shape[0]

    def body(y_ref, a_ref, gg_ref, x_ref, m_ref, wg_ref, wo_ref, abn_ref, sbn_ref, z_o, zz_o, mg_o, h1_o):
        z, _ = _gelu(y_ref[...])
        zb = z.astype(BF)
        z_o[...] = zb
        zz = _dot(zb, wg_ref[...])
        zz_o[...] = zz.astype(BF)
        _, _, _, _, _, _, sga, sgs, an, sn = _merge_parts(a_ref[...], zz, gg_ref[...], abn_ref[...], sbn_ref[...])
        mb = (sga * an + sgs * sn).astype(BF)
        mg_o[...] = mb
        h1_o[...] = _seq_tile(x_ref, m_ref) + _dot(mb, wo_ref[...])

    return pl.pallas_call(
        body, name="mid_fwd", grid=(lp // TM,),
        in_specs=[_rows(TM, D), _rows(TM, D), _rows(TM, 2 * D)] + _seq_specs() + [_full((D, 2 * D)), _full((D, D)),
                  _full((1, D)), _full((1, D))],
        out_specs=[_rows(TM, D), _rows(TM, 2 * D), _rows(TM, D), _rows(TM, D)],
        out_shape=[jax.ShapeDtypeStruct((lp, D), BF), jax.ShapeDtypeStruct((lp, 2 * D), BF),
                   jax.ShapeDtypeStruct((lp, D), BF), jax.ShapeDtypeStruct((lp, D), F32)],
        compiler_params=_cp(("arbitrary",)),
    )(y, attn, gg, x, mpad, w_glu, w_out, abn, sbn)


def mid_bwd(dh1b, y, attn, gg, zzb, w_glu, w_out, abn, sbn):
    lp = y.shape[0]

    def body(dh_ref, y_ref, a_ref, gg_ref, zz_ref, wg_ref, wo_ref, abn_ref, sbn_ref,
             dy_o, dattn_o, dgg_o, dzz_o, dabn_o, dsbn_o):
        i = pl.program_id(0)

        @pl.when(i == 0)
        def _():
            dabn_o[...] = jnp.zeros_like(dabn_o)
            dsbn_o[...] = jnp.zeros_like(dsbn_o)

        dm = _dot_nt(dh_ref[...], wo_ref[...])
        abn, sbn = abn_ref[...], sbn_ref[...]
        za, sgz, ra, rs, ah, sh, sga, sgs, an, sn = _merge_parts(
            a_ref[...], zz_ref[...].astype(F32), gg_ref[...], abn, sbn)
        dgg_o[:, 0:D] = (dm * an * sga * (1.0 - sga)).astype(BF)
        dgg_o[:, D:2 * D] = (dm * sn * sgs * (1.0 - sgs)).astype(BF)
        da = dm * sga
        dabn_o[...] += jnp.sum(da * ah, axis=0, keepdims=True)
        dah = da * abn
        dattn_o[...] = (ra * (dah - ah * jnp.mean(dah * ah, axis=-1, keepdims=True))).astype(BF)
        ds = dm * sgs
        dsbn_o[...] += jnp.sum(ds * sh, axis=0, keepdims=True)
        dsh = ds * sbn
        dssm = rs * (dsh - sh * jnp.mean(dsh * sh, axis=-1, keepdims=True))
        dzz_o[:, 0:D] = (dssm * sgz).astype(BF)
        dzz_o[:, D:2 * D] = (dssm * za * sgz * (1.0 - sgz)).astype(BF)
        dz = _dot_nt(dzz_o[...], wg_ref[...])
        yv = y_ref[...]
        _, t = _gelu(yv)
        dy_o[...] = dz * _gelu_grad(yv, t)

    return pl.pallas_call(
        body, name="mid_bwd", grid=(lp // TM,),
        in_specs=[_rows(TM, D), _rows(TM, D), _rows(TM, D), _rows(TM, 2 * D), _rows(TM, 2 * D),
                  _full((D, 2 * D)), _full((D, D)), _full((1, D)), _full((1, D))],
        out_specs=[_rows(TM, D), _rows(TM, D), _rows(TM, 2 * D), _rows(TM, 2 * D), _full((1, D)), _full((1, D))],
        out_shape=[jax.ShapeDtypeStruct((lp, D), F32), jax.ShapeDtypeStruct((lp, D), BF),
                   jax.ShapeDtypeStruct((lp, 2 * D), BF), jax.ShapeDtypeStruct((lp, 2 * D), BF),
                   jax.ShapeDtypeStruct((1, D), F32), jax.ShapeDtypeStruct((1, D), F32)],
        compiler_params=_cp(("arbitrary",)),
    )(dh1b, y, attn, gg, zzb, w_glu, w_out, abn, sbn)


def ffn_fwd_bwd(h1, tgt, w_fi, w_fo, gf):
    lp = h1.shape[0]
    tf = 256
    bounds = (0, 1536, DFF)

    def body(h_ref, t_ref, wi_ref, wo_ref, gf_ref, hn_o, dgu_o, act_o, dh2_o, dh1_o, dh1b_o, dgf_o, loss_o,
             gate_s, up_s):
        i = pl.program_id(0)

        @pl.when(i == 0)
        def _():
            dgf_o[...] = jnp.zeros_like(dgf_o)
            loss_o[...] = jnp.zeros_like(loss_o)

        h = h_ref[...]
        r = lax.rsqrt(jnp.mean(h * h, axis=-1, keepdims=True) + EPS)
        xh = h * r
        hb = (xh * gf_ref[...]).astype(BF)
        hn_o[...] = hb
        h2 = h
        for c0, c1 in zip(bounds[:-1], bounds[1:]):
            gate = _dot(hb, wi_ref[:, c0:c1])
            up = _dot(hb, wi_ref[:, DFF + c0:DFF + c1])
            gate_s[:, c0:c1] = gate
            up_s[:, c0:c1] = up
            ab = (gate * _sigmoid(gate) * up).astype(BF)
            act_o[:, c0:c1] = ab
            h2 = h2 + _dot(ab, wo_ref[c0:c1, :])
        real = (i >= PAD // tf).astype(F32)
        e = (h2 - t_ref[...]) * real
        loss_o[...] += 0.5 * jnp.sum(jnp.sum(e * e, axis=-1, keepdims=True), axis=0, keepdims=True) * (1.0 / D)
        dh2 = e * (1.0 / D)
        db = dh2.astype(BF)
        dh2_o[...] = db
        dhn = jnp.zeros((tf, D), F32)
        for c0, c1 in zip(bounds[:-1], bounds[1:]):
            gate = gate_s[:, c0:c1]
            up = up_s[:, c0:c1]
            sg = _sigmoid(gate)
            dact = _dot_nt(db, wo_ref[c0:c1, :])
            dgate = (dact * up * (sg * (1.0 + gate * (1.0 - sg)))).astype(BF)
            dup = (dact * (gate * sg)).astype(BF)
            dgu_o[:, c0:c1] = dgate
            dgu_o[:, DFF + c0:DFF + c1] = dup
            dhn += _dot_nt(dgate, wi_ref[:, c0:c1]) + _dot_nt(dup, wi_ref[:, DFF + c0:DFF + c1])
        dgf_o[...] += jnp.sum(dhn * xh, axis=0, keepdims=True)
        dxh = dhn * gf_ref[...]
        dh1 = dh2 + r * (dxh - xh * jnp.mean(dxh * xh, axis=-1, keepdims=True))
        dh1_o[...] = dh1
        dh1b_o[...] = dh1.astype(BF)

    return pl.pallas_call(
        body, name="ffn_fwd_bwd", grid=(lp // tf,),
        in_specs=[_rows(tf, D), pl.BlockSpec((tf, D), lambda i: (jnp.maximum(i - PAD // tf, 0), 0)),
                  _const((D, 2 * DFF)), _const((DFF, D)), _full((1, D))],
        out_specs=[_rows(tf, D), _rows(tf, 2 * DFF), _rows(tf, DFF), _rows(tf, D), _rows(tf, D), _rows(tf, D),
                   _full((1, D)), _full((1, 1))],
        out_shape=[jax.ShapeDtypeStruct((lp, D), BF), jax.ShapeDtypeStruct((lp, 2 * DFF), BF),
                   jax.ShapeDtypeStruct((lp, DFF), BF), jax.ShapeDtypeStruct((lp, D), BF),
                   jax.ShapeDtypeStruct((lp, D), F32), jax.ShapeDtypeStruct((lp, D), BF),
                   jax.ShapeDtypeStruct((1, D), F32), jax.ShapeDtypeStruct((1, 1), F32)],
        scratch_shapes=[pltpu.VMEM((tf, DFF), F32), pltpu.VMEM((tf, DFF), F32)],
        compiler_params=_cp(("arbitrary",), vmem_mb=58),
    )(h1, tgt, w_fi, w_fo, gf)


def weight_grad(name, a, b, tn, with_bf16=False):
    lp, k = a.shape
    n = b.shape[1]
    tr = next(t for t in (2112, 1056, 768, 512, 384, 256, 128) if lp % t == 0 and t * k * 2 <= (5 << 20))

    def body(a_ref, b_ref, o_ref, *ob_ref):
        @pl.when(pl.program_id(1) == 0)
        def _():
            o_ref[...] = jnp.zeros_like(o_ref)

        o_ref[...] += _dot_tn(a_ref[...], b_ref[...])
        if with_bf16:
            @pl.when(pl.program_id(1) == lp // tr - 1)
            def _():
                ob_ref[0][...] = o_ref[...].astype(BF)

    spec = pl.BlockSpec((k, tn), lambda j, m: (0, j))
    return pl.pallas_call(
        body, name=name, grid=(n // tn, lp // tr),
        in_specs=[pl.BlockSpec((tr, k), lambda j, m: (m, 0)), pl.BlockSpec((tr, tn), lambda j, m: (m, j))],
        out_specs=[spec, spec] if with_bf16 else spec,
        out_shape=([jax.ShapeDtypeStruct((k, n), F32), jax.ShapeDtypeStruct((k, n), BF)] if with_bf16
                   else jax.ShapeDtypeStruct((k, n), F32)),
        compiler_params=_cp(("arbitrary", "arbitrary")),
    )(a, b)


def _adamw_math(w, g, m, v):
    m = B1 * m + (1.0 - B1) * g
    v = B2 * v + (1.0 - B2) * (g * g)
    delta = -LR * ((m / BC1) / (jnp.sqrt(v / BC2) + AEPS) + WD * w)
    return delta, m, v


def _row_tile(r):
    return next((t for t in (256, 128, 64, 32, 16, 8) if r % t == 0), r)


def adamw(name, w, g, m, v):
    r, c = w.shape
    tr = _row_tile(r)

    def body(w_ref, g_ref, m_ref, v_ref, g_o, d_o, m_o, v_o):
        g = g_ref[...]
        g_o[...] = g
        d_o[...], m_o[...], v_o[...] = _adamw_math(w_ref[...], g, m_ref[...], v_ref[...])

    spec = pl.BlockSpec((tr, c), lambda i: (i, 0))
    return pl.pallas_call(
        body, name=name, grid=(r // tr,), in_specs=[spec] * 4, out_specs=[spec] * 4,
        out_shape=[jax.ShapeDtypeStruct((r, c), F32)] * 4,
        compiler_params=_cp(("arbitrary",)),
    )(w, g, m, v)


def adamw_small(ws, gs, ms, vs):
    n = len(ws)
    steps = 8

    def spec(a):
        if a.ndim == 4:
            return pl.BlockSpec((1, a.shape[1] // steps) + a.shape[2:], lambda i: (0, i, 0, 0))
        nd = a.ndim
        return pl.BlockSpec(a.shape, lambda i: (0,) * nd)

    def body(*refs):
        w, g, m, v, d_o, m_o, v_o = (refs[k * n:(k + 1) * n] for k in range(7))
        for t in range(n):
            d_o[t][...], m_o[t][...], v_o[t][...] = _adamw_math(w[t][...], g[t][...], m[t][...], v[t][...])

    specs = [spec(a) for a in ws]
    res = pl.pallas_call(
        body, name="adamw_small", grid=(steps,), in_specs=specs * 4, out_specs=specs * 3,
        out_shape=[jax.ShapeDtypeStruct(a.shape, F32) for a in ws] * 3,
        compiler_params=_cp(("arbitrary",)),
    )(*ws, *gs, *ms, *vs)
    return res[:n], res[n:2 * n], res[2 * n:]


def pair_sum(name, place, full, got, ax):
    r, c = got.shape
    tr = _row_tile(r)

    def body(s_ref, a_ref, b_ref, o_ref):
        o_ref[...] = (a_ref[...] + b_ref[...]).astype(BF)

    if ax == 1:
        mine = pl.BlockSpec((tr, c), lambda i, s: (s[1] * (r // tr) + i, 0))
    else:
        mine = pl.BlockSpec((tr, c), lambda i, s: (i, s[1]))
    spec = pl.BlockSpec((tr, c), lambda i, s: (i, 0))
    return pl.pallas_call(
        body, name=name,
        grid_spec=pltpu.PrefetchScalarGridSpec(num_scalar_prefetch=1, grid=(r // tr,), in_specs=[mine, spec],
                                               out_specs=spec),
        out_shape=jax.ShapeDtypeStruct((r, c), BF),
        compiler_params=_cp(("arbitrary",)),
    )(place, full, got)


def sum4(name, place, parts, half, ax):
    _, r, c = parts.shape
    tr = _row_tile(r)

    def body(s_ref, p_ref, h_ref, o_ref):
        o_ref[...] = ((p_ref[0].astype(F32) + p_ref[1].astype(F32))
                      + (p_ref[2].astype(F32) + h_ref[...].astype(F32)))

    if ax == 1:
        own = pl.BlockSpec((tr, c), lambda i, s: (i, s[0]))
        out = pl.BlockSpec((tr, c), lambda i, s: (s[1] * (r // tr) + i, 0))
        shape = (2 * r, c)
    else:
        own = pl.BlockSpec((tr, c), lambda i, s: (s[0] * (r // tr) + i, 0))
        out = pl.BlockSpec((tr, c), lambda i, s: (i, s[1]))
        shape = (r, 2 * c)
    return pl.pallas_call(
        body, name=name,
        grid_spec=pltpu.PrefetchScalarGridSpec(
            num_scalar_prefetch=1, grid=(r // tr,),
            in_specs=[pl.BlockSpec((3, tr, c), lambda i, s: (0, i, 0)), own], out_specs=out),
        out_shape=jax.ShapeDtypeStruct(shape, F32),
        compiler_params=_cp(("arbitrary",)),
    )(place, parts, half)


def sum8(name, place, parts, full, ax):
    _, r, c = parts.shape
    tr = _row_tile(r)

    def body(s_ref, p_ref, g_ref, o_ref):
        p = [p_ref[k].astype(F32) for k in range(7)]
        o_ref[...] = ((p[0] + p[1]) + (p[2] + p[3])) + ((p[4] + p[5]) + (p[6] + g_ref[...]))

    if ax == 1:
        own = pl.BlockSpec((tr, c), lambda i, s: (s[1] * (r // tr) + i, s[0]))
        out = pl.BlockSpec((tr, c), lambda i, s: (s[1] * (r // tr) + i, 0))
        shape = (2 * r, c)
    else:
        own = pl.BlockSpec((tr, c), lambda i, s: (s[0] * (r // tr) + i, s[1]))
        out = pl.BlockSpec((tr, c), lambda i, s: (i, s[1]))
        shape = (r, 2 * c)
    return pl.pallas_call(
        body, name=name,
        grid_spec=pltpu.PrefetchScalarGridSpec(
            num_scalar_prefetch=1, grid=(r // tr,),
            in_specs=[pl.BlockSpec((7, tr, c), lambda i, s: (0, i, 0)), own], out_specs=out),
        out_shape=jax.ShapeDtypeStruct(shape, F32),
        compiler_params=_cp(("arbitrary",)),
    )(place, parts, full)


def _place():
    x, y, c = lax.axis_index("x"), lax.axis_index("y"), lax.axis_index("c")
    return x, y, c


def _other_chips(x, y):
    return [(1 - x, y), (x, 1 - y), (1 - x, 1 - y)]


SHARDED = (("w_in", D, IN_COLS, 1), ("w_glu", D, 2 * D, 1), ("w_out", D, D, 0),
           ("w_ffn_in", D, 2 * DFF, 1), ("w_ffn_out", DFF, D, 0))


def _shard_of(ref, axis, k, size):
    if axis == 0:
        return ref.at[pl.ds(k * size, size), :]
    return ref.at[:, pl.ds(k * size, size)]


def _gather_piece(full, dims, k, h):
    r, cc, ax = dims
    if ax == 0:
        return full.at[pl.ds(k * (r // 4) + h * (r // 8), r // 8), :]
    return full.at[pl.ds(h * (r // 2), r // 2), pl.ds(k * (cc // 4), cc // 4)]


def _rows_half(shard, h):
    rs = shard.shape[0]
    return shard.at[pl.ds(h * (rs // 2), rs // 2), :]


def gather_weights(shards, dims):
    items = [(s,) + tuple(d) for s, d in zip(shards, dims)]
    n = len(items)

    def body(*refs):
        ins, outs = refs[:n], refs[n:2 * n]
        ici_send, ici_recv, d2d_send, d2d_recv, local_sems = refs[2 * n:]
        x, y, c = _place()
        mine = 2 * x + y
        chips = _other_chips(x, y)

        def piece(t, k, h):
            return _gather_piece(outs[t], dims[t], k, h)

        def src_half(t, h):
            return _rows_half(ins[t], h)

        local, first, passed = [], [], []
        for t in range(n):
            _, r, cc, ax = items[t]
            size = (r if ax == 0 else cc) // 4
            loc = pltpu.make_async_remote_copy(
                src_ref=ins[t], dst_ref=_shard_of(outs[t], ax, mine, size), send_sem=local_sems.at[t],
                recv_sem=local_sems.at[n + t], device_id=(x, y, 1 - c), device_id_type=MESH)
            loc.start()
            local.append(loc)
            for j, chip in enumerate(chips):
                cp = pltpu.make_async_remote_copy(
                    src_ref=src_half(t, c), dst_ref=piece(t, mine, c),
                    send_sem=ici_send.at[3 * t + j], recv_sem=ici_recv.at[3 * t + j],
                    device_id=(*chip, c), device_id_type=MESH)
                cp.start()
                first.append(cp)
        for t in range(n):
            for j, (px, py) in enumerate(chips):
                got = piece(t, 2 * px + py, c)
                pltpu.make_async_remote_copy(
                    src_ref=src_half(t, c), dst_ref=got, send_sem=ici_send.at[3 * t + j],
                    recv_sem=ici_recv.at[3 * t + j], device_id=(px, py, c), device_id_type=MESH).wait_recv()
                cp = pltpu.make_async_remote_copy(
                    src_ref=got, dst_ref=got, send_sem=d2d_send.at[3 * t + j], recv_sem=d2d_recv.at[3 * t + j],
                    device_id=(x, y, 1 - c), device_id_type=MESH)
                cp.start()
                passed.append(cp)
        for t in range(n):
            for j, (px, py) in enumerate(chips):
                other = piece(t, 2 * px + py, 1 - c)
                pltpu.make_async_remote_copy(
                    src_ref=other, dst_ref=other, send_sem=d2d_send.at[3 * t + j], recv_sem=d2d_recv.at[3 * t + j],
                    device_id=(x, y, 1 - c), device_id_type=MESH).wait_recv()
        for cp in first + passed:
            cp.wait_send()
        for loc in local:
            loc.wait()

    return pl.pallas_call(
        body, name="gather_weights",
        in_specs=[ANY] * n, out_specs=[ANY] * n,
        out_shape=[jax.ShapeDtypeStruct((r, c), s.dtype) for s, r, c, _ in items],
        scratch_shapes=[pltpu.SemaphoreType.DMA((3 * n,)), pltpu.SemaphoreType.DMA((3 * n,)),
                        pltpu.SemaphoreType.DMA((3 * n,)), pltpu.SemaphoreType.DMA((3 * n,)),
                        pltpu.SemaphoreType.DMA((2 * n,))],
        compiler_params=pltpu.CompilerParams(has_side_effects=True),
    )(*shards)


SEM = pl.BlockSpec(memory_space=pltpu.SEMAPHORE)
HBM = pl.BlockSpec(memory_space=pltpu.HBM)
EFFECT = pltpu.SideEffectType.DATAFLOW_SIDE_EFFECTING


def _in_hbm(a):
    return pltpu.with_memory_space_constraint(a, pltpu.HBM)


def _split_copy_start(name, copies, ncopies, srcs, lands, after):
    ns, nl = len(srcs), len(lands)

    def body(*refs):
        ins, land_refs = refs[:ns], refs[ns:ns + nl]
        send_sems, recv_sems = refs[ns + nl + 1], refs[ns + nl + 2]
        token = refs[-1]
        for k, (src, dst, dev) in enumerate(copies(ins, land_refs)):
            pltpu.make_async_remote_copy(src_ref=src, dst_ref=dst, send_sem=send_sems.at[k],
                                         recv_sem=recv_sems.at[k], device_id=dev, device_id_type=MESH).start()
        token[...] = jnp.zeros_like(token)

    res = pl.pallas_call(
        body, name=name,
        in_specs=[HBM] * (ns + nl) + [ANY],
        out_specs=(SEM, SEM) + (HBM,) * (ns + nl) + (pl.BlockSpec(memory_space=pltpu.VMEM),),
        out_shape=(pltpu.SemaphoreType.DMA((ncopies,)), pltpu.SemaphoreType.DMA((ncopies,)))
        + tuple(pltpu.HBM(a.shape, a.dtype) for a in list(srcs) + list(lands))
        + (jax.ShapeDtypeStruct((8, 128), F32),),
        input_output_aliases={t: 2 + t for t in range(ns + nl)},
        compiler_params=pltpu.CompilerParams(has_side_effects=EFFECT),
    )(*[_in_hbm(a) for a in srcs], *[_in_hbm(a) for a in lands], after)
    return res[0], res[1], list(res[2:2 + ns]), list(res[2 + ns:2 + ns + nl]), res[-1]


def _split_copy_wait(name, copies, arrivals, send_sems, recv_sems, srcs, lands, after):
    ns, nl = len(srcs), len(lands)

    def body(*refs):
        ins, land_refs = refs[:ns], refs[ns:ns + nl]
        send_sems_ref, recv_sems_ref = refs[ns + nl], refs[ns + nl + 1]
        mine = copies(ins, land_refs)
        for k, ((src, dst, dev), got) in enumerate(zip(mine, arrivals(ins, land_refs))):
            pltpu.make_async_remote_copy(src_ref=src, dst_ref=dst, send_sem=send_sems_ref.at[k],
                                         recv_sem=recv_sems_ref.at[k], device_id=dev,
                                         device_id_type=MESH).wait_send()
            pltpu.make_async_remote_copy(src_ref=got, dst_ref=got, send_sem=send_sems_ref.at[k],
                                         recv_sem=recv_sems_ref.at[k], device_id=dev,
                                         device_id_type=MESH).wait_recv()

    res = pl.pallas_call(
        body, name=name,
        in_specs=[HBM] * (ns + nl) + [SEM, SEM, ANY],
        out_specs=(HBM,) * (ns + nl),
        out_shape=tuple(pltpu.HBM(a.shape, a.dtype) for a in list(srcs) + list(lands)),
        input_output_aliases={t: t for t in range(ns + nl)},
        compiler_params=pltpu.CompilerParams(has_side_effects=EFFECT),
    )(*srcs, *lands, send_sems, recv_sems, after)
    return list(res[:ns]), list(res[ns:])


def _gather_copies(dims):
    def copies(ins, lands):
        x, y, c = _place()
        mine = 2 * x + y
        out = []
        for t, d in enumerate(dims):
            size = (d[0] if d[2] == 0 else d[1]) // 4
            for px, py in _other_chips(x, y):
                for dc in range(2):
                    out.append((_rows_half(ins[t], c), _gather_piece(lands[t], d, mine, c), (px, py, c ^ dc)))
            out.append((ins[t], _shard_of(lands[t], d[2], mine, size), (x, y, 1 - c)))
        return out

    def arrivals(ins, lands):
        x, y, c = _place()
        mine = 2 * x + y
        out = []
        for t, d in enumerate(dims):
            size = (d[0] if d[2] == 0 else d[1]) // 4
            for px, py in _other_chips(x, y):
                for dc in range(2):
                    out.append(_gather_piece(lands[t], d, 2 * px + py, c ^ dc))
            out.append(_shard_of(lands[t], d[2], mine, size))
        return out

    return copies, arrivals


def _half_of(ref, ax, h):
    r, c = ref.shape
    if ax == 1:
        return ref.at[pl.ds(h * (r // 2), r // 2), :]
    return ref.at[:, pl.ds(h * (c // 2), c // 2)]


def _half_shape(r, c, ax):
    return (r // 2, c) if ax == 1 else (r, c // 2)


def pair_exchange(name, grads, idxs):
    n = len(grads)
    which = [SHARDED[i] for i in idxs]

    def body(*refs):
        ins, got = refs[:n], refs[n:2 * n]
        send_sems, recv_sems = refs[2 * n:]
        x, y, c = _place()
        cps = []
        for t, (_, _, _, ax) in enumerate(which):
            cp = pltpu.make_async_remote_copy(
                src_ref=_half_of(ins[t], ax, 1 - c), dst_ref=got[t], send_sem=send_sems.at[t],
                recv_sem=recv_sems.at[t], device_id=(x, y, 1 - c), device_id_type=MESH)
            cp.start()
            cps.append(cp)
        for cp in cps:
            cp.wait()

    return pl.pallas_call(
        body, name=name,
        in_specs=[ANY] * n, out_specs=[ANY] * n,
        out_shape=[jax.ShapeDtypeStruct(_half_shape(r, c, ax), F32) for _, r, c, ax in which],
        scratch_shapes=[pltpu.SemaphoreType.DMA((n,)), pltpu.SemaphoreType.DMA((n,))],
        compiler_params=pltpu.CompilerParams(has_side_effects=True),
    )(*grads)


def _piece_shapes(idxs):
    out = []
    for _, r, c, ax in [SHARDED[i] for i in idxs]:
        hr, hc = _half_shape(r, c, ax)
        out.append((hr // 4, hc) if ax == 0 else (hr, hc // 4))
    return out


def _piece_of(full, ax, k, h):
    r, c = full.shape
    if ax == 1:
        return full.at[pl.ds(h * (r // 2), r // 2), pl.ds(k * (c // 4), c // 4)]
    return full.at[pl.ds(k * (r // 4), r // 4), pl.ds(h * (c // 2), c // 2)]


def _scatter8_copies(idxs):
    which = [SHARDED[i] for i in idxs]

    def copies(ins, lands):
        x, y, c = _place()
        out = []
        for t, (_, _, _, ax) in enumerate(which):
            for j, (px, py) in enumerate(_other_chips(x, y)):
                for d in range(2):
                    out.append((_piece_of(ins[t], ax, 2 * px + py, c ^ d), lands[t].at[2 * j + d], (px, py, c ^ d)))
            out.append((_piece_of(ins[t], ax, 2 * x + y, 1 - c), lands[t].at[6], (x, y, 1 - c)))
        return out

    def arrivals(ins, lands):
        return [lands[t].at[k] for t in range(len(which)) for k in range(7)]

    return copies, arrivals


def scatter_grads(halves, idxs):
    n = len(halves)
    which = [SHARDED[i] for i in idxs]

    def shapes():
        return _piece_shapes(idxs)

    def body(*refs):
        ins, outs = refs[:n], refs[n:2 * n]
        send_sems, recv_sems = refs[2 * n:]
        x, y, c = _place()
        started = []
        for t, (src, dst, (_, _, _, ax)) in enumerate(zip(ins, outs, which)):
            size = src.shape[ax] // 4
            for j, (px, py) in enumerate(_other_chips(x, y)):
                cp = pltpu.make_async_remote_copy(
                    src_ref=_shard_of(src, ax, 2 * px + py, size), dst_ref=dst.at[j],
                    send_sem=send_sems.at[3 * t + j], recv_sem=recv_sems.at[3 * t + j],
                    device_id=(px, py, c), device_id_type=MESH)
                cp.start()
                started.append(cp)
        for cp in started:
            cp.wait()

    return pl.pallas_call(
        body, name="scatter_grads",
        in_specs=[ANY] * n, out_specs=[ANY] * n,
        out_shape=[jax.ShapeDtypeStruct((3,) + s, BF) for s in shapes()],
        scratch_shapes=[pltpu.SemaphoreType.DMA((3 * n,)), pltpu.SemaphoreType.DMA((3 * n,))],
        compiler_params=pltpu.CompilerParams(has_side_effects=True),
    )(*halves)


def join_halves(name, shards, idxs):
    n = len(shards)
    which = [SHARDED[i] for i in idxs]

    def body(*refs):
        ins, outs = refs[:n], refs[n:2 * n]
        send_sems, recv_sems = refs[2 * n:]
        x, y, c = _place()
        cps = []
        for t, (_, _, _, ax) in enumerate(which):
            cp = pltpu.make_async_remote_copy(
                src_ref=_half_of(ins[t], ax, c), dst_ref=_half_of(outs[t], ax, c), send_sem=send_sems.at[t],
                recv_sem=recv_sems.at[t], device_id=(x, y, 1 - c), device_id_type=MESH)
            cp.start()
            cps.append(cp)
        for t, cp in enumerate(cps):
            cp.wait_send()
            theirs = _half_of(outs[t], which[t][3], 1 - c)
            pltpu.make_async_remote_copy(
                src_ref=theirs, dst_ref=theirs, send_sem=send_sems.at[t], recv_sem=recv_sems.at[t],
                device_id=(x, y, 1 - c), device_id_type=MESH).wait_recv()

    return pl.pallas_call(
        body, name=name,
        in_specs=[ANY] * n, out_specs=[ANY] * n,
        out_shape=[jax.ShapeDtypeStruct(s.shape, F32) for s in shards],
        input_output_aliases={t: t for t in range(n)},
        scratch_shapes=[pltpu.SemaphoreType.DMA((n,)), pltpu.SemaphoreType.DMA((n,))],
        compiler_params=pltpu.CompilerParams(has_side_effects=True),
    )(*shards)


def allreduce_small(pack):
    rows = pack.shape[0]
    hr = rows // 2

    def body(p_ref, o_ref, sib_ref, parts_ref, send_sems, recv_sems):
        x, y, c = _place()
        mine = 2 * x + y
        sibling = (x, y, 1 - c)
        swap = pltpu.make_async_remote_copy(
            src_ref=p_ref, dst_ref=sib_ref, send_sem=send_sems.at[0], recv_sem=recv_sems.at[0],
            device_id=sibling, device_id_type=MESH)
        swap.start()
        swap.wait()
        half = pl.ds(pl.multiple_of(c * hr, 8), hr)
        parts_ref[mine] = p_ref[half, :] + sib_ref[half, :]
        cps = []
        for j, chip in enumerate(_other_chips(x, y)):
            cp = pltpu.make_async_remote_copy(
                src_ref=parts_ref.at[mine], dst_ref=parts_ref.at[mine], send_sem=send_sems.at[1 + j],
                recv_sem=recv_sems.at[1 + j], device_id=(*chip, c), device_id_type=MESH)
            cp.start()
            cps.append(cp)
        for j, (px, py) in enumerate(_other_chips(x, y)):
            cps[j].wait_send()
            theirs = parts_ref.at[2 * px + py]
            pltpu.make_async_remote_copy(
                src_ref=theirs, dst_ref=theirs, send_sem=send_sems.at[1 + j], recv_sem=recv_sems.at[1 + j],
                device_id=(px, py, c), device_id_type=MESH).wait_recv()
        o_ref[half, :] = (parts_ref[0] + parts_ref[1]) + (parts_ref[2] + parts_ref[3])
        back = pltpu.make_async_remote_copy(
            src_ref=o_ref.at[half, :], dst_ref=o_ref.at[half, :], send_sem=send_sems.at[4],
            recv_sem=recv_sems.at[4], device_id=sibling, device_id_type=MESH)
        back.start()
        back.wait_send()
        other = pl.ds(pl.multiple_of((1 - c) * hr, 8), hr)
        pltpu.make_async_remote_copy(
            src_ref=o_ref.at[other, :], dst_ref=o_ref.at[other, :], send_sem=send_sems.at[4],
            recv_sem=recv_sems.at[4], device_id=sibling, device_id_type=MESH).wait_recv()

    return pl.pallas_call(
        body, name="allreduce_small",
        in_specs=[pl.BlockSpec(memory_space=pltpu.VMEM)], out_specs=pl.BlockSpec(memory_space=pltpu.VMEM),
        out_shape=jax.ShapeDtypeStruct((rows, 128), F32),
        scratch_shapes=[pltpu.VMEM((rows, 128), F32), pltpu.VMEM((4, hr, 128), F32),
                        pltpu.SemaphoreType.DMA((5,)), pltpu.SemaphoreType.DMA((5,))],
        compiler_params=pltpu.CompilerParams(has_side_effects=True, vmem_limit_bytes=40 << 20),
    )(pack)


def _ssm_discretize(lam_re, lam_im, log_dt, b_re, b_im):
    dt = jnp.exp(log_dt)[:, None]
    mag = jnp.exp(lam_re * dt)
    ar, ai = mag * jnp.cos(lam_im * dt), mag * jnp.sin(lam_im * dt)
    den = lam_re * lam_re + lam_im * lam_im
    nr, ni = ar - 1.0, ai
    fr, fi = (nr * lam_re + ni * lam_im) / den, (ni * lam_re - nr * lam_im) / den
    bbr = fr[:, None, :] * b_re - fi[:, None, :] * b_im
    bbi = fr[:, None, :] * b_im + fi[:, None, :] * b_re
    return ar, ai, bbr, bbi


def _cmul(a, b):
    return a[0] * b[0] - a[1] * b[1], a[0] * b[1] + a[1] * b[0]


def _scan_powers(ar, ai):
    a = (ar.reshape(1, SW), ai.reshape(1, SW))
    big = a
    for _ in range(SEG.bit_length() - 1):
        big = _cmul(big, big)
    assert 1 << (SEG.bit_length() - 1) == SEG
    pows = {1: big}
    pows[2] = _cmul(big, big)
    pows[4] = _cmul(pows[2], pows[2])
    rid = jnp.arange(8)[:, None]
    ones = jnp.ones((8, 1), F32)
    fwd, rev = [ones * a[0], ones * a[1]], [ones * a[0], -ones * a[1]]
    for k in (1, 2, 4):
        pr, pi = pows[k]
        fwd += [jnp.where(rid >= k, pr, 0.0), jnp.where(rid >= k, pi, 0.0)]
        rev += [jnp.where(rid < 8 - k, pr, 0.0), jnp.where(rid < 8 - k, -pi, 0.0)]
    return jnp.concatenate(fwd + rev, axis=0)


def _pack_b(bb):
    t = bb.reshape(NLC, 8, GCH, NSTATE)
    return jnp.einsum("jgcp,gh->jgchp", t, jnp.eye(8, dtype=bb.dtype)).reshape(NLC, 128, LC)


def _unpack_b(db):
    t = jnp.einsum("jgchp,gh->jgcp", db.reshape(NLC, 8, GCH, 8, NSTATE), jnp.eye(8, dtype=db.dtype))
    return t.reshape(GROUPS, GCH, NSTATE)


def _pack_c(cc):
    t = cc.reshape(NLC, 8, GCH, NSTATE).transpose(0, 1, 3, 2)
    return jnp.einsum("jgpc,gh->jgphc", t, jnp.eye(8, dtype=cc.dtype)).reshape(NLC, LC, 128)


def _unpack_c(dc):
    t = jnp.einsum("jgphc,gh->jgpc", dc.reshape(NLC, 8, NSTATE, 8, GCH), jnp.eye(8, dtype=dc.dtype))
    return t.transpose(0, 1, 3, 2).reshape(GROUPS, GCH, NSTATE)


SMALL = ("norm_mix", "q_norm", "k_norm", "attn_sinks", "lam_re", "lam_im", "log_dt", "ssm_b_re", "ssm_b_im",
         "ssm_c_re", "ssm_c_im", "ssm_d", "attn_branch_norm", "ssm_branch_norm", "norm_ffn")


def _pack_small(arrs, rows=None):
    flat = jnp.concatenate([a.reshape(-1).astype(F32) for a in arrs])
    n = flat.shape[0]
    if rows is None:
        rows = -(-n // (128 * 256)) * 256
    return jnp.pad(flat, (0, rows * 128 - n)).reshape(rows, 128)


def _unpack_small(pack, like):
    flat = pack.reshape(-1)
    out, off = [], 0
    for a in like:
        out.append(flat[off:off + a.size].reshape(a.shape))
        off += a.size
    return out


def _tie(a, token):
    return a if token is None else a + token[0, 0]


class _NoExchange:
    def __init__(self, w):
        self.w = w
        self.grads = {}

    def begin(self, after):
        return None

    def rest_weights(self, after):
        return self.w

    def start(self, group, idxs, grads):
        self.grads.update(zip(idxs, [g for g, _ in grads]))
        return None

    def finish(self, group, after):
        pass


class _Exchange:
    def __init__(self, rest_shards, place):
        self.rest_shards = rest_shards
        self.place = place
        self.rest_dims = [s[1:] for s in SHARDED[1:]]
        self.pending = {}
        self.halves = {}
        self.full = {}
        self.parts = {}

    def begin(self, after):
        copies, arrivals = _gather_copies(self.rest_dims)
        lands = [lax.empty((r, c), BF) for r, c, _ in self.rest_dims]
        send, recv, srcs, lands, token = _split_copy_start(
            "gather_start", copies, 7 * len(lands), self.rest_shards, lands, after)
        self.pending["gather"] = (copies, arrivals, send, recv, srcs, lands)
        return token

    def rest_weights(self, after):
        copies, arrivals, send, recv, srcs, lands = self.pending.pop("gather")
        _, fulls = _split_copy_wait("gather_wait", copies, arrivals, send, recv, srcs, lands, after)
        return dict(zip([s[0] for s in SHARDED[1:]], fulls))

    def _halves(self, group, idxs, grads):
        got = pair_exchange("pair_exchange_" + group, grads, idxs)
        return [pair_sum("pair_sum_" + SHARDED[i][0], self.place, g, b, SHARDED[i][3])
                for i, g, b in zip(idxs, grads, got)]

    def start(self, group, idxs, grads):
        if grads[0][1] is None:
            halves = self._halves(group, idxs, [g for g, _ in grads])
            for i, h, pt in zip(idxs, halves, scatter_grads(halves, idxs)):
                self.halves[i], self.parts[i] = h, pt
            return None
        copies, arrivals = _scatter8_copies(idxs)
        lands = [lax.empty((7,) + sh, BF) for sh in _piece_shapes(idxs)]
        send, recv, srcs, lands, token = _split_copy_start(
            "scatter_start_" + group, copies, 7 * len(idxs), [g16 for _, g16 in grads], lands, grads[0][0])
        self.pending[group] = (copies, arrivals, send, recv, srcs, lands, idxs)
        for i, (g32, _) in zip(idxs, grads):
            self.full[i] = g32
        return token

    def finish(self, group, after):
        if group not in self.pending:
            return
        copies, arrivals, send, recv, srcs, lands, idxs = self.pending.pop(group)
        _, lands = _split_copy_wait("scatter_wait_" + group, copies, arrivals, send, recv, srcs, lands, after)
        for i, pt in zip(idxs, lands):
            self.parts[i] = pt


def local_step(x, tgt, meta_full, p, w_in, ex):
    mpad = jnp.concatenate([jnp.zeros((META0, D), F32), meta_full], axis=0)
    qn_t = jnp.tile(p["q_norm"].reshape(1, HEAD), (1, NQ)) * HEAD ** -0.5
    kn_t = jnp.tile(p["k_norm"].reshape(1, HEAD), (1, NKV))
    seg = jnp.arange(256) // HEAD
    bd = (seg[:, None] == seg[None, :]).astype(BF)
    nm = p["norm_mix"].reshape(1, D)
    sinks = p["attn_sinks"].reshape(NQ)
    dsk = p["ssm_d"].reshape(1, D)
    abn, sbn, gf = (p[k].reshape(1, D) for k in ("attn_branch_norm", "ssm_branch_norm", "norm_ffn"))

    disc_in = (p["lam_re"], p["lam_im"], p["log_dt"], p["ssm_b_re"], p["ssm_b_im"])
    (ar, ai, bbr, bbi), disc_vjp = jax.vjp(_ssm_discretize, *disc_in)
    pw = _scan_powers(ar, ai)
    brp, bip = _pack_b(bbr).astype(BF), _pack_b(bbi).astype(BF)
    crp, cip = _pack_c(p["ssm_c_re"]).astype(BF), _pack_c(p["ssm_c_im"]).astype(BF)

    token = ex.begin(w_in)
    xn, qr, kr, qn, kn, v, u, gg = in_proj_fwd(x, mpad, w_in, _tie(nm, token), qn_t, kn_t, bd)
    attn = attention_fwd(sinks, qn, kn, v)
    perm = _segment_order()
    y, sr, si = ssm_fwd(u, pw, brp, bip, crp, cip, dsk, perm)
    w = ex.rest_weights(y)
    zb, zzb, mgb, h1 = mid_fwd(y, attn, gg, x, mpad, w["w_glu"], w["w_out"], abn, sbn)
    hnb, dgub, actb, dh2b, dh1, dh1b, dgf, loss = ffn_fwd_bwd(h1, tgt, w["w_ffn_in"], w["w_ffn_out"], gf)
    token = ex.start("ffn", (3, 4), [weight_grad("grad_w_ffn_in", hnb, dgub, 1408, True),
                                     weight_grad("grad_w_ffn_out", actb, dh2b, 512, True)])
    dy, dattn, dggb, dzzb, dabn, dsbn = mid_bwd(dh1b, y, attn, gg, zzb, w["w_glu"], w["w_out"], _tie(abn, token),
                                                sbn)
    grads_mid = [weight_grad("grad_w_glu", zb, dzzb, 1024, True), weight_grad("grad_w_out", mgb, dh1b, 1024, True)]
    dub, da, dbr, dbi, dcr, dci, dd = ssm_bwd(dy, u, sr, si, pw, brp, bip, crp, cip, dsk, perm)
    ex.finish("ffn", dub)
    token = ex.start("mid", (1, 2), grads_mid)
    dqn, dkn, dv, dsink = attention_bwd(_tie(sinks, token), qn, kn, v, dattn)
    dproj, dx, dmpad, dnm, dqw, dkw = in_proj_bwd(dqn, dkn, dv, dub, dggb, qr, kr, x, mpad, dh1, w_in, nm, qn_t,
                                                  kn_t, bd)
    ex.finish("mid", dproj)
    ex.start("in", (0,), [(weight_grad("grad_w_in", xn, dproj, 1536), None)])

    dar = jnp.sum(da[0:8], axis=0).reshape(GROUPS, NSTATE)
    dai = jnp.sum(da[8:16], axis=0).reshape(GROUPS, NSTATE)
    dlr, dli, dldt, dbre, dbim = disc_vjp((dar, dai, _unpack_b(dbr), _unpack_b(dbi)))
    small = {
        "norm_mix": dnm, "q_norm": dqw.reshape(NQ, HEAD).sum(0) * HEAD ** -0.5, "k_norm": dkw.reshape(NKV, HEAD).sum(0),
        "attn_sinks": dsink[0, 0:NQ], "lam_re": dlr, "lam_im": dli, "log_dt": dldt,
        "ssm_b_re": dbre, "ssm_b_im": dbim, "ssm_c_re": _unpack_c(jnp.swapaxes(dcr, 1, 2)), "ssm_c_im": _unpack_c(jnp.swapaxes(dci, 1, 2)),
        "ssm_d": dd, "attn_branch_norm": dabn, "ssm_branch_norm": dsbn, "norm_ffn": dgf,
    }
    return loss[0, 0], dx, dmpad[META0:PAD], small


def kernel(x, meta_tokens, norm_mix, w_in, q_norm, k_norm, attn_sinks, lam_re, lam_im, log_dt, ssm_b_re, ssm_b_im, ssm_c_re, ssm_c_im, ssm_d, w_glu, attn_branch_norm, ssm_branch_norm, w_out, norm_ffn, w_ffn_in, w_ffn_out, loss_target, m_meta_tokens, m_norm_mix, m_w_in, m_q_norm, m_k_norm, m_attn_sinks, m_lam_re, m_lam_im, m_log_dt, m_ssm_b_re, m_ssm_b_im, m_ssm_c_re, m_ssm_c_im, m_ssm_d, m_w_glu, m_attn_branch_norm, m_ssm_branch_norm, m_w_out, m_norm_ffn, m_w_ffn_in, m_w_ffn_out, v_meta_tokens, v_norm_mix, v_w_in, v_q_norm, v_k_norm, v_attn_sinks, v_lam_re, v_lam_im, v_log_dt, v_ssm_b_re, v_ssm_b_im, v_ssm_c_re, v_ssm_c_im, v_ssm_d, v_w_glu, v_attn_branch_norm, v_ssm_branch_norm, v_w_out, v_norm_ffn, v_w_ffn_in, v_w_ffn_out):
    args = dict(locals())
    names = ["meta_tokens", "norm_mix", "w_in", "q_norm", "k_norm", "attn_sinks", "lam_re", "lam_im", "log_dt",
             "ssm_b_re", "ssm_b_im", "ssm_c_re", "ssm_c_im", "ssm_d", "w_glu", "attn_branch_norm",
             "ssm_branch_norm", "w_out", "norm_ffn", "w_ffn_in", "w_ffn_out"]
    big_names = [s[0] for s in SHARDED]
    xs, ys, cs = _place()
    chip = 2 * xs + ys

    shards = [args[n][0].astype(BF) for n in big_names]
    w_in_full, meta_full = gather_weights([shards[0], meta_tokens], [SHARDED[0][1:], (N_META, D, 1)])
    place = jnp.stack([chip, cs]).astype(jnp.int32)
    ex = _Exchange(shards[1:], place)

    swapped = ("ssm_b_re", "ssm_b_im")

    def natural(n, a):
        return jnp.swapaxes(a, -1, -2) if n in swapped else a

    p = {n: natural(n, args[n][0]) for n in SMALL}
    loss, dx, dmeta, gsmall = local_step(x[0], loss_target[0], meta_full, p, w_in_full, ex)

    out = {}

    def finish(idxs, name):
        mine = [sum8("sum8_" + SHARDED[i][0], place, ex.parts[i], ex.full[i], SHARDED[i][3]) if i in ex.full
                else sum4("sum4_" + SHARDED[i][0], place, ex.parts[i], ex.halves[i], SHARDED[i][3]) for i in idxs]
        for i, g in zip(idxs, join_halves(name, mine, idxs)):
            n = SHARDED[i][0]
            out[n] = tuple(t[None] for t in adamw("adamw_" + n, args[n][0], g, args["m_" + n][0], args["v_" + n][0]))

    finish((0, 1, 2, 3, 4), "join_halves")
    small_list = [gsmall[n].reshape(natural(n, args[n]).shape) for n in SMALL] + [dmeta, loss.reshape(1, 1)]
    red = allreduce_small(_pack_small(small_list))
    *small_red, dmeta_sum, loss = _unpack_small(red, small_list)
    loss = loss[0, 0]
    gmeta = lax.dynamic_slice_in_dim(dmeta_sum, chip * (D // 4), D // 4, axis=1)
    out["meta_tokens"] = tuple(adamw("adamw_meta", meta_tokens, gmeta, m_meta_tokens, v_meta_tokens))
    ds, nms, nvs = adamw_small([natural(n, args[n]) for n in SMALL], small_red,
                               [natural(n, args["m_" + n]) for n in SMALL],
                               [natural(n, args["v_" + n]) for n in SMALL])
    for n, g_, d_, m_, v_ in zip(SMALL, small_red, ds, nms, nvs):
        out[n] = tuple(natural(n, t) for t in (g_, d_, m_, v_))

    res = [loss, dx[None]]
    for k in range(4):
        res += [out[n][k] for n in names]
    return tuple(res)
```

```python
import functools
import math

import jax
import jax.numpy as jnp
from jax import lax
from jax.experimental import pallas as pl
from jax.experimental.pallas import tpu as pltpu

F32 = jnp.float32
BF = jnp.bfloat16

D = 1024
N_META = 16
HEAD = 64
NQ = 16
NKV = 4
QW = NQ * HEAD
KVW = NKV * HEAD
WIN = 128
GROUPS = 64
GCH = 16
NSTATE = 64
SW = GROUPS * NSTATE
DFF = 2816
IN_COLS = QW + 2 * KVW + D + 2 * D
PAD = 256
META0 = PAD - N_META
EPS = 1e-6
NEG = -1e30

TM = 256
TS = 256
LC = 512
NLC = SW // LC

LR, B1, B2, AEPS, WD, STEP = 0.001, 0.9, 0.999, 1e-08, 0.01, 10
BC1 = 1.0 - B1 ** STEP
BC2 = 1.0 - B2 ** STEP

MESH = pl.DeviceIdType.MESH
ANY = pl.BlockSpec(memory_space=pl.ANY)


def _cp(sem=None, vmem_mb=48):
    return pltpu.CompilerParams(dimension_semantics=sem, vmem_limit_bytes=vmem_mb << 20)


def _full(shape):
    n = len(shape)
    return pl.BlockSpec(shape, lambda *a: (0,) * n)


def _const(shape):
    n = len(shape)
    return pl.BlockSpec(shape, lambda *a: (0,) * n, pipeline_mode=pl.Buffered(1))


def _rows(tm, width):
    return pl.BlockSpec((tm, width), lambda i: (i, 0))


def _dot(a, b):
    return jnp.dot(a, b, preferred_element_type=F32)


def _dot_nt(a, b):
    return lax.dot_general(a, b, (((1,), (1,)), ((), ())), preferred_element_type=F32)


def _dot_tn(a, b):
    return lax.dot_general(a, b, (((0,), (0,)), ((), ())), preferred_element_type=F32)


def _sigmoid(x):
    return 0.5 * jnp.tanh(0.5 * x) + 0.5


def _seg_mean(x, bd):
    outs = []
    for j in range(x.shape[1] // 256):
        c = x[:, 256 * j:256 * (j + 1)]
        hi = c.astype(BF)
        lo = (c - hi.astype(F32)).astype(BF)
        outs.append(_dot(hi, bd) + _dot(lo, bd))
    out = outs[0] if len(outs) == 1 else jnp.concatenate(outs, axis=1)
    return out * (1.0 / HEAD)


_GC = math.sqrt(2.0 / math.pi)


def _gelu(x):
    t = jnp.tanh(_GC * (x + 0.044715 * x * x * x))
    return 0.5 * x * (1.0 + t), t


def _gelu_grad(x, t):
    return 0.5 * (1.0 + t) + 0.5 * x * (1.0 - t * t) * _GC * (1.0 + 3.0 * 0.044715 * x * x)


def _seq_specs():
    assert TM == PAD
    return [pl.BlockSpec((TM, D), lambda i: (jnp.maximum(i - 1, 0), 0)), _full((PAD, D))]


def _seq_tile(x_ref, m_ref):
    return jnp.where(pl.program_id(0) == 0, m_ref[...], x_ref[...])


def in_proj_fwd(x, mpad, w_in, nm, qn_t, kn_t, bd):
    lp = x.shape[0] + PAD

    def body(x_ref, m_ref, w_ref, nm_ref, qn_ref, kn_ref, bd_ref,
             xn_o, qr_o, kr_o, qn_o, kn_o, v_o, u_o, gg_o):
        h = _seq_tile(x_ref, m_ref)
        r = lax.rsqrt(jnp.mean(h * h, axis=-1, keepdims=True) + EPS)
        xb = ((h * r) * nm_ref[...]).astype(BF)
        xn_o[...] = xb
        bdv = bd_ref[...]
        q = _dot(xb, w_ref[:, 0:QW])
        qr_o[...] = q
        qn_o[...] = (q * lax.rsqrt(_seg_mean(q * q, bdv) + EPS) * qn_ref[...]).astype(BF)
        k = _dot(xb, w_ref[:, QW:QW + KVW])
        kr_o[...] = k
        kn_o[...] = (k * lax.rsqrt(_seg_mean(k * k, bdv) + EPS) * kn_ref[...]).astype(BF)
        v_o[...] = _dot(xb, w_ref[:, QW + KVW:QW + 2 * KVW]).astype(BF)
        u_o[...] = _dot(xb, w_ref[:, QW + 2 * KVW:QW + 2 * KVW + D])
        gg_o[...] = _dot(xb, w_ref[:, QW + 2 * KVW + D:IN_COLS])

    outs = [(D, BF), (QW, F32), (KVW, F32), (QW, BF), (KVW, BF), (KVW, BF), (D, F32), (2 * D, F32)]
    return pl.pallas_call(
        body, name="in_proj_fwd", grid=(lp // TM,),
        in_specs=_seq_specs() + [_full((D, IN_COLS)), _full((1, D)), _full((1, QW)), _full((1, KVW)),
                                 _full((256, 256))],
        out_specs=[_rows(TM, w) for w, _ in outs],
        out_shape=[jax.ShapeDtypeStruct((lp, w), t) for w, t in outs],
        compiler_params=_cp(("arbitrary",)),
    )(x, mpad, w_in, nm, qn_t, kn_t, bd)


def in_proj_bwd(dqn, dkn, dv, du, dgg, qr, kr, x, mpad, dh1, w_in, nm, qn_t, kn_t, bd):
    lp = x.shape[0] + PAD

    def body(dqn_ref, dkn_ref, dv_ref, du_ref, dgg_ref, qr_ref, kr_ref, x_ref, m_ref, dh1_ref,
             w_ref, nm_ref, qn_ref, kn_ref, bd_ref,
             dproj_o, dx_o, dm_o, dnm_o, dqw_o, dkw_o):
        i = pl.program_id(0)

        @pl.when(i == 0)
        def _():
            dnm_o[...] = jnp.zeros_like(dnm_o)
            dqw_o[...] = jnp.zeros_like(dqw_o)
            dkw_o[...] = jnp.zeros_like(dkw_o)

        bdv = bd_ref[...]

        def norm_bwd(x, dy, g, acc_ref):
            rr = lax.rsqrt(_seg_mean(x * x, bdv) + EPS)
            xh = x * rr
            acc_ref[...] += jnp.sum(dy * xh, axis=0, keepdims=True)
            dxh = dy * g
            return rr * (dxh - xh * _seg_mean(dxh * xh, bdv))

        dq = norm_bwd(qr_ref[...], dqn_ref[...], qn_ref[...], dqw_o)
        dk = norm_bwd(kr_ref[...], dkn_ref[...], kn_ref[...], dkw_o)
        dproj_o[:, 0:QW] = dq.astype(BF)
        dproj_o[:, QW:QW + KVW] = dk.astype(BF)
        dproj_o[:, QW + KVW:QW + 2 * KVW] = dv_ref[...].astype(BF)
        dproj_o[:, QW + 2 * KVW:QW + 2 * KVW + D] = du_ref[...]
        dproj_o[:, QW + 2 * KVW + D:IN_COLS] = dgg_ref[...]
        dxn = _dot_nt(dproj_o[...], w_ref[...])
        h = _seq_tile(x_ref, m_ref)
        r = lax.rsqrt(jnp.mean(h * h, axis=-1, keepdims=True) + EPS)
        xh = h * r
        dnm_o[...] += jnp.sum(dxn * xh, axis=0, keepdims=True)
        dxh = dxn * nm_ref[...]
        dh0 = dh1_ref[...] + r * (dxh - xh * jnp.mean(dxh * xh, axis=-1, keepdims=True))
        dx_o[...] = dh0

        @pl.when(i == 0)
        def _():
            dm_o[...] = dh0

    return pl.pallas_call(
        body, name="in_proj_bwd", grid=(lp // TM,),
        in_specs=[_rows(TM, QW), _rows(TM, KVW), _rows(TM, KVW), _rows(TM, D), _rows(TM, 2 * D),
                  _rows(TM, QW), _rows(TM, KVW)] + _seq_specs() + [_rows(TM, D),
                  _full((D, IN_COLS)), _full((1, D)), _full((1, QW)), _full((1, KVW)), _full((256, 256))],
        out_specs=[_rows(TM, IN_COLS), _seq_specs()[0], _full((PAD, D)), _full((1, D)), _full((1, QW)),
                   _full((1, KVW))],
        out_shape=[jax.ShapeDtypeStruct((lp, IN_COLS), BF), jax.ShapeDtypeStruct((lp - PAD, D), F32),
                   jax.ShapeDtypeStruct((PAD, D), F32),
                   jax.ShapeDtypeStruct((1, D), F32), jax.ShapeDtypeStruct((1, QW), F32),
                   jax.ShapeDtypeStruct((1, KVW), F32)],
        compiler_params=_cp(("arbitrary",)),
    )(dqn, dkn, dv, du, dgg, qr, kr, x, mpad, dh1, w_in, nm, qn_t, kn_t, bd)


def _att_bias(b):
    qi = lax.broadcasted_iota(jnp.int32, (WIN, WIN), 0)
    kj = lax.broadcasted_iota(jnp.int32, (WIN, WIN), 1)
    upper = kj > qi
    valid = (upper & ((b - 1) * WIN + kj >= PAD)) | (jnp.logical_not(upper) & (b * WIN + kj >= META0))
    bias_pc = jnp.where(valid, 0.0, NEG)
    kj1 = lax.broadcasted_iota(jnp.int32, (1, WIN), 1)
    bias_m = jnp.where((kj1 >= WIN - N_META) & (b * WIN >= PAD), 0.0, NEG)
    return bias_m, bias_pc, upper


def _dup_half(ref, kp, pos, lo):
    pair = ref[:, 128 * kp:128 * kp + 128].astype(F32)
    rolled = pltpu.roll(pair, 64, axis=1)
    keep = lo if pos == 0 else jnp.logical_not(lo)
    return jnp.where(keep, pair, rolled).astype(BF)


def _stack_heads(ref, kv, lo):
    parts = []
    for pp in range(2):
        pair = ref[:, 256 * kv + 128 * pp:256 * kv + 128 * pp + 128]
        zero = jnp.zeros_like(pair)
        parts.append(jnp.where(lo, pair, zero))
        parts.append(jnp.where(lo, zero, pair))
    return jnp.concatenate(parts, axis=0)


def _unstack_heads(x4, lo):
    a = jnp.where(lo, x4[0:WIN], x4[WIN:2 * WIN])
    b = jnp.where(lo, x4[2 * WIN:3 * WIN], x4[3 * WIN:4 * WIN])
    return a, b


def _split_pc(x, upper):
    zero = jnp.zeros_like(x)
    return jnp.where(upper, x, zero).astype(BF), jnp.where(upper, zero, x).astype(BF)


def _head_softmax(s_m, s_p, s_c, bias, sink):
    bias_m, bias_pc, upper = bias
    sm = s_m + bias_m
    spc = jnp.where(upper, s_p, s_c) + bias_pc
    m = jnp.maximum(jnp.max(jnp.maximum(sm, spc), axis=-1, keepdims=True), sink)
    em, epc = jnp.exp(sm - m), jnp.exp(spc - m)
    esink = jnp.exp(sink - m)
    inv = 1.0 / (esink + jnp.sum(em + epc, axis=-1, keepdims=True))
    return em, epc, inv, esink


def _kv_specs():
    return [pl.BlockSpec((WIN, KVW), lambda b: (1, 0)),
            pl.BlockSpec((WIN, KVW), lambda b: (jnp.maximum(b - 1, 0), 0)),
            pl.BlockSpec((WIN, KVW), lambda b: (b, 0))]


def attention_fwd(sinks, qn, kn, v):
    lp = qn.shape[0]

    def body(sink_ref, q_ref, km_ref, kp_ref, kc_ref, vm_ref, vp_ref, vc_ref, o_ref):
        b = pl.program_id(0)
        bias = _att_bias(b)
        lo = lax.broadcasted_iota(jnp.int32, (WIN, WIN), 1) < HEAD
        for kv in range(NKV):
            kp, pos = kv // 2, kv % 2
            kds = [_dup_half(r, kp, pos, lo) for r in (km_ref, kp_ref, kc_ref)]
            vds = [_dup_half(r, kp, pos, lo) for r in (vm_ref, vp_ref, vc_ref)]
            q4 = _stack_heads(q_ref, kv, lo)
            ss = [_dot_nt(q4, kd) for kd in kds]
            es, invs = ([], [], []), []
            for h in range(4):
                rs = slice(WIN * h, WIN * h + WIN)
                em, epc, inv, _ = _head_softmax(ss[0][rs], ss[1][rs], ss[2][rs], bias, sink_ref[4 * kv + h])
                e_p, e_c = _split_pc(epc, bias[2])
                for lst, e in zip(es, (em.astype(BF), e_p, e_c)):
                    lst.append(e)
                invs.append(inv)
            e_m, e_p, e_c = (jnp.concatenate(lst, axis=0) for lst in es)
            o4 = (_dot(e_m, vds[0]) + _dot(e_p, vds[1]) + _dot(e_c, vds[2])) * jnp.concatenate(invs, axis=0)
            oa, ob = _unstack_heads(o4, lo)
            o_ref[:, 256 * kv:256 * kv + 128] = oa
            o_ref[:, 256 * kv + 128:256 * kv + 256] = ob

    return pl.pallas_call(
        body, name="attention_fwd", grid=(lp // WIN,),
        in_specs=[pl.BlockSpec(memory_space=pltpu.SMEM), _rows(WIN, QW)] + _kv_specs() + _kv_specs(),
        out_specs=_rows(WIN, QW),
        out_shape=jax.ShapeDtypeStruct((lp, QW), F32),
        compiler_params=_cp(("arbitrary",)),
    )(sinks, qn, kn, kn, kn, v, v, v)


def attention_bwd(sinks, qn, kn, v, do):
    lp = qn.shape[0]
    nb = lp // WIN

    def body(sink_ref, q_ref, km_ref, kp_ref, kc_ref, vm_ref, vp_ref, vc_ref, do_ref,
             dq_ref, dk_hbm, dv_hbm, dsink_ref, dk_acc, dv_acc):
        b = pl.program_id(0)

        @pl.when(b == 0)
        def _():
            dk_acc[...] = jnp.zeros_like(dk_acc)
            dv_acc[...] = jnp.zeros_like(dv_acc)
            dsink_ref[...] = jnp.zeros_like(dsink_ref)

        bias = _att_bias(b)
        upper = bias[2]
        lo = lax.broadcasted_iota(jnp.int32, (WIN, WIN), 1) < HEAD
        lane8 = lax.broadcasted_iota(jnp.int32, (8, 128), 1)
        starts = [WIN, pl.multiple_of(jnp.maximum(b - 1, 0) * WIN, WIN), pl.multiple_of(b * WIN, WIN)]
        for kv in range(NKV):
            kp, pos = kv // 2, kv % 2
            keep = lo if pos == 0 else jnp.logical_not(lo)
            kds = [_dup_half(r, kp, pos, lo) for r in (km_ref, kp_ref, kc_ref)]
            vds = [_dup_half(r, kp, pos, lo) for r in (vm_ref, vp_ref, vc_ref)]
            q4 = _stack_heads(q_ref, kv, lo)
            do4 = _stack_heads(do_ref, kv, lo)
            ss = [_dot_nt(q4, kd) for kd in kds]
            dps = [_dot_nt(do4, vd) for vd in vds]
            ds_l, p_l = ([], [], []), ([], [], [])
            for h in range(4):
                rs = slice(WIN * h, WIN * h + WIN)
                em, epc, inv, esink = _head_softmax(ss[0][rs], ss[1][rs], ss[2][rs], bias, sink_ref[4 * kv + h])
                p_m, p_pc = em * inv, epc * inv
                dp_m = dps[0][rs]
                dp_pc = jnp.where(upper, dps[1][rs], dps[2][rs])
                delta = jnp.sum(p_m * dp_m + p_pc * dp_pc, axis=-1, keepdims=True)
                val = -jnp.sum(esink * inv * delta, axis=0, keepdims=True)
                dsink_ref[...] += jnp.where(lane8 == 4 * kv + h, val, 0.0)
                for lst, t in zip(ds_l, ((p_m * (dp_m - delta)).astype(BF),) + _split_pc(p_pc * (dp_pc - delta), upper)):
                    lst.append(t)
                for lst, t in zip(p_l, (p_m.astype(BF),) + _split_pc(p_pc, upper)):
                    lst.append(t)
            dss = [jnp.concatenate(lst, axis=0) for lst in ds_l]
            ps = [jnp.concatenate(lst, axis=0) for lst in p_l]
            dq4 = _dot(dss[0], kds[0]) + _dot(dss[1], kds[1]) + _dot(dss[2], kds[2])
            dqa, dqb = _unstack_heads(dq4, lo)
            dq_ref[:, 256 * kv:256 * kv + 128] = dqa
            dq_ref[:, 256 * kv + 128:256 * kv + 256] = dqb
            for x in range(3):
                dkd = _dot_tn(dss[x], q4)
                dvd = _dot_tn(ps[x], do4)
                dkd = jnp.where(keep, dkd + pltpu.roll(dkd, 64, axis=1), 0.0)
                dvd = jnp.where(keep, dvd + pltpu.roll(dvd, 64, axis=1), 0.0)
                dk_acc[pl.ds(starts[x], WIN), 128 * kp:128 * kp + 128] += dkd
                dv_acc[pl.ds(starts[x], WIN), 128 * kp:128 * kp + 128] += dvd

        @pl.when(b == nb - 1)
        def _():
            pltpu.sync_copy(dk_acc, dk_hbm)
            pltpu.sync_copy(dv_acc, dv_hbm)

    return pl.pallas_call(
        body, name="attention_bwd", grid=(nb,),
        in_specs=[pl.BlockSpec(memory_space=pltpu.SMEM), _rows(WIN, QW)] + _kv_specs() + _kv_specs()
                 + [_rows(WIN, QW)],
        out_specs=[_rows(WIN, QW), ANY, ANY, _full((8, 128))],
        out_shape=[jax.ShapeDtypeStruct((lp, QW), F32), jax.ShapeDtypeStruct((lp, KVW), F32),
                   jax.ShapeDtypeStruct((lp, KVW), F32), jax.ShapeDtypeStruct((8, 128), F32)],
        scratch_shapes=[pltpu.VMEM((lp, KVW), F32), pltpu.VMEM((lp, KVW), F32)],
        compiler_params=_cp(("arbitrary",)),
    )(sinks, qn, kn, kn, kn, v, v, v, do)


PW_ROWS = 64
SEG = TS // 8


def _segment_order():
    rho = jnp.arange(TS)
    token = SEG * (rho % 8) + rho // 8
    p = (token[:, None] == jnp.arange(TS)[None, :]).astype(BF)
    return p, p.T


def _reorder(p, x):
    hi = x.astype(BF)
    r1 = x - hi.astype(F32)
    mid = r1.astype(BF)
    lo = (r1 - mid.astype(F32)).astype(BF)
    return _dot(p, hi) + _dot(p, mid) + _dot(p, lo)


def _segment_scan(xr_ref, xi_ref, pw_ref, base, ls, c_r, c_i, reverse, visit=None):
    ar, ai = pw_ref[base:base + 8, ls], pw_ref[base + 8:base + 16, ls]
    order = list(range(SEG - 1, -1, -1)) if reverse else list(range(SEG))
    s_r = s_i = jnp.zeros_like(ar)
    for i in order:
        rows = pl.ds(8 * i, 8)
        s_r, s_i = ar * s_r - ai * s_i + xr_ref[rows, ls], ar * s_i + ai * s_r + xi_ref[rows, ls]
        xr_ref[rows, ls] = s_r
        xi_ref[rows, ls] = s_i
    rid = lax.broadcasted_iota(jnp.int32, s_r.shape, 0)
    if reverse:
        e_r = jnp.where(rid < 7, pltpu.roll(s_r, 7, axis=0), c_r)
        e_i = jnp.where(rid < 7, pltpu.roll(s_i, 7, axis=0), c_i)
    else:
        e_r = jnp.where(rid >= 1, pltpu.roll(s_r, 1, axis=0), c_r)
        e_i = jnp.where(rid >= 1, pltpu.roll(s_i, 1, axis=0), c_i)
    for n, k in enumerate((1, 2, 4)):
        shift = 8 - k if reverse else k
        t_r, t_i = pltpu.roll(e_r, shift, axis=0), pltpu.roll(e_i, shift, axis=0)
        kr = pw_ref[base + 16 + 16 * n:base + 24 + 16 * n, ls]
        ki = pw_ref[base + 24 + 16 * n:base + 32 + 16 * n, ls]
        e_r, e_i = e_r + kr * t_r - ki * t_i, e_i + kr * t_i + ki * t_r
    last = 0 if reverse else 7
    a16r, a16i = pw_ref[base + 16 + last:base + 17 + last, ls], pw_ref[base + 24 + last:base + 25 + last, ls]
    lr, li, fr, fi = (t[last:last + 1] for t in (e_r, e_i, s_r, s_i))
    out = (a16r * lr - a16i * li + fr, a16r * li + a16i * lr + fi)
    d_r, d_i = e_r, e_i
    extra = None
    for i in order:
        rows = pl.ds(8 * i, 8)
        d_r, d_i = ar * d_r - ai * d_i, ar * d_i + ai * d_r
        f_r, f_i = xr_ref[rows, ls] + d_r, xi_ref[rows, ls] + d_i
        xr_ref[rows, ls] = f_r
        xi_ref[rows, ls] = f_i
        if visit is not None:
            extra = visit(i, f_r, f_i, extra)
    return out if visit is None else (out, extra)


def ssm_fwd(u, pw, br, bi, cr, ci, dsk, perm):
    lp = u.shape[0]

    def body(u_ref, pw_ref, br_ref, bi_ref, cr_ref, ci_ref, d_ref, p_ref, pt_ref,
             y_ref, sr_ref, si_ref, car_r, car_i, yp_s):
        i = pl.program_id(0)

        @pl.when(i == 0)
        def _():
            car_r[...] = jnp.zeros_like(car_r)
            car_i[...] = jnp.zeros_like(car_i)

        u = u_ref[...]
        ub = _dot(p_ref[...], u.astype(BF)).astype(BF)
        for j in range(NLC):
            uj = ub[:, 128 * j:128 * j + 128]
            sr_ref[:, LC * j:LC * j + LC] = _dot(uj, br_ref[j])
            si_ref[:, LC * j:LC * j + LC] = _dot(uj, bi_ref[j])
        for j in range(NLC):
            ls = slice(LC * j, LC * j + LC)
            car_r[:, ls], car_i[:, ls] = _segment_scan(sr_ref, si_ref, pw_ref, 0, ls, car_r[:, ls], car_i[:, ls],
                                                       False)
        for j in range(NLC):
            ls = slice(LC * j, LC * j + LC)
            yp_s[:, 128 * j:128 * j + 128] = (_dot(sr_ref[:, ls].astype(BF), cr_ref[j])
                                              - _dot(si_ref[:, ls].astype(BF), ci_ref[j]))
        y_ref[...] = _reorder(pt_ref[...], yp_s[...]) + d_ref[...] * u

    p, pt = perm
    return pl.pallas_call(
        body, name="ssm_fwd", grid=(lp // TS,),
        in_specs=[_rows(TS, D), _full((2 * PW_ROWS, SW)), _full((NLC, 128, LC)), _full((NLC, 128, LC)),
                  _full((NLC, LC, 128)), _full((NLC, LC, 128)), _full((1, D)), _full((TS, TS)), _full((TS, TS))],
        out_specs=[_rows(TS, D), _rows(TS, SW), _rows(TS, SW)],
        out_shape=[jax.ShapeDtypeStruct((lp, D), F32), jax.ShapeDtypeStruct((lp, SW), F32),
                   jax.ShapeDtypeStruct((lp, SW), F32)],
        scratch_shapes=[pltpu.VMEM((1, SW), F32), pltpu.VMEM((1, SW), F32), pltpu.VMEM((TS, D), F32)],
        compiler_params=_cp(("arbitrary",)),
    )(u, pw, br, bi, cr, ci, dsk, p, pt)


def ssm_bwd(dy, u, sr, si, pw, br, bi, cr, ci, dsk, perm):
    lp = u.shape[0]
    nt = lp // TS

    def rev(i):
        return (nt - 1 - i, 0)

    def prev8(i):
        return (jnp.maximum((nt - 1 - i) * (TS // 8) - 1, 0), 0)

    def body(dy_ref, u_ref, sr_ref, si_ref, pr8_ref, pi8_ref, pw_ref, br_ref, bi_ref, cr_ref, ci_ref, d_ref,
             p_ref, pt_ref, du_ref, da_ref, dbr_ref, dbi_ref, dcr_ref, dci_ref, dd_ref,
             gr_s, gi_s, car_r, car_i, dup_s):
        i = pl.program_id(0)

        @pl.when(i == 0)
        def _():
            car_r[...] = jnp.zeros_like(car_r)
            car_i[...] = jnp.zeros_like(car_i)
            for ref in (da_ref, dbr_ref, dbi_ref, dcr_ref, dci_ref, dd_ref):
                ref[...] = jnp.zeros_like(ref)

        keep = 1.0 - (i == nt - 1).astype(F32)
        dy = dy_ref[...]
        u = u_ref[...]
        dd_ref[...] += jnp.sum(dy * u, axis=0, keepdims=True)
        pm = p_ref[...]
        dyb = _dot(pm, dy.astype(BF)).astype(BF)
        ub = _dot(pm, u.astype(BF)).astype(BF)
        for j in range(NLC):
            ls = slice(LC * j, LC * j + LC)
            dyj = dyb[:, 128 * j:128 * j + 128]
            gr_s[:, ls] = _dot_nt(dyj, cr_ref[j])
            gi_s[:, ls] = -_dot_nt(dyj, ci_ref[j])
            dcr_ref[j] += _dot_tn(dyj, sr_ref[:, ls].astype(BF))
            dci_ref[j] -= _dot_tn(dyj, si_ref[:, ls].astype(BF))
        rid = lax.broadcasted_iota(jnp.int32, (8, LC), 0)
        for j in range(NLC):
            ls = slice(LC * j, LC * j + LC)

            def accumulate(i, g_r, g_i, acc):
                if i >= 1:
                    rows = pl.ds(8 * (i - 1), 8)
                    p_r, p_i = sr_ref[rows, ls], si_ref[rows, ls]
                else:
                    rows = pl.ds(8 * (SEG - 1), 8)
                    p_r = jnp.where(rid >= 1, pltpu.roll(sr_ref[rows, ls], 1, axis=0), pr8_ref[7:8, ls] * keep)
                    p_i = jnp.where(rid >= 1, pltpu.roll(si_ref[rows, ls], 1, axis=0), pi8_ref[7:8, ls] * keep)
                t_r, t_i = g_r * p_r + g_i * p_i, g_i * p_r - g_r * p_i
                return (t_r, t_i) if acc is None else (acc[0] + t_r, acc[1] + t_i)

            (car_r[:, ls], car_i[:, ls]), (t_r, t_i) = _segment_scan(
                gr_s, gi_s, pw_ref, PW_ROWS, ls, car_r[:, ls], car_i[:, ls], True, accumulate)
            da_ref[0:8, ls] += t_r
            da_ref[8:16, ls] += t_i
        for j in range(NLC):
            ls = slice(LC * j, LC * j + LC)
            uj = ub[:, 128 * j:128 * j + 128]
            grb = gr_s[:, ls].astype(BF)
            gib = gi_s[:, ls].astype(BF)
            dup_s[:, 128 * j:128 * j + 128] = _dot_nt(grb, br_ref[j]) + _dot_nt(gib, bi_ref[j])
            dbr_ref[j] += _dot_tn(uj, grb)
            dbi_ref[j] += _dot_tn(uj, gib)
        du_ref[...] = (_reorder(pt_ref[...], dup_s[...]) + d_ref[...] * dy).astype(BF)

    p, pt = perm
    return pl.pallas_call(
        body, name="ssm_bwd", grid=(nt,),
        in_specs=[pl.BlockSpec((TS, D), rev), pl.BlockSpec((TS, D), rev), pl.BlockSpec((TS, SW), rev),
                  pl.BlockSpec((TS, SW), rev), pl.BlockSpec((8, SW), prev8), pl.BlockSpec((8, SW), prev8),
                  _full((2 * PW_ROWS, SW)), _full((NLC, 128, LC)), _full((NLC, 128, LC)),
                  _full((NLC, LC, 128)), _full((NLC, LC, 128)), _full((1, D)), _full((TS, TS)), _full((TS, TS))],
        out_specs=[pl.BlockSpec((TS, D), rev), _full((16, SW)), _full((NLC, 128, LC)), _full((NLC, 128, LC)),
                   _full((NLC, 128, LC)), _full((NLC, 128, LC)), _full((1, D))],
        out_shape=[jax.ShapeDtypeStruct((lp, D), BF), jax.ShapeDtypeStruct((16, SW), F32),
                   jax.ShapeDtypeStruct((NLC, 128, LC), F32), jax.ShapeDtypeStruct((NLC, 128, LC), F32),
                   jax.ShapeDtypeStruct((NLC, 128, LC), F32), jax.ShapeDtypeStruct((NLC, 128, LC), F32),
                   jax.ShapeDtypeStruct((1, D), F32)],
        scratch_shapes=[pltpu.VMEM((TS, SW), F32), pltpu.VMEM((TS, SW), F32),
                        pltpu.VMEM((1, SW), F32), pltpu.VMEM((1, SW), F32), pltpu.VMEM((TS, D), F32)],
        compiler_params=_cp(("arbitrary",)),
    )(dy, u, sr, si, sr, si, pw, br, bi, cr, ci, dsk, p, pt)


def _merge_parts(attn, zz, gg, abn, sbn):
    za, zb = zz[:, 0:D], zz[:, D:2 * D]
    sgz = _sigmoid(zb)
    ssm = za * sgz
    ra = lax.rsqrt(jnp.mean(attn * attn, axis=-1, keepdims=True) + EPS)
    rs = lax.rsqrt(jnp.mean(ssm * ssm, axis=-1, keepdims=True) + EPS)
    ah, sh = attn * ra, ssm * rs
    sga, sgs = _sigmoid(gg[:, 0:D]), _sigmoid(gg[:, D:2 * D])
    return za, sgz, ra, rs, ah, sh, sga, sgs, ah * abn, sh * sbn


def mid_fwd(y, attn, gg, x, mpad, w_glu, w_out, abn, sbn):
    lp = y.shape[0]

    def body(y_ref, a_ref, gg_ref, x_ref, m_ref, wg_ref, wo_ref, abn_ref, sbn_ref, z_o, zz_o, mg_o, h1_o):
        z, _ = _gelu(y_ref[...])
        zb = z.astype(BF)
        z_o[...] = zb
        zz = _dot(zb, wg_ref[...])
        zz_o[...] = zz.astype(BF)
        _, _, _, _, _, _, sga, sgs, an, sn = _merge_parts(a_ref[...], zz, gg_ref[...], abn_ref[...], sbn_ref[...])
        mb = (sga * an + sgs * sn).astype(BF)
        mg_o[...] = mb
        h1_o[...] = _seq_tile(x_ref, m_ref) + _dot(mb, wo_ref[...])

    return pl.pallas_call(
        body, name="mid_fwd", grid=(lp // TM,),
        in_specs=[_rows(TM, D), _rows(TM, D), _rows(TM, 2 * D)] + _seq_specs() + [_full((D, 2 * D)), _full((D, D)),
                  _full((1, D)), _full((1, D))],
        out_specs=[_rows(TM, D), _rows(TM, 2 * D), _rows(TM, D), _rows(TM, D)],
        out_shape=[jax.ShapeDtypeStruct((lp, D), BF), jax.ShapeDtypeStruct((lp, 2 * D), BF),
                   jax.ShapeDtypeStruct((lp, D), BF), jax.ShapeDtypeStruct((lp, D), F32)],
        compiler_params=_cp(("arbitrary",)),
    )(y, attn, gg, x, mpad, w_glu, w_out, abn, sbn)


def mid_bwd(dh1b, y, attn, gg, zzb, w_glu, w_out, abn, sbn):
    lp = y.shape[0]

    def body(dh_ref, y_ref, a_ref, gg_ref, zz_ref, wg_ref, wo_ref, abn_ref, sbn_ref,
             dy_o, dattn_o, dgg_o, dzz_o, dabn_o, dsbn_o):
        i = pl.program_id(0)

        @pl.when(i == 0)
        def _():
            dabn_o[...] = jnp.zeros_like(dabn_o)
            dsbn_o[...] = jnp.zeros_like(dsbn_o)

        dm = _dot_nt(dh_ref[...], wo_ref[...])
        abn, sbn = abn_ref[...], sbn_ref[...]
        za, sgz, ra, rs, ah, sh, sga, sgs, an, sn = _merge_parts(
            a_ref[...], zz_ref[...].astype(F32), gg_ref[...], abn, sbn)
        dgg_o[:, 0:D] = (dm * an * sga * (1.0 - sga)).astype(BF)
        dgg_o[:, D:2 * D] = (dm * sn * sgs * (1.0 - sgs)).astype(BF)
        da = dm * sga
        dabn_o[...] += jnp.sum(da * ah, axis=0, keepdims=True)
        dah = da * abn
        dattn_o[...] = (ra * (dah - ah * jnp.mean(dah * ah, axis=-1, keepdims=True))).astype(BF)
        ds = dm * sgs
        dsbn_o[...] += jnp.sum(ds * sh, axis=0, keepdims=True)
        dsh = ds * sbn
        dssm = rs * (dsh - sh * jnp.mean(dsh * sh, axis=-1, keepdims=True))
        dzz_o[:, 0:D] = (dssm * sgz).astype(BF)
        dzz_o[:, D:2 * D] = (dssm * za * sgz * (1.0 - sgz)).astype(BF)
        dz = _dot_nt(dzz_o[...], wg_ref[...])
        yv = y_ref[...]
        _, t = _gelu(yv)
        dy_o[...] = dz * _gelu_grad(yv, t)

    return pl.pallas_call(
        body, name="mid_bwd", grid=(lp // TM,),
        in_specs=[_rows(TM, D), _rows(TM, D), _rows(TM, D), _rows(TM, 2 * D), _rows(TM, 2 * D),
                  _full((D, 2 * D)), _full((D, D)), _full((1, D)), _full((1, D))],
        out_specs=[_rows(TM, D), _rows(TM, D), _rows(TM, 2 * D), _rows(TM, 2 * D), _full((1, D)), _full((1, D))],
        out_shape=[jax.ShapeDtypeStruct((lp, D), F32), jax.ShapeDtypeStruct((lp, D), BF),
                   jax.ShapeDtypeStruct((lp, 2 * D), BF), jax.ShapeDtypeStruct((lp, 2 * D), BF),
                   jax.ShapeDtypeStruct((1, D), F32), jax.ShapeDtypeStruct((1, D), F32)],
        compiler_params=_cp(("arbitrary",)),
    )(dh1b, y, attn, gg, zzb, w_glu, w_out, abn, sbn)


def ffn_fwd_bwd(h1, tgt, w_fi, w_fo, gf):
    lp = h1.shape[0]
    tf = 256
    bounds = (0, 1536, DFF)

    def body(h_ref, t_ref, wi_ref, wo_ref, gf_ref, hn_o, dgu_o, act_o, dh2_o, dh1_o, dh1b_o, dgf_o, loss_o,
             gate_s, up_s):
        i = pl.program_id(0)

        @pl.when(i == 0)
        def _():
            dgf_o[...] = jnp.zeros_like(dgf_o)
            loss_o[...] = jnp.zeros_like(loss_o)

        h = h_ref[...]
        r = lax.rsqrt(jnp.mean(h * h, axis=-1, keepdims=True) + EPS)
        xh = h * r
        hb = (xh * gf_ref[...]).astype(BF)
        hn_o[...] = hb
        h2 = h
        for c0, c1 in zip(bounds[:-1], bounds[1:]):
            gate = _dot(hb, wi_ref[:, c0:c1])
            up = _dot(hb, wi_ref[:, DFF + c0:DFF + c1])
            gate_s[:, c0:c1] = gate
            up_s[:, c0:c1] = up
            ab = (gate * _sigmoid(gate) * up).astype(BF)
            act_o[:, c0:c1] = ab
            h2 = h2 + _dot(ab, wo_ref[c0:c1, :])
        real = (i >= PAD // tf).astype(F32)
        e = (h2 - t_ref[...]) * real
        loss_o[...] += 0.5 * jnp.sum(jnp.sum(e * e, axis=-1, keepdims=True), axis=0, keepdims=True) * (1.0 / D)
        dh2 = e * (1.0 / D)
        db = dh2.astype(BF)
        dh2_o[...] = db
        dhn = jnp.zeros((tf, D), F32)
        for c0, c1 in zip(bounds[:-1], bounds[1:]):
            gate = gate_s[:, c0:c1]
            up = up_s[:, c0:c1]
            sg = _sigmoid(gate)
            dact = _dot_nt(db, wo_ref[c0:c1, :])
            dgate = (dact * up * (sg * (1.0 + gate * (1.0 - sg)))).astype(BF)
            dup = (dact * (gate * sg)).astype(BF)
            dgu_o[:, c0:c1] = dgate
            dgu_o[:, DFF + c0:DFF + c1] = dup
            dhn += _dot_nt(dgate, wi_ref[:, c0:c1]) + _dot_nt(dup, wi_ref[:, DFF + c0:DFF + c1])
        dgf_o[...] += jnp.sum(dhn * xh, axis=0, keepdims=True)
        dxh = dhn * gf_ref[...]
        dh1 = dh2 + r * (dxh - xh * jnp.mean(dxh * xh, axis=-1, keepdims=True))
        dh1_o[...] = dh1
        dh1b_o[...] = dh1.astype(BF)

    return pl.pallas_call(
        body, name="ffn_fwd_bwd", grid=(lp // tf,),
        in_specs=[_rows(tf, D), pl.BlockSpec((tf, D), lambda i: (jnp.maximum(i - PAD // tf, 0), 0)),
                  _const((D, 2 * DFF)), _const((DFF, D)), _full((1, D))],
        out_specs=[_rows(tf, D), _rows(tf, 2 * DFF), _rows(tf, DFF), _rows(tf, D), _rows(tf, D), _rows(tf, D),
                   _full((1, D)), _full((1, 1))],
        out_shape=[jax.ShapeDtypeStruct((lp, D), BF), jax.ShapeDtypeStruct((lp, 2 * DFF), BF),
                   jax.ShapeDtypeStruct((lp, DFF), BF), jax.ShapeDtypeStruct((lp, D), BF),
                   jax.ShapeDtypeStruct((lp, D), F32), jax.ShapeDtypeStruct((lp, D), BF),
                   jax.ShapeDtypeStruct((1, D), F32), jax.ShapeDtypeStruct((1, 1), F32)],
        scratch_shapes=[pltpu.VMEM((tf, DFF), F32), pltpu.VMEM((tf, DFF), F32)],
        compiler_params=_cp(("arbitrary",), vmem_mb=58),
    )(h1, tgt, w_fi, w_fo, gf)


def weight_grad(name, a, b, tn, with_bf16=False):
    lp, k = a.shape
    n = b.shape[1]
    tr = next(t for t in (2112, 1056, 768, 512, 384, 256, 128) if lp % t == 0 and t * k * 2 <= (5 << 20))

    def body(a_ref, b_ref, o_ref, *ob_ref):
        @pl.when(pl.program_id(1) == 0)
        def _():
            o_ref[...] = jnp.zeros_like(o_ref)

        o_ref[...] += _dot_tn(a_ref[...], b_ref[...])
        if with_bf16:
            @pl.when(pl.program_id(1) == lp // tr - 1)
            def _():
                ob_ref[0][...] = o_ref[...].astype(BF)

    spec = pl.BlockSpec((k, tn), lambda j, m: (0, j))
    return pl.pallas_call(
        body, name=name, grid=(n // tn, lp // tr),
        in_specs=[pl.BlockSpec((tr, k), lambda j, m: (m, 0)), pl.BlockSpec((tr, tn), lambda j, m: (m, j))],
        out_specs=[spec, spec] if with_bf16 else spec,
        out_shape=([jax.ShapeDtypeStruct((k, n), F32), jax.ShapeDtypeStruct((k, n), BF)] if with_bf16
                   else jax.ShapeDtypeStruct((k, n), F32)),
        compiler_params=_cp(("arbitrary", "arbitrary")),
    )(a, b)


def _adamw_math(w, g, m, v):
    m = B1 * m + (1.0 - B1) * g
    v = B2 * v + (1.0 - B2) * (g * g)
    delta = -LR * ((m / BC1) / (jnp.sqrt(v / BC2) + AEPS) + WD * w)
    return delta, m, v


def _row_tile(r):
    return next((t for t in (256, 128, 64, 32, 16, 8) if r % t == 0), r)


def adamw(name, w, g, m, v):
    r, c = w.shape
    tr = _row_tile(r)

    def body(w_ref, g_ref, m_ref, v_ref, g_o, d_o, m_o, v_o):
        g = g_ref[...]
        g_o[...] = g
        d_o[...], m_o[...], v_o[...] = _adamw_math(w_ref[...], g, m_ref[...], v_ref[...])

    spec = pl.BlockSpec((tr, c), lambda i: (i, 0))
    return pl.pallas_call(
        body, name=name, grid=(r // tr,), in_specs=[spec] * 4, out_specs=[spec] * 4,
        out_shape=[jax.ShapeDtypeStruct((r, c), F32)] * 4,
        compiler_params=_cp(("arbitrary",)),
    )(w, g, m, v)


def adamw_small(ws, gs, ms, vs):
    n = len(ws)
    steps = 8

    def spec(a):
        if a.ndim == 4:
            return pl.BlockSpec((1, a.shape[1] // steps) + a.shape[2:], lambda i: (0, i, 0, 0))
        nd = a.ndim
        return pl.BlockSpec(a.shape, lambda i: (0,) * nd)

    def body(*refs):
        w, g, m, v, d_o, m_o, v_o = (refs[k * n:(k + 1) * n] for k in range(7))
        for t in range(n):
            d_o[t][...], m_o[t][...], v_o[t][...] = _adamw_math(w[t][...], g[t][...], m[t][...], v[t][...])

    specs = [spec(a) for a in ws]
    res = pl.pallas_call(
        body, name="adamw_small", grid=(steps,), in_specs=specs * 4, out_specs=specs * 3,
        out_shape=[jax.ShapeDtypeStruct(a.shape, F32) for a in ws] * 3,
        compiler_params=_cp(("arbitrary",)),
    )(*ws, *gs, *ms, *vs)
    return res[:n], res[n:2 * n], res[2 * n:]


def pair_sum(name, place, full, got, ax):
    r, c = got.shape
    tr = _row_tile(r)

    def body(s_ref, a_ref, b_ref, o_ref):
        o_ref[...] = (a_ref[...] + b_ref[...]).astype(BF)

    if ax == 1:
        mine = pl.BlockSpec((tr, c), lambda i, s: (s[1] * (r // tr) + i, 0))
    else:
        mine = pl.BlockSpec((tr, c), lambda i, s: (i, s[1]))
    spec = pl.BlockSpec((tr, c), lambda i, s: (i, 0))
    return pl.pallas_call(
        body, name=name,
        grid_spec=pltpu.PrefetchScalarGridSpec(num_scalar_prefetch=1, grid=(r // tr,), in_specs=[mine, spec],
                                               out_specs=spec),
        out_shape=jax.ShapeDtypeStruct((r, c), BF),
        compiler_params=_cp(("arbitrary",)),
    )(place, full, got)


def sum4(name, place, parts, half, ax):
    _, r, c = parts.shape
    tr = _row_tile(r)

    def body(s_ref, p_ref, h_ref, o_ref):
        o_ref[...] = ((p_ref[0].astype(F32) + p_ref[1].astype(F32))
                      + (p_ref[2].astype(F32) + h_ref[...].astype(F32)))

    if ax == 1:
        own = pl.BlockSpec((tr, c), lambda i, s: (i, s[0]))
        out = pl.BlockSpec((tr, c), lambda i, s: (s[1] * (r // tr) + i, 0))
        shape = (2 * r, c)
    else:
        own = pl.BlockSpec((tr, c), lambda i, s: (s[0] * (r // tr) + i, 0))
        out = pl.BlockSpec((tr, c), lambda i, s: (i, s[1]))
        shape = (r, 2 * c)
    return pl.pallas_call(
        body, name=name,
        grid_spec=pltpu.PrefetchScalarGridSpec(
            num_scalar_prefetch=1, grid=(r // tr,),
            in_specs=[pl.BlockSpec((3, tr, c), lambda i, s: (0, i, 0)), own], out_specs=out),
        out_shape=jax.ShapeDtypeStruct(shape, F32),
        compiler_params=_cp(("arbitrary",)),
    )(place, parts, half)


def sum8(name, place, parts, full, ax):
    _, r, c = parts.shape
    tr = _row_tile(r)

    def body(s_ref, p_ref, g_ref, o_ref):
        p = [p_ref[k].astype(F32) for k in range(7)]
        o_ref[...] = ((p[0] + p[1]) + (p[2] + p[3])) + ((p[4] + p[5]) + (p[6] + g_ref[...]))

    if ax == 1:
        own = pl.BlockSpec((tr, c), lambda i, s: (s[1] * (r // tr) + i, s[0]))
        out = pl.BlockSpec((tr, c), lambda i, s: (s[1] * (r // tr) + i, 0))
        shape = (2 * r, c)
    else:
        own = pl.BlockSpec((tr, c), lambda i, s: (s[0] * (r // tr) + i, s[1]))
        out = pl.BlockSpec((tr, c), lambda i, s: (i, s[1]))
        shape = (r, 2 * c)
    return pl.pallas_call(
        body, name=name,
        grid_spec=pltpu.PrefetchScalarGridSpec(
            num_scalar_prefetch=1, grid=(r // tr,),
            in_specs=[pl.BlockSpec((7, tr, c), lambda i, s: (0, i, 0)), own], out_specs=out),
        out_shape=jax.ShapeDtypeStruct(shape, F32),
        compiler_params=_cp(("arbitrary",)),
    )(place, parts, full)


def _place():
    x, y, c = lax.axis_index("x"), lax.axis_index("y"), lax.axis_index("c")
    return x, y, c


def _other_chips(x, y):
    return [(1 - x, y), (x, 1 - y), (1 - x, 1 - y)]


SHARDED = (("w_in", D, IN_COLS, 1), ("w_glu", D, 2 * D, 1), ("w_out", D, D, 0),
           ("w_ffn_in", D, 2 * DFF, 1), ("w_ffn_out", DFF, D, 0))


def _shard_of(ref, axis, k, size):
    if axis == 0:
        return ref.at[pl.ds(k * size, size), :]
    return ref.at[:, pl.ds(k * size, size)]


def _gather_piece(full, dims, k, h):
    r, cc, ax = dims
    if ax == 0:
        return full.at[pl.ds(k * (r // 4) + h * (r // 8), r // 8), :]
    return full.at[pl.ds(h * (r // 2), r // 2), pl.ds(k * (cc // 4), cc // 4)]


def _rows_half(shard, h):
    rs = shard.shape[0]
    return shard.at[pl.ds(h * (rs // 2), rs // 2), :]


def gather_weights(shards, dims):
    items = [(s,) + tuple(d) for s, d in zip(shards, dims)]
    n = len(items)

    def body(*refs):
        ins, outs = refs[:n], refs[n:2 * n]
        ici_send, ici_recv, d2d_send, d2d_recv, local_sems = refs[2 * n:]
        x, y, c = _place()
        mine = 2 * x + y
        chips = _other_chips(x, y)

        def piece(t, k, h):
            return _gather_piece(outs[t], dims[t], k, h)

        def src_half(t, h):
            return _rows_half(ins[t], h)

        local, first, passed = [], [], []
        for t in range(n):
            _, r, cc, ax = items[t]
            size = (r if ax == 0 else cc) // 4
            loc = pltpu.make_async_remote_copy(
                src_ref=ins[t], dst_ref=_shard_of(outs[t], ax, mine, size), send_sem=local_sems.at[t],
                recv_sem=local_sems.at[n + t], device_id=(x, y, 1 - c), device_id_type=MESH)
            loc.start()
            local.append(loc)
            for j, chip in enumerate(chips):
                cp = pltpu.make_async_remote_copy(
                    src_ref=src_half(t, c), dst_ref=piece(t, mine, c),
                    send_sem=ici_send.at[3 * t + j], recv_sem=ici_recv.at[3 * t + j],
                    device_id=(*chip, c), device_id_type=MESH)
                cp.start()
                first.append(cp)
        for t in range(n):
            for j, (px, py) in enumerate(chips):
                got = piece(t, 2 * px + py, c)
                pltpu.make_async_remote_copy(
                    src_ref=src_half(t, c), dst_ref=got, send_sem=ici_send.at[3 * t + j],
                    recv_sem=ici_recv.at[3 * t + j], device_id=(px, py, c), device_id_type=MESH).wait_recv()
                cp = pltpu.make_async_remote_copy(
                    src_ref=got, dst_ref=got, send_sem=d2d_send.at[3 * t + j], recv_sem=d2d_recv.at[3 * t + j],
                    device_id=(x, y, 1 - c), device_id_type=MESH)
                cp.start()
                passed.append(cp)
        for t in range(n):
            for j, (px, py) in enumerate(chips):
                other = piece(t, 2 * px + py, 1 - c)
                pltpu.make_async_remote_copy(
                    src_ref=other, dst_ref=other, send_sem=d2d_send.at[3 * t + j], recv_sem=d2d_recv.at[3 * t + j],
                    device_id=(x, y, 1 - c), device_id_type=MESH).wait_recv()
        for cp in first + passed:
            cp.wait_send()
        for loc in local:
            loc.wait()

    return pl.pallas_call(
        body, name="gather_weights",
        in_specs=[ANY] * n, out_specs=[ANY] * n,
        out_shape=[jax.ShapeDtypeStruct((r, c), s.dtype) for s, r, c, _ in items],
        scratch_shapes=[pltpu.SemaphoreType.DMA((3 * n,)), pltpu.SemaphoreType.DMA((3 * n,)),
                        pltpu.SemaphoreType.DMA((3 * n,)), pltpu.SemaphoreType.DMA((3 * n,)),
                        pltpu.SemaphoreType.DMA((2 * n,))],
        compiler_params=pltpu.CompilerParams(has_side_effects=True),
    )(*shards)


SEM = pl.BlockSpec(memory_space=pltpu.SEMAPHORE)
HBM = pl.BlockSpec(memory_space=pltpu.HBM)
EFFECT = pltpu.SideEffectType.DATAFLOW_SIDE_EFFECTING


def _in_hbm(a):
    return pltpu.with_memory_space_constraint(a, pltpu.HBM)


def _split_copy_start(name, copies, ncopies, srcs, lands, after):
    ns, nl = len(srcs), len(lands)

    def body(*refs):
        ins, land_refs = refs[:ns], refs[ns:ns + nl]
        send_sems, recv_sems = refs[ns + nl + 1], refs[ns + nl + 2]
        token = refs[-1]
        for k, (src, dst, dev) in enumerate(copies(ins, land_refs)):
            pltpu.make_async_remote_copy(src_ref=src, dst_ref=dst, send_sem=send_sems.at[k],
                                         recv_sem=recv_sems.at[k], device_id=dev, device_id_type=MESH).start()
        token[...] = jnp.zeros_like(token)

    res = pl.pallas_call(
        body, name=name,
        in_specs=[HBM] * (ns + nl) + [ANY],
        out_specs=(SEM, SEM) + (HBM,) * (ns + nl) + (pl.BlockSpec(memory_space=pltpu.VMEM),),
        out_shape=(pltpu.SemaphoreType.DMA((ncopies,)), pltpu.SemaphoreType.DMA((ncopies,)))
        + tuple(pltpu.HBM(a.shape, a.dtype) for a in list(srcs) + list(lands))
        + (jax.ShapeDtypeStruct((8, 128), F32),),
        input_output_aliases={t: 2 + t for t in range(ns + nl)},
        compiler_params=pltpu.CompilerParams(has_side_effects=EFFECT),
    )(*[_in_hbm(a) for a in srcs], *[_in_hbm(a) for a in lands], after)
    return res[0], res[1], list(res[2:2 + ns]), list(res[2 + ns:2 + ns + nl]), res[-1]


def _split_copy_wait(name, copies, arrivals, send_sems, recv_sems, srcs, lands, after):
    ns, nl = len(srcs), len(lands)

    def body(*refs):
        ins, land_refs = refs[:ns], refs[ns:ns + nl]
        send_sems_ref, recv_sems_ref = refs[ns + nl], refs[ns + nl + 1]
        mine = copies(ins, land_refs)
        for k, ((src, dst, dev), got) in enumerate(zip(mine, arrivals(ins, land_refs))):
            pltpu.make_async_remote_copy(src_ref=src, dst_ref=dst, send_sem=send_sems_ref.at[k],
                                         recv_sem=recv_sems_ref.at[k], device_id=dev,
                                         device_id_type=MESH).wait_send()
            pltpu.make_async_remote_copy(src_ref=got, dst_ref=got, send_sem=send_sems_ref.at[k],
                                         recv_sem=recv_sems_ref.at[k], device_id=dev,
                                         device_id_type=MESH).wait_recv()

    res = pl.pallas_call(
        body, name=name,
        in_specs=[HBM] * (ns + nl) + [SEM, SEM, ANY],
        out_specs=(HBM,) * (ns + nl),
        out_shape=tuple(pltpu.HBM(a.shape, a.dtype) for a in list(srcs) + list(lands)),
        input_output_aliases={t: t for t in range(ns + nl)},
        compiler_params=pltpu.CompilerParams(has_side_effects=EFFECT),
    )(*srcs, *lands, send_sems, recv_sems, after)
    return list(res[:ns]), list(res[ns:])


def _gather_copies(dims):
    def copies(ins, lands):
        x, y, c = _place()
        mine = 2 * x + y
        out = []
        for t, d in enumerate(dims):
            size = (d[0] if d[2] == 0 else d[1]) // 4
            for px, py in _other_chips(x, y):
                for dc in range(2):
                    out.append((_rows_half(ins[t], c), _gather_piece(lands[t], d, mine, c), (px, py, c ^ dc)))
            out.append((ins[t], _shard_of(lands[t], d[2], mine, size), (x, y, 1 - c)))
        return out

    def arrivals(ins, lands):
        x, y, c = _place()
        mine = 2 * x + y
        out = []
        for t, d in enumerate(dims):
            size = (d[0] if d[2] == 0 else d[1]) // 4
            for px, py in _other_chips(x, y):
                for dc in range(2):
                    out.append(_gather_piece(lands[t], d, 2 * px + py, c ^ dc))
            out.append(_shard_of(lands[t], d[2], mine, size))
        return out

    return copies, arrivals


def _half_of(ref, ax, h):
    r, c = ref.shape
    if ax == 1:
        return ref.at[pl.ds(h * (r // 2), r // 2), :]
    return ref.at[:, pl.ds(h * (c // 2), c // 2)]


def _half_shape(r, c, ax):
    return (r // 2, c) if ax == 1 else (r, c // 2)


def pair_exchange(name, grads, idxs):
    n = len(grads)
    which = [SHARDED[i] for i in idxs]

    def body(*refs):
        ins, got = refs[:n], refs[n:2 * n]
        send_sems, recv_sems = refs[2 * n:]
        x, y, c = _place()
        cps = []
        for t, (_, _, _, ax) in enumerate(which):
            cp = pltpu.make_async_remote_copy(
                src_ref=_half_of(ins[t], ax, 1 - c), dst_ref=got[t], send_sem=send_sems.at[t],
                recv_sem=recv_sems.at[t], device_id=(x, y, 1 - c), device_id_type=MESH)
            cp.start()
            cps.append(cp)
        for cp in cps:
            cp.wait()

    return pl.pallas_call(
        body, name=name,
        in_specs=[ANY] * n, out_specs=[ANY] * n,
        out_shape=[jax.ShapeDtypeStruct(_half_shape(r, c, ax), F32) for _, r, c, ax in which],
        scratch_shapes=[pltpu.SemaphoreType.DMA((n,)), pltpu.SemaphoreType.DMA((n,))],
        compiler_params=pltpu.CompilerParams(has_side_effects=True),
    )(*grads)


def _piece_shapes(idxs):
    out = []
    for _, r, c, ax in [SHARDED[i] for i in idxs]:
        hr, hc = _half_shape(r, c, ax)
        out.append((hr // 4, hc) if ax == 0 else (hr, hc // 4))
    return out


def _piece_of(full, ax, k, h):
    r, c = full.shape
    if ax == 1:
        return full.at[pl.ds(h * (r // 2), r // 2), pl.ds(k * (c // 4), c // 4)]
    return full.at[pl.ds(k * (r // 4), r // 4), pl.ds(h * (c // 2), c // 2)]


def _scatter8_copies(idxs):
    which = [SHARDED[i] for i in idxs]

    def copies(ins, lands):
        x, y, c = _place()
        out = []
        for t, (_, _, _, ax) in enumerate(which):
            for j, (px, py) in enumerate(_other_chips(x, y)):
                for d in range(2):
                    out.append((_piece_of(ins[t], ax, 2 * px + py, c ^ d), lands[t].at[2 * j + d], (px, py, c ^ d)))
            out.append((_piece_of(ins[t], ax, 2 * x + y, 1 - c), lands[t].at[6], (x, y, 1 - c)))
        return out

    def arrivals(ins, lands):
        return [lands[t].at[k] for t in range(len(which)) for k in range(7)]

    return copies, arrivals


def scatter_grads(halves, idxs):
    n = len(halves)
    which = [SHARDED[i] for i in idxs]

    def shapes():
        return _piece_shapes(idxs)

    def body(*refs):
        ins, outs = refs[:n], refs[n:2 * n]
        send_sems, recv_sems = refs[2 * n:]
        x, y, c = _place()
        started = []
        for t, (src, dst, (_, _, _, ax)) in enumerate(zip(ins, outs, which)):
            size = src.shape[ax] // 4
            for j, (px, py) in enumerate(_other_chips(x, y)):
                cp = pltpu.make_async_remote_copy(
                    src_ref=_shard_of(src, ax, 2 * px + py, size), dst_ref=dst.at[j],
                    send_sem=send_sems.at[3 * t + j], recv_sem=recv_sems.at[3 * t + j],
                    device_id=(px, py, c), device_id_type=MESH)
                cp.start()
                started.append(cp)
        for cp in started:
            cp.wait()

    return pl.pallas_call(
        body, name="scatter_grads",
        in_specs=[ANY] * n, out_specs=[ANY] * n,
        out_shape=[jax.ShapeDtypeStruct((3,) + s, BF) for s in shapes()],
        scratch_shapes=[pltpu.SemaphoreType.DMA((3 * n,)), pltpu.SemaphoreType.DMA((3 * n,))],
        compiler_params=pltpu.CompilerParams(has_side_effects=True),
    )(*halves)


def join_halves(name, shards, idxs):
    n = len(shards)
    which = [SHARDED[i] for i in idxs]

    def body(*refs):
        ins, outs = refs[:n], refs[n:2 * n]
        send_sems, recv_sems = refs[2 * n:]
        x, y, c = _place()
        cps = []
        for t, (_, _, _, ax) in enumerate(which):
            cp = pltpu.make_async_remote_copy(
                src_ref=_half_of(ins[t], ax, c), dst_ref=_half_of(outs[t], ax, c), send_sem=send_sems.at[t],
                recv_sem=recv_sems.at[t], device_id=(x, y, 1 - c), device_id_type=MESH)
            cp.start()
            cps.append(cp)
        for t, cp in enumerate(cps):
            cp.wait_send()
            theirs = _half_of(outs[t], which[t][3], 1 - c)
            pltpu.make_async_remote_copy(
                src_ref=theirs, dst_ref=theirs, send_sem=send_sems.at[t], recv_sem=recv_sems.at[t],
                device_id=(x, y, 1 - c), device_id_type=MESH).wait_recv()

    return pl.pallas_call(
        body, name=name,
        in_specs=[ANY] * n, out_specs=[ANY] * n,
        out_shape=[jax.ShapeDtypeStruct(s.shape, F32) for s in shards],
        input_output_aliases={t: t for t in range(n)},
        scratch_shapes=[pltpu.SemaphoreType.DMA((n,)), pltpu.SemaphoreType.DMA((n,))],
        compiler_params=pltpu.CompilerParams(has_side_effects=True),
    )(*shards)


def allreduce_small(pack):
    rows = pack.shape[0]
    hr = rows // 2

    def body(p_ref, o_ref, sib_ref, parts_ref, send_sems, recv_sems):
        x, y, c = _place()
        mine = 2 * x + y
        sibling = (x, y, 1 - c)
        swap = pltpu.make_async_remote_copy(
            src_ref=p_ref, dst_ref=sib_ref, send_sem=send_sems.at[0], recv_sem=recv_sems.at[0],
            device_id=sibling, device_id_type=MESH)
        swap.start()
        swap.wait()
        half = pl.ds(pl.multiple_of(c * hr, 8), hr)
        parts_ref[mine] = p_ref[half, :] + sib_ref[half, :]
        cps = []
        for j, chip in enumerate(_other_chips(x, y)):
            cp = pltpu.make_async_remote_copy(
                src_ref=parts_ref.at[mine], dst_ref=parts_ref.at[mine], send_sem=send_sems.at[1 + j],
                recv_sem=recv_sems.at[1 + j], device_id=(*chip, c), device_id_type=MESH)
            cp.start()
            cps.append(cp)
        for j, (px, py) in enumerate(_other_chips(x, y)):
            cps[j].wait_send()
            theirs = parts_ref.at[2 * px + py]
            pltpu.make_async_remote_copy(
                src_ref=theirs, dst_ref=theirs, send_sem=send_sems.at[1 + j], recv_sem=recv_sems.at[1 + j],
                device_id=(px, py, c), device_id_type=MESH).wait_recv()
        o_ref[half, :] = (parts_ref[0] + parts_ref[1]) + (parts_ref[2] + parts_ref[3])
        back = pltpu.make_async_remote_copy(
            src_ref=o_ref.at[half, :], dst_ref=o_ref.at[half, :], send_sem=send_sems.at[4],
            recv_sem=recv_sems.at[4], device_id=sibling, device_id_type=MESH)
        back.start()
        back.wait_send()
        other = pl.ds(pl.multiple_of((1 - c) * hr, 8), hr)
        pltpu.make_async_remote_copy(
            src_ref=o_ref.at[other, :], dst_ref=o_ref.at[other, :], send_sem=send_sems.at[4],
            recv_sem=recv_sems.at[4], device_id=sibling, device_id_type=MESH).wait_recv()

    return pl.pallas_call(
        body, name="allreduce_small",
        in_specs=[pl.BlockSpec(memory_space=pltpu.VMEM)], out_specs=pl.BlockSpec(memory_space=pltpu.VMEM),
        out_shape=jax.ShapeDtypeStruct((rows, 128), F32),
        scratch_shapes=[pltpu.VMEM((rows, 128), F32), pltpu.VMEM((4, hr, 128), F32),
                        pltpu.SemaphoreType.DMA((5,)), pltpu.SemaphoreType.DMA((5,))],
        compiler_params=pltpu.CompilerParams(has_side_effects=True, vmem_limit_bytes=40 << 20),
    )(pack)


def _ssm_discretize(lam_re, lam_im, log_dt, b_re, b_im):
    dt = jnp.exp(log_dt)[:, None]
    mag = jnp.exp(lam_re * dt)
    ar, ai = mag * jnp.cos(lam_im * dt), mag * jnp.sin(lam_im * dt)
    den = lam_re * lam_re + lam_im * lam_im
    nr, ni = ar - 1.0, ai
    fr, fi = (nr * lam_re + ni * lam_im) / den, (ni * lam_re - nr * lam_im) / den
    bbr = fr[:, None, :] * b_re - fi[:, None, :] * b_im
    bbi = fr[:, None, :] * b_im + fi[:, None, :] * b_re
    return ar, ai, bbr, bbi


def _cmul(a, b):
    return a[0] * b[0] - a[1] * b[1], a[0] * b[1] + a[1] * b[0]


def _scan_powers(ar, ai):
    a = (ar.reshape(1, SW), ai.reshape(1, SW))
    big = a
    for _ in range(SEG.bit_length() - 1):
        big = _cmul(big, big)
    assert 1 << (SEG.bit_length() - 1) == SEG
    pows = {1: big}
    pows[2] = _cmul(big, big)
    pows[4] = _cmul(pows[2], pows[2])
    rid = jnp.arange(8)[:, None]
    ones = jnp.ones((8, 1), F32)
    fwd, rev = [ones * a[0], ones * a[1]], [ones * a[0], -ones * a[1]]
    for k in (1, 2, 4):
        pr, pi = pows[k]
        fwd += [jnp.where(rid >= k, pr, 0.0), jnp.where(rid >= k, pi, 0.0)]
        rev += [jnp.where(rid < 8 - k, pr, 0.0), jnp.where(rid < 8 - k, -pi, 0.0)]
    return jnp.concatenate(fwd + rev, axis=0)


def _pack_b(bb):
    t = bb.reshape(NLC, 8, GCH, NSTATE)
    return jnp.einsum("jgcp,gh->jgchp", t, jnp.eye(8, dtype=bb.dtype)).reshape(NLC, 128, LC)


def _unpack_b(db):
    t = jnp.einsum("jgchp,gh->jgcp", db.reshape(NLC, 8, GCH, 8, NSTATE), jnp.eye(8, dtype=db.dtype))
    return t.reshape(GROUPS, GCH, NSTATE)


def _pack_c(cc):
    t = cc.reshape(NLC, 8, GCH, NSTATE).transpose(0, 1, 3, 2)
    return jnp.einsum("jgpc,gh->jgphc", t, jnp.eye(8, dtype=cc.dtype)).reshape(NLC, LC, 128)


def _unpack_c(dc):
    t = jnp.einsum("jgphc,gh->jgpc", dc.reshape(NLC, 8, NSTATE, 8, GCH), jnp.eye(8, dtype=dc.dtype))
    return t.transpose(0, 1, 3, 2).reshape(GROUPS, GCH, NSTATE)


SMALL = ("norm_mix", "q_norm", "k_norm", "attn_sinks", "lam_re", "lam_im", "log_dt", "ssm_b_re", "ssm_b_im",
         "ssm_c_re", "ssm_c_im", "ssm_d", "attn_branch_norm", "ssm_branch_norm", "norm_ffn")


def _pack_small(arrs, rows=None):
    flat = jnp.concatenate([a.reshape(-1).astype(F32) for a in arrs])
    n = flat.shape[0]
    if rows is None:
        rows = -(-n // (128 * 256)) * 256
    return jnp.pad(flat, (0, rows * 128 - n)).reshape(rows, 128)


def _unpack_small(pack, like):
    flat = pack.reshape(-1)
    out, off = [], 0
    for a in like:
        out.append(flat[off:off + a.size].reshape(a.shape))
        off += a.size
    return out


def _gather_half_copies(dims):
    def copies(ins, lands):
        x, y, c = _place()
        mine = 2 * x + y
        out = []
        for t, d in enumerate(dims):
            size = (d[0] if d[2] == 0 else d[1]) // 4
            for chip in _other_chips(x, y):
                out.append((_rows_half(ins[t], c), _gather_piece(lands[t], d, mine, c), (*chip, c)))
            out.append((ins[t], _shard_of(lands[t], d[2], mine, size), (x, y, 1 - c)))
        return out

    def arrivals(ins, lands):
        x, y, c = _place()
        mine = 2 * x + y
        out = []
        for t, d in enumerate(dims):
            size = (d[0] if d[2] == 0 else d[1]) // 4
            for px, py in _other_chips(x, y):
                out.append(_gather_piece(lands[t], d, 2 * px + py, c))
            out.append(_shard_of(lands[t], d[2], mine, size))
        return out

    return copies, arrivals


def gather_forward(fulls, dims):
    n = len(fulls)

    def body(*refs):
        ins, outs = refs[:n], refs[n:2 * n]
        send_sems, recv_sems = refs[2 * n:]
        x, y, c = _place()
        cps = []
        for t in range(n):
            for j, (px, py) in enumerate(_other_chips(x, y)):
                cp = pltpu.make_async_remote_copy(
                    src_ref=_gather_piece(ins[t], dims[t], 2 * px + py, c),
                    dst_ref=_gather_piece(outs[t], dims[t], 2 * px + py, c),
                    send_sem=send_sems.at[3 * t + j], recv_sem=recv_sems.at[3 * t + j],
                    device_id=(x, y, 1 - c), device_id_type=MESH)
                cp.start()
                cps.append(cp)
        for t in range(n):
            for j, (px, py) in enumerate(_other_chips(x, y)):
                cps[3 * t + j].wait_send()
                other = _gather_piece(outs[t], dims[t], 2 * px + py, 1 - c)
                pltpu.make_async_remote_copy(
                    src_ref=other, dst_ref=other, send_sem=send_sems.at[3 * t + j], recv_sem=recv_sems.at[3 * t + j],
                    device_id=(x, y, 1 - c), device_id_type=MESH).wait_recv()

    return pl.pallas_call(
        body, name="gather_forward",
        in_specs=[ANY] * n, out_specs=[ANY] * n,
        out_shape=[jax.ShapeDtypeStruct(a.shape, a.dtype) for a in fulls],
        input_output_aliases={t: t for t in range(n)},
        scratch_shapes=[pltpu.SemaphoreType.DMA((3 * n,)), pltpu.SemaphoreType.DMA((3 * n,))],
        compiler_params=pltpu.CompilerParams(has_side_effects=True),
    )(*fulls)


def _tie(a, token):
    return a if token is None else a + token[0, 0]


class _NoGather:
    token = None

    def __init__(self, w_in, meta):
        self.out = (w_in, meta)

    def finish(self, after):
        return self.out


class _FirstGather:
    dims = [SHARDED[0][1:], (N_META, D, 1)]

    def __init__(self, w_in_shard, meta_shard):
        self.copies, self.arrivals = _gather_half_copies(self.dims)
        lands = [lax.empty((D, IN_COLS), BF), lax.empty((N_META, D), F32)]
        self.send, self.recv, self.srcs, self.lands, self.token = _split_copy_start(
            "gather_start_first", self.copies, 8, [w_in_shard, meta_shard], lands, w_in_shard)

    def finish(self, after):
        _, lands = _split_copy_wait("gather_wait_first", self.copies, self.arrivals, self.send, self.recv, self.srcs,
                                    self.lands, after)
        return gather_forward(lands, self.dims)


class _NoExchange:
    def __init__(self, w):
        self.w = w
        self.grads = {}

    def begin(self, after):
        return None

    def rest_weights(self, after):
        return self.w

    def start(self, group, idxs, grads):
        self.grads.update(zip(idxs, [g for g, _ in grads]))
        return None

    def finish(self, group, after):
        pass


class _Exchange:
    def __init__(self, rest_shards, place):
        self.rest_shards = rest_shards
        self.place = place
        self.rest_dims = [s[1:] for s in SHARDED[1:]]
        self.pending = {}
        self.halves = {}
        self.full = {}
        self.parts = {}

    def begin(self, after):
        copies, arrivals = _gather_copies(self.rest_dims)
        lands = [lax.empty((r, c), BF) for r, c, _ in self.rest_dims]
        send, recv, srcs, lands, token = _split_copy_start(
            "gather_start", copies, 7 * len(lands), self.rest_shards, lands, after)
        self.pending["gather"] = (copies, arrivals, send, recv, srcs, lands)
        return token

    def rest_weights(self, after):
        copies, arrivals, send, recv, srcs, lands = self.pending.pop("gather")
        _, fulls = _split_copy_wait("gather_wait", copies, arrivals, send, recv, srcs, lands, after)
        return dict(zip([s[0] for s in SHARDED[1:]], fulls))

    def _halves(self, group, idxs, grads):
        got = pair_exchange("pair_exchange_" + group, grads, idxs)
        return [pair_sum("pair_sum_" + SHARDED[i][0], self.place, g, b, SHARDED[i][3])
                for i, g, b in zip(idxs, grads, got)]

    def start(self, group, idxs, grads):
        if grads[0][1] is None:
            halves = self._halves(group, idxs, [g for g, _ in grads])
            for i, h, pt in zip(idxs, halves, scatter_grads(halves, idxs)):
                self.halves[i], self.parts[i] = h, pt
            return None
        copies, arrivals = _scatter8_copies(idxs)
        lands = [lax.empty((7,) + sh, BF) for sh in _piece_shapes(idxs)]
        send, recv, srcs, lands, token = _split_copy_start(
            "scatter_start_" + group, copies, 7 * len(idxs), [g16 for _, g16 in grads], lands, grads[0][0])
        self.pending[group] = (copies, arrivals, send, recv, srcs, lands, idxs)
        for i, (g32, _) in zip(idxs, grads):
            self.full[i] = g32
        return token

    def finish(self, group, after):
        if group not in self.pending:
            return
        copies, arrivals, send, recv, srcs, lands, idxs = self.pending.pop(group)
        _, lands = _split_copy_wait("scatter_wait_" + group, copies, arrivals, send, recv, srcs, lands, after)
        for i, pt in zip(idxs, lands):
            self.parts[i] = pt


def local_step(x, tgt, p, first, ex):
    qn_t = jnp.tile(p["q_norm"].reshape(1, HEAD), (1, NQ)) * HEAD ** -0.5
    kn_t = jnp.tile(p["k_norm"].reshape(1, HEAD), (1, NKV))
    seg = jnp.arange(256) // HEAD
    bd = (seg[:, None] == seg[None, :]).astype(BF)
    nm = p["norm_mix"].reshape(1, D)
    sinks = p["attn_sinks"].reshape(NQ)
    dsk = p["ssm_d"].reshape(1, D)
    abn, sbn, gf = (p[k].reshape(1, D) for k in ("attn_branch_norm", "ssm_branch_norm", "norm_ffn"))

    disc_in = (p["lam_re"], p["lam_im"], _tie(p["log_dt"], first.token), p["ssm_b_re"], p["ssm_b_im"])
    (ar, ai, bbr, bbi), disc_vjp = jax.vjp(_ssm_discretize, *disc_in)
    pw = _scan_powers(ar, ai)
    brp, bip = _pack_b(bbr.astype(BF)), _pack_b(bbi.astype(BF))
    crp = _pack_c(_tie(p["ssm_c_re"], first.token).astype(BF))
    cip = _pack_c(_tie(p["ssm_c_im"], first.token).astype(BF))

    w_in, meta_full = first.finish(brp)
    mpad = jnp.concatenate([jnp.zeros((META0, D), F32), meta_full], axis=0)
    token = ex.begin(w_in)
    xn, qr, kr, qn, kn, v, u, gg = in_proj_fwd(x, mpad, w_in, _tie(nm, token), qn_t, kn_t, bd)
    attn = attention_fwd(sinks, qn, kn, v)
    perm = _segment_order()
    y, sr, si = ssm_fwd(u, pw, brp, bip, crp, cip, dsk, perm)
    w = ex.rest_weights(y)
    zb, zzb, mgb, h1 = mid_fwd(y, attn, gg, x, mpad, w["w_glu"], w["w_out"], abn, sbn)
    hnb, dgub, actb, dh2b, dh1, dh1b, dgf, loss = ffn_fwd_bwd(h1, tgt, w["w_ffn_in"], w["w_ffn_out"], gf)
    token = ex.start("ffn", (3, 4), [weight_grad("grad_w_ffn_in", hnb, dgub, 1408, True),
                                     weight_grad("grad_w_ffn_out", actb, dh2b, 512, True)])
    dy, dattn, dggb, dzzb, dabn, dsbn = mid_bwd(dh1b, y, attn, gg, zzb, w["w_glu"], w["w_out"], _tie(abn, token),
                                                sbn)
    grads_mid = [weight_grad("grad_w_glu", zb, dzzb, 1024, True), weight_grad("grad_w_out", mgb, dh1b, 1024, True)]
    dub, da, dbr, dbi, dcr, dci, dd = ssm_bwd(dy, u, sr, si, pw, brp, bip, crp, cip, dsk, perm)
    ex.finish("ffn", dub)
    token = ex.start("mid", (1, 2), grads_mid)
    dqn, dkn, dv, dsink = attention_bwd(_tie(sinks, token), qn, kn, v, dattn)
    dproj, dx, dmpad, dnm, dqw, dkw = in_proj_bwd(dqn, dkn, dv, dub, dggb, qr, kr, x, mpad, dh1, w_in, nm, qn_t,
                                                  kn_t, bd)
    ex.finish("mid", dproj)
    ex.start("in", (0,), [(weight_grad("grad_w_in", xn, dproj, 1536), None)])

    dar = jnp.sum(da[0:8], axis=0).reshape(GROUPS, NSTATE)
    dai = jnp.sum(da[8:16], axis=0).reshape(GROUPS, NSTATE)
    dlr, dli, dldt, dbre, dbim = disc_vjp((dar, dai, _unpack_b(dbr), _unpack_b(dbi)))
    small = {
        "norm_mix": dnm, "q_norm": dqw.reshape(NQ, HEAD).sum(0) * HEAD ** -0.5, "k_norm": dkw.reshape(NKV, HEAD).sum(0),
        "attn_sinks": dsink[0, 0:NQ], "lam_re": dlr, "lam_im": dli, "log_dt": dldt,
        "ssm_b_re": dbre, "ssm_b_im": dbim, "ssm_c_re": _unpack_c(jnp.swapaxes(dcr, 1, 2)), "ssm_c_im": _unpack_c(jnp.swapaxes(dci, 1, 2)),
        "ssm_d": dd, "attn_branch_norm": dabn, "ssm_branch_norm": dsbn, "norm_ffn": dgf,
    }
    return loss[0, 0], dx, dmpad[META0:PAD], small


def kernel(x, meta_tokens, norm_mix, w_in, q_norm, k_norm, attn_sinks, lam_re, lam_im, log_dt, ssm_b_re, ssm_b_im, ssm_c_re, ssm_c_im, ssm_d, w_glu, attn_branch_norm, ssm_branch_norm, w_out, norm_ffn, w_ffn_in, w_ffn_out, loss_target, m_meta_tokens, m_norm_mix, m_w_in, m_q_norm, m_k_norm, m_attn_sinks, m_lam_re, m_lam_im, m_log_dt, m_ssm_b_re, m_ssm_b_im, m_ssm_c_re, m_ssm_c_im, m_ssm_d, m_w_glu, m_attn_branch_norm, m_ssm_branch_norm, m_w_out, m_norm_ffn, m_w_ffn_in, m_w_ffn_out, v_meta_tokens, v_norm_mix, v_w_in, v_q_norm, v_k_norm, v_attn_sinks, v_lam_re, v_lam_im, v_log_dt, v_ssm_b_re, v_ssm_b_im, v_ssm_c_re, v_ssm_c_im, v_ssm_d, v_w_glu, v_attn_branch_norm, v_ssm_branch_norm, v_w_out, v_norm_ffn, v_w_ffn_in, v_w_ffn_out):
    args = dict(locals())
    names = ["meta_tokens", "norm_mix", "w_in", "q_norm", "k_norm", "attn_sinks", "lam_re", "lam_im", "log_dt",
             "ssm_b_re", "ssm_b_im", "ssm_c_re", "ssm_c_im", "ssm_d", "w_glu", "attn_branch_norm",
             "ssm_branch_norm", "w_out", "norm_ffn", "w_ffn_in", "w_ffn_out"]
    big_names = [s[0] for s in SHARDED]
    xs, ys, cs = _place()
    chip = 2 * xs + ys

    first = _FirstGather(args["w_in"][0].astype(BF), meta_tokens)
    shards = [None] + [_tie(args[n][0], first.token).astype(BF) for n in big_names[1:]]
    place = jnp.stack([chip, cs]).astype(jnp.int32)
    ex = _Exchange(shards[1:], place)

    swapped = ("ssm_b_re", "ssm_b_im")

    def natural(n, a):
        return jnp.swapaxes(a, -1, -2) if n in swapped else a

    p = {n: natural(n, args[n][0]) for n in SMALL}
    loss, dx, dmeta, gsmall = local_step(x[0], loss_target[0], p, first, ex)

    out = {}

    def finish(idxs, name):
        mine = [sum8("sum8_" + SHARDED[i][0], place, ex.parts[i], ex.full[i], SHARDED[i][3]) if i in ex.full
                else sum4("sum4_" + SHARDED[i][0], place, ex.parts[i], ex.halves[i], SHARDED[i][3]) for i in idxs]
        for i, g in zip(idxs, join_halves(name, mine, idxs)):
            n = SHARDED[i][0]
            out[n] = tuple(t[None] for t in adamw("adamw_" + n, args[n][0], g, args["m_" + n][0], args["v_" + n][0]))

    finish((0, 1, 2, 3, 4), "join_halves")
    small_list = [gsmall[n].reshape(natural(n, args[n]).shape) for n in SMALL] + [dmeta, loss.reshape(1, 1)]
    red = allreduce_small(_pack_small(small_list))
    *small_red, dmeta_sum, loss = _unpack_small(red, small_list)
    loss = loss[0, 0]
    gmeta = lax.dynamic_slice_in_dim(dmeta_sum, chip * (D // 4), D // 4, axis=1)
    out["meta_tokens"] = tuple(adamw("adamw_meta", meta_tokens, gmeta, m_meta_tokens, v_meta_tokens))
    ds, nms, nvs = adamw_small([natural(n, args[n]) for n in SMALL], small_red,
                               [natural(n, args["m_" + n]) for n in SMALL],
                               [natural(n, args["v_" + n]) for n in SMALL])
    for n, g_, d_, m_, v_ in zip(SMALL, small_red, ds, nms, nvs):
        out[n] = tuple(natural(n, t) for t in (g_, d_, m_, v_))

    res = [loss, dx[None]]
    for k in range(4):
        res += [out[n][k] for n in names]
    return tuple(res)
```

```python
import functools
import math

import jax
import jax.numpy as jnp
from jax import lax
from jax.experimental import pallas as pl
from jax.experimental.pallas import tpu as pltpu

F32 = jnp.float32
BF = jnp.bfloat16

D = 1024
N_META = 16
HEAD = 64
NQ = 16
NKV = 4
QW = NQ * HEAD
KVW = NKV * HEAD
WIN = 128
GROUPS = 64
GCH = 16
NSTATE = 64
SW = GROUPS * NSTATE
DFF = 2816
IN_COLS = QW + 2 * KVW + D + 2 * D
PAD = 256
META0 = PAD - N_META
EPS = 1e-6
NEG = -1e30

TM = 256
TS = 256
LC = 512
NLC = SW // LC

LR, B1, B2, AEPS, WD, STEP = 0.001, 0.9, 0.999, 1e-08, 0.01, 10
BC1 = 1.0 - B1 ** STEP
BC2 = 1.0 - B2 ** STEP

MESH = pl.DeviceIdType.MESH
ANY = pl.BlockSpec(memory_space=pl.ANY)


def _cp(sem=None, vmem_mb=48):
    return pltpu.CompilerParams(dimension_semantics=sem, vmem_limit_bytes=vmem_mb << 20)


def _full(shape):
    n = len(shape)
    return pl.BlockSpec(shape, lambda *a: (0,) * n)


def _const(shape):
    n = len(shape)
    return pl.BlockSpec(shape, lambda *a: (0,) * n, pipeline_mode=pl.Buffered(1))


def _rows(tm, width):
    return pl.BlockSpec((tm, width), lambda i: (i, 0))


def _dot(a, b):
    return jnp.dot(a, b, preferred_element_type=F32)


def _dot_nt(a, b):
    return lax.dot_general(a, b, (((1,), (1,)), ((), ())), preferred_element_type=F32)


def _dot_tn(a, b):
    return lax.dot_general(a, b, (((0,), (0,)), ((), ())), preferred_element_type=F32)


def _sigmoid(x):
    return 0.5 * jnp.tanh(0.5 * x) + 0.5


def _seg_mean(x, bd, split=True):
    outs = []
    for j in range(x.shape[1] // 256):
        c = x[:, 256 * j:256 * (j + 1)]
        hi = c.astype(BF)
        part = _dot(hi, bd)
        if split:
            part += _dot((c - hi.astype(F32)).astype(BF), bd)
        outs.append(part)
    out = outs[0] if len(outs) == 1 else jnp.concatenate(outs, axis=1)
    return out * (1.0 / HEAD)


_GC = math.sqrt(2.0 / math.pi)


def _gelu(x):
    t = jnp.tanh(_GC * (x + 0.044715 * x * x * x))
    return 0.5 * x * (1.0 + t), t


def _gelu_grad(x, t):
    return 0.5 * (1.0 + t) + 0.5 * x * (1.0 - t * t) * _GC * (1.0 + 3.0 * 0.044715 * x * x)


def _seq_specs():
    assert TM == PAD
    return [pl.BlockSpec((TM, D), lambda i: (jnp.maximum(i - 1, 0), 0)), _full((PAD, D))]


def _seq_tile(x_ref, m_ref):
    return jnp.where(pl.program_id(0) == 0, m_ref[...], x_ref[...])


def in_proj_fwd(x, mpad, w_in, nm, qn_t, kn_t, bd):
    lp = x.shape[0] + PAD

    def body(x_ref, m_ref, w_ref, nm_ref, qn_ref, kn_ref, bd_ref,
             xn_o, qr_o, kr_o, qn_o, kn_o, v_o, u_o, gg_o):
        h = _seq_tile(x_ref, m_ref)
        r = lax.rsqrt(jnp.mean(h * h, axis=-1, keepdims=True) + EPS)
        xb = ((h * r) * nm_ref[...]).astype(BF)
        xn_o[...] = xb
        bdv = bd_ref[...]
        q = _dot(xb, w_ref[:, 0:QW])
        qr_o[...] = q
        qn_o[...] = (q * lax.rsqrt(_seg_mean(q * q, bdv, False) + EPS) * qn_ref[...]).astype(BF)
        k = _dot(xb, w_ref[:, QW:QW + KVW])
        kr_o[...] = k
        kn_o[...] = (k * lax.rsqrt(_seg_mean(k * k, bdv, False) + EPS) * kn_ref[...]).astype(BF)
        v_o[...] = _dot(xb, w_ref[:, QW + KVW:QW + 2 * KVW]).astype(BF)
        u_o[...] = _dot(xb, w_ref[:, QW + 2 * KVW:QW + 2 * KVW + D])
        gg_o[...] = _dot(xb, w_ref[:, QW + 2 * KVW + D:IN_COLS])

    outs = [(D, BF), (QW, F32), (KVW, F32), (QW, BF), (KVW, BF), (KVW, BF), (D, F32), (2 * D, F32)]
    return pl.pallas_call(
        body, name="in_proj_fwd", grid=(lp // TM,),
        in_specs=_seq_specs() + [_full((D, IN_COLS)), _full((1, D)), _full((1, QW)), _full((1, KVW)),
                                 _full((256, 256))],
        out_specs=[_rows(TM, w) for w, _ in outs],
        out_shape=[jax.ShapeDtypeStruct((lp, w), t) for w, t in outs],
        compiler_params=_cp(("arbitrary",)),
    )(x, mpad, w_in, nm, qn_t, kn_t, bd)


def in_proj_bwd(dqn, dkn, dv, du, dgg, qr, kr, x, mpad, dh1, w_in, nm, qn_t, kn_t, bd):
    lp = x.shape[0] + PAD

    def body(dqn_ref, dkn_ref, dv_ref, du_ref, dgg_ref, qr_ref, kr_ref, x_ref, m_ref, dh1_ref,
             w_ref, nm_ref, qn_ref, kn_ref, bd_ref,
             dproj_o, dx_o, dm_o, dnm_o, dqw_o, dkw_o):
        i = pl.program_id(0)

        @pl.when(i == 0)
        def _():
            dnm_o[...] = jnp.zeros_like(dnm_o)
            dqw_o[...] = jnp.zeros_like(dqw_o)
            dkw_o[...] = jnp.zeros_like(dkw_o)

        bdv = bd_ref[...]

        def norm_bwd(x, dy, g, acc_ref):
            rr = lax.rsqrt(_seg_mean(x * x, bdv, False) + EPS)
            xh = x * rr
            acc_ref[...] += jnp.sum(dy * xh, axis=0, keepdims=True)
            dxh = dy * g
            return rr * (dxh - xh * _seg_mean(dxh * xh, bdv))

        dq = norm_bwd(qr_ref[...], dqn_ref[...], qn_ref[...], dqw_o)
        dk = norm_bwd(kr_ref[...], dkn_ref[...], kn_ref[...], dkw_o)
        dproj_o[:, 0:QW] = dq.astype(BF)
        dproj_o[:, QW:QW + KVW] = dk.astype(BF)
        dproj_o[:, QW + KVW:QW + 2 * KVW] = dv_ref[...].astype(BF)
        dproj_o[:, QW + 2 * KVW:QW + 2 * KVW + D] = du_ref[...]
        dproj_o[:, QW + 2 * KVW + D:IN_COLS] = dgg_ref[...]
        dxn = _dot_nt(dproj_o[...], w_ref[...])
        h = _seq_tile(x_ref, m_ref)
        r = lax.rsqrt(jnp.mean(h * h, axis=-1, keepdims=True) + EPS)
        xh = h * r
        dnm_o[...] += jnp.sum(dxn * xh, axis=0, keepdims=True)
        dxh = dxn * nm_ref[...]
        dh0 = dh1_ref[...] + r * (dxh - xh * jnp.mean(dxh * xh, axis=-1, keepdims=True))
        dx_o[...] = dh0

        @pl.when(i == 0)
        def _():
            dm_o[...] = dh0

    return pl.pallas_call(
        body, name="in_proj_bwd", grid=(lp // TM,),
        in_specs=[_rows(TM, QW), _rows(TM, KVW), _rows(TM, KVW), _rows(TM, D), _rows(TM, 2 * D),
                  _rows(TM, QW), _rows(TM, KVW)] + _seq_specs() + [_rows(TM, D),
                  _full((D, IN_COLS)), _full((1, D)), _full((1, QW)), _full((1, KVW)), _full((256, 256))],
        out_specs=[_rows(TM, IN_COLS), _seq_specs()[0], _full((PAD, D)), _full((1, D)), _full((1, QW)),
                   _full((1, KVW))],
        out_shape=[jax.ShapeDtypeStruct((lp, IN_COLS), BF), jax.ShapeDtypeStruct((lp - PAD, D), F32),
                   jax.ShapeDtypeStruct((PAD, D), F32),
                   jax.ShapeDtypeStruct((1, D), F32), jax.ShapeDtypeStruct((1, QW), F32),
                   jax.ShapeDtypeStruct((1, KVW), F32)],
        compiler_params=_cp(("arbitrary",)),
    )(dqn, dkn, dv, du, dgg, qr, kr, x, mpad, dh1, w_in, nm, qn_t, kn_t, bd)


def _att_bias(b):
    qi = lax.broadcasted_iota(jnp.int32, (WIN, WIN), 0)
    kj = lax.broadcasted_iota(jnp.int32, (WIN, WIN), 1)
    upper = kj > qi
    valid = (upper & ((b - 1) * WIN + kj >= PAD)) | (jnp.logical_not(upper) & (b * WIN + kj >= META0))
    bias_pc = jnp.where(valid, 0.0, NEG)
    kj1 = lax.broadcasted_iota(jnp.int32, (1, WIN), 1)
    bias_m = jnp.where((kj1 >= WIN - N_META) & (b * WIN >= PAD), 0.0, NEG)
    return bias_m, bias_pc, upper


def _dup_half(ref, kp, pos, lo):
    pair = ref[:, 128 * kp:128 * kp + 128].astype(F32)
    rolled = pltpu.roll(pair, 64, axis=1)
    keep = lo if pos == 0 else jnp.logical_not(lo)
    return jnp.where(keep, pair, rolled).astype(BF)


def _stack_heads(ref, kv, lo):
    parts = []
    for pp in range(2):
        pair = ref[:, 256 * kv + 128 * pp:256 * kv + 128 * pp + 128]
        zero = jnp.zeros_like(pair)
        parts.append(jnp.where(lo, pair, zero))
        parts.append(jnp.where(lo, zero, pair))
    return jnp.concatenate(parts, axis=0)


def _unstack_heads(x4, lo):
    a = jnp.where(lo, x4[0:WIN], x4[WIN:2 * WIN])
    b = jnp.where(lo, x4[2 * WIN:3 * WIN], x4[3 * WIN:4 * WIN])
    return a, b


def _split_pc(x, upper):
    zero = jnp.zeros_like(x)
    return jnp.where(upper, x, zero).astype(BF), jnp.where(upper, zero, x).astype(BF)


def _head_softmax(s_m, s_p, s_c, bias, sink):
    bias_m, bias_pc, upper = bias
    sm = s_m + bias_m
    spc = jnp.where(upper, s_p, s_c) + bias_pc
    m = jnp.maximum(jnp.max(jnp.maximum(sm, spc), axis=-1, keepdims=True), sink)
    em, epc = jnp.exp(sm - m), jnp.exp(spc - m)
    esink = jnp.exp(sink - m)
    inv = 1.0 / (esink + jnp.sum(em + epc, axis=-1, keepdims=True))
    return em, epc, inv, esink


def _kv_specs():
    return [pl.BlockSpec((WIN, KVW), lambda b: (1, 0)),
            pl.BlockSpec((WIN, KVW), lambda b: (jnp.maximum(b - 1, 0), 0)),
            pl.BlockSpec((WIN, KVW), lambda b: (b, 0))]


def attention_fwd(sinks, qn, kn, v):
    lp = qn.shape[0]

    def body(sink_ref, q_ref, km_ref, kp_ref, kc_ref, vm_ref, vp_ref, vc_ref, o_ref):
        b = pl.program_id(0)
        bias = _att_bias(b)
        lo = lax.broadcasted_iota(jnp.int32, (WIN, WIN), 1) < HEAD
        for kv in range(NKV):
            kp, pos = kv // 2, kv % 2
            kds = [_dup_half(r, kp, pos, lo) for r in (km_ref, kp_ref, kc_ref)]
            vds = [_dup_half(r, kp, pos, lo) for r in (vm_ref, vp_ref, vc_ref)]
            q4 = _stack_heads(q_ref, kv, lo)
            ss = [_dot_nt(q4, kd) for kd in kds]
            es, invs = ([], [], []), []
            for h in range(4):
                rs = slice(WIN * h, WIN * h + WIN)
                em, epc, inv, _ = _head_softmax(ss[0][rs], ss[1][rs], ss[2][rs], bias, sink_ref[4 * kv + h])
                e_p, e_c = _split_pc(epc, bias[2])
                for lst, e in zip(es, (em.astype(BF), e_p, e_c)):
                    lst.append(e)
                invs.append(inv)
            e_m, e_p, e_c = (jnp.concatenate(lst, axis=0) for lst in es)
            o4 = (_dot(e_m, vds[0]) + _dot(e_p, vds[1]) + _dot(e_c, vds[2])) * jnp.concatenate(invs, axis=0)
            oa, ob = _unstack_heads(o4, lo)
            o_ref[:, 256 * kv:256 * kv + 128] = oa
            o_ref[:, 256 * kv + 128:256 * kv + 256] = ob

    return pl.pallas_call(
        body, name="attention_fwd", grid=(lp // WIN,),
        in_specs=[pl.BlockSpec(memory_space=pltpu.SMEM), _rows(WIN, QW)] + _kv_specs() + _kv_specs(),
        out_specs=_rows(WIN, QW),
        out_shape=jax.ShapeDtypeStruct((lp, QW), F32),
        compiler_params=_cp(("arbitrary",)),
    )(sinks, qn, kn, kn, kn, v, v, v)


def attention_bwd(sinks, qn, kn, v, do):
    lp = qn.shape[0]
    nb = lp // WIN

    def body(sink_ref, q_ref, km_ref, kp_ref, kc_ref, vm_ref, vp_ref, vc_ref, do_ref,
             dq_ref, dk_hbm, dv_hbm, dsink_ref, dk_acc, dv_acc):
        b = pl.program_id(0)

        @pl.when(b == 0)
        def _():
            dk_acc[...] = jnp.zeros_like(dk_acc)
            dv_acc[...] = jnp.zeros_like(dv_acc)
            dsink_ref[...] = jnp.zeros_like(dsink_ref)

        bias = _att_bias(b)
        upper = bias[2]
        lo = lax.broadcasted_iota(jnp.int32, (WIN, WIN), 1) < HEAD
        lane8 = lax.broadcasted_iota(jnp.int32, (8, 128), 1)
        starts = [WIN, pl.multiple_of(jnp.maximum(b - 1, 0) * WIN, WIN), pl.multiple_of(b * WIN, WIN)]
        for kv in range(NKV):
            kp, pos = kv // 2, kv % 2
            keep = lo if pos == 0 else jnp.logical_not(lo)
            kds = [_dup_half(r, kp, pos, lo) for r in (km_ref, kp_ref, kc_ref)]
            vds = [_dup_half(r, kp, pos, lo) for r in (vm_ref, vp_ref, vc_ref)]
            q4 = _stack_heads(q_ref, kv, lo)
            do4 = _stack_heads(do_ref, kv, lo)
            ss = [_dot_nt(q4, kd) for kd in kds]
            dps = [_dot_nt(do4, vd) for vd in vds]
            ds_l, p_l = ([], [], []), ([], [], [])
            for h in range(4):
                rs = slice(WIN * h, WIN * h + WIN)
                em, epc, inv, esink = _head_softmax(ss[0][rs], ss[1][rs], ss[2][rs], bias, sink_ref[4 * kv + h])
                p_m, p_pc = em * inv, epc * inv
                dp_m = dps[0][rs]
                dp_pc = jnp.where(upper, dps[1][rs], dps[2][rs])
                delta = jnp.sum(p_m * dp_m + p_pc * dp_pc, axis=-1, keepdims=True)
                val = -jnp.sum(esink * inv * delta, axis=0, keepdims=True)
                dsink_ref[...] += jnp.where(lane8 == 4 * kv + h, val, 0.0)
                for lst, t in zip(ds_l, ((p_m * (dp_m - delta)).astype(BF),) + _split_pc(p_pc * (dp_pc - delta), upper)):
                    lst.append(t)
                for lst, t in zip(p_l, (p_m.astype(BF),) + _split_pc(p_pc, upper)):
                    lst.append(t)
            dss = [jnp.concatenate(lst, axis=0) for lst in ds_l]
            ps = [jnp.concatenate(lst, axis=0) for lst in p_l]
            dq4 = _dot(dss[0], kds[0]) + _dot(dss[1], kds[1]) + _dot(dss[2], kds[2])
            dqa, dqb = _unstack_heads(dq4, lo)
            dq_ref[:, 256 * kv:256 * kv + 128] = dqa
            dq_ref[:, 256 * kv + 128:256 * kv + 256] = dqb
            for x in range(3):
                dkd = _dot_tn(dss[x], q4)
                dvd = _dot_tn(ps[x], do4)
                dkd = jnp.where(keep, dkd + pltpu.roll(dkd, 64, axis=1), 0.0)
                dvd = jnp.where(keep, dvd + pltpu.roll(dvd, 64, axis=1), 0.0)
                dk_acc[pl.ds(starts[x], WIN), 128 * kp:128 * kp + 128] += dkd
                dv_acc[pl.ds(starts[x], WIN), 128 * kp:128 * kp + 128] += dvd

        @pl.when(b == nb - 1)
        def _():
            pltpu.sync_copy(dk_acc, dk_hbm)
            pltpu.sync_copy(dv_acc, dv_hbm)

    return pl.pallas_call(
        body, name="attention_bwd", grid=(nb,),
        in_specs=[pl.BlockSpec(memory_space=pltpu.SMEM), _rows(WIN, QW)] + _kv_specs() + _kv_specs()
                 + [_rows(WIN, QW)],
        out_specs=[_rows(WIN, QW), ANY, ANY, _full((8, 128))],
        out_shape=[jax.ShapeDtypeStruct((lp, QW), F32), jax.ShapeDtypeStruct((lp, KVW), F32),
                   jax.ShapeDtypeStruct((lp, KVW), F32), jax.ShapeDtypeStruct((8, 128), F32)],
        scratch_shapes=[pltpu.VMEM((lp, KVW), F32), pltpu.VMEM((lp, KVW), F32)],
        compiler_params=_cp(("arbitrary",)),
    )(sinks, qn, kn, kn, kn, v, v, v, do)


PW_ROWS = 64
SEG = TS // 8


def _segment_order():
    rho = jnp.arange(TS)
    token = SEG * (rho % 8) + rho // 8
    p = (token[:, None] == jnp.arange(TS)[None, :]).astype(BF)
    return p, p.T


def _reorder(p, x):
    hi = x.astype(BF)
    r1 = x - hi.astype(F32)
    mid = r1.astype(BF)
    lo = (r1 - mid.astype(F32)).astype(BF)
    return _dot(p, hi) + _dot(p, mid) + _dot(p, lo)


def _segment_scan(xr_ref, xi_ref, pw_ref, base, ls, c_r, c_i, reverse, visit=None):
    ar, ai = pw_ref[base:base + 8, ls], pw_ref[base + 8:base + 16, ls]
    order = list(range(SEG - 1, -1, -1)) if reverse else list(range(SEG))
    s_r = s_i = jnp.zeros_like(ar)
    for i in order:
        rows = pl.ds(8 * i, 8)
        s_r, s_i = ar * s_r - ai * s_i + xr_ref[rows, ls], ar * s_i + ai * s_r + xi_ref[rows, ls]
        xr_ref[rows, ls] = s_r
        xi_ref[rows, ls] = s_i
    rid = lax.broadcasted_iota(jnp.int32, s_r.shape, 0)
    if reverse:
        e_r = jnp.where(rid < 7, pltpu.roll(s_r, 7, axis=0), c_r)
        e_i = jnp.where(rid < 7, pltpu.roll(s_i, 7, axis=0), c_i)
    else:
        e_r = jnp.where(rid >= 1, pltpu.roll(s_r, 1, axis=0), c_r)
        e_i = jnp.where(rid >= 1, pltpu.roll(s_i, 1, axis=0), c_i)
    for n, k in enumerate((1, 2, 4)):
        shift = 8 - k if reverse else k
        t_r, t_i = pltpu.roll(e_r, shift, axis=0), pltpu.roll(e_i, shift, axis=0)
        kr = pw_ref[base + 16 + 16 * n:base + 24 + 16 * n, ls]
        ki = pw_ref[base + 24 + 16 * n:base + 32 + 16 * n, ls]
        e_r, e_i = e_r + kr * t_r - ki * t_i, e_i + kr * t_i + ki * t_r
    last = 0 if reverse else 7
    a16r, a16i = pw_ref[base + 16 + last:base + 17 + last, ls], pw_ref[base + 24 + last:base + 25 + last, ls]
    lr, li, fr, fi = (t[last:last + 1] for t in (e_r, e_i, s_r, s_i))
    out = (a16r * lr - a16i * li + fr, a16r * li + a16i * lr + fi)
    d_r, d_i = e_r, e_i
    extra = None
    for i in order:
        rows = pl.ds(8 * i, 8)
        d_r, d_i = ar * d_r - ai * d_i, ar * d_i + ai * d_r
        f_r, f_i = xr_ref[rows, ls] + d_r, xi_ref[rows, ls] + d_i
        xr_ref[rows, ls] = f_r
        xi_ref[rows, ls] = f_i
        if visit is not None:
            extra = visit(i, f_r, f_i, extra)
    return out if visit is None else (out, extra)


def ssm_fwd(u, pw, br, bi, cr, ci, dsk, perm):
    lp = u.shape[0]

    def body(u_ref, pw_ref, br_ref, bi_ref, cr_ref, ci_ref, d_ref, p_ref, pt_ref,
             y_ref, sr_ref, si_ref, car_r, car_i, yp_s):
        i = pl.program_id(0)

        @pl.when(i == 0)
        def _():
            car_r[...] = jnp.zeros_like(car_r)
            car_i[...] = jnp.zeros_like(car_i)

        u = u_ref[...]
        ub = _dot(p_ref[...], u.astype(BF)).astype(BF)
        for j in range(NLC):
            uj = ub[:, 128 * j:128 * j + 128]
            sr_ref[:, LC * j:LC * j + LC] = _dot(uj, br_ref[j])
            si_ref[:, LC * j:LC * j + LC] = _dot(uj, bi_ref[j])
        for j in range(NLC):
            ls = slice(LC * j, LC * j + LC)
            car_r[:, ls], car_i[:, ls] = _segment_scan(sr_ref, si_ref, pw_ref, 0, ls, car_r[:, ls], car_i[:, ls],
                                                       False)
        for j in range(NLC):
            ls = slice(LC * j, LC * j + LC)
            yp_s[:, 128 * j:128 * j + 128] = (_dot(sr_ref[:, ls].astype(BF), cr_ref[j])
                                              - _dot(si_ref[:, ls].astype(BF), ci_ref[j]))
        y_ref[...] = _reorder(pt_ref[...], yp_s[...]) + d_ref[...] * u

    p, pt = perm
    return pl.pallas_call(
        body, name="ssm_fwd", grid=(lp // TS,),
        in_specs=[_rows(TS, D), _full((2 * PW_ROWS, SW)), _full((NLC, 128, LC)), _full((NLC, 128, LC)),
                  _full((NLC, LC, 128)), _full((NLC, LC, 128)), _full((1, D)), _full((TS, TS)), _full((TS, TS))],
        out_specs=[_rows(TS, D), _rows(TS, SW), _rows(TS, SW)],
        out_shape=[jax.ShapeDtypeStruct((lp, D), F32), jax.ShapeDtypeStruct((lp, SW), F32),
                   jax.ShapeDtypeStruct((lp, SW), F32)],
        scratch_shapes=[pltpu.VMEM((1, SW), F32), pltpu.VMEM((1, SW), F32), pltpu.VMEM((TS, D), F32)],
        compiler_params=_cp(("arbitrary",)),
    )(u, pw, br, bi, cr, ci, dsk, p, pt)


def ssm_bwd(dy, u, sr, si, pw, br, bi, cr, ci, dsk, perm):
    lp = u.shape[0]
    nt = lp // TS

    def rev(i):
        return (nt - 1 - i, 0)

    def prev8(i):
        return (jnp.maximum((nt - 1 - i) * (TS // 8) - 1, 0), 0)

    def body(dy_ref, u_ref, sr_ref, si_ref, pr8_ref, pi8_ref, pw_ref, br_ref, bi_ref, cr_ref, ci_ref, d_ref,
             p_ref, pt_ref, du_ref, da_ref, dbr_ref, dbi_ref, dcr_ref, dci_ref, dd_ref,
             gr_s, gi_s, car_r, car_i, dup_s):
        i = pl.program_id(0)

        @pl.when(i == 0)
        def _():
            car_r[...] = jnp.zeros_like(car_r)
            car_i[...] = jnp.zeros_like(car_i)
            for ref in (da_ref, dbr_ref, dbi_ref, dcr_ref, dci_ref, dd_ref):
                ref[...] = jnp.zeros_like(ref)

        keep = 1.0 - (i == nt - 1).astype(F32)
        dy = dy_ref[...]
        u = u_ref[...]
        dd_ref[...] += jnp.sum(dy * u, axis=0, keepdims=True)
        pm = p_ref[...]
        dyb = _dot(pm, dy.astype(BF)).astype(BF)
        ub = _dot(pm, u.astype(BF)).astype(BF)
        for j in range(NLC):
            ls = slice(LC * j, LC * j + LC)
            dyj = dyb[:, 128 * j:128 * j + 128]
            gr_s[:, ls] = _dot_nt(dyj, cr_ref[j])
            gi_s[:, ls] = -_dot_nt(dyj, ci_ref[j])
            dcr_ref[j] += _dot_tn(dyj, sr_ref[:, ls].astype(BF))
            dci_ref[j] -= _dot_tn(dyj, si_ref[:, ls].astype(BF))
        rid = lax.broadcasted_iota(jnp.int32, (8, LC), 0)
        for j in range(NLC):
            ls = slice(LC * j, LC * j + LC)

            def accumulate(i, g_r, g_i, acc):
                if i >= 1:
                    rows = pl.ds(8 * (i - 1), 8)
                    p_r, p_i = sr_ref[rows, ls], si_ref[rows, ls]
                else:
                    rows = pl.ds(8 * (SEG - 1), 8)
                    p_r = jnp.where(rid >= 1, pltpu.roll(sr_ref[rows, ls], 1, axis=0), pr8_ref[7:8, ls] * keep)
                    p_i = jnp.where(rid >= 1, pltpu.roll(si_ref[rows, ls], 1, axis=0), pi8_ref[7:8, ls] * keep)
                t_r, t_i = g_r * p_r + g_i * p_i, g_i * p_r - g_r * p_i
                return (t_r, t_i) if acc is None else (acc[0] + t_r, acc[1] + t_i)

            (car_r[:, ls], car_i[:, ls]), (t_r, t_i) = _segment_scan(
                gr_s, gi_s, pw_ref, PW_ROWS, ls, car_r[:, ls], car_i[:, ls], True, accumulate)
            da_ref[0:8, ls] += t_r
            da_ref[8:16, ls] += t_i
        for j in range(NLC):
            ls = slice(LC * j, LC * j + LC)
            uj = ub[:, 128 * j:128 * j + 128]
            grb = gr_s[:, ls].astype(BF)
            gib = gi_s[:, ls].astype(BF)
            dup_s[:, 128 * j:128 * j + 128] = _dot_nt(grb, br_ref[j]) + _dot_nt(gib, bi_ref[j])
            dbr_ref[j] += _dot_tn(uj, grb)
            dbi_ref[j] += _dot_tn(uj, gib)
        du_ref[...] = (_reorder(pt_ref[...], dup_s[...]) + d_ref[...] * dy).astype(BF)

    p, pt = perm
    return pl.pallas_call(
        body, name="ssm_bwd", grid=(nt,),
        in_specs=[pl.BlockSpec((TS, D), rev), pl.BlockSpec((TS, D), rev), pl.BlockSpec((TS, SW), rev),
                  pl.BlockSpec((TS, SW), rev), pl.BlockSpec((8, SW), prev8), pl.BlockSpec((8, SW), prev8),
                  _full((2 * PW_ROWS, SW)), _full((NLC, 128, LC)), _full((NLC, 128, LC)),
                  _full((NLC, LC, 128)), _full((NLC, LC, 128)), _full((1, D)), _full((TS, TS)), _full((TS, TS))],
        out_specs=[pl.BlockSpec((TS, D), rev), _full((16, SW)), _full((NLC, 128, LC)), _full((NLC, 128, LC)),
                   _full((NLC, 128, LC)), _full((NLC, 128, LC)), _full((1, D))],
        out_shape=[jax.ShapeDtypeStruct((lp, D), BF), jax.ShapeDtypeStruct((16, SW), F32),
                   jax.ShapeDtypeStruct((NLC, 128, LC), F32), jax.ShapeDtypeStruct((NLC, 128, LC), F32),
                   jax.ShapeDtypeStruct((NLC, 128, LC), F32), jax.ShapeDtypeStruct((NLC, 128, LC), F32),
                   jax.ShapeDtypeStruct((1, D), F32)],
        scratch_shapes=[pltpu.VMEM((TS, SW), F32), pltpu.VMEM((TS, SW), F32),
                        pltpu.VMEM((1, SW), F32), pltpu.VMEM((1, SW), F32), pltpu.VMEM((TS, D), F32)],
        compiler_params=_cp(("arbitrary",)),
    )(dy, u, sr, si, sr, si, pw, br, bi, cr, ci, dsk, p, pt)


def _merge_parts(attn, zz, gg, abn, sbn):
    za, zb = zz[:, 0:D], zz[:, D:2 * D]
    sgz = _sigmoid(zb)
    ssm = za * sgz
    ra = lax.rsqrt(jnp.mean(attn * attn, axis=-1, keepdims=True) + EPS)
    rs = lax.rsqrt(jnp.mean(ssm * ssm, axis=-1, keepdims=True) + EPS)
    ah, sh = attn * ra, ssm * rs
    sga, sgs = _sigmoid(gg[:, 0:D]), _sigmoid(gg[:, D:2 * D])
    return za, sgz, ra, rs, ah, sh, sga, sgs, ah * abn, sh * sbn


def mid_fwd(y, attn, gg, x, mpad, w_glu, w_out, abn, sbn):
    lp = y.shape[0]

    def body(y_ref, a_ref, gg_ref, x_ref, m_ref, wg_ref, wo_ref, abn_ref, sbn_ref, z_o, zz_o, mg_o, h1_o):
        z, _ = _gelu(y_ref[...])
        zb = z.astype(BF)
        z_o[...] = zb
        zz = _dot(zb, wg_ref[...])
        zz_o[...] = zz.astype(BF)
        _, _, _, _, _, _, sga, sgs, an, sn = _merge_parts(a_ref[...], zz, gg_ref[...], abn_ref[...], sbn_ref[...])
        mb = (sga * an + sgs * sn).astype(BF)
        mg_o[...] = mb
        h1_o[...] = _seq_tile(x_ref, m_ref) + _dot(mb, wo_ref[...])

    return pl.pallas_call(
        body, name="mid_fwd", grid=(lp // TM,),
        in_specs=[_rows(TM, D), _rows(TM, D), _rows(TM, 2 * D)] + _seq_specs() + [_full((D, 2 * D)), _full((D, D)),
                  _full((1, D)), _full((1, D))],
        out_specs=[_rows(TM, D), _rows(TM, 2 * D), _rows(TM, D), _rows(TM, D)],
        out_shape=[jax.ShapeDtypeStruct((lp, D), BF), jax.ShapeDtypeStruct((lp, 2 * D), BF),
                   jax.ShapeDtypeStruct((lp, D), BF), jax.ShapeDtypeStruct((lp, D), F32)],
        compiler_params=_cp(("arbitrary",)),
    )(y, attn, gg, x, mpad, w_glu, w_out, abn, sbn)


def mid_bwd(dh1b, y, attn, gg, zzb, w_glu, w_out, abn, sbn):
    lp = y.shape[0]

    def body(dh_ref, y_ref, a_ref, gg_ref, zz_ref, wg_ref, wo_ref, abn_ref, sbn_ref,
             dy_o, dattn_o, dgg_o, dzz_o, dabn_o, dsbn_o):
        i = pl.program_id(0)

        @pl.when(i == 0)
        def _():
            dabn_o[...] = jnp.zeros_like(dabn_o)
            dsbn_o[...] = jnp.zeros_like(dsbn_o)

        dm = _dot_nt(dh_ref[...], wo_ref[...])
        abn, sbn = abn_ref[...], sbn_ref[...]
        za, sgz, ra, rs, ah, sh, sga, sgs, an, sn = _merge_parts(
            a_ref[...], zz_ref[...].astype(F32), gg_ref[...], abn, sbn)
        dgg_o[:, 0:D] = (dm * an * sga * (1.0 - sga)).astype(BF)
        dgg_o[:, D:2 * D] = (dm * sn * sgs * (1.0 - sgs)).astype(BF)
        da = dm * sga
        dabn_o[...] += jnp.sum(da * ah, axis=0, keepdims=True)
        dah = da * abn
        dattn_o[...] = (ra * (dah - ah * jnp.mean(dah * ah, axis=-1, keepdims=True))).astype(BF)
        ds = dm * sgs
        dsbn_o[...] += jnp.sum(ds * sh, axis=0, keepdims=True)
        dsh = ds * sbn
        dssm = rs * (dsh - sh * jnp.mean(dsh * sh, axis=-1, keepdims=True))
        dzz_o[:, 0:D] = (dssm * sgz).astype(BF)
        dzz_o[:, D:2 * D] = (dssm * za * sgz * (1.0 - sgz)).astype(BF)
        dz = _dot_nt(dzz_o[...], wg_ref[...])
        yv = y_ref[...]
        _, t = _gelu(yv)
        dy_o[...] = dz * _gelu_grad(yv, t)

    return pl.pallas_call(
        body, name="mid_bwd", grid=(lp // TM,),
        in_specs=[_rows(TM, D), _rows(TM, D), _rows(TM, D), _rows(TM, 2 * D), _rows(TM, 2 * D),
                  _full((D, 2 * D)), _full((D, D)), _full((1, D)), _full((1, D))],
        out_specs=[_rows(TM, D), _rows(TM, D), _rows(TM, 2 * D), _rows(TM, 2 * D), _full((1, D)), _full((1, D))],
        out_shape=[jax.ShapeDtypeStruct((lp, D), F32), jax.ShapeDtypeStruct((lp, D), BF),
                   jax.ShapeDtypeStruct((lp, 2 * D), BF), jax.ShapeDtypeStruct((lp, 2 * D), BF),
                   jax.ShapeDtypeStruct((1, D), F32), jax.ShapeDtypeStruct((1, D), F32)],
        compiler_params=_cp(("arbitrary",)),
    )(dh1b, y, attn, gg, zzb, w_glu, w_out, abn, sbn)


def ffn_fwd_bwd(h1, tgt, w_fi, w_fo, gf):
    lp = h1.shape[0]
    tf = 256
    bounds = (0, 1536, DFF)

    def body(h_ref, t_ref, wi_ref, wo_ref, gf_ref, hn_o, dgu_o, act_o, dh2_o, dh1_o, dh1b_o, dgf_o, loss_o,
             gate_s, up_s):
        i = pl.program_id(0)

        @pl.when(i == 0)
        def _():
            dgf_o[...] = jnp.zeros_like(dgf_o)
            loss_o[...] = jnp.zeros_like(loss_o)

        h = h_ref[...]
        r = lax.rsqrt(jnp.mean(h * h, axis=-1, keepdims=True) + EPS)
        xh = h * r
        hb = (xh * gf_ref[...]).astype(BF)
        hn_o[...] = hb
        h2 = h
        for c0, c1 in zip(bounds[:-1], bounds[1:]):
            gate = _dot(hb, wi_ref[:, c0:c1])
            up = _dot(hb, wi_ref[:, DFF + c0:DFF + c1])
            gate_s[:, c0:c1] = gate
            up_s[:, c0:c1] = up
            ab = (gate * _sigmoid(gate) * up).astype(BF)
            act_o[:, c0:c1] = ab
            h2 = h2 + _dot(ab, wo_ref[c0:c1, :])
        real = (i >= PAD // tf).astype(F32)
        e = (h2 - t_ref[...]) * real
        loss_o[...] += 0.5 * jnp.sum(jnp.sum(e * e, axis=-1, keepdims=True), axis=0, keepdims=True) * (1.0 / D)
        dh2 = e * (1.0 / D)
        db = dh2.astype(BF)
        dh2_o[...] = db
        dhn = jnp.zeros((tf, D), F32)
        for c0, c1 in zip(bounds[:-1], bounds[1:]):
            gate = gate_s[:, c0:c1]
            up = up_s[:, c0:c1]
            sg = _sigmoid(gate)
            dact = _dot_nt(db, wo_ref[c0:c1, :])
            dgate = (dact * up * (sg * (1.0 + gate * (1.0 - sg)))).astype(BF)
            dup = (dact * (gate * sg)).astype(BF)
            dgu_o[:, c0:c1] = dgate
            dgu_o[:, DFF + c0:DFF + c1] = dup
            dhn += _dot_nt(dgate, wi_ref[:, c0:c1]) + _dot_nt(dup, wi_ref[:, DFF + c0:DFF + c1])
        dgf_o[...] += jnp.sum(dhn * xh, axis=0, keepdims=True)
        dxh = dhn * gf_ref[...]
        dh1 = dh2 + r * (dxh - xh * jnp.mean(dxh * xh, axis=-1, keepdims=True))
        dh1_o[...] = dh1
        dh1b_o[...] = dh1.astype(BF)

    return pl.pallas_call(
        body, name="ffn_fwd_bwd", grid=(lp // tf,),
        in_specs=[_rows(tf, D), pl.BlockSpec((tf, D), lambda i: (jnp.maximum(i - PAD // tf, 0), 0)),
                  _const((D, 2 * DFF)), _const((DFF, D)), _full((1, D))],
        out_specs=[_rows(tf, D), _rows(tf, 2 * DFF), _rows(tf, DFF), _rows(tf, D), _rows(tf, D), _rows(tf, D),
                   _full((1, D)), _full((1, 1))],
        out_shape=[jax.ShapeDtypeStruct((lp, D), BF), jax.ShapeDtypeStruct((lp, 2 * DFF), BF),
                   jax.ShapeDtypeStruct((lp, DFF), BF), jax.ShapeDtypeStruct((lp, D), BF),
                   jax.ShapeDtypeStruct((lp, D), F32), jax.ShapeDtypeStruct((lp, D), BF),
                   jax.ShapeDtypeStruct((1, D), F32), jax.ShapeDtypeStruct((1, 1), F32)],
        scratch_shapes=[pltpu.VMEM((tf, DFF), F32), pltpu.VMEM((tf, DFF), F32)],
        compiler_params=_cp(("arbitrary",), vmem_mb=58),
    )(h1, tgt, w_fi, w_fo, gf)


def weight_grad(name, a, b, tn, with_bf16=False):
    lp, k = a.shape
    n = b.shape[1]
    tr = next(t for t in (2112, 1056, 768, 512, 384, 256, 128) if lp % t == 0 and t * k * 2 <= (5 << 20))

    def body(a_ref, b_ref, o_ref, *ob_ref):
        @pl.when(pl.program_id(1) == 0)
        def _():
            o_ref[...] = jnp.zeros_like(o_ref)

        o_ref[...] += _dot_tn(a_ref[...], b_ref[...])
        if with_bf16:
            @pl.when(pl.program_id(1) == lp // tr - 1)
            def _():
                ob_ref[0][...] = o_ref[...].astype(BF)

    spec = pl.BlockSpec((k, tn), lambda j, m: (0, j))
    return pl.pallas_call(
        body, name=name, grid=(n // tn, lp // tr),
        in_specs=[pl.BlockSpec((tr, k), lambda j, m: (m, 0)), pl.BlockSpec((tr, tn), lambda j, m: (m, j))],
        out_specs=[spec, spec] if with_bf16 else spec,
        out_shape=([jax.ShapeDtypeStruct((k, n), F32), jax.ShapeDtypeStruct((k, n), BF)] if with_bf16
                   else jax.ShapeDtypeStruct((k, n), F32)),
        compiler_params=_cp(("arbitrary", "arbitrary")),
    )(a, b)


def _adamw_math(w, g, m, v):
    m = B1 * m + (1.0 - B1) * g
    v = B2 * v + (1.0 - B2) * (g * g)
    delta = -LR * ((m / BC1) / (jnp.sqrt(v / BC2) + AEPS) + WD * w)
    return delta, m, v


def _row_tile(r):
    return next((t for t in (256, 128, 64, 32, 16, 8) if r % t == 0), r)


def adamw(name, w, g, m, v):
    r, c = w.shape
    tr = _row_tile(r)

    def body(w_ref, g_ref, m_ref, v_ref, g_o, d_o, m_o, v_o):
        g = g_ref[...]
        g_o[...] = g
        d_o[...], m_o[...], v_o[...] = _adamw_math(w_ref[...], g, m_ref[...], v_ref[...])

    spec = pl.BlockSpec((tr, c), lambda i: (i, 0))
    return pl.pallas_call(
        body, name=name, grid=(r // tr,), in_specs=[spec] * 4, out_specs=[spec] * 4,
        out_shape=[jax.ShapeDtypeStruct((r, c), F32)] * 4,
        compiler_params=_cp(("arbitrary",)),
    )(w, g, m, v)


def adamw_small(ws, gs, ms, vs):
    n = len(ws)
    steps = 8

    def spec(a):
        if a.ndim == 4:
            return pl.BlockSpec((1, a.shape[1] // steps) + a.shape[2:], lambda i: (0, i, 0, 0))
        nd = a.ndim
        return pl.BlockSpec(a.shape, lambda i: (0,) * nd)

    def body(*refs):
        w, g, m, v, d_o, m_o, v_o = (refs[k * n:(k + 1) * n] for k in range(7))
        for t in range(n):
            d_o[t][...], m_o[t][...], v_o[t][...] = _adamw_math(w[t][...], g[t][...], m[t][...], v[t][...])

    specs = [spec(a) for a in ws]
    res = pl.pallas_call(
        body, name="adamw_small", grid=(steps,), in_specs=specs * 4, out_specs=specs * 3,
        out_shape=[jax.ShapeDtypeStruct(a.shape, F32) for a in ws] * 3,
        compiler_params=_cp(("arbitrary",)),
    )(*ws, *gs, *ms, *vs)
    return res[:n], res[n:2 * n], res[2 * n:]


def pair_sum(name, place, full, got, ax):
    r, c = got.shape
    tr = _row_tile(r)

    def body(s_ref, a_ref, b_ref, o_ref):
        o_ref[...] = (a_ref[...] + b_ref[...]).astype(BF)

    if ax == 1:
        mine = pl.BlockSpec((tr, c), lambda i, s: (s[1] * (r // tr) + i, 0))
    else:
        mine = pl.BlockSpec((tr, c), lambda i, s: (i, s[1]))
    spec = pl.BlockSpec((tr, c), lambda i, s: (i, 0))
    return pl.pallas_call(
        body, name=name,
        grid_spec=pltpu.PrefetchScalarGridSpec(num_scalar_prefetch=1, grid=(r // tr,), in_specs=[mine, spec],
                                               out_specs=spec),
        out_shape=jax.ShapeDtypeStruct((r, c), BF),
        compiler_params=_cp(("arbitrary",)),
    )(place, full, got)


def sum4(name, place, parts, half, ax):
    _, r, c = parts.shape
    tr = _row_tile(r)

    def body(s_ref, p_ref, h_ref, o_ref):
        o_ref[...] = ((p_ref[0].astype(F32) + p_ref[1].astype(F32))
                      + (p_ref[2].astype(F32) + h_ref[...].astype(F32)))

    if ax == 1:
        own = pl.BlockSpec((tr, c), lambda i, s: (i, s[0]))
        out = pl.BlockSpec((tr, c), lambda i, s: (s[1] * (r // tr) + i, 0))
        shape = (2 * r, c)
    else:
        own = pl.BlockSpec((tr, c), lambda i, s: (s[0] * (r // tr) + i, 0))
        out = pl.BlockSpec((tr, c), lambda i, s: (i, s[1]))
        shape = (r, 2 * c)
    return pl.pallas_call(
        body, name=name,
        grid_spec=pltpu.PrefetchScalarGridSpec(
            num_scalar_prefetch=1, grid=(r // tr,),
            in_specs=[pl.BlockSpec((3, tr, c), lambda i, s: (0, i, 0)), own], out_specs=out),
        out_shape=jax.ShapeDtypeStruct(shape, F32),
        compiler_params=_cp(("arbitrary",)),
    )(place, parts, half)


def sum8(name, place, parts, full, ax):
    _, r, c = parts.shape
    tr = _row_tile(r)

    def body(s_ref, p_ref, g_ref, o_ref):
        p = [p_ref[k].astype(F32) for k in range(7)]
        o_ref[...] = ((p[0] + p[1]) + (p[2] + p[3])) + ((p[4] + p[5]) + (p[6] + g_ref[...]))

    if ax == 1:
        own = pl.BlockSpec((tr, c), lambda i, s: (s[1] * (r // tr) + i, s[0]))
        out = pl.BlockSpec((tr, c), lambda i, s: (s[1] * (r // tr) + i, 0))
        shape = (2 * r, c)
    else:
        own = pl.BlockSpec((tr, c), lambda i, s: (s[0] * (r // tr) + i, s[1]))
        out = pl.BlockSpec((tr, c), lambda i, s: (i, s[1]))
        shape = (r, 2 * c)
    return pl.pallas_call(
        body, name=name,
        grid_spec=pltpu.PrefetchScalarGridSpec(
            num_scalar_prefetch=1, grid=(r // tr,),
            in_specs=[pl.BlockSpec((7, tr, c), lambda i, s: (0, i, 0)), own], out_specs=out),
        out_shape=jax.ShapeDtypeStruct(shape, F32),
        compiler_params=_cp(("arbitrary",)),
    )(place, parts, full)


def _place():
    x, y, c = lax.axis_index("x"), lax.axis_index("y"), lax.axis_index("c")
    return x, y, c


def _other_chips(x, y):
    return [(1 - x, y), (x, 1 - y), (1 - x, 1 - y)]


SHARDED = (("w_in", D, IN_COLS, 1), ("w_glu", D, 2 * D, 1), ("w_out", D, D, 0),
           ("w_ffn_in", D, 2 * DFF, 1), ("w_ffn_out", DFF, D, 0))


def _shard_of(ref, axis, k, size):
    if axis == 0:
        return ref.at[pl.ds(k * size, size), :]
    return ref.at[:, pl.ds(k * size, size)]


def _gather_piece(full, dims, k, h):
    r, cc, ax = dims
    if ax == 0:
        return full.at[pl.ds(k * (r // 4) + h * (r // 8), r // 8), :]
    return full.at[pl.ds(h * (r // 2), r // 2), pl.ds(k * (cc // 4), cc // 4)]


def _rows_half(shard, h):
    rs = shard.shape[0]
    return shard.at[pl.ds(h * (rs // 2), rs // 2), :]


SEM = pl.BlockSpec(memory_space=pltpu.SEMAPHORE)
HBM = pl.BlockSpec(memory_space=pltpu.HBM)
EFFECT = pltpu.SideEffectType.DATAFLOW_SIDE_EFFECTING


def _in_hbm(a):
    return pltpu.with_memory_space_constraint(a, pltpu.HBM)


def _split_copy_start(name, copies, ncopies, srcs, lands, after):
    ns, nl = len(srcs), len(lands)

    def body(*refs):
        ins, land_refs = refs[:ns], refs[ns:ns + nl]
        send_sems, recv_sems = refs[ns + nl + 1], refs[ns + nl + 2]
        token = refs[-1]
        for k, (src, dst, dev) in enumerate(copies(ins, land_refs)):
            pltpu.make_async_remote_copy(src_ref=src, dst_ref=dst, send_sem=send_sems.at[k],
                                         recv_sem=recv_sems.at[k], device_id=dev, device_id_type=MESH).start()
        token[...] = jnp.zeros_like(token)

    res = pl.pallas_call(
        body, name=name,
        in_specs=[HBM] * (ns + nl) + [ANY],
        out_specs=(SEM, SEM) + (HBM,) * (ns + nl) + (pl.BlockSpec(memory_space=pltpu.VMEM),),
        out_shape=(pltpu.SemaphoreType.DMA((ncopies,)), pltpu.SemaphoreType.DMA((ncopies,)))
        + tuple(pltpu.HBM(a.shape, a.dtype) for a in list(srcs) + list(lands))
        + (jax.ShapeDtypeStruct((8, 128), F32),),
        input_output_aliases={t: 2 + t for t in range(ns + nl)},
        compiler_params=pltpu.CompilerParams(has_side_effects=EFFECT),
    )(*[_in_hbm(a) for a in srcs], *[_in_hbm(a) for a in lands], after)
    return res[0], res[1], list(res[2:2 + ns]), list(res[2 + ns:2 + ns + nl]), res[-1]


def _split_copy_wait(name, copies, arrivals, send_sems, recv_sems, srcs, lands, after):
    ns, nl = len(srcs), len(lands)

    def body(*refs):
        ins, land_refs = refs[:ns], refs[ns:ns + nl]
        send_sems_ref, recv_sems_ref = refs[ns + nl], refs[ns + nl + 1]
        mine = copies(ins, land_refs)
        for k, ((src, dst, dev), got) in enumerate(zip(mine, arrivals(ins, land_refs))):
            pltpu.make_async_remote_copy(src_ref=src, dst_ref=dst, send_sem=send_sems_ref.at[k],
                                         recv_sem=recv_sems_ref.at[k], device_id=dev,
                                         device_id_type=MESH).wait_send()
            pltpu.make_async_remote_copy(src_ref=got, dst_ref=got, send_sem=send_sems_ref.at[k],
                                         recv_sem=recv_sems_ref.at[k], device_id=dev,
                                         device_id_type=MESH).wait_recv()

    res = pl.pallas_call(
        body, name=name,
        in_specs=[HBM] * (ns + nl) + [SEM, SEM, ANY],
        out_specs=(HBM,) * (ns + nl),
        out_shape=tuple(pltpu.HBM(a.shape, a.dtype) for a in list(srcs) + list(lands)),
        input_output_aliases={t: t for t in range(ns + nl)},
        compiler_params=pltpu.CompilerParams(has_side_effects=EFFECT),
    )(*srcs, *lands, send_sems, recv_sems, after)
    return list(res[:ns]), list(res[ns:])


def _gather_copies(dims):
    def copies(ins, lands):
        x, y, c = _place()
        mine = 2 * x + y
        out = []
        for t, d in enumerate(dims):
            size = (d[0] if d[2] == 0 else d[1]) // 4
            for px, py in _other_chips(x, y):
                for dc in range(2):
                    out.append((_rows_half(ins[t], c), _gather_piece(lands[t], d, mine, c), (px, py, c ^ dc)))
            out.append((ins[t], _shard_of(lands[t], d[2], mine, size), (x, y, 1 - c)))
        return out

    def arrivals(ins, lands):
        x, y, c = _place()
        mine = 2 * x + y
        out = []
        for t, d in enumerate(dims):
            size = (d[0] if d[2] == 0 else d[1]) // 4
            for px, py in _other_chips(x, y):
                for dc in range(2):
                    out.append(_gather_piece(lands[t], d, 2 * px + py, c ^ dc))
            out.append(_shard_of(lands[t], d[2], mine, size))
        return out

    return copies, arrivals


def _half_of(ref, ax, h):
    r, c = ref.shape
    if ax == 1:
        return ref.at[pl.ds(h * (r // 2), r // 2), :]
    return ref.at[:, pl.ds(h * (c // 2), c // 2)]


def _half_shape(r, c, ax):
    return (r // 2, c) if ax == 1 else (r, c // 2)


def pair_exchange(name, grads, idxs):
    n = len(grads)
    which = [SHARDED[i] for i in idxs]

    def body(*refs):
        ins, got = refs[:n], refs[n:2 * n]
        send_sems, recv_sems = refs[2 * n:]
        x, y, c = _place()
        cps = []
        for t, (_, _, _, ax) in enumerate(which):
            cp = pltpu.make_async_remote_copy(
                src_ref=_half_of(ins[t], ax, 1 - c), dst_ref=got[t], send_sem=send_sems.at[t],
                recv_sem=recv_sems.at[t], device_id=(x, y, 1 - c), device_id_type=MESH)
            cp.start()
            cps.append(cp)
        for cp in cps:
            cp.wait()

    return pl.pallas_call(
        body, name=name,
        in_specs=[ANY] * n, out_specs=[ANY] * n,
        out_shape=[jax.ShapeDtypeStruct(_half_shape(r, c, ax), F32) for _, r, c, ax in which],
        scratch_shapes=[pltpu.SemaphoreType.DMA((n,)), pltpu.SemaphoreType.DMA((n,))],
        compiler_params=pltpu.CompilerParams(has_side_effects=True),
    )(*grads)


def _piece_shapes(idxs):
    out = []
    for _, r, c, ax in [SHARDED[i] for i in idxs]:
        hr, hc = _half_shape(r, c, ax)
        out.append((hr // 4, hc) if ax == 0 else (hr, hc // 4))
    return out


def _piece_of(full, ax, k, h):
    r, c = full.shape
    if ax == 1:
        return full.at[pl.ds(h * (r // 2), r // 2), pl.ds(k * (c // 4), c // 4)]
    return full.at[pl.ds(k * (r // 4), r // 4), pl.ds(h * (c // 2), c // 2)]


def _scatter8_copies(idxs):
    which = [SHARDED[i] for i in idxs]

    def copies(ins, lands):
        x, y, c = _place()
        out = []
        for t, (_, _, _, ax) in enumerate(which):
            for j, (px, py) in enumerate(_other_chips(x, y)):
                for d in range(2):
                    out.append((_piece_of(ins[t], ax, 2 * px + py, c ^ d), lands[t].at[2 * j + d], (px, py, c ^ d)))
            out.append((_piece_of(ins[t], ax, 2 * x + y, 1 - c), lands[t].at[6], (x, y, 1 - c)))
        return out

    def arrivals(ins, lands):
        return [lands[t].at[k] for t in range(len(which)) for k in range(7)]

    return copies, arrivals


def scatter_grads(halves, idxs):
    n = len(halves)
    which = [SHARDED[i] for i in idxs]

    def shapes():
        return _piece_shapes(idxs)

    def body(*refs):
        ins, outs = refs[:n], refs[n:2 * n]
        send_sems, recv_sems = refs[2 * n:]
        x, y, c = _place()
        started = []
        for t, (src, dst, (_, _, _, ax)) in enumerate(zip(ins, outs, which)):
            size = src.shape[ax] // 4
            for j, (px, py) in enumerate(_other_chips(x, y)):
                cp = pltpu.make_async_remote_copy(
                    src_ref=_shard_of(src, ax, 2 * px + py, size), dst_ref=dst.at[j],
                    send_sem=send_sems.at[3 * t + j], recv_sem=recv_sems.at[3 * t + j],
                    device_id=(px, py, c), device_id_type=MESH)
                cp.start()
                started.append(cp)
        for cp in started:
            cp.wait()

    return pl.pallas_call(
        body, name="scatter_grads",
        in_specs=[ANY] * n, out_specs=[ANY] * n,
        out_shape=[jax.ShapeDtypeStruct((3,) + s, BF) for s in shapes()],
        scratch_shapes=[pltpu.SemaphoreType.DMA((3 * n,)), pltpu.SemaphoreType.DMA((3 * n,))],
        compiler_params=pltpu.CompilerParams(has_side_effects=True),
    )(*halves)


def join_halves(name, shards, idxs):
    n = len(shards)
    which = [SHARDED[i] for i in idxs]

    def body(*refs):
        ins, outs = refs[:n], refs[n:2 * n]
        send_sems, recv_sems = refs[2 * n:]
        x, y, c = _place()
        cps = []
        for t, (_, _, _, ax) in enumerate(which):
            cp = pltpu.make_async_remote_copy(
                src_ref=_half_of(ins[t], ax, c), dst_ref=_half_of(outs[t], ax, c), send_sem=send_sems.at[t],
                recv_sem=recv_sems.at[t], device_id=(x, y, 1 - c), device_id_type=MESH)
            cp.start()
            cps.append(cp)
        for t, cp in enumerate(cps):
            cp.wait_send()
            theirs = _half_of(outs[t], which[t][3], 1 - c)
            pltpu.make_async_remote_copy(
                src_ref=theirs, dst_ref=theirs, send_sem=send_sems.at[t], recv_sem=recv_sems.at[t],
                device_id=(x, y, 1 - c), device_id_type=MESH).wait_recv()

    return pl.pallas_call(
        body, name=name,
        in_specs=[ANY] * n, out_specs=[ANY] * n,
        out_shape=[jax.ShapeDtypeStruct(s.shape, F32) for s in shards],
        input_output_aliases={t: t for t in range(n)},
        scratch_shapes=[pltpu.SemaphoreType.DMA((n,)), pltpu.SemaphoreType.DMA((n,))],
        compiler_params=pltpu.CompilerParams(has_side_effects=True),
    )(*shards)


def allreduce_small(pack):
    rows = pack.shape[0]
    hr = rows // 2

    def body(p_ref, o_ref, sib_ref, parts_ref, send_sems, recv_sems):
        x, y, c = _place()
        mine = 2 * x + y
        sibling = (x, y, 1 - c)
        swap = pltpu.make_async_remote_copy(
            src_ref=p_ref, dst_ref=sib_ref, send_sem=send_sems.at[0], recv_sem=recv_sems.at[0],
            device_id=sibling, device_id_type=MESH)
        swap.start()
        swap.wait()
        half = pl.ds(pl.multiple_of(c * hr, 8), hr)
        parts_ref[mine] = p_ref[half, :] + sib_ref[half, :]
        cps = []
        for j, chip in enumerate(_other_chips(x, y)):
            cp = pltpu.make_async_remote_copy(
                src_ref=parts_ref.at[mine], dst_ref=parts_ref.at[mine], send_sem=send_sems.at[1 + j],
                recv_sem=recv_sems.at[1 + j], device_id=(*chip, c), device_id_type=MESH)
            cp.start()
            cps.append(cp)
        for j, (px, py) in enumerate(_other_chips(x, y)):
            cps[j].wait_send()
            theirs = parts_ref.at[2 * px + py]
            pltpu.make_async_remote_copy(
                src_ref=theirs, dst_ref=theirs, send_sem=send_sems.at[1 + j], recv_sem=recv_sems.at[1 + j],
                device_id=(px, py, c), device_id_type=MESH).wait_recv()
        o_ref[half, :] = (parts_ref[0] + parts_ref[1]) + (parts_ref[2] + parts_ref[3])
        back = pltpu.make_async_remote_copy(
            src_ref=o_ref.at[half, :], dst_ref=o_ref.at[half, :], send_sem=send_sems.at[4],
            recv_sem=recv_sems.at[4], device_id=sibling, device_id_type=MESH)
        back.start()
        back.wait_send()
        other = pl.ds(pl.multiple_of((1 - c) * hr, 8), hr)
        pltpu.make_async_remote_copy(
            src_ref=o_ref.at[other, :], dst_ref=o_ref.at[other, :], send_sem=send_sems.at[4],
            recv_sem=recv_sems.at[4], device_id=sibling, device_id_type=MESH).wait_recv()

    return pl.pallas_call(
        body, name="allreduce_small",
        in_specs=[pl.BlockSpec(memory_space=pltpu.VMEM)], out_specs=pl.BlockSpec(memory_space=pltpu.VMEM),
        out_shape=jax.ShapeDtypeStruct((rows, 128), F32),
        scratch_shapes=[pltpu.VMEM((rows, 128), F32), pltpu.VMEM((4, hr, 128), F32),
                        pltpu.SemaphoreType.DMA((5,)), pltpu.SemaphoreType.DMA((5,))],
        compiler_params=pltpu.CompilerParams(has_side_effects=True, vmem_limit_bytes=40 << 20),
    )(pack)


def _ssm_discretize(lam_re, lam_im, log_dt, b_re, b_im):
    dt = jnp.exp(log_dt)[:, None]
    mag = jnp.exp(lam_re * dt)
    ar, ai = mag * jnp.cos(lam_im * dt), mag * jnp.sin(lam_im * dt)
    den = lam_re * lam_re + lam_im * lam_im
    nr, ni = ar - 1.0, ai
    fr, fi = (nr * lam_re + ni * lam_im) / den, (ni * lam_re - nr * lam_im) / den
    bbr = fr[:, None, :] * b_re - fi[:, None, :] * b_im
    bbi = fr[:, None, :] * b_im + fi[:, None, :] * b_re
    return ar, ai, bbr, bbi


def _cmul(a, b):
    return a[0] * b[0] - a[1] * b[1], a[0] * b[1] + a[1] * b[0]


def _scan_powers(ar, ai):
    a = (ar.reshape(1, SW), ai.reshape(1, SW))
    big = a
    for _ in range(SEG.bit_length() - 1):
        big = _cmul(big, big)
    assert 1 << (SEG.bit_length() - 1) == SEG
    pows = {1: big}
    pows[2] = _cmul(big, big)
    pows[4] = _cmul(pows[2], pows[2])
    rid = jnp.arange(8)[:, None]
    ones = jnp.ones((8, 1), F32)
    fwd, rev = [ones * a[0], ones * a[1]], [ones * a[0], -ones * a[1]]
    for k in (1, 2, 4):
        pr, pi = pows[k]
        fwd += [jnp.where(rid >= k, pr, 0.0), jnp.where(rid >= k, pi, 0.0)]
        rev += [jnp.where(rid < 8 - k, pr, 0.0), jnp.where(rid < 8 - k, -pi, 0.0)]
    return jnp.concatenate(fwd + rev, axis=0)


def _pack_b(bb):
    t = bb.reshape(NLC, 8, GCH, NSTATE)
    return jnp.einsum("jgcp,gh->jgchp", t, jnp.eye(8, dtype=bb.dtype)).reshape(NLC, 128, LC)


def _unpack_b(db):
    t = jnp.einsum("jgchp,gh->jgcp", db.reshape(NLC, 8, GCH, 8, NSTATE), jnp.eye(8, dtype=db.dtype))
    return t.reshape(GROUPS, GCH, NSTATE)


def _pack_c(cc):
    t = cc.reshape(NLC, 8, GCH, NSTATE).transpose(0, 1, 3, 2)
    return jnp.einsum("jgpc,gh->jgphc", t, jnp.eye(8, dtype=cc.dtype)).reshape(NLC, LC, 128)


def _unpack_c(dc):
    t = jnp.einsum("jgphc,gh->jgpc", dc.reshape(NLC, 8, NSTATE, 8, GCH), jnp.eye(8, dtype=dc.dtype))
    return t.transpose(0, 1, 3, 2).reshape(GROUPS, GCH, NSTATE)


SMALL = ("norm_mix", "q_norm", "k_norm", "attn_sinks", "lam_re", "lam_im", "log_dt", "ssm_b_re", "ssm_b_im",
         "ssm_c_re", "ssm_c_im", "ssm_d", "attn_branch_norm", "ssm_branch_norm", "norm_ffn")


def _pack_small(arrs, rows=None):
    flat = jnp.concatenate([a.reshape(-1).astype(F32) for a in arrs])
    n = flat.shape[0]
    if rows is None:
        rows = -(-n // (128 * 256)) * 256
    return jnp.pad(flat, (0, rows * 128 - n)).reshape(rows, 128)


def _unpack_small(pack, like):
    flat = pack.reshape(-1)
    out, off = [], 0
    for a in like:
        out.append(flat[off:off + a.size].reshape(a.shape))
        off += a.size
    return out


def _gather_half_copies(dims):
    def copies(ins, lands):
        x, y, c = _place()
        mine = 2 * x + y
        out = []
        for t, d in enumerate(dims):
            size = (d[0] if d[2] == 0 else d[1]) // 4
            for chip in _other_chips(x, y):
                out.append((_rows_half(ins[t], c), _gather_piece(lands[t], d, mine, c), (*chip, c)))
            out.append((ins[t], _shard_of(lands[t], d[2], mine, size), (x, y, 1 - c)))
        return out

    def arrivals(ins, lands):
        x, y, c = _place()
        mine = 2 * x + y
        out = []
        for t, d in enumerate(dims):
            size = (d[0] if d[2] == 0 else d[1]) // 4
            for px, py in _other_chips(x, y):
                out.append(_gather_piece(lands[t], d, 2 * px + py, c))
            out.append(_shard_of(lands[t], d[2], mine, size))
        return out

    return copies, arrivals


def gather_forward(fulls, dims):
    n = len(fulls)

    def body(*refs):
        ins, outs = refs[:n], refs[n:2 * n]
        send_sems, recv_sems = refs[2 * n:]
        x, y, c = _place()
        cps = []
        for t in range(n):
            for j, (px, py) in enumerate(_other_chips(x, y)):
                cp = pltpu.make_async_remote_copy(
                    src_ref=_gather_piece(ins[t], dims[t], 2 * px + py, c),
                    dst_ref=_gather_piece(outs[t], dims[t], 2 * px + py, c),
                    send_sem=send_sems.at[3 * t + j], recv_sem=recv_sems.at[3 * t + j],
                    device_id=(x, y, 1 - c), device_id_type=MESH)
                cp.start()
                cps.append(cp)
        for t in range(n):
            for j, (px, py) in enumerate(_other_chips(x, y)):
                cps[3 * t + j].wait_send()
                other = _gather_piece(outs[t], dims[t], 2 * px + py, 1 - c)
                pltpu.make_async_remote_copy(
                    src_ref=other, dst_ref=other, send_sem=send_sems.at[3 * t + j], recv_sem=recv_sems.at[3 * t + j],
                    device_id=(x, y, 1 - c), device_id_type=MESH).wait_recv()

    return pl.pallas_call(
        body, name="gather_forward",
        in_specs=[ANY] * n, out_specs=[ANY] * n,
        out_shape=[jax.ShapeDtypeStruct(a.shape, a.dtype) for a in fulls],
        input_output_aliases={t: t for t in range(n)},
        scratch_shapes=[pltpu.SemaphoreType.DMA((3 * n,)), pltpu.SemaphoreType.DMA((3 * n,))],
        compiler_params=pltpu.CompilerParams(has_side_effects=True),
    )(*fulls)


def _tie(a, token):
    return a if token is None else a + token[0, 0]


class _NoGather:
    token = None

    def __init__(self, w_in, meta):
        self.out = (w_in, meta)

    def finish(self, after):
        return self.out


class _FirstGather:
    dims = [SHARDED[0][1:], (N_META, D, 1)]

    def __init__(self, w_in_shard, meta_shard):
        self.copies, self.arrivals = _gather_half_copies(self.dims)
        lands = [lax.empty((D, IN_COLS), BF), lax.empty((N_META, D), F32)]
        self.send, self.recv, self.srcs, self.lands, self.token = _split_copy_start(
            "gather_start_first", self.copies, 8, [w_in_shard, meta_shard], lands, w_in_shard)

    def finish(self, after):
        _, lands = _split_copy_wait("gather_wait_first", self.copies, self.arrivals, self.send, self.recv, self.srcs,
                                    self.lands, after)
        return gather_forward(lands, self.dims)


class _NoExchange:
    def __init__(self, w):
        self.w = w
        self.grads = {}

    def begin(self, after):
        return None

    def rest_weights(self, after):
        return self.w

    def start(self, group, idxs, grads):
        self.grads.update(zip(idxs, [g for g, _ in grads]))
        return None

    def finish(self, group, after):
        pass


class _Exchange:
    def __init__(self, rest_shards, place):
        self.rest_shards = rest_shards
        self.place = place
        self.rest_dims = [s[1:] for s in SHARDED[1:]]
        self.pending = {}
        self.halves = {}
        self.full = {}
        self.parts = {}

    def begin(self, after):
        copies, arrivals = _gather_copies(self.rest_dims)
        lands = [lax.empty((r, c), BF) for r, c, _ in self.rest_dims]
        send, recv, srcs, lands, token = _split_copy_start(
            "gather_start", copies, 7 * len(lands), self.rest_shards, lands, after)
        self.pending["gather"] = (copies, arrivals, send, recv, srcs, lands)
        return token

    def rest_weights(self, after):
        copies, arrivals, send, recv, srcs, lands = self.pending.pop("gather")
        _, fulls = _split_copy_wait("gather_wait", copies, arrivals, send, recv, srcs, lands, after)
        return dict(zip([s[0] for s in SHARDED[1:]], fulls))

    def _halves(self, group, idxs, grads):
        got = pair_exchange("pair_exchange_" + group, grads, idxs)
        return [pair_sum("pair_sum_" + SHARDED[i][0], self.place, g, b, SHARDED[i][3])
                for i, g, b in zip(idxs, grads, got)]

    def start(self, group, idxs, grads):
        if grads[0][1] is None:
            halves = self._halves(group, idxs, [g for g, _ in grads])
            for i, h, pt in zip(idxs, halves, scatter_grads(halves, idxs)):
                self.halves[i], self.parts[i] = h, pt
            return None
        copies, arrivals = _scatter8_copies(idxs)
        lands = [lax.empty((7,) + sh, BF) for sh in _piece_shapes(idxs)]
        send, recv, srcs, lands, token = _split_copy_start(
            "scatter_start_" + group, copies, 7 * len(idxs), [g16 for _, g16 in grads], lands, grads[0][0])
        self.pending[group] = (copies, arrivals, send, recv, srcs, lands, idxs)
        for i, (g32, _) in zip(idxs, grads):
            self.full[i] = g32
        return token

    def finish(self, group, after):
        if group not in self.pending:
            return
        copies, arrivals, send, recv, srcs, lands, idxs = self.pending.pop(group)
        _, lands = _split_copy_wait("scatter_wait_" + group, copies, arrivals, send, recv, srcs, lands, after)
        for i, pt in zip(idxs, lands):
            self.parts[i] = pt


def local_step(x, tgt, p, first, ex):
    qn_t = jnp.tile(p["q_norm"].reshape(1, HEAD), (1, NQ)) * HEAD ** -0.5
    kn_t = jnp.tile(p["k_norm"].reshape(1, HEAD), (1, NKV))
    seg = jnp.arange(256) // HEAD
    bd = (seg[:, None] == seg[None, :]).astype(BF)
    nm = p["norm_mix"].reshape(1, D)
    sinks = p["attn_sinks"].reshape(NQ)
    dsk = p["ssm_d"].reshape(1, D)
    abn, sbn, gf = (p[k].reshape(1, D) for k in ("attn_branch_norm", "ssm_branch_norm", "norm_ffn"))

    disc_in = (p["lam_re"], p["lam_im"], _tie(p["log_dt"], first.token), p["ssm_b_re"], p["ssm_b_im"])
    (ar, ai, bbr, bbi), disc_vjp = jax.vjp(_ssm_discretize, *disc_in)
    pw = _scan_powers(ar, ai)
    brp, bip = _pack_b(bbr.astype(BF)), _pack_b(bbi.astype(BF))
    crp = _pack_c(_tie(p["ssm_c_re"], first.token).astype(BF))
    cip = _pack_c(_tie(p["ssm_c_im"], first.token).astype(BF))

    w_in, meta_full = first.finish(brp)
    mpad = jnp.concatenate([jnp.zeros((META0, D), F32), meta_full], axis=0)
    token = ex.begin(w_in)
    xn, qr, kr, qn, kn, v, u, gg = in_proj_fwd(x, mpad, w_in, _tie(nm, token), qn_t, kn_t, bd)
    attn = attention_fwd(sinks, qn, kn, v)
    perm = _segment_order()
    y, sr, si = ssm_fwd(u, pw, brp, bip, crp, cip, dsk, perm)
    w = ex.rest_weights(y)
    zb, zzb, mgb, h1 = mid_fwd(y, attn, gg, x, mpad, w["w_glu"], w["w_out"], abn, sbn)
    hnb, dgub, actb, dh2b, dh1, dh1b, dgf, loss = ffn_fwd_bwd(h1, tgt, w["w_ffn_in"], w["w_ffn_out"], gf)
    token = ex.start("ffn", (3, 4), [weight_grad("grad_w_ffn_in", hnb, dgub, 1408, True),
                                     weight_grad("grad_w_ffn_out", actb, dh2b, 512, True)])
    dy, dattn, dggb, dzzb, dabn, dsbn = mid_bwd(dh1b, y, attn, gg, zzb, w["w_glu"], w["w_out"], _tie(abn, token),
                                                sbn)
    grads_mid = [weight_grad("grad_w_glu", zb, dzzb, 1024, True), weight_grad("grad_w_out", mgb, dh1b, 1024, True)]
    dub, da, dbr, dbi, dcr, dci, dd = ssm_bwd(dy, u, sr, si, pw, brp, bip, crp, cip, dsk, perm)
    ex.finish("ffn", dub)
    token = ex.start("mid", (1, 2), grads_mid)
    dqn, dkn, dv, dsink = attention_bwd(_tie(sinks, token), qn, kn, v, dattn)
    dproj, dx, dmpad, dnm, dqw, dkw = in_proj_bwd(dqn, dkn, dv, dub, dggb, qr, kr, x, mpad, dh1, w_in, nm, qn_t,
                                                  kn_t, bd)
    ex.finish("mid", dproj)
    ex.start("in", (0,), [(weight_grad("grad_w_in", xn, dproj, 1536), None)])

    dar = jnp.sum(da[0:8], axis=0).reshape(GROUPS, NSTATE)
    dai = jnp.sum(da[8:16], axis=0).reshape(GROUPS, NSTATE)
    dlr, dli, dldt, dbre, dbim = disc_vjp((dar, dai, _unpack_b(dbr), _unpack_b(dbi)))
    small = {
        "norm_mix": dnm, "q_norm": dqw.reshape(NQ, HEAD).sum(0) * HEAD ** -0.5, "k_norm": dkw.reshape(NKV, HEAD).sum(0),
        "attn_sinks": dsink[0, 0:NQ], "lam_re": dlr, "lam_im": dli, "log_dt": dldt,
        "ssm_b_re": dbre, "ssm_b_im": dbim, "ssm_c_re": _unpack_c(jnp.swapaxes(dcr, 1, 2)), "ssm_c_im": _unpack_c(jnp.swapaxes(dci, 1, 2)),
        "ssm_d": dd, "attn_branch_norm": dabn, "ssm_branch_norm": dsbn, "norm_ffn": dgf,
    }
    return loss[0, 0], dx, dmpad[META0:PAD], small


def kernel(x, meta_tokens, norm_mix, w_in, q_norm, k_norm, attn_sinks, lam_re, lam_im, log_dt, ssm_b_re, ssm_b_im, ssm_c_re, ssm_c_im, ssm_d, w_glu, attn_branch_norm, ssm_branch_norm, w_out, norm_ffn, w_ffn_in, w_ffn_out, loss_target, m_meta_tokens, m_norm_mix, m_w_in, m_q_norm, m_k_norm, m_attn_sinks, m_lam_re, m_lam_im, m_log_dt, m_ssm_b_re, m_ssm_b_im, m_ssm_c_re, m_ssm_c_im, m_ssm_d, m_w_glu, m_attn_branch_norm, m_ssm_branch_norm, m_w_out, m_norm_ffn, m_w_ffn_in, m_w_ffn_out, v_meta_tokens, v_norm_mix, v_w_in, v_q_norm, v_k_norm, v_attn_sinks, v_lam_re, v_lam_im, v_log_dt, v_ssm_b_re, v_ssm_b_im, v_ssm_c_re, v_ssm_c_im, v_ssm_d, v_w_glu, v_attn_branch_norm, v_ssm_branch_norm, v_w_out, v_norm_ffn, v_w_ffn_in, v_w_ffn_out):
    args = dict(locals())
    names = ["meta_tokens", "norm_mix", "w_in", "q_norm", "k_norm", "attn_sinks", "lam_re", "lam_im", "log_dt",
             "ssm_b_re", "ssm_b_im", "ssm_c_re", "ssm_c_im", "ssm_d", "w_glu", "attn_branch_norm",
             "ssm_branch_norm", "w_out", "norm_ffn", "w_ffn_in", "w_ffn_out"]
    big_names = [s[0] for s in SHARDED]
    xs, ys, cs = _place()
    chip = 2 * xs + ys

    first = _FirstGather(args["w_in"][0].astype(BF), meta_tokens)
    shards = [None] + [_tie(args[n][0], first.token).astype(BF) for n in big_names[1:]]
    place = jnp.stack([chip, cs]).astype(jnp.int32)
    ex = _Exchange(shards[1:], place)

    swapped = ("ssm_b_re", "ssm_b_im")

    def natural(n, a):
        return jnp.swapaxes(a, -1, -2) if n in swapped else a

    p = {n: natural(n, args[n][0]) for n in SMALL}
    loss, dx, dmeta, gsmall = local_step(x[0], loss_target[0], p, first, ex)

    out = {}

    def finish(idxs, name):
        mine = [sum8("sum8_" + SHARDED[i][0], place, ex.parts[i], ex.full[i], SHARDED[i][3]) if i in ex.full
                else sum4("sum4_" + SHARDED[i][0], place, ex.parts[i], ex.halves[i], SHARDED[i][3]) for i in idxs]
        for i, g in zip(idxs, join_halves(name, mine, idxs)):
            n = SHARDED[i][0]
            out[n] = tuple(t[None] for t in adamw("adamw_" + n, args[n][0], g, args["m_" + n][0], args["v_" + n][0]))

    finish((0, 1, 2, 3, 4), "join_halves")
    small_list = [gsmall[n].reshape(natural(n, args[n]).shape) for n in SMALL] + [dmeta, loss.reshape(1, 1)]
    red = allreduce_small(_pack_small(small_list))
    *small_red, dmeta_sum, loss = _unpack_small(red, small_list)
    loss = loss[0, 0]
    gmeta = lax.dynamic_slice_in_dim(dmeta_sum, chip * (D // 4), D // 4, axis=1)
    out["meta_tokens"] = tuple(adamw("adamw_meta", meta_tokens, gmeta, m_meta_tokens, v_meta_tokens))
    ds, nms, nvs = adamw_small([natural(n, args[n]) for n in SMALL], small_red,
                               [natural(n, args["m_" + n]) for n in SMALL],
                               [natural(n, args["v_" + n]) for n in SMALL])
    for n, g_, d_, m_, v_ in zip(SMALL, small_red, ds, nms, nvs):
        out[n] = tuple(natural(n, t) for t in (g_, d_, m_, v_))

    res = [loss, dx[None]]
    for k in range(4):
        res += [out[n][k] for n in names]
    return tuple(res)
```

```python
import functools
import math

import jax
import jax.numpy as jnp
from jax import lax
from jax.experimental import pallas as pl
from jax.experimental.pallas import tpu as pltpu

F32 = jnp.float32
BF = jnp.bfloat16

D = 1024
N_META = 16
HEAD = 64
NQ = 16
NKV = 4
QW = NQ * HEAD
KVW = NKV * HEAD
WIN = 128
GROUPS = 64
GCH = 16
NSTATE = 64
SW = GROUPS * NSTATE
DFF = 2816
IN_COLS = QW + 2 * KVW + D + 2 * D
PAD = 256
META0 = PAD - N_META
EPS = 1e-6
NEG = -1e30

TM = 256
TS = 256
LC = 512
NLC = SW // LC

LR, B1, B2, AEPS, WD, STEP = 0.001, 0.9, 0.999, 1e-08, 0.01, 10
BC1 = 1.0 - B1 ** STEP
BC2 = 1.0 - B2 ** STEP

MESH = pl.DeviceIdType.MESH
ANY = pl.BlockSpec(memory_space=pl.ANY)


def _cp(sem=None, vmem_mb=48):
    return pltpu.CompilerParams(dimension_semantics=sem, vmem_limit_bytes=vmem_mb << 20)


def _full(shape):
    n = len(shape)
    return pl.BlockSpec(shape, lambda *a: (0,) * n)


def _const(shape):
    n = len(shape)
    return pl.BlockSpec(shape, lambda *a: (0,) * n, pipeline_mode=pl.Buffered(1))


def _rows(tm, width):
    return pl.BlockSpec((tm, width), lambda i: (i, 0))


def _dot(a, b):
    return jnp.dot(a, b, preferred_element_type=F32)


def _dot_nt(a, b):
    return lax.dot_general(a, b, (((1,), (1,)), ((), ())), preferred_element_type=F32)


def _dot_tn(a, b):
    return lax.dot_general(a, b, (((0,), (0,)), ((), ())), preferred_element_type=F32)


def _sigmoid(x):
    return 0.5 * jnp.tanh(0.5 * x) + 0.5


def _seg_mean(x, bd, split=True):
    outs = []
    for j in range(x.shape[1] // 256):
        c = x[:, 256 * j:256 * (j + 1)]
        hi = c.astype(BF)
        part = _dot(hi, bd)
        if split:
            part += _dot((c - hi.astype(F32)).astype(BF), bd)
        outs.append(part)
    out = outs[0] if len(outs) == 1 else jnp.concatenate(outs, axis=1)
    return out * (1.0 / HEAD)


_GC = math.sqrt(2.0 / math.pi)


def _gelu(x):
    t = jnp.tanh(_GC * (x + 0.044715 * x * x * x))
    return 0.5 * x * (1.0 + t), t


def _gelu_grad(x, t):
    return 0.5 * (1.0 + t) + 0.5 * x * (1.0 - t * t) * _GC * (1.0 + 3.0 * 0.044715 * x * x)


def _seq_specs():
    assert TM == PAD
    return [pl.BlockSpec((TM, D), lambda i: (jnp.maximum(i - 1, 0), 0)), _full((PAD, D))]


def _seq_tile(x_ref, m_ref):
    return jnp.where(pl.program_id(0) == 0, m_ref[...], x_ref[...])


def in_proj_fwd(x, mpad, w_in, nm, qn_t, kn_t, bd):
    lp = x.shape[0] + PAD

    def body(x_ref, m_ref, w_ref, nm_ref, qn_ref, kn_ref, bd_ref,
             xn_o, qr_o, kr_o, qn_o, kn_o, v_o, u_o, gg_o):
        h = _seq_tile(x_ref, m_ref)
        r = lax.rsqrt(jnp.mean(h * h, axis=-1, keepdims=True) + EPS)
        xb = ((h * r) * nm_ref[...]).astype(BF)
        xn_o[...] = xb
        bdv = bd_ref[...]
        q = _dot(xb, w_ref[:, 0:QW])
        qr_o[...] = q
        qn_o[...] = (q * lax.rsqrt(_seg_mean(q * q, bdv, False) + EPS) * qn_ref[...]).astype(BF)
        k = _dot(xb, w_ref[:, QW:QW + KVW])
        kr_o[...] = k
        kn_o[...] = (k * lax.rsqrt(_seg_mean(k * k, bdv, False) + EPS) * kn_ref[...]).astype(BF)
        v_o[...] = _dot(xb, w_ref[:, QW + KVW:QW + 2 * KVW]).astype(BF)
        u_o[...] = _dot(xb, w_ref[:, QW + 2 * KVW:QW + 2 * KVW + D])
        gg_o[...] = _dot(xb, w_ref[:, QW + 2 * KVW + D:IN_COLS])

    outs = [(D, BF), (QW, F32), (KVW, F32), (QW, BF), (KVW, BF), (KVW, BF), (D, F32), (2 * D, F32)]
    return pl.pallas_call(
        body, name="in_proj_fwd", grid=(lp // TM,),
        in_specs=_seq_specs() + [_full((D, IN_COLS)), _full((1, D)), _full((1, QW)), _full((1, KVW)),
                                 _full((256, 256))],
        out_specs=[_rows(TM, w) for w, _ in outs],
        out_shape=[jax.ShapeDtypeStruct((lp, w), t) for w, t in outs],
        compiler_params=_cp(("arbitrary",)),
    )(x, mpad, w_in, nm, qn_t, kn_t, bd)


def in_proj_bwd(dqn, dkn, dv, du, dgg, qr, kr, x, mpad, dh1, w_in, nm, qn_t, kn_t, bd):
    lp = x.shape[0] + PAD

    def body(dqn_ref, dkn_ref, dv_ref, du_ref, dgg_ref, qr_ref, kr_ref, x_ref, m_ref, dh1_ref,
             w_ref, nm_ref, qn_ref, kn_ref, bd_ref,
             dproj_o, dx_o, dm_o, dnm_o, dqw_o, dkw_o):
        i = pl.program_id(0)

        @pl.when(i == 0)
        def _():
            dnm_o[...] = jnp.zeros_like(dnm_o)
            dqw_o[...] = jnp.zeros_like(dqw_o)
            dkw_o[...] = jnp.zeros_like(dkw_o)

        bdv = bd_ref[...]

        def norm_bwd(x, dy, g, acc_ref):
            rr = lax.rsqrt(_seg_mean(x * x, bdv, False) + EPS)
            xh = x * rr
            acc_ref[...] += jnp.sum(dy * xh, axis=0, keepdims=True)
            dxh = dy * g
            return rr * (dxh - xh * _seg_mean(dxh * xh, bdv))

        dq = norm_bwd(qr_ref[...], dqn_ref[...], qn_ref[...], dqw_o)
        dk = norm_bwd(kr_ref[...], dkn_ref[...], kn_ref[...], dkw_o)
        dproj_o[:, 0:QW] = dq.astype(BF)
        dproj_o[:, QW:QW + KVW] = dk.astype(BF)
        dproj_o[:, QW + KVW:QW + 2 * KVW] = dv_ref[...].astype(BF)
        dproj_o[:, QW + 2 * KVW:QW + 2 * KVW + D] = du_ref[...]
        dproj_o[:, QW + 2 * KVW + D:IN_COLS] = dgg_ref[...]
        dxn = _dot_nt(dproj_o[...], w_ref[...])
        h = _seq_tile(x_ref, m_ref)
        r = lax.rsqrt(jnp.mean(h * h, axis=-1, keepdims=True) + EPS)
        xh = h * r
        dnm_o[...] += jnp.sum(dxn * xh, axis=0, keepdims=True)
        dxh = dxn * nm_ref[...]
        dh0 = dh1_ref[...] + r * (dxh - xh * jnp.mean(dxh * xh, axis=-1, keepdims=True))
        dx_o[...] = dh0

        @pl.when(i == 0)
        def _():
            dm_o[...] = dh0

    return pl.pallas_call(
        body, name="in_proj_bwd", grid=(lp // TM,),
        in_specs=[_rows(TM, QW), _rows(TM, KVW), _rows(TM, KVW), _rows(TM, D), _rows(TM, 2 * D),
                  _rows(TM, QW), _rows(TM, KVW)] + _seq_specs() + [_rows(TM, D),
                  _full((D, IN_COLS)), _full((1, D)), _full((1, QW)), _full((1, KVW)), _full((256, 256))],
        out_specs=[_rows(TM, IN_COLS), _seq_specs()[0], _full((PAD, D)), _full((1, D)), _full((1, QW)),
                   _full((1, KVW))],
        out_shape=[jax.ShapeDtypeStruct((lp, IN_COLS), BF), jax.ShapeDtypeStruct((lp - PAD, D), F32),
                   jax.ShapeDtypeStruct((PAD, D), F32),
                   jax.ShapeDtypeStruct((1, D), F32), jax.ShapeDtypeStruct((1, QW), F32),
                   jax.ShapeDtypeStruct((1, KVW), F32)],
        compiler_params=_cp(("arbitrary",)),
    )(dqn, dkn, dv, du, dgg, qr, kr, x, mpad, dh1, w_in, nm, qn_t, kn_t, bd)


def _att_bias(b):
    qi = lax.broadcasted_iota(jnp.int32, (WIN, WIN), 0)
    kj = lax.broadcasted_iota(jnp.int32, (WIN, WIN), 1)
    upper = kj > qi
    valid = (upper & ((b - 1) * WIN + kj >= PAD)) | (jnp.logical_not(upper) & (b * WIN + kj >= META0))
    bias_pc = jnp.where(valid, 0.0, NEG)
    kj1 = lax.broadcasted_iota(jnp.int32, (1, WIN), 1)
    bias_m = jnp.where((kj1 >= WIN - N_META) & (b * WIN >= PAD), 0.0, NEG)
    return bias_m, bias_pc, upper


def _dup_half(ref, kp, pos, lo):
    pair = ref[:, 128 * kp:128 * kp + 128].astype(F32)
    rolled = pltpu.roll(pair, 64, axis=1)
    keep = lo if pos == 0 else jnp.logical_not(lo)
    return jnp.where(keep, pair, rolled).astype(BF)


def _stack_heads(ref, kv, lo):
    parts = []
    for pp in range(2):
        pair = ref[:, 256 * kv + 128 * pp:256 * kv + 128 * pp + 128]
        zero = jnp.zeros_like(pair)
        parts.append(jnp.where(lo, pair, zero))
        parts.append(jnp.where(lo, zero, pair))
    return jnp.concatenate(parts, axis=0)


def _unstack_heads(x4, lo):
    a = jnp.where(lo, x4[0:WIN], x4[WIN:2 * WIN])
    b = jnp.where(lo, x4[2 * WIN:3 * WIN], x4[3 * WIN:4 * WIN])
    return a, b


def _split_pc(x, upper):
    zero = jnp.zeros_like(x)
    return jnp.where(upper, x, zero).astype(BF), jnp.where(upper, zero, x).astype(BF)


def _head_softmax(s_m, s_p, s_c, bias, sink):
    bias_m, bias_pc, upper = bias
    sm = s_m + bias_m
    spc = jnp.where(upper, s_p, s_c) + bias_pc
    m = jnp.maximum(jnp.max(jnp.maximum(sm, spc), axis=-1, keepdims=True), sink)
    em, epc = jnp.exp(sm - m), jnp.exp(spc - m)
    esink = jnp.exp(sink - m)
    inv = 1.0 / (esink + jnp.sum(em + epc, axis=-1, keepdims=True))
    return em, epc, inv, esink


def _kv_specs():
    return [pl.BlockSpec((WIN, KVW), lambda b: (1, 0)),
            pl.BlockSpec((WIN, KVW), lambda b: (jnp.maximum(b - 1, 0), 0)),
            pl.BlockSpec((WIN, KVW), lambda b: (b, 0))]


def attention_fwd(sinks, qn, kn, v):
    lp = qn.shape[0]

    def body(sink_ref, q_ref, km_ref, kp_ref, kc_ref, vm_ref, vp_ref, vc_ref, o_ref):
        b = pl.program_id(0)
        bias = _att_bias(b)
        lo = lax.broadcasted_iota(jnp.int32, (WIN, WIN), 1) < HEAD
        for kv in range(NKV):
            kp, pos = kv // 2, kv % 2
            kds = [_dup_half(r, kp, pos, lo) for r in (km_ref, kp_ref, kc_ref)]
            vds = [_dup_half(r, kp, pos, lo) for r in (vm_ref, vp_ref, vc_ref)]
            q4 = _stack_heads(q_ref, kv, lo)
            ss = [_dot_nt(q4, kd) for kd in kds]
            es, invs = ([], [], []), []
            for h in range(4):
                rs = slice(WIN * h, WIN * h + WIN)
                em, epc, inv, _ = _head_softmax(ss[0][rs], ss[1][rs], ss[2][rs], bias, sink_ref[4 * kv + h])
                e_p, e_c = _split_pc(epc, bias[2])
                for lst, e in zip(es, (em.astype(BF), e_p, e_c)):
                    lst.append(e)
                invs.append(inv)
            e_m, e_p, e_c = (jnp.concatenate(lst, axis=0) for lst in es)
            o4 = (_dot(e_m, vds[0]) + _dot(e_p, vds[1]) + _dot(e_c, vds[2])) * jnp.concatenate(invs, axis=0)
            oa, ob = _unstack_heads(o4, lo)
            o_ref[:, 256 * kv:256 * kv + 128] = oa
            o_ref[:, 256 * kv + 128:256 * kv + 256] = ob

    return pl.pallas_call(
        body, name="attention_fwd", grid=(lp // WIN,),
        in_specs=[pl.BlockSpec(memory_space=pltpu.SMEM), _rows(WIN, QW)] + _kv_specs() + _kv_specs(),
        out_specs=_rows(WIN, QW),
        out_shape=jax.ShapeDtypeStruct((lp, QW), F32),
        compiler_params=_cp(("arbitrary",)),
    )(sinks, qn, kn, kn, kn, v, v, v)


def attention_bwd(sinks, qn, kn, v, do):
    lp = qn.shape[0]
    nb = lp // WIN

    def body(sink_ref, q_ref, km_ref, kp_ref, kc_ref, vm_ref, vp_ref, vc_ref, do_ref,
             dq_ref, dk_hbm, dv_hbm, dsink_ref, dk_acc, dv_acc):
        b = pl.program_id(0)

        @pl.when(b == 0)
        def _():
            dk_acc[...] = jnp.zeros_like(dk_acc)
            dv_acc[...] = jnp.zeros_like(dv_acc)
            dsink_ref[...] = jnp.zeros_like(dsink_ref)

        bias = _att_bias(b)
        upper = bias[2]
        lo = lax.broadcasted_iota(jnp.int32, (WIN, WIN), 1) < HEAD
        lane8 = lax.broadcasted_iota(jnp.int32, (8, 128), 1)
        starts = [WIN, pl.multiple_of(jnp.maximum(b - 1, 0) * WIN, WIN), pl.multiple_of(b * WIN, WIN)]
        for kv in range(NKV):
            kp, pos = kv // 2, kv % 2
            keep = lo if pos == 0 else jnp.logical_not(lo)
            kds = [_dup_half(r, kp, pos, lo) for r in (km_ref, kp_ref, kc_ref)]
            vds = [_dup_half(r, kp, pos, lo) for r in (vm_ref, vp_ref, vc_ref)]
            q4 = _stack_heads(q_ref, kv, lo)
            do4 = _stack_heads(do_ref, kv, lo)
            ss = [_dot_nt(q4, kd) for kd in kds]
            dps = [_dot_nt(do4, vd) for vd in vds]
            ds_l, p_l = ([], [], []), ([], [], [])
            for h in range(4):
                rs = slice(WIN * h, WIN * h + WIN)
                em, epc, inv, esink = _head_softmax(ss[0][rs], ss[1][rs], ss[2][rs], bias, sink_ref[4 * kv + h])
                p_m, p_pc = em * inv, epc * inv
                dp_m = dps[0][rs]
                dp_pc = jnp.where(upper, dps[1][rs], dps[2][rs])
                delta = jnp.sum(p_m * dp_m + p_pc * dp_pc, axis=-1, keepdims=True)
                val = -jnp.sum(esink * inv * delta, axis=0, keepdims=True)
                dsink_ref[...] += jnp.where(lane8 == 4 * kv + h, val, 0.0)
                for lst, t in zip(ds_l, ((p_m * (dp_m - delta)).astype(BF),) + _split_pc(p_pc * (dp_pc - delta), upper)):
                    lst.append(t)
                for lst, t in zip(p_l, (p_m.astype(BF),) + _split_pc(p_pc, upper)):
                    lst.append(t)
            dss = [jnp.concatenate(lst, axis=0) for lst in ds_l]
            ps = [jnp.concatenate(lst, axis=0) for lst in p_l]
            dq4 = _dot(dss[0], kds[0]) + _dot(dss[1], kds[1]) + _dot(dss[2], kds[2])
            dqa, dqb = _unstack_heads(dq4, lo)
            dq_ref[:, 256 * kv:256 * kv + 128] = dqa
            dq_ref[:, 256 * kv + 128:256 * kv + 256] = dqb
            for x in range(3):
                dkd = _dot_tn(dss[x], q4)
                dvd = _dot_tn(ps[x], do4)
                dkd = jnp.where(keep, dkd + pltpu.roll(dkd, 64, axis=1), 0.0)
                dvd = jnp.where(keep, dvd + pltpu.roll(dvd, 64, axis=1), 0.0)
                dk_acc[pl.ds(starts[x], WIN), 128 * kp:128 * kp + 128] += dkd
                dv_acc[pl.ds(starts[x], WIN), 128 * kp:128 * kp + 128] += dvd

        @pl.when(b == nb - 1)
        def _():
            pltpu.sync_copy(dk_acc, dk_hbm)
            pltpu.sync_copy(dv_acc, dv_hbm)

    return pl.pallas_call(
        body, name="attention_bwd", grid=(nb,),
        in_specs=[pl.BlockSpec(memory_space=pltpu.SMEM), _rows(WIN, QW)] + _kv_specs() + _kv_specs()
                 + [_rows(WIN, QW)],
        out_specs=[_rows(WIN, QW), ANY, ANY, _full((8, 128))],
        out_shape=[jax.ShapeDtypeStruct((lp, QW), F32), jax.ShapeDtypeStruct((lp, KVW), F32),
                   jax.ShapeDtypeStruct((lp, KVW), F32), jax.ShapeDtypeStruct((8, 128), F32)],
        scratch_shapes=[pltpu.VMEM((lp, KVW), F32), pltpu.VMEM((lp, KVW), F32)],
        compiler_params=_cp(("arbitrary",)),
    )(sinks, qn, kn, kn, kn, v, v, v, do)


PW_ROWS = 64
SEG = TS // 8


def _segment_order():
    rho = jnp.arange(TS)
    token = SEG * (rho % 8) + rho // 8
    p = (token[:, None] == jnp.arange(TS)[None, :]).astype(BF)
    return p, p.T


def _reorder(p, x):
    hi = x.astype(BF)
    lo = (x - hi.astype(F32)).astype(BF)
    return _dot(p, hi) + _dot(p, lo)


def _segment_scan(xr_ref, xi_ref, pw_ref, base, ls, c_r, c_i, reverse, visit=None):
    ar, ai = pw_ref[base:base + 8, ls], pw_ref[base + 8:base + 16, ls]
    order = list(range(SEG - 1, -1, -1)) if reverse else list(range(SEG))
    s_r = s_i = jnp.zeros_like(ar)
    for i in order:
        rows = pl.ds(8 * i, 8)
        s_r, s_i = ar * s_r - ai * s_i + xr_ref[rows, ls], ar * s_i + ai * s_r + xi_ref[rows, ls]
        xr_ref[rows, ls] = s_r
        xi_ref[rows, ls] = s_i
    rid = lax.broadcasted_iota(jnp.int32, s_r.shape, 0)
    if reverse:
        e_r = jnp.where(rid < 7, pltpu.roll(s_r, 7, axis=0), c_r)
        e_i = jnp.where(rid < 7, pltpu.roll(s_i, 7, axis=0), c_i)
    else:
        e_r = jnp.where(rid >= 1, pltpu.roll(s_r, 1, axis=0), c_r)
        e_i = jnp.where(rid >= 1, pltpu.roll(s_i, 1, axis=0), c_i)
    for n, k in enumerate((1, 2, 4)):
        shift = 8 - k if reverse else k
        t_r, t_i = pltpu.roll(e_r, shift, axis=0), pltpu.roll(e_i, shift, axis=0)
        kr = pw_ref[base + 16 + 16 * n:base + 24 + 16 * n, ls]
        ki = pw_ref[base + 24 + 16 * n:base + 32 + 16 * n, ls]
        e_r, e_i = e_r + kr * t_r - ki * t_i, e_i + kr * t_i + ki * t_r
    last = 0 if reverse else 7
    a16r, a16i = pw_ref[base + 16 + last:base + 17 + last, ls], pw_ref[base + 24 + last:base + 25 + last, ls]
    lr, li, fr, fi = (t[last:last + 1] for t in (e_r, e_i, s_r, s_i))
    out = (a16r * lr - a16i * li + fr, a16r * li + a16i * lr + fi)
    d_r, d_i = e_r, e_i
    extra = None
    for i in order:
        rows = pl.ds(8 * i, 8)
        d_r, d_i = ar * d_r - ai * d_i, ar * d_i + ai * d_r
        f_r, f_i = xr_ref[rows, ls] + d_r, xi_ref[rows, ls] + d_i
        xr_ref[rows, ls] = f_r
        xi_ref[rows, ls] = f_i
        if visit is not None:
            extra = visit(i, f_r, f_i, extra)
    return out if visit is None else (out, extra)


def ssm_fwd(u, pw, br, bi, cr, ci, dsk, perm):
    lp = u.shape[0]

    def body(u_ref, pw_ref, br_ref, bi_ref, cr_ref, ci_ref, d_ref, p_ref, pt_ref,
             y_ref, sr_ref, si_ref, car_r, car_i, yp_s):
        i = pl.program_id(0)

        @pl.when(i == 0)
        def _():
            car_r[...] = jnp.zeros_like(car_r)
            car_i[...] = jnp.zeros_like(car_i)

        u = u_ref[...]
        ub = _dot(p_ref[...], u.astype(BF)).astype(BF)
        for j in range(NLC):
            uj = ub[:, 128 * j:128 * j + 128]
            sr_ref[:, LC * j:LC * j + LC] = _dot(uj, br_ref[j])
            si_ref[:, LC * j:LC * j + LC] = _dot(uj, bi_ref[j])
        for j in range(NLC):
            ls = slice(LC * j, LC * j + LC)
            car_r[:, ls], car_i[:, ls] = _segment_scan(sr_ref, si_ref, pw_ref, 0, ls, car_r[:, ls], car_i[:, ls],
                                                       False)
        for j in range(NLC):
            ls = slice(LC * j, LC * j + LC)
            yp_s[:, 128 * j:128 * j + 128] = (_dot(sr_ref[:, ls].astype(BF), cr_ref[j])
                                              - _dot(si_ref[:, ls].astype(BF), ci_ref[j]))
        y_ref[...] = _reorder(pt_ref[...], yp_s[...]) + d_ref[...] * u

    p, pt = perm
    return pl.pallas_call(
        body, name="ssm_fwd", grid=(lp // TS,),
        in_specs=[_rows(TS, D), _full((2 * PW_ROWS, SW)), _full((NLC, 128, LC)), _full((NLC, 128, LC)),
                  _full((NLC, LC, 128)), _full((NLC, LC, 128)), _full((1, D)), _full((TS, TS)), _full((TS, TS))],
        out_specs=[_rows(TS, D), _rows(TS, SW), _rows(TS, SW)],
        out_shape=[jax.ShapeDtypeStruct((lp, D), F32), jax.ShapeDtypeStruct((lp, SW), F32),
                   jax.ShapeDtypeStruct((lp, SW), F32)],
        scratch_shapes=[pltpu.VMEM((1, SW), F32), pltpu.VMEM((1, SW), F32), pltpu.VMEM((TS, D), F32)],
        compiler_params=_cp(("arbitrary",)),
    )(u, pw, br, bi, cr, ci, dsk, p, pt)


def ssm_bwd(dy, u, sr, si, pw, br, bi, cr, ci, dsk, perm):
    lp = u.shape[0]
    nt = lp // TS

    def rev(i):
        return (nt - 1 - i, 0)

    def prev8(i):
        return (jnp.maximum((nt - 1 - i) * (TS // 8) - 1, 0), 0)

    def body(dy_ref, u_ref, sr_ref, si_ref, pr8_ref, pi8_ref, pw_ref, br_ref, bi_ref, cr_ref, ci_ref, d_ref,
             p_ref, pt_ref, du_ref, da_ref, dbr_ref, dbi_ref, dcr_ref, dci_ref, dd_ref,
             gr_s, gi_s, car_r, car_i, dup_s):
        i = pl.program_id(0)

        @pl.when(i == 0)
        def _():
            car_r[...] = jnp.zeros_like(car_r)
            car_i[...] = jnp.zeros_like(car_i)
            for ref in (da_ref, dbr_ref, dbi_ref, dcr_ref, dci_ref, dd_ref):
                ref[...] = jnp.zeros_like(ref)

        keep = 1.0 - (i == nt - 1).astype(F32)
        dy = dy_ref[...]
        u = u_ref[...]
        dd_ref[...] += jnp.sum(dy * u, axis=0, keepdims=True)
        pm = p_ref[...]
        dyb = _dot(pm, dy.astype(BF)).astype(BF)
        ub = _dot(pm, u.astype(BF)).astype(BF)
        for j in range(NLC):
            ls = slice(LC * j, LC * j + LC)
            dyj = dyb[:, 128 * j:128 * j + 128]
            gr_s[:, ls] = _dot_nt(dyj, cr_ref[j])
            gi_s[:, ls] = -_dot_nt(dyj, ci_ref[j])
            dcr_ref[j] += _dot_tn(dyj, sr_ref[:, ls].astype(BF))
            dci_ref[j] -= _dot_tn(dyj, si_ref[:, ls].astype(BF))
        rid = lax.broadcasted_iota(jnp.int32, (8, LC), 0)
        for j in range(NLC):
            ls = slice(LC * j, LC * j + LC)

            def accumulate(i, g_r, g_i, acc):
                if i >= 1:
                    rows = pl.ds(8 * (i - 1), 8)
                    p_r, p_i = sr_ref[rows, ls], si_ref[rows, ls]
                else:
                    rows = pl.ds(8 * (SEG - 1), 8)
                    p_r = jnp.where(rid >= 1, pltpu.roll(sr_ref[rows, ls], 1, axis=0), pr8_ref[7:8, ls] * keep)
                    p_i = jnp.where(rid >= 1, pltpu.roll(si_ref[rows, ls], 1, axis=0), pi8_ref[7:8, ls] * keep)
                t_r, t_i = g_r * p_r + g_i * p_i, g_i * p_r - g_r * p_i
                return (t_r, t_i) if acc is None else (acc[0] + t_r, acc[1] + t_i)

            (car_r[:, ls], car_i[:, ls]), (t_r, t_i) = _segment_scan(
                gr_s, gi_s, pw_ref, PW_ROWS, ls, car_r[:, ls], car_i[:, ls], True, accumulate)
            da_ref[0:8, ls] += t_r
            da_ref[8:16, ls] += t_i
        for j in range(NLC):
            ls = slice(LC * j, LC * j + LC)
            uj = ub[:, 128 * j:128 * j + 128]
            grb = gr_s[:, ls].astype(BF)
            gib = gi_s[:, ls].astype(BF)
            dup_s[:, 128 * j:128 * j + 128] = _dot_nt(grb, br_ref[j]) + _dot_nt(gib, bi_ref[j])
            dbr_ref[j] += _dot_tn(uj, grb)
            dbi_ref[j] += _dot_tn(uj, gib)
        du_ref[...] = (_reorder(pt_ref[...], dup_s[...]) + d_ref[...] * dy).astype(BF)

    p, pt = perm
    return pl.pallas_call(
        body, name="ssm_bwd", grid=(nt,),
        in_specs=[pl.BlockSpec((TS, D), rev), pl.BlockSpec((TS, D), rev), pl.BlockSpec((TS, SW), rev),
                  pl.BlockSpec((TS, SW), rev), pl.BlockSpec((8, SW), prev8), pl.BlockSpec((8, SW), prev8),
                  _full((2 * PW_ROWS, SW)), _full((NLC, 128, LC)), _full((NLC, 128, LC)),
                  _full((NLC, LC, 128)), _full((NLC, LC, 128)), _full((1, D)), _full((TS, TS)), _full((TS, TS))],
        out_specs=[pl.BlockSpec((TS, D), rev), _full((16, SW)), _full((NLC, 128, LC)), _full((NLC, 128, LC)),
                   _full((NLC, 128, LC)), _full((NLC, 128, LC)), _full((1, D))],
        out_shape=[jax.ShapeDtypeStruct((lp, D), BF), jax.ShapeDtypeStruct((16, SW), F32),
                   jax.ShapeDtypeStruct((NLC, 128, LC), F32), jax.ShapeDtypeStruct((NLC, 128, LC), F32),
                   jax.ShapeDtypeStruct((NLC, 128, LC), F32), jax.ShapeDtypeStruct((NLC, 128, LC), F32),
                   jax.ShapeDtypeStruct((1, D), F32)],
        scratch_shapes=[pltpu.VMEM((TS, SW), F32), pltpu.VMEM((TS, SW), F32),
                        pltpu.VMEM((1, SW), F32), pltpu.VMEM((1, SW), F32), pltpu.VMEM((TS, D), F32)],
        compiler_params=_cp(("arbitrary",)),
    )(dy, u, sr, si, sr, si, pw, br, bi, cr, ci, dsk, p, pt)


def _merge_parts(attn, zz, gg, abn, sbn):
    za, zb = zz[:, 0:D], zz[:, D:2 * D]
    sgz = _sigmoid(zb)
    ssm = za * sgz
    ra = lax.rsqrt(jnp.mean(attn * attn, axis=-1, keepdims=True) + EPS)
    rs = lax.rsqrt(jnp.mean(ssm * ssm, axis=-1, keepdims=True) + EPS)
    ah, sh = attn * ra, ssm * rs
    sga, sgs = _sigmoid(gg[:, 0:D]), _sigmoid(gg[:, D:2 * D])
    return za, sgz, ra, rs, ah, sh, sga, sgs, ah * abn, sh * sbn


def mid_fwd(y, attn, gg, x, mpad, w_glu, w_out, abn, sbn):
    lp = y.shape[0]

    def body(y_ref, a_ref, gg_ref, x_ref, m_ref, wg_ref, wo_ref, abn_ref, sbn_ref, z_o, zz_o, mg_o, h1_o):
        z, _ = _gelu(y_ref[...])
        zb = z.astype(BF)
        z_o[...] = zb
        zz = _dot(zb, wg_ref[...])
        zz_o[...] = zz.astype(BF)
        _, _, _, _, _, _, sga, sgs, an, sn = _merge_parts(a_ref[...], zz, gg_ref[...], abn_ref[...], sbn_ref[...])
        mb = (sga * an + sgs * sn).astype(BF)
        mg_o[...] = mb
        h1_o[...] = _seq_tile(x_ref, m_ref) + _dot(mb, wo_ref[...])

    return pl.pallas_call(
        body, name="mid_fwd", grid=(lp // TM,),
        in_specs=[_rows(TM, D), _rows(TM, D), _rows(TM, 2 * D)] + _seq_specs() + [_full((D, 2 * D)), _full((D, D)),
                  _full((1, D)), _full((1, D))],
        out_specs=[_rows(TM, D), _rows(TM, 2 * D), _rows(TM, D), _rows(TM, D)],
        out_shape=[jax.ShapeDtypeStruct((lp, D), BF), jax.ShapeDtypeStruct((lp, 2 * D), BF),
                   jax.ShapeDtypeStruct((lp, D), BF), jax.ShapeDtypeStruct((lp, D), F32)],
        compiler_params=_cp(("arbitrary",)),
    )(y, attn, gg, x, mpad, w_glu, w_out, abn, sbn)


def mid_bwd(dh1b, y, attn, gg, zzb, w_glu, w_out, abn, sbn):
    lp = y.shape[0]

    def body(dh_ref, y_ref, a_ref, gg_ref, zz_ref, wg_ref, wo_ref, abn_ref, sbn_ref,
             dy_o, dattn_o, dgg_o, dzz_o, dabn_o, dsbn_o):
        i = pl.program_id(0)

        @pl.when(i == 0)
        def _():
            dabn_o[...] = jnp.zeros_like(dabn_o)
            dsbn_o[...] = jnp.zeros_like(dsbn_o)

        dm = _dot_nt(dh_ref[...], wo_ref[...])
        abn, sbn = abn_ref[...], sbn_ref[...]
        za, sgz, ra, rs, ah, sh, sga, sgs, an, sn = _merge_parts(
            a_ref[...], zz_ref[...].astype(F32), gg_ref[...], abn, sbn)
        da = dm * sga
        dabn_o[...] += jnp.sum(da * ah, axis=0, keepdims=True)
        dah = da * abn
        pa = dah * ah
        dgg_o[:, 0:D] = (pa * (1.0 - sga)).astype(BF)
        dattn_o[...] = (ra * (dah - ah * jnp.mean(pa, axis=-1, keepdims=True))).astype(BF)
        ds = dm * sgs
        dsbn_o[...] += jnp.sum(ds * sh, axis=0, keepdims=True)
        dsh = ds * sbn
        ps = dsh * sh
        dgg_o[:, D:2 * D] = (ps * (1.0 - sgs)).astype(BF)
        dssm = rs * (dsh - sh * jnp.mean(ps, axis=-1, keepdims=True))
        dza = dssm * sgz
        dzz_o[:, 0:D] = dza.astype(BF)
        dzz_o[:, D:2 * D] = (dza * za * (1.0 - sgz)).astype(BF)
        dz = _dot_nt(dzz_o[...], wg_ref[...])
        yv = y_ref[...]
        _, t = _gelu(yv)
        dy_o[...] = dz * _gelu_grad(yv, t)

    return pl.pallas_call(
        body, name="mid_bwd", grid=(lp // TM,),
        in_specs=[_rows(TM, D), _rows(TM, D), _rows(TM, D), _rows(TM, 2 * D), _rows(TM, 2 * D),
                  _full((D, 2 * D)), _full((D, D)), _full((1, D)), _full((1, D))],
        out_specs=[_rows(TM, D), _rows(TM, D), _rows(TM, 2 * D), _rows(TM, 2 * D), _full((1, D)), _full((1, D))],
        out_shape=[jax.ShapeDtypeStruct((lp, D), F32), jax.ShapeDtypeStruct((lp, D), BF),
                   jax.ShapeDtypeStruct((lp, 2 * D), BF), jax.ShapeDtypeStruct((lp, 2 * D), BF),
                   jax.ShapeDtypeStruct((1, D), F32), jax.ShapeDtypeStruct((1, D), F32)],
        compiler_params=_cp(("arbitrary",)),
    )(dh1b, y, attn, gg, zzb, w_glu, w_out, abn, sbn)


def ffn_fwd_bwd(h1, tgt, w_fi, w_fo, gf):
    lp = h1.shape[0]
    tf = 256
    bounds = (0, 1536, DFF)

    def body(h_ref, t_ref, wi_ref, wo_ref, gf_ref, hn_o, dgu_o, act_o, dh2_o, dh1_o, dh1b_o, dgf_o, loss_o,
             gate_s, up_s):
        i = pl.program_id(0)

        @pl.when(i == 0)
        def _():
            dgf_o[...] = jnp.zeros_like(dgf_o)
            loss_o[...] = jnp.zeros_like(loss_o)

        h = h_ref[...]
        r = lax.rsqrt(jnp.mean(h * h, axis=-1, keepdims=True) + EPS)
        xh = h * r
        hb = (xh * gf_ref[...]).astype(BF)
        hn_o[...] = hb
        h2 = h
        for c0, c1 in zip(bounds[:-1], bounds[1:]):
            gate = _dot(hb, wi_ref[:, c0:c1])
            up = _dot(hb, wi_ref[:, DFF + c0:DFF + c1])
            gate_s[:, c0:c1] = gate
            up_s[:, c0:c1] = up
            ab = (gate * _sigmoid(gate) * up).astype(BF)
            act_o[:, c0:c1] = ab
            h2 = h2 + _dot(ab, wo_ref[c0:c1, :])
        real = (i >= PAD // tf).astype(F32)
        e = (h2 - t_ref[...]) * real
        loss_o[...] += 0.5 * jnp.sum(jnp.sum(e * e, axis=-1, keepdims=True), axis=0, keepdims=True) * (1.0 / D)
        dh2 = e * (1.0 / D)
        db = dh2.astype(BF)
        dh2_o[...] = db
        dhn = jnp.zeros((tf, D), F32)
        for c0, c1 in zip(bounds[:-1], bounds[1:]):
            gate = gate_s[:, c0:c1]
            up = up_s[:, c0:c1]
            sg = _sigmoid(gate)
            dact = _dot_nt(db, wo_ref[c0:c1, :])
            dgate = (dact * up * (sg * (1.0 + gate * (1.0 - sg)))).astype(BF)
            dup = (dact * (gate * sg)).astype(BF)
            dgu_o[:, c0:c1] = dgate
            dgu_o[:, DFF + c0:DFF + c1] = dup
            dhn += _dot_nt(dgate, wi_ref[:, c0:c1]) + _dot_nt(dup, wi_ref[:, DFF + c0:DFF + c1])
        dgf_o[...] += jnp.sum(dhn * xh, axis=0, keepdims=True)
        dxh = dhn * gf_ref[...]
        dh1 = dh2 + r * (dxh - xh * jnp.mean(dxh * xh, axis=-1, keepdims=True))
        dh1_o[...] = dh1
        dh1b_o[...] = dh1.astype(BF)

    return pl.pallas_call(
        body, name="ffn_fwd_bwd", grid=(lp // tf,),
        in_specs=[_rows(tf, D), pl.BlockSpec((tf, D), lambda i: (jnp.maximum(i - PAD // tf, 0), 0)),
                  _const((D, 2 * DFF)), _const((DFF, D)), _full((1, D))],
        out_specs=[_rows(tf, D), _rows(tf, 2 * DFF), _rows(tf, DFF), _rows(tf, D), _rows(tf, D), _rows(tf, D),
                   _full((1, D)), _full((1, 1))],
        out_shape=[jax.ShapeDtypeStruct((lp, D), BF), jax.ShapeDtypeStruct((lp, 2 * DFF), BF),
                   jax.ShapeDtypeStruct((lp, DFF), BF), jax.ShapeDtypeStruct((lp, D), BF),
                   jax.ShapeDtypeStruct((lp, D), F32), jax.ShapeDtypeStruct((lp, D), BF),
                   jax.ShapeDtypeStruct((1, D), F32), jax.ShapeDtypeStruct((1, 1), F32)],
        scratch_shapes=[pltpu.VMEM((tf, DFF), F32), pltpu.VMEM((tf, DFF), F32)],
        compiler_params=_cp(("arbitrary",), vmem_mb=58),
    )(h1, tgt, w_fi, w_fo, gf)


def weight_grad(name, a, b, tn, with_bf16=False):
    lp, k = a.shape
    n = b.shape[1]
    tr = next(t for t in (2112, 1056, 768, 512, 384, 256, 128) if lp % t == 0 and t * k * 2 <= (5 << 20))

    def body(a_ref, b_ref, o_ref, *ob_ref):
        @pl.when(pl.program_id(1) == 0)
        def _():
            o_ref[...] = jnp.zeros_like(o_ref)

        o_ref[...] += _dot_tn(a_ref[...], b_ref[...])
        if with_bf16:
            @pl.when(pl.program_id(1) == lp // tr - 1)
            def _():
                ob_ref[0][...] = o_ref[...].astype(BF)

    spec = pl.BlockSpec((k, tn), lambda j, m: (0, j))
    return pl.pallas_call(
        body, name=name, grid=(n // tn, lp // tr),
        in_specs=[pl.BlockSpec((tr, k), lambda j, m: (m, 0)), pl.BlockSpec((tr, tn), lambda j, m: (m, j))],
        out_specs=[spec, spec] if with_bf16 else spec,
        out_shape=([jax.ShapeDtypeStruct((k, n), F32), jax.ShapeDtypeStruct((k, n), BF)] if with_bf16
                   else jax.ShapeDtypeStruct((k, n), F32)),
        compiler_params=_cp(("arbitrary", "arbitrary")),
    )(a, b)


def _adamw_math(w, g, m, v):
    m = B1 * m + (1.0 - B1) * g
    v = B2 * v + (1.0 - B2) * (g * g)
    delta = -LR * ((m / BC1) / (jnp.sqrt(v / BC2) + AEPS) + WD * w)
    return delta, m, v


def _row_tile(r):
    return next((t for t in (256, 128, 64, 32, 16, 8) if r % t == 0), r)


def adamw(name, w, g, m, v):
    r, c = w.shape
    tr = _row_tile(r)

    def body(w_ref, g_ref, m_ref, v_ref, g_o, d_o, m_o, v_o):
        g = g_ref[...]
        g_o[...] = g
        d_o[...], m_o[...], v_o[...] = _adamw_math(w_ref[...], g, m_ref[...], v_ref[...])

    spec = pl.BlockSpec((tr, c), lambda i: (i, 0))
    return pl.pallas_call(
        body, name=name, grid=(r // tr,), in_specs=[spec] * 4, out_specs=[spec] * 4,
        out_shape=[jax.ShapeDtypeStruct((r, c), F32)] * 4,
        compiler_params=_cp(("arbitrary",)),
    )(w, g, m, v)


def adamw_small(ws, gs, ms, vs):
    n = len(ws)
    steps = 8

    def spec(a):
        if a.ndim == 4:
            return pl.BlockSpec((1, a.shape[1] // steps) + a.shape[2:], lambda i: (0, i, 0, 0))
        nd = a.ndim
        return pl.BlockSpec(a.shape, lambda i: (0,) * nd)

    def body(*refs):
        w, g, m, v, d_o, m_o, v_o = (refs[k * n:(k + 1) * n] for k in range(7))
        for t in range(n):
            d_o[t][...], m_o[t][...], v_o[t][...] = _adamw_math(w[t][...], g[t][...], m[t][...], v[t][...])

    specs = [spec(a) for a in ws]
    res = pl.pallas_call(
        body, name="adamw_small", grid=(steps,), in_specs=specs * 4, out_specs=specs * 3,
        out_shape=[jax.ShapeDtypeStruct(a.shape, F32) for a in ws] * 3,
        compiler_params=_cp(("arbitrary",)),
    )(*ws, *gs, *ms, *vs)
    return res[:n], res[n:2 * n], res[2 * n:]


def pair_sum(name, place, full, got, ax):
    r, c = got.shape
    tr = _row_tile(r)

    def body(s_ref, a_ref, b_ref, o_ref):
        o_ref[...] = (a_ref[...] + b_ref[...]).astype(BF)

    if ax == 1:
        mine = pl.BlockSpec((tr, c), lambda i, s: (s[1] * (r // tr) + i, 0))
    else:
        mine = pl.BlockSpec((tr, c), lambda i, s: (i, s[1]))
    spec = pl.BlockSpec((tr, c), lambda i, s: (i, 0))
    return pl.pallas_call(
        body, name=name,
        grid_spec=pltpu.PrefetchScalarGridSpec(num_scalar_prefetch=1, grid=(r // tr,), in_specs=[mine, spec],
                                               out_specs=spec),
        out_shape=jax.ShapeDtypeStruct((r, c), BF),
        compiler_params=_cp(("arbitrary",)),
    )(place, full, got)


def sum4(name, place, parts, half, ax):
    _, r, c = parts.shape
    tr = _row_tile(r)

    def body(s_ref, p_ref, h_ref, o_ref):
        o_ref[...] = ((p_ref[0].astype(F32) + p_ref[1].astype(F32))
                      + (p_ref[2].astype(F32) + h_ref[...].astype(F32)))

    if ax == 1:
        own = pl.BlockSpec((tr, c), lambda i, s: (i, s[0]))
        out = pl.BlockSpec((tr, c), lambda i, s: (s[1] * (r // tr) + i, 0))
        shape = (2 * r, c)
    else:
        own = pl.BlockSpec((tr, c), lambda i, s: (s[0] * (r // tr) + i, 0))
        out = pl.BlockSpec((tr, c), lambda i, s: (i, s[1]))
        shape = (r, 2 * c)
    return pl.pallas_call(
        body, name=name,
        grid_spec=pltpu.PrefetchScalarGridSpec(
            num_scalar_prefetch=1, grid=(r // tr,),
            in_specs=[pl.BlockSpec((3, tr, c), lambda i, s: (0, i, 0)), own], out_specs=out),
        out_shape=jax.ShapeDtypeStruct(shape, F32),
        compiler_params=_cp(("arbitrary",)),
    )(place, parts, half)


def sum8(name, place, parts, full, ax):
    _, r, c = parts.shape
    tr = _row_tile(r)

    def body(s_ref, p_ref, g_ref, o_ref):
        p = [p_ref[k].astype(F32) for k in range(7)]
        o_ref[...] = ((p[0] + p[1]) + (p[2] + p[3])) + ((p[4] + p[5]) + (p[6] + g_ref[...]))

    if ax == 1:
        own = pl.BlockSpec((tr, c), lambda i, s: (s[1] * (r // tr) + i, s[0]))
        out = pl.BlockSpec((tr, c), lambda i, s: (s[1] * (r // tr) + i, 0))
        shape = (2 * r, c)
    else:
        own = pl.BlockSpec((tr, c), lambda i, s: (s[0] * (r // tr) + i, s[1]))
        out = pl.BlockSpec((tr, c), lambda i, s: (i, s[1]))
        shape = (r, 2 * c)
    return pl.pallas_call(
        body, name=name,
        grid_spec=pltpu.PrefetchScalarGridSpec(
            num_scalar_prefetch=1, grid=(r // tr,),
            in_specs=[pl.BlockSpec((7, tr, c), lambda i, s: (0, i, 0)), own], out_specs=out),
        out_shape=jax.ShapeDtypeStruct(shape, F32),
        compiler_params=_cp(("arbitrary",)),
    )(place, parts, full)


def _place():
    x, y, c = lax.axis_index("x"), lax.axis_index("y"), lax.axis_index("c")
    return x, y, c


def _other_chips(x, y):
    return [(1 - x, y), (x, 1 - y), (1 - x, 1 - y)]


SHARDED = (("w_in", D, IN_COLS, 1), ("w_glu", D, 2 * D, 1), ("w_out", D, D, 0),
           ("w_ffn_in", D, 2 * DFF, 1), ("w_ffn_out", DFF, D, 0))


def _shard_of(ref, axis, k, size):
    if axis == 0:
        return ref.at[pl.ds(k * size, size), :]
    return ref.at[:, pl.ds(k * size, size)]


def _gather_piece(full, dims, k, h):
    r, cc, ax = dims
    if ax == 0:
        return full.at[pl.ds(k * (r // 4) + h * (r // 8), r // 8), :]
    return full.at[pl.ds(h * (r // 2), r // 2), pl.ds(k * (cc // 4), cc // 4)]


def _rows_half(shard, h):
    rs = shard.shape[0]
    return shard.at[pl.ds(h * (rs // 2), rs // 2), :]


SEM = pl.BlockSpec(memory_space=pltpu.SEMAPHORE)
HBM = pl.BlockSpec(memory_space=pltpu.HBM)
EFFECT = pltpu.SideEffectType.DATAFLOW_SIDE_EFFECTING


def _in_hbm(a):
    return pltpu.with_memory_space_constraint(a, pltpu.HBM)


def _split_copy_start(name, copies, ncopies, srcs, lands, after):
    ns, nl = len(srcs), len(lands)

    def body(*refs):
        ins, land_refs = refs[:ns], refs[ns:ns + nl]
        send_sems, recv_sems = refs[ns + nl + 1], refs[ns + nl + 2]
        token = refs[-1]
        for k, (src, dst, dev) in enumerate(copies(ins, land_refs)):
            pltpu.make_async_remote_copy(src_ref=src, dst_ref=dst, send_sem=send_sems.at[k],
                                         recv_sem=recv_sems.at[k], device_id=dev, device_id_type=MESH).start()
        token[...] = jnp.zeros_like(token)

    res = pl.pallas_call(
        body, name=name,
        in_specs=[HBM] * (ns + nl) + [ANY],
        out_specs=(SEM, SEM) + (HBM,) * (ns + nl) + (pl.BlockSpec(memory_space=pltpu.VMEM),),
        out_shape=(pltpu.SemaphoreType.DMA((ncopies,)), pltpu.SemaphoreType.DMA((ncopies,)))
        + tuple(pltpu.HBM(a.shape, a.dtype) for a in list(srcs) + list(lands))
        + (jax.ShapeDtypeStruct((8, 128), F32),),
        input_output_aliases={t: 2 + t for t in range(ns + nl)},
        compiler_params=pltpu.CompilerParams(has_side_effects=EFFECT),
    )(*[_in_hbm(a) for a in srcs], *[_in_hbm(a) for a in lands], after)
    return res[0], res[1], list(res[2:2 + ns]), list(res[2 + ns:2 + ns + nl]), res[-1]


def _split_copy_wait(name, copies, arrivals, send_sems, recv_sems, srcs, lands, after):
    ns, nl = len(srcs), len(lands)

    def body(*refs):
        ins, land_refs = refs[:ns], refs[ns:ns + nl]
        send_sems_ref, recv_sems_ref = refs[ns + nl], refs[ns + nl + 1]
        mine = copies(ins, land_refs)
        for k, ((src, dst, dev), got) in enumerate(zip(mine, arrivals(ins, land_refs))):
            pltpu.make_async_remote_copy(src_ref=src, dst_ref=dst, send_sem=send_sems_ref.at[k],
                                         recv_sem=recv_sems_ref.at[k], device_id=dev,
                                         device_id_type=MESH).wait_send()
            pltpu.make_async_remote_copy(src_ref=got, dst_ref=got, send_sem=send_sems_ref.at[k],
                                         recv_sem=recv_sems_ref.at[k], device_id=dev,
                                         device_id_type=MESH).wait_recv()

    res = pl.pallas_call(
        body, name=name,
        in_specs=[HBM] * (ns + nl) + [SEM, SEM, ANY],
        out_specs=(HBM,) * (ns + nl),
        out_shape=tuple(pltpu.HBM(a.shape, a.dtype) for a in list(srcs) + list(lands)),
        input_output_aliases={t: t for t in range(ns + nl)},
        compiler_params=pltpu.CompilerParams(has_side_effects=EFFECT),
    )(*srcs, *lands, send_sems, recv_sems, after)
    return list(res[:ns]), list(res[ns:])


def _gather_copies(dims):
    def copies(ins, lands):
        x, y, c = _place()
        mine = 2 * x + y
        out = []
        for t, d in enumerate(dims):
            size = (d[0] if d[2] == 0 else d[1]) // 4
            for px, py in _other_chips(x, y):
                for dc in range(2):
                    out.append((_rows_half(ins[t], c), _gather_piece(lands[t], d, mine, c), (px, py, c ^ dc)))
            out.append((ins[t], _shard_of(lands[t], d[2], mine, size), (x, y, 1 - c)))
        return out

    def arrivals(ins, lands):
        x, y, c = _place()
        mine = 2 * x + y
        out = []
        for t, d in enumerate(dims):
            size = (d[0] if d[2] == 0 else d[1]) // 4
            for px, py in _other_chips(x, y):
                for dc in range(2):
                    out.append(_gather_piece(lands[t], d, 2 * px + py, c ^ dc))
            out.append(_shard_of(lands[t], d[2], mine, size))
        return out

    return copies, arrivals


def _half_of(ref, ax, h):
    r, c = ref.shape
    if ax == 1:
        return ref.at[pl.ds(h * (r // 2), r // 2), :]
    return ref.at[:, pl.ds(h * (c // 2), c // 2)]


def _half_shape(r, c, ax):
    return (r // 2, c) if ax == 1 else (r, c // 2)


def pair_exchange(name, grads, idxs):
    n = len(grads)
    which = [SHARDED[i] for i in idxs]

    def body(*refs):
        ins, got = refs[:n], refs[n:2 * n]
        send_sems, recv_sems = refs[2 * n:]
        x, y, c = _place()
        cps = []
        for t, (_, _, _, ax) in enumerate(which):
            cp = pltpu.make_async_remote_copy(
                src_ref=_half_of(ins[t], ax, 1 - c), dst_ref=got[t], send_sem=send_sems.at[t],
                recv_sem=recv_sems.at[t], device_id=(x, y, 1 - c), device_id_type=MESH)
            cp.start()
            cps.append(cp)
        for cp in cps:
            cp.wait()

    return pl.pallas_call(
        body, name=name,
        in_specs=[ANY] * n, out_specs=[ANY] * n,
        out_shape=[jax.ShapeDtypeStruct(_half_shape(r, c, ax), F32) for _, r, c, ax in which],
        scratch_shapes=[pltpu.SemaphoreType.DMA((n,)), pltpu.SemaphoreType.DMA((n,))],
        compiler_params=pltpu.CompilerParams(has_side_effects=True),
    )(*grads)


def _piece_shapes(idxs):
    out = []
    for _, r, c, ax in [SHARDED[i] for i in idxs]:
        hr, hc = _half_shape(r, c, ax)
        out.append((hr // 4, hc) if ax == 0 else (hr, hc // 4))
    return out


def _piece_of(full, ax, k, h):
    r, c = full.shape
    if ax == 1:
        return full.at[pl.ds(h * (r // 2), r // 2), pl.ds(k * (c // 4), c // 4)]
    return full.at[pl.ds(k * (r // 4), r // 4), pl.ds(h * (c // 2), c // 2)]


def _scatter8_copies(idxs):
    which = [SHARDED[i] for i in idxs]

    def copies(ins, lands):
        x, y, c = _place()
        out = []
        for t, (_, _, _, ax) in enumerate(which):
            for j, (px, py) in enumerate(_other_chips(x, y)):
                for d in range(2):
                    out.append((_piece_of(ins[t], ax, 2 * px + py, c ^ d), lands[t].at[2 * j + d], (px, py, c ^ d)))
            out.append((_piece_of(ins[t], ax, 2 * x + y, 1 - c), lands[t].at[6], (x, y, 1 - c)))
        return out

    def arrivals(ins, lands):
        return [lands[t].at[k] for t in range(len(which)) for k in range(7)]

    return copies, arrivals


def scatter_grads(halves, idxs):
    n = len(halves)
    which = [SHARDED[i] for i in idxs]

    def shapes():
        return _piece_shapes(idxs)

    def body(*refs):
        ins, outs = refs[:n], refs[n:2 * n]
        send_sems, recv_sems = refs[2 * n:]
        x, y, c = _place()
        started = []
        for t, (src, dst, (_, _, _, ax)) in enumerate(zip(ins, outs, which)):
            size = src.shape[ax] // 4
            for j, (px, py) in enumerate(_other_chips(x, y)):
                cp = pltpu.make_async_remote_copy(
                    src_ref=_shard_of(src, ax, 2 * px + py, size), dst_ref=dst.at[j],
                    send_sem=send_sems.at[3 * t + j], recv_sem=recv_sems.at[3 * t + j],
                    device_id=(px, py, c), device_id_type=MESH)
                cp.start()
                started.append(cp)
        for cp in started:
            cp.wait()

    return pl.pallas_call(
        body, name="scatter_grads",
        in_specs=[ANY] * n, out_specs=[ANY] * n,
        out_shape=[jax.ShapeDtypeStruct((3,) + s, BF) for s in shapes()],
        scratch_shapes=[pltpu.SemaphoreType.DMA((3 * n,)), pltpu.SemaphoreType.DMA((3 * n,))],
        compiler_params=pltpu.CompilerParams(has_side_effects=True),
    )(*halves)


def join_halves(name, shards, idxs):
    n = len(shards)
    which = [SHARDED[i] for i in idxs]

    def body(*refs):
        ins, outs = refs[:n], refs[n:2 * n]
        send_sems, recv_sems = refs[2 * n:]
        x, y, c = _place()
        cps = []
        for t, (_, _, _, ax) in enumerate(which):
            cp = pltpu.make_async_remote_copy(
                src_ref=_half_of(ins[t], ax, c), dst_ref=_half_of(outs[t], ax, c), send_sem=send_sems.at[t],
                recv_sem=recv_sems.at[t], device_id=(x, y, 1 - c), device_id_type=MESH)
            cp.start()
            cps.append(cp)
        for t, cp in enumerate(cps):
            cp.wait_send()
            theirs = _half_of(outs[t], which[t][3], 1 - c)
            pltpu.make_async_remote_copy(
                src_ref=theirs, dst_ref=theirs, send_sem=send_sems.at[t], recv_sem=recv_sems.at[t],
                device_id=(x, y, 1 - c), device_id_type=MESH).wait_recv()

    return pl.pallas_call(
        body, name=name,
        in_specs=[ANY] * n, out_specs=[ANY] * n,
        out_shape=[jax.ShapeDtypeStruct(s.shape, F32) for s in shards],
        input_output_aliases={t: t for t in range(n)},
        scratch_shapes=[pltpu.SemaphoreType.DMA((n,)), pltpu.SemaphoreType.DMA((n,))],
        compiler_params=pltpu.CompilerParams(has_side_effects=True),
    )(*shards)


def allreduce_small(pack):
    rows = pack.shape[0]
    hr = rows // 2

    def body(p_ref, o_ref, sib_ref, parts_ref, send_sems, recv_sems):
        x, y, c = _place()
        mine = 2 * x + y
        sibling = (x, y, 1 - c)
        swap = pltpu.make_async_remote_copy(
            src_ref=p_ref, dst_ref=sib_ref, send_sem=send_sems.at[0], recv_sem=recv_sems.at[0],
            device_id=sibling, device_id_type=MESH)
        swap.start()
        swap.wait()
        half = pl.ds(pl.multiple_of(c * hr, 8), hr)
        parts_ref[mine] = p_ref[half, :] + sib_ref[half, :]
        cps = []
        for j, chip in enumerate(_other_chips(x, y)):
            cp = pltpu.make_async_remote_copy(
                src_ref=parts_ref.at[mine], dst_ref=parts_ref.at[mine], send_sem=send_sems.at[1 + j],
                recv_sem=recv_sems.at[1 + j], device_id=(*chip, c), device_id_type=MESH)
            cp.start()
            cps.append(cp)
        for j, (px, py) in enumerate(_other_chips(x, y)):
            cps[j].wait_send()
            theirs = parts_ref.at[2 * px + py]
            pltpu.make_async_remote_copy(
                src_ref=theirs, dst_ref=theirs, send_sem=send_sems.at[1 + j], recv_sem=recv_sems.at[1 + j],
                device_id=(px, py, c), device_id_type=MESH).wait_recv()
        o_ref[half, :] = (parts_ref[0] + parts_ref[1]) + (parts_ref[2] + parts_ref[3])
        back = pltpu.make_async_remote_copy(
            src_ref=o_ref.at[half, :], dst_ref=o_ref.at[half, :], send_sem=send_sems.at[4],
            recv_sem=recv_sems.at[4], device_id=sibling, device_id_type=MESH)
        back.start()
        back.wait_send()
        other = pl.ds(pl.multiple_of((1 - c) * hr, 8), hr)
        pltpu.make_async_remote_copy(
            src_ref=o_ref.at[other, :], dst_ref=o_ref.at[other, :], send_sem=send_sems.at[4],
            recv_sem=recv_sems.at[4], device_id=sibling, device_id_type=MESH).wait_recv()

    return pl.pallas_call(
        body, name="allreduce_small",
        in_specs=[pl.BlockSpec(memory_space=pltpu.VMEM)], out_specs=pl.BlockSpec(memory_space=pltpu.VMEM),
        out_shape=jax.ShapeDtypeStruct((rows, 128), F32),
        scratch_shapes=[pltpu.VMEM((rows, 128), F32), pltpu.VMEM((4, hr, 128), F32),
                        pltpu.SemaphoreType.DMA((5,)), pltpu.SemaphoreType.DMA((5,))],
        compiler_params=pltpu.CompilerParams(has_side_effects=True, vmem_limit_bytes=40 << 20),
    )(pack)


def _ssm_discretize(lam_re, lam_im, log_dt, b_re, b_im):
    dt = jnp.exp(log_dt)[:, None]
    mag = jnp.exp(lam_re * dt)
    ar, ai = mag * jnp.cos(lam_im * dt), mag * jnp.sin(lam_im * dt)
    den = lam_re * lam_re + lam_im * lam_im
    nr, ni = ar - 1.0, ai
    fr, fi = (nr * lam_re + ni * lam_im) / den, (ni * lam_re - nr * lam_im) / den
    bbr = fr[:, None, :] * b_re - fi[:, None, :] * b_im
    bbi = fr[:, None, :] * b_im + fi[:, None, :] * b_re
    return ar, ai, bbr, bbi


def _cmul(a, b):
    return a[0] * b[0] - a[1] * b[1], a[0] * b[1] + a[1] * b[0]


def _scan_powers(ar, ai):
    a = (ar.reshape(1, SW), ai.reshape(1, SW))
    big = a
    for _ in range(SEG.bit_length() - 1):
        big = _cmul(big, big)
    assert 1 << (SEG.bit_length() - 1) == SEG
    pows = {1: big}
    pows[2] = _cmul(big, big)
    pows[4] = _cmul(pows[2], pows[2])
    rid = jnp.arange(8)[:, None]
    ones = jnp.ones((8, 1), F32)
    fwd, rev = [ones * a[0], ones * a[1]], [ones * a[0], -ones * a[1]]
    for k in (1, 2, 4):
        pr, pi = pows[k]
        fwd += [jnp.where(rid >= k, pr, 0.0), jnp.where(rid >= k, pi, 0.0)]
        rev += [jnp.where(rid < 8 - k, pr, 0.0), jnp.where(rid < 8 - k, -pi, 0.0)]
    return jnp.concatenate(fwd + rev, axis=0)


def _pack_b(bb):
    t = bb.reshape(NLC, 8, GCH, NSTATE)
    return jnp.einsum("jgcp,gh->jgchp", t, jnp.eye(8, dtype=bb.dtype)).reshape(NLC, 128, LC)


def _unpack_b(db):
    t = jnp.einsum("jgchp,gh->jgcp", db.reshape(NLC, 8, GCH, 8, NSTATE), jnp.eye(8, dtype=db.dtype))
    return t.reshape(GROUPS, GCH, NSTATE)


def _pack_c(cc):
    t = cc.reshape(NLC, 8, GCH, NSTATE).transpose(0, 1, 3, 2)
    return jnp.einsum("jgpc,gh->jgphc", t, jnp.eye(8, dtype=cc.dtype)).reshape(NLC, LC, 128)


def _unpack_c(dc):
    t = jnp.einsum("jgphc,gh->jgpc", dc.reshape(NLC, 8, NSTATE, 8, GCH), jnp.eye(8, dtype=dc.dtype))
    return t.transpose(0, 1, 3, 2).reshape(GROUPS, GCH, NSTATE)


SMALL = ("norm_mix", "q_norm", "k_norm", "attn_sinks", "lam_re", "lam_im", "log_dt", "ssm_b_re", "ssm_b_im",
         "ssm_c_re", "ssm_c_im", "ssm_d", "attn_branch_norm", "ssm_branch_norm", "norm_ffn")


def _pack_small(arrs, rows=None):
    flat = jnp.concatenate([a.reshape(-1).astype(F32) for a in arrs])
    n = flat.shape[0]
    if rows is None:
        rows = -(-n // (128 * 256)) * 256
    return jnp.pad(flat, (0, rows * 128 - n)).reshape(rows, 128)


def _unpack_small(pack, like):
    flat = pack.reshape(-1)
    out, off = [], 0
    for a in like:
        out.append(flat[off:off + a.size].reshape(a.shape))
        off += a.size
    return out


def _gather_half_copies(dims):
    def copies(ins, lands):
        x, y, c = _place()
        mine = 2 * x + y
        out = []
        for t, d in enumerate(dims):
            size = (d[0] if d[2] == 0 else d[1]) // 4
            for chip in _other_chips(x, y):
                out.append((_rows_half(ins[t], c), _gather_piece(lands[t], d, mine, c), (*chip, c)))
            out.append((ins[t], _shard_of(lands[t], d[2], mine, size), (x, y, 1 - c)))
        return out

    def arrivals(ins, lands):
        x, y, c = _place()
        mine = 2 * x + y
        out = []
        for t, d in enumerate(dims):
            size = (d[0] if d[2] == 0 else d[1]) // 4
            for px, py in _other_chips(x, y):
                out.append(_gather_piece(lands[t], d, 2 * px + py, c))
            out.append(_shard_of(lands[t], d[2], mine, size))
        return out

    return copies, arrivals


def gather_forward(fulls, dims):
    n = len(fulls)

    def body(*refs):
        ins, outs = refs[:n], refs[n:2 * n]
        send_sems, recv_sems = refs[2 * n:]
        x, y, c = _place()
        cps = []
        for t in range(n):
            for j, (px, py) in enumerate(_other_chips(x, y)):
                cp = pltpu.make_async_remote_copy(
                    src_ref=_gather_piece(ins[t], dims[t], 2 * px + py, c),
                    dst_ref=_gather_piece(outs[t], dims[t], 2 * px + py, c),
                    send_sem=send_sems.at[3 * t + j], recv_sem=recv_sems.at[3 * t + j],
                    device_id=(x, y, 1 - c), device_id_type=MESH)
                cp.start()
                cps.append(cp)
        for t in range(n):
            for j, (px, py) in enumerate(_other_chips(x, y)):
                cps[3 * t + j].wait_send()
                other = _gather_piece(outs[t], dims[t], 2 * px + py, 1 - c)
                pltpu.make_async_remote_copy(
                    src_ref=other, dst_ref=other, send_sem=send_sems.at[3 * t + j], recv_sem=recv_sems.at[3 * t + j],
                    device_id=(x, y, 1 - c), device_id_type=MESH).wait_recv()

    return pl.pallas_call(
        body, name="gather_forward",
        in_specs=[ANY] * n, out_specs=[ANY] * n,
        out_shape=[jax.ShapeDtypeStruct(a.shape, a.dtype) for a in fulls],
        input_output_aliases={t: t for t in range(n)},
        scratch_shapes=[pltpu.SemaphoreType.DMA((3 * n,)), pltpu.SemaphoreType.DMA((3 * n,))],
        compiler_params=pltpu.CompilerParams(has_side_effects=True),
    )(*fulls)


def _tie(a, token):
    return a if token is None else a + token[0, 0]


class _FirstGather:
    dims = [SHARDED[0][1:], (N_META, D, 1)]

    def __init__(self, w_in_shard, meta_shard):
        self.copies, self.arrivals = _gather_half_copies(self.dims)
        lands = [lax.empty((D, IN_COLS), BF), lax.empty((N_META, D), F32)]
        self.send, self.recv, self.srcs, self.lands, self.token = _split_copy_start(
            "gather_start_first", self.copies, 8, [w_in_shard, meta_shard], lands, w_in_shard)

    def finish(self, after):
        _, lands = _split_copy_wait("gather_wait_first", self.copies, self.arrivals, self.send, self.recv, self.srcs,
                                    self.lands, after)
        return gather_forward(lands, self.dims)


class _Exchange:
    def __init__(self, rest_shards, place):
        self.rest_shards = rest_shards
        self.place = place
        self.rest_dims = [s[1:] for s in SHARDED[1:]]
        self.pending = {}
        self.halves = {}
        self.full = {}
        self.parts = {}

    def begin(self, after):
        copies, arrivals = _gather_copies(self.rest_dims)
        lands = [lax.empty((r, c), BF) for r, c, _ in self.rest_dims]
        send, recv, srcs, lands, token = _split_copy_start(
            "gather_start", copies, 7 * len(lands), self.rest_shards, lands, after)
        self.pending["gather"] = (copies, arrivals, send, recv, srcs, lands)
        return token

    def rest_weights(self, after):
        copies, arrivals, send, recv, srcs, lands = self.pending.pop("gather")
        _, fulls = _split_copy_wait("gather_wait", copies, arrivals, send, recv, srcs, lands, after)
        return dict(zip([s[0] for s in SHARDED[1:]], fulls))

    def _halves(self, group, idxs, grads):
        got = pair_exchange("pair_exchange_" + group, grads, idxs)
        return [pair_sum("pair_sum_" + SHARDED[i][0], self.place, g, b, SHARDED[i][3])
                for i, g, b in zip(idxs, grads, got)]

    def start(self, group, idxs, grads):
        if grads[0][1] is None:
            halves = self._halves(group, idxs, [g for g, _ in grads])
            for i, h, pt in zip(idxs, halves, scatter_grads(halves, idxs)):
                self.halves[i], self.parts[i] = h, pt
            return None
        copies, arrivals = _scatter8_copies(idxs)
        lands = [lax.empty((7,) + sh, BF) for sh in _piece_shapes(idxs)]
        send, recv, srcs, lands, token = _split_copy_start(
            "scatter_start_" + group, copies, 7 * len(idxs), [g16 for _, g16 in grads], lands, grads[0][0])
        self.pending[group] = (copies, arrivals, send, recv, srcs, lands, idxs)
        for i, (g32, _) in zip(idxs, grads):
            self.full[i] = g32
        return token

    def finish(self, group, after):
        if group not in self.pending:
            return
        copies, arrivals, send, recv, srcs, lands, idxs = self.pending.pop(group)
        _, lands = _split_copy_wait("scatter_wait_" + group, copies, arrivals, send, recv, srcs, lands, after)
        for i, pt in zip(idxs, lands):
            self.parts[i] = pt


def local_step(x, tgt, p, first, ex):
    qn_t = jnp.tile(p["q_norm"].reshape(1, HEAD), (1, NQ)) * HEAD ** -0.5
    kn_t = jnp.tile(p["k_norm"].reshape(1, HEAD), (1, NKV))
    seg = jnp.arange(256) // HEAD
    bd = (seg[:, None] == seg[None, :]).astype(BF)
    nm = p["norm_mix"].reshape(1, D)
    sinks = p["attn_sinks"].reshape(NQ)
    dsk = p["ssm_d"].reshape(1, D)
    abn, sbn, gf = (p[k].reshape(1, D) for k in ("attn_branch_norm", "ssm_branch_norm", "norm_ffn"))

    disc_in = (p["lam_re"], p["lam_im"], _tie(p["log_dt"], first.token), p["ssm_b_re"], p["ssm_b_im"])
    (ar, ai, bbr, bbi), disc_vjp = jax.vjp(_ssm_discretize, *disc_in)
    pw = _scan_powers(ar, ai)
    brp, bip = _pack_b(bbr.astype(BF)), _pack_b(bbi.astype(BF))
    crp = _pack_c(_tie(p["ssm_c_re"], first.token).astype(BF))
    cip = _pack_c(_tie(p["ssm_c_im"], first.token).astype(BF))

    w_in, meta_full = first.finish(brp)
    mpad = jnp.concatenate([jnp.zeros((META0, D), F32), meta_full], axis=0)
    token = ex.begin(w_in)
    xn, qr, kr, qn, kn, v, u, gg = in_proj_fwd(x, mpad, w_in, _tie(nm, token), qn_t, kn_t, bd)
    attn = attention_fwd(sinks, qn, kn, v)
    perm = _segment_order()
    y, sr, si = ssm_fwd(u, pw, brp, bip, crp, cip, dsk, perm)
    w = ex.rest_weights(y)
    zb, zzb, mgb, h1 = mid_fwd(y, attn, gg, x, mpad, w["w_glu"], w["w_out"], abn, sbn)
    hnb, dgub, actb, dh2b, dh1, dh1b, dgf, loss = ffn_fwd_bwd(h1, tgt, w["w_ffn_in"], w["w_ffn_out"], gf)
    token = ex.start("ffn", (3, 4), [weight_grad("grad_w_ffn_in", hnb, dgub, 1408, True),
                                     weight_grad("grad_w_ffn_out", actb, dh2b, 512, True)])
    dy, dattn, dggb, dzzb, dabn, dsbn = mid_bwd(dh1b, y, attn, gg, zzb, w["w_glu"], w["w_out"], _tie(abn, token),
                                                sbn)
    grads_mid = [weight_grad("grad_w_glu", zb, dzzb, 1024, True), weight_grad("grad_w_out", mgb, dh1b, 1024, True)]
    dub, da, dbr, dbi, dcr, dci, dd = ssm_bwd(dy, u, sr, si, pw, brp, bip, crp, cip, dsk, perm)
    ex.finish("ffn", dub)
    token = ex.start("mid", (1, 2), grads_mid)
    dqn, dkn, dv, dsink = attention_bwd(_tie(sinks, token), qn, kn, v, dattn)
    dproj, dx, dmpad, dnm, dqw, dkw = in_proj_bwd(dqn, dkn, dv, dub, dggb, qr, kr, x, mpad, dh1, w_in, nm, qn_t,
                                                  kn_t, bd)
    ex.finish("mid", dproj)
    ex.start("in", (0,), [(weight_grad("grad_w_in", xn, dproj, 1536), None)])

    dar = jnp.sum(da[0:8], axis=0).reshape(GROUPS, NSTATE)
    dai = jnp.sum(da[8:16], axis=0).reshape(GROUPS, NSTATE)
    dlr, dli, dldt, dbre, dbim = disc_vjp((dar, dai, _unpack_b(dbr), _unpack_b(dbi)))
    small = {
        "norm_mix": dnm, "q_norm": dqw.reshape(NQ, HEAD).sum(0) * HEAD ** -0.5, "k_norm": dkw.reshape(NKV, HEAD).sum(0),
        "attn_sinks": dsink[0, 0:NQ], "lam_re": dlr, "lam_im": dli, "log_dt": dldt,
        "ssm_b_re": dbre, "ssm_b_im": dbim, "ssm_c_re": _unpack_c(jnp.swapaxes(dcr, 1, 2)), "ssm_c_im": _unpack_c(jnp.swapaxes(dci, 1, 2)),
        "ssm_d": dd, "attn_branch_norm": dabn, "ssm_branch_norm": dsbn, "norm_ffn": dgf,
    }
    return loss[0, 0], dx, dmpad[META0:PAD], small


def kernel(x, meta_tokens, norm_mix, w_in, q_norm, k_norm, attn_sinks, lam_re, lam_im, log_dt, ssm_b_re, ssm_b_im, ssm_c_re, ssm_c_im, ssm_d, w_glu, attn_branch_norm, ssm_branch_norm, w_out, norm_ffn, w_ffn_in, w_ffn_out, loss_target, m_meta_tokens, m_norm_mix, m_w_in, m_q_norm, m_k_norm, m_attn_sinks, m_lam_re, m_lam_im, m_log_dt, m_ssm_b_re, m_ssm_b_im, m_ssm_c_re, m_ssm_c_im, m_ssm_d, m_w_glu, m_attn_branch_norm, m_ssm_branch_norm, m_w_out, m_norm_ffn, m_w_ffn_in, m_w_ffn_out, v_meta_tokens, v_norm_mix, v_w_in, v_q_norm, v_k_norm, v_attn_sinks, v_lam_re, v_lam_im, v_log_dt, v_ssm_b_re, v_ssm_b_im, v_ssm_c_re, v_ssm_c_im, v_ssm_d, v_w_glu, v_attn_branch_norm, v_ssm_branch_norm, v_w_out, v_norm_ffn, v_w_ffn_in, v_w_ffn_out):
    args = dict(locals())
    names = ["meta_tokens", "norm_mix", "w_in", "q_norm", "k_norm", "attn_sinks", "lam_re", "lam_im", "log_dt",
             "ssm_b_re", "ssm_b_im", "ssm_c_re", "ssm_c_im", "ssm_d", "w_glu", "attn_branch_norm",
             "ssm_branch_norm", "w_out", "norm_ffn", "w_ffn_in", "w_ffn_out"]
    big_names = [s[0] for s in SHARDED]
    xs, ys, cs = _place()
    chip = 2 * xs + ys

    first = _FirstGather(args["w_in"][0].astype(BF), meta_tokens)
    shards = [None] + [_tie(args[n][0], first.token).astype(BF) for n in big_names[1:]]
    place = jnp.stack([chip, cs]).astype(jnp.int32)
    ex = _Exchange(shards[1:], place)

    swapped = ("ssm_b_re", "ssm_b_im")

    def natural(n, a):
        return jnp.swapaxes(a, -1, -2) if n in swapped else a

    p = {n: natural(n, args[n][0]) for n in SMALL}
    loss, dx, dmeta, gsmall = local_step(x[0], loss_target[0], p, first, ex)

    out = {}

    def finish(idxs, name):
        mine = [sum8("sum8_" + SHARDED[i][0], place, ex.parts[i], ex.full[i], SHARDED[i][3]) if i in ex.full
                else sum4("sum4_" + SHARDED[i][0], place, ex.parts[i], ex.halves[i], SHARDED[i][3]) for i in idxs]
        for i, g in zip(idxs, join_halves(name, mine, idxs)):
            n = SHARDED[i][0]
            out[n] = tuple(t[None] for t in adamw("adamw_" + n, args[n][0], g, args["m_" + n][0], args["v_" + n][0]))

    finish((0, 1, 2, 3, 4), "join_halves")
    small_list = [gsmall[n].reshape(natural(n, args[n]).shape) for n in SMALL] + [dmeta, loss.reshape(1, 1)]
    red = allreduce_small(_pack_small(small_list))
    *small_red, dmeta_sum, loss = _unpack_small(red, small_list)
    loss = loss[0, 0]
    gmeta = lax.dynamic_slice_in_dim(dmeta_sum, chip * (D // 4), D // 4, axis=1)
    out["meta_tokens"] = tuple(adamw("adamw_meta", meta_tokens, gmeta, m_meta_tokens, v_meta_tokens))
    ds, nms, nvs = adamw_small([natural(n, args[n]) for n in SMALL], small_red,
                               [natural(n, args["m_" + n]) for n in SMALL],
                               [natural(n, args["v_" + n]) for n in SMALL])
    for n, g_, d_, m_, v_ in zip(SMALL, small_red, ds, nms, nvs):
        out[n] = tuple(natural(n, t) for t in (g_, d_, m_, v_))

    res = [loss, dx[None]]
    for k in range(4):
        res += [out[n][k] for n in names]
    return tuple(res)
```

```python
import functools
import math

import jax
import jax.numpy as jnp
from jax import lax
from jax.experimental import pallas as pl
from jax.experimental.pallas import tpu as pltpu

F32 = jnp.float32
BF = jnp.bfloat16

D = 1024
N_META = 16
HEAD = 64
NQ = 16
NKV = 4
QW = NQ * HEAD
KVW = NKV * HEAD
WIN = 128
GROUPS = 64
GCH = 16
NSTATE = 64
SW = GROUPS * NSTATE
DFF = 2816
IN_COLS = QW + 2 * KVW + D + 2 * D
PAD = 256
META0 = PAD - N_META
EPS = 1e-6
NEG = -1e30

TM = 256
TS = 256
LC = 512
NLC = SW // LC

LR, B1, B2, AEPS, WD, STEP = 0.001, 0.9, 0.999, 1e-08, 0.01, 10
BC1 = 1.0 - B1 ** STEP
BC2 = 1.0 - B2 ** STEP

MESH = pl.DeviceIdType.MESH
ANY = pl.BlockSpec(memory_space=pl.ANY)


def _cp(sem=None, vmem_mb=48):
    return pltpu.CompilerParams(dimension_semantics=sem, vmem_limit_bytes=vmem_mb << 20)


def _full(shape):
    n = len(shape)
    return pl.BlockSpec(shape, lambda *a: (0,) * n)


def _const(shape):
    n = len(shape)
    return pl.BlockSpec(shape, lambda *a: (0,) * n, pipeline_mode=pl.Buffered(1))


def _rows(tm, width):
    return pl.BlockSpec((tm, width), lambda i: (i, 0))


def _dot(a, b):
    return jnp.dot(a, b, preferred_element_type=F32)


def _dot_nt(a, b):
    return lax.dot_general(a, b, (((1,), (1,)), ((), ())), preferred_element_type=F32)


def _dot_tn(a, b):
    return lax.dot_general(a, b, (((0,), (0,)), ((), ())), preferred_element_type=F32)


def _sigmoid(x):
    return 0.5 * jnp.tanh(0.5 * x) + 0.5


def _seg_mean(x, bd, split=True):
    outs = []
    for j in range(x.shape[1] // 256):
        c = x[:, 256 * j:256 * (j + 1)]
        hi = c.astype(BF)
        part = _dot(hi, bd)
        if split:
            part += _dot((c - hi.astype(F32)).astype(BF), bd)
        outs.append(part)
    out = outs[0] if len(outs) == 1 else jnp.concatenate(outs, axis=1)
    return out * (1.0 / HEAD)


_GC = math.sqrt(2.0 / math.pi)


def _gelu(x):
    t = jnp.tanh(_GC * (x + 0.044715 * x * x * x))
    return 0.5 * x * (1.0 + t), t


def _gelu_grad(x, t):
    return 0.5 * (1.0 + t) + 0.5 * x * (1.0 - t * t) * _GC * (1.0 + 3.0 * 0.044715 * x * x)


def _seq_specs():
    assert TM == PAD
    return [pl.BlockSpec((TM, D), lambda i: (jnp.maximum(i - 1, 0), 0)), _full((PAD, D))]


def _seq_tile(x_ref, m_ref):
    return jnp.where(pl.program_id(0) == 0, m_ref[...], x_ref[...])


def in_proj_fwd(x, mpad, w_in, nm, qn_t, kn_t, bd):
    lp = x.shape[0] + PAD

    def body(x_ref, m_ref, w_ref, nm_ref, qn_ref, kn_ref, bd_ref,
             xn_o, qr_o, kr_o, qn_o, kn_o, v_o, u_o, gg_o):
        h = _seq_tile(x_ref, m_ref)
        r = lax.rsqrt(jnp.mean(h * h, axis=-1, keepdims=True) + EPS)
        xb = ((h * r) * nm_ref[...]).astype(BF)
        xn_o[...] = xb
        bdv = bd_ref[...]
        q = _dot(xb, w_ref[:, 0:QW])
        qr_o[...] = q
        qn_o[...] = (q * lax.rsqrt(_seg_mean(q * q, bdv, False) + EPS) * qn_ref[...]).astype(BF)
        k = _dot(xb, w_ref[:, QW:QW + KVW])
        kr_o[...] = k
        kn_o[...] = (k * lax.rsqrt(_seg_mean(k * k, bdv, False) + EPS) * kn_ref[...]).astype(BF)
        v_o[...] = _dot(xb, w_ref[:, QW + KVW:QW + 2 * KVW]).astype(BF)
        u_o[...] = _dot(xb, w_ref[:, QW + 2 * KVW:QW + 2 * KVW + D])
        gg_o[...] = _dot(xb, w_ref[:, QW + 2 * KVW + D:IN_COLS])

    outs = [(D, BF), (QW, F32), (KVW, F32), (QW, BF), (KVW, BF), (KVW, BF), (D, F32), (2 * D, F32)]
    return pl.pallas_call(
        body, name="in_proj_fwd", grid=(lp // TM,),
        in_specs=_seq_specs() + [_full((D, IN_COLS)), _full((1, D)), _full((1, QW)), _full((1, KVW)),
                                 _full((256, 256))],
        out_specs=[_rows(TM, w) for w, _ in outs],
        out_shape=[jax.ShapeDtypeStruct((lp, w), t) for w, t in outs],
        compiler_params=_cp(("arbitrary",)),
    )(x, mpad, w_in, nm, qn_t, kn_t, bd)


def in_proj_bwd(dqn, dkn, dv, du, dgg, qr, kr, x, mpad, dh1, w_in, nm, qn_t, kn_t, bd):
    lp = x.shape[0] + PAD

    def body(dqn_ref, dkn_ref, dv_ref, du_ref, dgg_ref, qr_ref, kr_ref, x_ref, m_ref, dh1_ref,
             w_ref, nm_ref, qn_ref, kn_ref, bd_ref,
             dproj_o, dx_o, dm_o, dnm_o, dqw_o, dkw_o):
        i = pl.program_id(0)

        @pl.when(i == 0)
        def _():
            dnm_o[...] = jnp.zeros_like(dnm_o)
            dqw_o[...] = jnp.zeros_like(dqw_o)
            dkw_o[...] = jnp.zeros_like(dkw_o)

        bdv = bd_ref[...]

        def norm_bwd(x, dy, g, acc_ref):
            rr = lax.rsqrt(_seg_mean(x * x, bdv, False) + EPS)
            xh = x * rr
            acc_ref[...] += jnp.sum(dy * xh, axis=0, keepdims=True)
            dxh = dy * g
            return rr * (dxh - xh * _seg_mean(dxh * xh, bdv))

        dq = norm_bwd(qr_ref[...], dqn_ref[...], qn_ref[...], dqw_o)
        dk = norm_bwd(kr_ref[...], dkn_ref[...], kn_ref[...], dkw_o)
        dproj_o[:, 0:QW] = dq.astype(BF)
        dproj_o[:, QW:QW + KVW] = dk.astype(BF)
        dproj_o[:, QW + KVW:QW + 2 * KVW] = dv_ref[...].astype(BF)
        dproj_o[:, QW + 2 * KVW:QW + 2 * KVW + D] = du_ref[...]
        dproj_o[:, QW + 2 * KVW + D:IN_COLS] = dgg_ref[...]
        dxn = _dot_nt(dproj_o[...], w_ref[...])
        h = _seq_tile(x_ref, m_ref)
        r = lax.rsqrt(jnp.mean(h * h, axis=-1, keepdims=True) + EPS)
        xh = h * r
        dnm_o[...] += jnp.sum(dxn * xh, axis=0, keepdims=True)
        dxh = dxn * nm_ref[...]
        dh0 = dh1_ref[...] + r * (dxh - xh * jnp.mean(dxh * xh, axis=-1, keepdims=True))
        dx_o[...] = dh0

        @pl.when(i == 0)
        def _():
            dm_o[...] = dh0

    return pl.pallas_call(
        body, name="in_proj_bwd", grid=(lp // TM,),
        in_specs=[_rows(TM, QW), _rows(TM, KVW), _rows(TM, KVW), _rows(TM, D), _rows(TM, 2 * D),
                  _rows(TM, QW), _rows(TM, KVW)] + _seq_specs() + [_rows(TM, D),
                  _full((D, IN_COLS)), _full((1, D)), _full((1, QW)), _full((1, KVW)), _full((256, 256))],
        out_specs=[_rows(TM, IN_COLS), _seq_specs()[0], _full((PAD, D)), _full((1, D)), _full((1, QW)),
                   _full((1, KVW))],
        out_shape=[jax.ShapeDtypeStruct((lp, IN_COLS), BF), jax.ShapeDtypeStruct((lp - PAD, D), F32),
                   jax.ShapeDtypeStruct((PAD, D), F32),
                   jax.ShapeDtypeStruct((1, D), F32), jax.ShapeDtypeStruct((1, QW), F32),
                   jax.ShapeDtypeStruct((1, KVW), F32)],
        compiler_params=_cp(("arbitrary",)),
    )(dqn, dkn, dv, du, dgg, qr, kr, x, mpad, dh1, w_in, nm, qn_t, kn_t, bd)


def _att_bias(b):
    qi = lax.broadcasted_iota(jnp.int32, (WIN, WIN), 0)
    kj = lax.broadcasted_iota(jnp.int32, (WIN, WIN), 1)
    upper = kj > qi
    valid = (upper & ((b - 1) * WIN + kj >= PAD)) | (jnp.logical_not(upper) & (b * WIN + kj >= META0))
    bias_pc = jnp.where(valid, 0.0, NEG)
    kj1 = lax.broadcasted_iota(jnp.int32, (1, WIN), 1)
    bias_m = jnp.where((kj1 >= WIN - N_META) & (b * WIN >= PAD), 0.0, NEG)
    return bias_m, bias_pc, upper


def _dup_half(ref, kp, pos, lo):
    pair = ref[:, 128 * kp:128 * kp + 128].astype(F32)
    rolled = pltpu.roll(pair, 64, axis=1)
    keep = lo if pos == 0 else jnp.logical_not(lo)
    return jnp.where(keep, pair, rolled).astype(BF)


def _stack_heads(ref, kv, lo):
    parts = []
    for pp in range(2):
        pair = ref[:, 256 * kv + 128 * pp:256 * kv + 128 * pp + 128]
        zero = jnp.zeros_like(pair)
        parts.append(jnp.where(lo, pair, zero))
        parts.append(jnp.where(lo, zero, pair))
    return jnp.concatenate(parts, axis=0)


def _unstack_heads(x4, lo):
    a = jnp.where(lo, x4[0:WIN], x4[WIN:2 * WIN])
    b = jnp.where(lo, x4[2 * WIN:3 * WIN], x4[3 * WIN:4 * WIN])
    return a, b


def _split_pc(x, upper):
    zero = jnp.zeros_like(x)
    return jnp.where(upper, x, zero).astype(BF), jnp.where(upper, zero, x).astype(BF)


def _head_softmax(s_m, s_p, s_c, bias, sink):
    bias_m, bias_pc, upper = bias
    sm = s_m + bias_m
    spc = jnp.where(upper, s_p, s_c) + bias_pc
    m = jnp.maximum(jnp.max(jnp.maximum(sm, spc), axis=-1, keepdims=True), sink)
    em, epc = jnp.exp(sm - m), jnp.exp(spc - m)
    esink = jnp.exp(sink - m)
    inv = 1.0 / (esink + jnp.sum(em + epc, axis=-1, keepdims=True))
    return em, epc, inv, esink


def _kv_specs():
    return [pl.BlockSpec((WIN, KVW), lambda b: (1, 0)),
            pl.BlockSpec((WIN, KVW), lambda b: (jnp.maximum(b - 1, 0), 0)),
            pl.BlockSpec((WIN, KVW), lambda b: (b, 0))]


def attention_fwd(sinks, qn, kn, v):
    lp = qn.shape[0]

    def body(sink_ref, q_ref, km_ref, kp_ref, kc_ref, vm_ref, vp_ref, vc_ref, o_ref):
        b = pl.program_id(0)
        bias = _att_bias(b)
        lo = lax.broadcasted_iota(jnp.int32, (WIN, WIN), 1) < HEAD
        for kv in range(NKV):
            kp, pos = kv // 2, kv % 2
            kds = [_dup_half(r, kp, pos, lo) for r in (km_ref, kp_ref, kc_ref)]
            vds = [_dup_half(r, kp, pos, lo) for r in (vm_ref, vp_ref, vc_ref)]
            q4 = _stack_heads(q_ref, kv, lo)
            ss = [_dot_nt(q4, kd) for kd in kds]
            es, invs = ([], [], []), []
            for h in range(4):
                rs = slice(WIN * h, WIN * h + WIN)
                em, epc, inv, _ = _head_softmax(ss[0][rs], ss[1][rs], ss[2][rs], bias, sink_ref[4 * kv + h])
                e_p, e_c = _split_pc(epc, bias[2])
                for lst, e in zip(es, (em.astype(BF), e_p, e_c)):
                    lst.append(e)
                invs.append(inv)
            e_m, e_p, e_c = (jnp.concatenate(lst, axis=0) for lst in es)
            o4 = (_dot(e_m, vds[0]) + _dot(e_p, vds[1]) + _dot(e_c, vds[2])) * jnp.concatenate(invs, axis=0)
            oa, ob = _unstack_heads(o4, lo)
            o_ref[:, 256 * kv:256 * kv + 128] = oa
            o_ref[:, 256 * kv + 128:256 * kv + 256] = ob

    return pl.pallas_call(
        body, name="attention_fwd", grid=(lp // WIN,),
        in_specs=[pl.BlockSpec(memory_space=pltpu.SMEM), _rows(WIN, QW)] + _kv_specs() + _kv_specs(),
        out_specs=_rows(WIN, QW),
        out_shape=jax.ShapeDtypeStruct((lp, QW), F32),
        compiler_params=_cp(("arbitrary",)),
    )(sinks, qn, kn, kn, kn, v, v, v)


def attention_bwd(sinks, qn, kn, v, do):
    lp = qn.shape[0]
    nb = lp // WIN

    def body(sink_ref, q_ref, km_ref, kp_ref, kc_ref, vm_ref, vp_ref, vc_ref, do_ref,
             dq_ref, dk_hbm, dv_hbm, dsink_ref, dk_acc, dv_acc):
        b = pl.program_id(0)

        @pl.when(b == 0)
        def _():
            dk_acc[...] = jnp.zeros_like(dk_acc)
            dv_acc[...] = jnp.zeros_like(dv_acc)
            dsink_ref[...] = jnp.zeros_like(dsink_ref)

        bias = _att_bias(b)
        upper = bias[2]
        lo = lax.broadcasted_iota(jnp.int32, (WIN, WIN), 1) < HEAD
        lane8 = lax.broadcasted_iota(jnp.int32, (8, 128), 1)
        starts = [WIN, pl.multiple_of(jnp.maximum(b - 1, 0) * WIN, WIN), pl.multiple_of(b * WIN, WIN)]
        for kv in range(NKV):
            kp, pos = kv // 2, kv % 2
            keep = lo if pos == 0 else jnp.logical_not(lo)
            kds = [_dup_half(r, kp, pos, lo) for r in (km_ref, kp_ref, kc_ref)]
            vds = [_dup_half(r, kp, pos, lo) for r in (vm_ref, vp_ref, vc_ref)]
            q4 = _stack_heads(q_ref, kv, lo)
            do4 = _stack_heads(do_ref, kv, lo)
            ss = [_dot_nt(q4, kd) for kd in kds]
            dps = [_dot_nt(do4, vd) for vd in vds]
            ds_l, p_l = ([], [], []), ([], [], [])
            for h in range(4):
                rs = slice(WIN * h, WIN * h + WIN)
                em, epc, inv, esink = _head_softmax(ss[0][rs], ss[1][rs], ss[2][rs], bias, sink_ref[4 * kv + h])
                p_m, p_pc = em * inv, epc * inv
                dp_m = dps[0][rs]
                dp_pc = jnp.where(upper, dps[1][rs], dps[2][rs])
                delta = jnp.sum(p_m * dp_m + p_pc * dp_pc, axis=-1, keepdims=True)
                val = -jnp.sum(esink * inv * delta, axis=0, keepdims=True)
                dsink_ref[...] += jnp.where(lane8 == 4 * kv + h, val, 0.0)
                for lst, t in zip(ds_l, ((p_m * (dp_m - delta)).astype(BF),) + _split_pc(p_pc * (dp_pc - delta), upper)):
                    lst.append(t)
                for lst, t in zip(p_l, (p_m.astype(BF),) + _split_pc(p_pc, upper)):
                    lst.append(t)
            dss = [jnp.concatenate(lst, axis=0) for lst in ds_l]
            ps = [jnp.concatenate(lst, axis=0) for lst in p_l]
            dq4 = _dot(dss[0], kds[0]) + _dot(dss[1], kds[1]) + _dot(dss[2], kds[2])
            dqa, dqb = _unstack_heads(dq4, lo)
            dq_ref[:, 256 * kv:256 * kv + 128] = dqa
            dq_ref[:, 256 * kv + 128:256 * kv + 256] = dqb
            for x in range(3):
                dkd = _dot_tn(dss[x], q4)
                dvd = _dot_tn(ps[x], do4)
                dkd = jnp.where(keep, dkd + pltpu.roll(dkd, 64, axis=1), 0.0)
                dvd = jnp.where(keep, dvd + pltpu.roll(dvd, 64, axis=1), 0.0)
                dk_acc[pl.ds(starts[x], WIN), 128 * kp:128 * kp + 128] += dkd
                dv_acc[pl.ds(starts[x], WIN), 128 * kp:128 * kp + 128] += dvd

        @pl.when(b == nb - 1)
        def _():
            pltpu.sync_copy(dk_acc, dk_hbm)
            pltpu.sync_copy(dv_acc, dv_hbm)

    return pl.pallas_call(
        body, name="attention_bwd", grid=(nb,),
        in_specs=[pl.BlockSpec(memory_space=pltpu.SMEM), _rows(WIN, QW)] + _kv_specs() + _kv_specs()
                 + [_rows(WIN, QW)],
        out_specs=[_rows(WIN, QW), ANY, ANY, _full((8, 128))],
        out_shape=[jax.ShapeDtypeStruct((lp, QW), F32), jax.ShapeDtypeStruct((lp, KVW), F32),
                   jax.ShapeDtypeStruct((lp, KVW), F32), jax.ShapeDtypeStruct((8, 128), F32)],
        scratch_shapes=[pltpu.VMEM((lp, KVW), F32), pltpu.VMEM((lp, KVW), F32)],
        compiler_params=_cp(("arbitrary",)),
    )(sinks, qn, kn, kn, kn, v, v, v, do)


PW_ROWS = 64
SEG = TS // 8


def _segment_order():
    rho = jnp.arange(TS)
    token = SEG * (rho % 8) + rho // 8
    p = (token[:, None] == jnp.arange(TS)[None, :]).astype(BF)
    return p, p.T


def _reorder(p, x):
    hi = x.astype(BF)
    lo = (x - hi.astype(F32)).astype(BF)
    return _dot(p, hi) + _dot(p, lo)


def _segment_scan(xr_ref, xi_ref, pw_ref, base, ls, c_r, c_i, reverse, visit=None):
    ar, ai = pw_ref[base:base + 8, ls], pw_ref[base + 8:base + 16, ls]
    order = list(range(SEG - 1, -1, -1)) if reverse else list(range(SEG))
    s_r = s_i = jnp.zeros_like(ar)
    for i in order:
        rows = pl.ds(8 * i, 8)
        s_r, s_i = ar * s_r - ai * s_i + xr_ref[rows, ls], ar * s_i + ai * s_r + xi_ref[rows, ls]
    rid = lax.broadcasted_iota(jnp.int32, s_r.shape, 0)
    if reverse:
        e_r = jnp.where(rid < 7, pltpu.roll(s_r, 7, axis=0), c_r)
        e_i = jnp.where(rid < 7, pltpu.roll(s_i, 7, axis=0), c_i)
    else:
        e_r = jnp.where(rid >= 1, pltpu.roll(s_r, 1, axis=0), c_r)
        e_i = jnp.where(rid >= 1, pltpu.roll(s_i, 1, axis=0), c_i)
    for n, k in enumerate((1, 2, 4)):
        shift = 8 - k if reverse else k
        t_r, t_i = pltpu.roll(e_r, shift, axis=0), pltpu.roll(e_i, shift, axis=0)
        kr = pw_ref[base + 16 + 16 * n:base + 24 + 16 * n, ls]
        ki = pw_ref[base + 24 + 16 * n:base + 32 + 16 * n, ls]
        e_r, e_i = e_r + kr * t_r - ki * t_i, e_i + kr * t_i + ki * t_r
    last = 0 if reverse else 7
    a16r, a16i = pw_ref[base + 16 + last:base + 17 + last, ls], pw_ref[base + 24 + last:base + 25 + last, ls]
    lr, li, fr, fi = (t[last:last + 1] for t in (e_r, e_i, s_r, s_i))
    out = (a16r * lr - a16i * li + fr, a16r * li + a16i * lr + fi)
    s_r, s_i = e_r, e_i
    extra = None
    for i in order:
        rows = pl.ds(8 * i, 8)
        s_r, s_i = ar * s_r - ai * s_i + xr_ref[rows, ls], ar * s_i + ai * s_r + xi_ref[rows, ls]
        xr_ref[rows, ls] = s_r
        xi_ref[rows, ls] = s_i
        if visit is not None:
            extra = visit(i, s_r, s_i, extra)
    return out if visit is None else (out, extra)


def ssm_fwd(u, pw, br, bi, cr, ci, dsk, perm):
    lp = u.shape[0]

    def body(u_ref, pw_ref, br_ref, bi_ref, cr_ref, ci_ref, d_ref, p_ref, pt_ref,
             y_ref, sr_ref, si_ref, car_r, car_i, yp_s):
        i = pl.program_id(0)

        @pl.when(i == 0)
        def _():
            car_r[...] = jnp.zeros_like(car_r)
            car_i[...] = jnp.zeros_like(car_i)

        u = u_ref[...]
        ub = _dot(p_ref[...], u.astype(BF)).astype(BF)
        for j in range(NLC):
            uj = ub[:, 128 * j:128 * j + 128]
            sr_ref[:, LC * j:LC * j + LC] = _dot(uj, br_ref[j])
            si_ref[:, LC * j:LC * j + LC] = _dot(uj, bi_ref[j])
        for j in range(NLC):
            ls = slice(LC * j, LC * j + LC)
            car_r[:, ls], car_i[:, ls] = _segment_scan(sr_ref, si_ref, pw_ref, 0, ls, car_r[:, ls], car_i[:, ls],
                                                       False)
        for j in range(NLC):
            ls = slice(LC * j, LC * j + LC)
            yp_s[:, 128 * j:128 * j + 128] = (_dot(sr_ref[:, ls].astype(BF), cr_ref[j])
                                              - _dot(si_ref[:, ls].astype(BF), ci_ref[j]))
        y_ref[...] = _reorder(pt_ref[...], yp_s[...]) + d_ref[...] * u

    p, pt = perm
    return pl.pallas_call(
        body, name="ssm_fwd", grid=(lp // TS,),
        in_specs=[_rows(TS, D), _full((2 * PW_ROWS, SW)), _full((NLC, 128, LC)), _full((NLC, 128, LC)),
                  _full((NLC, LC, 128)), _full((NLC, LC, 128)), _full((1, D)), _full((TS, TS)), _full((TS, TS))],
        out_specs=[_rows(TS, D), _rows(TS, SW), _rows(TS, SW)],
        out_shape=[jax.ShapeDtypeStruct((lp, D), F32), jax.ShapeDtypeStruct((lp, SW), F32),
                   jax.ShapeDtypeStruct((lp, SW), F32)],
        scratch_shapes=[pltpu.VMEM((1, SW), F32), pltpu.VMEM((1, SW), F32), pltpu.VMEM((TS, D), F32)],
        compiler_params=_cp(("arbitrary",)),
    )(u, pw, br, bi, cr, ci, dsk, p, pt)


def ssm_bwd(dy, u, sr, si, pw, br, bi, cr, ci, dsk, perm):
    lp = u.shape[0]
    nt = lp // TS

    def rev(i):
        return (nt - 1 - i, 0)

    def prev8(i):
        return (jnp.maximum((nt - 1 - i) * (TS // 8) - 1, 0), 0)

    def body(dy_ref, u_ref, sr_ref, si_ref, pr8_ref, pi8_ref, pw_ref, br_ref, bi_ref, cr_ref, ci_ref, d_ref,
             p_ref, pt_ref, du_ref, da_ref, dbr_ref, dbi_ref, dcr_ref, dci_ref, dd_ref,
             gr_s, gi_s, car_r, car_i, dup_s):
        i = pl.program_id(0)

        @pl.when(i == 0)
        def _():
            car_r[...] = jnp.zeros_like(car_r)
            car_i[...] = jnp.zeros_like(car_i)
            for ref in (da_ref, dbr_ref, dbi_ref, dcr_ref, dci_ref, dd_ref):
                ref[...] = jnp.zeros_like(ref)

        keep = 1.0 - (i == nt - 1).astype(F32)
        dy = dy_ref[...]
        u = u_ref[...]
        dd_ref[...] += jnp.sum(dy * u, axis=0, keepdims=True)
        pm = p_ref[...]
        dyb = _dot(pm, dy.astype(BF)).astype(BF)
        ub = _dot(pm, u.astype(BF)).astype(BF)
        for j in range(NLC):
            ls = slice(LC * j, LC * j + LC)
            dyj = dyb[:, 128 * j:128 * j + 128]
            gr_s[:, ls] = _dot_nt(dyj, cr_ref[j])
            gi_s[:, ls] = -_dot_nt(dyj, ci_ref[j])
            dcr_ref[j] += _dot_tn(dyj, sr_ref[:, ls].astype(BF))
            dci_ref[j] -= _dot_tn(dyj, si_ref[:, ls].astype(BF))
        rid = lax.broadcasted_iota(jnp.int32, (8, LC), 0)
        for j in range(NLC):
            ls = slice(LC * j, LC * j + LC)

            def accumulate(i, g_r, g_i, acc):
                if i >= 1:
                    rows = pl.ds(8 * (i - 1), 8)
                    p_r, p_i = sr_ref[rows, ls], si_ref[rows, ls]
                else:
                    rows = pl.ds(8 * (SEG - 1), 8)
                    p_r = jnp.where(rid >= 1, pltpu.roll(sr_ref[rows, ls], 1, axis=0), pr8_ref[7:8, ls] * keep)
                    p_i = jnp.where(rid >= 1, pltpu.roll(si_ref[rows, ls], 1, axis=0), pi8_ref[7:8, ls] * keep)
                t_r, t_i = g_r * p_r + g_i * p_i, g_i * p_r - g_r * p_i
                return (t_r, t_i) if acc is None else (acc[0] + t_r, acc[1] + t_i)

            (car_r[:, ls], car_i[:, ls]), (t_r, t_i) = _segment_scan(
                gr_s, gi_s, pw_ref, PW_ROWS, ls, car_r[:, ls], car_i[:, ls], True, accumulate)
            da_ref[0:8, ls] += t_r
            da_ref[8:16, ls] += t_i
        for j in range(NLC):
            ls = slice(LC * j, LC * j + LC)
            uj = ub[:, 128 * j:128 * j + 128]
            grb = gr_s[:, ls].astype(BF)
            gib = gi_s[:, ls].astype(BF)
            dup_s[:, 128 * j:128 * j + 128] = _dot_nt(grb, br_ref[j]) + _dot_nt(gib, bi_ref[j])
            dbr_ref[j] += _dot_tn(uj, grb)
            dbi_ref[j] += _dot_tn(uj, gib)
        du_ref[...] = (_reorder(pt_ref[...], dup_s[...]) + d_ref[...] * dy).astype(BF)

    p, pt = perm
    return pl.pallas_call(
        body, name="ssm_bwd", grid=(nt,),
        in_specs=[pl.BlockSpec((TS, D), rev), pl.BlockSpec((TS, D), rev), pl.BlockSpec((TS, SW), rev),
                  pl.BlockSpec((TS, SW), rev), pl.BlockSpec((8, SW), prev8), pl.BlockSpec((8, SW), prev8),
                  _full((2 * PW_ROWS, SW)), _full((NLC, 128, LC)), _full((NLC, 128, LC)),
                  _full((NLC, LC, 128)), _full((NLC, LC, 128)), _full((1, D)), _full((TS, TS)), _full((TS, TS))],
        out_specs=[pl.BlockSpec((TS, D), rev), _full((16, SW)), _full((NLC, 128, LC)), _full((NLC, 128, LC)),
                   _full((NLC, 128, LC)), _full((NLC, 128, LC)), _full((1, D))],
        out_shape=[jax.ShapeDtypeStruct((lp, D), BF), jax.ShapeDtypeStruct((16, SW), F32),
                   jax.ShapeDtypeStruct((NLC, 128, LC), F32), jax.ShapeDtypeStruct((NLC, 128, LC), F32),
                   jax.ShapeDtypeStruct((NLC, 128, LC), F32), jax.ShapeDtypeStruct((NLC, 128, LC), F32),
                   jax.ShapeDtypeStruct((1, D), F32)],
        scratch_shapes=[pltpu.VMEM((TS, SW), F32), pltpu.VMEM((TS, SW), F32),
                        pltpu.VMEM((1, SW), F32), pltpu.VMEM((1, SW), F32), pltpu.VMEM((TS, D), F32)],
        compiler_params=_cp(("arbitrary",)),
    )(dy, u, sr, si, sr, si, pw, br, bi, cr, ci, dsk, p, pt)


def _merge_parts(attn, zz, gg, abn, sbn):
    za, zb = zz[:, 0:D], zz[:, D:2 * D]
    sgz = _sigmoid(zb)
    ssm = za * sgz
    ra = lax.rsqrt(jnp.mean(attn * attn, axis=-1, keepdims=True) + EPS)
    rs = lax.rsqrt(jnp.mean(ssm * ssm, axis=-1, keepdims=True) + EPS)
    ah, sh = attn * ra, ssm * rs
    sga, sgs = _sigmoid(gg[:, 0:D]), _sigmoid(gg[:, D:2 * D])
    return za, sgz, ra, rs, ah, sh, sga, sgs, ah * abn, sh * sbn


def mid_fwd(y, attn, gg, x, mpad, w_glu, w_out, abn, sbn):
    lp = y.shape[0]

    def body(y_ref, a_ref, gg_ref, x_ref, m_ref, wg_ref, wo_ref, abn_ref, sbn_ref, z_o, zz_o, mg_o, h1_o):
        z, _ = _gelu(y_ref[...])
        zb = z.astype(BF)
        z_o[...] = zb
        zz = _dot(zb, wg_ref[...])
        zz_o[...] = zz.astype(BF)
        _, _, _, _, _, _, sga, sgs, an, sn = _merge_parts(a_ref[...], zz, gg_ref[...], abn_ref[...], sbn_ref[...])
        mb = (sga * an + sgs * sn).astype(BF)
        mg_o[...] = mb
        h1_o[...] = _seq_tile(x_ref, m_ref) + _dot(mb, wo_ref[...])

    return pl.pallas_call(
        body, name="mid_fwd", grid=(lp // TM,),
        in_specs=[_rows(TM, D), _rows(TM, D), _rows(TM, 2 * D)] + _seq_specs() + [_full((D, 2 * D)), _full((D, D)),
                  _full((1, D)), _full((1, D))],
        out_specs=[_rows(TM, D), _rows(TM, 2 * D), _rows(TM, D), _rows(TM, D)],
        out_shape=[jax.ShapeDtypeStruct((lp, D), BF), jax.ShapeDtypeStruct((lp, 2 * D), BF),
                   jax.ShapeDtypeStruct((lp, D), BF), jax.ShapeDtypeStruct((lp, D), F32)],
        compiler_params=_cp(("arbitrary",)),
    )(y, attn, gg, x, mpad, w_glu, w_out, abn, sbn)


def mid_bwd(dh1b, y, attn, gg, zzb, w_glu, w_out, abn, sbn):
    lp = y.shape[0]

    def body(dh_ref, y_ref, a_ref, gg_ref, zz_ref, wg_ref, wo_ref, abn_ref, sbn_ref,
             dy_o, dattn_o, dgg_o, dzz_o, dabn_o, dsbn_o):
        i = pl.program_id(0)

        @pl.when(i == 0)
        def _():
            dabn_o[...] = jnp.zeros_like(dabn_o)
            dsbn_o[...] = jnp.zeros_like(dsbn_o)

        dm = _dot_nt(dh_ref[...], wo_ref[...])
        abn, sbn = abn_ref[...], sbn_ref[...]
        za, sgz, ra, rs, ah, sh, sga, sgs, an, sn = _merge_parts(
            a_ref[...], zz_ref[...].astype(F32), gg_ref[...], abn, sbn)
        da = dm * sga
        dabn_o[...] += jnp.sum(da * ah, axis=0, keepdims=True)
        dah = da * abn
        pa = dah * ah
        dgg_o[:, 0:D] = (pa * (1.0 - sga)).astype(BF)
        dattn_o[...] = (ra * (dah - ah * jnp.mean(pa, axis=-1, keepdims=True))).astype(BF)
        ds = dm * sgs
        dsbn_o[...] += jnp.sum(ds * sh, axis=0, keepdims=True)
        dsh = ds * sbn
        ps = dsh * sh
        dgg_o[:, D:2 * D] = (ps * (1.0 - sgs)).astype(BF)
        dssm = rs * (dsh - sh * jnp.mean(ps, axis=-1, keepdims=True))
        dza = dssm * sgz
        dzz_o[:, 0:D] = dza.astype(BF)
        dzz_o[:, D:2 * D] = (dza * za * (1.0 - sgz)).astype(BF)
        dz = _dot_nt(dzz_o[...], wg_ref[...])
        yv = y_ref[...]
        _, t = _gelu(yv)
        dy_o[...] = dz * _gelu_grad(yv, t)

    return pl.pallas_call(
        body, name="mid_bwd", grid=(lp // TM,),
        in_specs=[_rows(TM, D), _rows(TM, D), _rows(TM, D), _rows(TM, 2 * D), _rows(TM, 2 * D),
                  _full((D, 2 * D)), _full((D, D)), _full((1, D)), _full((1, D))],
        out_specs=[_rows(TM, D), _rows(TM, D), _rows(TM, 2 * D), _rows(TM, 2 * D), _full((1, D)), _full((1, D))],
        out_shape=[jax.ShapeDtypeStruct((lp, D), F32), jax.ShapeDtypeStruct((lp, D), BF),
                   jax.ShapeDtypeStruct((lp, 2 * D), BF), jax.ShapeDtypeStruct((lp, 2 * D), BF),
                   jax.ShapeDtypeStruct((1, D), F32), jax.ShapeDtypeStruct((1, D), F32)],
        compiler_params=_cp(("arbitrary",)),
    )(dh1b, y, attn, gg, zzb, w_glu, w_out, abn, sbn)


def ffn_fwd_bwd(h1, tgt, w_fi, w_fo, gf):
    lp = h1.shape[0]
    tf = 256
    bounds = (0, 1536, DFF)

    def body(h_ref, t_ref, wi_ref, wo_ref, gf_ref, hn_o, dgu_o, act_o, dh2_o, dh1_o, dh1b_o, dgf_o, loss_o,
             gate_s, up_s):
        i = pl.program_id(0)

        @pl.when(i == 0)
        def _():
            dgf_o[...] = jnp.zeros_like(dgf_o)
            loss_o[...] = jnp.zeros_like(loss_o)

        h = h_ref[...]
        r = lax.rsqrt(jnp.mean(h * h, axis=-1, keepdims=True) + EPS)
        xh = h * r
        hb = (xh * gf_ref[...]).astype(BF)
        hn_o[...] = hb
        h2 = h
        for c0, c1 in zip(bounds[:-1], bounds[1:]):
            gate = _dot(hb, wi_ref[:, c0:c1])
            up = _dot(hb, wi_ref[:, DFF + c0:DFF + c1])
            gate_s[:, c0:c1] = gate
            up_s[:, c0:c1] = up
            ab = (gate * _sigmoid(gate) * up).astype(BF)
            act_o[:, c0:c1] = ab
            h2 = h2 + _dot(ab, wo_ref[c0:c1, :])
        real = (i >= PAD // tf).astype(F32)
        e = (h2 - t_ref[...]) * real
        loss_o[...] += 0.5 * jnp.sum(jnp.sum(e * e, axis=-1, keepdims=True), axis=0, keepdims=True) * (1.0 / D)
        dh2 = e * (1.0 / D)
        db = dh2.astype(BF)
        dh2_o[...] = db
        dhn = jnp.zeros((tf, D), F32)
        for c0, c1 in zip(bounds[:-1], bounds[1:]):
            gate = gate_s[:, c0:c1]
            up = up_s[:, c0:c1]
            sg = _sigmoid(gate)
            dact = _dot_nt(db, wo_ref[c0:c1, :])
            dgate = (dact * up * (sg * (1.0 + gate * (1.0 - sg)))).astype(BF)
            dup = (dact * (gate * sg)).astype(BF)
            dgu_o[:, c0:c1] = dgate
            dgu_o[:, DFF + c0:DFF + c1] = dup
            dhn += _dot_nt(dgate, wi_ref[:, c0:c1]) + _dot_nt(dup, wi_ref[:, DFF + c0:DFF + c1])
        dgf_o[...] += jnp.sum(dhn * xh, axis=0, keepdims=True)
        dxh = dhn * gf_ref[...]
        dh1 = dh2 + r * (dxh - xh * jnp.mean(dxh * xh, axis=-1, keepdims=True))
        dh1_o[...] = dh1
        dh1b_o[...] = dh1.astype(BF)

    return pl.pallas_call(
        body, name="ffn_fwd_bwd", grid=(lp // tf,),
        in_specs=[_rows(tf, D), pl.BlockSpec((tf, D), lambda i: (jnp.maximum(i - PAD // tf, 0), 0)),
                  _const((D, 2 * DFF)), _const((DFF, D)), _full((1, D))],
        out_specs=[_rows(tf, D), _rows(tf, 2 * DFF), _rows(tf, DFF), _rows(tf, D), _rows(tf, D), _rows(tf, D),
                   _full((1, D)), _full((1, 1))],
        out_shape=[jax.ShapeDtypeStruct((lp, D), BF), jax.ShapeDtypeStruct((lp, 2 * DFF), BF),
                   jax.ShapeDtypeStruct((lp, DFF), BF), jax.ShapeDtypeStruct((lp, D), BF),
                   jax.ShapeDtypeStruct((lp, D), F32), jax.ShapeDtypeStruct((lp, D), BF),
                   jax.ShapeDtypeStruct((1, D), F32), jax.ShapeDtypeStruct((1, 1), F32)],
        scratch_shapes=[pltpu.VMEM((tf, DFF), F32), pltpu.VMEM((tf, DFF), F32)],
        compiler_params=_cp(("arbitrary",), vmem_mb=58),
    )(h1, tgt, w_fi, w_fo, gf)


def weight_grad(name, a, b, tn, with_bf16=False):
    lp, k = a.shape
    n = b.shape[1]
    tr = next(t for t in (2112, 1056, 768, 512, 384, 256, 128) if lp % t == 0 and t * k * 2 <= (5 << 20))

    def body(a_ref, b_ref, o_ref, *ob_ref):
        @pl.when(pl.program_id(1) == 0)
        def _():
            o_ref[...] = jnp.zeros_like(o_ref)

        o_ref[...] += _dot_tn(a_ref[...], b_ref[...])
        if with_bf16:
            @pl.when(pl.program_id(1) == lp // tr - 1)
            def _():
                ob_ref[0][...] = o_ref[...].astype(BF)

    spec = pl.BlockSpec((k, tn), lambda j, m: (0, j))
    return pl.pallas_call(
        body, name=name, grid=(n // tn, lp // tr),
        in_specs=[pl.BlockSpec((tr, k), lambda j, m: (m, 0)), pl.BlockSpec((tr, tn), lambda j, m: (m, j))],
        out_specs=[spec, spec] if with_bf16 else spec,
        out_shape=([jax.ShapeDtypeStruct((k, n), F32), jax.ShapeDtypeStruct((k, n), BF)] if with_bf16
                   else jax.ShapeDtypeStruct((k, n), F32)),
        compiler_params=_cp(("arbitrary", "arbitrary")),
    )(a, b)


def _adamw_math(w, g, m, v):
    m = B1 * m + (1.0 - B1) * g
    v = B2 * v + (1.0 - B2) * (g * g)
    delta = -LR * ((m / BC1) / (jnp.sqrt(v / BC2) + AEPS) + WD * w)
    return delta, m, v


def _row_tile(r):
    return next((t for t in (256, 128, 64, 32, 16, 8) if r % t == 0), r)


def adamw(name, w, g, m, v):
    r, c = w.shape
    tr = _row_tile(r)

    def body(w_ref, g_ref, m_ref, v_ref, g_o, d_o, m_o, v_o):
        g = g_ref[...]
        g_o[...] = g
        d_o[...], m_o[...], v_o[...] = _adamw_math(w_ref[...], g, m_ref[...], v_ref[...])

    spec = pl.BlockSpec((tr, c), lambda i: (i, 0))
    return pl.pallas_call(
        body, name=name, grid=(r // tr,), in_specs=[spec] * 4, out_specs=[spec] * 4,
        out_shape=[jax.ShapeDtypeStruct((r, c), F32)] * 4,
        compiler_params=_cp(("arbitrary",)),
    )(w, g, m, v)


def adamw_small(ws, gs, ms, vs):
    n = len(ws)
    steps = 8

    def spec(a):
        if a.ndim == 4:
            return pl.BlockSpec((1, a.shape[1] // steps) + a.shape[2:], lambda i: (0, i, 0, 0))
        nd = a.ndim
        return pl.BlockSpec(a.shape, lambda i: (0,) * nd)

    def body(*refs):
        w, g, m, v, d_o, m_o, v_o = (refs[k * n:(k + 1) * n] for k in range(7))
        for t in range(n):
            d_o[t][...], m_o[t][...], v_o[t][...] = _adamw_math(w[t][...], g[t][...], m[t][...], v[t][...])

    specs = [spec(a) for a in ws]
    res = pl.pallas_call(
        body, name="adamw_small", grid=(steps,), in_specs=specs * 4, out_specs=specs * 3,
        out_shape=[jax.ShapeDtypeStruct(a.shape, F32) for a in ws] * 3,
        compiler_params=_cp(("arbitrary",)),
    )(*ws, *gs, *ms, *vs)
    return res[:n], res[n:2 * n], res[2 * n:]


def pair_sum(name, place, full, got, ax):
    r, c = got.shape
    tr = _row_tile(r)

    def body(s_ref, a_ref, b_ref, o_ref):
        o_ref[...] = (a_ref[...] + b_ref[...]).astype(BF)

    if ax == 1:
        mine = pl.BlockSpec((tr, c), lambda i, s: (s[1] * (r // tr) + i, 0))
    else:
        mine = pl.BlockSpec((tr, c), lambda i, s: (i, s[1]))
    spec = pl.BlockSpec((tr, c), lambda i, s: (i, 0))
    return pl.pallas_call(
        body, name=name,
        grid_spec=pltpu.PrefetchScalarGridSpec(num_scalar_prefetch=1, grid=(r // tr,), in_specs=[mine, spec],
                                               out_specs=spec),
        out_shape=jax.ShapeDtypeStruct((r, c), BF),
        compiler_params=_cp(("arbitrary",)),
    )(place, full, got)


def sum4(name, place, parts, half, ax):
    _, r, c = parts.shape
    tr = _row_tile(r)

    def body(s_ref, p_ref, h_ref, o_ref):
        o_ref[...] = ((p_ref[0].astype(F32) + p_ref[1].astype(F32))
                      + (p_ref[2].astype(F32) + h_ref[...].astype(F32)))

    if ax == 1:
        own = pl.BlockSpec((tr, c), lambda i, s: (i, s[0]))
        out = pl.BlockSpec((tr, c), lambda i, s: (s[1] * (r // tr) + i, 0))
        shape = (2 * r, c)
    else:
        own = pl.BlockSpec((tr, c), lambda i, s: (s[0] * (r // tr) + i, 0))
        out = pl.BlockSpec((tr, c), lambda i, s: (i, s[1]))
        shape = (r, 2 * c)
    return pl.pallas_call(
        body, name=name,
        grid_spec=pltpu.PrefetchScalarGridSpec(
            num_scalar_prefetch=1, grid=(r // tr,),
            in_specs=[pl.BlockSpec((3, tr, c), lambda i, s: (0, i, 0)), own], out_specs=out),
        out_shape=jax.ShapeDtypeStruct(shape, F32),
        compiler_params=_cp(("arbitrary",)),
    )(place, parts, half)


def sum8(name, place, parts, full, ax):
    _, r, c = parts.shape
    tr = _row_tile(r)

    def body(s_ref, p_ref, g_ref, o_ref):
        p = [p_ref[k].astype(F32) for k in range(7)]
        o_ref[...] = ((p[0] + p[1]) + (p[2] + p[3])) + ((p[4] + p[5]) + (p[6] + g_ref[...]))

    if ax == 1:
        own = pl.BlockSpec((tr, c), lambda i, s: (s[1] * (r // tr) + i, s[0]))
        out = pl.BlockSpec((tr, c), lambda i, s: (s[1] * (r // tr) + i, 0))
        shape = (2 * r, c)
    else:
        own = pl.BlockSpec((tr, c), lambda i, s: (s[0] * (r // tr) + i, s[1]))
        out = pl.BlockSpec((tr, c), lambda i, s: (i, s[1]))
        shape = (r, 2 * c)
    return pl.pallas_call(
        body, name=name,
        grid_spec=pltpu.PrefetchScalarGridSpec(
            num_scalar_prefetch=1, grid=(r // tr,),
            in_specs=[pl.BlockSpec((7, tr, c), lambda i, s: (0, i, 0)), own], out_specs=out),
        out_shape=jax.ShapeDtypeStruct(shape, F32),
        compiler_params=_cp(("arbitrary",)),
    )(place, parts, full)


def _place():
    x, y, c = lax.axis_index("x"), lax.axis_index("y"), lax.axis_index("c")
    return x, y, c


def _other_chips(x, y):
    return [(1 - x, y), (x, 1 - y), (1 - x, 1 - y)]


SHARDED = (("w_in", D, IN_COLS, 1), ("w_glu", D, 2 * D, 1), ("w_out", D, D, 0),
           ("w_ffn_in", D, 2 * DFF, 1), ("w_ffn_out", DFF, D, 0))


def _shard_of(ref, axis, k, size):
    if axis == 0:
        return ref.at[pl.ds(k * size, size), :]
    return ref.at[:, pl.ds(k * size, size)]


def _gather_piece(full, dims, k, h):
    r, cc, ax = dims
    if ax == 0:
        return full.at[pl.ds(k * (r // 4) + h * (r // 8), r // 8), :]
    return full.at[pl.ds(h * (r // 2), r // 2), pl.ds(k * (cc // 4), cc // 4)]


def _rows_half(shard, h):
    rs = shard.shape[0]
    return shard.at[pl.ds(h * (rs // 2), rs // 2), :]


SEM = pl.BlockSpec(memory_space=pltpu.SEMAPHORE)
HBM = pl.BlockSpec(memory_space=pltpu.HBM)
EFFECT = pltpu.SideEffectType.DATAFLOW_SIDE_EFFECTING


def _in_hbm(a):
    return pltpu.with_memory_space_constraint(a, pltpu.HBM)


def _split_copy_start(name, copies, ncopies, srcs, lands, after):
    ns, nl = len(srcs), len(lands)

    def body(*refs):
        ins, land_refs = refs[:ns], refs[ns:ns + nl]
        send_sems, recv_sems = refs[ns + nl + 1], refs[ns + nl + 2]
        token = refs[-1]
        for k, (src, dst, dev) in enumerate(copies(ins, land_refs)):
            pltpu.make_async_remote_copy(src_ref=src, dst_ref=dst, send_sem=send_sems.at[k],
                                         recv_sem=recv_sems.at[k], device_id=dev, device_id_type=MESH).start()
        token[...] = jnp.zeros_like(token)

    res = pl.pallas_call(
        body, name=name,
        in_specs=[HBM] * (ns + nl) + [ANY],
        out_specs=(SEM, SEM) + (HBM,) * (ns + nl) + (pl.BlockSpec(memory_space=pltpu.VMEM),),
        out_shape=(pltpu.SemaphoreType.DMA((ncopies,)), pltpu.SemaphoreType.DMA((ncopies,)))
        + tuple(pltpu.HBM(a.shape, a.dtype) for a in list(srcs) + list(lands))
        + (jax.ShapeDtypeStruct((8, 128), F32),),
        input_output_aliases={t: 2 + t for t in range(ns + nl)},
        compiler_params=pltpu.CompilerParams(has_side_effects=EFFECT),
    )(*[_in_hbm(a) for a in srcs], *[_in_hbm(a) for a in lands], after)
    return res[0], res[1], list(res[2:2 + ns]), list(res[2 + ns:2 + ns + nl]), res[-1]


def _split_copy_wait(name, copies, arrivals, send_sems, recv_sems, srcs, lands, after):
    ns, nl = len(srcs), len(lands)

    def body(*refs):
        ins, land_refs = refs[:ns], refs[ns:ns + nl]
        send_sems_ref, recv_sems_ref = refs[ns + nl], refs[ns + nl + 1]
        mine = copies(ins, land_refs)
        for k, ((src, dst, dev), got) in enumerate(zip(mine, arrivals(ins, land_refs))):
            pltpu.make_async_remote_copy(src_ref=src, dst_ref=dst, send_sem=send_sems_ref.at[k],
                                         recv_sem=recv_sems_ref.at[k], device_id=dev,
                                         device_id_type=MESH).wait_send()
            pltpu.make_async_remote_copy(src_ref=got, dst_ref=got, send_sem=send_sems_ref.at[k],
                                         recv_sem=recv_sems_ref.at[k], device_id=dev,
                                         device_id_type=MESH).wait_recv()

    res = pl.pallas_call(
        body, name=name,
        in_specs=[HBM] * (ns + nl) + [SEM, SEM, ANY],
        out_specs=(HBM,) * (ns + nl),
        out_shape=tuple(pltpu.HBM(a.shape, a.dtype) for a in list(srcs) + list(lands)),
        input_output_aliases={t: t for t in range(ns + nl)},
        compiler_params=pltpu.CompilerParams(has_side_effects=EFFECT),
    )(*srcs, *lands, send_sems, recv_sems, after)
    return list(res[:ns]), list(res[ns:])


def _gather_copies(dims):
    def copies(ins, lands):
        x, y, c = _place()
        mine = 2 * x + y
        out = []
        for t, d in enumerate(dims):
            size = (d[0] if d[2] == 0 else d[1]) // 4
            for px, py in _other_chips(x, y):
                for dc in range(2):
                    out.append((_rows_half(ins[t], c), _gather_piece(lands[t], d, mine, c), (px, py, c ^ dc)))
            out.append((ins[t], _shard_of(lands[t], d[2], mine, size), (x, y, 1 - c)))
        return out

    def arrivals(ins, lands):
        x, y, c = _place()
        mine = 2 * x + y
        out = []
        for t, d in enumerate(dims):
            size = (d[0] if d[2] == 0 else d[1]) // 4
            for px, py in _other_chips(x, y):
                for dc in range(2):
                    out.append(_gather_piece(lands[t], d, 2 * px + py, c ^ dc))
            out.append(_shard_of(lands[t], d[2], mine, size))
        return out

    return copies, arrivals


def _half_of(ref, ax, h):
    r, c = ref.shape
    if ax == 1:
        return ref.at[pl.ds(h * (r // 2), r // 2), :]
    return ref.at[:, pl.ds(h * (c // 2), c // 2)]


def _half_shape(r, c, ax):
    return (r // 2, c) if ax == 1 else (r, c // 2)


def pair_exchange(name, grads, idxs):
    n = len(grads)
    which = [SHARDED[i] for i in idxs]

    def body(*refs):
        ins, got = refs[:n], refs[n:2 * n]
        send_sems, recv_sems = refs[2 * n:]
        x, y, c = _place()
        cps = []
        for t, (_, _, _, ax) in enumerate(which):
            cp = pltpu.make_async_remote_copy(
                src_ref=_half_of(ins[t], ax, 1 - c), dst_ref=got[t], send_sem=send_sems.at[t],
                recv_sem=recv_sems.at[t], device_id=(x, y, 1 - c), device_id_type=MESH)
            cp.start()
            cps.append(cp)
        for cp in cps:
            cp.wait()

    return pl.pallas_call(
        body, name=name,
        in_specs=[ANY] * n, out_specs=[ANY] * n,
        out_shape=[jax.ShapeDtypeStruct(_half_shape(r, c, ax), F32) for _, r, c, ax in which],
        scratch_shapes=[pltpu.SemaphoreType.DMA((n,)), pltpu.SemaphoreType.DMA((n,))],
        compiler_params=pltpu.CompilerParams(has_side_effects=True),
    )(*grads)


def _piece_shapes(idxs):
    out = []
    for _, r, c, ax in [SHARDED[i] for i in idxs]:
        hr, hc = _half_shape(r, c, ax)
        out.append((hr // 4, hc) if ax == 0 else (hr, hc // 4))
    return out


def _piece_of(full, ax, k, h):
    r, c = full.shape
    if ax == 1:
        return full.at[pl.ds(h * (r // 2), r // 2), pl.ds(k * (c // 4), c // 4)]
    return full.at[pl.ds(k * (r // 4), r // 4), pl.ds(h * (c // 2), c // 2)]


def _scatter8_copies(idxs):
    which = [SHARDED[i] for i in idxs]

    def copies(ins, lands):
        x, y, c = _place()
        out = []
        for t, (_, _, _, ax) in enumerate(which):
            for j, (px, py) in enumerate(_other_chips(x, y)):
                for d in range(2):
                    out.append((_piece_of(ins[t], ax, 2 * px + py, c ^ d), lands[t].at[2 * j + d], (px, py, c ^ d)))
            out.append((_piece_of(ins[t], ax, 2 * x + y, 1 - c), lands[t].at[6], (x, y, 1 - c)))
        return out

    def arrivals(ins, lands):
        return [lands[t].at[k] for t in range(len(which)) for k in range(7)]

    return copies, arrivals


def scatter_grads(halves, idxs):
    n = len(halves)
    which = [SHARDED[i] for i in idxs]

    def shapes():
        return _piece_shapes(idxs)

    def body(*refs):
        ins, outs = refs[:n], refs[n:2 * n]
        send_sems, recv_sems = refs[2 * n:]
        x, y, c = _place()
        started = []
        for t, (src, dst, (_, _, _, ax)) in enumerate(zip(ins, outs, which)):
            size = src.shape[ax] // 4
            for j, (px, py) in enumerate(_other_chips(x, y)):
                cp = pltpu.make_async_remote_copy(
                    src_ref=_shard_of(src, ax, 2 * px + py, size), dst_ref=dst.at[j],
                    send_sem=send_sems.at[3 * t + j], recv_sem=recv_sems.at[3 * t + j],
                    device_id=(px, py, c), device_id_type=MESH)
                cp.start()
                started.append(cp)
        for cp in started:
            cp.wait()

    return pl.pallas_call(
        body, name="scatter_grads",
        in_specs=[ANY] * n, out_specs=[ANY] * n,
        out_shape=[jax.ShapeDtypeStruct((3,) + s, BF) for s in shapes()],
        scratch_shapes=[pltpu.SemaphoreType.DMA((3 * n,)), pltpu.SemaphoreType.DMA((3 * n,))],
        compiler_params=pltpu.CompilerParams(has_side_effects=True),
    )(*halves)


def join_halves(name, shards, idxs):
    n = len(shards)
    which = [SHARDED[i] for i in idxs]

    def body(*refs):
        ins, outs = refs[:n], refs[n:2 * n]
        send_sems, recv_sems = refs[2 * n:]
        x, y, c = _place()
        cps = []
        for t, (_, _, _, ax) in enumerate(which):
            cp = pltpu.make_async_remote_copy(
                src_ref=_half_of(ins[t], ax, c), dst_ref=_half_of(outs[t], ax, c), send_sem=send_sems.at[t],
                recv_sem=recv_sems.at[t], device_id=(x, y, 1 - c), device_id_type=MESH)
            cp.start()
            cps.append(cp)
        for t, cp in enumerate(cps):
            cp.wait_send()
            theirs = _half_of(outs[t], which[t][3], 1 - c)
            pltpu.make_async_remote_copy(
                src_ref=theirs, dst_ref=theirs, send_sem=send_sems.at[t], recv_sem=recv_sems.at[t],
                device_id=(x, y, 1 - c), device_id_type=MESH).wait_recv()

    return pl.pallas_call(
        body, name=name,
        in_specs=[ANY] * n, out_specs=[ANY] * n,
        out_shape=[jax.ShapeDtypeStruct(s.shape, F32) for s in shards],
        input_output_aliases={t: t for t in range(n)},
        scratch_shapes=[pltpu.SemaphoreType.DMA((n,)), pltpu.SemaphoreType.DMA((n,))],
        compiler_params=pltpu.CompilerParams(has_side_effects=True),
    )(*shards)


def allreduce_small(pack):
    rows = pack.shape[0]
    hr = rows // 2

    def body(p_ref, o_ref, sib_ref, parts_ref, send_sems, recv_sems):
        x, y, c = _place()
        mine = 2 * x + y
        sibling = (x, y, 1 - c)
        swap = pltpu.make_async_remote_copy(
            src_ref=p_ref, dst_ref=sib_ref, send_sem=send_sems.at[0], recv_sem=recv_sems.at[0],
            device_id=sibling, device_id_type=MESH)
        swap.start()
        swap.wait()
        half = pl.ds(pl.multiple_of(c * hr, 8), hr)
        parts_ref[mine] = p_ref[half, :] + sib_ref[half, :]
        cps = []
        for j, chip in enumerate(_other_chips(x, y)):
            cp = pltpu.make_async_remote_copy(
                src_ref=parts_ref.at[mine], dst_ref=parts_ref.at[mine], send_sem=send_sems.at[1 + j],
                recv_sem=recv_sems.at[1 + j], device_id=(*chip, c), device_id_type=MESH)
            cp.start()
            cps.append(cp)
        for j, (px, py) in enumerate(_other_chips(x, y)):
            cps[j].wait_send()
            theirs = parts_ref.at[2 * px + py]
            pltpu.make_async_remote_copy(
                src_ref=theirs, dst_ref=theirs, send_sem=send_sems.at[1 + j], recv_sem=recv_sems.at[1 + j],
                device_id=(px, py, c), device_id_type=MESH).wait_recv()
        o_ref[half, :] = (parts_ref[0] + parts_ref[1]) + (parts_ref[2] + parts_ref[3])
        back = pltpu.make_async_remote_copy(
            src_ref=o_ref.at[half, :], dst_ref=o_ref.at[half, :], send_sem=send_sems.at[4],
            recv_sem=recv_sems.at[4], device_id=sibling, device_id_type=MESH)
        back.start()
        back.wait_send()
        other = pl.ds(pl.multiple_of((1 - c) * hr, 8), hr)
        pltpu.make_async_remote_copy(
            src_ref=o_ref.at[other, :], dst_ref=o_ref.at[other, :], send_sem=send_sems.at[4],
            recv_sem=recv_sems.at[4], device_id=sibling, device_id_type=MESH).wait_recv()

    return pl.pallas_call(
        body, name="allreduce_small",
        in_specs=[pl.BlockSpec(memory_space=pltpu.VMEM)], out_specs=pl.BlockSpec(memory_space=pltpu.VMEM),
        out_shape=jax.ShapeDtypeStruct((rows, 128), F32),
        scratch_shapes=[pltpu.VMEM((rows, 128), F32), pltpu.VMEM((4, hr, 128), F32),
                        pltpu.SemaphoreType.DMA((5,)), pltpu.SemaphoreType.DMA((5,))],
        compiler_params=pltpu.CompilerParams(has_side_effects=True, vmem_limit_bytes=40 << 20),
    )(pack)


def _ssm_discretize(lam_re, lam_im, log_dt, b_re, b_im):
    dt = jnp.exp(log_dt)[:, None]
    mag = jnp.exp(lam_re * dt)
    ar, ai = mag * jnp.cos(lam_im * dt), mag * jnp.sin(lam_im * dt)
    den = lam_re * lam_re + lam_im * lam_im
    nr, ni = ar - 1.0, ai
    fr, fi = (nr * lam_re + ni * lam_im) / den, (ni * lam_re - nr * lam_im) / den
    bbr = fr[:, None, :] * b_re - fi[:, None, :] * b_im
    bbi = fr[:, None, :] * b_im + fi[:, None, :] * b_re
    return ar, ai, bbr, bbi


def _cmul(a, b):
    return a[0] * b[0] - a[1] * b[1], a[0] * b[1] + a[1] * b[0]


def _scan_powers(ar, ai):
    a = (ar.reshape(1, SW), ai.reshape(1, SW))
    big = a
    for _ in range(SEG.bit_length() - 1):
        big = _cmul(big, big)
    assert 1 << (SEG.bit_length() - 1) == SEG
    pows = {1: big}
    pows[2] = _cmul(big, big)
    pows[4] = _cmul(pows[2], pows[2])
    rid = jnp.arange(8)[:, None]
    ones = jnp.ones((8, 1), F32)
    fwd, rev = [ones * a[0], ones * a[1]], [ones * a[0], -ones * a[1]]
    for k in (1, 2, 4):
        pr, pi = pows[k]
        fwd += [jnp.where(rid >= k, pr, 0.0), jnp.where(rid >= k, pi, 0.0)]
        rev += [jnp.where(rid < 8 - k, pr, 0.0), jnp.where(rid < 8 - k, -pi, 0.0)]
    return jnp.concatenate(fwd + rev, axis=0)


def _pack_b(bb):
    t = bb.reshape(NLC, 8, GCH, NSTATE)
    return jnp.einsum("jgcp,gh->jgchp", t, jnp.eye(8, dtype=bb.dtype)).reshape(NLC, 128, LC)


def _unpack_b(db):
    t = jnp.einsum("jgchp,gh->jgcp", db.reshape(NLC, 8, GCH, 8, NSTATE), jnp.eye(8, dtype=db.dtype))
    return t.reshape(GROUPS, GCH, NSTATE)


def _pack_c(cc):
    t = cc.reshape(NLC, 8, GCH, NSTATE).transpose(0, 1, 3, 2)
    return jnp.einsum("jgpc,gh->jgphc", t, jnp.eye(8, dtype=cc.dtype)).reshape(NLC, LC, 128)


def _unpack_c(dc):
    t = jnp.einsum("jgphc,gh->jgpc", dc.reshape(NLC, 8, NSTATE, 8, GCH), jnp.eye(8, dtype=dc.dtype))
    return t.transpose(0, 1, 3, 2).reshape(GROUPS, GCH, NSTATE)


SMALL = ("norm_mix", "q_norm", "k_norm", "attn_sinks", "lam_re", "lam_im", "log_dt", "ssm_b_re", "ssm_b_im",
         "ssm_c_re", "ssm_c_im", "ssm_d", "attn_branch_norm", "ssm_branch_norm", "norm_ffn")


def _pack_small(arrs, rows=None):
    flat = jnp.concatenate([a.reshape(-1).astype(F32) for a in arrs])
    n = flat.shape[0]
    if rows is None:
        rows = -(-n // (128 * 256)) * 256
    return jnp.pad(flat, (0, rows * 128 - n)).reshape(rows, 128)


def _unpack_small(pack, like):
    flat = pack.reshape(-1)
    out, off = [], 0
    for a in like:
        out.append(flat[off:off + a.size].reshape(a.shape))
        off += a.size
    return out


def _gather_half_copies(dims):
    def copies(ins, lands):
        x, y, c = _place()
        mine = 2 * x + y
        out = []
        for t, d in enumerate(dims):
            size = (d[0] if d[2] == 0 else d[1]) // 4
            for chip in _other_chips(x, y):
                out.append((_rows_half(ins[t], c), _gather_piece(lands[t], d, mine, c), (*chip, c)))
            out.append((ins[t], _shard_of(lands[t], d[2], mine, size), (x, y, 1 - c)))
        return out

    def arrivals(ins, lands):
        x, y, c = _place()
        mine = 2 * x + y
        out = []
        for t, d in enumerate(dims):
            size = (d[0] if d[2] == 0 else d[1]) // 4
            for px, py in _other_chips(x, y):
                out.append(_gather_piece(lands[t], d, 2 * px + py, c))
            out.append(_shard_of(lands[t], d[2], mine, size))
        return out

    return copies, arrivals


def gather_forward(fulls, dims):
    n = len(fulls)

    def body(*refs):
        ins, outs = refs[:n], refs[n:2 * n]
        send_sems, recv_sems = refs[2 * n:]
        x, y, c = _place()
        cps = []
        for t in range(n):
            for j, (px, py) in enumerate(_other_chips(x, y)):
                cp = pltpu.make_async_remote_copy(
                    src_ref=_gather_piece(ins[t], dims[t], 2 * px + py, c),
                    dst_ref=_gather_piece(outs[t], dims[t], 2 * px + py, c),
                    send_sem=send_sems.at[3 * t + j], recv_sem=recv_sems.at[3 * t + j],
                    device_id=(x, y, 1 - c), device_id_type=MESH)
                cp.start()
                cps.append(cp)
        for t in range(n):
            for j, (px, py) in enumerate(_other_chips(x, y)):
                cps[3 * t + j].wait_send()
                other = _gather_piece(outs[t], dims[t], 2 * px + py, 1 - c)
                pltpu.make_async_remote_copy(
                    src_ref=other, dst_ref=other, send_sem=send_sems.at[3 * t + j], recv_sem=recv_sems.at[3 * t + j],
                    device_id=(x, y, 1 - c), device_id_type=MESH).wait_recv()

    return pl.pallas_call(
        body, name="gather_forward",
        in_specs=[ANY] * n, out_specs=[ANY] * n,
        out_shape=[jax.ShapeDtypeStruct(a.shape, a.dtype) for a in fulls],
        input_output_aliases={t: t for t in range(n)},
        scratch_shapes=[pltpu.SemaphoreType.DMA((3 * n,)), pltpu.SemaphoreType.DMA((3 * n,))],
        compiler_params=pltpu.CompilerParams(has_side_effects=True),
    )(*fulls)


def _tie(a, token):
    return a if token is None else a + token[0, 0]


class _FirstGather:
    dims = [SHARDED[0][1:], (N_META, D, 1)]

    def __init__(self, w_in_shard, meta_shard):
        self.copies, self.arrivals = _gather_half_copies(self.dims)
        lands = [lax.empty((D, IN_COLS), BF), lax.empty((N_META, D), F32)]
        self.send, self.recv, self.srcs, self.lands, self.token = _split_copy_start(
            "gather_start_first", self.copies, 8, [w_in_shard, meta_shard], lands, w_in_shard)

    def finish(self, after):
        _, lands = _split_copy_wait("gather_wait_first", self.copies, self.arrivals, self.send, self.recv, self.srcs,
                                    self.lands, after)
        return gather_forward(lands, self.dims)


class _Exchange:
    def __init__(self, rest_shards, place):
        self.rest_shards = rest_shards
        self.place = place
        self.rest_dims = [s[1:] for s in SHARDED[1:]]
        self.pending = {}
        self.halves = {}
        self.full = {}
        self.parts = {}

    def begin(self, after):
        copies, arrivals = _gather_copies(self.rest_dims)
        lands = [lax.empty((r, c), BF) for r, c, _ in self.rest_dims]
        send, recv, srcs, lands, token = _split_copy_start(
            "gather_start", copies, 7 * len(lands), self.rest_shards, lands, after)
        self.pending["gather"] = (copies, arrivals, send, recv, srcs, lands)
        return token

    def rest_weights(self, after):
        copies, arrivals, send, recv, srcs, lands = self.pending.pop("gather")
        _, fulls = _split_copy_wait("gather_wait", copies, arrivals, send, recv, srcs, lands, after)
        return dict(zip([s[0] for s in SHARDED[1:]], fulls))

    def _halves(self, group, idxs, grads):
        got = pair_exchange("pair_exchange_" + group, grads, idxs)
        return [pair_sum("pair_sum_" + SHARDED[i][0], self.place, g, b, SHARDED[i][3])
                for i, g, b in zip(idxs, grads, got)]

    def start(self, group, idxs, grads):
        if grads[0][1] is None:
            halves = self._halves(group, idxs, [g for g, _ in grads])
            for i, h, pt in zip(idxs, halves, scatter_grads(halves, idxs)):
                self.halves[i], self.parts[i] = h, pt
            return None
        copies, arrivals = _scatter8_copies(idxs)
        lands = [lax.empty((7,) + sh, BF) for sh in _piece_shapes(idxs)]
        send, recv, srcs, lands, token = _split_copy_start(
            "scatter_start_" + group, copies, 7 * len(idxs), [g16 for _, g16 in grads], lands, grads[0][0])
        self.pending[group] = (copies, arrivals, send, recv, srcs, lands, idxs)
        for i, (g32, _) in zip(idxs, grads):
            self.full[i] = g32
        return token

    def finish(self, group, after):
        if group not in self.pending:
            return
        copies, arrivals, send, recv, srcs, lands, idxs = self.pending.pop(group)
        _, lands = _split_copy_wait("scatter_wait_" + group, copies, arrivals, send, recv, srcs, lands, after)
        for i, pt in zip(idxs, lands):
            self.parts[i] = pt


def local_step(x, tgt, p, first, ex):
    qn_t = jnp.tile(p["q_norm"].reshape(1, HEAD), (1, NQ)) * HEAD ** -0.5
    kn_t = jnp.tile(p["k_norm"].reshape(1, HEAD), (1, NKV))
    seg = jnp.arange(256) // HEAD
    bd = (seg[:, None] == seg[None, :]).astype(BF)
    nm = p["norm_mix"].reshape(1, D)
    sinks = p["attn_sinks"].reshape(NQ)
    dsk = p["ssm_d"].reshape(1, D)
    abn, sbn, gf = (p[k].reshape(1, D) for k in ("attn_branch_norm", "ssm_branch_norm", "norm_ffn"))

    disc_in = (p["lam_re"], p["lam_im"], _tie(p["log_dt"], first.token), p["ssm_b_re"], p["ssm_b_im"])
    (ar, ai, bbr, bbi), disc_vjp = jax.vjp(_ssm_discretize, *disc_in)
    pw = _scan_powers(ar, ai)
    brp, bip = _pack_b(bbr.astype(BF)), _pack_b(bbi.astype(BF))
    crp = _pack_c(_tie(p["ssm_c_re"], first.token).astype(BF))
    cip = _pack_c(_tie(p["ssm_c_im"], first.token).astype(BF))

    w_in, meta_full = first.finish(brp)
    mpad = jnp.concatenate([jnp.zeros((META0, D), F32), meta_full], axis=0)
    token = ex.begin(w_in)
    xn, qr, kr, qn, kn, v, u, gg = in_proj_fwd(x, mpad, w_in, _tie(nm, token), qn_t, kn_t, bd)
    attn = attention_fwd(sinks, qn, kn, v)
    perm = _segment_order()
    y, sr, si = ssm_fwd(u, pw, brp, bip, crp, cip, dsk, perm)
    w = ex.rest_weights(y)
    zb, zzb, mgb, h1 = mid_fwd(y, attn, gg, x, mpad, w["w_glu"], w["w_out"], abn, sbn)
    hnb, dgub, actb, dh2b, dh1, dh1b, dgf, loss = ffn_fwd_bwd(h1, tgt, w["w_ffn_in"], w["w_ffn_out"], gf)
    token = ex.start("ffn", (3, 4), [weight_grad("grad_w_ffn_in", hnb, dgub, 1408, True),
                                     weight_grad("grad_w_ffn_out", actb, dh2b, 512, True)])
    dy, dattn, dggb, dzzb, dabn, dsbn = mid_bwd(dh1b, y, attn, gg, zzb, w["w_glu"], w["w_out"], _tie(abn, token),
                                                sbn)
    grads_mid = [weight_grad("grad_w_glu", zb, dzzb, 1024, True), weight_grad("grad_w_out", mgb, dh1b, 1024, True)]
    dub, da, dbr, dbi, dcr, dci, dd = ssm_bwd(dy, u, sr, si, pw, brp, bip, crp, cip, dsk, perm)
    ex.finish("ffn", dub)
    token = ex.start("mid", (1, 2), grads_mid)
    dqn, dkn, dv, dsink = attention_bwd(_tie(sinks, token), qn, kn, v, dattn)
    dproj, dx, dmpad, dnm, dqw, dkw = in_proj_bwd(dqn, dkn, dv, dub, dggb, qr, kr, x, mpad, dh1, w_in, nm, qn_t,
                                                  kn_t, bd)
    ex.finish("mid", dproj)
    ex.start("in", (0,), [(weight_grad("grad_w_in", xn, dproj, 1536), None)])

    dar = jnp.sum(da[0:8], axis=0).reshape(GROUPS, NSTATE)
    dai = jnp.sum(da[8:16], axis=0).reshape(GROUPS, NSTATE)
    dlr, dli, dldt, dbre, dbim = disc_vjp((dar, dai, _unpack_b(dbr), _unpack_b(dbi)))
    small = {
        "norm_mix": dnm, "q_norm": dqw.reshape(NQ, HEAD).sum(0) * HEAD ** -0.5, "k_norm": dkw.reshape(NKV, HEAD).sum(0),
        "attn_sinks": dsink[0, 0:NQ], "lam_re": dlr, "lam_im": dli, "log_dt": dldt,
        "ssm_b_re": dbre, "ssm_b_im": dbim, "ssm_c_re": _unpack_c(jnp.swapaxes(dcr, 1, 2)), "ssm_c_im": _unpack_c(jnp.swapaxes(dci, 1, 2)),
        "ssm_d": dd, "attn_branch_norm": dabn, "ssm_branch_norm": dsbn, "norm_ffn": dgf,
    }
    return loss[0, 0], dx, dmpad[META0:PAD], small


def kernel(x, meta_tokens, norm_mix, w_in, q_norm, k_norm, attn_sinks, lam_re, lam_im, log_dt, ssm_b_re, ssm_b_im, ssm_c_re, ssm_c_im, ssm_d, w_glu, attn_branch_norm, ssm_branch_norm, w_out, norm_ffn, w_ffn_in, w_ffn_out, loss_target, m_meta_tokens, m_norm_mix, m_w_in, m_q_norm, m_k_norm, m_attn_sinks, m_lam_re, m_lam_im, m_log_dt, m_ssm_b_re, m_ssm_b_im, m_ssm_c_re, m_ssm_c_im, m_ssm_d, m_w_glu, m_attn_branch_norm, m_ssm_branch_norm, m_w_out, m_norm_ffn, m_w_ffn_in, m_w_ffn_out, v_meta_tokens, v_norm_mix, v_w_in, v_q_norm, v_k_norm, v_attn_sinks, v_lam_re, v_lam_im, v_log_dt, v_ssm_b_re, v_ssm_b_im, v_ssm_c_re, v_ssm_c_im, v_ssm_d, v_w_glu, v_attn_branch_norm, v_ssm_branch_norm, v_w_out, v_norm_ffn, v_w_ffn_in, v_w_ffn_out):
    args = dict(locals())
    names = ["meta_tokens", "norm_mix", "w_in", "q_norm", "k_norm", "attn_sinks", "lam_re", "lam_im", "log_dt",
             "ssm_b_re", "ssm_b_im", "ssm_c_re", "ssm_c_im", "ssm_d", "w_glu", "attn_branch_norm",
             "ssm_branch_norm", "w_out", "norm_ffn", "w_ffn_in", "w_ffn_out"]
    big_names = [s[0] for s in SHARDED]
    xs, ys, cs = _place()
    chip = 2 * xs + ys

    first = _FirstGather(args["w_in"][0].astype(BF), meta_tokens)
    shards = [None] + [_tie(args[n][0], first.token).astype(BF) for n in big_names[1:]]
    place = jnp.stack([chip, cs]).astype(jnp.int32)
    ex = _Exchange(shards[1:], place)

    swapped = ("ssm_b_re", "ssm_b_im")

    def natural(n, a):
        return jnp.swapaxes(a, -1, -2) if n in swapped else a

    p = {n: natural(n, args[n][0]) for n in SMALL}
    loss, dx, dmeta, gsmall = local_step(x[0], loss_target[0], p, first, ex)

    out = {}

    def finish(idxs, name):
        mine = [sum8("sum8_" + SHARDED[i][0], place, ex.parts[i], ex.full[i], SHARDED[i][3]) if i in ex.full
                else sum4("sum4_" + SHARDED[i][0], place, ex.parts[i], ex.halves[i], SHARDED[i][3]) for i in idxs]
        for i, g in zip(idxs, join_halves(name, mine, idxs)):
            n = SHARDED[i][0]
            out[n] = tuple(t[None] for t in adamw("adamw_" + n, args[n][0], g, args["m_" + n][0], args["v_" + n][0]))

    finish((0, 1, 2, 3, 4), "join_halves")
    small_list = [gsmall[n].reshape(natural(n, args[n]).shape) for n in SMALL] + [dmeta, loss.reshape(1, 1)]
    red = allreduce_small(_pack_small(small_list))
    *small_red, dmeta_sum, loss = _unpack_small(red, small_list)
    loss = loss[0, 0]
    gmeta = lax.dynamic_slice_in_dim(dmeta_sum, chip * (D // 4), D // 4, axis=1)
    out["meta_tokens"] = tuple(adamw("adamw_meta", meta_tokens, gmeta, m_meta_tokens, v_meta_tokens))
    ds, nms, nvs = adamw_small([natural(n, args[n]) for n in SMALL], small_red,
                               [natural(n, args["m_" + n]) for n in SMALL],
                               [natural(n, args["v_" + n]) for n in SMALL])
    for n, g_, d_, m_, v_ in zip(SMALL, small_red, ds, nms, nvs):
        out[n] = tuple(natural(n, t) for t in (g_, d_, m_, v_))

    res = [loss, dx[None]]
    for k in range(4):
        res += [out[n][k] for n in names]
    return tuple(res)
```

```python
import functools
import math

import jax
import jax.numpy as jnp
from jax import lax
from jax.experimental import pallas as pl
from jax.experimental.pallas import tpu as pltpu

F32 = jnp.float32
BF = jnp.bfloat16

D = 1024
N_META = 16
HEAD = 64
NQ = 16
NKV = 4
QW = NQ * HEAD
KVW = NKV * HEAD
WIN = 128
GROUPS = 64
GCH = 16
NSTATE = 64
SW = GROUPS * NSTATE
DFF = 2816
IN_COLS = QW + 2 * KVW + D + 2 * D
PAD = 256
META0 = PAD - N_META
EPS = 1e-6
NEG = -1e30

TM = 256
TS = 256
LC = 512
NLC = SW // LC

LR, B1, B2, AEPS, WD, STEP = 0.001, 0.9, 0.999, 1e-08, 0.01, 10
BC1 = 1.0 - B1 ** STEP
BC2 = 1.0 - B2 ** STEP

MESH = pl.DeviceIdType.MESH
ANY = pl.BlockSpec(memory_space=pl.ANY)


def _cp(sem=None, vmem_mb=48):
    return pltpu.CompilerParams(dimension_semantics=sem, vmem_limit_bytes=vmem_mb << 20)


def _full(shape):
    n = len(shape)
    return pl.BlockSpec(shape, lambda *a: (0,) * n)


def _const(shape):
    n = len(shape)
    return pl.BlockSpec(shape, lambda *a: (0,) * n, pipeline_mode=pl.Buffered(1))


def _rows(tm, width):
    return pl.BlockSpec((tm, width), lambda i: (i, 0))


def _dot(a, b):
    return jnp.dot(a, b, preferred_element_type=F32)


def _dot_nt(a, b):
    return lax.dot_general(a, b, (((1,), (1,)), ((), ())), preferred_element_type=F32)


def _dot_tn(a, b):
    return lax.dot_general(a, b, (((0,), (0,)), ((), ())), preferred_element_type=F32)


def _sigmoid(x):
    return 0.5 * jnp.tanh(0.5 * x) + 0.5


def _seg_mean(x, bd, split=True):
    outs = []
    for j in range(x.shape[1] // 256):
        c = x[:, 256 * j:256 * (j + 1)]
        hi = c.astype(BF)
        part = _dot(hi, bd)
        if split:
            part += _dot((c - hi.astype(F32)).astype(BF), bd)
        outs.append(part)
    out = outs[0] if len(outs) == 1 else jnp.concatenate(outs, axis=1)
    return out * (1.0 / HEAD)


_GC = math.sqrt(2.0 / math.pi)


def _gelu(x):
    t = jnp.tanh(_GC * (x + 0.044715 * x * x * x))
    return 0.5 * x * (1.0 + t), t


def _gelu_grad(x, t):
    return 0.5 * (1.0 + t) + 0.5 * x * (1.0 - t * t) * _GC * (1.0 + 3.0 * 0.044715 * x * x)


def _seq_specs():
    assert TM == PAD
    return [pl.BlockSpec((TM, D), lambda i: (jnp.maximum(i - 1, 0), 0)), _full((PAD, D))]


def _seq_tile(x_ref, m_ref):
    return jnp.where(pl.program_id(0) == 0, m_ref[...], x_ref[...])


def in_proj_fwd(x, mpad, w_in, nm, qn_t, kn_t, bd):
    lp = x.shape[0] + PAD

    def body(x_ref, m_ref, w_ref, nm_ref, qn_ref, kn_ref, bd_ref,
             xn_o, qr_o, kr_o, qn_o, kn_o, v_o, u_o, gg_o):
        h = _seq_tile(x_ref, m_ref)
        r = lax.rsqrt(jnp.mean(h * h, axis=-1, keepdims=True) + EPS)
        xb = ((h * r) * nm_ref[...]).astype(BF)
        xn_o[...] = xb
        bdv = bd_ref[...]
        q = _dot(xb, w_ref[:, 0:QW])
        qr_o[...] = q
        qn_o[...] = (q * lax.rsqrt(_seg_mean(q * q, bdv, False) + EPS) * qn_ref[...]).astype(BF)
        k = _dot(xb, w_ref[:, QW:QW + KVW])
        kr_o[...] = k
        kn_o[...] = (k * lax.rsqrt(_seg_mean(k * k, bdv, False) + EPS) * kn_ref[...]).astype(BF)
        v_o[...] = _dot(xb, w_ref[:, QW + KVW:QW + 2 * KVW]).astype(BF)
        u_o[...] = _dot(xb, w_ref[:, QW + 2 * KVW:QW + 2 * KVW + D])
        gg_o[...] = _dot(xb, w_ref[:, QW + 2 * KVW + D:IN_COLS])

    outs = [(D, BF), (QW, F32), (KVW, F32), (QW, BF), (KVW, BF), (KVW, BF), (D, F32), (2 * D, F32)]
    return pl.pallas_call(
        body, name="in_proj_fwd", grid=(lp // TM,),
        in_specs=_seq_specs() + [_full((D, IN_COLS)), _full((1, D)), _full((1, QW)), _full((1, KVW)),
                                 _full((256, 256))],
        out_specs=[_rows(TM, w) for w, _ in outs],
        out_shape=[jax.ShapeDtypeStruct((lp, w), t) for w, t in outs],
        compiler_params=_cp(("arbitrary",)),
    )(x, mpad, w_in, nm, qn_t, kn_t, bd)


def in_proj_bwd(dqn, dkn, dv, du, dgg, qr, kr, x, mpad, dh1, w_in, nm, qn_t, kn_t, bd):
    lp = x.shape[0] + PAD

    def body(dqn_ref, dkn_ref, dv_ref, du_ref, dgg_ref, qr_ref, kr_ref, x_ref, m_ref, dh1_ref,
             w_ref, nm_ref, qn_ref, kn_ref, bd_ref,
             dproj_o, dx_o, dm_o, dnm_o, dqw_o, dkw_o):
        i = pl.program_id(0)

        @pl.when(i == 0)
        def _():
            dnm_o[...] = jnp.zeros_like(dnm_o)
            dqw_o[...] = jnp.zeros_like(dqw_o)
            dkw_o[...] = jnp.zeros_like(dkw_o)

        bdv = bd_ref[...]

        def norm_bwd(x, dy, g, acc_ref):
            rr = lax.rsqrt(_seg_mean(x * x, bdv, False) + EPS)
            xh = x * rr
            acc_ref[...] += jnp.sum(dy * xh, axis=0, keepdims=True)
            dxh = dy * g
            return rr * (dxh - xh * _seg_mean(dxh * xh, bdv))

        dq = norm_bwd(qr_ref[...], dqn_ref[...], qn_ref[...], dqw_o)
        dk = norm_bwd(kr_ref[...], dkn_ref[...], kn_ref[...], dkw_o)
        dproj_o[:, 0:QW] = dq.astype(BF)
        dproj_o[:, QW:QW + KVW] = dk.astype(BF)
        dproj_o[:, QW + KVW:QW + 2 * KVW] = dv_ref[...].astype(BF)
        dproj_o[:, QW + 2 * KVW:QW + 2 * KVW + D] = du_ref[...]
        dproj_o[:, QW + 2 * KVW + D:IN_COLS] = dgg_ref[...]
        dxn = _dot_nt(dproj_o[...], w_ref[...])
        h = _seq_tile(x_ref, m_ref)
        r = lax.rsqrt(jnp.mean(h * h, axis=-1, keepdims=True) + EPS)
        xh = h * r
        dnm_o[...] += jnp.sum(dxn * xh, axis=0, keepdims=True)
        dxh = dxn * nm_ref[...]
        dh0 = dh1_ref[...] + r * (dxh - xh * jnp.mean(dxh * xh, axis=-1, keepdims=True))
        dx_o[...] = dh0

        @pl.when(i == 0)
        def _():
            dm_o[...] = dh0

    return pl.pallas_call(
        body, name="in_proj_bwd", grid=(lp // TM,),
        in_specs=[_rows(TM, QW), _rows(TM, KVW), _rows(TM, KVW), _rows(TM, D), _rows(TM, 2 * D),
                  _rows(TM, QW), _rows(TM, KVW)] + _seq_specs() + [_rows(TM, D),
                  _full((D, IN_COLS)), _full((1, D)), _full((1, QW)), _full((1, KVW)), _full((256, 256))],
        out_specs=[_rows(TM, IN_COLS), _seq_specs()[0], _full((PAD, D)), _full((1, D)), _full((1, QW)),
                   _full((1, KVW))],
        out_shape=[jax.ShapeDtypeStruct((lp, IN_COLS), BF), jax.ShapeDtypeStruct((lp - PAD, D), F32),
                   jax.ShapeDtypeStruct((PAD, D), F32),
                   jax.ShapeDtypeStruct((1, D), F32), jax.ShapeDtypeStruct((1, QW), F32),
                   jax.ShapeDtypeStruct((1, KVW), F32)],
        compiler_params=_cp(("arbitrary",)),
    )(dqn, dkn, dv, du, dgg, qr, kr, x, mpad, dh1, w_in, nm, qn_t, kn_t, bd)


def _att_bias(b):
    qi = lax.broadcasted_iota(jnp.int32, (WIN, WIN), 0)
    kj = lax.broadcasted_iota(jnp.int32, (WIN, WIN), 1)
    upper = kj > qi
    valid = (upper & ((b - 1) * WIN + kj >= PAD)) | (jnp.logical_not(upper) & (b * WIN + kj >= META0))
    bias_pc = jnp.where(valid, 0.0, NEG)
    kj1 = lax.broadcasted_iota(jnp.int32, (1, WIN), 1)
    bias_m = jnp.where((kj1 >= WIN - N_META) & (b * WIN >= PAD), 0.0, NEG)
    return bias_m, bias_pc, upper


def _dup_half(ref, kp, pos, lo):
    pair = ref[:, 128 * kp:128 * kp + 128].astype(F32)
    rolled = pltpu.roll(pair, 64, axis=1)
    keep = lo if pos == 0 else jnp.logical_not(lo)
    return jnp.where(keep, pair, rolled).astype(BF)


def _stack_heads(ref, kv, lo):
    parts = []
    for pp in range(2):
        pair = ref[:, 256 * kv + 128 * pp:256 * kv + 128 * pp + 128]
        zero = jnp.zeros_like(pair)
        parts.append(jnp.where(lo, pair, zero))
        parts.append(jnp.where(lo, zero, pair))
    return jnp.concatenate(parts, axis=0)


def _unstack_heads(x4, lo):
    a = jnp.where(lo, x4[0:WIN], x4[WIN:2 * WIN])
    b = jnp.where(lo, x4[2 * WIN:3 * WIN], x4[3 * WIN:4 * WIN])
    return a, b


def _split_pc(x, upper):
    zero = jnp.zeros_like(x)
    return jnp.where(upper, x, zero).astype(BF), jnp.where(upper, zero, x).astype(BF)


def _head_softmax(s_m, s_p, s_c, bias, sink):
    bias_m, bias_pc, upper = bias
    sm = s_m + bias_m
    spc = jnp.where(upper, s_p, s_c) + bias_pc
    m = jnp.maximum(jnp.max(jnp.maximum(sm, spc), axis=-1, keepdims=True), sink)
    em, epc = jnp.exp(sm - m), jnp.exp(spc - m)
    esink = jnp.exp(sink - m)
    inv = 1.0 / (esink + jnp.sum(em + epc, axis=-1, keepdims=True))
    return em, epc, inv, esink


def _kv_specs():
    return [pl.BlockSpec((WIN, KVW), lambda b: (1, 0)),
            pl.BlockSpec((WIN, KVW), lambda b: (jnp.maximum(b - 1, 0), 0)),
            pl.BlockSpec((WIN, KVW), lambda b: (b, 0))]


def attention_fwd(sinks, qn, kn, v):
    lp = qn.shape[0]

    def body(sink_ref, q_ref, km_ref, kp_ref, kc_ref, vm_ref, vp_ref, vc_ref, o_ref):
        b = pl.program_id(0)
        bias = _att_bias(b)
        lo = lax.broadcasted_iota(jnp.int32, (WIN, WIN), 1) < HEAD
        for kv in range(NKV):
            kp, pos = kv // 2, kv % 2
            kds = [_dup_half(r, kp, pos, lo) for r in (km_ref, kp_ref, kc_ref)]
            vds = [_dup_half(r, kp, pos, lo) for r in (vm_ref, vp_ref, vc_ref)]
            q4 = _stack_heads(q_ref, kv, lo)
            ss = [_dot_nt(q4, kd) for kd in kds]
            es, invs = ([], [], []), []
            for h in range(4):
                rs = slice(WIN * h, WIN * h + WIN)
                em, epc, inv, _ = _head_softmax(ss[0][rs], ss[1][rs], ss[2][rs], bias, sink_ref[4 * kv + h])
                e_p, e_c = _split_pc(epc, bias[2])
                for lst, e in zip(es, (em.astype(BF), e_p, e_c)):
                    lst.append(e)
                invs.append(inv)
            e_m, e_p, e_c = (jnp.concatenate(lst, axis=0) for lst in es)
            o4 = (_dot(e_m, vds[0]) + _dot(e_p, vds[1]) + _dot(e_c, vds[2])) * jnp.concatenate(invs, axis=0)
            oa, ob = _unstack_heads(o4, lo)
            o_ref[:, 256 * kv:256 * kv + 128] = oa
            o_ref[:, 256 * kv + 128:256 * kv + 256] = ob

    return pl.pallas_call(
        body, name="attention_fwd", grid=(lp // WIN,),
        in_specs=[pl.BlockSpec(memory_space=pltpu.SMEM), _rows(WIN, QW)] + _kv_specs() + _kv_specs(),
        out_specs=_rows(WIN, QW),
        out_shape=jax.ShapeDtypeStruct((lp, QW), F32),
        compiler_params=_cp(("arbitrary",)),
    )(sinks, qn, kn, kn, kn, v, v, v)


def attention_bwd(sinks, qn, kn, v, do):
    lp = qn.shape[0]
    nb = lp // WIN

    def body(sink_ref, q_ref, km_ref, kp_ref, kc_ref, vm_ref, vp_ref, vc_ref, do_ref,
             dq_ref, dk_hbm, dv_hbm, dsink_ref, dk_acc, dv_acc):
        b = pl.program_id(0)

        @pl.when(b == 0)
        def _():
            dk_acc[...] = jnp.zeros_like(dk_acc)
            dv_acc[...] = jnp.zeros_like(dv_acc)
            dsink_ref[...] = jnp.zeros_like(dsink_ref)

        bias = _att_bias(b)
        upper = bias[2]
        lo = lax.broadcasted_iota(jnp.int32, (WIN, WIN), 1) < HEAD
        lane8 = lax.broadcasted_iota(jnp.int32, (8, 128), 1)
        starts = [WIN, pl.multiple_of(jnp.maximum(b - 1, 0) * WIN, WIN), pl.multiple_of(b * WIN, WIN)]
        for kv in range(NKV):
            kp, pos = kv // 2, kv % 2
            keep = lo if pos == 0 else jnp.logical_not(lo)
            kds = [_dup_half(r, kp, pos, lo) for r in (km_ref, kp_ref, kc_ref)]
            vds = [_dup_half(r, kp, pos, lo) for r in (vm_ref, vp_ref, vc_ref)]
            q4 = _stack_heads(q_ref, kv, lo)
            do4 = _stack_heads(do_ref, kv, lo)
            ss = [_dot_nt(q4, kd) for kd in kds]
            dps = [_dot_nt(do4, vd) for vd in vds]
            ds_l, p_l = ([], [], []), ([], [], [])
            for h in range(4):
                rs = slice(WIN * h, WIN * h + WIN)
                em, epc, inv, esink = _head_softmax(ss[0][rs], ss[1][rs], ss[2][rs], bias, sink_ref[4 * kv + h])
                p_m, p_pc = em * inv, epc * inv
                dp_m = dps[0][rs]
                dp_pc = jnp.where(upper, dps[1][rs], dps[2][rs])
                delta = jnp.sum(p_m * dp_m + p_pc * dp_pc, axis=-1, keepdims=True)
                val = -jnp.sum(esink * inv * delta, axis=0, keepdims=True)
                dsink_ref[...] += jnp.where(lane8 == 4 * kv + h, val, 0.0)
                for lst, t in zip(ds_l, ((p_m * (dp_m - delta)).astype(BF),) + _split_pc(p_pc * (dp_pc - delta), upper)):
                    lst.append(t)
                for lst, t in zip(p_l, (p_m.astype(BF),) + _split_pc(p_pc, upper)):
                    lst.append(t)
            dss = [jnp.concatenate(lst, axis=0) for lst in ds_l]
            ps = [jnp.concatenate(lst, axis=0) for lst in p_l]
            dq4 = _dot(dss[0], kds[0]) + _dot(dss[1], kds[1]) + _dot(dss[2], kds[2])
            dqa, dqb = _unstack_heads(dq4, lo)
            dq_ref[:, 256 * kv:256 * kv + 128] = dqa
            dq_ref[:, 256 * kv + 128:256 * kv + 256] = dqb
            for x in range(3):
                dkd = _dot_tn(dss[x], q4)
                dvd = _dot_tn(ps[x], do4)
                dkd = jnp.where(keep, dkd + pltpu.roll(dkd, 64, axis=1), 0.0)
                dvd = jnp.where(keep, dvd + pltpu.roll(dvd, 64, axis=1), 0.0)
                dk_acc[pl.ds(starts[x], WIN), 128 * kp:128 * kp + 128] += dkd
                dv_acc[pl.ds(starts[x], WIN), 128 * kp:128 * kp + 128] += dvd

        @pl.when(b == nb - 1)
        def _():
            pltpu.sync_copy(dk_acc, dk_hbm)
            pltpu.sync_copy(dv_acc, dv_hbm)

    return pl.pallas_call(
        body, name="attention_bwd", grid=(nb,),
        in_specs=[pl.BlockSpec(memory_space=pltpu.SMEM), _rows(WIN, QW)] + _kv_specs() + _kv_specs()
                 + [_rows(WIN, QW)],
        out_specs=[_rows(WIN, QW), ANY, ANY, _full((8, 128))],
        out_shape=[jax.ShapeDtypeStruct((lp, QW), F32), jax.ShapeDtypeStruct((lp, KVW), F32),
                   jax.ShapeDtypeStruct((lp, KVW), F32), jax.ShapeDtypeStruct((8, 128), F32)],
        scratch_shapes=[pltpu.VMEM((lp, KVW), F32), pltpu.VMEM((lp, KVW), F32)],
        compiler_params=_cp(("arbitrary",)),
    )(sinks, qn, kn, kn, kn, v, v, v, do)


PW_ROWS = 64
SEG = TS // 8


def _segment_order():
    rho = jnp.arange(TS)
    token = SEG * (rho % 8) + rho // 8
    p = (token[:, None] == jnp.arange(TS)[None, :]).astype(BF)
    return p, p.T


def _reorder(p, x):
    hi = x.astype(BF)
    lo = (x - hi.astype(F32)).astype(BF)
    return _dot(p, hi) + _dot(p, lo)


def _segment_scan(xr_ref, xi_ref, pw_ref, base, ls, c_r, c_i, reverse, visit=None, rescan=True):
    ar, ai = pw_ref[base:base + 8, ls], pw_ref[base + 8:base + 16, ls]
    order = list(range(SEG - 1, -1, -1)) if reverse else list(range(SEG))
    s_r = s_i = jnp.zeros_like(ar)
    for i in order:
        rows = pl.ds(8 * i, 8)
        s_r, s_i = ar * s_r - ai * s_i + xr_ref[rows, ls], ar * s_i + ai * s_r + xi_ref[rows, ls]
        if not rescan:
            xr_ref[rows, ls] = s_r
            xi_ref[rows, ls] = s_i
    rid = lax.broadcasted_iota(jnp.int32, s_r.shape, 0)
    if reverse:
        e_r = jnp.where(rid < 7, pltpu.roll(s_r, 7, axis=0), c_r)
        e_i = jnp.where(rid < 7, pltpu.roll(s_i, 7, axis=0), c_i)
    else:
        e_r = jnp.where(rid >= 1, pltpu.roll(s_r, 1, axis=0), c_r)
        e_i = jnp.where(rid >= 1, pltpu.roll(s_i, 1, axis=0), c_i)
    for n, k in enumerate((1, 2, 4)):
        shift = 8 - k if reverse else k
        t_r, t_i = pltpu.roll(e_r, shift, axis=0), pltpu.roll(e_i, shift, axis=0)
        kr = pw_ref[base + 16 + 16 * n:base + 24 + 16 * n, ls]
        ki = pw_ref[base + 24 + 16 * n:base + 32 + 16 * n, ls]
        e_r, e_i = e_r + kr * t_r - ki * t_i, e_i + kr * t_i + ki * t_r
    last = 0 if reverse else 7
    a16r, a16i = pw_ref[base + 16 + last:base + 17 + last, ls], pw_ref[base + 24 + last:base + 25 + last, ls]
    lr, li, fr, fi = (t[last:last + 1] for t in (e_r, e_i, s_r, s_i))
    out = (a16r * lr - a16i * li + fr, a16r * li + a16i * lr + fi)
    s_r, s_i = e_r, e_i
    extra = None
    for i in order:
        rows = pl.ds(8 * i, 8)
        if rescan:
            s_r, s_i = ar * s_r - ai * s_i + xr_ref[rows, ls], ar * s_i + ai * s_r + xi_ref[rows, ls]
            f_r, f_i = s_r, s_i
        else:
            s_r, s_i = ar * s_r - ai * s_i, ar * s_i + ai * s_r
            f_r, f_i = xr_ref[rows, ls] + s_r, xi_ref[rows, ls] + s_i
        xr_ref[rows, ls] = f_r
        xi_ref[rows, ls] = f_i
        if visit is not None:
            extra = visit(i, f_r, f_i, extra)
    return out if visit is None else (out, extra)


def ssm_fwd(u, pw, br, bi, cr, ci, dsk, perm):
    lp = u.shape[0]

    def body(u_ref, pw_ref, br_ref, bi_ref, cr_ref, ci_ref, d_ref, p_ref, pt_ref,
             y_ref, sr_ref, si_ref, car_r, car_i, yp_s):
        i = pl.program_id(0)

        @pl.when(i == 0)
        def _():
            car_r[...] = jnp.zeros_like(car_r)
            car_i[...] = jnp.zeros_like(car_i)

        u = u_ref[...]
        ub = _dot(p_ref[...], u.astype(BF)).astype(BF)
        for j in range(NLC):
            uj = ub[:, 128 * j:128 * j + 128]
            sr_ref[:, LC * j:LC * j + LC] = _dot(uj, br_ref[j])
            si_ref[:, LC * j:LC * j + LC] = _dot(uj, bi_ref[j])
        for j in range(NLC):
            ls = slice(LC * j, LC * j + LC)
            car_r[:, ls], car_i[:, ls] = _segment_scan(sr_ref, si_ref, pw_ref, 0, ls, car_r[:, ls], car_i[:, ls],
                                                       False, rescan=False)
        for j in range(NLC):
            ls = slice(LC * j, LC * j + LC)
            yp_s[:, 128 * j:128 * j + 128] = (_dot(sr_ref[:, ls].astype(BF), cr_ref[j])
                                              - _dot(si_ref[:, ls].astype(BF), ci_ref[j]))
        y_ref[...] = _reorder(pt_ref[...], yp_s[...]) + d_ref[...] * u

    p, pt = perm
    return pl.pallas_call(
        body, name="ssm_fwd", grid=(lp // TS,),
        in_specs=[_rows(TS, D), _full((2 * PW_ROWS, SW)), _full((NLC, 128, LC)), _full((NLC, 128, LC)),
                  _full((NLC, LC, 128)), _full((NLC, LC, 128)), _full((1, D)), _full((TS, TS)), _full((TS, TS))],
        out_specs=[_rows(TS, D), _rows(TS, SW), _rows(TS, SW)],
        out_shape=[jax.ShapeDtypeStruct((lp, D), F32), jax.ShapeDtypeStruct((lp, SW), F32),
                   jax.ShapeDtypeStruct((lp, SW), F32)],
        scratch_shapes=[pltpu.VMEM((1, SW), F32), pltpu.VMEM((1, SW), F32), pltpu.VMEM((TS, D), F32)],
        compiler_params=_cp(("arbitrary",)),
    )(u, pw, br, bi, cr, ci, dsk, p, pt)


def ssm_bwd(dy, u, sr, si, pw, br, bi, cr, ci, dsk, perm):
    lp = u.shape[0]
    nt = lp // TS

    def rev(i):
        return (nt - 1 - i, 0)

    def prev8(i):
        return (jnp.maximum((nt - 1 - i) * (TS // 8) - 1, 0), 0)

    def body(dy_ref, u_ref, sr_ref, si_ref, pr8_ref, pi8_ref, pw_ref, br_ref, bi_ref, cr_ref, ci_ref, d_ref,
             p_ref, pt_ref, du_ref, da_ref, dbr_ref, dbi_ref, dcr_ref, dci_ref, dd_ref,
             gr_s, gi_s, car_r, car_i, dup_s):
        i = pl.program_id(0)

        @pl.when(i == 0)
        def _():
            car_r[...] = jnp.zeros_like(car_r)
            car_i[...] = jnp.zeros_like(car_i)
            for ref in (da_ref, dbr_ref, dbi_ref, dcr_ref, dci_ref, dd_ref):
                ref[...] = jnp.zeros_like(ref)

        keep = 1.0 - (i == nt - 1).astype(F32)
        dy = dy_ref[...]
        u = u_ref[...]
        dd_ref[...] += jnp.sum(dy * u, axis=0, keepdims=True)
        pm = p_ref[...]
        dyb = _dot(pm, dy.astype(BF)).astype(BF)
        ub = _dot(pm, u.astype(BF)).astype(BF)
        for j in range(NLC):
            ls = slice(LC * j, LC * j + LC)
            dyj = dyb[:, 128 * j:128 * j + 128]
            gr_s[:, ls] = _dot_nt(dyj, cr_ref[j])
            gi_s[:, ls] = -_dot_nt(dyj, ci_ref[j])
            dcr_ref[j] += _dot_tn(dyj, sr_ref[:, ls].astype(BF))
            dci_ref[j] -= _dot_tn(dyj, si_ref[:, ls].astype(BF))
        rid = lax.broadcasted_iota(jnp.int32, (8, LC), 0)
        for j in range(NLC):
            ls = slice(LC * j, LC * j + LC)

            def accumulate(i, g_r, g_i, acc):
                if i >= 1:
                    rows = pl.ds(8 * (i - 1), 8)
                    p_r, p_i = sr_ref[rows, ls], si_ref[rows, ls]
                else:
                    rows = pl.ds(8 * (SEG - 1), 8)
                    p_r = jnp.where(rid >= 1, pltpu.roll(sr_ref[rows, ls], 1, axis=0), pr8_ref[7:8, ls] * keep)
                    p_i = jnp.where(rid >= 1, pltpu.roll(si_ref[rows, ls], 1, axis=0), pi8_ref[7:8, ls] * keep)
                t_r, t_i = g_r * p_r + g_i * p_i, g_i * p_r - g_r * p_i
                return (t_r, t_i) if acc is None else (acc[0] + t_r, acc[1] + t_i)

            (car_r[:, ls], car_i[:, ls]), (t_r, t_i) = _segment_scan(
                gr_s, gi_s, pw_ref, PW_ROWS, ls, car_r[:, ls], car_i[:, ls], True, accumulate)
            da_ref[0:8, ls] += t_r
            da_ref[8:16, ls] += t_i
        for j in range(NLC):
            ls = slice(LC * j, LC * j + LC)
            uj = ub[:, 128 * j:128 * j + 128]
            grb = gr_s[:, ls].astype(BF)
            gib = gi_s[:, ls].astype(BF)
            dup_s[:, 128 * j:128 * j + 128] = _dot_nt(grb, br_ref[j]) + _dot_nt(gib, bi_ref[j])
            dbr_ref[j] += _dot_tn(uj, grb)
            dbi_ref[j] += _dot_tn(uj, gib)
        du_ref[...] = (_reorder(pt_ref[...], dup_s[...]) + d_ref[...] * dy).astype(BF)

    p, pt = perm
    return pl.pallas_call(
        body, name="ssm_bwd", grid=(nt,),
        in_specs=[pl.BlockSpec((TS, D), rev), pl.BlockSpec((TS, D), rev), pl.BlockSpec((TS, SW), rev),
                  pl.BlockSpec((TS, SW), rev), pl.BlockSpec((8, SW), prev8), pl.BlockSpec((8, SW), prev8),
                  _full((2 * PW_ROWS, SW)), _full((NLC, 128, LC)), _full((NLC, 128, LC)),
                  _full((NLC, LC, 128)), _full((NLC, LC, 128)), _full((1, D)), _full((TS, TS)), _full((TS, TS))],
        out_specs=[pl.BlockSpec((TS, D), rev), _full((16, SW)), _full((NLC, 128, LC)), _full((NLC, 128, LC)),
                   _full((NLC, 128, LC)), _full((NLC, 128, LC)), _full((1, D))],
        out_shape=[jax.ShapeDtypeStruct((lp, D), BF), jax.ShapeDtypeStruct((16, SW), F32),
                   jax.ShapeDtypeStruct((NLC, 128, LC), F32), jax.ShapeDtypeStruct((NLC, 128, LC), F32),
                   jax.ShapeDtypeStruct((NLC, 128, LC), F32), jax.ShapeDtypeStruct((NLC, 128, LC), F32),
                   jax.ShapeDtypeStruct((1, D), F32)],
        scratch_shapes=[pltpu.VMEM((TS, SW), F32), pltpu.VMEM((TS, SW), F32),
                        pltpu.VMEM((1, SW), F32), pltpu.VMEM((1, SW), F32), pltpu.VMEM((TS, D), F32)],
        compiler_params=_cp(("arbitrary",)),
    )(dy, u, sr, si, sr, si, pw, br, bi, cr, ci, dsk, p, pt)


def _merge_parts(attn, zz, gg, abn, sbn):
    za, zb = zz[:, 0:D], zz[:, D:2 * D]
    sgz = _sigmoid(zb)
    ssm = za * sgz
    ra = lax.rsqrt(jnp.mean(attn * attn, axis=-1, keepdims=True) + EPS)
    rs = lax.rsqrt(jnp.mean(ssm * ssm, axis=-1, keepdims=True) + EPS)
    ah, sh = attn * ra, ssm * rs
    sga, sgs = _sigmoid(gg[:, 0:D]), _sigmoid(gg[:, D:2 * D])
    return za, sgz, ra, rs, ah, sh, sga, sgs, ah * abn, sh * sbn


def mid_fwd(y, attn, gg, x, mpad, w_glu, w_out, abn, sbn):
    lp = y.shape[0]

    def body(y_ref, a_ref, gg_ref, x_ref, m_ref, wg_ref, wo_ref, abn_ref, sbn_ref, z_o, zz_o, mg_o, h1_o):
        z, _ = _gelu(y_ref[...])
        zb = z.astype(BF)
        z_o[...] = zb
        zz = _dot(zb, wg_ref[...])
        zz_o[...] = zz.astype(BF)
        _, _, _, _, _, _, sga, sgs, an, sn = _merge_parts(a_ref[...], zz, gg_ref[...], abn_ref[...], sbn_ref[...])
        mb = (sga * an + sgs * sn).astype(BF)
        mg_o[...] = mb
        h1_o[...] = _seq_tile(x_ref, m_ref) + _dot(mb, wo_ref[...])

    return pl.pallas_call(
        body, name="mid_fwd", grid=(lp // TM,),
        in_specs=[_rows(TM, D), _rows(TM, D), _rows(TM, 2 * D)] + _seq_specs() + [_full((D, 2 * D)), _full((D, D)),
                  _full((1, D)), _full((1, D))],
        out_specs=[_rows(TM, D), _rows(TM, 2 * D), _rows(TM, D), _rows(TM, D)],
        out_shape=[jax.ShapeDtypeStruct((lp, D), BF), jax.ShapeDtypeStruct((lp, 2 * D), BF),
                   jax.ShapeDtypeStruct((lp, D), BF), jax.ShapeDtypeStruct((lp, D), F32)],
        compiler_params=_cp(("arbitrary",)),
    )(y, attn, gg, x, mpad, w_glu, w_out, abn, sbn)


def mid_bwd(dh1b, y, attn, gg, zzb, w_glu, w_out, abn, sbn):
    lp = y.shape[0]

    def body(dh_ref, y_ref, a_ref, gg_ref, zz_ref, wg_ref, wo_ref, abn_ref, sbn_ref,
             dy_o, dattn_o, dgg_o, dzz_o, dabn_o, dsbn_o):
        i = pl.program_id(0)

        @pl.when(i == 0)
        def _():
            dabn_o[...] = jnp.zeros_like(dabn_o)
            dsbn_o[...] = jnp.zeros_like(dsbn_o)

        dm = _dot_nt(dh_ref[...], wo_ref[...])
        abn, sbn = abn_ref[...], sbn_ref[...]
        za, sgz, ra, rs, ah, sh, sga, sgs, an, sn = _merge_parts(
            a_ref[...], zz_ref[...].astype(F32), gg_ref[...], abn, sbn)
        da = dm * sga
        dabn_o[...] += jnp.sum(da * ah, axis=0, keepdims=True)
        dah = da * abn
        pa = dah * ah
        dgg_o[:, 0:D] = (pa * (1.0 - sga)).astype(BF)
        dattn_o[...] = (ra * (dah - ah * jnp.mean(pa, axis=-1, keepdims=True))).astype(BF)
        ds = dm * sgs
        dsbn_o[...] += jnp.sum(ds * sh, axis=0, keepdims=True)
        dsh = ds * sbn
        ps = dsh * sh
        dgg_o[:, D:2 * D] = (ps * (1.0 - sgs)).astype(BF)
        dssm = rs * (dsh - sh * jnp.mean(ps, axis=-1, keepdims=True))
        dza = dssm * sgz
        dzz_o[:, 0:D] = dza.astype(BF)
        dzz_o[:, D:2 * D] = (dza * za * (1.0 - sgz)).astype(BF)
        dz = _dot_nt(dzz_o[...], wg_ref[...])
        yv = y_ref[...]
        _, t = _gelu(yv)
        dy_o[...] = dz * _gelu_grad(yv, t)

    return pl.pallas_call(
        body, name="mid_bwd", grid=(lp // TM,),
        in_specs=[_rows(TM, D), _rows(TM, D), _rows(TM, D), _rows(TM, 2 * D), _rows(TM, 2 * D),
                  _full((D, 2 * D)), _full((D, D)), _full((1, D)), _full((1, D))],
        out_specs=[_rows(TM, D), _rows(TM, D), _rows(TM, 2 * D), _rows(TM, 2 * D), _full((1, D)), _full((1, D))],
        out_shape=[jax.ShapeDtypeStruct((lp, D), F32), jax.ShapeDtypeStruct((lp, D), BF),
                   jax.ShapeDtypeStruct((lp, 2 * D), BF), jax.ShapeDtypeStruct((lp, 2 * D), BF),
                   jax.ShapeDtypeStruct((1, D), F32), jax.ShapeDtypeStruct((1, D), F32)],
        compiler_params=_cp(("arbitrary",)),
    )(dh1b, y, attn, gg, zzb, w_glu, w_out, abn, sbn)


def ffn_fwd_bwd(h1, tgt, w_fi, w_fo, gf):
    lp = h1.shape[0]
    tf = 256
    bounds = (0, 1536, DFF)

    def body(h_ref, t_ref, wi_ref, wo_ref, gf_ref, hn_o, dgu_o, act_o, dh2_o, dh1_o, dh1b_o, dgf_o, loss_o,
             gate_s, up_s):
        i = pl.program_id(0)

        @pl.when(i == 0)
        def _():
            dgf_o[...] = jnp.zeros_like(dgf_o)
            loss_o[...] = jnp.zeros_like(loss_o)

        h = h_ref[...]
        r = lax.rsqrt(jnp.mean(h * h, axis=-1, keepdims=True) + EPS)
        xh = h * r
        hb = (xh * gf_ref[...]).astype(BF)
        hn_o[...] = hb
        h2 = h
        for c0, c1 in zip(bounds[:-1], bounds[1:]):
            gate = _dot(hb, wi_ref[:, c0:c1])
            up = _dot(hb, wi_ref[:, DFF + c0:DFF + c1])
            gate_s[:, c0:c1] = gate
            up_s[:, c0:c1] = up
            ab = (gate * _sigmoid(gate) * up).astype(BF)
            act_o[:, c0:c1] = ab
            h2 = h2 + _dot(ab, wo_ref[c0:c1, :])
        real = (i >= PAD // tf).astype(F32)
        e = (h2 - t_ref[...]) * real
        loss_o[...] += 0.5 * jnp.sum(jnp.sum(e * e, axis=-1, keepdims=True), axis=0, keepdims=True) * (1.0 / D)
        dh2 = e * (1.0 / D)
        db = dh2.astype(BF)
        dh2_o[...] = db
        dhn = jnp.zeros((tf, D), F32)
        for c0, c1 in zip(bounds[:-1], bounds[1:]):
            gate = gate_s[:, c0:c1]
            up = up_s[:, c0:c1]
            sg = _sigmoid(gate)
            dact = _dot_nt(db, wo_ref[c0:c1, :])
            dgate = (dact * up * (sg * (1.0 + gate * (1.0 - sg)))).astype(BF)
            dup = (dact * (gate * sg)).astype(BF)
            dgu_o[:, c0:c1] = dgate
            dgu_o[:, DFF + c0:DFF + c1] = dup
            dhn += _dot_nt(dgate, wi_ref[:, c0:c1]) + _dot_nt(dup, wi_ref[:, DFF + c0:DFF + c1])
        dgf_o[...] += jnp.sum(dhn * xh, axis=0, keepdims=True)
        dxh = dhn * gf_ref[...]
        dh1 = dh2 + r * (dxh - xh * jnp.mean(dxh * xh, axis=-1, keepdims=True))
        dh1_o[...] = dh1
        dh1b_o[...] = dh1.astype(BF)

    return pl.pallas_call(
        body, name="ffn_fwd_bwd", grid=(lp // tf,),
        in_specs=[_rows(tf, D), pl.BlockSpec((tf, D), lambda i: (jnp.maximum(i - PAD // tf, 0), 0)),
                  _const((D, 2 * DFF)), _const((DFF, D)), _full((1, D))],
        out_specs=[_rows(tf, D), _rows(tf, 2 * DFF), _rows(tf, DFF), _rows(tf, D), _rows(tf, D), _rows(tf, D),
                   _full((1, D)), _full((1, 1))],
        out_shape=[jax.ShapeDtypeStruct((lp, D), BF), jax.ShapeDtypeStruct((lp, 2 * DFF), BF),
                   jax.ShapeDtypeStruct((lp, DFF), BF), jax.ShapeDtypeStruct((lp, D), BF),
                   jax.ShapeDtypeStruct((lp, D), F32), jax.ShapeDtypeStruct((lp, D), BF),
                   jax.ShapeDtypeStruct((1, D), F32), jax.ShapeDtypeStruct((1, 1), F32)],
        scratch_shapes=[pltpu.VMEM((tf, DFF), F32), pltpu.VMEM((tf, DFF), F32)],
        compiler_params=_cp(("arbitrary",), vmem_mb=58),
    )(h1, tgt, w_fi, w_fo, gf)


def weight_grad(name, a, b, tn, with_bf16=False):
    lp, k = a.shape
    n = b.shape[1]
    tr = next(t for t in (2112, 1056, 768, 512, 384, 256, 128) if lp % t == 0 and t * k * 2 <= (5 << 20))

    def body(a_ref, b_ref, o_ref, *ob_ref):
        @pl.when(pl.program_id(1) == 0)
        def _():
            o_ref[...] = jnp.zeros_like(o_ref)

        o_ref[...] += _dot_tn(a_ref[...], b_ref[...])
        if with_bf16:
            @pl.when(pl.program_id(1) == lp // tr - 1)
            def _():
                ob_ref[0][...] = o_ref[...].astype(BF)

    spec = pl.BlockSpec((k, tn), lambda j, m: (0, j))
    return pl.pallas_call(
        body, name=name, grid=(n // tn, lp // tr),
        in_specs=[pl.BlockSpec((tr, k), lambda j, m: (m, 0)), pl.BlockSpec((tr, tn), lambda j, m: (m, j))],
        out_specs=[spec, spec] if with_bf16 else spec,
        out_shape=([jax.ShapeDtypeStruct((k, n), F32), jax.ShapeDtypeStruct((k, n), BF)] if with_bf16
                   else jax.ShapeDtypeStruct((k, n), F32)),
        compiler_params=_cp(("arbitrary", "arbitrary")),
    )(a, b)


def _adamw_math(w, g, m, v):
    m = B1 * m + (1.0 - B1) * g
    v = B2 * v + (1.0 - B2) * (g * g)
    delta = -LR * ((m / BC1) / (jnp.sqrt(v / BC2) + AEPS) + WD * w)
    return delta, m, v


def _row_tile(r):
    return next((t for t in (256, 128, 64, 32, 16, 8) if r % t == 0), r)


def adamw(name, w, g, m, v):
    r, c = w.shape
    tr = _row_tile(r)

    def body(w_ref, g_ref, m_ref, v_ref, g_o, d_o, m_o, v_o):
        g = g_ref[...]
        g_o[...] = g
        d_o[...], m_o[...], v_o[...] = _adamw_math(w_ref[...], g, m_ref[...], v_ref[...])

    spec = pl.BlockSpec((tr, c), lambda i: (i, 0))
    return pl.pallas_call(
        body, name=name, grid=(r // tr,), in_specs=[spec] * 4, out_specs=[spec] * 4,
        out_shape=[jax.ShapeDtypeStruct((r, c), F32)] * 4,
        compiler_params=_cp(("arbitrary",)),
    )(w, g, m, v)


def adamw_small(ws, gs, ms, vs):
    n = len(ws)
    steps = 8

    def spec(a):
        if a.ndim == 4:
            return pl.BlockSpec((1, a.shape[1] // steps) + a.shape[2:], lambda i: (0, i, 0, 0))
        nd = a.ndim
        return pl.BlockSpec(a.shape, lambda i: (0,) * nd)

    def body(*refs):
        w, g, m, v, d_o, m_o, v_o = (refs[k * n:(k + 1) * n] for k in range(7))
        for t in range(n):
            d_o[t][...], m_o[t][...], v_o[t][...] = _adamw_math(w[t][...], g[t][...], m[t][...], v[t][...])

    specs = [spec(a) for a in ws]
    res = pl.pallas_call(
        body, name="adamw_small", grid=(steps,), in_specs=specs * 4, out_specs=specs * 3,
        out_shape=[jax.ShapeDtypeStruct(a.shape, F32) for a in ws] * 3,
        compiler_params=_cp(("arbitrary",)),
    )(*ws, *gs, *ms, *vs)
    return res[:n], res[n:2 * n], res[2 * n:]


def pair_sum(name, place, full, got, ax):
    r, c = got.shape
    tr = _row_tile(r)

    def body(s_ref, a_ref, b_ref, o_ref):
        o_ref[...] = (a_ref[...] + b_ref[...]).astype(BF)

    if ax == 1:
        mine = pl.BlockSpec((tr, c), lambda i, s: (s[1] * (r // tr) + i, 0))
    else:
        mine = pl.BlockSpec((tr, c), lambda i, s: (i, s[1]))
    spec = pl.BlockSpec((tr, c), lambda i, s: (i, 0))
    return pl.pallas_call(
        body, name=name,
        grid_spec=pltpu.PrefetchScalarGridSpec(num_scalar_prefetch=1, grid=(r // tr,), in_specs=[mine, spec],
                                               out_specs=spec),
        out_shape=jax.ShapeDtypeStruct((r, c), BF),
        compiler_params=_cp(("arbitrary",)),
    )(place, full, got)


def sum4(name, place, parts, half, ax):
    _, r, c = parts.shape
    tr = _row_tile(r)

    def body(s_ref, p_ref, h_ref, o_ref):
        o_ref[...] = ((p_ref[0].astype(F32) + p_ref[1].astype(F32))
                      + (p_ref[2].astype(F32) + h_ref[...].astype(F32)))

    if ax == 1:
        own = pl.BlockSpec((tr, c), lambda i, s: (i, s[0]))
        out = pl.BlockSpec((tr, c), lambda i, s: (s[1] * (r // tr) + i, 0))
        shape = (2 * r, c)
    else:
        own = pl.BlockSpec((tr, c), lambda i, s: (s[0] * (r // tr) + i, 0))
        out = pl.BlockSpec((tr, c), lambda i, s: (i, s[1]))
        shape = (r, 2 * c)
    return pl.pallas_call(
        body, name=name,
        grid_spec=pltpu.PrefetchScalarGridSpec(
            num_scalar_prefetch=1, grid=(r // tr,),
            in_specs=[pl.BlockSpec((3, tr, c), lambda i, s: (0, i, 0)), own], out_specs=out),
        out_shape=jax.ShapeDtypeStruct(shape, F32),
        compiler_params=_cp(("arbitrary",)),
    )(place, parts, half)


def sum8(name, place, parts, full, ax):
    _, r, c = parts.shape
    tr = _row_tile(r)

    def body(s_ref, p_ref, g_ref, o_ref):
        p = [p_ref[k].astype(F32) for k in range(7)]
        o_ref[...] = ((p[0] + p[1]) + (p[2] + p[3])) + ((p[4] + p[5]) + (p[6] + g_ref[...]))

    if ax == 1:
        own = pl.BlockSpec((tr, c), lambda i, s: (s[1] * (r // tr) + i, s[0]))
        out = pl.BlockSpec((tr, c), lambda i, s: (s[1] * (r // tr) + i, 0))
        shape = (2 * r, c)
    else:
        own = pl.BlockSpec((tr, c), lambda i, s: (s[0] * (r // tr) + i, s[1]))
        out = pl.BlockSpec((tr, c), lambda i, s: (i, s[1]))
        shape = (r, 2 * c)
    return pl.pallas_call(
        body, name=name,
        grid_spec=pltpu.PrefetchScalarGridSpec(
            num_scalar_prefetch=1, grid=(r // tr,),
            in_specs=[pl.BlockSpec((7, tr, c), lambda i, s: (0, i, 0)), own], out_specs=out),
        out_shape=jax.ShapeDtypeStruct(shape, F32),
        compiler_params=_cp(("arbitrary",)),
    )(place, parts, full)


def _place():
    x, y, c = lax.axis_index("x"), lax.axis_index("y"), lax.axis_index("c")
    return x, y, c


def _other_chips(x, y):
    return [(1 - x, y), (x, 1 - y), (1 - x, 1 - y)]


SHARDED = (("w_in", D, IN_COLS, 1), ("w_glu", D, 2 * D, 1), ("w_out", D, D, 0),
           ("w_ffn_in", D, 2 * DFF, 1), ("w_ffn_out", DFF, D, 0))


def _shard_of(ref, axis, k, size):
    if axis == 0:
        return ref.at[pl.ds(k * size, size), :]
    return ref.at[:, pl.ds(k * size, size)]


def _gather_piece(full, dims, k, h):
    r, cc, ax = dims
    if ax == 0:
        return full.at[pl.ds(k * (r // 4) + h * (r // 8), r // 8), :]
    return full.at[pl.ds(h * (r // 2), r // 2), pl.ds(k * (cc // 4), cc // 4)]


def _rows_half(shard, h):
    rs = shard.shape[0]
    return shard.at[pl.ds(h * (rs // 2), rs // 2), :]


SEM = pl.BlockSpec(memory_space=pltpu.SEMAPHORE)
HBM = pl.BlockSpec(memory_space=pltpu.HBM)
EFFECT = pltpu.SideEffectType.DATAFLOW_SIDE_EFFECTING


def _in_hbm(a):
    return pltpu.with_memory_space_constraint(a, pltpu.HBM)


def _split_copy_start(name, copies, ncopies, srcs, lands, after):
    ns, nl = len(srcs), len(lands)

    def body(*refs):
        ins, land_refs = refs[:ns], refs[ns:ns + nl]
        send_sems, recv_sems = refs[ns + nl + 1], refs[ns + nl + 2]
        token = refs[-1]
        for k, (src, dst, dev) in enumerate(copies(ins, land_refs)):
            pltpu.make_async_remote_copy(src_ref=src, dst_ref=dst, send_sem=send_sems.at[k],
                                         recv_sem=recv_sems.at[k], device_id=dev, device_id_type=MESH).start()
        token[...] = jnp.zeros_like(token)

    res = pl.pallas_call(
        body, name=name,
        in_specs=[HBM] * (ns + nl) + [ANY],
        out_specs=(SEM, SEM) + (HBM,) * (ns + nl) + (pl.BlockSpec(memory_space=pltpu.VMEM),),
        out_shape=(pltpu.SemaphoreType.DMA((ncopies,)), pltpu.SemaphoreType.DMA((ncopies,)))
        + tuple(pltpu.HBM(a.shape, a.dtype) for a in list(srcs) + list(lands))
        + (jax.ShapeDtypeStruct((8, 128), F32),),
        input_output_aliases={t: 2 + t for t in range(ns + nl)},
        compiler_params=pltpu.CompilerParams(has_side_effects=EFFECT),
    )(*[_in_hbm(a) for a in srcs], *[_in_hbm(a) for a in lands], after)
    return res[0], res[1], list(res[2:2 + ns]), list(res[2 + ns:2 + ns + nl]), res[-1]


def _split_copy_wait(name, copies, arrivals, send_sems, recv_sems, srcs, lands, after):
    ns, nl = len(srcs), len(lands)

    def body(*refs):
        ins, land_refs = refs[:ns], refs[ns:ns + nl]
        send_sems_ref, recv_sems_ref = refs[ns + nl], refs[ns + nl + 1]
        mine = copies(ins, land_refs)
        for k, ((src, dst, dev), got) in enumerate(zip(mine, arrivals(ins, land_refs))):
            pltpu.make_async_remote_copy(src_ref=src, dst_ref=dst, send_sem=send_sems_ref.at[k],
                                         recv_sem=recv_sems_ref.at[k], device_id=dev,
                                         device_id_type=MESH).wait_send()
            pltpu.make_async_remote_copy(src_ref=got, dst_ref=got, send_sem=send_sems_ref.at[k],
                                         recv_sem=recv_sems_ref.at[k], device_id=dev,
                                         device_id_type=MESH).wait_recv()

    res = pl.pallas_call(
        body, name=name,
        in_specs=[HBM] * (ns + nl) + [SEM, SEM, ANY],
        out_specs=(HBM,) * (ns + nl),
        out_shape=tuple(pltpu.HBM(a.shape, a.dtype) for a in list(srcs) + list(lands)),
        input_output_aliases={t: t for t in range(ns + nl)},
        compiler_params=pltpu.CompilerParams(has_side_effects=EFFECT),
    )(*srcs, *lands, send_sems, recv_sems, after)
    return list(res[:ns]), list(res[ns:])


def _gather_copies(dims):
    def copies(ins, lands):
        x, y, c = _place()
        mine = 2 * x + y
        out = []
        for t, d in enumerate(dims):
            size = (d[0] if d[2] == 0 else d[1]) // 4
            for px, py in _other_chips(x, y):
                for dc in range(2):
                    out.append((_rows_half(ins[t], c), _gather_piece(lands[t], d, mine, c), (px, py, c ^ dc)))
            out.append((ins[t], _shard_of(lands[t], d[2], mine, size), (x, y, 1 - c)))
        return out

    def arrivals(ins, lands):
        x, y, c = _place()
        mine = 2 * x + y
        out = []
        for t, d in enumerate(dims):
            size = (d[0] if d[2] == 0 else d[1]) // 4
            for px, py in _other_chips(x, y):
                for dc in range(2):
                    out.append(_gather_piece(lands[t], d, 2 * px + py, c ^ dc))
            out.append(_shard_of(lands[t], d[2], mine, size))
        return out

    return copies, arrivals


def _half_of(ref, ax, h):
    r, c = ref.shape
    if ax == 1:
        return ref.at[pl.ds(h * (r // 2), r // 2), :]
    return ref.at[:, pl.ds(h * (c // 2), c // 2)]


def _half_shape(r, c, ax):
    return (r // 2, c) if ax == 1 else (r, c // 2)


def pair_exchange(name, grads, idxs):
    n = len(grads)
    which = [SHARDED[i] for i in idxs]

    def body(*refs):
        ins, got = refs[:n], refs[n:2 * n]
        send_sems, recv_sems = refs[2 * n:]
        x, y, c = _place()
        cps = []
        for t, (_, _, _, ax) in enumerate(which):
            cp = pltpu.make_async_remote_copy(
                src_ref=_half_of(ins[t], ax, 1 - c), dst_ref=got[t], send_sem=send_sems.at[t],
                recv_sem=recv_sems.at[t], device_id=(x, y, 1 - c), device_id_type=MESH)
            cp.start()
            cps.append(cp)
        for cp in cps:
            cp.wait()

    return pl.pallas_call(
        body, name=name,
        in_specs=[ANY] * n, out_specs=[ANY] * n,
        out_shape=[jax.ShapeDtypeStruct(_half_shape(r, c, ax), F32) for _, r, c, ax in which],
        scratch_shapes=[pltpu.SemaphoreType.DMA((n,)), pltpu.SemaphoreType.DMA((n,))],
        compiler_params=pltpu.CompilerParams(has_side_effects=True),
    )(*grads)


def _piece_shapes(idxs):
    out = []
    for _, r, c, ax in [SHARDED[i] for i in idxs]:
        hr, hc = _half_shape(r, c, ax)
        out.append((hr // 4, hc) if ax == 0 else (hr, hc // 4))
    return out


def _piece_of(full, ax, k, h):
    r, c = full.shape
    if ax == 1:
        return full.at[pl.ds(h * (r // 2), r // 2), pl.ds(k * (c // 4), c // 4)]
    return full.at[pl.ds(k * (r // 4), r // 4), pl.ds(h * (c // 2), c // 2)]


def _scatter8_copies(idxs):
    which = [SHARDED[i] for i in idxs]

    def copies(ins, lands):
        x, y, c = _place()
        out = []
        for t, (_, _, _, ax) in enumerate(which):
            for j, (px, py) in enumerate(_other_chips(x, y)):
                for d in range(2):
                    out.append((_piece_of(ins[t], ax, 2 * px + py, c ^ d), lands[t].at[2 * j + d], (px, py, c ^ d)))
            out.append((_piece_of(ins[t], ax, 2 * x + y, 1 - c), lands[t].at[6], (x, y, 1 - c)))
        return out

    def arrivals(ins, lands):
        return [lands[t].at[k] for t in range(len(which)) for k in range(7)]

    return copies, arrivals


def scatter_grads(halves, idxs):
    n = len(halves)
    which = [SHARDED[i] for i in idxs]

    def shapes():
        return _piece_shapes(idxs)

    def body(*refs):
        ins, outs = refs[:n], refs[n:2 * n]
        send_sems, recv_sems = refs[2 * n:]
        x, y, c = _place()
        started = []
        for t, (src, dst, (_, _, _, ax)) in enumerate(zip(ins, outs, which)):
            size = src.shape[ax] // 4
            for j, (px, py) in enumerate(_other_chips(x, y)):
                cp = pltpu.make_async_remote_copy(
                    src_ref=_shard_of(src, ax, 2 * px + py, size), dst_ref=dst.at[j],
                    send_sem=send_sems.at[3 * t + j], recv_sem=recv_sems.at[3 * t + j],
                    device_id=(px, py, c), device_id_type=MESH)
                cp.start()
                started.append(cp)
        for cp in started:
            cp.wait()

    return pl.pallas_call(
        body, name="scatter_grads",
        in_specs=[ANY] * n, out_specs=[ANY] * n,
        out_shape=[jax.ShapeDtypeStruct((3,) + s, BF) for s in shapes()],
        scratch_shapes=[pltpu.SemaphoreType.DMA((3 * n,)), pltpu.SemaphoreType.DMA((3 * n,))],
        compiler_params=pltpu.CompilerParams(has_side_effects=True),
    )(*halves)


def join_halves(name, shards, idxs):
    n = len(shards)
    which = [SHARDED[i] for i in idxs]

    def body(*refs):
        ins, outs = refs[:n], refs[n:2 * n]
        send_sems, recv_sems = refs[2 * n:]
        x, y, c = _place()
        cps = []
        for t, (_, _, _, ax) in enumerate(which):
            cp = pltpu.make_async_remote_copy(
                src_ref=_half_of(ins[t], ax, c), dst_ref=_half_of(outs[t], ax, c), send_sem=send_sems.at[t],
                recv_sem=recv_sems.at[t], device_id=(x, y, 1 - c), device_id_type=MESH)
            cp.start()
            cps.append(cp)
        for t, cp in enumerate(cps):
            cp.wait_send()
            theirs = _half_of(outs[t], which[t][3], 1 - c)
            pltpu.make_async_remote_copy(
                src_ref=theirs, dst_ref=theirs, send_sem=send_sems.at[t], recv_sem=recv_sems.at[t],
                device_id=(x, y, 1 - c), device_id_type=MESH).wait_recv()

    return pl.pallas_call(
        body, name=name,
        in_specs=[ANY] * n, out_specs=[ANY] * n,
        out_shape=[jax.ShapeDtypeStruct(s.shape, F32) for s in shards],
        input_output_aliases={t: t for t in range(n)},
        scratch_shapes=[pltpu.SemaphoreType.DMA((n,)), pltpu.SemaphoreType.DMA((n,))],
        compiler_params=pltpu.CompilerParams(has_side_effects=True),
    )(*shards)


def allreduce_small(pack):
    rows = pack.shape[0]
    hr = rows // 2

    def body(p_ref, o_ref, sib_ref, parts_ref, send_sems, recv_sems):
        x, y, c = _place()
        mine = 2 * x + y
        sibling = (x, y, 1 - c)
        swap = pltpu.make_async_remote_copy(
            src_ref=p_ref, dst_ref=sib_ref, send_sem=send_sems.at[0], recv_sem=recv_sems.at[0],
            device_id=sibling, device_id_type=MESH)
        swap.start()
        swap.wait()
        half = pl.ds(pl.multiple_of(c * hr, 8), hr)
        parts_ref[mine] = p_ref[half, :] + sib_ref[half, :]
        cps = []
        for j, chip in enumerate(_other_chips(x, y)):
            cp = pltpu.make_async_remote_copy(
                src_ref=parts_ref.at[mine], dst_ref=parts_ref.at[mine], send_sem=send_sems.at[1 + j],
                recv_sem=recv_sems.at[1 + j], device_id=(*chip, c), device_id_type=MESH)
            cp.start()
            cps.append(cp)
        for j, (px, py) in enumerate(_other_chips(x, y)):
            cps[j].wait_send()
            theirs = parts_ref.at[2 * px + py]
            pltpu.make_async_remote_copy(
                src_ref=theirs, dst_ref=theirs, send_sem=send_sems.at[1 + j], recv_sem=recv_sems.at[1 + j],
                device_id=(px, py, c), device_id_type=MESH).wait_recv()
        o_ref[half, :] = (parts_ref[0] + parts_ref[1]) + (parts_ref[2] + parts_ref[3])
        back = pltpu.make_async_remote_copy(
            src_ref=o_ref.at[half, :], dst_ref=o_ref.at[half, :], send_sem=send_sems.at[4],
            recv_sem=recv_sems.at[4], device_id=sibling, device_id_type=MESH)
        back.start()
        back.wait_send()
        other = pl.ds(pl.multiple_of((1 - c) * hr, 8), hr)
        pltpu.make_async_remote_copy(
            src_ref=o_ref.at[other, :], dst_ref=o_ref.at[other, :], send_sem=send_sems.at[4],
            recv_sem=recv_sems.at[4], device_id=sibling, device_id_type=MESH).wait_recv()

    return pl.pallas_call(
        body, name="allreduce_small",
        in_specs=[pl.BlockSpec(memory_space=pltpu.VMEM)], out_specs=pl.BlockSpec(memory_space=pltpu.VMEM),
        out_shape=jax.ShapeDtypeStruct((rows, 128), F32),
        scratch_shapes=[pltpu.VMEM((rows, 128), F32), pltpu.VMEM((4, hr, 128), F32),
                        pltpu.SemaphoreType.DMA((5,)), pltpu.SemaphoreType.DMA((5,))],
        compiler_params=pltpu.CompilerParams(has_side_effects=True, vmem_limit_bytes=40 << 20),
    )(pack)


def _ssm_discretize(lam_re, lam_im, log_dt, b_re, b_im):
    dt = jnp.exp(log_dt)[:, None]
    mag = jnp.exp(lam_re * dt)
    ar, ai = mag * jnp.cos(lam_im * dt), mag * jnp.sin(lam_im * dt)
    den = lam_re * lam_re + lam_im * lam_im
    nr, ni = ar - 1.0, ai
    fr, fi = (nr * lam_re + ni * lam_im) / den, (ni * lam_re - nr * lam_im) / den
    bbr = fr[:, None, :] * b_re - fi[:, None, :] * b_im
    bbi = fr[:, None, :] * b_im + fi[:, None, :] * b_re
    return ar, ai, bbr, bbi


def _cmul(a, b):
    return a[0] * b[0] - a[1] * b[1], a[0] * b[1] + a[1] * b[0]


def _scan_powers(ar, ai):
    a = (ar.reshape(1, SW), ai.reshape(1, SW))
    big = a
    for _ in range(SEG.bit_length() - 1):
        big = _cmul(big, big)
    assert 1 << (SEG.bit_length() - 1) == SEG
    pows = {1: big}
    pows[2] = _cmul(big, big)
    pows[4] = _cmul(pows[2], pows[2])
    rid = jnp.arange(8)[:, None]
    ones = jnp.ones((8, 1), F32)
    fwd, rev = [ones * a[0], ones * a[1]], [ones * a[0], -ones * a[1]]
    for k in (1, 2, 4):
        pr, pi = pows[k]
        fwd += [jnp.where(rid >= k, pr, 0.0), jnp.where(rid >= k, pi, 0.0)]
        rev += [jnp.where(rid < 8 - k, pr, 0.0), jnp.where(rid < 8 - k, -pi, 0.0)]
    return jnp.concatenate(fwd + rev, axis=0)


def _pack_b(bb):
    t = bb.reshape(NLC, 8, GCH, NSTATE)
    return jnp.einsum("jgcp,gh->jgchp", t, jnp.eye(8, dtype=bb.dtype)).reshape(NLC, 128, LC)


def _unpack_b(db):
    t = jnp.einsum("jgchp,gh->jgcp", db.reshape(NLC, 8, GCH, 8, NSTATE), jnp.eye(8, dtype=db.dtype))
    return t.reshape(GROUPS, GCH, NSTATE)


def _pack_c(cc):
    t = cc.reshape(NLC, 8, GCH, NSTATE).transpose(0, 1, 3, 2)
    return jnp.einsum("jgpc,gh->jgphc", t, jnp.eye(8, dtype=cc.dtype)).reshape(NLC, LC, 128)


def _unpack_c(dc):
    t = jnp.einsum("jgphc,gh->jgpc", dc.reshape(NLC, 8, NSTATE, 8, GCH), jnp.eye(8, dtype=dc.dtype))
    return t.transpose(0, 1, 3, 2).reshape(GROUPS, GCH, NSTATE)


SMALL = ("norm_mix", "q_norm", "k_norm", "attn_sinks", "lam_re", "lam_im", "log_dt", "ssm_b_re", "ssm_b_im",
         "ssm_c_re", "ssm_c_im", "ssm_d", "attn_branch_norm", "ssm_branch_norm", "norm_ffn")


def _pack_small(arrs, rows=None):
    flat = jnp.concatenate([a.reshape(-1).astype(F32) for a in arrs])
    n = flat.shape[0]
    if rows is None:
        rows = -(-n // (128 * 256)) * 256
    return jnp.pad(flat, (0, rows * 128 - n)).reshape(rows, 128)


def _unpack_small(pack, like):
    flat = pack.reshape(-1)
    out, off = [], 0
    for a in like:
        out.append(flat[off:off + a.size].reshape(a.shape))
        off += a.size
    return out


def _gather_half_copies(dims):
    def copies(ins, lands):
        x, y, c = _place()
        mine = 2 * x + y
        out = []
        for t, d in enumerate(dims):
            size = (d[0] if d[2] == 0 else d[1]) // 4
            for chip in _other_chips(x, y):
                out.append((_rows_half(ins[t], c), _gather_piece(lands[t], d, mine, c), (*chip, c)))
            out.append((ins[t], _shard_of(lands[t], d[2], mine, size), (x, y, 1 - c)))
        return out

    def arrivals(ins, lands):
        x, y, c = _place()
        mine = 2 * x + y
        out = []
        for t, d in enumerate(dims):
            size = (d[0] if d[2] == 0 else d[1]) // 4
            for px, py in _other_chips(x, y):
                out.append(_gather_piece(lands[t], d, 2 * px + py, c))
            out.append(_shard_of(lands[t], d[2], mine, size))
        return out

    return copies, arrivals


def gather_forward(fulls, dims):
    n = len(fulls)

    def body(*refs):
        ins, outs = refs[:n], refs[n:2 * n]
        send_sems, recv_sems = refs[2 * n:]
        x, y, c = _place()
        cps = []
        for t in range(n):
            for j, (px, py) in enumerate(_other_chips(x, y)):
                cp = pltpu.make_async_remote_copy(
                    src_ref=_gather_piece(ins[t], dims[t], 2 * px + py, c),
                    dst_ref=_gather_piece(outs[t], dims[t], 2 * px + py, c),
                    send_sem=send_sems.at[3 * t + j], recv_sem=recv_sems.at[3 * t + j],
                    device_id=(x, y, 1 - c), device_id_type=MESH)
                cp.start()
                cps.append(cp)
        for t in range(n):
            for j, (px, py) in enumerate(_other_chips(x, y)):
                cps[3 * t + j].wait_send()
                other = _gather_piece(outs[t], dims[t], 2 * px + py, 1 - c)
                pltpu.make_async_remote_copy(
                    src_ref=other, dst_ref=other, send_sem=send_sems.at[3 * t + j], recv_sem=recv_sems.at[3 * t + j],
                    device_id=(x, y, 1 - c), device_id_type=MESH).wait_recv()

    return pl.pallas_call(
        body, name="gather_forward",
        in_specs=[ANY] * n, out_specs=[ANY] * n,
        out_shape=[jax.ShapeDtypeStruct(a.shape, a.dtype) for a in fulls],
        input_output_aliases={t: t for t in range(n)},
        scratch_shapes=[pltpu.SemaphoreType.DMA((3 * n,)), pltpu.SemaphoreType.DMA((3 * n,))],
        compiler_params=pltpu.CompilerParams(has_side_effects=True),
    )(*fulls)


def _tie(a, token):
    return a if token is None else a + token[0, 0]


class _FirstGather:
    dims = [SHARDED[0][1:], (N_META, D, 1)]

    def __init__(self, w_in_shard, meta_shard):
        self.copies, self.arrivals = _gather_half_copies(self.dims)
        lands = [lax.empty((D, IN_COLS), BF), lax.empty((N_META, D), F32)]
        self.send, self.recv, self.srcs, self.lands, self.token = _split_copy_start(
            "gather_start_first", self.copies, 8, [w_in_shard, meta_shard], lands, w_in_shard)

    def finish(self, after):
        _, lands = _split_copy_wait("gather_wait_first", self.copies, self.arrivals, self.send, self.recv, self.srcs,
                                    self.lands, after)
        return gather_forward(lands, self.dims)


class _Exchange:
    def __init__(self, rest_shards, place):
        self.rest_shards = rest_shards
        self.place = place
        self.rest_dims = [s[1:] for s in SHARDED[1:]]
        self.pending = {}
        self.halves = {}
        self.full = {}
        self.parts = {}

    def begin(self, after):
        copies, arrivals = _gather_copies(self.rest_dims)
        lands = [lax.empty((r, c), BF) for r, c, _ in self.rest_dims]
        send, recv, srcs, lands, token = _split_copy_start(
            "gather_start", copies, 7 * len(lands), self.rest_shards, lands, after)
        self.pending["gather"] = (copies, arrivals, send, recv, srcs, lands)
        return token

    def rest_weights(self, after):
        copies, arrivals, send, recv, srcs, lands = self.pending.pop("gather")
        _, fulls = _split_copy_wait("gather_wait", copies, arrivals, send, recv, srcs, lands, after)
        return dict(zip([s[0] for s in SHARDED[1:]], fulls))

    def _halves(self, group, idxs, grads):
        got = pair_exchange("pair_exchange_" + group, grads, idxs)
        return [pair_sum("pair_sum_" + SHARDED[i][0], self.place, g, b, SHARDED[i][3])
                for i, g, b in zip(idxs, grads, got)]

    def start(self, group, idxs, grads):
        if grads[0][1] is None:
            halves = self._halves(group, idxs, [g for g, _ in grads])
            for i, h, pt in zip(idxs, halves, scatter_grads(halves, idxs)):
                self.halves[i], self.parts[i] = h, pt
            return None
        copies, arrivals = _scatter8_copies(idxs)
        lands = [lax.empty((7,) + sh, BF) for sh in _piece_shapes(idxs)]
        send, recv, srcs, lands, token = _split_copy_start(
            "scatter_start_" + group, copies, 7 * len(idxs), [g16 for _, g16 in grads], lands, grads[0][0])
        self.pending[group] = (copies, arrivals, send, recv, srcs, lands, idxs)
        for i, (g32, _) in zip(idxs, grads):
            self.full[i] = g32
        return token

    def finish(self, group, after):
        if group not in self.pending:
            return
        copies, arrivals, send, recv, srcs, lands, idxs = self.pending.pop(group)
        _, lands = _split_copy_wait("scatter_wait_" + group, copies, arrivals, send, recv, srcs, lands, after)
        for i, pt in zip(idxs, lands):
            self.parts[i] = pt


def local_step(x, tgt, p, first, ex):
    qn_t = jnp.tile(p["q_norm"].reshape(1, HEAD), (1, NQ)) * HEAD ** -0.5
    kn_t = jnp.tile(p["k_norm"].reshape(1, HEAD), (1, NKV))
    seg = jnp.arange(256) // HEAD
    bd = (seg[:, None] == seg[None, :]).astype(BF)
    nm = p["norm_mix"].reshape(1, D)
    sinks = p["attn_sinks"].reshape(NQ)
    dsk = p["ssm_d"].reshape(1, D)
    abn, sbn, gf = (p[k].reshape(1, D) for k in ("attn_branch_norm", "ssm_branch_norm", "norm_ffn"))

    disc_in = (p["lam_re"], p["lam_im"], _tie(p["log_dt"], first.token), p["ssm_b_re"], p["ssm_b_im"])
    (ar, ai, bbr, bbi), disc_vjp = jax.vjp(_ssm_discretize, *disc_in)
    pw = _scan_powers(ar, ai)
    brp, bip = _pack_b(bbr.astype(BF)), _pack_b(bbi.astype(BF))
    crp = _pack_c(_tie(p["ssm_c_re"], first.token).astype(BF))
    cip = _pack_c(_tie(p["ssm_c_im"], first.token).astype(BF))

    w_in, meta_full = first.finish(brp)
    mpad = jnp.concatenate([jnp.zeros((META0, D), F32), meta_full], axis=0)
    token = ex.begin(w_in)
    xn, qr, kr, qn, kn, v, u, gg = in_proj_fwd(x, mpad, w_in, _tie(nm, token), qn_t, kn_t, bd)
    attn = attention_fwd(sinks, qn, kn, v)
    perm = _segment_order()
    y, sr, si = ssm_fwd(u, pw, brp, bip, crp, cip, dsk, perm)
    w = ex.rest_weights(y)
    zb, zzb, mgb, h1 = mid_fwd(y, attn, gg, x, mpad, w["w_glu"], w["w_out"], abn, sbn)
    hnb, dgub, actb, dh2b, dh1, dh1b, dgf, loss = ffn_fwd_bwd(h1, tgt, w["w_ffn_in"], w["w_ffn_out"], gf)
    token = ex.start("ffn", (3, 4), [weight_grad("grad_w_ffn_in", hnb, dgub, 1408, True),
                                     weight_grad("grad_w_ffn_out", actb, dh2b, 512, True)])
    dy, dattn, dggb, dzzb, dabn, dsbn = mid_bwd(dh1b, y, attn, gg, zzb, w["w_glu"], w["w_out"], _tie(abn, token),
                                                sbn)
    grads_mid = [weight_grad("grad_w_glu", zb, dzzb, 1024, True), weight_grad("grad_w_out", mgb, dh1b, 1024, True)]
    dub, da, dbr, dbi, dcr, dci, dd = ssm_bwd(dy, u, sr, si, pw, brp, bip, crp, cip, dsk, perm)
    ex.finish("ffn", dub)
    token = ex.start("mid", (1, 2), grads_mid)
    dqn, dkn, dv, dsink = attention_bwd(_tie(sinks, token), qn, kn, v, dattn)
    dproj, dx, dmpad, dnm, dqw, dkw = in_proj_bwd(dqn, dkn, dv, dub, dggb, qr, kr, x, mpad, dh1, w_in, nm, qn_t,
                                                  kn_t, bd)
    ex.finish("mid", dproj)
    ex.start("in", (0,), [(weight_grad("grad_w_in", xn, dproj, 1536), None)])

    dar = jnp.sum(da[0:8], axis=0).reshape(GROUPS, NSTATE)
    dai = jnp.sum(da[8:16], axis=0).reshape(GROUPS, NSTATE)
    dlr, dli, dldt, dbre, dbim = disc_vjp((dar, dai, _unpack_b(dbr), _unpack_b(dbi)))
    small = {
        "norm_mix": dnm, "q_norm": dqw.reshape(NQ, HEAD).sum(0) * HEAD ** -0.5, "k_norm": dkw.reshape(NKV, HEAD).sum(0),
        "attn_sinks": dsink[0, 0:NQ], "lam_re": dlr, "lam_im": dli, "log_dt": dldt,
        "ssm_b_re": dbre, "ssm_b_im": dbim, "ssm_c_re": _unpack_c(jnp.swapaxes(dcr, 1, 2)), "ssm_c_im": _unpack_c(jnp.swapaxes(dci, 1, 2)),
        "ssm_d": dd, "attn_branch_norm": dabn, "ssm_branch_norm": dsbn, "norm_ffn": dgf,
    }
    return loss[0, 0], dx, dmpad[META0:PAD], small


def kernel(x, meta_tokens, norm_mix, w_in, q_norm, k_norm, attn_sinks, lam_re, lam_im, log_dt, ssm_b_re, ssm_b_im, ssm_c_re, ssm_c_im, ssm_d, w_glu, attn_branch_norm, ssm_branch_norm, w_out, norm_ffn, w_ffn_in, w_ffn_out, loss_target, m_meta_tokens, m_norm_mix, m_w_in, m_q_norm, m_k_norm, m_attn_sinks, m_lam_re, m_lam_im, m_log_dt, m_ssm_b_re, m_ssm_b_im, m_ssm_c_re, m_ssm_c_im, m_ssm_d, m_w_glu, m_attn_branch_norm, m_ssm_branch_norm, m_w_out, m_norm_ffn, m_w_ffn_in, m_w_ffn_out, v_meta_tokens, v_norm_mix, v_w_in, v_q_norm, v_k_norm, v_attn_sinks, v_lam_re, v_lam_im, v_log_dt, v_ssm_b_re, v_ssm_b_im, v_ssm_c_re, v_ssm_c_im, v_ssm_d, v_w_glu, v_attn_branch_norm, v_ssm_branch_norm, v_w_out, v_norm_ffn, v_w_ffn_in, v_w_ffn_out):
    args = dict(locals())
    names = ["meta_tokens", "norm_mix", "w_in", "q_norm", "k_norm", "attn_sinks", "lam_re", "lam_im", "log_dt",
             "ssm_b_re", "ssm_b_im", "ssm_c_re", "ssm_c_im", "ssm_d", "w_glu", "attn_branch_norm",
             "ssm_branch_norm", "w_out", "norm_ffn", "w_ffn_in", "w_ffn_out"]
    big_names = [s[0] for s in SHARDED]
    xs, ys, cs = _place()
    chip = 2 * xs + ys

    first = _FirstGather(args["w_in"][0].astype(BF), meta_tokens)
    shards = [None] + [_tie(args[n][0], first.token).astype(BF) for n in big_names[1:]]
    place = jnp.stack([chip, cs]).astype(jnp.int32)
    ex = _Exchange(shards[1:], place)

    swapped = ("ssm_b_re", "ssm_b_im")

    def natural(n, a):
        return jnp.swapaxes(a, -1, -2) if n in swapped else a

    p = {n: natural(n, args[n][0]) for n in SMALL}
    loss, dx, dmeta, gsmall = local_step(x[0], loss_target[0], p, first, ex)

    out = {}

    def finish(idxs, name):
        mine = [sum8("sum8_" + SHARDED[i][0], place, ex.parts[i], ex.full[i], SHARDED[i][3]) if i in ex.full
                else sum4("sum4_" + SHARDED[i][0], place, ex.parts[i], ex.halves[i], SHARDED[i][3]) for i in idxs]
        for i, g in zip(idxs, join_halves(name, mine, idxs)):
            n = SHARDED[i][0]
            out[n] = tuple(t[None] for t in adamw("adamw_" + n, args[n][0], g, args["m_" + n][0], args["v_" + n][0]))

    finish((0, 1, 2, 3, 4), "join_halves")
    small_list = [gsmall[n].reshape(natural(n, args[n]).shape) for n in SMALL] + [dmeta, loss.reshape(1, 1)]
    red = allreduce_small(_pack_small(small_list))
    *small_red, dmeta_sum, loss = _unpack_small(red, small_list)
    loss = loss[0, 0]
    gmeta = lax.dynamic_slice_in_dim(dmeta_sum, chip * (D // 4), D // 4, axis=1)
    out["meta_tokens"] = tuple(adamw("adamw_meta", meta_tokens, gmeta, m_meta_tokens, v_meta_tokens))
    ds, nms, nvs = adamw_small([natural(n, args[n]) for n in SMALL], small_red,
                               [natural(n, args["m_" + n]) for n in SMALL],
                               [natural(n, args["v_" + n]) for n in SMALL])
    for n, g_, d_, m_, v_ in zip(SMALL, small_red, ds, nms, nvs):
        out[n] = tuple(natural(n, t) for t in (g_, d_, m_, v_))

    res = [loss, dx[None]]
    for k in range(4):
        res += [out[n][k] for n in names]
    return tuple(res)
```

```python
import functools
import math

import jax
import jax.numpy as jnp
from jax import lax
from jax.experimental import pallas as pl
from jax.experimental.pallas import tpu as pltpu

F32 = jnp.float32
BF = jnp.bfloat16

D = 1024
N_META = 16
HEAD = 64
NQ = 16
NKV = 4
QW = NQ * HEAD
KVW = NKV * HEAD
WIN = 128
GROUPS = 64
GCH = 16
NSTATE = 64
SW = GROUPS * NSTATE
DFF = 2816
IN_COLS = QW + 2 * KVW + D + 2 * D
PAD = 256
META0 = PAD - N_META
EPS = 1e-6
NEG = -1e30

TM = 256
TS = 256
LC = 512
NLC = SW // LC

LR, B1, B2, AEPS, WD, STEP = 0.001, 0.9, 0.999, 1e-08, 0.01, 10
BC1 = 1.0 - B1 ** STEP
BC2 = 1.0 - B2 ** STEP

MESH = pl.DeviceIdType.MESH
ANY = pl.BlockSpec(memory_space=pl.ANY)


def _cp(sem=None, vmem_mb=48):
    return pltpu.CompilerParams(dimension_semantics=sem, vmem_limit_bytes=vmem_mb << 20)


def _full(shape):
    n = len(shape)
    return pl.BlockSpec(shape, lambda *a: (0,) * n)


def _const(shape):
    n = len(shape)
    return pl.BlockSpec(shape, lambda *a: (0,) * n, pipeline_mode=pl.Buffered(1))


def _rows(tm, width):
    return pl.BlockSpec((tm, width), lambda i: (i, 0))


def _dot(a, b):
    return jnp.dot(a, b, preferred_element_type=F32)


def _dot_nt(a, b):
    return lax.dot_general(a, b, (((1,), (1,)), ((), ())), preferred_element_type=F32)


def _dot_tn(a, b):
    return lax.dot_general(a, b, (((0,), (0,)), ((), ())), preferred_element_type=F32)


def _sigmoid(x):
    return 0.5 * jnp.tanh(0.5 * x) + 0.5


def _seg_mean(x, bd, split=True):
    outs = []
    for j in range(x.shape[1] // 256):
        c = x[:, 256 * j:256 * (j + 1)]
        hi = c.astype(BF)
        part = _dot(hi, bd)
        if split:
            part += _dot((c - hi.astype(F32)).astype(BF), bd)
        outs.append(part)
    out = outs[0] if len(outs) == 1 else jnp.concatenate(outs, axis=1)
    return out * (1.0 / HEAD)


_GC = math.sqrt(2.0 / math.pi)


def _gelu(x):
    t = jnp.tanh(_GC * (x + 0.044715 * x * x * x))
    return 0.5 * x * (1.0 + t), t


def _gelu_grad(x, t):
    return 0.5 * (1.0 + t) + 0.5 * x * (1.0 - t * t) * _GC * (1.0 + 3.0 * 0.044715 * x * x)


def _seq_specs():
    assert TM == PAD
    return [pl.BlockSpec((TM, D), lambda i: (jnp.maximum(i - 1, 0), 0)), _full((PAD, D))]


def _seq_tile(x_ref, m_ref):
    return jnp.where(pl.program_id(0) == 0, m_ref[...], x_ref[...])


def in_proj_fwd(x, mpad, w_in, nm, qn_t, kn_t, bd):
    lp = x.shape[0] + PAD

    def body(x_ref, m_ref, w_ref, nm_ref, qn_ref, kn_ref, bd_ref,
             xn_o, qr_o, kr_o, qn_o, kn_o, v_o, u_o, gg_o):
        h = _seq_tile(x_ref, m_ref)
        r = lax.rsqrt(jnp.mean(h * h, axis=-1, keepdims=True) + EPS)
        xb = ((h * r) * nm_ref[...]).astype(BF)
        xn_o[...] = xb
        bdv = bd_ref[...]
        q = _dot(xb, w_ref[:, 0:QW])
        qr_o[...] = q
        qn_o[...] = (q * lax.rsqrt(_seg_mean(q * q, bdv, False) + EPS) * qn_ref[...]).astype(BF)
        k = _dot(xb, w_ref[:, QW:QW + KVW])
        kr_o[...] = k
        kn_o[...] = (k * lax.rsqrt(_seg_mean(k * k, bdv, False) + EPS) * kn_ref[...]).astype(BF)
        v_o[...] = _dot(xb, w_ref[:, QW + KVW:QW + 2 * KVW]).astype(BF)
        u_o[...] = _dot(xb, w_ref[:, QW + 2 * KVW:QW + 2 * KVW + D])
        gg_o[...] = _dot(xb, w_ref[:, QW + 2 * KVW + D:IN_COLS])

    outs = [(D, BF), (QW, F32), (KVW, F32), (QW, BF), (KVW, BF), (KVW, BF), (D, F32), (2 * D, F32)]
    return pl.pallas_call(
        body, name="in_proj_fwd", grid=(lp // TM,),
        in_specs=_seq_specs() + [_full((D, IN_COLS)), _full((1, D)), _full((1, QW)), _full((1, KVW)),
                                 _full((256, 256))],
        out_specs=[_rows(TM, w) for w, _ in outs],
        out_shape=[jax.ShapeDtypeStruct((lp, w), t) for w, t in outs],
        compiler_params=_cp(("arbitrary",)),
    )(x, mpad, w_in, nm, qn_t, kn_t, bd)


def in_proj_bwd(dqn, dkn, dv, du, dgg, qr, kr, x, mpad, dh1, w_in, nm, qn_t, kn_t, bd):
    lp = x.shape[0] + PAD

    def body(dqn_ref, dkn_ref, dv_ref, du_ref, dgg_ref, qr_ref, kr_ref, x_ref, m_ref, dh1_ref,
             w_ref, nm_ref, qn_ref, kn_ref, bd_ref,
             dproj_o, dx_o, dm_o, dnm_o, dqw_o, dkw_o):
        i = pl.program_id(0)

        @pl.when(i == 0)
        def _():
            dnm_o[...] = jnp.zeros_like(dnm_o)
            dqw_o[...] = jnp.zeros_like(dqw_o)
            dkw_o[...] = jnp.zeros_like(dkw_o)

        bdv = bd_ref[...]

        def norm_bwd(x, dy, g, acc_ref):
            rr = lax.rsqrt(_seg_mean(x * x, bdv, False) + EPS)
            xh = x * rr
            acc_ref[...] += jnp.sum(dy * xh, axis=0, keepdims=True)
            dxh = dy * g
            return rr * (dxh - xh * _seg_mean(dxh * xh, bdv))

        dq = norm_bwd(qr_ref[...], dqn_ref[...], qn_ref[...], dqw_o)
        dk = norm_bwd(kr_ref[...], dkn_ref[...], kn_ref[...], dkw_o)
        dproj_o[:, 0:QW] = dq.astype(BF)
        dproj_o[:, QW:QW + KVW] = dk.astype(BF)
        dproj_o[:, QW + KVW:QW + 2 * KVW] = dv_ref[...].astype(BF)
        dproj_o[:, QW + 2 * KVW:QW + 2 * KVW + D] = du_ref[...]
        dproj_o[:, QW + 2 * KVW + D:IN_COLS] = dgg_ref[...]
        dxn = _dot_nt(dproj_o[...], w_ref[...])
        h = _seq_tile(x_ref, m_ref)
        r = lax.rsqrt(jnp.mean(h * h, axis=-1, keepdims=True) + EPS)
        xh = h * r
        dnm_o[...] += jnp.sum(dxn * xh, axis=0, keepdims=True)
        dxh = dxn * nm_ref[...]
        dh0 = dh1_ref[...] + r * (dxh - xh * jnp.mean(dxh * xh, axis=-1, keepdims=True))
        dx_o[...] = dh0

        @pl.when(i == 0)
        def _():
            dm_o[...] = dh0

    return pl.pallas_call(
        body, name="in_proj_bwd", grid=(lp // TM,),
        in_specs=[_rows(TM, QW), _rows(TM, KVW), _rows(TM, KVW), _rows(TM, D), _rows(TM, 2 * D),
                  _rows(TM, QW), _rows(TM, KVW)] + _seq_specs() + [_rows(TM, D),
                  _full((D, IN_COLS)), _full((1, D)), _full((1, QW)), _full((1, KVW)), _full((256, 256))],
        out_specs=[_rows(TM, IN_COLS), _seq_specs()[0], _full((PAD, D)), _full((1, D)), _full((1, QW)),
                   _full((1, KVW))],
        out_shape=[jax.ShapeDtypeStruct((lp, IN_COLS), BF), jax.ShapeDtypeStruct((lp - PAD, D), F32),
                   jax.ShapeDtypeStruct((PAD, D), F32),
                   jax.ShapeDtypeStruct((1, D), F32), jax.ShapeDtypeStruct((1, QW), F32),
                   jax.ShapeDtypeStruct((1, KVW), F32)],
        compiler_params=_cp(("arbitrary",)),
    )(dqn, dkn, dv, du, dgg, qr, kr, x, mpad, dh1, w_in, nm, qn_t, kn_t, bd)


def _att_bias(b):
    qi = lax.broadcasted_iota(jnp.int32, (WIN, WIN), 0)
    kj = lax.broadcasted_iota(jnp.int32, (WIN, WIN), 1)
    upper = kj > qi
    valid = (upper & ((b - 1) * WIN + kj >= PAD)) | (jnp.logical_not(upper) & (b * WIN + kj >= META0))
    bias_pc = jnp.where(valid, 0.0, NEG)
    kj1 = lax.broadcasted_iota(jnp.int32, (1, WIN), 1)
    bias_m = jnp.where((kj1 >= WIN - N_META) & (b * WIN >= PAD), 0.0, NEG)
    return bias_m, bias_pc, upper


def _dup_half(ref, kp, pos, lo):
    pair = ref[:, 128 * kp:128 * kp + 128].astype(F32)
    rolled = pltpu.roll(pair, 64, axis=1)
    keep = lo if pos == 0 else jnp.logical_not(lo)
    return jnp.where(keep, pair, rolled).astype(BF)


def _stack_heads(ref, kv, lo):
    parts = []
    for pp in range(2):
        pair = ref[:, 256 * kv + 128 * pp:256 * kv + 128 * pp + 128]
        zero = jnp.zeros_like(pair)
        parts.append(jnp.where(lo, pair, zero))
        parts.append(jnp.where(lo, zero, pair))
    return jnp.concatenate(parts, axis=0)


def _unstack_heads(x4, lo):
    a = jnp.where(lo, x4[0:WIN], x4[WIN:2 * WIN])
    b = jnp.where(lo, x4[2 * WIN:3 * WIN], x4[3 * WIN:4 * WIN])
    return a, b


def _split_pc(x, upper):
    zero = jnp.zeros_like(x)
    return jnp.where(upper, x, zero).astype(BF), jnp.where(upper, zero, x).astype(BF)


def _head_softmax(s_m, s_p, s_c, bias, sink):
    bias_m, bias_pc, upper = bias
    sm = s_m + bias_m
    spc = jnp.where(upper, s_p, s_c) + bias_pc
    m = jnp.maximum(jnp.max(jnp.maximum(sm, spc), axis=-1, keepdims=True), sink)
    em, epc = jnp.exp(sm - m), jnp.exp(spc - m)
    esink = jnp.exp(sink - m)
    inv = 1.0 / (esink + jnp.sum(em + epc, axis=-1, keepdims=True))
    return em, epc, inv, esink


def _kv_specs():
    return [pl.BlockSpec((WIN, KVW), lambda b: (1, 0)),
            pl.BlockSpec((WIN, KVW), lambda b: (jnp.maximum(b - 1, 0), 0)),
            pl.BlockSpec((WIN, KVW), lambda b: (b, 0))]


def attention_fwd(sinks, qn, kn, v):
    lp = qn.shape[0]

    def body(sink_ref, q_ref, km_ref, kp_ref, kc_ref, vm_ref, vp_ref, vc_ref, o_ref):
        b = pl.program_id(0)
        bias = _att_bias(b)
        lo = lax.broadcasted_iota(jnp.int32, (WIN, WIN), 1) < HEAD
        for kv in range(NKV):
            kp, pos = kv // 2, kv % 2
            kds = [_dup_half(r, kp, pos, lo) for r in (km_ref, kp_ref, kc_ref)]
            vds = [_dup_half(r, kp, pos, lo) for r in (vm_ref, vp_ref, vc_ref)]
            q4 = _stack_heads(q_ref, kv, lo)
            ss = [_dot_nt(q4, kd) for kd in kds]
            es, invs = ([], [], []), []
            for h in range(4):
                rs = slice(WIN * h, WIN * h + WIN)
                em, epc, inv, _ = _head_softmax(ss[0][rs], ss[1][rs], ss[2][rs], bias, sink_ref[4 * kv + h])
                e_p, e_c = _split_pc(epc, bias[2])
                for lst, e in zip(es, (em.astype(BF), e_p, e_c)):
                    lst.append(e)
                invs.append(inv)
            e_m, e_p, e_c = (jnp.concatenate(lst, axis=0) for lst in es)
            o4 = (_dot(e_m, vds[0]) + _dot(e_p, vds[1]) + _dot(e_c, vds[2])) * jnp.concatenate(invs, axis=0)
            oa, ob = _unstack_heads(o4, lo)
            o_ref[:, 256 * kv:256 * kv + 128] = oa
            o_ref[:, 256 * kv + 128:256 * kv + 256] = ob

    return pl.pallas_call(
        body, name="attention_fwd", grid=(lp // WIN,),
        in_specs=[pl.BlockSpec(memory_space=pltpu.SMEM), _rows(WIN, QW)] + _kv_specs() + _kv_specs(),
        out_specs=_rows(WIN, QW),
        out_shape=jax.ShapeDtypeStruct((lp, QW), F32),
        compiler_params=_cp(("arbitrary",)),
    )(sinks, qn, kn, kn, kn, v, v, v)


def attention_bwd(sinks, qn, kn, v, do):
    lp = qn.shape[0]
    nb = lp // WIN

    def body(sink_ref, q_ref, km_ref, kp_ref, kc_ref, vm_ref, vp_ref, vc_ref, do_ref,
             dq_ref, dk_hbm, dv_hbm, dsink_ref, dk_acc, dv_acc):
        b = pl.program_id(0)

        @pl.when(b == 0)
        def _():
            dk_acc[...] = jnp.zeros_like(dk_acc)
            dv_acc[...] = jnp.zeros_like(dv_acc)
            dsink_ref[...] = jnp.zeros_like(dsink_ref)

        bias = _att_bias(b)
        upper = bias[2]
        lo = lax.broadcasted_iota(jnp.int32, (WIN, WIN), 1) < HEAD
        lane8 = lax.broadcasted_iota(jnp.int32, (8, 128), 1)
        starts = [WIN, pl.multiple_of(jnp.maximum(b - 1, 0) * WIN, WIN), pl.multiple_of(b * WIN, WIN)]
        for kv in range(NKV):
            kp, pos = kv // 2, kv % 2
            keep = lo if pos == 0 else jnp.logical_not(lo)
            kds = [_dup_half(r, kp, pos, lo) for r in (km_ref, kp_ref, kc_ref)]
            vds = [_dup_half(r, kp, pos, lo) for r in (vm_ref, vp_ref, vc_ref)]
            q4 = _stack_heads(q_ref, kv, lo)
            do4 = _stack_heads(do_ref, kv, lo)
            ss = [_dot_nt(q4, kd) for kd in kds]
            dps = [_dot_nt(do4, vd) for vd in vds]
            ds_l, p_l = ([], [], []), ([], [], [])
            for h in range(4):
                rs = slice(WIN * h, WIN * h + WIN)
                em, epc, inv, esink = _head_softmax(ss[0][rs], ss[1][rs], ss[2][rs], bias, sink_ref[4 * kv + h])
                p_m, p_pc = em * inv, epc * inv
                dp_m = dps[0][rs]
                dp_pc = jnp.where(upper, dps[1][rs], dps[2][rs])
                delta = jnp.sum(p_m * dp_m + p_pc * dp_pc, axis=-1, keepdims=True)
                val = -jnp.sum(esink * inv * delta, axis=0, keepdims=True)
                dsink_ref[...] += jnp.where(lane8 == 4 * kv + h, val, 0.0)
                for lst, t in zip(ds_l, ((p_m * (dp_m - delta)).astype(BF),) + _split_pc(p_pc * (dp_pc - delta), upper)):
                    lst.append(t)
                for lst, t in zip(p_l, (p_m.astype(BF),) + _split_pc(p_pc, upper)):
                    lst.append(t)
            dss = [jnp.concatenate(lst, axis=0) for lst in ds_l]
            ps = [jnp.concatenate(lst, axis=0) for lst in p_l]
            dq4 = _dot(dss[0], kds[0]) + _dot(dss[1], kds[1]) + _dot(dss[2], kds[2])
            dqa, dqb = _unstack_heads(dq4, lo)
            dq_ref[:, 256 * kv:256 * kv + 128] = dqa
            dq_ref[:, 256 * kv + 128:256 * kv + 256] = dqb
            for x in range(3):
                dkd = _dot_tn(dss[x], q4)
                dvd = _dot_tn(ps[x], do4)
                dkd = jnp.where(keep, dkd + pltpu.roll(dkd, 64, axis=1), 0.0)
                dvd = jnp.where(keep, dvd + pltpu.roll(dvd, 64, axis=1), 0.0)
                dk_acc[pl.ds(starts[x], WIN), 128 * kp:128 * kp + 128] += dkd
                dv_acc[pl.ds(starts[x], WIN), 128 * kp:128 * kp + 128] += dvd

        @pl.when(b == nb - 1)
        def _():
            pltpu.sync_copy(dk_acc, dk_hbm)
            pltpu.sync_copy(dv_acc, dv_hbm)

    return pl.pallas_call(
        body, name="attention_bwd", grid=(nb,),
        in_specs=[pl.BlockSpec(memory_space=pltpu.SMEM), _rows(WIN, QW)] + _kv_specs() + _kv_specs()
                 + [_rows(WIN, QW)],
        out_specs=[_rows(WIN, QW), ANY, ANY, _full((8, 128))],
        out_shape=[jax.ShapeDtypeStruct((lp, QW), F32), jax.ShapeDtypeStruct((lp, KVW), F32),
                   jax.ShapeDtypeStruct((lp, KVW), F32), jax.ShapeDtypeStruct((8, 128), F32)],
        scratch_shapes=[pltpu.VMEM((lp, KVW), F32), pltpu.VMEM((lp, KVW), F32)],
        compiler_params=_cp(("arbitrary",)),
    )(sinks, qn, kn, kn, kn, v, v, v, do)


PW_ROWS = 64
SEG = TS // 8


def _segment_order():
    rho = jnp.arange(TS)
    token = SEG * (rho % 8) + rho // 8
    p = (token[:, None] == jnp.arange(TS)[None, :]).astype(BF)
    return p, p.T


def _reorder(p, x):
    hi = x.astype(BF)
    lo = (x - hi.astype(F32)).astype(BF)
    return _dot(p, hi) + _dot(p, lo)


def _segment_scan(xr_ref, xi_ref, pw_ref, base, ls, c_r, c_i, reverse, visit=None, rescan=True):
    ar, ai = pw_ref[base:base + 8, ls], pw_ref[base + 8:base + 16, ls]
    order = list(range(SEG - 1, -1, -1)) if reverse else list(range(SEG))
    s_r = s_i = jnp.zeros_like(ar)
    for i in order:
        rows = pl.ds(8 * i, 8)
        s_r, s_i = ar * s_r - ai * s_i + xr_ref[rows, ls], ar * s_i + ai * s_r + xi_ref[rows, ls]
        if not rescan:
            xr_ref[rows, ls] = s_r
            xi_ref[rows, ls] = s_i
    rid = lax.broadcasted_iota(jnp.int32, s_r.shape, 0)
    if reverse:
        e_r = jnp.where(rid < 7, pltpu.roll(s_r, 7, axis=0), c_r)
        e_i = jnp.where(rid < 7, pltpu.roll(s_i, 7, axis=0), c_i)
    else:
        e_r = jnp.where(rid >= 1, pltpu.roll(s_r, 1, axis=0), c_r)
        e_i = jnp.where(rid >= 1, pltpu.roll(s_i, 1, axis=0), c_i)
    for n, k in enumerate((1, 2, 4)):
        shift = 8 - k if reverse else k
        t_r, t_i = pltpu.roll(e_r, shift, axis=0), pltpu.roll(e_i, shift, axis=0)
        kr = pw_ref[base + 16 + 16 * n:base + 24 + 16 * n, ls]
        ki = pw_ref[base + 24 + 16 * n:base + 32 + 16 * n, ls]
        e_r, e_i = e_r + kr * t_r - ki * t_i, e_i + kr * t_i + ki * t_r
    last = 0 if reverse else 7
    a16r, a16i = pw_ref[base + 16 + last:base + 17 + last, ls], pw_ref[base + 24 + last:base + 25 + last, ls]
    lr, li, fr, fi = (t[last:last + 1] for t in (e_r, e_i, s_r, s_i))
    out = (a16r * lr - a16i * li + fr, a16r * li + a16i * lr + fi)
    s_r, s_i = e_r, e_i
    extra = None
    for i in order:
        rows = pl.ds(8 * i, 8)
        if rescan:
            s_r, s_i = ar * s_r - ai * s_i + xr_ref[rows, ls], ar * s_i + ai * s_r + xi_ref[rows, ls]
            f_r, f_i = s_r, s_i
        else:
            s_r, s_i = ar * s_r - ai * s_i, ar * s_i + ai * s_r
            f_r, f_i = xr_ref[rows, ls] + s_r, xi_ref[rows, ls] + s_i
        xr_ref[rows, ls] = f_r
        xi_ref[rows, ls] = f_i
        if visit is not None:
            extra = visit(i, f_r, f_i, extra)
    return out if visit is None else (out, extra)


def ssm_fwd(u, pw, br, bi, cr, ci, dsk, perm):
    lp = u.shape[0]

    def body(u_ref, pw_ref, br_ref, bi_ref, cr_ref, ci_ref, d_ref, p_ref, pt_ref,
             y_ref, sr_ref, si_ref, car_r, car_i, yp_s):
        i = pl.program_id(0)

        @pl.when(i == 0)
        def _():
            car_r[...] = jnp.zeros_like(car_r)
            car_i[...] = jnp.zeros_like(car_i)

        u = u_ref[...]
        ub = _dot(p_ref[...], u.astype(BF)).astype(BF)
        for j in range(NLC):
            uj = ub[:, 128 * j:128 * j + 128]
            sr_ref[:, LC * j:LC * j + LC] = _dot(uj, br_ref[j])
            si_ref[:, LC * j:LC * j + LC] = _dot(uj, bi_ref[j])
        for j in range(NLC):
            ls = slice(LC * j, LC * j + LC)
            car_r[:, ls], car_i[:, ls] = _segment_scan(sr_ref, si_ref, pw_ref, 0, ls, car_r[:, ls], car_i[:, ls],
                                                       False, rescan=False)
        for j in range(NLC):
            ls = slice(LC * j, LC * j + LC)
            yp_s[:, 128 * j:128 * j + 128] = (_dot(sr_ref[:, ls].astype(BF), cr_ref[j])
                                              - _dot(si_ref[:, ls].astype(BF), ci_ref[j]))
        y_ref[...] = _reorder(pt_ref[...], yp_s[...]) + d_ref[...] * u

    p, pt = perm
    return pl.pallas_call(
        body, name="ssm_fwd", grid=(lp // TS,),
        in_specs=[_rows(TS, D), _full((2 * PW_ROWS, SW)), _full((NLC, 128, LC)), _full((NLC, 128, LC)),
                  _full((NLC, LC, 128)), _full((NLC, LC, 128)), _full((1, D)), _full((TS, TS)), _full((TS, TS))],
        out_specs=[_rows(TS, D), _rows(TS, SW), _rows(TS, SW)],
        out_shape=[jax.ShapeDtypeStruct((lp, D), F32), jax.ShapeDtypeStruct((lp, SW), F32),
                   jax.ShapeDtypeStruct((lp, SW), F32)],
        scratch_shapes=[pltpu.VMEM((1, SW), F32), pltpu.VMEM((1, SW), F32), pltpu.VMEM((TS, D), F32)],
        compiler_params=_cp(("arbitrary",)),
    )(u, pw, br, bi, cr, ci, dsk, p, pt)


def ssm_bwd(dy, u, sr, si, pw, br, bi, cr, ci, dsk, perm):
    lp = u.shape[0]
    nt = lp // TS

    def rev(i):
        return (nt - 1 - i, 0)

    def prev8(i):
        return (jnp.maximum((nt - 1 - i) * (TS // 8) - 1, 0), 0)

    def body(dy_ref, u_ref, sr_ref, si_ref, pr8_ref, pi8_ref, pw_ref, br_ref, bi_ref, cr_ref, ci_ref, d_ref,
             p_ref, pt_ref, du_ref, da_ref, dbr_ref, dbi_ref, dcr_ref, dci_ref, dd_ref,
             gr_s, gi_s, car_r, car_i, dup_s):
        i = pl.program_id(0)

        @pl.when(i == 0)
        def _():
            car_r[...] = jnp.zeros_like(car_r)
            car_i[...] = jnp.zeros_like(car_i)
            for ref in (da_ref, dbr_ref, dbi_ref, dcr_ref, dci_ref, dd_ref):
                ref[...] = jnp.zeros_like(ref)

        keep = 1.0 - (i == nt - 1).astype(F32)
        dy = dy_ref[...]
        u = u_ref[...]
        dd_ref[...] += jnp.sum(dy * u, axis=0, keepdims=True)
        pm = p_ref[...]
        dyb = _dot(pm, dy.astype(BF)).astype(BF)
        ub = _dot(pm, u.astype(BF)).astype(BF)
        for j in range(NLC):
            ls = slice(LC * j, LC * j + LC)
            dyj = dyb[:, 128 * j:128 * j + 128]
            gr_s[:, ls] = _dot_nt(dyj, cr_ref[j])
            gi_s[:, ls] = -_dot_nt(dyj, ci_ref[j])
            dcr_ref[j] += _dot_tn(dyj, sr_ref[:, ls].astype(BF))
            dci_ref[j] -= _dot_tn(dyj, si_ref[:, ls].astype(BF))
        rid = lax.broadcasted_iota(jnp.int32, (8, LC), 0)
        for j in range(NLC):
            ls = slice(LC * j, LC * j + LC)

            def accumulate(i, g_r, g_i, acc):
                if i >= 1:
                    rows = pl.ds(8 * (i - 1), 8)
                    p_r, p_i = sr_ref[rows, ls], si_ref[rows, ls]
                else:
                    rows = pl.ds(8 * (SEG - 1), 8)
                    p_r = jnp.where(rid >= 1, pltpu.roll(sr_ref[rows, ls], 1, axis=0), pr8_ref[7:8, ls] * keep)
                    p_i = jnp.where(rid >= 1, pltpu.roll(si_ref[rows, ls], 1, axis=0), pi8_ref[7:8, ls] * keep)
                t_r, t_i = g_r * p_r + g_i * p_i, g_i * p_r - g_r * p_i
                return (t_r, t_i) if acc is None else (acc[0] + t_r, acc[1] + t_i)

            (car_r[:, ls], car_i[:, ls]), (t_r, t_i) = _segment_scan(
                gr_s, gi_s, pw_ref, PW_ROWS, ls, car_r[:, ls], car_i[:, ls], True, accumulate)
            da_ref[0:8, ls] += t_r
            da_ref[8:16, ls] += t_i
        for j in range(NLC):
            ls = slice(LC * j, LC * j + LC)
            uj = ub[:, 128 * j:128 * j + 128]
            grb = gr_s[:, ls].astype(BF)
            gib = gi_s[:, ls].astype(BF)
            dup_s[:, 128 * j:128 * j + 128] = _dot_nt(grb, br_ref[j]) + _dot_nt(gib, bi_ref[j])
            dbr_ref[j] += _dot_tn(uj, grb)
            dbi_ref[j] += _dot_tn(uj, gib)
        du_ref[...] = (_reorder(pt_ref[...], dup_s[...]) + d_ref[...] * dy).astype(BF)

    p, pt = perm
    return pl.pallas_call(
        body, name="ssm_bwd", grid=(nt,),
        in_specs=[pl.BlockSpec((TS, D), rev), pl.BlockSpec((TS, D), rev), pl.BlockSpec((TS, SW), rev),
                  pl.BlockSpec((TS, SW), rev), pl.BlockSpec((8, SW), prev8), pl.BlockSpec((8, SW), prev8),
                  _full((2 * PW_ROWS, SW)), _full((NLC, 128, LC)), _full((NLC, 128, LC)),
                  _full((NLC, LC, 128)), _full((NLC, LC, 128)), _full((1, D)), _full((TS, TS)), _full((TS, TS))],
        out_specs=[pl.BlockSpec((TS, D), rev), _full((16, SW)), _full((NLC, 128, LC)), _full((NLC, 128, LC)),
                   _full((NLC, 128, LC)), _full((NLC, 128, LC)), _full((1, D))],
        out_shape=[jax.ShapeDtypeStruct((lp, D), BF), jax.ShapeDtypeStruct((16, SW), F32),
                   jax.ShapeDtypeStruct((NLC, 128, LC), F32), jax.ShapeDtypeStruct((NLC, 128, LC), F32),
                   jax.ShapeDtypeStruct((NLC, 128, LC), F32), jax.ShapeDtypeStruct((NLC, 128, LC), F32),
                   jax.ShapeDtypeStruct((1, D), F32)],
        scratch_shapes=[pltpu.VMEM((TS, SW), F32), pltpu.VMEM((TS, SW), F32),
                        pltpu.VMEM((1, SW), F32), pltpu.VMEM((1, SW), F32), pltpu.VMEM((TS, D), F32)],
        compiler_params=_cp(("arbitrary",)),
    )(dy, u, sr, si, sr, si, pw, br, bi, cr, ci, dsk, p, pt)


def _merge_parts(attn, zz, gg, abn, sbn):
    za, zb = zz[:, 0:D], zz[:, D:2 * D]
    sgz = _sigmoid(zb)
    ssm = za * sgz
    ra = lax.rsqrt(jnp.mean(attn * attn, axis=-1, keepdims=True) + EPS)
    rs = lax.rsqrt(jnp.mean(ssm * ssm, axis=-1, keepdims=True) + EPS)
    ah, sh = attn * ra, ssm * rs
    sga, sgs = _sigmoid(gg[:, 0:D]), _sigmoid(gg[:, D:2 * D])
    return za, sgz, ra, rs, ah, sh, sga, sgs, ah * abn, sh * sbn


def mid_fwd(y, attn, gg, x, mpad, w_glu, w_out, abn, sbn):
    lp = y.shape[0]

    def body(y_ref, a_ref, gg_ref, x_ref, m_ref, wg_ref, wo_ref, abn_ref, sbn_ref, z_o, zz_o, mg_o, h1_o):
        z, _ = _gelu(y_ref[...])
        zb = z.astype(BF)
        z_o[...] = zb
        zz = _dot(zb, wg_ref[...])
        zz_o[...] = zz.astype(BF)
        _, _, _, _, _, _, sga, sgs, an, sn = _merge_parts(a_ref[...], zz, gg_ref[...], abn_ref[...], sbn_ref[...])
        mb = (sga * an + sgs * sn).astype(BF)
        mg_o[...] = mb
        h1_o[...] = _seq_tile(x_ref, m_ref) + _dot(mb, wo_ref[...])

    return pl.pallas_call(
        body, name="mid_fwd", grid=(lp // TM,),
        in_specs=[_rows(TM, D), _rows(TM, D), _rows(TM, 2 * D)] + _seq_specs() + [_full((D, 2 * D)), _full((D, D)),
                  _full((1, D)), _full((1, D))],
        out_specs=[_rows(TM, D), _rows(TM, 2 * D), _rows(TM, D), _rows(TM, D)],
        out_shape=[jax.ShapeDtypeStruct((lp, D), BF), jax.ShapeDtypeStruct((lp, 2 * D), BF),
                   jax.ShapeDtypeStruct((lp, D), BF), jax.ShapeDtypeStruct((lp, D), F32)],
        compiler_params=_cp(("arbitrary",)),
    )(y, attn, gg, x, mpad, w_glu, w_out, abn, sbn)


def mid_bwd(dh1b, y, attn, gg, zzb, w_glu, w_out, abn, sbn):
    lp = y.shape[0]

    def body(dh_ref, y_ref, a_ref, gg_ref, zz_ref, wg_ref, wo_ref, abn_ref, sbn_ref,
             dy_o, dattn_o, dgg_o, dzz_o, dabn_o, dsbn_o):
        i = pl.program_id(0)

        @pl.when(i == 0)
        def _():
            dabn_o[...] = jnp.zeros_like(dabn_o)
            dsbn_o[...] = jnp.zeros_like(dsbn_o)

        dm = _dot_nt(dh_ref[...], wo_ref[...])
        abn, sbn = abn_ref[...], sbn_ref[...]
        za, sgz, ra, rs, ah, sh, sga, sgs, an, sn = _merge_parts(
            a_ref[...], zz_ref[...].astype(F32), gg_ref[...], abn, sbn)
        da = dm * sga
        dabn_o[...] += jnp.sum(da * ah, axis=0, keepdims=True)
        dah = da * abn
        pa = dah * ah
        dgg_o[:, 0:D] = (pa * (1.0 - sga)).astype(BF)
        dattn_o[...] = (ra * (dah - ah * jnp.mean(pa, axis=-1, keepdims=True))).astype(BF)
        ds = dm * sgs
        dsbn_o[...] += jnp.sum(ds * sh, axis=0, keepdims=True)
        dsh = ds * sbn
        ps = dsh * sh
        dgg_o[:, D:2 * D] = (ps * (1.0 - sgs)).astype(BF)
        dssm = rs * (dsh - sh * jnp.mean(ps, axis=-1, keepdims=True))
        dza = dssm * sgz
        dzz_o[:, 0:D] = dza.astype(BF)
        dzz_o[:, D:2 * D] = (dza * za * (1.0 - sgz)).astype(BF)
        dz = _dot_nt(dzz_o[...], wg_ref[...])
        yv = y_ref[...]
        _, t = _gelu(yv)
        dy_o[...] = dz * _gelu_grad(yv, t)

    return pl.pallas_call(
        body, name="mid_bwd", grid=(lp // TM,),
        in_specs=[_rows(TM, D), _rows(TM, D), _rows(TM, D), _rows(TM, 2 * D), _rows(TM, 2 * D),
                  _full((D, 2 * D)), _full((D, D)), _full((1, D)), _full((1, D))],
        out_specs=[_rows(TM, D), _rows(TM, D), _rows(TM, 2 * D), _rows(TM, 2 * D), _full((1, D)), _full((1, D))],
        out_shape=[jax.ShapeDtypeStruct((lp, D), F32), jax.ShapeDtypeStruct((lp, D), BF),
                   jax.ShapeDtypeStruct((lp, 2 * D), BF), jax.ShapeDtypeStruct((lp, 2 * D), BF),
                   jax.ShapeDtypeStruct((1, D), F32), jax.ShapeDtypeStruct((1, D), F32)],
        compiler_params=_cp(("arbitrary",)),
    )(dh1b, y, attn, gg, zzb, w_glu, w_out, abn, sbn)


def ffn_fwd_bwd(h1, tgt, w_fi, w_fo, gf):
    lp = h1.shape[0]
    tf = 256
    bounds = (0, 1536, DFF)

    def body(h_ref, t_ref, wi_ref, wo_ref, gf_ref, hn_o, dgu_o, act_o, dh2_o, dh1_o, dh1b_o, dgf_o, loss_o,
             gate_s, up_s):
        i = pl.program_id(0)

        @pl.when(i == 0)
        def _():
            dgf_o[...] = jnp.zeros_like(dgf_o)
            loss_o[...] = jnp.zeros_like(loss_o)

        h = h_ref[...]
        r = lax.rsqrt(jnp.mean(h * h, axis=-1, keepdims=True) + EPS)
        xh = h * r
        hb = (xh * gf_ref[...]).astype(BF)
        hn_o[...] = hb
        h2 = h
        for c0, c1 in zip(bounds[:-1], bounds[1:]):
            gate = _dot(hb, wi_ref[:, c0:c1])
            up = _dot(hb, wi_ref[:, DFF + c0:DFF + c1])
            gate_s[:, c0:c1] = gate
            up_s[:, c0:c1] = up
            ab = (gate * _sigmoid(gate) * up).astype(BF)
            act_o[:, c0:c1] = ab
            h2 = h2 + _dot(ab, wo_ref[c0:c1, :])
        real = (i >= PAD // tf).astype(F32)
        e = (h2 - t_ref[...]) * real
        loss_o[...] += 0.5 * jnp.sum(jnp.sum(e * e, axis=-1, keepdims=True), axis=0, keepdims=True) * (1.0 / D)
        dh2 = e * (1.0 / D)
        db = dh2.astype(BF)
        dh2_o[...] = db
        dhn = jnp.zeros((tf, D), F32)
        for c0, c1 in zip(bounds[:-1], bounds[1:]):
            gate = gate_s[:, c0:c1]
            up = up_s[:, c0:c1]
            sg = _sigmoid(gate)
            dact = _dot_nt(db, wo_ref[c0:c1, :])
            dgate = (dact * up * (sg * (1.0 + gate * (1.0 - sg)))).astype(BF)
            dup = (dact * (gate * sg)).astype(BF)
            dgu_o[:, c0:c1] = dgate
            dgu_o[:, DFF + c0:DFF + c1] = dup
            dhn += _dot_nt(dgate, wi_ref[:, c0:c1]) + _dot_nt(dup, wi_ref[:, DFF + c0:DFF + c1])
        dgf_o[...] += jnp.sum(dhn * xh, axis=0, keepdims=True)
        dxh = dhn * gf_ref[...]
        dh1 = dh2 + r * (dxh - xh * jnp.mean(dxh * xh, axis=-1, keepdims=True))
        dh1_o[...] = dh1
        dh1b_o[...] = dh1.astype(BF)

    return pl.pallas_call(
        body, name="ffn_fwd_bwd", grid=(lp // tf,),
        in_specs=[_rows(tf, D), pl.BlockSpec((tf, D), lambda i: (jnp.maximum(i - PAD // tf, 0), 0)),
                  _const((D, 2 * DFF)), _const((DFF, D)), _full((1, D))],
        out_specs=[_rows(tf, D), _rows(tf, 2 * DFF), _rows(tf, DFF), _rows(tf, D), _rows(tf, D), _rows(tf, D),
                   _full((1, D)), _full((1, 1))],
        out_shape=[jax.ShapeDtypeStruct((lp, D), BF), jax.ShapeDtypeStruct((lp, 2 * DFF), BF),
                   jax.ShapeDtypeStruct((lp, DFF), BF), jax.ShapeDtypeStruct((lp, D), BF),
                   jax.ShapeDtypeStruct((lp, D), F32), jax.ShapeDtypeStruct((lp, D), BF),
                   jax.ShapeDtypeStruct((1, D), F32), jax.ShapeDtypeStruct((1, 1), F32)],
        scratch_shapes=[pltpu.VMEM((tf, DFF), F32), pltpu.VMEM((tf, DFF), F32)],
        compiler_params=_cp(("arbitrary",), vmem_mb=58),
    )(h1, tgt, w_fi, w_fo, gf)


def weight_grad(name, a, b, tn, with_bf16=False):
    lp, k = a.shape
    n = b.shape[1]
    tr = next(t for t in (2112, 1056, 768, 512, 384, 256, 128) if lp % t == 0 and t * k * 2 <= (5 << 20))

    def body(a_ref, b_ref, o_ref, *ob_ref):
        @pl.when(pl.program_id(1) == 0)
        def _():
            o_ref[...] = jnp.zeros_like(o_ref)

        o_ref[...] += _dot_tn(a_ref[...], b_ref[...])
        if with_bf16:
            @pl.when(pl.program_id(1) == lp // tr - 1)
            def _():
                ob_ref[0][...] = o_ref[...].astype(BF)

    spec = pl.BlockSpec((k, tn), lambda j, m: (0, j))
    return pl.pallas_call(
        body, name=name, grid=(n // tn, lp // tr),
        in_specs=[pl.BlockSpec((tr, k), lambda j, m: (m, 0)), pl.BlockSpec((tr, tn), lambda j, m: (m, j))],
        out_specs=[spec, spec] if with_bf16 else spec,
        out_shape=([jax.ShapeDtypeStruct((k, n), F32), jax.ShapeDtypeStruct((k, n), BF)] if with_bf16
                   else jax.ShapeDtypeStruct((k, n), F32)),
        compiler_params=_cp(("arbitrary", "arbitrary")),
    )(a, b)


def _adamw_math(w, g, m, v):
    m = B1 * m + (1.0 - B1) * g
    v = B2 * v + (1.0 - B2) * (g * g)
    delta = -LR * ((m / BC1) / (jnp.sqrt(v / BC2) + AEPS) + WD * w)
    return delta, m, v


def _row_tile(r):
    return next((t for t in (256, 128, 64, 32, 16, 8) if r % t == 0), r)


def adamw(name, w, g, m, v):
    r, c = w.shape
    tr = _row_tile(r)

    def body(w_ref, g_ref, m_ref, v_ref, g_o, d_o, m_o, v_o):
        g = g_ref[...]
        g_o[...] = g
        d_o[...], m_o[...], v_o[...] = _adamw_math(w_ref[...], g, m_ref[...], v_ref[...])

    spec = pl.BlockSpec((tr, c), lambda i: (i, 0))
    return pl.pallas_call(
        body, name=name, grid=(r // tr,), in_specs=[spec] * 4, out_specs=[spec] * 4,
        out_shape=[jax.ShapeDtypeStruct((r, c), F32)] * 4,
        compiler_params=_cp(("arbitrary",)),
    )(w, g, m, v)


def adamw_small(ws, gs, ms, vs):
    n = len(ws)
    steps = 8

    def spec(a):
        if a.ndim == 4:
            return pl.BlockSpec((1, a.shape[1] // steps) + a.shape[2:], lambda i: (0, i, 0, 0))
        nd = a.ndim
        return pl.BlockSpec(a.shape, lambda i: (0,) * nd)

    def body(*refs):
        w, g, m, v, d_o, m_o, v_o = (refs[k * n:(k + 1) * n] for k in range(7))
        for t in range(n):
            d_o[t][...], m_o[t][...], v_o[t][...] = _adamw_math(w[t][...], g[t][...], m[t][...], v[t][...])

    specs = [spec(a) for a in ws]
    res = pl.pallas_call(
        body, name="adamw_small", grid=(steps,), in_specs=specs * 4, out_specs=specs * 3,
        out_shape=[jax.ShapeDtypeStruct(a.shape, F32) for a in ws] * 3,
        compiler_params=_cp(("arbitrary",)),
    )(*ws, *gs, *ms, *vs)
    return res[:n], res[n:2 * n], res[2 * n:]


def pair_sum(name, place, full, got, ax):
    r, c = got.shape
    tr = _row_tile(r)

    def body(s_ref, a_ref, b_ref, o_ref):
        o_ref[...] = (a_ref[...] + b_ref[...]).astype(BF)

    if ax == 1:
        mine = pl.BlockSpec((tr, c), lambda i, s: (s[1] * (r // tr) + i, 0))
    else:
        mine = pl.BlockSpec((tr, c), lambda i, s: (i, s[1]))
    spec = pl.BlockSpec((tr, c), lambda i, s: (i, 0))
    return pl.pallas_call(
        body, name=name,
        grid_spec=pltpu.PrefetchScalarGridSpec(num_scalar_prefetch=1, grid=(r // tr,), in_specs=[mine, spec],
                                               out_specs=spec),
        out_shape=jax.ShapeDtypeStruct((r, c), BF),
        compiler_params=_cp(("arbitrary",)),
    )(place, full, got)


def sum4(name, place, parts, half, ax):
    _, r, c = parts.shape
    tr = _row_tile(r)

    def body(s_ref, p_ref, h_ref, o_ref):
        o_ref[...] = ((p_ref[0].astype(F32) + p_ref[1].astype(F32))
                      + (p_ref[2].astype(F32) + h_ref[...].astype(F32)))

    if ax == 1:
        own = pl.BlockSpec((tr, c), lambda i, s: (i, s[0]))
        out = pl.BlockSpec((tr, c), lambda i, s: (s[1] * (r // tr) + i, 0))
        shape = (2 * r, c)
    else:
        own = pl.BlockSpec((tr, c), lambda i, s: (s[0] * (r // tr) + i, 0))
        out = pl.BlockSpec((tr, c), lambda i, s: (i, s[1]))
        shape = (r, 2 * c)
    return pl.pallas_call(
        body, name=name,
        grid_spec=pltpu.PrefetchScalarGridSpec(
            num_scalar_prefetch=1, grid=(r // tr,),
            in_specs=[pl.BlockSpec((3, tr, c), lambda i, s: (0, i, 0)), own], out_specs=out),
        out_shape=jax.ShapeDtypeStruct(shape, F32),
        compiler_params=_cp(("arbitrary",)),
    )(place, parts, half)


def sum8(name, place, parts, full, ax):
    _, r, c = parts.shape
    tr = _row_tile(r)

    def body(s_ref, p_ref, g_ref, o_ref):
        p = [p_ref[k].astype(F32) for k in range(7)]
        o_ref[...] = ((p[0] + p[1]) + (p[2] + p[3])) + ((p[4] + p[5]) + (p[6] + g_ref[...]))

    if ax == 1:
        own = pl.BlockSpec((tr, c), lambda i, s: (s[1] * (r // tr) + i, s[0]))
        out = pl.BlockSpec((tr, c), lambda i, s: (s[1] * (r // tr) + i, 0))
        shape = (2 * r, c)
    else:
        own = pl.BlockSpec((tr, c), lambda i, s: (s[0] * (r // tr) + i, s[1]))
        out = pl.BlockSpec((tr, c), lambda i, s: (i, s[1]))
        shape = (r, 2 * c)
    return pl.pallas_call(
        body, name=name,
        grid_spec=pltpu.PrefetchScalarGridSpec(
            num_scalar_prefetch=1, grid=(r // tr,),
            in_specs=[pl.BlockSpec((7, tr, c), lambda i, s: (0, i, 0)), own], out_specs=out),
        out_shape=jax.ShapeDtypeStruct(shape, F32),
        compiler_params=_cp(("arbitrary",)),
    )(place, parts, full)


def _place():
    x, y, c = lax.axis_index("x"), lax.axis_index("y"), lax.axis_index("c")
    return x, y, c


def _other_chips(x, y):
    return [(1 - x, y), (x, 1 - y), (1 - x, 1 - y)]


SHARDED = (("w_in", D, IN_COLS, 1), ("w_glu", D, 2 * D, 1), ("w_out", D, D, 0),
           ("w_ffn_in", D, 2 * DFF, 1), ("w_ffn_out", DFF, D, 0))


def _shard_of(ref, axis, k, size):
    if axis == 0:
        return ref.at[pl.ds(k * size, size), :]
    return ref.at[:, pl.ds(k * size, size)]


def _gather_piece(full, dims, k, h):
    r, cc, ax = dims
    if ax == 0:
        return full.at[pl.ds(k * (r // 4) + h * (r // 8), r // 8), :]
    return full.at[pl.ds(h * (r // 2), r // 2), pl.ds(k * (cc // 4), cc // 4)]


def _rows_half(shard, h):
    rs = shard.shape[0]
    return shard.at[pl.ds(h * (rs // 2), rs // 2), :]


SEM = pl.BlockSpec(memory_space=pltpu.SEMAPHORE)
HBM = pl.BlockSpec(memory_space=pltpu.HBM)
EFFECT = pltpu.SideEffectType.DATAFLOW_SIDE_EFFECTING


def _in_hbm(a):
    return pltpu.with_memory_space_constraint(a, pltpu.HBM)


def _split_copy_start(name, copies, ncopies, srcs, lands, after):
    ns, nl = len(srcs), len(lands)

    def body(*refs):
        ins, land_refs = refs[:ns], refs[ns:ns + nl]
        send_sems, recv_sems = refs[ns + nl + 1], refs[ns + nl + 2]
        token = refs[-1]
        for k, (src, dst, dev) in enumerate(copies(ins, land_refs)):
            pltpu.make_async_remote_copy(src_ref=src, dst_ref=dst, send_sem=send_sems.at[k],
                                         recv_sem=recv_sems.at[k], device_id=dev, device_id_type=MESH).start()
        token[...] = jnp.zeros_like(token)

    res = pl.pallas_call(
        body, name=name,
        in_specs=[HBM] * (ns + nl) + [ANY],
        out_specs=(SEM, SEM) + (HBM,) * (ns + nl) + (pl.BlockSpec(memory_space=pltpu.VMEM),),
        out_shape=(pltpu.SemaphoreType.DMA((ncopies,)), pltpu.SemaphoreType.DMA((ncopies,)))
        + tuple(pltpu.HBM(a.shape, a.dtype) for a in list(srcs) + list(lands))
        + (jax.ShapeDtypeStruct((8, 128), F32),),
        input_output_aliases={t: 2 + t for t in range(ns + nl)},
        compiler_params=pltpu.CompilerParams(has_side_effects=EFFECT),
    )(*[_in_hbm(a) for a in srcs], *[_in_hbm(a) for a in lands], after)
    return res[0], res[1], list(res[2:2 + ns]), list(res[2 + ns:2 + ns + nl]), res[-1]


def _split_copy_wait(name, copies, arrivals, send_sems, recv_sems, srcs, lands, after):
    ns, nl = len(srcs), len(lands)

    def body(*refs):
        ins, land_refs = refs[:ns], refs[ns:ns + nl]
        send_sems_ref, recv_sems_ref = refs[ns + nl], refs[ns + nl + 1]
        mine = copies(ins, land_refs)
        for k, ((src, dst, dev), got) in enumerate(zip(mine, arrivals(ins, land_refs))):
            pltpu.make_async_remote_copy(src_ref=src, dst_ref=dst, send_sem=send_sems_ref.at[k],
                                         recv_sem=recv_sems_ref.at[k], device_id=dev,
                                         device_id_type=MESH).wait_send()
            pltpu.make_async_remote_copy(src_ref=got, dst_ref=got, send_sem=send_sems_ref.at[k],
                                         recv_sem=recv_sems_ref.at[k], device_id=dev,
                                         device_id_type=MESH).wait_recv()

    res = pl.pallas_call(
        body, name=name,
        in_specs=[HBM] * (ns + nl) + [SEM, SEM, ANY],
        out_specs=(HBM,) * (ns + nl),
        out_shape=tuple(pltpu.HBM(a.shape, a.dtype) for a in list(srcs) + list(lands)),
        input_output_aliases={t: t for t in range(ns + nl)},
        compiler_params=pltpu.CompilerParams(has_side_effects=EFFECT),
    )(*srcs, *lands, send_sems, recv_sems, after)
    return list(res[:ns]), list(res[ns:])


def _gather_copies(dims):
    def copies(ins, lands):
        x, y, c = _place()
        mine = 2 * x + y
        out = []
        for t, d in enumerate(dims):
            size = (d[0] if d[2] == 0 else d[1]) // 4
            for px, py in _other_chips(x, y):
                for dc in range(2):
                    out.append((_rows_half(ins[t], c), _gather_piece(lands[t], d, mine, c), (px, py, c ^ dc)))
            out.append((ins[t], _shard_of(lands[t], d[2], mine, size), (x, y, 1 - c)))
        return out

    def arrivals(ins, lands):
        x, y, c = _place()
        mine = 2 * x + y
        out = []
        for t, d in enumerate(dims):
            size = (d[0] if d[2] == 0 else d[1]) // 4
            for px, py in _other_chips(x, y):
                for dc in range(2):
                    out.append(_gather_piece(lands[t], d, 2 * px + py, c ^ dc))
            out.append(_shard_of(lands[t], d[2], mine, size))
        return out

    return copies, arrivals


def _half_of(ref, ax, h):
    r, c = ref.shape
    if ax == 1:
        return ref.at[pl.ds(h * (r // 2), r // 2), :]
    return ref.at[:, pl.ds(h * (c // 2), c // 2)]


def _half_shape(r, c, ax):
    return (r // 2, c) if ax == 1 else (r, c // 2)


def pair_exchange(name, grads, idxs):
    n = len(grads)
    which = [SHARDED[i] for i in idxs]

    def body(*refs):
        ins, got = refs[:n], refs[n:2 * n]
        send_sems, recv_sems = refs[2 * n:]
        x, y, c = _place()
        cps = []
        for t, (_, _, _, ax) in enumerate(which):
            cp = pltpu.make_async_remote_copy(
                src_ref=_half_of(ins[t], ax, 1 - c), dst_ref=got[t], send_sem=send_sems.at[t],
                recv_sem=recv_sems.at[t], device_id=(x, y, 1 - c), device_id_type=MESH)
            cp.start()
            cps.append(cp)
        for cp in cps:
            cp.wait()

    return pl.pallas_call(
        body, name=name,
        in_specs=[ANY] * n, out_specs=[ANY] * n,
        out_shape=[jax.ShapeDtypeStruct(_half_shape(r, c, ax), F32) for _, r, c, ax in which],
        scratch_shapes=[pltpu.SemaphoreType.DMA((n,)), pltpu.SemaphoreType.DMA((n,))],
        compiler_params=pltpu.CompilerParams(has_side_effects=True),
    )(*grads)


def _piece_shapes(idxs):
    out = []
    for _, r, c, ax in [SHARDED[i] for i in idxs]:
        hr, hc = _half_shape(r, c, ax)
        out.append((hr // 4, hc) if ax == 0 else (hr, hc // 4))
    return out


def _scatter3_copies(idxs):
    which = [SHARDED[i] for i in idxs]

    def copies(ins, lands):
        x, y, c = _place()
        out = []
        for t, (_, _, _, ax) in enumerate(which):
            size = ins[t].shape[ax] // 4
            for j, (px, py) in enumerate(_other_chips(x, y)):
                out.append((_shard_of(ins[t], ax, 2 * px + py, size), lands[t].at[j], (px, py, c)))
        return out

    def arrivals(ins, lands):
        return [lands[t].at[j] for t in range(len(which)) for j in range(3)]

    return copies, arrivals


def _piece_of(full, ax, k, h):
    r, c = full.shape
    if ax == 1:
        return full.at[pl.ds(h * (r // 2), r // 2), pl.ds(k * (c // 4), c // 4)]
    return full.at[pl.ds(k * (r // 4), r // 4), pl.ds(h * (c // 2), c // 2)]


def _scatter8_copies(idxs):
    which = [SHARDED[i] for i in idxs]

    def copies(ins, lands):
        x, y, c = _place()
        out = []
        for t, (_, _, _, ax) in enumerate(which):
            for j, (px, py) in enumerate(_other_chips(x, y)):
                for d in range(2):
                    out.append((_piece_of(ins[t], ax, 2 * px + py, c ^ d), lands[t].at[2 * j + d], (px, py, c ^ d)))
            out.append((_piece_of(ins[t], ax, 2 * x + y, 1 - c), lands[t].at[6], (x, y, 1 - c)))
        return out

    def arrivals(ins, lands):
        return [lands[t].at[k] for t in range(len(which)) for k in range(7)]

    return copies, arrivals


def scatter_grads(halves, idxs):
    n = len(halves)
    which = [SHARDED[i] for i in idxs]

    def shapes():
        return _piece_shapes(idxs)

    def body(*refs):
        ins, outs = refs[:n], refs[n:2 * n]
        send_sems, recv_sems = refs[2 * n:]
        x, y, c = _place()
        started = []
        for t, (src, dst, (_, _, _, ax)) in enumerate(zip(ins, outs, which)):
            size = src.shape[ax] // 4
            for j, (px, py) in enumerate(_other_chips(x, y)):
                cp = pltpu.make_async_remote_copy(
                    src_ref=_shard_of(src, ax, 2 * px + py, size), dst_ref=dst.at[j],
                    send_sem=send_sems.at[3 * t + j], recv_sem=recv_sems.at[3 * t + j],
                    device_id=(px, py, c), device_id_type=MESH)
                cp.start()
                started.append(cp)
        for cp in started:
            cp.wait()

    return pl.pallas_call(
        body, name="scatter_grads",
        in_specs=[ANY] * n, out_specs=[ANY] * n,
        out_shape=[jax.ShapeDtypeStruct((3,) + s, BF) for s in shapes()],
        scratch_shapes=[pltpu.SemaphoreType.DMA((3 * n,)), pltpu.SemaphoreType.DMA((3 * n,))],
        compiler_params=pltpu.CompilerParams(has_side_effects=True),
    )(*halves)


def join_halves(name, shards, idxs):
    n = len(shards)
    which = [SHARDED[i] for i in idxs]

    def body(*refs):
        ins, outs = refs[:n], refs[n:2 * n]
        send_sems, recv_sems = refs[2 * n:]
        x, y, c = _place()
        cps = []
        for t, (_, _, _, ax) in enumerate(which):
            cp = pltpu.make_async_remote_copy(
                src_ref=_half_of(ins[t], ax, c), dst_ref=_half_of(outs[t], ax, c), send_sem=send_sems.at[t],
                recv_sem=recv_sems.at[t], device_id=(x, y, 1 - c), device_id_type=MESH)
            cp.start()
            cps.append(cp)
        for t, cp in enumerate(cps):
            cp.wait_send()
            theirs = _half_of(outs[t], which[t][3], 1 - c)
            pltpu.make_async_remote_copy(
                src_ref=theirs, dst_ref=theirs, send_sem=send_sems.at[t], recv_sem=recv_sems.at[t],
                device_id=(x, y, 1 - c), device_id_type=MESH).wait_recv()

    return pl.pallas_call(
        body, name=name,
        in_specs=[ANY] * n, out_specs=[ANY] * n,
        out_shape=[jax.ShapeDtypeStruct(s.shape, F32) for s in shards],
        input_output_aliases={t: t for t in range(n)},
        scratch_shapes=[pltpu.SemaphoreType.DMA((n,)), pltpu.SemaphoreType.DMA((n,))],
        compiler_params=pltpu.CompilerParams(has_side_effects=True),
    )(*shards)


def allreduce_small(pack):
    rows = pack.shape[0]
    hr = rows // 2

    def body(p_ref, o_ref, sib_ref, parts_ref, send_sems, recv_sems):
        x, y, c = _place()
        mine = 2 * x + y
        sibling = (x, y, 1 - c)
        swap = pltpu.make_async_remote_copy(
            src_ref=p_ref, dst_ref=sib_ref, send_sem=send_sems.at[0], recv_sem=recv_sems.at[0],
            device_id=sibling, device_id_type=MESH)
        swap.start()
        swap.wait()
        half = pl.ds(pl.multiple_of(c * hr, 8), hr)
        parts_ref[mine] = p_ref[half, :] + sib_ref[half, :]
        cps = []
        for j, chip in enumerate(_other_chips(x, y)):
            cp = pltpu.make_async_remote_copy(
                src_ref=parts_ref.at[mine], dst_ref=parts_ref.at[mine], send_sem=send_sems.at[1 + j],
                recv_sem=recv_sems.at[1 + j], device_id=(*chip, c), device_id_type=MESH)
            cp.start()
            cps.append(cp)
        for j, (px, py) in enumerate(_other_chips(x, y)):
            cps[j].wait_send()
            theirs = parts_ref.at[2 * px + py]
            pltpu.make_async_remote_copy(
                src_ref=theirs, dst_ref=theirs, send_sem=send_sems.at[1 + j], recv_sem=recv_sems.at[1 + j],
                device_id=(px, py, c), device_id_type=MESH).wait_recv()
        o_ref[half, :] = (parts_ref[0] + parts_ref[1]) + (parts_ref[2] + parts_ref[3])
        back = pltpu.make_async_remote_copy(
            src_ref=o_ref.at[half, :], dst_ref=o_ref.at[half, :], send_sem=send_sems.at[4],
            recv_sem=recv_sems.at[4], device_id=sibling, device_id_type=MESH)
        back.start()
        back.wait_send()
        other = pl.ds(pl.multiple_of((1 - c) * hr, 8), hr)
        pltpu.make_async_remote_copy(
            src_ref=o_ref.at[other, :], dst_ref=o_ref.at[other, :], send_sem=send_sems.at[4],
            recv_sem=recv_sems.at[4], device_id=sibling, device_id_type=MESH).wait_recv()

    return pl.pallas_call(
        body, name="allreduce_small",
        in_specs=[pl.BlockSpec(memory_space=pltpu.VMEM)], out_specs=pl.BlockSpec(memory_space=pltpu.VMEM),
        out_shape=jax.ShapeDtypeStruct((rows, 128), F32),
        scratch_shapes=[pltpu.VMEM((rows, 128), F32), pltpu.VMEM((4, hr, 128), F32),
                        pltpu.SemaphoreType.DMA((5,)), pltpu.SemaphoreType.DMA((5,))],
        compiler_params=pltpu.CompilerParams(has_side_effects=True, vmem_limit_bytes=40 << 20),
    )(pack)


def _ssm_discretize(lam_re, lam_im, log_dt, b_re, b_im):
    dt = jnp.exp(log_dt)[:, None]
    mag = jnp.exp(lam_re * dt)
    ar, ai = mag * jnp.cos(lam_im * dt), mag * jnp.sin(lam_im * dt)
    den = lam_re * lam_re + lam_im * lam_im
    nr, ni = ar - 1.0, ai
    fr, fi = (nr * lam_re + ni * lam_im) / den, (ni * lam_re - nr * lam_im) / den
    bbr = fr[:, None, :] * b_re - fi[:, None, :] * b_im
    bbi = fr[:, None, :] * b_im + fi[:, None, :] * b_re
    return ar, ai, bbr, bbi


def _cmul(a, b):
    return a[0] * b[0] - a[1] * b[1], a[0] * b[1] + a[1] * b[0]


def _scan_powers(ar, ai):
    a = (ar.reshape(1, SW), ai.reshape(1, SW))
    big = a
    for _ in range(SEG.bit_length() - 1):
        big = _cmul(big, big)
    assert 1 << (SEG.bit_length() - 1) == SEG
    pows = {1: big}
    pows[2] = _cmul(big, big)
    pows[4] = _cmul(pows[2], pows[2])
    rid = jnp.arange(8)[:, None]
    ones = jnp.ones((8, 1), F32)
    fwd, rev = [ones * a[0], ones * a[1]], [ones * a[0], -ones * a[1]]
    for k in (1, 2, 4):
        pr, pi = pows[k]
        fwd += [jnp.where(rid >= k, pr, 0.0), jnp.where(rid >= k, pi, 0.0)]
        rev += [jnp.where(rid < 8 - k, pr, 0.0), jnp.where(rid < 8 - k, -pi, 0.0)]
    return jnp.concatenate(fwd + rev, axis=0)


def _pack_b(bb):
    t = bb.reshape(NLC, 8, GCH, NSTATE)
    return jnp.einsum("jgcp,gh->jgchp", t, jnp.eye(8, dtype=bb.dtype)).reshape(NLC, 128, LC)


def _unpack_b(db):
    t = jnp.einsum("jgchp,gh->jgcp", db.reshape(NLC, 8, GCH, 8, NSTATE), jnp.eye(8, dtype=db.dtype))
    return t.reshape(GROUPS, GCH, NSTATE)


def _pack_c(cc):
    t = cc.reshape(NLC, 8, GCH, NSTATE).transpose(0, 1, 3, 2)
    return jnp.einsum("jgpc,gh->jgphc", t, jnp.eye(8, dtype=cc.dtype)).reshape(NLC, LC, 128)


def _unpack_c(dc):
    t = jnp.einsum("jgphc,gh->jgpc", dc.reshape(NLC, 8, NSTATE, 8, GCH), jnp.eye(8, dtype=dc.dtype))
    return t.transpose(0, 1, 3, 2).reshape(GROUPS, GCH, NSTATE)


SMALL = ("norm_mix", "q_norm", "k_norm", "attn_sinks", "lam_re", "lam_im", "log_dt", "ssm_b_re", "ssm_b_im",
         "ssm_c_re", "ssm_c_im", "ssm_d", "attn_branch_norm", "ssm_branch_norm", "norm_ffn")


def _pack_small(arrs, rows=None):
    flat = jnp.concatenate([a.reshape(-1).astype(F32) for a in arrs])
    n = flat.shape[0]
    if rows is None:
        rows = -(-n // (128 * 256)) * 256
    return jnp.pad(flat, (0, rows * 128 - n)).reshape(rows, 128)


def _unpack_small(pack, like):
    flat = pack.reshape(-1)
    out, off = [], 0
    for a in like:
        out.append(flat[off:off + a.size].reshape(a.shape))
        off += a.size
    return out


def _gather_half_copies(dims):
    def copies(ins, lands):
        x, y, c = _place()
        mine = 2 * x + y
        out = []
        for t, d in enumerate(dims):
            size = (d[0] if d[2] == 0 else d[1]) // 4
            for chip in _other_chips(x, y):
                out.append((_rows_half(ins[t], c), _gather_piece(lands[t], d, mine, c), (*chip, c)))
            out.append((ins[t], _shard_of(lands[t], d[2], mine, size), (x, y, 1 - c)))
        return out

    def arrivals(ins, lands):
        x, y, c = _place()
        mine = 2 * x + y
        out = []
        for t, d in enumerate(dims):
            size = (d[0] if d[2] == 0 else d[1]) // 4
            for px, py in _other_chips(x, y):
                out.append(_gather_piece(lands[t], d, 2 * px + py, c))
            out.append(_shard_of(lands[t], d[2], mine, size))
        return out

    return copies, arrivals


def gather_forward(fulls, dims):
    n = len(fulls)

    def body(*refs):
        ins, outs = refs[:n], refs[n:2 * n]
        send_sems, recv_sems = refs[2 * n:]
        x, y, c = _place()
        cps = []
        for t in range(n):
            for j, (px, py) in enumerate(_other_chips(x, y)):
                cp = pltpu.make_async_remote_copy(
                    src_ref=_gather_piece(ins[t], dims[t], 2 * px + py, c),
                    dst_ref=_gather_piece(outs[t], dims[t], 2 * px + py, c),
                    send_sem=send_sems.at[3 * t + j], recv_sem=recv_sems.at[3 * t + j],
                    device_id=(x, y, 1 - c), device_id_type=MESH)
                cp.start()
                cps.append(cp)
        for t in range(n):
            for j, (px, py) in enumerate(_other_chips(x, y)):
                cps[3 * t + j].wait_send()
                other = _gather_piece(outs[t], dims[t], 2 * px + py, 1 - c)
                pltpu.make_async_remote_copy(
                    src_ref=other, dst_ref=other, send_sem=send_sems.at[3 * t + j], recv_sem=recv_sems.at[3 * t + j],
                    device_id=(x, y, 1 - c), device_id_type=MESH).wait_recv()

    return pl.pallas_call(
        body, name="gather_forward",
        in_specs=[ANY] * n, out_specs=[ANY] * n,
        out_shape=[jax.ShapeDtypeStruct(a.shape, a.dtype) for a in fulls],
        input_output_aliases={t: t for t in range(n)},
        scratch_shapes=[pltpu.SemaphoreType.DMA((3 * n,)), pltpu.SemaphoreType.DMA((3 * n,))],
        compiler_params=pltpu.CompilerParams(has_side_effects=True),
    )(*fulls)


def _tie(a, token):
    return a if token is None else a + token[0, 0]


class _FirstGather:
    dims = [SHARDED[0][1:], (N_META, D, 1)]

    def __init__(self, w_in_shard, meta_shard):
        self.copies, self.arrivals = _gather_half_copies(self.dims)
        lands = [lax.empty((D, IN_COLS), BF), lax.empty((N_META, D), F32)]
        self.send, self.recv, self.srcs, self.lands, self.token = _split_copy_start(
            "gather_start_first", self.copies, 8, [w_in_shard, meta_shard], lands, w_in_shard)

    def finish(self, after):
        _, lands = _split_copy_wait("gather_wait_first", self.copies, self.arrivals, self.send, self.recv, self.srcs,
                                    self.lands, after)
        return gather_forward(lands, self.dims)


class _Exchange:
    def __init__(self, rest_shards, place):
        self.rest_shards = rest_shards
        self.place = place
        self.rest_dims = [s[1:] for s in SHARDED[1:]]
        self.pending = {}
        self.halves = {}
        self.full = {}
        self.parts = {}

    def begin(self, after):
        copies, arrivals = _gather_copies(self.rest_dims)
        lands = [lax.empty((r, c), BF) for r, c, _ in self.rest_dims]
        send, recv, srcs, lands, token = _split_copy_start(
            "gather_start", copies, 7 * len(lands), self.rest_shards, lands, after)
        self.pending["gather"] = (copies, arrivals, send, recv, srcs, lands)
        return token

    def rest_weights(self, after):
        copies, arrivals, send, recv, srcs, lands = self.pending.pop("gather")
        _, fulls = _split_copy_wait("gather_wait", copies, arrivals, send, recv, srcs, lands, after)
        return dict(zip([s[0] for s in SHARDED[1:]], fulls))

    def _halves(self, group, idxs, grads):
        got = pair_exchange("pair_exchange_" + group, grads, idxs)
        return [pair_sum("pair_sum_" + SHARDED[i][0], self.place, g, b, SHARDED[i][3])
                for i, g, b in zip(idxs, grads, got)]

    def start(self, group, idxs, grads):
        if grads[0][1] is None:
            halves = self._halves(group, idxs, [g for g, _ in grads])
            copies, arrivals = _scatter3_copies(idxs)
            lands = [lax.empty((3,) + sh, BF) for sh in _piece_shapes(idxs)]
            send, recv, srcs, lands, token = _split_copy_start(
                "scatter_start_" + group, copies, 3 * len(idxs), halves, lands, grads[0][0])
            self.pending[group] = (copies, arrivals, send, recv, srcs, lands, idxs)
            return token
        copies, arrivals = _scatter8_copies(idxs)
        lands = [lax.empty((7,) + sh, BF) for sh in _piece_shapes(idxs)]
        send, recv, srcs, lands, token = _split_copy_start(
            "scatter_start_" + group, copies, 7 * len(idxs), [g16 for _, g16 in grads], lands, grads[0][0])
        self.pending[group] = (copies, arrivals, send, recv, srcs, lands, idxs)
        for i, (g32, _) in zip(idxs, grads):
            self.full[i] = g32
        return token

    def finish(self, group, after):
        if group not in self.pending:
            return
        copies, arrivals, send, recv, srcs, lands, idxs = self.pending.pop(group)
        srcs, lands = _split_copy_wait("scatter_wait_" + group, copies, arrivals, send, recv, srcs, lands, after)
        for i, h, pt in zip(idxs, srcs, lands):
            self.parts[i] = pt
            if i not in self.full:
                self.halves[i] = h


def local_step(x, tgt, p, first, ex):
    qn_t = jnp.tile(p["q_norm"].reshape(1, HEAD), (1, NQ)) * HEAD ** -0.5
    kn_t = jnp.tile(p["k_norm"].reshape(1, HEAD), (1, NKV))
    seg = jnp.arange(256) // HEAD
    bd = (seg[:, None] == seg[None, :]).astype(BF)
    nm = p["norm_mix"].reshape(1, D)
    sinks = p["attn_sinks"].reshape(NQ)
    dsk = p["ssm_d"].reshape(1, D)
    abn, sbn, gf = (p[k].reshape(1, D) for k in ("attn_branch_norm", "ssm_branch_norm", "norm_ffn"))

    disc_in = (p["lam_re"], p["lam_im"], _tie(p["log_dt"], first.token), p["ssm_b_re"], p["ssm_b_im"])
    (ar, ai, bbr, bbi), disc_vjp = jax.vjp(_ssm_discretize, *disc_in)
    pw = _scan_powers(ar, ai)
    brp, bip = _pack_b(bbr.astype(BF)), _pack_b(bbi.astype(BF))
    crp = _pack_c(_tie(p["ssm_c_re"], first.token).astype(BF))
    cip = _pack_c(_tie(p["ssm_c_im"], first.token).astype(BF))

    w_in, meta_full = first.finish(brp)
    mpad = jnp.concatenate([jnp.zeros((META0, D), F32), meta_full], axis=0)
    token = ex.begin(w_in)
    xn, qr, kr, qn, kn, v, u, gg = in_proj_fwd(x, mpad, w_in, _tie(nm, token), qn_t, kn_t, bd)
    attn = attention_fwd(sinks, qn, kn, v)
    perm = _segment_order()
    y, sr, si = ssm_fwd(u, pw, brp, bip, crp, cip, dsk, perm)
    w = ex.rest_weights(y)
    zb, zzb, mgb, h1 = mid_fwd(y, attn, gg, x, mpad, w["w_glu"], w["w_out"], abn, sbn)
    hnb, dgub, actb, dh2b, dh1, dh1b, dgf, loss = ffn_fwd_bwd(h1, tgt, w["w_ffn_in"], w["w_ffn_out"], gf)
    token = ex.start("ffn", (3, 4), [weight_grad("grad_w_ffn_in", hnb, dgub, 1408, True),
                                     weight_grad("grad_w_ffn_out", actb, dh2b, 512, True)])
    dy, dattn, dggb, dzzb, dabn, dsbn = mid_bwd(dh1b, y, attn, gg, zzb, w["w_glu"], w["w_out"], _tie(abn, token),
                                                sbn)
    grads_mid = [weight_grad("grad_w_glu", zb, dzzb, 1024, True), weight_grad("grad_w_out", mgb, dh1b, 1024, True)]
    dub, da, dbr, dbi, dcr, dci, dd = ssm_bwd(dy, u, sr, si, pw, brp, bip, crp, cip, dsk, perm)
    ex.finish("ffn", dub)
    token = ex.start("mid", (1, 2), grads_mid)
    dqn, dkn, dv, dsink = attention_bwd(_tie(sinks, token), qn, kn, v, dattn)
    dproj, dx, dmpad, dnm, dqw, dkw = in_proj_bwd(dqn, dkn, dv, dub, dggb, qr, kr, x, mpad, dh1, w_in, nm, qn_t,
                                                  kn_t, bd)
    ex.finish("mid", dproj)
    ex.start("in", (0,), [(weight_grad("grad_w_in", xn, dproj, 1536), None)])

    dar = jnp.sum(da[0:8], axis=0).reshape(GROUPS, NSTATE)
    dai = jnp.sum(da[8:16], axis=0).reshape(GROUPS, NSTATE)
    dlr, dli, dldt, dbre, dbim = disc_vjp((dar, dai, _unpack_b(dbr), _unpack_b(dbi)))
    small = {
        "norm_mix": dnm, "q_norm": dqw.reshape(NQ, HEAD).sum(0) * HEAD ** -0.5, "k_norm": dkw.reshape(NKV, HEAD).sum(0),
        "attn_sinks": dsink[0, 0:NQ], "lam_re": dlr, "lam_im": dli, "log_dt": dldt,
        "ssm_b_re": dbre, "ssm_b_im": dbim, "ssm_c_re": _unpack_c(jnp.swapaxes(dcr, 1, 2)), "ssm_c_im": _unpack_c(jnp.swapaxes(dci, 1, 2)),
        "ssm_d": dd, "attn_branch_norm": dabn, "ssm_branch_norm": dsbn, "norm_ffn": dgf,
    }
    return loss[0, 0], dx, dmpad[META0:PAD], small


def kernel(x, meta_tokens, norm_mix, w_in, q_norm, k_norm, attn_sinks, lam_re, lam_im, log_dt, ssm_b_re, ssm_b_im, ssm_c_re, ssm_c_im, ssm_d, w_glu, attn_branch_norm, ssm_branch_norm, w_out, norm_ffn, w_ffn_in, w_ffn_out, loss_target, m_meta_tokens, m_norm_mix, m_w_in, m_q_norm, m_k_norm, m_attn_sinks, m_lam_re, m_lam_im, m_log_dt, m_ssm_b_re, m_ssm_b_im, m_ssm_c_re, m_ssm_c_im, m_ssm_d, m_w_glu, m_attn_branch_norm, m_ssm_branch_norm, m_w_out, m_norm_ffn, m_w_ffn_in, m_w_ffn_out, v_meta_tokens, v_norm_mix, v_w_in, v_q_norm, v_k_norm, v_attn_sinks, v_lam_re, v_lam_im, v_log_dt, v_ssm_b_re, v_ssm_b_im, v_ssm_c_re, v_ssm_c_im, v_ssm_d, v_w_glu, v_attn_branch_norm, v_ssm_branch_norm, v_w_out, v_norm_ffn, v_w_ffn_in, v_w_ffn_out):
    args = dict(locals())
    names = ["meta_tokens", "norm_mix", "w_in", "q_norm", "k_norm", "attn_sinks", "lam_re", "lam_im", "log_dt",
             "ssm_b_re", "ssm_b_im", "ssm_c_re", "ssm_c_im", "ssm_d", "w_glu", "attn_branch_norm",
             "ssm_branch_norm", "w_out", "norm_ffn", "w_ffn_in", "w_ffn_out"]
    big_names = [s[0] for s in SHARDED]
    xs, ys, cs = _place()
    chip = 2 * xs + ys

    first = _FirstGather(args["w_in"][0].astype(BF), meta_tokens)
    shards = [None] + [_tie(args[n][0], first.token).astype(BF) for n in big_names[1:]]
    place = jnp.stack([chip, cs]).astype(jnp.int32)
    ex = _Exchange(shards[1:], place)

    swapped = ("ssm_b_re", "ssm_b_im")

    def natural(n, a):
        return jnp.swapaxes(a, -1, -2) if n in swapped else a

    p = {n: natural(n, args[n][0]) for n in SMALL}
    loss, dx, dmeta, gsmall = local_step(x[0], loss_target[0], p, first, ex)

    out = {}

    def finish(idxs, name):
        mine = [sum8("sum8_" + SHARDED[i][0], place, ex.parts[i], ex.full[i], SHARDED[i][3]) if i in ex.full
                else sum4("sum4_" + SHARDED[i][0], place, ex.parts[i], ex.halves[i], SHARDED[i][3]) for i in idxs]
        for i, g in zip(idxs, join_halves(name, mine, idxs)):
            n = SHARDED[i][0]
            out[n] = tuple(t[None] for t in adamw("adamw_" + n, args[n][0], g, args["m_" + n][0], args["v_" + n][0]))

    finish((1, 2, 3, 4), "join_halves_rest")
    small_list = [gsmall[n].reshape(natural(n, args[n]).shape) for n in SMALL] + [dmeta, loss.reshape(1, 1)]
    red = allreduce_small(_pack_small(small_list))
    *small_red, dmeta_sum, loss = _unpack_small(red, small_list)
    loss = loss[0, 0]
    gmeta = lax.dynamic_slice_in_dim(dmeta_sum, chip * (D // 4), D // 4, axis=1)
    out["meta_tokens"] = tuple(adamw("adamw_meta", meta_tokens, gmeta, m_meta_tokens, v_meta_tokens))
    ds, nms, nvs = adamw_small([natural(n, args[n]) for n in SMALL], small_red,
                               [natural(n, args["m_" + n]) for n in SMALL],
                               [natural(n, args["v_" + n]) for n in SMALL])
    for n, g_, d_, m_, v_ in zip(SMALL, small_red, ds, nms, nvs):
        out[n] = tuple(natural(n, t) for t in (g_, d_, m_, v_))
    ex.finish("in", out["w_ffn_in"][3])
    finish((0,), "join_halves_in")

    res = [loss, dx[None]]
    for k in range(4):
        res += [out[n][k] for n in names]
    return tuple(res)
```

```python
import functools
import math

import jax
import jax.numpy as jnp
from jax import lax
from jax.experimental import pallas as pl
from jax.experimental.pallas import tpu as pltpu

F32 = jnp.float32
BF = jnp.bfloat16

D = 1024
N_META = 16
HEAD = 64
NQ = 16
NKV = 4
QW = NQ * HEAD
KVW = NKV * HEAD
WIN = 128
GROUPS = 64
GCH = 16
NSTATE = 64
SW = GROUPS * NSTATE
DFF = 2816
IN_COLS = QW + 2 * KVW + D + 2 * D
PAD = 256
META0 = PAD - N_META
EPS = 1e-6
NEG = -1e30

TM = 256
TS = 256
LC = 512
NLC = SW // LC

LR, B1, B2, AEPS, WD, STEP = 0.001, 0.9, 0.999, 1e-08, 0.01, 10
BC1 = 1.0 - B1 ** STEP
BC2 = 1.0 - B2 ** STEP

MESH = pl.DeviceIdType.MESH
ANY = pl.BlockSpec(memory_space=pl.ANY)


def _cp(sem=None, vmem_mb=48):
    return pltpu.CompilerParams(dimension_semantics=sem, vmem_limit_bytes=vmem_mb << 20)


def _full(shape):
    n = len(shape)
    return pl.BlockSpec(shape, lambda *a: (0,) * n)


def _const(shape):
    n = len(shape)
    return pl.BlockSpec(shape, lambda *a: (0,) * n, pipeline_mode=pl.Buffered(1))


def _rows(tm, width):
    return pl.BlockSpec((tm, width), lambda i: (i, 0))


def _dot(a, b):
    return jnp.dot(a, b, preferred_element_type=F32)


def _dot_nt(a, b):
    return lax.dot_general(a, b, (((1,), (1,)), ((), ())), preferred_element_type=F32)


def _dot_tn(a, b):
    return lax.dot_general(a, b, (((0,), (0,)), ((), ())), preferred_element_type=F32)


def _sigmoid(x):
    return 0.5 * jnp.tanh(0.5 * x) + 0.5


def _seg_mean(x, bd, split=True):
    outs = []
    for j in range(x.shape[1] // 256):
        c = x[:, 256 * j:256 * (j + 1)]
        hi = c.astype(BF)
        part = _dot(hi, bd)
        if split:
            part += _dot((c - hi.astype(F32)).astype(BF), bd)
        outs.append(part)
    out = outs[0] if len(outs) == 1 else jnp.concatenate(outs, axis=1)
    return out * (1.0 / HEAD)


_GC = math.sqrt(2.0 / math.pi)


def _gelu(x):
    t = jnp.tanh(_GC * (x + 0.044715 * x * x * x))
    return 0.5 * x * (1.0 + t), t


def _gelu_grad(x, t):
    return 0.5 * (1.0 + t) + 0.5 * x * (1.0 - t * t) * _GC * (1.0 + 3.0 * 0.044715 * x * x)


def _seq_specs():
    assert TM == PAD
    return [pl.BlockSpec((TM, D), lambda i: (jnp.maximum(i - 1, 0), 0)), _full((PAD, D))]


def _seq_tile(x_ref, m_ref):
    return jnp.where(pl.program_id(0) == 0, m_ref[...], x_ref[...])


def in_proj_fwd(x, mpad, w_in, nm, qn_t, kn_t, bd):
    lp = x.shape[0] + PAD

    def body(x_ref, m_ref, w_ref, nm_ref, qn_ref, kn_ref, bd_ref,
             xn_o, qr_o, kr_o, qn_o, kn_o, v_o, u_o, gg_o):
        h = _seq_tile(x_ref, m_ref)
        r = lax.rsqrt(jnp.mean(h * h, axis=-1, keepdims=True) + EPS)
        xb = ((h * r) * nm_ref[...]).astype(BF)
        xn_o[...] = xb
        bdv = bd_ref[...]
        q = _dot(xb, w_ref[:, 0:QW])
        qr_o[...] = q
        qn_o[...] = (q * lax.rsqrt(_seg_mean(q * q, bdv, False) + EPS) * qn_ref[...]).astype(BF)
        k = _dot(xb, w_ref[:, QW:QW + KVW])
        kr_o[...] = k
        kn_o[...] = (k * lax.rsqrt(_seg_mean(k * k, bdv, False) + EPS) * kn_ref[...]).astype(BF)
        v_o[...] = _dot(xb, w_ref[:, QW + KVW:QW + 2 * KVW]).astype(BF)
        u_o[...] = _dot(xb, w_ref[:, QW + 2 * KVW:QW + 2 * KVW + D])
        gg_o[...] = _dot(xb, w_ref[:, QW + 2 * KVW + D:IN_COLS])

    outs = [(D, BF), (QW, F32), (KVW, F32), (QW, BF), (KVW, BF), (KVW, BF), (D, F32), (2 * D, F32)]
    return pl.pallas_call(
        body, name="in_proj_fwd", grid=(lp // TM,),
        in_specs=_seq_specs() + [_full((D, IN_COLS)), _full((1, D)), _full((1, QW)), _full((1, KVW)),
                                 _full((256, 256))],
        out_specs=[_rows(TM, w) for w, _ in outs],
        out_shape=[jax.ShapeDtypeStruct((lp, w), t) for w, t in outs],
        compiler_params=_cp(("arbitrary",)),
    )(x, mpad, w_in, nm, qn_t, kn_t, bd)


def in_proj_bwd(dqn, dkn, dv, du, dgg, qr, kr, x, mpad, dh1, w_in, nm, qn_t, kn_t, bd):
    lp = x.shape[0] + PAD

    def body(dqn_ref, dkn_ref, dv_ref, du_ref, dgg_ref, qr_ref, kr_ref, x_ref, m_ref, dh1_ref,
             w_ref, nm_ref, qn_ref, kn_ref, bd_ref,
             dproj_o, dx_o, dm_o, dnm_o, dqw_o, dkw_o):
        i = pl.program_id(0)

        @pl.when(i == 0)
        def _():
            dnm_o[...] = jnp.zeros_like(dnm_o)
            dqw_o[...] = jnp.zeros_like(dqw_o)
            dkw_o[...] = jnp.zeros_like(dkw_o)

        bdv = bd_ref[...]

        def norm_bwd(x, dy, g, acc_ref):
            rr = lax.rsqrt(_seg_mean(x * x, bdv, False) + EPS)
            xh = x * rr
            acc_ref[...] += jnp.sum(dy * xh, axis=0, keepdims=True)
            dxh = dy * g
            return rr * (dxh - xh * _seg_mean(dxh * xh, bdv))

        dq = norm_bwd(qr_ref[...], dqn_ref[...], qn_ref[...], dqw_o)
        dk = norm_bwd(kr_ref[...], dkn_ref[...], kn_ref[...], dkw_o)
        dproj_o[:, 0:QW] = dq.astype(BF)
        dproj_o[:, QW:QW + KVW] = dk.astype(BF)
        dproj_o[:, QW + KVW:QW + 2 * KVW] = dv_ref[...].astype(BF)
        dproj_o[:, QW + 2 * KVW:QW + 2 * KVW + D] = du_ref[...]
        dproj_o[:, QW + 2 * KVW + D:IN_COLS] = dgg_ref[...]
        dxn = _dot_nt(dproj_o[...], w_ref[...])
        h = _seq_tile(x_ref, m_ref)
        r = lax.rsqrt(jnp.mean(h * h, axis=-1, keepdims=True) + EPS)
        xh = h * r
        dnm_o[...] += jnp.sum(dxn * xh, axis=0, keepdims=True)
        dxh = dxn * nm_ref[...]
        dh0 = dh1_ref[...] + r * (dxh - xh * jnp.mean(dxh * xh, axis=-1, keepdims=True))
        dx_o[...] = dh0

        @pl.when(i == 0)
        def _():
            dm_o[...] = dh0

    return pl.pallas_call(
        body, name="in_proj_bwd", grid=(lp // TM,),
        in_specs=[_rows(TM, QW), _rows(TM, KVW), _rows(TM, KVW), _rows(TM, D), _rows(TM, 2 * D),
                  _rows(TM, QW), _rows(TM, KVW)] + _seq_specs() + [_rows(TM, D),
                  _full((D, IN_COLS)), _full((1, D)), _full((1, QW)), _full((1, KVW)), _full((256, 256))],
        out_specs=[_rows(TM, IN_COLS), _seq_specs()[0], _full((PAD, D)), _full((1, D)), _full((1, QW)),
                   _full((1, KVW))],
        out_shape=[jax.ShapeDtypeStruct((lp, IN_COLS), BF), jax.ShapeDtypeStruct((lp - PAD, D), F32),
                   jax.ShapeDtypeStruct((PAD, D), F32),
                   jax.ShapeDtypeStruct((1, D), F32), jax.ShapeDtypeStruct((1, QW), F32),
                   jax.ShapeDtypeStruct((1, KVW), F32)],
        compiler_params=_cp(("arbitrary",)),
    )(dqn, dkn, dv, du, dgg, qr, kr, x, mpad, dh1, w_in, nm, qn_t, kn_t, bd)


def _att_bias(b):
    qi = lax.broadcasted_iota(jnp.int32, (WIN, WIN), 0)
    kj = lax.broadcasted_iota(jnp.int32, (WIN, WIN), 1)
    upper = kj > qi
    valid = (upper & ((b - 1) * WIN + kj >= PAD)) | (jnp.logical_not(upper) & (b * WIN + kj >= META0))
    bias_pc = jnp.where(valid, 0.0, NEG)
    kj1 = lax.broadcasted_iota(jnp.int32, (1, WIN), 1)
    bias_m = jnp.where((kj1 >= WIN - N_META) & (b * WIN >= PAD), 0.0, NEG)
    return bias_m, bias_pc, upper


def _dup_half(ref, kp, pos, lo):
    pair = ref[:, 128 * kp:128 * kp + 128].astype(F32)
    rolled = pltpu.roll(pair, 64, axis=1)
    keep = lo if pos == 0 else jnp.logical_not(lo)
    return jnp.where(keep, pair, rolled).astype(BF)


def _stack_heads(ref, kv, lo):
    parts = []
    for pp in range(2):
        pair = ref[:, 256 * kv + 128 * pp:256 * kv + 128 * pp + 128]
        zero = jnp.zeros_like(pair)
        parts.append(jnp.where(lo, pair, zero))
        parts.append(jnp.where(lo, zero, pair))
    return jnp.concatenate(parts, axis=0)


def _unstack_heads(x4, lo):
    a = jnp.where(lo, x4[0:WIN], x4[WIN:2 * WIN])
    b = jnp.where(lo, x4[2 * WIN:3 * WIN], x4[3 * WIN:4 * WIN])
    return a, b


def _split_pc(x, upper):
    zero = jnp.zeros_like(x)
    return jnp.where(upper, x, zero).astype(BF), jnp.where(upper, zero, x).astype(BF)


def _head_softmax(s_m, s_p, s_c, bias, sink):
    bias_m, bias_pc, upper = bias
    sm = s_m + bias_m
    spc = jnp.where(upper, s_p, s_c) + bias_pc
    m = jnp.maximum(jnp.max(jnp.maximum(sm, spc), axis=-1, keepdims=True), sink)
    em, epc = jnp.exp(sm - m), jnp.exp(spc - m)
    esink = jnp.exp(sink - m)
    inv = 1.0 / (esink + jnp.sum(em + epc, axis=-1, keepdims=True))
    return em, epc, inv, esink


def _kv_specs():
    return [pl.BlockSpec((WIN, KVW), lambda b: (1, 0)),
            pl.BlockSpec((WIN, KVW), lambda b: (jnp.maximum(b - 1, 0), 0)),
            pl.BlockSpec((WIN, KVW), lambda b: (b, 0))]


def attention_fwd(sinks, qn, kn, v):
    lp = qn.shape[0]

    def body(sink_ref, q_ref, km_ref, kp_ref, kc_ref, vm_ref, vp_ref, vc_ref, o_ref):
        b = pl.program_id(0)

        @pl.when(b == 0)
        def _():
            o_ref[...] = jnp.zeros_like(o_ref)

        @pl.when(b > 0)
        def _():
            bias = _att_bias(b)
            lo = lax.broadcasted_iota(jnp.int32, (WIN, WIN), 1) < HEAD
            for kv in range(NKV):
                kp, pos = kv // 2, kv % 2
                kds = [_dup_half(r, kp, pos, lo) for r in (km_ref, kp_ref, kc_ref)]
                vds = [_dup_half(r, kp, pos, lo) for r in (vm_ref, vp_ref, vc_ref)]
                q4 = _stack_heads(q_ref, kv, lo)
                ss = [_dot_nt(q4, kd) for kd in kds]
                es, invs = ([], [], []), []
                for h in range(4):
                    rs = slice(WIN * h, WIN * h + WIN)
                    em, epc, inv, _ = _head_softmax(ss[0][rs], ss[1][rs], ss[2][rs], bias, sink_ref[4 * kv + h])
                    e_p, e_c = _split_pc(epc, bias[2])
                    for lst, e in zip(es, (em.astype(BF), e_p, e_c)):
                        lst.append(e)
                    invs.append(inv)
                e_m, e_p, e_c = (jnp.concatenate(lst, axis=0) for lst in es)
                o4 = (_dot(e_m, vds[0]) + _dot(e_p, vds[1]) + _dot(e_c, vds[2])) * jnp.concatenate(invs, axis=0)
                oa, ob = _unstack_heads(o4, lo)
                o_ref[:, 256 * kv:256 * kv + 128] = oa
                o_ref[:, 256 * kv + 128:256 * kv + 256] = ob

    return pl.pallas_call(
        body, name="attention_fwd", grid=(lp // WIN,),
        in_specs=[pl.BlockSpec(memory_space=pltpu.SMEM), _rows(WIN, QW)] + _kv_specs() + _kv_specs(),
        out_specs=_rows(WIN, QW),
        out_shape=jax.ShapeDtypeStruct((lp, QW), F32),
        compiler_params=_cp(("arbitrary",)),
    )(sinks, qn, kn, kn, kn, v, v, v)


def attention_bwd(sinks, qn, kn, v, do):
    lp = qn.shape[0]
    nb = lp // WIN

    def body(sink_ref, q_ref, km_ref, kp_ref, kc_ref, vm_ref, vp_ref, vc_ref, do_ref,
             dq_ref, dk_hbm, dv_hbm, dsink_ref, dk_acc, dv_acc):
        b = pl.program_id(0)

        @pl.when(b == 0)
        def _():
            dk_acc[...] = jnp.zeros_like(dk_acc)
            dv_acc[...] = jnp.zeros_like(dv_acc)
            dsink_ref[...] = jnp.zeros_like(dsink_ref)

        @pl.when(b == 0)
        def _():
            dq_ref[...] = jnp.zeros_like(dq_ref)

        @pl.when(b > 0)
        def _():
            bias = _att_bias(b)
            upper = bias[2]
            lo = lax.broadcasted_iota(jnp.int32, (WIN, WIN), 1) < HEAD
            lane8 = lax.broadcasted_iota(jnp.int32, (8, 128), 1)
            starts = [WIN, pl.multiple_of(jnp.maximum(b - 1, 0) * WIN, WIN), pl.multiple_of(b * WIN, WIN)]
            for kv in range(NKV):
                kp, pos = kv // 2, kv % 2
                keep = lo if pos == 0 else jnp.logical_not(lo)
                kds = [_dup_half(r, kp, pos, lo) for r in (km_ref, kp_ref, kc_ref)]
                vds = [_dup_half(r, kp, pos, lo) for r in (vm_ref, vp_ref, vc_ref)]
                q4 = _stack_heads(q_ref, kv, lo)
                do4 = _stack_heads(do_ref, kv, lo)
                ss = [_dot_nt(q4, kd) for kd in kds]
                dps = [_dot_nt(do4, vd) for vd in vds]
                ds_l, p_l = ([], [], []), ([], [], [])
                for h in range(4):
                    rs = slice(WIN * h, WIN * h + WIN)
                    em, epc, inv, esink = _head_softmax(ss[0][rs], ss[1][rs], ss[2][rs], bias, sink_ref[4 * kv + h])
                    p_m, p_pc = em * inv, epc * inv
                    dp_m = dps[0][rs]
                    dp_pc = jnp.where(upper, dps[1][rs], dps[2][rs])
                    delta = jnp.sum(p_m * dp_m + p_pc * dp_pc, axis=-1, keepdims=True)
                    val = -jnp.sum(esink * inv * delta, axis=0, keepdims=True)
                    dsink_ref[...] += jnp.where(lane8 == 4 * kv + h, val, 0.0)
                    for lst, t in zip(ds_l, ((p_m * (dp_m - delta)).astype(BF),) + _split_pc(p_pc * (dp_pc - delta), upper)):
                        lst.append(t)
                    for lst, t in zip(p_l, (p_m.astype(BF),) + _split_pc(p_pc, upper)):
                        lst.append(t)
                dss = [jnp.concatenate(lst, axis=0) for lst in ds_l]
                ps = [jnp.concatenate(lst, axis=0) for lst in p_l]
                dq4 = _dot(dss[0], kds[0]) + _dot(dss[1], kds[1]) + _dot(dss[2], kds[2])
                dqa, dqb = _unstack_heads(dq4, lo)
                dq_ref[:, 256 * kv:256 * kv + 128] = dqa
                dq_ref[:, 256 * kv + 128:256 * kv + 256] = dqb
                for x in range(3):
                    dkd = _dot_tn(dss[x], q4)
                    dvd = _dot_tn(ps[x], do4)
                    dkd = jnp.where(keep, dkd + pltpu.roll(dkd, 64, axis=1), 0.0)
                    dvd = jnp.where(keep, dvd + pltpu.roll(dvd, 64, axis=1), 0.0)
                    dk_acc[pl.ds(starts[x], WIN), 128 * kp:128 * kp + 128] += dkd
                    dv_acc[pl.ds(starts[x], WIN), 128 * kp:128 * kp + 128] += dvd

        @pl.when(b == nb - 1)
        def _():
            pltpu.sync_copy(dk_acc, dk_hbm)
            pltpu.sync_copy(dv_acc, dv_hbm)

    return pl.pallas_call(
        body, name="attention_bwd", grid=(nb,),
        in_specs=[pl.BlockSpec(memory_space=pltpu.SMEM), _rows(WIN, QW)] + _kv_specs() + _kv_specs()
                 + [_rows(WIN, QW)],
        out_specs=[_rows(WIN, QW), ANY, ANY, _full((8, 128))],
        out_shape=[jax.ShapeDtypeStruct((lp, QW), F32), jax.ShapeDtypeStruct((lp, KVW), F32),
                   jax.ShapeDtypeStruct((lp, KVW), F32), jax.ShapeDtypeStruct((8, 128), F32)],
        scratch_shapes=[pltpu.VMEM((lp, KVW), F32), pltpu.VMEM((lp, KVW), F32)],
        compiler_params=_cp(("arbitrary",)),
    )(sinks, qn, kn, kn, kn, v, v, v, do)


PW_ROWS = 64
SEG = TS // 8


def _segment_order():
    rho = jnp.arange(TS)
    token = SEG * (rho % 8) + rho // 8
    p = (token[:, None] == jnp.arange(TS)[None, :]).astype(BF)
    return p, p.T


def _reorder(p, x):
    hi = x.astype(BF)
    lo = (x - hi.astype(F32)).astype(BF)
    return _dot(p, hi) + _dot(p, lo)


def _segment_scan(xr_ref, xi_ref, pw_ref, base, ls, c_r, c_i, reverse, visit=None, rescan=True):
    ar, ai = pw_ref[base:base + 8, ls], pw_ref[base + 8:base + 16, ls]
    order = list(range(SEG - 1, -1, -1)) if reverse else list(range(SEG))
    s_r = s_i = jnp.zeros_like(ar)
    for i in order:
        rows = pl.ds(8 * i, 8)
        s_r, s_i = ar * s_r - ai * s_i + xr_ref[rows, ls], ar * s_i + ai * s_r + xi_ref[rows, ls]
        if not rescan:
            xr_ref[rows, ls] = s_r
            xi_ref[rows, ls] = s_i
    rid = lax.broadcasted_iota(jnp.int32, s_r.shape, 0)
    if reverse:
        e_r = jnp.where(rid < 7, pltpu.roll(s_r, 7, axis=0), c_r)
        e_i = jnp.where(rid < 7, pltpu.roll(s_i, 7, axis=0), c_i)
    else:
        e_r = jnp.where(rid >= 1, pltpu.roll(s_r, 1, axis=0), c_r)
        e_i = jnp.where(rid >= 1, pltpu.roll(s_i, 1, axis=0), c_i)
    for n, k in enumerate((1, 2, 4)):
        shift = 8 - k if reverse else k
        t_r, t_i = pltpu.roll(e_r, shift, axis=0), pltpu.roll(e_i, shift, axis=0)
        kr = pw_ref[base + 16 + 16 * n:base + 24 + 16 * n, ls]
        ki = pw_ref[base + 24 + 16 * n:base + 32 + 16 * n, ls]
        e_r, e_i = e_r + kr * t_r - ki * t_i, e_i + kr * t_i + ki * t_r
    last = 0 if reverse else 7
    a16r, a16i = pw_ref[base + 16 + last:base + 17 + last, ls], pw_ref[base + 24 + last:base + 25 + last, ls]
    lr, li, fr, fi = (t[last:last + 1] for t in (e_r, e_i, s_r, s_i))
    out = (a16r * lr - a16i * li + fr, a16r * li + a16i * lr + fi)
    s_r, s_i = e_r, e_i
    extra = None
    for i in order:
        rows = pl.ds(8 * i, 8)
        if rescan:
            s_r, s_i = ar * s_r - ai * s_i + xr_ref[rows, ls], ar * s_i + ai * s_r + xi_ref[rows, ls]
            f_r, f_i = s_r, s_i
        else:
            s_r, s_i = ar * s_r - ai * s_i, ar * s_i + ai * s_r
            f_r, f_i = xr_ref[rows, ls] + s_r, xi_ref[rows, ls] + s_i
        xr_ref[rows, ls] = f_r
        xi_ref[rows, ls] = f_i
        if visit is not None:
            extra = visit(i, f_r, f_i, extra)
    return out if visit is None else (out, extra)


def ssm_fwd(u, pw, br, bi, cr, ci, dsk, perm):
    lp = u.shape[0]

    def body(u_ref, pw_ref, br_ref, bi_ref, cr_ref, ci_ref, d_ref, p_ref, pt_ref,
             y_ref, sr_ref, si_ref, car_r, car_i, yp_s):
        i = pl.program_id(0)

        @pl.when(i == 0)
        def _():
            car_r[...] = jnp.zeros_like(car_r)
            car_i[...] = jnp.zeros_like(car_i)

        u = u_ref[...]
        ub = _dot(p_ref[...], u.astype(BF)).astype(BF)
        for j in range(NLC):
            uj = ub[:, 128 * j:128 * j + 128]
            sr_ref[:, LC * j:LC * j + LC] = _dot(uj, br_ref[j])
            si_ref[:, LC * j:LC * j + LC] = _dot(uj, bi_ref[j])
        for j in range(NLC):
            ls = slice(LC * j, LC * j + LC)
            car_r[:, ls], car_i[:, ls] = _segment_scan(sr_ref, si_ref, pw_ref, 0, ls, car_r[:, ls], car_i[:, ls],
                                                       False, rescan=False)
        for j in range(NLC):
            ls = slice(LC * j, LC * j + LC)
            yp_s[:, 128 * j:128 * j + 128] = (_dot(sr_ref[:, ls].astype(BF), cr_ref[j])
                                              - _dot(si_ref[:, ls].astype(BF), ci_ref[j]))
        y_ref[...] = _reorder(pt_ref[...], yp_s[...]) + d_ref[...] * u

    p, pt = perm
    return pl.pallas_call(
        body, name="ssm_fwd", grid=(lp // TS,),
        in_specs=[_rows(TS, D), _full((2 * PW_ROWS, SW)), _full((NLC, 128, LC)), _full((NLC, 128, LC)),
                  _full((NLC, LC, 128)), _full((NLC, LC, 128)), _full((1, D)), _full((TS, TS)), _full((TS, TS))],
        out_specs=[_rows(TS, D), _rows(TS, SW), _rows(TS, SW)],
        out_shape=[jax.ShapeDtypeStruct((lp, D), F32), jax.ShapeDtypeStruct((lp, SW), F32),
                   jax.ShapeDtypeStruct((lp, SW), F32)],
        scratch_shapes=[pltpu.VMEM((1, SW), F32), pltpu.VMEM((1, SW), F32), pltpu.VMEM((TS, D), F32)],
        compiler_params=_cp(("arbitrary",)),
    )(u, pw, br, bi, cr, ci, dsk, p, pt)


def ssm_bwd(dy, u, sr, si, pw, br, bi, cr, ci, dsk, perm):
    lp = u.shape[0]
    nt = lp // TS

    def rev(i):
        return (nt - 1 - i, 0)

    def prev8(i):
        return (jnp.maximum((nt - 1 - i) * (TS // 8) - 1, 0), 0)

    def body(dy_ref, u_ref, sr_ref, si_ref, pr8_ref, pi8_ref, pw_ref, br_ref, bi_ref, cr_ref, ci_ref, d_ref,
             p_ref, pt_ref, du_ref, da_ref, dbr_ref, dbi_ref, dcr_ref, dci_ref, dd_ref,
             gr_s, gi_s, car_r, car_i, dup_s):
        i = pl.program_id(0)

        @pl.when(i == 0)
        def _():
            car_r[...] = jnp.zeros_like(car_r)
            car_i[...] = jnp.zeros_like(car_i)
            for ref in (da_ref, dbr_ref, dbi_ref, dcr_ref, dci_ref, dd_ref):
                ref[...] = jnp.zeros_like(ref)

        keep = 1.0 - (i == nt - 1).astype(F32)
        dy = dy_ref[...]
        u = u_ref[...]
        dd_ref[...] += jnp.sum(dy * u, axis=0, keepdims=True)
        pm = p_ref[...]
        dyb = _dot(pm, dy.astype(BF)).astype(BF)
        ub = _dot(pm, u.astype(BF)).astype(BF)
        for j in range(NLC):
            ls = slice(LC * j, LC * j + LC)
            dyj = dyb[:, 128 * j:128 * j + 128]
            gr_s[:, ls] = _dot_nt(dyj, cr_ref[j])
            gi_s[:, ls] = -_dot_nt(dyj, ci_ref[j])
            dcr_ref[j] += _dot_tn(dyj, sr_ref[:, ls].astype(BF))
            dci_ref[j] -= _dot_tn(dyj, si_ref[:, ls].astype(BF))
        rid = lax.broadcasted_iota(jnp.int32, (8, LC), 0)
        for j in range(NLC):
            ls = slice(LC * j, LC * j + LC)

            def accumulate(i, g_r, g_i, acc):
                if i >= 1:
                    rows = pl.ds(8 * (i - 1), 8)
                    p_r, p_i = sr_ref[rows, ls], si_ref[rows, ls]
                else:
                    rows = pl.ds(8 * (SEG - 1), 8)
                    p_r = jnp.where(rid >= 1, pltpu.roll(sr_ref[rows, ls], 1, axis=0), pr8_ref[7:8, ls] * keep)
                    p_i = jnp.where(rid >= 1, pltpu.roll(si_ref[rows, ls], 1, axis=0), pi8_ref[7:8, ls] * keep)
                t_r, t_i = g_r * p_r + g_i * p_i, g_i * p_r - g_r * p_i
                return (t_r, t_i) if acc is None else (acc[0] + t_r, acc[1] + t_i)

            (car_r[:, ls], car_i[:, ls]), (t_r, t_i) = _segment_scan(
                gr_s, gi_s, pw_ref, PW_ROWS, ls, car_r[:, ls], car_i[:, ls], True, accumulate)
            da_ref[0:8, ls] += t_r
            da_ref[8:16, ls] += t_i
        for j in range(NLC):
            ls = slice(LC * j, LC * j + LC)
            uj = ub[:, 128 * j:128 * j + 128]
            grb = gr_s[:, ls].astype(BF)
            gib = gi_s[:, ls].astype(BF)
            dup_s[:, 128 * j:128 * j + 128] = _dot_nt(grb, br_ref[j]) + _dot_nt(gib, bi_ref[j])
            dbr_ref[j] += _dot_tn(uj, grb)
            dbi_ref[j] += _dot_tn(uj, gib)
        du_ref[...] = (_reorder(pt_ref[...], dup_s[...]) + d_ref[...] * dy).astype(BF)

    p, pt = perm
    return pl.pallas_call(
        body, name="ssm_bwd", grid=(nt,),
        in_specs=[pl.BlockSpec((TS, D), rev), pl.BlockSpec((TS, D), rev), pl.BlockSpec((TS, SW), rev),
                  pl.BlockSpec((TS, SW), rev), pl.BlockSpec((8, SW), prev8), pl.BlockSpec((8, SW), prev8),
                  _full((2 * PW_ROWS, SW)), _full((NLC, 128, LC)), _full((NLC, 128, LC)),
                  _full((NLC, LC, 128)), _full((NLC, LC, 128)), _full((1, D)), _full((TS, TS)), _full((TS, TS))],
        out_specs=[pl.BlockSpec((TS, D), rev), _full((16, SW)), _full((NLC, 128, LC)), _full((NLC, 128, LC)),
                   _full((NLC, 128, LC)), _full((NLC, 128, LC)), _full((1, D))],
        out_shape=[jax.ShapeDtypeStruct((lp, D), BF), jax.ShapeDtypeStruct((16, SW), F32),
                   jax.ShapeDtypeStruct((NLC, 128, LC), F32), jax.ShapeDtypeStruct((NLC, 128, LC), F32),
                   jax.ShapeDtypeStruct((NLC, 128, LC), F32), jax.ShapeDtypeStruct((NLC, 128, LC), F32),
                   jax.ShapeDtypeStruct((1, D), F32)],
        scratch_shapes=[pltpu.VMEM((TS, SW), F32), pltpu.VMEM((TS, SW), F32),
                        pltpu.VMEM((1, SW), F32), pltpu.VMEM((1, SW), F32), pltpu.VMEM((TS, D), F32)],
        compiler_params=_cp(("arbitrary",)),
    )(dy, u, sr, si, sr, si, pw, br, bi, cr, ci, dsk, p, pt)


def _merge_parts(attn, zz, gg, abn, sbn):
    za, zb = zz[:, 0:D], zz[:, D:2 * D]
    sgz = _sigmoid(zb)
    ssm = za * sgz
    ra = lax.rsqrt(jnp.mean(attn * attn, axis=-1, keepdims=True) + EPS)
    rs = lax.rsqrt(jnp.mean(ssm * ssm, axis=-1, keepdims=True) + EPS)
    ah, sh = attn * ra, ssm * rs
    sga, sgs = _sigmoid(gg[:, 0:D]), _sigmoid(gg[:, D:2 * D])
    return za, sgz, ra, rs, ah, sh, sga, sgs, ah * abn, sh * sbn


def mid_fwd(y, attn, gg, x, mpad, w_glu, w_out, abn, sbn):
    lp = y.shape[0]

    def body(y_ref, a_ref, gg_ref, x_ref, m_ref, wg_ref, wo_ref, abn_ref, sbn_ref, z_o, zz_o, mg_o, h1_o):
        z, _ = _gelu(y_ref[...])
        zb = z.astype(BF)
        z_o[...] = zb
        zz = _dot(zb, wg_ref[...])
        zz_o[...] = zz.astype(BF)
        _, _, _, _, _, _, sga, sgs, an, sn = _merge_parts(a_ref[...], zz, gg_ref[...], abn_ref[...], sbn_ref[...])
        mb = (sga * an + sgs * sn).astype(BF)
        mg_o[...] = mb
        h1_o[...] = _seq_tile(x_ref, m_ref) + _dot(mb, wo_ref[...])

    return pl.pallas_call(
        body, name="mid_fwd", grid=(lp // TM,),
        in_specs=[_rows(TM, D), _rows(TM, D), _rows(TM, 2 * D)] + _seq_specs() + [_full((D, 2 * D)), _full((D, D)),
                  _full((1, D)), _full((1, D))],
        out_specs=[_rows(TM, D), _rows(TM, 2 * D), _rows(TM, D), _rows(TM, D)],
        out_shape=[jax.ShapeDtypeStruct((lp, D), BF), jax.ShapeDtypeStruct((lp, 2 * D), BF),
                   jax.ShapeDtypeStruct((lp, D), BF), jax.ShapeDtypeStruct((lp, D), F32)],
        compiler_params=_cp(("arbitrary",)),
    )(y, attn, gg, x, mpad, w_glu, w_out, abn, sbn)


def mid_bwd(dh1b, y, attn, gg, zzb, w_glu, w_out, abn, sbn):
    lp = y.shape[0]

    def body(dh_ref, y_ref, a_ref, gg_ref, zz_ref, wg_ref, wo_ref, abn_ref, sbn_ref,
             dy_o, dattn_o, dgg_o, dzz_o, dabn_o, dsbn_o):
        i = pl.program_id(0)

        @pl.when(i == 0)
        def _():
            dabn_o[...] = jnp.zeros_like(dabn_o)
            dsbn_o[...] = jnp.zeros_like(dsbn_o)

        dm = _dot_nt(dh_ref[...], wo_ref[...])
        abn, sbn = abn_ref[...], sbn_ref[...]
        za, sgz, ra, rs, ah, sh, sga, sgs, an, sn = _merge_parts(
            a_ref[...], zz_ref[...].astype(F32), gg_ref[...], abn, sbn)
        da = dm * sga
        dabn_o[...] += jnp.sum(da * ah, axis=0, keepdims=True)
        dah = da * abn
        pa = dah * ah
        dgg_o[:, 0:D] = (pa * (1.0 - sga)).astype(BF)
        dattn_o[...] = (ra * (dah - ah * jnp.mean(pa, axis=-1, keepdims=True))).astype(BF)
        ds = dm * sgs
        dsbn_o[...] += jnp.sum(ds * sh, axis=0, keepdims=True)
        dsh = ds * sbn
        ps = dsh * sh
        dgg_o[:, D:2 * D] = (ps * (1.0 - sgs)).astype(BF)
        dssm = rs * (dsh - sh * jnp.mean(ps, axis=-1, keepdims=True))
        dza = dssm * sgz
        dzz_o[:, 0:D] = dza.astype(BF)
        dzz_o[:, D:2 * D] = (dza * za * (1.0 - sgz)).astype(BF)
        dz = _dot_nt(dzz_o[...], wg_ref[...])
        yv = y_ref[...]
        _, t = _gelu(yv)
        dy_o[...] = dz * _gelu_grad(yv, t)

    return pl.pallas_call(
        body, name="mid_bwd", grid=(lp // TM,),
        in_specs=[_rows(TM, D), _rows(TM, D), _rows(TM, D), _rows(TM, 2 * D), _rows(TM, 2 * D),
                  _full((D, 2 * D)), _full((D, D)), _full((1, D)), _full((1, D))],
        out_specs=[_rows(TM, D), _rows(TM, D), _rows(TM, 2 * D), _rows(TM, 2 * D), _full((1, D)), _full((1, D))],
        out_shape=[jax.ShapeDtypeStruct((lp, D), F32), jax.ShapeDtypeStruct((lp, D), BF),
                   jax.ShapeDtypeStruct((lp, 2 * D), BF), jax.ShapeDtypeStruct((lp, 2 * D), BF),
                   jax.ShapeDtypeStruct((1, D), F32), jax.ShapeDtypeStruct((1, D), F32)],
        compiler_params=_cp(("arbitrary",)),
    )(dh1b, y, attn, gg, zzb, w_glu, w_out, abn, sbn)


def ffn_fwd_bwd(h1, tgt, w_fi, w_fo, gf):
    lp = h1.shape[0]
    tf = 256
    bounds = (0, 1536, DFF)

    def body(h_ref, t_ref, wi_ref, wo_ref, gf_ref, hn_o, dgu_o, act_o, dh2_o, dh1_o, dh1b_o, dgf_o, loss_o,
             gate_s, up_s):
        i = pl.program_id(0)

        @pl.when(i == 0)
        def _():
            dgf_o[...] = jnp.zeros_like(dgf_o)
            loss_o[...] = jnp.zeros_like(loss_o)

        h = h_ref[...]
        r = lax.rsqrt(jnp.mean(h * h, axis=-1, keepdims=True) + EPS)
        xh = h * r
        hb = (xh * gf_ref[...]).astype(BF)
        hn_o[...] = hb
        h2 = h
        for c0, c1 in zip(bounds[:-1], bounds[1:]):
            gate = _dot(hb, wi_ref[:, c0:c1])
            up = _dot(hb, wi_ref[:, DFF + c0:DFF + c1])
            gate_s[:, c0:c1] = gate
            up_s[:, c0:c1] = up
            ab = (gate * _sigmoid(gate) * up).astype(BF)
            act_o[:, c0:c1] = ab
            h2 = h2 + _dot(ab, wo_ref[c0:c1, :])
        real = (i >= PAD // tf).astype(F32)
        e = (h2 - t_ref[...]) * real
        loss_o[...] += 0.5 * jnp.sum(jnp.sum(e * e, axis=-1, keepdims=True), axis=0, keepdims=True) * (1.0 / D)
        dh2 = e * (1.0 / D)
        db = dh2.astype(BF)
        dh2_o[...] = db
        dhn = jnp.zeros((tf, D), F32)
        for c0, c1 in zip(bounds[:-1], bounds[1:]):
            gate = gate_s[:, c0:c1]
            up = up_s[:, c0:c1]
            sg = _sigmoid(gate)
            dact = _dot_nt(db, wo_ref[c0:c1, :])
            dgate = (dact * up * (sg * (1.0 + gate * (1.0 - sg)))).astype(BF)
            dup = (dact * (gate * sg)).astype(BF)
            dgu_o[:, c0:c1] = dgate
            dgu_o[:, DFF + c0:DFF + c1] = dup
            dhn += _dot_nt(dgate, wi_ref[:, c0:c1]) + _dot_nt(dup, wi_ref[:, DFF + c0:DFF + c1])
        dgf_o[...] += jnp.sum(dhn * xh, axis=0, keepdims=True)
        dxh = dhn * gf_ref[...]
        dh1 = dh2 + r * (dxh - xh * jnp.mean(dxh * xh, axis=-1, keepdims=True))
        dh1_o[...] = dh1
        dh1b_o[...] = dh1.astype(BF)

    return pl.pallas_call(
        body, name="ffn_fwd_bwd", grid=(lp // tf,),
        in_specs=[_rows(tf, D), pl.BlockSpec((tf, D), lambda i: (jnp.maximum(i - PAD // tf, 0), 0)),
                  _const((D, 2 * DFF)), _const((DFF, D)), _full((1, D))],
        out_specs=[_rows(tf, D), _rows(tf, 2 * DFF), _rows(tf, DFF), _rows(tf, D), _rows(tf, D), _rows(tf, D),
                   _full((1, D)), _full((1, 1))],
        out_shape=[jax.ShapeDtypeStruct((lp, D), BF), jax.ShapeDtypeStruct((lp, 2 * DFF), BF),
                   jax.ShapeDtypeStruct((lp, DFF), BF), jax.ShapeDtypeStruct((lp, D), BF),
                   jax.ShapeDtypeStruct((lp, D), F32), jax.ShapeDtypeStruct((lp, D), BF),
                   jax.ShapeDtypeStruct((1, D), F32), jax.ShapeDtypeStruct((1, 1), F32)],
        scratch_shapes=[pltpu.VMEM((tf, DFF), F32), pltpu.VMEM((tf, DFF), F32)],
        compiler_params=_cp(("arbitrary",), vmem_mb=58),
    )(h1, tgt, w_fi, w_fo, gf)


def weight_grad(name, a, b, tn, with_bf16=False):
    lp, k = a.shape
    n = b.shape[1]
    tr = next(t for t in (2112, 1056, 768, 512, 384, 256, 128) if lp % t == 0 and t * k * 2 <= (5 << 20))

    def body(a_ref, b_ref, o_ref, *ob_ref):
        @pl.when(pl.program_id(1) == 0)
        def _():
            o_ref[...] = jnp.zeros_like(o_ref)

        o_ref[...] += _dot_tn(a_ref[...], b_ref[...])
        if with_bf16:
            @pl.when(pl.program_id(1) == lp // tr - 1)
            def _():
                ob_ref[0][...] = o_ref[...].astype(BF)

    spec = pl.BlockSpec((k, tn), lambda j, m: (0, j))
    return pl.pallas_call(
        body, name=name, grid=(n // tn, lp // tr),
        in_specs=[pl.BlockSpec((tr, k), lambda j, m: (m, 0)), pl.BlockSpec((tr, tn), lambda j, m: (m, j))],
        out_specs=[spec, spec] if with_bf16 else spec,
        out_shape=([jax.ShapeDtypeStruct((k, n), F32), jax.ShapeDtypeStruct((k, n), BF)] if with_bf16
                   else jax.ShapeDtypeStruct((k, n), F32)),
        compiler_params=_cp(("arbitrary", "arbitrary")),
    )(a, b)


def _adamw_math(w, g, m, v):
    m = B1 * m + (1.0 - B1) * g
    v = B2 * v + (1.0 - B2) * (g * g)
    delta = -LR * ((m / BC1) / (jnp.sqrt(v / BC2) + AEPS) + WD * w)
    return delta, m, v


def _row_tile(r):
    return next((t for t in (256, 128, 64, 32, 16, 8) if r % t == 0), r)


def adamw(name, w, g, m, v):
    r, c = w.shape
    tr = _row_tile(r)

    def body(w_ref, g_ref, m_ref, v_ref, g_o, d_o, m_o, v_o):
        g = g_ref[...]
        g_o[...] = g
        d_o[...], m_o[...], v_o[...] = _adamw_math(w_ref[...], g, m_ref[...], v_ref[...])

    spec = pl.BlockSpec((tr, c), lambda i: (i, 0))
    return pl.pallas_call(
        body, name=name, grid=(r // tr,), in_specs=[spec] * 4, out_specs=[spec] * 4,
        out_shape=[jax.ShapeDtypeStruct((r, c), F32)] * 4,
        compiler_params=_cp(("arbitrary",)),
    )(w, g, m, v)


def adamw_small(ws, gs, ms, vs):
    n = len(ws)
    steps = 8

    def spec(a):
        if a.ndim == 4:
            return pl.BlockSpec((1, a.shape[1] // steps) + a.shape[2:], lambda i: (0, i, 0, 0))
        nd = a.ndim
        return pl.BlockSpec(a.shape, lambda i: (0,) * nd)

    def body(*refs):
        w, g, m, v, d_o, m_o, v_o = (refs[k * n:(k + 1) * n] for k in range(7))
        for t in range(n):
            d_o[t][...], m_o[t][...], v_o[t][...] = _adamw_math(w[t][...], g[t][...], m[t][...], v[t][...])

    specs = [spec(a) for a in ws]
    res = pl.pallas_call(
        body, name="adamw_small", grid=(steps,), in_specs=specs * 4, out_specs=specs * 3,
        out_shape=[jax.ShapeDtypeStruct(a.shape, F32) for a in ws] * 3,
        compiler_params=_cp(("arbitrary",)),
    )(*ws, *gs, *ms, *vs)
    return res[:n], res[n:2 * n], res[2 * n:]


def pair_sum(name, place, full, got, ax):
    r, c = got.shape
    tr = _row_tile(r)

    def body(s_ref, a_ref, b_ref, o_ref):
        o_ref[...] = (a_ref[...] + b_ref[...]).astype(BF)

    if ax == 1:
        mine = pl.BlockSpec((tr, c), lambda i, s: (s[1] * (r // tr) + i, 0))
    else:
        mine = pl.BlockSpec((tr, c), lambda i, s: (i, s[1]))
    spec = pl.BlockSpec((tr, c), lambda i, s: (i, 0))
    return pl.pallas_call(
        body, name=name,
        grid_spec=pltpu.PrefetchScalarGridSpec(num_scalar_prefetch=1, grid=(r // tr,), in_specs=[mine, spec],
                                               out_specs=spec),
        out_shape=jax.ShapeDtypeStruct((r, c), BF),
        compiler_params=_cp(("arbitrary",)),
    )(place, full, got)


def sum4(name, place, parts, half, ax):
    _, r, c = parts.shape
    tr = _row_tile(r)

    def body(s_ref, p_ref, h_ref, o_ref):
        o_ref[...] = ((p_ref[0].astype(F32) + p_ref[1].astype(F32))
                      + (p_ref[2].astype(F32) + h_ref[...].astype(F32)))

    if ax == 1:
        own = pl.BlockSpec((tr, c), lambda i, s: (i, s[0]))
        out = pl.BlockSpec((tr, c), lambda i, s: (s[1] * (r // tr) + i, 0))
        shape = (2 * r, c)
    else:
        own = pl.BlockSpec((tr, c), lambda i, s: (s[0] * (r // tr) + i, 0))
        out = pl.BlockSpec((tr, c), lambda i, s: (i, s[1]))
        shape = (r, 2 * c)
    return pl.pallas_call(
        body, name=name,
        grid_spec=pltpu.PrefetchScalarGridSpec(
            num_scalar_prefetch=1, grid=(r // tr,),
            in_specs=[pl.BlockSpec((3, tr, c), lambda i, s: (0, i, 0)), own], out_specs=out),
        out_shape=jax.ShapeDtypeStruct(shape, F32),
        compiler_params=_cp(("arbitrary",)),
    )(place, parts, half)


def sum8(name, place, parts, full, ax):
    _, r, c = parts.shape
    tr = _row_tile(r)

    def body(s_ref, p_ref, g_ref, o_ref):
        p = [p_ref[k].astype(F32) for k in range(7)]
        o_ref[...] = ((p[0] + p[1]) + (p[2] + p[3])) + ((p[4] + p[5]) + (p[6] + g_ref[...]))

    if ax == 1:
        own = pl.BlockSpec((tr, c), lambda i, s: (s[1] * (r // tr) + i, s[0]))
        out = pl.BlockSpec((tr, c), lambda i, s: (s[1] * (r // tr) + i, 0))
        shape = (2 * r, c)
    else:
        own = pl.BlockSpec((tr, c), lambda i, s: (s[0] * (r // tr) + i, s[1]))
        out = pl.BlockSpec((tr, c), lambda i, s: (i, s[1]))
        shape = (r, 2 * c)
    return pl.pallas_call(
        body, name=name,
        grid_spec=pltpu.PrefetchScalarGridSpec(
            num_scalar_prefetch=1, grid=(r // tr,),
            in_specs=[pl.BlockSpec((7, tr, c), lambda i, s: (0, i, 0)), own], out_specs=out),
        out_shape=jax.ShapeDtypeStruct(shape, F32),
        compiler_params=_cp(("arbitrary",)),
    )(place, parts, full)


def _place():
    x, y, c = lax.axis_index("x"), lax.axis_index("y"), lax.axis_index("c")
    return x, y, c


def _other_chips(x, y):
    return [(1 - x, y), (x, 1 - y), (1 - x, 1 - y)]


SHARDED = (("w_in", D, IN_COLS, 1), ("w_glu", D, 2 * D, 1), ("w_out", D, D, 0),
           ("w_ffn_in", D, 2 * DFF, 1), ("w_ffn_out", DFF, D, 0))


def _shard_of(ref, axis, k, size):
    if axis == 0:
        return ref.at[pl.ds(k * size, size), :]
    return ref.at[:, pl.ds(k * size, size)]


def _gather_piece(full, dims, k, h):
    r, cc, ax = dims
    if ax == 0:
        return full.at[pl.ds(k * (r // 4) + h * (r // 8), r // 8), :]
    return full.at[pl.ds(h * (r // 2), r // 2), pl.ds(k * (cc // 4), cc // 4)]


def _rows_half(shard, h):
    rs = shard.shape[0]
    return shard.at[pl.ds(h * (rs // 2), rs // 2), :]


SEM = pl.BlockSpec(memory_space=pltpu.SEMAPHORE)
HBM = pl.BlockSpec(memory_space=pltpu.HBM)
EFFECT = pltpu.SideEffectType.DATAFLOW_SIDE_EFFECTING


def _in_hbm(a):
    return pltpu.with_memory_space_constraint(a, pltpu.HBM)


def _split_copy_start(name, copies, ncopies, srcs, lands, after):
    ns, nl = len(srcs), len(lands)

    def body(*refs):
        ins, land_refs = refs[:ns], refs[ns:ns + nl]
        send_sems, recv_sems = refs[ns + nl + 1], refs[ns + nl + 2]
        token = refs[-1]
        for k, (src, dst, dev) in enumerate(copies(ins, land_refs)):
            pltpu.make_async_remote_copy(src_ref=src, dst_ref=dst, send_sem=send_sems.at[k],
                                         recv_sem=recv_sems.at[k], device_id=dev, device_id_type=MESH).start()
        token[...] = jnp.zeros_like(token)

    res = pl.pallas_call(
        body, name=name,
        in_specs=[HBM] * (ns + nl) + [ANY],
        out_specs=(SEM, SEM) + (HBM,) * (ns + nl) + (pl.BlockSpec(memory_space=pltpu.VMEM),),
        out_shape=(pltpu.SemaphoreType.DMA((ncopies,)), pltpu.SemaphoreType.DMA((ncopies,)))
        + tuple(pltpu.HBM(a.shape, a.dtype) for a in list(srcs) + list(lands))
        + (jax.ShapeDtypeStruct((8, 128), F32),),
        input_output_aliases={t: 2 + t for t in range(ns + nl)},
        compiler_params=pltpu.CompilerParams(has_side_effects=EFFECT),
    )(*[_in_hbm(a) for a in srcs], *[_in_hbm(a) for a in lands], after)
    return res[0], res[1], list(res[2:2 + ns]), list(res[2 + ns:2 + ns + nl]), res[-1]


def _split_copy_wait(name, copies, arrivals, send_sems, recv_sems, srcs, lands, after):
    ns, nl = len(srcs), len(lands)

    def body(*refs):
        ins, land_refs = refs[:ns], refs[ns:ns + nl]
        send_sems_ref, recv_sems_ref = refs[ns + nl], refs[ns + nl + 1]
        mine = copies(ins, land_refs)
        for k, ((src, dst, dev), got) in enumerate(zip(mine, arrivals(ins, land_refs))):
            pltpu.make_async_remote_copy(src_ref=src, dst_ref=dst, send_sem=send_sems_ref.at[k],
                                         recv_sem=recv_sems_ref.at[k], device_id=dev,
                                         device_id_type=MESH).wait_send()
            pltpu.make_async_remote_copy(src_ref=got, dst_ref=got, send_sem=send_sems_ref.at[k],
                                         recv_sem=recv_sems_ref.at[k], device_id=dev,
                                         device_id_type=MESH).wait_recv()

    res = pl.pallas_call(
        body, name=name,
        in_specs=[HBM] * (ns + nl) + [SEM, SEM, ANY],
        out_specs=(HBM,) * (ns + nl),
        out_shape=tuple(pltpu.HBM(a.shape, a.dtype) for a in list(srcs) + list(lands)),
        input_output_aliases={t: t for t in range(ns + nl)},
        compiler_params=pltpu.CompilerParams(has_side_effects=EFFECT),
    )(*srcs, *lands, send_sems, recv_sems, after)
    return list(res[:ns]), list(res[ns:])


def _gather_copies(dims):
    def copies(ins, lands):
        x, y, c = _place()
        mine = 2 * x + y
        out = []
        for t, d in enumerate(dims):
            size = (d[0] if d[2] == 0 else d[1]) // 4
            for px, py in _other_chips(x, y):
                for dc in range(2):
                    out.append((_rows_half(ins[t], c), _gather_piece(lands[t], d, mine, c), (px, py, c ^ dc)))
            out.append((ins[t], _shard_of(lands[t], d[2], mine, size), (x, y, 1 - c)))
        return out

    def arrivals(ins, lands):
        x, y, c = _place()
        mine = 2 * x + y
        out = []
        for t, d in enumerate(dims):
            size = (d[0] if d[2] == 0 else d[1]) // 4
            for px, py in _other_chips(x, y):
                for dc in range(2):
                    out.append(_gather_piece(lands[t], d, 2 * px + py, c ^ dc))
            out.append(_shard_of(lands[t], d[2], mine, size))
        return out

    return copies, arrivals


def _half_of(ref, ax, h):
    r, c = ref.shape
    if ax == 1:
        return ref.at[pl.ds(h * (r // 2), r // 2), :]
    return ref.at[:, pl.ds(h * (c // 2), c // 2)]


def _half_shape(r, c, ax):
    return (r // 2, c) if ax == 1 else (r, c // 2)


def pair_exchange(name, grads, idxs):
    n = len(grads)
    which = [SHARDED[i] for i in idxs]

    def body(*refs):
        ins, got = refs[:n], refs[n:2 * n]
        send_sems, recv_sems = refs[2 * n:]
        x, y, c = _place()
        cps = []
        for t, (_, _, _, ax) in enumerate(which):
            cp = pltpu.make_async_remote_copy(
                src_ref=_half_of(ins[t], ax, 1 - c), dst_ref=got[t], send_sem=send_sems.at[t],
                recv_sem=recv_sems.at[t], device_id=(x, y, 1 - c), device_id_type=MESH)
            cp.start()
            cps.append(cp)
        for cp in cps:
            cp.wait()

    return pl.pallas_call(
        body, name=name,
        in_specs=[ANY] * n, out_specs=[ANY] * n,
        out_shape=[jax.ShapeDtypeStruct(_half_shape(r, c, ax), F32) for _, r, c, ax in which],
        scratch_shapes=[pltpu.SemaphoreType.DMA((n,)), pltpu.SemaphoreType.DMA((n,))],
        compiler_params=pltpu.CompilerParams(has_side_effects=True),
    )(*grads)


def _piece_shapes(idxs):
    out = []
    for _, r, c, ax in [SHARDED[i] for i in idxs]:
        hr, hc = _half_shape(r, c, ax)
        out.append((hr // 4, hc) if ax == 0 else (hr, hc // 4))
    return out


def _scatter3_copies(idxs):
    which = [SHARDED[i] for i in idxs]

    def copies(ins, lands):
        x, y, c = _place()
        out = []
        for t, (_, _, _, ax) in enumerate(which):
            size = ins[t].shape[ax] // 4
            for j, (px, py) in enumerate(_other_chips(x, y)):
                out.append((_shard_of(ins[t], ax, 2 * px + py, size), lands[t].at[j], (px, py, c)))
        return out

    def arrivals(ins, lands):
        return [lands[t].at[j] for t in range(len(which)) for j in range(3)]

    return copies, arrivals


def _piece_of(full, ax, k, h):
    r, c = full.shape
    if ax == 1:
        return full.at[pl.ds(h * (r // 2), r // 2), pl.ds(k * (c // 4), c // 4)]
    return full.at[pl.ds(k * (r // 4), r // 4), pl.ds(h * (c // 2), c // 2)]


def _scatter8_copies(idxs):
    which = [SHARDED[i] for i in idxs]

    def copies(ins, lands):
        x, y, c = _place()
        out = []
        for t, (_, _, _, ax) in enumerate(which):
            for j, (px, py) in enumerate(_other_chips(x, y)):
                for d in range(2):
                    out.append((_piece_of(ins[t], ax, 2 * px + py, c ^ d), lands[t].at[2 * j + d], (px, py, c ^ d)))
            out.append((_piece_of(ins[t], ax, 2 * x + y, 1 - c), lands[t].at[6], (x, y, 1 - c)))
        return out

    def arrivals(ins, lands):
        return [lands[t].at[k] for t in range(len(which)) for k in range(7)]

    return copies, arrivals


def scatter_grads(halves, idxs):
    n = len(halves)
    which = [SHARDED[i] for i in idxs]

    def shapes():
        return _piece_shapes(idxs)

    def body(*refs):
        ins, outs = refs[:n], refs[n:2 * n]
        send_sems, recv_sems = refs[2 * n:]
        x, y, c = _place()
        started = []
        for t, (src, dst, (_, _, _, ax)) in enumerate(zip(ins, outs, which)):
            size = src.shape[ax] // 4
            for j, (px, py) in enumerate(_other_chips(x, y)):
                cp = pltpu.make_async_remote_copy(
                    src_ref=_shard_of(src, ax, 2 * px + py, size), dst_ref=dst.at[j],
                    send_sem=send_sems.at[3 * t + j], recv_sem=recv_sems.at[3 * t + j],
                    device_id=(px, py, c), device_id_type=MESH)
                cp.start()
                started.append(cp)
        for cp in started:
            cp.wait()

    return pl.pallas_call(
        body, name="scatter_grads",
        in_specs=[ANY] * n, out_specs=[ANY] * n,
        out_shape=[jax.ShapeDtypeStruct((3,) + s, BF) for s in shapes()],
        scratch_shapes=[pltpu.SemaphoreType.DMA((3 * n,)), pltpu.SemaphoreType.DMA((3 * n,))],
        compiler_params=pltpu.CompilerParams(has_side_effects=True),
    )(*halves)


def join_halves(name, shards, idxs):
    n = len(shards)
    which = [SHARDED[i] for i in idxs]

    def body(*refs):
        ins, outs = refs[:n], refs[n:2 * n]
        send_sems, recv_sems = refs[2 * n:]
        x, y, c = _place()
        cps = []
        for t, (_, _, _, ax) in enumerate(which):
            cp = pltpu.make_async_remote_copy(
                src_ref=_half_of(ins[t], ax, c), dst_ref=_half_of(outs[t], ax, c), send_sem=send_sems.at[t],
                recv_sem=recv_sems.at[t], device_id=(x, y, 1 - c), device_id_type=MESH)
            cp.start()
            cps.append(cp)
        for t, cp in enumerate(cps):
            cp.wait_send()
            theirs = _half_of(outs[t], which[t][3], 1 - c)
            pltpu.make_async_remote_copy(
                src_ref=theirs, dst_ref=theirs, send_sem=send_sems.at[t], recv_sem=recv_sems.at[t],
                device_id=(x, y, 1 - c), device_id_type=MESH).wait_recv()

    return pl.pallas_call(
        body, name=name,
        in_specs=[ANY] * n, out_specs=[ANY] * n,
        out_shape=[jax.ShapeDtypeStruct(s.shape, F32) for s in shards],
        input_output_aliases={t: t for t in range(n)},
        scratch_shapes=[pltpu.SemaphoreType.DMA((n,)), pltpu.SemaphoreType.DMA((n,))],
        compiler_params=pltpu.CompilerParams(has_side_effects=True),
    )(*shards)


def allreduce_small(pack):
    rows = pack.shape[0]
    hr = rows // 2

    def body(p_ref, o_ref, sib_ref, parts_ref, send_sems, recv_sems):
        x, y, c = _place()
        mine = 2 * x + y
        sibling = (x, y, 1 - c)
        swap = pltpu.make_async_remote_copy(
            src_ref=p_ref, dst_ref=sib_ref, send_sem=send_sems.at[0], recv_sem=recv_sems.at[0],
            device_id=sibling, device_id_type=MESH)
        swap.start()
        swap.wait()
        half = pl.ds(pl.multiple_of(c * hr, 8), hr)
        parts_ref[mine] = p_ref[half, :] + sib_ref[half, :]
        cps = []
        for j, chip in enumerate(_other_chips(x, y)):
            cp = pltpu.make_async_remote_copy(
                src_ref=parts_ref.at[mine], dst_ref=parts_ref.at[mine], send_sem=send_sems.at[1 + j],
                recv_sem=recv_sems.at[1 + j], device_id=(*chip, c), device_id_type=MESH)
            cp.start()
            cps.append(cp)
        for j, (px, py) in enumerate(_other_chips(x, y)):
            cps[j].wait_send()
            theirs = parts_ref.at[2 * px + py]
            pltpu.make_async_remote_copy(
                src_ref=theirs, dst_ref=theirs, send_sem=send_sems.at[1 + j], recv_sem=recv_sems.at[1 + j],
                device_id=(px, py, c), device_id_type=MESH).wait_recv()
        o_ref[half, :] = (parts_ref[0] + parts_ref[1]) + (parts_ref[2] + parts_ref[3])
        back = pltpu.make_async_remote_copy(
            src_ref=o_ref.at[half, :], dst_ref=o_ref.at[half, :], send_sem=send_sems.at[4],
            recv_sem=recv_sems.at[4], device_id=sibling, device_id_type=MESH)
        back.start()
        back.wait_send()
        other = pl.ds(pl.multiple_of((1 - c) * hr, 8), hr)
        pltpu.make_async_remote_copy(
            src_ref=o_ref.at[other, :], dst_ref=o_ref.at[other, :], send_sem=send_sems.at[4],
            recv_sem=recv_sems.at[4], device_id=sibling, device_id_type=MESH).wait_recv()

    return pl.pallas_call(
        body, name="allreduce_small",
        in_specs=[pl.BlockSpec(memory_space=pltpu.VMEM)], out_specs=pl.BlockSpec(memory_space=pltpu.VMEM),
        out_shape=jax.ShapeDtypeStruct((rows, 128), F32),
        scratch_shapes=[pltpu.VMEM((rows, 128), F32), pltpu.VMEM((4, hr, 128), F32),
                        pltpu.SemaphoreType.DMA((5,)), pltpu.SemaphoreType.DMA((5,))],
        compiler_params=pltpu.CompilerParams(has_side_effects=True, vmem_limit_bytes=40 << 20),
    )(pack)


def _ssm_discretize(lam_re, lam_im, log_dt, b_re, b_im):
    dt = jnp.exp(log_dt)[:, None]
    mag = jnp.exp(lam_re * dt)
    ar, ai = mag * jnp.cos(lam_im * dt), mag * jnp.sin(lam_im * dt)
    den = lam_re * lam_re + lam_im * lam_im
    nr, ni = ar - 1.0, ai
    fr, fi = (nr * lam_re + ni * lam_im) / den, (ni * lam_re - nr * lam_im) / den
    bbr = fr[:, None, :] * b_re - fi[:, None, :] * b_im
    bbi = fr[:, None, :] * b_im + fi[:, None, :] * b_re
    return ar, ai, bbr, bbi


def _cmul(a, b):
    return a[0] * b[0] - a[1] * b[1], a[0] * b[1] + a[1] * b[0]


def _scan_powers(ar, ai):
    a = (ar.reshape(1, SW), ai.reshape(1, SW))
    big = a
    for _ in range(SEG.bit_length() - 1):
        big = _cmul(big, big)
    assert 1 << (SEG.bit_length() - 1) == SEG
    pows = {1: big}
    pows[2] = _cmul(big, big)
    pows[4] = _cmul(pows[2], pows[2])
    rid = jnp.arange(8)[:, None]
    ones = jnp.ones((8, 1), F32)
    fwd, rev = [ones * a[0], ones * a[1]], [ones * a[0], -ones * a[1]]
    for k in (1, 2, 4):
        pr, pi = pows[k]
        fwd += [jnp.where(rid >= k, pr, 0.0), jnp.where(rid >= k, pi, 0.0)]
        rev += [jnp.where(rid < 8 - k, pr, 0.0), jnp.where(rid < 8 - k, -pi, 0.0)]
    return jnp.concatenate(fwd + rev, axis=0)


def _pack_b(bb):
    t = bb.reshape(NLC, 8, GCH, NSTATE)
    return jnp.einsum("jgcp,gh->jgchp", t, jnp.eye(8, dtype=bb.dtype)).reshape(NLC, 128, LC)


def _unpack_b(db):
    t = jnp.einsum("jgchp,gh->jgcp", db.reshape(NLC, 8, GCH, 8, NSTATE), jnp.eye(8, dtype=db.dtype))
    return t.reshape(GROUPS, GCH, NSTATE)


def _pack_c(cc):
    t = cc.reshape(NLC, 8, GCH, NSTATE).transpose(0, 1, 3, 2)
    return jnp.einsum("jgpc,gh->jgphc", t, jnp.eye(8, dtype=cc.dtype)).reshape(NLC, LC, 128)


def _unpack_c(dc):
    t = jnp.einsum("jgphc,gh->jgpc", dc.reshape(NLC, 8, NSTATE, 8, GCH), jnp.eye(8, dtype=dc.dtype))
    return t.transpose(0, 1, 3, 2).reshape(GROUPS, GCH, NSTATE)


SMALL = ("norm_mix", "q_norm", "k_norm", "attn_sinks", "lam_re", "lam_im", "log_dt", "ssm_b_re", "ssm_b_im",
         "ssm_c_re", "ssm_c_im", "ssm_d", "attn_branch_norm", "ssm_branch_norm", "norm_ffn")


def _pack_small(arrs, rows=None):
    flat = jnp.concatenate([a.reshape(-1).astype(F32) for a in arrs])
    n = flat.shape[0]
    if rows is None:
        rows = -(-n // (128 * 256)) * 256
    return jnp.pad(flat, (0, rows * 128 - n)).reshape(rows, 128)


def _unpack_small(pack, like):
    flat = pack.reshape(-1)
    out, off = [], 0
    for a in like:
        out.append(flat[off:off + a.size].reshape(a.shape))
        off += a.size
    return out


def _gather_half_copies(dims):
    def copies(ins, lands):
        x, y, c = _place()
        mine = 2 * x + y
        out = []
        for t, d in enumerate(dims):
            size = (d[0] if d[2] == 0 else d[1]) // 4
            for chip in _other_chips(x, y):
                out.append((_rows_half(ins[t], c), _gather_piece(lands[t], d, mine, c), (*chip, c)))
            out.append((ins[t], _shard_of(lands[t], d[2], mine, size), (x, y, 1 - c)))
        return out

    def arrivals(ins, lands):
        x, y, c = _place()
        mine = 2 * x + y
        out = []
        for t, d in enumerate(dims):
            size = (d[0] if d[2] == 0 else d[1]) // 4
            for px, py in _other_chips(x, y):
                out.append(_gather_piece(lands[t], d, 2 * px + py, c))
            out.append(_shard_of(lands[t], d[2], mine, size))
        return out

    return copies, arrivals


def gather_forward(fulls, dims):
    n = len(fulls)

    def body(*refs):
        ins, outs = refs[:n], refs[n:2 * n]
        send_sems, recv_sems = refs[2 * n:]
        x, y, c = _place()
        cps = []
        for t in range(n):
            for j, (px, py) in enumerate(_other_chips(x, y)):
                cp = pltpu.make_async_remote_copy(
                    src_ref=_gather_piece(ins[t], dims[t], 2 * px + py, c),
                    dst_ref=_gather_piece(outs[t], dims[t], 2 * px + py, c),
                    send_sem=send_sems.at[3 * t + j], recv_sem=recv_sems.at[3 * t + j],
                    device_id=(x, y, 1 - c), device_id_type=MESH)
                cp.start()
                cps.append(cp)
        for t in range(n):
            for j, (px, py) in enumerate(_other_chips(x, y)):
                cps[3 * t + j].wait_send()
                other = _gather_piece(outs[t], dims[t], 2 * px + py, 1 - c)
                pltpu.make_async_remote_copy(
                    src_ref=other, dst_ref=other, send_sem=send_sems.at[3 * t + j], recv_sem=recv_sems.at[3 * t + j],
                    device_id=(x, y, 1 - c), device_id_type=MESH).wait_recv()

    return pl.pallas_call(
        body, name="gather_forward",
        in_specs=[ANY] * n, out_specs=[ANY] * n,
        out_shape=[jax.ShapeDtypeStruct(a.shape, a.dtype) for a in fulls],
        input_output_aliases={t: t for t in range(n)},
        scratch_shapes=[pltpu.SemaphoreType.DMA((3 * n,)), pltpu.SemaphoreType.DMA((3 * n,))],
        compiler_params=pltpu.CompilerParams(has_side_effects=True),
    )(*fulls)


def _tie(a, token):
    return a if token is None else a + token[0, 0]


class _FirstGather:
    dims = [SHARDED[0][1:], (N_META, D, 1)]

    def __init__(self, w_in_shard, meta_shard):
        self.copies, self.arrivals = _gather_half_copies(self.dims)
        lands = [lax.empty((D, IN_COLS), BF), lax.empty((N_META, D), F32)]
        self.send, self.recv, self.srcs, self.lands, self.token = _split_copy_start(
            "gather_start_first", self.copies, 8, [w_in_shard, meta_shard], lands, w_in_shard)

    def finish(self, after):
        _, lands = _split_copy_wait("gather_wait_first", self.copies, self.arrivals, self.send, self.recv, self.srcs,
                                    self.lands, after)
        return gather_forward(lands, self.dims)


class _Exchange:
    def __init__(self, rest_shards, place):
        self.rest_shards = rest_shards
        self.place = place
        self.rest_dims = [s[1:] for s in SHARDED[1:]]
        self.pending = {}
        self.halves = {}
        self.full = {}
        self.parts = {}

    def begin(self, after):
        copies, arrivals = _gather_copies(self.rest_dims)
        lands = [lax.empty((r, c), BF) for r, c, _ in self.rest_dims]
        send, recv, srcs, lands, token = _split_copy_start(
            "gather_start", copies, 7 * len(lands), self.rest_shards, lands, after)
        self.pending["gather"] = (copies, arrivals, send, recv, srcs, lands)
        return token

    def rest_weights(self, after):
        copies, arrivals, send, recv, srcs, lands = self.pending.pop("gather")
        _, fulls = _split_copy_wait("gather_wait", copies, arrivals, send, recv, srcs, lands, after)
        return dict(zip([s[0] for s in SHARDED[1:]], fulls))

    def _halves(self, group, idxs, grads):
        got = pair_exchange("pair_exchange_" + group, grads, idxs)
        return [pair_sum("pair_sum_" + SHARDED[i][0], self.place, g, b, SHARDED[i][3])
                for i, g, b in zip(idxs, grads, got)]

    def start(self, group, idxs, grads):
        if grads[0][1] is None:
            halves = self._halves(group, idxs, [g for g, _ in grads])
            copies, arrivals = _scatter3_copies(idxs)
            lands = [lax.empty((3,) + sh, BF) for sh in _piece_shapes(idxs)]
            send, recv, srcs, lands, token = _split_copy_start(
                "scatter_start_" + group, copies, 3 * len(idxs), halves, lands, grads[0][0])
            self.pending[group] = (copies, arrivals, send, recv, srcs, lands, idxs)
            return token
        copies, arrivals = _scatter8_copies(idxs)
        lands = [lax.empty((7,) + sh, BF) for sh in _piece_shapes(idxs)]
        send, recv, srcs, lands, token = _split_copy_start(
            "scatter_start_" + group, copies, 7 * len(idxs), [g16 for _, g16 in grads], lands, grads[0][0])
        self.pending[group] = (copies, arrivals, send, recv, srcs, lands, idxs)
        for i, (g32, _) in zip(idxs, grads):
            self.full[i] = g32
        return token

    def finish(self, group, after):
        if group not in self.pending:
            return
        copies, arrivals, send, recv, srcs, lands, idxs = self.pending.pop(group)
        srcs, lands = _split_copy_wait("scatter_wait_" + group, copies, arrivals, send, recv, srcs, lands, after)
        for i, h, pt in zip(idxs, srcs, lands):
            self.parts[i] = pt
            if i not in self.full:
                self.halves[i] = h


def local_step(x, tgt, p, first, ex):
    qn_t = jnp.tile(p["q_norm"].reshape(1, HEAD), (1, NQ)) * HEAD ** -0.5
    kn_t = jnp.tile(p["k_norm"].reshape(1, HEAD), (1, NKV))
    seg = jnp.arange(256) // HEAD
    bd = (seg[:, None] == seg[None, :]).astype(BF)
    nm = p["norm_mix"].reshape(1, D)
    sinks = p["attn_sinks"].reshape(NQ)
    dsk = p["ssm_d"].reshape(1, D)
    abn, sbn, gf = (p[k].reshape(1, D) for k in ("attn_branch_norm", "ssm_branch_norm", "norm_ffn"))

    disc_in = (p["lam_re"], p["lam_im"], _tie(p["log_dt"], first.token), p["ssm_b_re"], p["ssm_b_im"])
    (ar, ai, bbr, bbi), disc_vjp = jax.vjp(_ssm_discretize, *disc_in)
    pw = _scan_powers(ar, ai)
    brp, bip = _pack_b(bbr.astype(BF)), _pack_b(bbi.astype(BF))
    crp = _pack_c(_tie(p["ssm_c_re"], first.token).astype(BF))
    cip = _pack_c(_tie(p["ssm_c_im"], first.token).astype(BF))

    w_in, meta_full = first.finish(brp)
    mpad = jnp.concatenate([jnp.zeros((META0, D), F32), meta_full], axis=0)
    token = ex.begin(w_in)
    xn, qr, kr, qn, kn, v, u, gg = in_proj_fwd(x, mpad, w_in, _tie(nm, token), qn_t, kn_t, bd)
    attn = attention_fwd(sinks, qn, kn, v)
    perm = _segment_order()
    y, sr, si = ssm_fwd(u, pw, brp, bip, crp, cip, dsk, perm)
    w = ex.rest_weights(y)
    zb, zzb, mgb, h1 = mid_fwd(y, attn, gg, x, mpad, w["w_glu"], w["w_out"], abn, sbn)
    hnb, dgub, actb, dh2b, dh1, dh1b, dgf, loss = ffn_fwd_bwd(h1, tgt, w["w_ffn_in"], w["w_ffn_out"], gf)
    token = ex.start("ffn", (3, 4), [weight_grad("grad_w_ffn_in", hnb, dgub, 1408, True),
                                     weight_grad("grad_w_ffn_out", actb, dh2b, 512, True)])
    dy, dattn, dggb, dzzb, dabn, dsbn = mid_bwd(dh1b, y, attn, gg, zzb, w["w_glu"], w["w_out"], _tie(abn, token),
                                                sbn)
    grads_mid = [weight_grad("grad_w_glu", zb, dzzb, 1024, True), weight_grad("grad_w_out", mgb, dh1b, 1024, True)]
    dub, da, dbr, dbi, dcr, dci, dd = ssm_bwd(dy, u, sr, si, pw, brp, bip, crp, cip, dsk, perm)
    ex.finish("ffn", dub)
    token = ex.start("mid", (1, 2), grads_mid)
    dqn, dkn, dv, dsink = attention_bwd(_tie(sinks, token), qn, kn, v, dattn)
    dproj, dx, dmpad, dnm, dqw, dkw = in_proj_bwd(dqn, dkn, dv, dub, dggb, qr, kr, x, mpad, dh1, w_in, nm, qn_t,
                                                  kn_t, bd)
    ex.finish("mid", dproj)
    ex.start("in", (0,), [(weight_grad("grad_w_in", xn, dproj, 1536), None)])

    dar = jnp.sum(da[0:8], axis=0).reshape(GROUPS, NSTATE)
    dai = jnp.sum(da[8:16], axis=0).reshape(GROUPS, NSTATE)
    dlr, dli, dldt, dbre, dbim = disc_vjp((dar, dai, _unpack_b(dbr), _unpack_b(dbi)))
    small = {
        "norm_mix": dnm, "q_norm": dqw.reshape(NQ, HEAD).sum(0) * HEAD ** -0.5, "k_norm": dkw.reshape(NKV, HEAD).sum(0),
        "attn_sinks": dsink[0, 0:NQ], "lam_re": dlr, "lam_im": dli, "log_dt": dldt,
        "ssm_b_re": dbre, "ssm_b_im": dbim, "ssm_c_re": _unpack_c(jnp.swapaxes(dcr, 1, 2)), "ssm_c_im": _unpack_c(jnp.swapaxes(dci, 1, 2)),
        "ssm_d": dd, "attn_branch_norm": dabn, "ssm_branch_norm": dsbn, "norm_ffn": dgf,
    }
    return loss[0, 0], dx, dmpad[META0:PAD], small


def kernel(x, meta_tokens, norm_mix, w_in, q_norm, k_norm, attn_sinks, lam_re, lam_im, log_dt, ssm_b_re, ssm_b_im, ssm_c_re, ssm_c_im, ssm_d, w_glu, attn_branch_norm, ssm_branch_norm, w_out, norm_ffn, w_ffn_in, w_ffn_out, loss_target, m_meta_tokens, m_norm_mix, m_w_in, m_q_norm, m_k_norm, m_attn_sinks, m_lam_re, m_lam_im, m_log_dt, m_ssm_b_re, m_ssm_b_im, m_ssm_c_re, m_ssm_c_im, m_ssm_d, m_w_glu, m_attn_branch_norm, m_ssm_branch_norm, m_w_out, m_norm_ffn, m_w_ffn_in, m_w_ffn_out, v_meta_tokens, v_norm_mix, v_w_in, v_q_norm, v_k_norm, v_attn_sinks, v_lam_re, v_lam_im, v_log_dt, v_ssm_b_re, v_ssm_b_im, v_ssm_c_re, v_ssm_c_im, v_ssm_d, v_w_glu, v_attn_branch_norm, v_ssm_branch_norm, v_w_out, v_norm_ffn, v_w_ffn_in, v_w_ffn_out):
    args = dict(locals())
    names = ["meta_tokens", "norm_mix", "w_in", "q_norm", "k_norm", "attn_sinks", "lam_re", "lam_im", "log_dt",
             "ssm_b_re", "ssm_b_im", "ssm_c_re", "ssm_c_im", "ssm_d", "w_glu", "attn_branch_norm",
             "ssm_branch_norm", "w_out", "norm_ffn", "w_ffn_in", "w_ffn_out"]
    big_names = [s[0] for s in SHARDED]
    xs, ys, cs = _place()
    chip = 2 * xs + ys

    first = _FirstGather(args["w_in"][0].astype(BF), meta_tokens)
    shards = [None] + [_tie(args[n][0], first.token).astype(BF) for n in big_names[1:]]
    place = jnp.stack([chip, cs]).astype(jnp.int32)
    ex = _Exchange(shards[1:], place)

    swapped = ("ssm_b_re", "ssm_b_im")

    def natural(n, a):
        return jnp.swapaxes(a, -1, -2) if n in swapped else a

    p = {n: natural(n, args[n][0]) for n in SMALL}
    loss, dx, dmeta, gsmall = local_step(x[0], loss_target[0], p, first, ex)

    out = {}

    def finish(idxs, name):
        mine = [sum8("sum8_" + SHARDED[i][0], place, ex.parts[i], ex.full[i], SHARDED[i][3]) if i in ex.full
                else sum4("sum4_" + SHARDED[i][0], place, ex.parts[i], ex.halves[i], SHARDED[i][3]) for i in idxs]
        for i, g in zip(idxs, join_halves(name, mine, idxs)):
            n = SHARDED[i][0]
            out[n] = tuple(t[None] for t in adamw("adamw_" + n, args[n][0], g, args["m_" + n][0], args["v_" + n][0]))

    finish((1, 2, 3, 4), "join_halves_rest")
    small_list = [gsmall[n].reshape(natural(n, args[n]).shape) for n in SMALL] + [dmeta, loss.reshape(1, 1)]
    red = allreduce_small(_pack_small(small_list))
    *small_red, dmeta_sum, loss = _unpack_small(red, small_list)
    loss = loss[0, 0]
    gmeta = lax.dynamic_slice_in_dim(dmeta_sum, chip * (D // 4), D // 4, axis=1)
    out["meta_tokens"] = tuple(adamw("adamw_meta", meta_tokens, gmeta, m_meta_tokens, v_meta_tokens))
    ds, nms, nvs = adamw_small([natural(n, args[n]) for n in SMALL], small_red,
                               [natural(n, args["m_" + n]) for n in SMALL],
                               [natural(n, args["v_" + n]) for n in SMALL])
    for n, g_, d_, m_, v_ in zip(SMALL, small_red, ds, nms, nvs):
        out[n] = tuple(natural(n, t) for t in (g_, d_, m_, v_))
    ex.finish("in", out["w_ffn_in"][3])
    finish((0,), "join_halves_in")

    res = [loss, dx[None]]
    for k in range(4):
        res += [out[n][k] for n in names]
    return tuple(res)
```
